```python
import math
import jax, jax.numpy as jnp
from jax import lax
import numpy as np

D_MODEL = 1024
BATCH = 8
SEQ = 2048
DEPTH = 4

CHUNK = 64
N_META = 16
N_A_LAYERS = DEPTH // 2
N_B_LAYERS = DEPTH - N_A_LAYERS
D_RNN = 3 * D_MODEL // 2
N_LRU_BLOCKS = 16
LRU_BLOCK = D_RNN // N_LRU_BLOCKS
LRU_C = 8.0
CONV_A_WIDTH = 4
N_FOX_HEADS = 16
FOX_HEAD_DIM = D_MODEL // N_FOX_HEADS
Q_BLOCK = 128
D_FF = ((8 * D_MODEL // 3 + 255) // 256) * 256
CONV_F_WIDTH = 3
DN_ALPHA = (2 * DEPTH) ** 0.25
DN_BETA = (8 * DEPTH) ** -0.25
LN_EPS = 1e-5

kernel_name = "yoco_rglru_fox_convffn_deepnorm"


def layer_norm(x, g, b):
    xf = x.astype(jnp.float32)
    mu = jnp.mean(xf, axis=-1, keepdims=True)
    var = jnp.mean(jnp.square(xf - mu), axis=-1, keepdims=True)
    y = (xf - mu) * lax.rsqrt(var + LN_EPS)
    return (y * g.astype(jnp.float32) + b.astype(jnp.float32)).astype(x.dtype)


def causal_dwconv(x, w, b):
    width = w.shape[0]
    length = x.shape[1]
    xp = jnp.pad(x, ((0, 0), (width - 1, 0), (0, 0)))
    y = b
    for k in range(width):
        y = y + xp[:, k:k + length] * w[k]
    return y


def rg_lru(x, w_r, b_r, w_i, b_i, lam):
    bsz, length, width = x.shape
    xb = x.reshape(bsz, length, N_LRU_BLOCKS, LRU_BLOCK)
    r = jax.nn.sigmoid(jnp.einsum('blnc,ncd->blnd', xb, w_r).reshape(bsz, length, width) + b_r)
    i = jax.nn.sigmoid(jnp.einsum('blnc,ncd->blnd', xb, w_i).reshape(bsz, length, width) + b_i)
    log_a = -LRU_C * r.astype(jnp.float32) * jax.nn.softplus(-lam.astype(jnp.float32))
    a = jnp.exp(log_a)
    u = jnp.sqrt(-jnp.expm1(2.0 * log_a)) * (i * x).astype(jnp.float32)

    def combine(left, right):
        a1, b1 = left
        a2, b2 = right
        return a1 * a2, a2 * b1 + b2

    _, h = lax.associative_scan(combine, (a, u), axis=1)
    return h.astype(x.dtype)


def recurrent_mixer(x, w_in, conv_w, conv_b, w_r, b_r, w_i, b_i, lam, w_out):
    gr = x @ w_in
    gate, rec = gr[..., :D_RNN], gr[..., D_RNN:]
    rec = causal_dwconv(rec, conv_w, conv_b)
    h = rg_lru(rec, w_r, b_r, w_i, b_i, lam)
    return (jax.nn.gelu(gate) * h) @ w_out


def conv_ffn(x, w_in, conv_w, conv_b, w_out):
    h = causal_dwconv(x @ w_in, conv_w, conv_b)
    gate, val = h[..., :D_FF], h[..., D_FF:]
    return (jax.nn.gelu(gate) * val) @ w_out


def to_heads_padded(t, lp):
    bsz, length, _ = t.shape
    t = t.reshape(bsz, length, N_FOX_HEADS, FOX_HEAD_DIM).transpose(0, 2, 1, 3)
    return jnp.pad(t, ((0, 0), (0, 0), (0, lp - length), (0, 0)))


def shared_kv(x, kv_w, f_b):
    length = x.shape[1]
    lp = -(-length // Q_BLOCK) * Q_BLOCK
    z = x @ kv_w
    k = to_heads_padded(z[..., :D_MODEL], lp)
    v = to_heads_padded(z[..., D_MODEL:2 * D_MODEL], lp)
    log_f = jax.nn.log_sigmoid(z[..., 2 * D_MODEL:].astype(jnp.float32) + f_b.astype(jnp.float32))
    c = jnp.cumsum(log_f, axis=1).transpose(0, 2, 1)
    c = jnp.pad(c, ((0, 0), (0, 0), (0, lp - length)), mode='edge')
    return k, v, c


def forgetting_attention(q, k, v, c):
    lp = q.shape[2]
    scale = q.shape[-1] ** -0.5
    outs = []
    for q0 in range(0, lp, Q_BLOCK):
        end = q0 + Q_BLOCK
        s = jnp.einsum('bhqd,bhkd->bhqk', q[:, :, q0:end], k[:, :, :end]).astype(jnp.float32) * scale
        s = s + c[:, :, q0:end, None] - c[:, :, None, :end]
        mask = jnp.arange(end)[None, :] <= jnp.arange(q0, end)[:, None]
        s = jnp.where(mask, s, -jnp.inf)
        p = jax.nn.softmax(s, axis=-1).astype(v.dtype)
        outs.append(jnp.einsum('bhqk,bhkd->bhqd', p, v[:, :, :end]))
    return jnp.concatenate(outs, axis=2)


def fox_mixer(x, w_in, w_out, k, v, c):
    bsz, length, _ = x.shape
    qg = x @ w_in
    q = to_heads_padded(qg[..., :D_MODEL], k.shape[2])
    o = forgetting_attention(q, k, v, c)[:, :, :length]
    o = o.transpose(0, 2, 1, 3).reshape(bsz, length, D_MODEL)
    return (o * jax.nn.sigmoid(qg[..., D_MODEL:])) @ w_out


def _fwd_setup_inputs(seed: int = 0) -> dict:
    key = jax.random.key(seed)
    ks = jax.random.split(key, 24)
    f32 = jnp.float32
    d = D_MODEL

    def nrm(k, shape, scale):
        return jax.random.normal(k, shape, f32) * scale

    u = jax.random.uniform(ks[9], (N_A_LAYERS, D_RNN), f32, 0.9, 0.999)
    a0 = u ** (1.0 / LRU_C)
    lam = jnp.log(a0) - jnp.log1p(-a0)

    kv_w = jnp.concatenate([
        nrm(ks[11], (d, d), d ** -0.5),
        nrm(ks[12], (d, d), d ** -0.5 * DN_BETA),
        nrm(ks[13], (d, N_FOX_HEADS), d ** -0.5),
    ], axis=1)

    return {
        "x": nrm(ks[0], (BATCH, SEQ, d), 1.0),
        "meta": nrm(ks[1], (N_META, d), 1.0),
        "a_w_in": nrm(ks[2], (N_A_LAYERS, d, 2 * D_RNN), d ** -0.5),
        "a_conv_w": nrm(ks[3], (N_A_LAYERS, CONV_A_WIDTH, D_RNN), CONV_A_WIDTH ** -0.5),
        "a_conv_b": nrm(ks[4], (N_A_LAYERS, D_RNN), 0.02),
        "a_w_r": nrm(ks[5], (N_A_LAYERS, N_LRU_BLOCKS, LRU_BLOCK, LRU_BLOCK), LRU_BLOCK ** -0.5),
        "a_b_r": nrm(ks[6], (N_A_LAYERS, D_RNN), 0.02),
        "a_w_i": nrm(ks[7], (N_A_LAYERS, N_LRU_BLOCKS, LRU_BLOCK, LRU_BLOCK), LRU_BLOCK ** -0.5),
        "a_b_i": nrm(ks[8], (N_A_LAYERS, D_RNN), 0.02),
        "a_lambda": lam,
        "a_w_out": nrm(ks[10], (N_A_LAYERS, D_RNN, d), D_RNN ** -0.5 * DN_BETA),
        "kv_w": kv_w,
        "kv_f_b": jax.random.uniform(ks[14], (N_FOX_HEADS,), f32, 1.0, 4.0),
        "b_w_in": nrm(ks[15], (N_B_LAYERS, d, 2 * d), d ** -0.5),
        "b_w_out": nrm(ks[16], (N_B_LAYERS, d, d), d ** -0.5 * DN_BETA),
        "f_w_in": nrm(ks[17], (DEPTH, d, 2 * D_FF), d ** -0.5),
        "f_conv_w": nrm(ks[18], (DEPTH, CONV_F_WIDTH, 2 * D_FF), CONV_F_WIDTH ** -0.5),
        "f_conv_b": nrm(ks[19], (DEPTH, 2 * D_FF), 0.02),
        "f_w_out": nrm(ks[20], (DEPTH, D_FF, d), D_FF ** -0.5 * DN_BETA),
        "ln1_g": 1.0 + nrm(ks[21], (DEPTH, d), 0.02),
        "ln1_b": nrm(ks[22], (DEPTH, d), 0.02),
        "ln2_g": 1.0 + nrm(ks[23], (DEPTH, d), 0.02),
        "ln2_b": nrm(jax.random.fold_in(key, 99), (DEPTH, d), 0.02),
    }


def _fwd_reference(x, meta, a_w_in, a_conv_w, a_conv_b, a_w_r, a_b_r, a_w_i, a_b_i, a_lambda, a_w_out,
              kv_w, kv_f_b, b_w_in, b_w_out, f_w_in, f_conv_w, f_conv_b, f_w_out,
              ln1_g, ln1_b, ln2_g, ln2_b):
    bsz = x.shape[0]
    h = jnp.concatenate([jnp.broadcast_to(meta.astype(x.dtype), (bsz, N_META, D_MODEL)), x], axis=1)
    k = v = c = None
    for layer in range(DEPTH):
        if layer < N_A_LAYERS:
            mix = recurrent_mixer(h, a_w_in[layer], a_conv_w[layer], a_conv_b[layer],
                                  a_w_r[layer], a_b_r[layer], a_w_i[layer], a_b_i[layer],
                                  a_lambda[layer], a_w_out[layer])
        else:
            if layer == N_A_LAYERS:
                k, v, c = shared_kv(h, kv_w, kv_f_b)
            j = layer - N_A_LAYERS
            mix = fox_mixer(h, b_w_in[j], b_w_out[j], k, v, c)
        h = layer_norm(DN_ALPHA * h + mix, ln1_g[layer], ln1_b[layer])
        ffn = conv_ffn(h, f_w_in[layer], f_conv_w[layer], f_conv_b[layer], f_w_out[layer])
        h = layer_norm(DN_ALPHA * h + ffn, ln2_g[layer], ln2_b[layer])
    return h[:, N_META:]


import jax as _jax
import jax.numpy as _jnp

TWIN_FORMAT = 'train_step'
FWD_PARAMS = ['x', 'meta', 'a_w_in', 'a_conv_w', 'a_conv_b', 'a_w_r', 'a_b_r', 'a_w_i', 'a_b_i', 'a_lambda', 'a_w_out', 'kv_w', 'kv_f_b', 'b_w_in', 'b_w_out', 'f_w_in', 'f_conv_w', 'f_conv_b', 'f_w_out', 'ln1_g', 'ln1_b', 'ln2_g', 'ln2_b']
TWIN_WEIGHTS = ['meta', 'a_w_in', 'a_conv_w', 'a_conv_b', 'a_w_r', 'a_b_r', 'a_w_i', 'a_b_i', 'a_lambda', 'a_w_out', 'kv_w', 'kv_f_b', 'b_w_in', 'b_w_out', 'f_w_in', 'f_conv_w', 'f_conv_b', 'f_w_out', 'ln1_g', 'ln1_b', 'ln2_g', 'ln2_b']
TWIN_DIFF_INPUT = 'x'
TWIN_INPUTS = ['x', 'meta', 'a_w_in', 'a_conv_w', 'a_conv_b', 'a_w_r', 'a_b_r', 'a_w_i', 'a_b_i', 'a_lambda', 'a_w_out', 'kv_w', 'kv_f_b', 'b_w_in', 'b_w_out', 'f_w_in', 'f_conv_w', 'f_conv_b', 'f_w_out', 'ln1_g', 'ln1_b', 'ln2_g', 'ln2_b', 'loss_target', 'm_meta', 'm_a_w_in', 'm_a_conv_w', 'm_a_conv_b', 'm_a_w_r', 'm_a_b_r', 'm_a_w_i', 'm_a_b_i', 'm_a_lambda', 'm_a_w_out', 'm_kv_w', 'm_kv_f_b', 'm_b_w_in', 'm_b_w_out', 'm_f_w_in', 'm_f_conv_w', 'm_f_conv_b', 'm_f_w_out', 'm_ln1_g', 'm_ln1_b', 'm_ln2_g', 'm_ln2_b', 'v_meta', 'v_a_w_in', 'v_a_conv_w', 'v_a_conv_b', 'v_a_w_r', 'v_a_b_r', 'v_a_w_i', 'v_a_b_i', 'v_a_lambda', 'v_a_w_out', 'v_kv_w', 'v_kv_f_b', 'v_b_w_in', 'v_b_w_out', 'v_f_w_in', 'v_f_conv_w', 'v_f_conv_b', 'v_f_w_out', 'v_ln1_g', 'v_ln1_b', 'v_ln2_g', 'v_ln2_b']
TWIN_OUTPUTS = ['loss', 'grad_x', 'grad_meta', 'grad_a_w_in', 'grad_a_conv_w', 'grad_a_conv_b', 'grad_a_w_r', 'grad_a_b_r', 'grad_a_w_i', 'grad_a_b_i', 'grad_a_lambda', 'grad_a_w_out', 'grad_kv_w', 'grad_kv_f_b', 'grad_b_w_in', 'grad_b_w_out', 'grad_f_w_in', 'grad_f_conv_w', 'grad_f_conv_b', 'grad_f_w_out', 'grad_ln1_g', 'grad_ln1_b', 'grad_ln2_g', 'grad_ln2_b', 'delta_meta', 'delta_a_w_in', 'delta_a_conv_w', 'delta_a_conv_b', 'delta_a_w_r', 'delta_a_b_r', 'delta_a_w_i', 'delta_a_b_i', 'delta_a_lambda', 'delta_a_w_out', 'delta_kv_w', 'delta_kv_f_b', 'delta_b_w_in', 'delta_b_w_out', 'delta_f_w_in', 'delta_f_conv_w', 'delta_f_conv_b', 'delta_f_w_out', 'delta_ln1_g', 'delta_ln1_b', 'delta_ln2_g', 'delta_ln2_b', 'new_m_meta', 'new_m_a_w_in', 'new_m_a_conv_w', 'new_m_a_conv_b', 'new_m_a_w_r', 'new_m_a_b_r', 'new_m_a_w_i', 'new_m_a_b_i', 'new_m_a_lambda', 'new_m_a_w_out', 'new_m_kv_w', 'new_m_kv_f_b', 'new_m_b_w_in', 'new_m_b_w_out', 'new_m_f_w_in', 'new_m_f_conv_w', 'new_m_f_conv_b', 'new_m_f_w_out', 'new_m_ln1_g', 'new_m_ln1_b', 'new_m_ln2_g', 'new_m_ln2_b', 'new_v_meta', 'new_v_a_w_in', 'new_v_a_conv_w', 'new_v_a_conv_b', 'new_v_a_w_r', 'new_v_a_b_r', 'new_v_a_w_i', 'new_v_a_b_i', 'new_v_a_lambda', 'new_v_a_w_out', 'new_v_kv_w', 'new_v_kv_f_b', 'new_v_b_w_in', 'new_v_b_w_out', 'new_v_f_w_in', 'new_v_f_conv_w', 'new_v_f_conv_b', 'new_v_f_w_out', 'new_v_ln1_g', 'new_v_ln1_b', 'new_v_ln2_g', 'new_v_ln2_b']
TWIN_LEAF_KINDS = {'loss': 'loss', 'grad_x': 'grad_x', 'grad_meta': 'grad_w', 'grad_a_w_in': 'grad_w', 'grad_a_conv_w': 'grad_w', 'grad_a_conv_b': 'grad_w', 'grad_a_w_r': 'grad_w', 'grad_a_b_r': 'grad_w', 'grad_a_w_i': 'grad_w', 'grad_a_b_i': 'grad_w', 'grad_a_lambda': 'grad_w', 'grad_a_w_out': 'grad_w', 'grad_kv_w': 'grad_w', 'grad_kv_f_b': 'grad_w', 'grad_b_w_in': 'grad_w', 'grad_b_w_out': 'grad_w', 'grad_f_w_in': 'grad_w', 'grad_f_conv_w': 'grad_w', 'grad_f_conv_b': 'grad_w', 'grad_f_w_out': 'grad_w', 'grad_ln1_g': 'grad_w', 'grad_ln1_b': 'grad_w', 'grad_ln2_g': 'grad_w', 'grad_ln2_b': 'grad_w', 'delta_meta': 'delta_w', 'delta_a_w_in': 'delta_w', 'delta_a_conv_w': 'delta_w', 'delta_a_conv_b': 'delta_w', 'delta_a_w_r': 'delta_w', 'delta_a_b_r': 'delta_w', 'delta_a_w_i': 'delta_w', 'delta_a_b_i': 'delta_w', 'delta_a_lambda': 'delta_w', 'delta_a_w_out': 'delta_w', 'delta_kv_w': 'delta_w', 'delta_kv_f_b': 'delta_w', 'delta_b_w_in': 'delta_w', 'delta_b_w_out': 'delta_w', 'delta_f_w_in': 'delta_w', 'delta_f_conv_w': 'delta_w', 'delta_f_conv_b': 'delta_w', 'delta_f_w_out': 'delta_w', 'delta_ln1_g': 'delta_w', 'delta_ln1_b': 'delta_w', 'delta_ln2_g': 'delta_w', 'delta_ln2_b': 'delta_w', 'new_m_meta': 'new_m', 'new_m_a_w_in': 'new_m', 'new_m_a_conv_w': 'new_m', 'new_m_a_conv_b': 'new_m', 'new_m_a_w_r': 'new_m', 'new_m_a_b_r': 'new_m', 'new_m_a_w_i': 'new_m', 'new_m_a_b_i': 'new_m', 'new_m_a_lambda': 'new_m', 'new_m_a_w_out': 'new_m', 'new_m_kv_w': 'new_m', 'new_m_kv_f_b': 'new_m', 'new_m_b_w_in': 'new_m', 'new_m_b_w_out': 'new_m', 'new_m_f_w_in': 'new_m', 'new_m_f_conv_w': 'new_m', 'new_m_f_conv_b': 'new_m', 'new_m_f_w_out': 'new_m', 'new_m_ln1_g': 'new_m', 'new_m_ln1_b': 'new_m', 'new_m_ln2_g': 'new_m', 'new_m_ln2_b': 'new_m', 'new_v_meta': 'new_v', 'new_v_a_w_in': 'new_v', 'new_v_a_conv_w': 'new_v', 'new_v_a_conv_b': 'new_v', 'new_v_a_w_r': 'new_v', 'new_v_a_b_r': 'new_v', 'new_v_a_w_i': 'new_v', 'new_v_a_b_i': 'new_v', 'new_v_a_lambda': 'new_v', 'new_v_a_w_out': 'new_v', 'new_v_kv_w': 'new_v', 'new_v_kv_f_b': 'new_v', 'new_v_b_w_in': 'new_v', 'new_v_b_w_out': 'new_v', 'new_v_f_w_in': 'new_v', 'new_v_f_conv_w': 'new_v', 'new_v_f_conv_b': 'new_v', 'new_v_f_w_out': 'new_v', 'new_v_ln1_g': 'new_v', 'new_v_ln1_b': 'new_v', 'new_v_ln2_g': 'new_v', 'new_v_ln2_b': 'new_v'}


def _forward(args):
    return _fwd_reference(*[args[k] for k in FWD_PARAMS])


def _output_shape():
    out = _jax.eval_shape(lambda: _forward(_fwd_setup_inputs(0)))
    return out.shape, out.dtype

N_MICROBATCH = 1
ADAM_LR = 0.001
ADAM_B1 = 0.9
ADAM_B2 = 0.999
ADAM_EPS = 1e-08
ADAM_WD = 0.01
ADAM_STEP = 10
PER_EXAMPLE_BATCH_AXIS = {'x': 0, 'loss_target': 0}
SHARED_INPUTS = []
_WEIGHT_DTYPES = {'meta': _jnp.float32, 'a_w_in': _jnp.float32, 'a_conv_w': _jnp.float32, 'a_conv_b': _jnp.float32, 'a_w_r': _jnp.float32, 'a_b_r': _jnp.float32, 'a_w_i': _jnp.float32, 'a_b_i': _jnp.float32, 'a_lambda': _jnp.float32, 'a_w_out': _jnp.float32, 'kv_w': _jnp.float32, 'kv_f_b': _jnp.float32, 'b_w_in': _jnp.float32, 'b_w_out': _jnp.float32, 'f_w_in': _jnp.float32, 'f_conv_w': _jnp.float32, 'f_conv_b': _jnp.float32, 'f_w_out': _jnp.float32, 'ln1_g': _jnp.float32, 'ln1_b': _jnp.float32, 'ln2_g': _jnp.float32, 'ln2_b': _jnp.float32}
MOMENT_SCALE = {'meta': 1.412515e-03, 'a_w_in': 1.219920e-02, 'a_conv_w': 1.286375e-02, 'a_conv_b': 1.466318e-01, 'a_w_r': 4.403261e-03, 'a_b_r': 3.378074e-03, 'a_w_i': 7.968856e-03, 'a_b_i': 4.300197e-03, 'a_lambda': 6.750673e-03, 'a_w_out': 3.949393e-02, 'kv_w': 7.231288e-03, 'kv_f_b': 2.823346e-02, 'b_w_in': 1.707286e-03, 'b_w_out': 6.947329e-03, 'f_w_in': 1.276464e-02, 'f_conv_w': 1.277196e-02, 'f_conv_b': 1.415621e-02, 'f_w_out': 4.964418e-02, 'ln1_g': 5.507307e-01, 'ln1_b': 2.679869e-01, 'ln2_g': 8.048218e+00, 'ln2_b': 5.472639e-01}


def _to_microbatches(a, axis):
    t = _jnp.moveaxis(a, axis, 0)
    t = t.reshape((N_MICROBATCH, t.shape[0] // N_MICROBATCH) + t.shape[1:])
    return _jnp.moveaxis(t, 1, axis + 1)


def setup_inputs(seed: int = 0) -> dict:
    inp = _fwd_setup_inputs(seed)
    key = _jax.random.fold_in(_jax.random.key(seed), 7919)
    shape, _ = _output_shape()
    out = dict(inp)
    out["loss_target"] = _jax.random.normal(_jax.random.fold_in(key, 0), shape, _jnp.float32)
    for i, name in enumerate(TWIN_WEIGHTS):
        w = inp[name].astype(_jnp.float32)
        if MOMENT_SCALE is None:
            s = _jnp.sqrt(_jnp.mean(_jnp.square(w)) + 1e-30)
        else:
            s = MOMENT_SCALE[name]
        km, kv = _jax.random.split(_jax.random.fold_in(key, i + 1))
        out[name] = w
        out["m_" + name] = s * _jax.random.normal(km, w.shape, _jnp.float32)
        out["v_" + name] = (s * s) * _jax.random.uniform(kv, w.shape, _jnp.float32, 0.5, 1.5)
    if N_MICROBATCH > 1:
        for name, axis in PER_EXAMPLE_BATCH_AXIS.items():
            out[name] = _to_microbatches(out[name], axis)
    return {'x': out['x'], 'meta': out['meta'], 'a_w_in': out['a_w_in'], 'a_conv_w': out['a_conv_w'], 'a_conv_b': out['a_conv_b'], 'a_w_r': out['a_w_r'], 'a_b_r': out['a_b_r'], 'a_w_i': out['a_w_i'], 'a_b_i': out['a_b_i'], 'a_lambda': out['a_lambda'], 'a_w_out': out['a_w_out'], 'kv_w': out['kv_w'], 'kv_f_b': out['kv_f_b'], 'b_w_in': out['b_w_in'], 'b_w_out': out['b_w_out'], 'f_w_in': out['f_w_in'], 'f_conv_w': out['f_conv_w'], 'f_conv_b': out['f_conv_b'], 'f_w_out': out['f_w_out'], 'ln1_g': out['ln1_g'], 'ln1_b': out['ln1_b'], 'ln2_g': out['ln2_g'], 'ln2_b': out['ln2_b'], 'loss_target': out['loss_target'], 'm_meta': out['m_meta'], 'm_a_w_in': out['m_a_w_in'], 'm_a_conv_w': out['m_a_conv_w'], 'm_a_conv_b': out['m_a_conv_b'], 'm_a_w_r': out['m_a_w_r'], 'm_a_b_r': out['m_a_b_r'], 'm_a_w_i': out['m_a_w_i'], 'm_a_b_i': out['m_a_b_i'], 'm_a_lambda': out['m_a_lambda'], 'm_a_w_out': out['m_a_w_out'], 'm_kv_w': out['m_kv_w'], 'm_kv_f_b': out['m_kv_f_b'], 'm_b_w_in': out['m_b_w_in'], 'm_b_w_out': out['m_b_w_out'], 'm_f_w_in': out['m_f_w_in'], 'm_f_conv_w': out['m_f_conv_w'], 'm_f_conv_b': out['m_f_conv_b'], 'm_f_w_out': out['m_f_w_out'], 'm_ln1_g': out['m_ln1_g'], 'm_ln1_b': out['m_ln1_b'], 'm_ln2_g': out['m_ln2_g'], 'm_ln2_b': out['m_ln2_b'], 'v_meta': out['v_meta'], 'v_a_w_in': out['v_a_w_in'], 'v_a_conv_w': out['v_a_conv_w'], 'v_a_conv_b': out['v_a_conv_b'], 'v_a_w_r': out['v_a_w_r'], 'v_a_b_r': out['v_a_b_r'], 'v_a_w_i': out['v_a_w_i'], 'v_a_b_i': out['v_a_b_i'], 'v_a_lambda': out['v_a_lambda'], 'v_a_w_out': out['v_a_w_out'], 'v_kv_w': out['v_kv_w'], 'v_kv_f_b': out['v_kv_f_b'], 'v_b_w_in': out['v_b_w_in'], 'v_b_w_out': out['v_b_w_out'], 'v_f_w_in': out['v_f_w_in'], 'v_f_conv_w': out['v_f_conv_w'], 'v_f_conv_b': out['v_f_conv_b'], 'v_f_w_out': out['v_f_w_out'], 'v_ln1_g': out['v_ln1_g'], 'v_ln1_b': out['v_ln1_b'], 'v_ln2_g': out['v_ln2_g'], 'v_ln2_b': out['v_ln2_b']}


def _loss(weights, diff, rest, loss_target):
    with _jax.named_scope("forward"):
        args = {**rest, TWIN_DIFF_INPUT: diff, **{k: w.astype(_WEIGHT_DTYPES[k]) for k, w in weights.items()}}
        y = _forward(args)
    with _jax.named_scope("loss_head"):
        err = _jnp.square(y.astype(_jnp.float32) - loss_target)
        return 0.5 * _jnp.sum(_jnp.mean(err, axis=-1)) if err.ndim else 0.5 * err


def _adamw(w, g, m, v):
    m = ADAM_B1 * m + (1.0 - ADAM_B1) * g
    v = ADAM_B2 * v + (1.0 - ADAM_B2) * _jnp.square(g)
    m_hat = m / (1.0 - ADAM_B1 ** ADAM_STEP)
    v_hat = v / (1.0 - ADAM_B2 ** ADAM_STEP)
    delta = -ADAM_LR * (m_hat / (_jnp.sqrt(v_hat) + ADAM_EPS) + ADAM_WD * w)
    return delta, m, v


def reference(x, meta, a_w_in, a_conv_w, a_conv_b, a_w_r, a_b_r, a_w_i, a_b_i, a_lambda, a_w_out, kv_w, kv_f_b, b_w_in, b_w_out, f_w_in, f_conv_w, f_conv_b, f_w_out, ln1_g, ln1_b, ln2_g, ln2_b, loss_target, m_meta, m_a_w_in, m_a_conv_w, m_a_conv_b, m_a_w_r, m_a_b_r, m_a_w_i, m_a_b_i, m_a_lambda, m_a_w_out, m_kv_w, m_kv_f_b, m_b_w_in, m_b_w_out, m_f_w_in, m_f_conv_w, m_f_conv_b, m_f_w_out, m_ln1_g, m_ln1_b, m_ln2_g, m_ln2_b, v_meta, v_a_w_in, v_a_conv_w, v_a_conv_b, v_a_w_r, v_a_b_r, v_a_w_i, v_a_b_i, v_a_lambda, v_a_w_out, v_kv_w, v_kv_f_b, v_b_w_in, v_b_w_out, v_f_w_in, v_f_conv_w, v_f_conv_b, v_f_w_out, v_ln1_g, v_ln1_b, v_ln2_g, v_ln2_b):
    given = dict(x=x, meta=meta, a_w_in=a_w_in, a_conv_w=a_conv_w, a_conv_b=a_conv_b, a_w_r=a_w_r, a_b_r=a_b_r, a_w_i=a_w_i, a_b_i=a_b_i, a_lambda=a_lambda, a_w_out=a_w_out, kv_w=kv_w, kv_f_b=kv_f_b, b_w_in=b_w_in, b_w_out=b_w_out, f_w_in=f_w_in, f_conv_w=f_conv_w, f_conv_b=f_conv_b, f_w_out=f_w_out, ln1_g=ln1_g, ln1_b=ln1_b, ln2_g=ln2_g, ln2_b=ln2_b, loss_target=loss_target, m_meta=m_meta, m_a_w_in=m_a_w_in, m_a_conv_w=m_a_conv_w, m_a_conv_b=m_a_conv_b, m_a_w_r=m_a_w_r, m_a_b_r=m_a_b_r, m_a_w_i=m_a_w_i, m_a_b_i=m_a_b_i, m_a_lambda=m_a_lambda, m_a_w_out=m_a_w_out, m_kv_w=m_kv_w, m_kv_f_b=m_kv_f_b, m_b_w_in=m_b_w_in, m_b_w_out=m_b_w_out, m_f_w_in=m_f_w_in, m_f_conv_w=m_f_conv_w, m_f_conv_b=m_f_conv_b, m_f_w_out=m_f_w_out, m_ln1_g=m_ln1_g, m_ln1_b=m_ln1_b, m_ln2_g=m_ln2_g, m_ln2_b=m_ln2_b, v_meta=v_meta, v_a_w_in=v_a_w_in, v_a_conv_w=v_a_conv_w, v_a_conv_b=v_a_conv_b, v_a_w_r=v_a_w_r, v_a_b_r=v_a_b_r, v_a_w_i=v_a_w_i, v_a_b_i=v_a_b_i, v_a_lambda=v_a_lambda, v_a_w_out=v_a_w_out, v_kv_w=v_kv_w, v_kv_f_b=v_kv_f_b, v_b_w_in=v_b_w_in, v_b_w_out=v_b_w_out, v_f_w_in=v_f_w_in, v_f_conv_w=v_f_conv_w, v_f_conv_b=v_f_conv_b, v_f_w_out=v_f_w_out, v_ln1_g=v_ln1_g, v_ln1_b=v_ln1_b, v_ln2_g=v_ln2_g, v_ln2_b=v_ln2_b)
    weights = {n: given[n] for n in TWIN_WEIGHTS}
    shared = {n: given[n] for n in SHARED_INPUTS}
    per_example = {n: given[n] for n in ['x']}
    grad_fn = _jax.value_and_grad(_loss, argnums=(0, 1))

    def one_microbatch(ex, loss_target):
        ex = dict(ex)
        diff = ex.pop(TWIN_DIFF_INPUT)
        return grad_fn(weights, diff, {**shared, **ex}, loss_target)

    if N_MICROBATCH == 1:
        loss, (grad_w, grad_x) = one_microbatch(per_example, given["loss_target"])
    else:
        def body(carry, xs):
            loss_sum, grad_sum = carry
            l_k, (gw_k, gx_k) = one_microbatch(xs[0], xs[1])
            with _jax.named_scope("update"):
                return (loss_sum + l_k, _jax.tree.map(_jnp.add, grad_sum, gw_k)), gx_k

        init = (_jnp.zeros((), _jnp.float32), _jax.tree.map(_jnp.zeros_like, weights))
        (loss, grad_w), grad_x = _jax.lax.scan(body, init, (per_example, given["loss_target"]))
    with _jax.named_scope("update"):
        delta_w, new_m, new_v = {}, {}, {}
        for n in TWIN_WEIGHTS:
            delta_w[n], new_m[n], new_v[n] = _adamw(weights[n], grad_w[n], given["m_" + n], given["v_" + n])
    return (loss, grad_x, *[grad_w[n] for n in TWIN_WEIGHTS], *[delta_w[n] for n in TWIN_WEIGHTS],
            *[new_m[n] for n in TWIN_WEIGHTS], *[new_v[n] for n in TWIN_WEIGHTS])
```

```python
import functools
import math

import jax
import jax.numpy as jnp
from jax import lax
from jax.experimental import pallas as pl
from jax.experimental.pallas import tpu as pltpu

F32 = jnp.float32
BF16 = jnp.bfloat16

LRU_C = 8.0
LN_EPS = 1e-5
ADAM_LR = 0.001
ADAM_B1 = 0.9
ADAM_B2 = 0.999
ADAM_EPS = 1e-08
ADAM_WD = 0.01
ADAM_STEP = 10

LANES = 128
SUBLANES = 8
V7X_VMEM_BYTES = 64 * 1024 * 1024
VMEM_LIMIT = V7X_VMEM_BYTES * 7 // 8
N_SHARDS = 4
GELU_C0 = math.sqrt(2.0 / math.pi)
GELU_C1 = 0.044715
NEG_BIG = -1e30


def _cp(*sem):
    return pltpu.CompilerParams(dimension_semantics=tuple(sem), vmem_limit_bytes=VMEM_LIMIT)


def _tile(n, cap, mult=LANES):
    best = None
    d = mult
    while d <= min(n, cap):
        if n % d == 0:
            best = d
        d += mult
    return n if best is None else best


def _row_block(t):
    if t % 3 == 0 and (t // 3) % 16 == 0:
        return t // 3
    return t


def _round_up(n, m):
    return (n + m - 1) // m * m


def _sigmoid(v):
    return 1.0 / (1.0 + jnp.exp(-v))


def _softplus(v):
    return jnp.maximum(v, 0.0) + jnp.log(1.0 + jnp.exp(-jnp.abs(v)))


def _gelu_parts(v):
    v2 = v * v
    u = GELU_C0 * (v + GELU_C1 * v * v2)
    t = jnp.tanh(u)
    g = 0.5 * v * (1.0 + t)
    dg = 0.5 * (1.0 + t) + 0.5 * v * (1.0 - t * t) * (GELU_C0 * (1.0 + 3.0 * GELU_C1 * v2))
    return g, dg


def _gelu(v):
    u = GELU_C0 * (v + GELU_C1 * v * v * v)
    return 0.5 * v * (1.0 + jnp.tanh(u))


def _neg_expm1(v):
    series = -v * (1.0 + 0.5 * v * (1.0 + (v / 3.0) * (1.0 + 0.25 * v)))
    return jnp.where(v > -0.05, series, 1.0 - jnp.exp(v))


def _shift_down(v, j):
    if j == 0:
        return v
    rows = lax.broadcasted_iota(jnp.int32, v.shape, 0)
    return jnp.where(rows >= j, pltpu.roll(v, j, 0), 0.0)


def _shift_up(v, j):
    if j == 0:
        return v
    n = v.shape[0]
    rows = lax.broadcasted_iota(jnp.int32, v.shape, 0)
    return jnp.where(rows < n - j, pltpu.roll(v, n - j, 0), 0.0)


def _scan_rows(a_ref, b_ref, out_ref, n_rows, width, reverse):
    n_groups = n_rows // SUBLANES
    rows = lax.broadcasted_iota(jnp.int32, (SUBLANES, width), 0)
    edge = 0 if reverse else SUBLANES - 1

    def body(g, carry):
        grp = (n_groups - 1 - g) if reverse else g
        off = pl.multiple_of(grp * SUBLANES, SUBLANES)
        b = b_ref[pl.ds(off, SUBLANES), :]
        a = None if a_ref is None else a_ref[pl.ds(off, SUBLANES), :]
        for d in (1, 2, 4):
            if reverse:
                keep = rows < SUBLANES - d
                sh = SUBLANES - d
            else:
                keep = rows >= d
                sh = d
            b_s = jnp.where(keep, pltpu.roll(b, sh, 0), 0.0)
            if a is None:
                b = b + b_s
            else:
                a_s = jnp.where(keep, pltpu.roll(a, sh, 0), 1.0)
                b = a * b_s + b
                a = a * a_s
        h = b + carry if a is None else b + a * carry
        out_ref[pl.ds(off, SUBLANES), :] = h
        return jnp.sum(jnp.where(rows == edge, h, 0.0), axis=0, keepdims=True)

    lax.fori_loop(0, n_groups, body, jnp.zeros((1, width), F32))


def mm_in(x, w, layer, out_dtype, name):
    t, k = x.shape
    s_n, _, _, ns = w.shape
    tn = _tile(ns, 1408)
    nj = ns // tn
    rb = _row_block(t)

    def body(x_ref, w_ref, o_ref):
        o_ref[...] = jnp.dot(x_ref[...], w_ref[...], preferred_element_type=F32).astype(o_ref.dtype)

    return pl.pallas_call(
        body, name=name, grid=(s_n, nj, t // rb),
        in_specs=[pl.BlockSpec((rb, k), lambda s, j, r: (r, 0)),
                  pl.BlockSpec((None, None, k, tn), lambda s, j, r: (s, layer, 0, j))],
        out_specs=pl.BlockSpec((rb, tn), lambda s, j, r: (r, s * nj + j)),
        out_shape=jax.ShapeDtypeStruct((t, s_n * ns), out_dtype),
        compiler_params=_cp("parallel", "parallel", "parallel"))(x, w)


def mm_out(x, w, layer, name):
    t, k = x.shape
    n = w.shape[3]
    rb = _row_block(t)

    def body(x_ref, w_ref, o_ref):
        o_ref[...] = jnp.dot(x_ref[...], w_ref[...], preferred_element_type=F32)

    return pl.pallas_call(
        body, name=name, grid=(t // rb,),
        in_specs=[pl.BlockSpec((rb, k), lambda r: (r, 0)),
                  pl.BlockSpec((None, None, k, n), lambda r: (0, layer, 0, 0))],
        out_specs=pl.BlockSpec((rb, n), lambda r: (r, 0)),
        out_shape=jax.ShapeDtypeStruct((t, n), F32),
        compiler_params=_cp("parallel"))(x, w)


def mm_in_nt(dy, w, layer, name):
    t, _ = dy.shape
    s_n, _, k, ns = w.shape
    tn = _tile(ns, 1408)
    nj = ns // tn
    rb = _row_block(t)

    def body(dy_ref, w_ref, o_ref):
        @pl.when((pl.program_id(1) == 0) & (pl.program_id(2) == 0))
        def _():
            o_ref[...] = jnp.zeros_like(o_ref)
        o_ref[...] += lax.dot_general(dy_ref[...], w_ref[...], (((1,), (1,)), ((), ())),
                                      preferred_element_type=F32)

    return pl.pallas_call(
        body, name=name, grid=(t // rb, s_n, nj),
        in_specs=[pl.BlockSpec((rb, tn), lambda r, s, j: (r, s * nj + j)),
                  pl.BlockSpec((None, None, k, tn), lambda r, s, j: (s, layer, 0, j))],
        out_specs=pl.BlockSpec((rb, k), lambda r, s, j: (r, 0)),
        out_shape=jax.ShapeDtypeStruct((t, k), F32),
        compiler_params=_cp("parallel", "arbitrary", "arbitrary"))(dy, w)


def mm_out_nt(dy, w, layer, name):
    t, n = dy.shape
    k = w.shape[2]
    rb = _row_block(t)

    def body(dy_ref, w_ref, o_ref):
        o_ref[...] = lax.dot_general(dy_ref[...], w_ref[...], (((1,), (1,)), ((), ())),
                                     preferred_element_type=F32)

    return pl.pallas_call(
        body, name=name, grid=(t // rb,),
        in_specs=[pl.BlockSpec((rb, n), lambda r: (r, 0)),
                  pl.BlockSpec((None, None, k, n), lambda r: (0, layer, 0, 0))],
        out_specs=pl.BlockSpec((rb, k), lambda r: (r, 0)),
        out_shape=jax.ShapeDtypeStruct((t, k), F32),
        compiler_params=_cp("parallel"))(dy, w)


def mm_tn(x, dy, buf, layer, name):
    t, _ = x.shape
    s_n = buf.shape[0]
    kb, nb = buf.shape[2], buf.shape[3]
    tk = _tile(kb, 1408)
    tn = _tile(nb, 1408)
    nkb, nnb = kb // tk, nb // tn

    def body(x_ref, dy_ref, buf_ref, o_ref):
        del buf_ref
        o_ref[...] = lax.dot_general(x_ref[...], dy_ref[...], (((0,), (0,)), ((), ())),
                                     preferred_element_type=F32).astype(o_ref.dtype)

    return pl.pallas_call(
        body, name=name, grid=(s_n, nkb, nnb),
        in_specs=[pl.BlockSpec((t, tk), lambda s, a, b: (0, a)),
                  pl.BlockSpec((t, tn), lambda s, a, b: (0, s * nnb + b)),
                  pl.BlockSpec(memory_space=pl.ANY)],
        out_specs=pl.BlockSpec((None, None, tk, tn), lambda s, a, b: (s, layer, a, b)),
        out_shape=jax.ShapeDtypeStruct(buf.shape, buf.dtype),
        input_output_aliases={2: 0},
        compiler_params=_cp("parallel", "parallel", "parallel"))(x, dy, buf)


def mm_bd(x, wr, wi, name):
    t, _ = x.shape
    g_n, gs, _ = wr.shape
    rb = _row_block(t)

    def body(x_ref, wr_ref, wi_ref, r_ref, i_ref):
        xv = x_ref[...]
        r_ref[...] = jnp.dot(xv, wr_ref[...], preferred_element_type=F32)
        i_ref[...] = jnp.dot(xv, wi_ref[...], preferred_element_type=F32)

    blk = pl.BlockSpec((rb, gs), lambda g, r: (r, g))
    wspec = pl.BlockSpec((None, gs, gs), lambda g, r: (g, 0, 0))
    return pl.pallas_call(
        body, name=name, grid=(g_n, t // rb), in_specs=[blk, wspec, wspec], out_specs=[blk, blk],
        out_shape=[jax.ShapeDtypeStruct((t, g_n * gs), F32)] * 2,
        compiler_params=_cp("parallel", "parallel"))(x, wr, wi)


def mm_bd_nt(dr, di, wr, wi, name):
    t, _ = dr.shape
    g_n, gs, _ = wr.shape
    rb = _row_block(t)
    nt = (((1,), (1,)), ((), ()))

    def body(dr_ref, di_ref, wr_ref, wi_ref, o_ref):
        o_ref[...] = (lax.dot_general(dr_ref[...], wr_ref[...], nt, preferred_element_type=F32)
                      + lax.dot_general(di_ref[...], wi_ref[...], nt, preferred_element_type=F32))

    blk = pl.BlockSpec((rb, gs), lambda g, r: (r, g))
    wspec = pl.BlockSpec((None, gs, gs), lambda g, r: (g, 0, 0))
    return pl.pallas_call(
        body, name=name, grid=(g_n, t // rb), in_specs=[blk, blk, wspec, wspec], out_specs=blk,
        out_shape=jax.ShapeDtypeStruct((t, g_n * gs), F32),
        compiler_params=_cp("parallel", "parallel"))(dr, di, wr, wi)


def mm_bd_tn(x, dr, di, gs, name):
    t, w = x.shape
    g_n = w // gs
    tn_dims = (((0,), (0,)), ((), ()))

    def body(x_ref, dr_ref, di_ref, gr_ref, gi_ref):
        xv = x_ref[...]
        gr_ref[...] = lax.dot_general(xv, dr_ref[...], tn_dims, preferred_element_type=F32)
        gi_ref[...] = lax.dot_general(xv, di_ref[...], tn_dims, preferred_element_type=F32)

    blk = pl.BlockSpec((t, gs), lambda g: (0, g))
    ospec = pl.BlockSpec((None, gs, gs), lambda g: (g, 0, 0))
    return pl.pallas_call(
        body, name=name, grid=(g_n,), in_specs=[blk, blk, blk], out_specs=[ospec, ospec],
        out_shape=[jax.ShapeDtypeStruct((g_n, gs, gs), F32)] * 2,
        compiler_params=_cp("parallel"))(x, dr, di)


def embed_fwd(meta, x2d, name):
    nm, d = meta.shape
    seq = x2d.shape[0]
    t = nm + seq
    cb = _tile(d, 256)

    def body(m_ref, x_ref, h_ref, hb_ref):
        h_ref[pl.ds(0, nm), :] = m_ref[...]
        h_ref[pl.ds(nm, seq), :] = x_ref[...]
        hb_ref[pl.ds(0, nm), :] = m_ref[...].astype(BF16)
        hb_ref[pl.ds(nm, seq), :] = x_ref[...].astype(BF16)

    return pl.pallas_call(
        body, name=name, grid=(d // cb,),
        in_specs=[pl.BlockSpec((nm, cb), lambda j: (0, j)), pl.BlockSpec((seq, cb), lambda j: (0, j))],
        out_specs=[pl.BlockSpec((t, cb), lambda j: (0, j))] * 2,
        out_shape=[jax.ShapeDtypeStruct((t, d), F32), jax.ShapeDtypeStruct((t, d), BF16)],
        compiler_params=_cp("parallel"))(meta, x2d)


def embed_bwd(adds, nm, name):
    t, d = adds[0][0].shape
    seq = t - nm
    cb = _tile(d, 256)
    scales = [s for _, s in adds]
    n = len(adds)

    def body(*refs):
        tot = None
        for r, sc in zip(refs[:n], scales):
            term = r[...] if sc == 1.0 else sc * r[...]
            tot = term if tot is None else tot + term
        gm_ref, gx_ref = refs[n], refs[n + 1]
        gm_ref[...] = tot[0:nm]
        gx_ref[...] = tot[nm:t]

    return pl.pallas_call(
        body, name=name, grid=(d // cb,),
        in_specs=[pl.BlockSpec((t, cb), lambda j: (0, j))] * n,
        out_specs=[pl.BlockSpec((nm, cb), lambda j: (0, j)), pl.BlockSpec((seq, cb), lambda j: (0, j))],
        out_shape=[jax.ShapeDtypeStruct((nm, d), F32), jax.ShapeDtypeStruct((seq, d), F32)],
        compiler_params=_cp("parallel"))(*[a for a, _ in adds])


def loss_fwd_bwd(h, tgt, nm, name):
    t, d = h.shape
    seq = t - nm
    cb = _tile(d, 256)
    inv_d = 1.0 / d

    def body(h_ref, t_ref, loss_ref, dy_ref):
        @pl.when(pl.program_id(0) == 0)
        def _():
            loss_ref[...] = jnp.zeros_like(loss_ref)
        err = h_ref[pl.ds(nm, seq), :] - t_ref[...]
        dy_ref[pl.ds(0, nm), :] = jnp.zeros((nm, cb), F32)
        dy_ref[pl.ds(nm, seq), :] = err * inv_d
        loss_ref[...] += (0.5 * inv_d) * jnp.sum(err * err, keepdims=True)

    return pl.pallas_call(
        body, name=name, grid=(d // cb,),
        in_specs=[pl.BlockSpec((t, cb), lambda j: (0, j)), pl.BlockSpec((seq, cb), lambda j: (0, j))],
        out_specs=[pl.BlockSpec((1, 1), lambda j: (0, 0)), pl.BlockSpec((t, cb), lambda j: (0, j))],
        out_shape=[jax.ShapeDtypeStruct((1, 1), F32), jax.ShapeDtypeStruct((t, d), F32)],
        compiler_params=_cp("arbitrary"))(h, tgt)


def ln_fwd(h, mix, g, b, alpha, name):
    t, d = h.shape
    rb = _row_block(t)

    def body(h_ref, m_ref, g_ref, b_ref, y_ref, yb_ref, xh_ref, rs_ref):
        s = alpha * h_ref[...] + m_ref[...]
        mu = jnp.mean(s, axis=-1, keepdims=True)
        c = s - mu
        var = jnp.mean(c * c, axis=-1, keepdims=True)
        rstd = lax.rsqrt(var + LN_EPS)
        xh = c * rstd
        y = xh * g_ref[...] + b_ref[...]
        y_ref[...] = y
        yb_ref[...] = y.astype(BF16)
        xh_ref[...] = xh
        rs_ref[...] = rstd

    row = pl.BlockSpec((rb, d), lambda r: (r, 0))
    vec = pl.BlockSpec((1, d), lambda r: (0, 0))
    return pl.pallas_call(
        body, name=name, grid=(t // rb,), in_specs=[row, row, vec, vec],
        out_specs=[row, row, row, pl.BlockSpec((rb, 1), lambda r: (r, 0))],
        out_shape=[jax.ShapeDtypeStruct((t, d), F32), jax.ShapeDtypeStruct((t, d), BF16),
                   jax.ShapeDtypeStruct((t, d), F32), jax.ShapeDtypeStruct((t, 1), F32)],
        compiler_params=_cp("parallel"))(h, mix, g, b)


def ln_bwd(adds, xhat, rstd, g, name):
    t, d = xhat.shape
    rb = _row_block(t)
    scales = [s for _, s in adds]
    n = len(adds)

    def body(*refs):
        xh_ref, rs_ref, g_ref = refs[n:n + 3]
        ds_ref, dsb_ref, dg_ref, db_ref = refs[n + 3:]
        dy = None
        for r, sc in zip(refs[:n], scales):
            term = r[...] if sc == 1.0 else sc * r[...]
            dy = term if dy is None else dy + term

        @pl.when(pl.program_id(0) == 0)
        def _():
            dg_ref[...] = jnp.zeros_like(dg_ref)
            db_ref[...] = jnp.zeros_like(db_ref)

        xh = xh_ref[...]
        dxh = dy * g_ref[...]
        m1 = jnp.mean(dxh, axis=-1, keepdims=True)
        m2 = jnp.mean(dxh * xh, axis=-1, keepdims=True)
        ds = rs_ref[...] * (dxh - m1 - xh * m2)
        ds_ref[...] = ds
        dsb_ref[...] = ds.astype(BF16)
        dg_ref[...] += jnp.sum(dy * xh, axis=0, keepdims=True)
        db_ref[...] += jnp.sum(dy, axis=0, keepdims=True)

    row = pl.BlockSpec((rb, d), lambda r: (r, 0))
    vec = pl.BlockSpec((1, d), lambda r: (0, 0))
    return pl.pallas_call(
        body, name=name, grid=(t // rb,),
        in_specs=[row] * n + [row, pl.BlockSpec((rb, 1), lambda r: (r, 0)), vec],
        out_specs=[row, row, vec, vec],
        out_shape=[jax.ShapeDtypeStruct((t, d), F32), jax.ShapeDtypeStruct((t, d), BF16),
                   jax.ShapeDtypeStruct((1, d), F32), jax.ShapeDtypeStruct((1, d), F32)],
        compiler_params=_cp("arbitrary"))(*[a for a, _ in adds], xhat, rstd, g)


def _conv_fwd_val(xv, w_ref, b_ref, width):
    y = b_ref[...]
    for j in range(width):
        y = y + _shift_down(xv, j) * w_ref[pl.ds(width - 1 - j, 1), :]
    return y


def _conv_bwd_val(dout, xv, w_ref, width):
    dx = None
    dws = [None] * width
    for j in range(width):
        k = width - 1 - j
        term = _shift_up(dout, j) * w_ref[pl.ds(k, 1), :]
        dx = term if dx is None else dx + term
        dws[k] = jnp.sum(dout * _shift_down(xv, j), axis=0, keepdims=True)
    return dx, dws, jnp.sum(dout, axis=0, keepdims=True)


def a_conv_fwd(gr, cw, cbias, name):
    t, two_dr = gr.shape
    dr = two_dr // 2
    width = cw.shape[0]
    cb = _tile(dr, 256)
    off = dr // cb

    def body(x_ref, w_ref, b_ref, rc_ref, rcb_ref):
        y = _conv_fwd_val(x_ref[...], w_ref, b_ref, width)
        rc_ref[...] = y
        rcb_ref[...] = y.astype(BF16)

    return pl.pallas_call(
        body, name=name, grid=(dr // cb,),
        in_specs=[pl.BlockSpec((t, cb), lambda j: (0, off + j)),
                  pl.BlockSpec((width, cb), lambda j: (0, j)), pl.BlockSpec((1, cb), lambda j: (0, j))],
        out_specs=[pl.BlockSpec((t, cb), lambda j: (0, j))] * 2,
        out_shape=[jax.ShapeDtypeStruct((t, dr), F32), jax.ShapeDtypeStruct((t, dr), BF16)],
        compiler_params=_cp("parallel"))(gr, cw, cbias)


def a_conv_bwd(drc_a, drc_b, gr, cw, name):
    t, two_dr = gr.shape
    dr = two_dr // 2
    width = cw.shape[0]
    cb = _tile(dr, 256)
    off = dr // cb

    def body(da_ref, db_ref, x_ref, w_ref, dx_ref, dw_ref, dbias_ref):
        dout = da_ref[...] + db_ref[...]
        dx, dws, dbias = _conv_bwd_val(dout, x_ref[...], w_ref, width)
        dx_ref[...] = dx.astype(BF16)
        for k in range(width):
            dw_ref[pl.ds(k, 1), :] = dws[k]
        dbias_ref[...] = dbias

    col = pl.BlockSpec((t, cb), lambda j: (0, j))
    return pl.pallas_call(
        body, name=name, grid=(dr // cb,),
        in_specs=[col, col, pl.BlockSpec((t, cb), lambda j: (0, off + j)),
                  pl.BlockSpec((width, cb), lambda j: (0, j))],
        out_specs=[col, pl.BlockSpec((width, cb), lambda j: (0, j)), pl.BlockSpec((1, cb), lambda j: (0, j))],
        out_shape=[jax.ShapeDtypeStruct((t, dr), BF16), jax.ShapeDtypeStruct((width, dr), F32),
                   jax.ShapeDtypeStruct((1, dr), F32)],
        compiler_params=_cp("parallel"))(drc_a, drc_b, gr, cw)


def _lru_gates(r_pre, i_pre, br, bi, lam):
    r = _sigmoid(r_pre + br)
    i = _sigmoid(i_pre + bi)
    sp = _softplus(-lam)
    la = -LRU_C * r * sp
    a = jnp.exp(la)
    m = jnp.sqrt(_neg_expm1(2.0 * la))
    return r, i, sp, la, a, m


def a_elem_fwd(gr, rc, r_pre, i_pre, br, bi, lam, name):
    t, dr = rc.shape
    cb = LANES

    def body(gate_ref, rc_ref, rp_ref, ip_ref, br_ref, bi_ref, lam_ref, hs_ref, g_ref, a_s, u_s):
        rcv = rc_ref[...]
        _, i, _, _, a, m = _lru_gates(rp_ref[...], ip_ref[...], br_ref[...], bi_ref[...], lam_ref[...])
        a_s[...] = a
        u_s[...] = m * (i * rcv)
        _scan_rows(a_s, u_s, hs_ref, t, cb, reverse=False)
        g_ref[...] = (_gelu(gate_ref[...]) * hs_ref[...]).astype(BF16)

    col = pl.BlockSpec((t, cb), lambda j: (0, j))
    vec = pl.BlockSpec((1, cb), lambda j: (0, j))
    return pl.pallas_call(
        body, name=name, grid=(dr // cb,),
        in_specs=[col, col, col, col, vec, vec, vec],
        out_specs=[col, col],
        out_shape=[jax.ShapeDtypeStruct((t, dr), F32), jax.ShapeDtypeStruct((t, dr), BF16)],
        scratch_shapes=[pltpu.VMEM((t, cb), F32), pltpu.VMEM((t, cb), F32)],
        compiler_params=_cp("parallel"))(gr, rc, r_pre, i_pre, br, bi, lam)


def a_elem_bwd(dg, gr, rc, r_pre, i_pre, hs, br, bi, lam, name):
    t, dr = rc.shape
    cb = LANES

    def body(dg_ref, gate_ref, rc_ref, rp_ref, ip_ref, hs_ref, br_ref, bi_ref, lam_ref,
             dgate_ref, dr_ref, di_ref, drc_ref, dlam_ref, dbr_ref, dbi_ref, a_s, b_s, g_s):
        lamv = lam_ref[...]
        rcv = rc_ref[...]
        hsv = hs_ref[...]
        dgv = dg_ref[...]
        r, i, sp, _, a, m = _lru_gates(rp_ref[...], ip_ref[...], br_ref[...], bi_ref[...], lamv)
        ge, dge = _gelu_parts(gate_ref[...])
        dgate_ref[...] = (dgv * hsv * dge).astype(BF16)
        a_s[...] = _shift_up(a, 1)
        b_s[...] = dgv * ge
        _scan_rows(a_s, b_s, g_s, t, cb, reverse=True)
        gsum = g_s[...]
        da = gsum * _shift_down(hsv, 1)
        dm = gsum * (i * rcv)
        d_i = gsum * m * rcv
        drc_ref[...] = gsum * m * i
        dla = a * da - dm * (a * a) / m
        d_r = (-LRU_C) * sp * dla
        dsp = jnp.sum((-LRU_C) * r * dla, axis=0, keepdims=True)
        dlam_ref[...] = -dsp * _sigmoid(-lamv)
        d_rp = d_r * r * (1.0 - r)
        d_ip = d_i * i * (1.0 - i)
        dr_ref[...] = d_rp.astype(BF16)
        di_ref[...] = d_ip.astype(BF16)
        dbr_ref[...] = jnp.sum(d_rp, axis=0, keepdims=True)
        dbi_ref[...] = jnp.sum(d_ip, axis=0, keepdims=True)

    col = pl.BlockSpec((t, cb), lambda j: (0, j))
    vec = pl.BlockSpec((1, cb), lambda j: (0, j))
    big_b = jax.ShapeDtypeStruct((t, dr), BF16)
    vec_s = jax.ShapeDtypeStruct((1, dr), F32)
    return pl.pallas_call(
        body, name=name, grid=(dr // cb,),
        in_specs=[col, col, col, col, col, col, vec, vec, vec],
        out_specs=[col, col, col, col, vec, vec, vec],
        out_shape=[big_b, big_b, big_b, jax.ShapeDtypeStruct((t, dr), F32), vec_s, vec_s, vec_s],
        scratch_shapes=[pltpu.VMEM((t, cb), F32)] * 3,
        compiler_params=_cp("parallel"))(dg, gr, rc, r_pre, i_pre, hs, br, bi, lam)


def f_elem_fwd(z, cw, cbias, name):
    t, two_f = z.shape
    dff = two_f // 2
    width = cw.shape[0]
    cb = _tile(dff, 256)
    off = dff // cb

    def body(zg_ref, zv_ref, wg_ref, wv_ref, bg_ref, bv_ref, o_ref):
        zcg = _conv_fwd_val(zg_ref[...], wg_ref, bg_ref, width)
        zcv = _conv_fwd_val(zv_ref[...], wv_ref, bv_ref, width)
        o_ref[...] = (_gelu(zcg) * zcv).astype(BF16)

    lo = lambda j: (0, j)
    hi = lambda j: (0, off + j)
    return pl.pallas_call(
        body, name=name, grid=(dff // cb,),
        in_specs=[pl.BlockSpec((t, cb), lo), pl.BlockSpec((t, cb), hi),
                  pl.BlockSpec((width, cb), lo), pl.BlockSpec((width, cb), hi),
                  pl.BlockSpec((1, cb), lo), pl.BlockSpec((1, cb), hi)],
        out_specs=pl.BlockSpec((t, cb), lo),
        out_shape=jax.ShapeDtypeStruct((t, dff), BF16),
        compiler_params=_cp("parallel"))(z, z, cw, cw, cbias, cbias)


def f_elem_bwd(z, dff_g, cw, cbias, name):
    t, two_f = z.shape
    dff = two_f // 2
    width = cw.shape[0]
    cb = _tile(dff, 256)
    off = dff // cb

    def body(zg_ref, zv_ref, d_ref, wg_ref, wv_ref, bg_ref, bv_ref,
             dzg_ref, dzv_ref, dwg_ref, dwv_ref, dbg_ref, dbv_ref):
        zg = zg_ref[...]
        zv = zv_ref[...]
        zcg = _conv_fwd_val(zg, wg_ref, bg_ref, width)
        zcv = _conv_fwd_val(zv, wv_ref, bv_ref, width)
        ge, dge = _gelu_parts(zcg)
        dv = d_ref[...]
        dx, dws, dbias = _conv_bwd_val(dv * zcv * dge, zg, wg_ref, width)
        dzg_ref[...] = dx.astype(BF16)
        for k in range(width):
            dwg_ref[pl.ds(k, 1), :] = dws[k]
        dbg_ref[...] = dbias
        dx, dws, dbias = _conv_bwd_val(dv * ge, zv, wv_ref, width)
        dzv_ref[...] = dx.astype(BF16)
        for k in range(width):
            dwv_ref[pl.ds(k, 1), :] = dws[k]
        dbv_ref[...] = dbias

    lo = lambda j: (0, j)
    hi = lambda j: (0, off + j)
    col = pl.BlockSpec((t, cb), lo)
    wsp = pl.BlockSpec((width, cb), lo)
    vsp = pl.BlockSpec((1, cb), lo)
    return pl.pallas_call(
        body, name=name, grid=(dff // cb,),
        in_specs=[col, pl.BlockSpec((t, cb), hi), col, wsp, pl.BlockSpec((width, cb), hi),
                  vsp, pl.BlockSpec((1, cb), hi)],
        out_specs=[col, col, wsp, wsp, vsp, vsp],
        out_shape=[jax.ShapeDtypeStruct((t, dff), BF16)] * 2
        + [jax.ShapeDtypeStruct((width, dff), F32)] * 2 + [jax.ShapeDtypeStruct((1, dff), F32)] * 2,
        compiler_params=_cp("parallel"))(z, z, dff_g, cw, cw, cbias, cbias)


def kv_fwd(z, fb, d_model, name):
    t, _ = z.shape
    blk = 2 * d_model // LANES

    def body(z_ref, fb_ref, c_ref, lf_s):
        v = z_ref[...] + fb_ref[...]
        lf_s[...] = -_softplus(-v)
        _scan_rows(None, lf_s, c_ref, t, LANES, reverse=False)

    return pl.pallas_call(
        body, name=name, grid=(1,),
        in_specs=[pl.BlockSpec((t, LANES), lambda j: (0, blk)), pl.BlockSpec((1, LANES), lambda j: (0, 0))],
        out_specs=pl.BlockSpec((t, LANES), lambda j: (0, 0)),
        out_shape=jax.ShapeDtypeStruct((t, LANES), F32),
        scratch_shapes=[pltpu.VMEM((t, LANES), F32)],
        compiler_params=_cp("arbitrary"))(z, fb)


def kv_bwd(dcs, z, fb, d_model, name):
    t, _ = z.shape
    blk = 2 * d_model // LANES
    n = len(dcs)

    def body(*refs):
        z_ref, fb_ref, dz_ref, dfb_ref, dc_s, dl_s = refs[n:]
        tot = refs[0][...]
        for r in refs[1:n]:
            tot = tot + r[...]
        dc_s[...] = tot
        _scan_rows(None, dc_s, dl_s, t, LANES, reverse=True)
        v = z_ref[...] + fb_ref[...]
        dz = dl_s[...] * _sigmoid(-v)
        dz_ref[...] = dz.astype(BF16)
        dfb_ref[...] = jnp.sum(dz, axis=0, keepdims=True)

    full = pl.BlockSpec((t, LANES), lambda j: (0, 0))
    return pl.pallas_call(
        body, name=name, grid=(1,),
        in_specs=[full] * n + [pl.BlockSpec((t, LANES), lambda j: (0, blk)),
                               pl.BlockSpec((1, LANES), lambda j: (0, 0))],
        out_specs=[full, pl.BlockSpec((1, LANES), lambda j: (0, 0))],
        out_shape=[jax.ShapeDtypeStruct((t, LANES), BF16), jax.ShapeDtypeStruct((1, LANES), F32)],
        scratch_shapes=[pltpu.VMEM((t, LANES), F32)] * 2,
        compiler_params=_cp("arbitrary"))(*dcs, z, fb)


def add_cast(a, b, name):
    t, d = a.shape
    cb = _tile(d, 512)

    def body(a_ref, b_ref, o_ref):
        o_ref[...] = (a_ref[...] + b_ref[...]).astype(BF16)

    col = pl.BlockSpec((t, cb), lambda j: (0, j))
    return pl.pallas_call(body, name=name, grid=(d // cb,), in_specs=[col, col], out_specs=col,
                          out_shape=jax.ShapeDtypeStruct((t, d), BF16),
                          compiler_params=_cp("parallel"))(a, b)


def _attn_geometry(t):
    nqb = 6 if t > 1024 else 2
    tp = _round_up(t, LANES * nqb)
    return nqb, tp, tp // nqb


def _attn_probs(qs, ks, ccol, crow, j, i, tq, dh, scale):
    r0 = i * tq
    nk = (i + 1) * tq
    lanes = pl.ds(j * dh, dh)
    qi = qs[pl.ds(r0, tq), lanes]
    kk = ks[pl.ds(0, nk), lanes]
    s = lax.dot_general(qi, kk, (((1,), (1,)), ((), ())), preferred_element_type=F32) * scale
    s = s + ccol[r0:r0 + tq] - crow[:, 0:nk]
    rows = lax.broadcasted_iota(jnp.int32, (tq, nk), 0) + r0
    cols = lax.broadcasted_iota(jnp.int32, (tq, nk), 1)
    s = jnp.where(cols <= rows, s, NEG_BIG)
    e = jnp.exp(s - jnp.max(s, axis=1, keepdims=True))
    return e / jnp.sum(e, axis=1, keepdims=True), qi, kk


def _head_col(cs, head):
    lane = lax.broadcasted_iota(jnp.int32, cs.shape, 1)
    return jnp.sum(jnp.where(lane == head, cs, 0.0), axis=1, keepdims=True)


def attn_fwd(qg, z, c_pad, ct_pad, d_model, n_heads, name):
    t = qg.shape[0]
    dh = d_model // n_heads
    hp = LANES // dh
    nqb, tp, tq = _attn_geometry(t)
    nblk = d_model // LANES
    scale = dh ** -0.5

    def body(q_ref, og_ref, k_ref, v_ref, c_ref, ct_ref, o_ref, mo_ref, qs, ks, vs, os_):
        pad = jnp.zeros((tp - t, LANES), BF16)
        for src, dst in ((q_ref, qs), (k_ref, ks), (v_ref, vs)):
            dst[pl.ds(0, t), :] = src[...].astype(BF16)
            dst[pl.ds(t, tp - t), :] = pad
        cs = c_ref[...]
        for j in range(hp):
            ccol = _head_col(cs, pl.program_id(0) * hp + j)
            crow = ct_ref[j]
            for i in range(nqb):
                p, _, _ = _attn_probs(qs, ks, ccol, crow, j, i, tq, dh, scale)
                vv = vs[pl.ds(0, (i + 1) * tq), pl.ds(j * dh, dh)]
                os_[pl.ds(i * tq, tq), pl.ds(j * dh, dh)] = jnp.dot(
                    p.astype(BF16), vv, preferred_element_type=F32)
        o = os_[pl.ds(0, t), :]
        o_ref[...] = o
        mo_ref[...] = (o * _sigmoid(og_ref[...])).astype(BF16)

    col = lambda off: pl.BlockSpec((t, LANES), lambda p: (0, off + p))
    return pl.pallas_call(
        body, name=name, grid=(nblk,),
        in_specs=[col(0), col(nblk), col(0), col(nblk),
                  pl.BlockSpec((tp, LANES), lambda p: (0, 0)),
                  pl.BlockSpec((hp, 1, tp), lambda p: (p, 0, 0))],
        out_specs=[col(0), col(0)],
        out_shape=[jax.ShapeDtypeStruct((t, d_model), F32), jax.ShapeDtypeStruct((t, d_model), BF16)],
        scratch_shapes=[pltpu.VMEM((tp, LANES), BF16)] * 3 + [pltpu.VMEM((tp, LANES), F32)],
        compiler_params=_cp("parallel"))(qg, qg, z, z, c_pad, ct_pad)


def attn_bwd(dmo, qg, z, o, c_pad, ct_pad, d_model, n_heads, name):
    t = qg.shape[0]
    dh = d_model // n_heads
    hp = LANES // dh
    nqb, tp, tq = _attn_geometry(t)
    nblk = d_model // LANES
    scale = dh ** -0.5
    tn_dims = (((0,), (0,)), ((), ()))
    nt_dims = (((1,), (1,)), ((), ()))

    def body(dmo_ref, q_ref, og_ref, k_ref, v_ref, o_ref, c_ref, ct_ref,
             dq_ref, dog_ref, dk_ref, dv_ref, dct_ref, qs, ks, vs, dos, dqs, dks, dvs):
        pad = jnp.zeros((tp - t, LANES), BF16)
        sg = _sigmoid(og_ref[...])
        dmo_v = dmo_ref[...]
        dog_ref[...] = (dmo_v * o_ref[...] * sg * (1.0 - sg)).astype(BF16)
        dos[pl.ds(0, t), :] = (dmo_v * sg).astype(BF16)
        dos[pl.ds(t, tp - t), :] = pad
        for src, dst in ((q_ref, qs), (k_ref, ks), (v_ref, vs)):
            dst[pl.ds(0, t), :] = src[...].astype(BF16)
            dst[pl.ds(t, tp - t), :] = pad
        dks[...] = jnp.zeros_like(dks)
        dvs[...] = jnp.zeros_like(dvs)
        dct_ref[...] = jnp.zeros_like(dct_ref)
        cs = c_ref[...]
        for j in range(hp):
            ccol = _head_col(cs, pl.program_id(0) * hp + j)
            crow = ct_ref[j]
            lanes = pl.ds(j * dh, dh)
            for i in range(nqb):
                nk = (i + 1) * tq
                p, qi, kk = _attn_probs(qs, ks, ccol, crow, j, i, tq, dh, scale)
                vv = vs[pl.ds(0, nk), lanes]
                do_i = dos[pl.ds(i * tq, tq), lanes]
                dp = lax.dot_general(do_i, vv, nt_dims, preferred_element_type=F32)
                ds = p * (dp - jnp.sum(p * dp, axis=1, keepdims=True))
                ds_b = ds.astype(BF16)
                dqs[pl.ds(i * tq, tq), lanes] = jnp.dot(ds_b, kk, preferred_element_type=F32) * scale
                dks[pl.ds(0, nk), lanes] += lax.dot_general(
                    ds_b, qi, tn_dims, preferred_element_type=F32) * scale
                dvs[pl.ds(0, nk), lanes] += lax.dot_general(
                    p.astype(BF16), do_i, tn_dims, preferred_element_type=F32)
                dct_ref[j, :, pl.ds(0, nk)] -= jnp.sum(ds, axis=0, keepdims=True)
        dq_ref[...] = dqs[pl.ds(0, t), :].astype(BF16)
        dk_ref[...] = dks[pl.ds(0, t), :]
        dv_ref[...] = dvs[pl.ds(0, t), :]

    col = lambda off: pl.BlockSpec((t, LANES), lambda p: (0, off + p))
    big = lambda dt: jax.ShapeDtypeStruct((t, d_model), dt)
    return pl.pallas_call(
        body, name=name, grid=(nblk,),
        in_specs=[col(0), col(0), col(nblk), col(0), col(nblk), col(0),
                  pl.BlockSpec((tp, LANES), lambda p: (0, 0)),
                  pl.BlockSpec((hp, 1, tp), lambda p: (p, 0, 0))],
        out_specs=[col(0), col(0), col(0), col(0), pl.BlockSpec((hp, 1, tp), lambda p: (p, 0, 0))],
        out_shape=[big(BF16), big(BF16), big(F32), big(F32),
                   jax.ShapeDtypeStruct((n_heads, 1, tp), F32)],
        scratch_shapes=[pltpu.VMEM((tp, LANES), BF16)] * 4 + [pltpu.VMEM((tp, LANES), F32)] * 3,
        compiler_params=_cp("parallel"))(dmo, qg, qg, z, z, o, c_pad, ct_pad)


def cast_halves(w2d, name):
    r, c = w2d.shape
    rh = r // 2
    tr = _tile(rh, 512, 16)

    def body(w_ref, o_ref):
        o_ref[...] = w_ref[...].astype(BF16)

    n = rh // tr
    return pl.pallas_call(
        body, name=name, grid=(2, n),
        in_specs=[pl.BlockSpec((tr, c), lambda h, i: (h * n + i, 0))],
        out_specs=pl.BlockSpec((None, tr, c), lambda h, i: (h, i, 0)),
        out_shape=jax.ShapeDtypeStruct((2, rh, c), BF16),
        compiler_params=_cp("parallel", "parallel"))(w2d)


def add_halves(core, g, other, name):
    s_n, _, rh, c = g.shape
    tr = _tile(rh, 512, 16)

    def body(core_ref, g_ref, o_ref, out_ref):
        del core_ref
        out_ref[...] = (g_ref[...].astype(F32) + o_ref[...].astype(F32)).astype(out_ref.dtype)

    return pl.pallas_call(
        body, name=name,
        grid_spec=pltpu.PrefetchScalarGridSpec(
            num_scalar_prefetch=1, grid=(s_n, rh // tr),
            in_specs=[pl.BlockSpec((None, None, tr, c), lambda s, i, cr: (s, cr[0], i, 0)),
                      pl.BlockSpec((None, tr, c), lambda s, i, cr: (s, i, 0))],
            out_specs=pl.BlockSpec((None, tr, c), lambda s, i, cr: (s, i, 0))),
        out_shape=jax.ShapeDtypeStruct((s_n, rh, c), g.dtype),
        compiler_params=_cp("parallel", "parallel"))(core, g, other)


def add_four(shard, csum, recv, name):
    _, rh, c = csum.shape
    tr = _tile(rh, 512, 16)

    def body(sh_ref, a_ref, r_ref, out_ref):
        del sh_ref
        acc = a_ref[...].astype(F32)
        for k in range(3):
            acc = acc + r_ref[k].astype(F32)
        out_ref[...] = acc

    return pl.pallas_call(
        body, name=name,
        grid_spec=pltpu.PrefetchScalarGridSpec(
            num_scalar_prefetch=1, grid=(rh // tr,),
            in_specs=[pl.BlockSpec((None, tr, c), lambda i, sh: (sh[0], i, 0)),
                      pl.BlockSpec((3, tr, c), lambda i, sh: (0, i, 0))],
            out_specs=pl.BlockSpec((tr, c), lambda i, sh: (i, 0))),
        out_shape=jax.ShapeDtypeStruct((rh, c), F32),
        compiler_params=_cp("parallel"))(shard, csum, recv)


def adamw(w, g, m, v, name):
    r, c = w.shape
    tr = _tile(r, 512, SUBLANES)
    c1 = 1.0 - ADAM_B1 ** ADAM_STEP
    c2 = 1.0 - ADAM_B2 ** ADAM_STEP

    def body(w_ref, g_ref, m_ref, v_ref, d_ref, mo_ref, vo_ref):
        gv = g_ref[...]
        mn = ADAM_B1 * m_ref[...] + (1.0 - ADAM_B1) * gv
        vn = ADAM_B2 * v_ref[...] + (1.0 - ADAM_B2) * (gv * gv)
        m_hat = mn / c1
        v_hat = vn / c2
        d_ref[...] = -ADAM_LR * (m_hat / (jnp.sqrt(v_hat) + ADAM_EPS) + ADAM_WD * w_ref[...])
        mo_ref[...] = mn
        vo_ref[...] = vn

    blk = pl.BlockSpec((tr, c), lambda i: (i, 0))
    return pl.pallas_call(
        body, name=name, grid=(r // tr,), in_specs=[blk] * 4, out_specs=[blk] * 3,
        out_shape=[jax.ShapeDtypeStruct((r, c), F32)] * 3,
        compiler_params=_cp("parallel"))(w, g, m, v)


def _coords():
    return lax.axis_index("x"), lax.axis_index("y"), lax.axis_index("c")


def _exchange(name, ins, out_shapes, plan):
    n_in = len(ins)
    n_out = len(out_shapes)
    probe = plan([None] * n_in, [None] * n_out, True)
    n_loc, n_rem = probe

    def body(*refs):
        in_refs = refs[:n_in]
        out_refs = refs[n_in:n_in + n_out]
        send_sems, recv_sems, loc_sems = refs[n_in + n_out:]
        local, remote = plan(list(in_refs), list(out_refs), False)
        loc_cps = [pltpu.make_async_copy(src, dst, loc_sems.at[q]) for q, (src, dst) in enumerate(local)]
        for cp in loc_cps:
            cp.start()
        rem_cps = [pltpu.make_async_remote_copy(
            src_ref=src, dst_ref=dst, send_sem=send_sems.at[q], recv_sem=recv_sems.at[q],
            device_id=peer, device_id_type=pl.DeviceIdType.MESH)
            for q, (src, dst, peer, _) in enumerate(remote)]
        waited = set()
        for q, (_, _, _, after) in enumerate(remote):
            if after is not None and after not in waited:
                rem_cps[after].wait_recv()
                waited.add(after)
            rem_cps[q].start()
        for q, cp in enumerate(rem_cps):
            if q not in waited:
                cp.wait_recv()
        for cp in rem_cps:
            cp.wait_send()
        for cp in loc_cps:
            cp.wait()

    hbm = pl.BlockSpec(memory_space=pl.ANY)
    return pl.pallas_call(
        body, name=name, in_specs=[hbm] * n_in, out_specs=[hbm] * n_out, out_shape=out_shapes,
        scratch_shapes=[pltpu.SemaphoreType.DMA((max(n_rem, 1),)), pltpu.SemaphoreType.DMA((max(n_rem, 1),)),
                        pltpu.SemaphoreType.DMA((max(n_loc, 1),))],
        compiler_params=pltpu.CompilerParams(has_side_effects=True))(*ins)


def gather_shards(parts, name):
    n = len(parts)
    shapes = [jax.ShapeDtypeStruct((N_SHARDS,) + p.shape, p.dtype) for p in parts]

    def plan(ins, outs, count_only):
        if count_only:
            return n, 6 * n
        x, y, c = _coords()
        me = 2 * x + y
        chips = [((1 - x, y, c), 2 * (1 - x) + y), ((x, 1 - y, c), 2 * x + 1 - y),
                 ((1 - x, 1 - y, c), 2 * (1 - x) + 1 - y)]
        local = [(ins[i], outs[i].at[me]) for i in range(n)]
        remote = []
        for i in range(n):
            for peer, _ in chips:
                remote.append((ins[i].at[c], outs[i].at[me, c], peer, None))
        for i in range(n):
            for k, (_, src_shard) in enumerate(chips):
                slab = outs[i].at[src_shard, c]
                remote.append((slab, slab, (x, y, 1 - c), 3 * i + k))
        return local, remote

    return _exchange(name, parts, shapes, plan)


def swap_halves(grads, name):
    n = len(grads)
    shapes = [jax.ShapeDtypeStruct((g.shape[0],) + g.shape[2:], g.dtype) for g in grads]

    def plan(ins, outs, count_only):
        if count_only:
            return 0, n * N_SHARDS
        x, y, c = _coords()
        remote = []
        for i in range(n):
            for s in range(N_SHARDS):
                remote.append((ins[i].at[s, 1 - c], outs[i].at[s], (x, y, 1 - c), None))
        return [], remote

    return _exchange(name, grads, shapes, plan)


def chip_all_to_all(csums, name):
    n = len(csums)
    shapes = [jax.ShapeDtypeStruct((3,) + g.shape[1:], g.dtype) for g in csums]

    def plan(ins, outs, count_only):
        if count_only:
            return 0, 3 * n
        x, y, c = _coords()
        chips = [((1 - x, y, c), 2 * (1 - x) + y), ((x, 1 - y, c), 2 * x + 1 - y),
                 ((1 - x, 1 - y, c), 2 * (1 - x) + 1 - y)]
        remote = []
        for i in range(n):
            for k, (peer, shard) in enumerate(chips):
                remote.append((ins[i].at[shard], outs[i].at[k], peer, None))
        return [], remote

    return _exchange(name, csums, shapes, plan)


def join_halves(finals, name):
    n = len(finals)
    shapes = [jax.ShapeDtypeStruct((2,) + f.shape, f.dtype) for f in finals]

    def plan(ins, outs, count_only):
        if count_only:
            return n, n
        x, y, c = _coords()
        local = [(ins[i], outs[i].at[c]) for i in range(n)]
        remote = [(ins[i], outs[i].at[c], (x, y, 1 - c), None) for i in range(n)]
        return local, remote

    return _exchange(name, finals, shapes, plan)


def gather_full(part, name):
    shapes = [jax.ShapeDtypeStruct((N_SHARDS,) + part.shape, part.dtype)]

    def plan(ins, outs, count_only):
        if count_only:
            return 1, 3
        x, y, c = _coords()
        me = 2 * x + y
        peers = [(1 - x, y, c), (x, 1 - y, c), (1 - x, 1 - y, c)]
        local = [(ins[0], outs[0].at[me])]
        remote = [(ins[0], outs[0].at[me], peer, None) for peer in peers]
        return local, remote

    return _exchange(name, [part], shapes, plan)[0]


def _pack(arrays, multiple):
    flat = jnp.concatenate([a.reshape(-1) for a in arrays])
    n = flat.shape[0]
    return jnp.pad(flat, (0, _round_up(n, multiple) - n))


def _unpack(flat, shapes):
    out, pos = [], 0
    for shp in shapes:
        n = math.prod(shp)
        out.append(flat[pos:pos + n].reshape(shp))
        pos += n
    return out


def _block_diag(w, per_group):
    nb, bs, _ = w.shape
    g = nb // per_group
    w4 = w.reshape(g, per_group, bs, bs)
    eye = jnp.eye(per_group, dtype=w.dtype)
    full = w4[:, :, :, None, :] * eye[None, :, None, :, None]
    return full.reshape(g, per_group * bs, per_group * bs).astype(BF16)


def _block_diag_extract(full, per_group, bs):
    g = full.shape[0]
    f5 = full.reshape(g, per_group, bs, per_group, bs)
    idx = jnp.arange(per_group)
    picked = f5[:, idx, :, idx, :]
    return jnp.moveaxis(picked, 0, 1).reshape(g * per_group, bs, bs)


def kernel(x, meta, a_w_in, a_conv_w, a_conv_b, a_w_r, a_b_r, a_w_i, a_b_i, a_lambda, a_w_out, kv_w, kv_f_b, b_w_in, b_w_out, f_w_in, f_conv_w, f_conv_b, f_w_out, ln1_g, ln1_b, ln2_g, ln2_b, loss_target, m_meta, m_a_w_in, m_a_conv_w, m_a_conv_b, m_a_w_r, m_a_b_r, m_a_w_i, m_a_b_i, m_a_lambda, m_a_w_out, m_kv_w, m_kv_f_b, m_b_w_in, m_b_w_out, m_f_w_in, m_f_conv_w, m_f_conv_b, m_f_w_out, m_ln1_g, m_ln1_b, m_ln2_g, m_ln2_b, v_meta, v_a_w_in, v_a_conv_w, v_a_conv_b, v_a_w_r, v_a_b_r, v_a_w_i, v_a_b_i, v_a_lambda, v_a_w_out, v_kv_w, v_kv_f_b, v_b_w_in, v_b_w_out, v_f_w_in, v_f_conv_w, v_f_conv_b, v_f_w_out, v_ln1_g, v_ln1_b, v_ln2_g, v_ln2_b):
    weights = dict(meta=meta, a_w_in=a_w_in, a_conv_w=a_conv_w, a_conv_b=a_conv_b, a_w_r=a_w_r, a_b_r=a_b_r,
                   a_w_i=a_w_i, a_b_i=a_b_i, a_lambda=a_lambda, a_w_out=a_w_out, kv_w=kv_w, kv_f_b=kv_f_b,
                   b_w_in=b_w_in, b_w_out=b_w_out, f_w_in=f_w_in, f_conv_w=f_conv_w, f_conv_b=f_conv_b,
                   f_w_out=f_w_out, ln1_g=ln1_g, ln1_b=ln1_b, ln2_g=ln2_g, ln2_b=ln2_b)
    mom_m = dict(meta=m_meta, a_w_in=m_a_w_in, a_conv_w=m_a_conv_w, a_conv_b=m_a_conv_b, a_w_r=m_a_w_r,
                 a_b_r=m_a_b_r, a_w_i=m_a_w_i, a_b_i=m_a_b_i, a_lambda=m_a_lambda, a_w_out=m_a_w_out,
                 kv_w=m_kv_w, kv_f_b=m_kv_f_b, b_w_in=m_b_w_in, b_w_out=m_b_w_out, f_w_in=m_f_w_in,
                 f_conv_w=m_f_conv_w, f_conv_b=m_f_conv_b, f_w_out=m_f_w_out, ln1_g=m_ln1_g, ln1_b=m_ln1_b,
                 ln2_g=m_ln2_g, ln2_b=m_ln2_b)
    mom_v = dict(meta=v_meta, a_w_in=v_a_w_in, a_conv_w=v_a_conv_w, a_conv_b=v_a_conv_b, a_w_r=v_a_w_r,
                 a_b_r=v_a_b_r, a_w_i=v_a_w_i, a_b_i=v_a_b_i, a_lambda=v_a_lambda, a_w_out=v_a_w_out,
                 kv_w=v_kv_w, kv_f_b=v_kv_f_b, b_w_in=v_b_w_in, b_w_out=v_b_w_out, f_w_in=v_f_w_in,
                 f_conv_w=v_f_conv_w, f_conv_b=v_f_conv_b, f_w_out=v_f_w_out, ln1_g=v_ln1_g, ln1_b=v_ln1_b,
                 ln2_g=v_ln2_g, ln2_b=v_ln2_b)
    return _train_step(x, loss_target, weights, mom_m, mom_v)


WEIGHT_ORDER = ("meta", "a_w_in", "a_conv_w", "a_conv_b", "a_w_r", "a_b_r", "a_w_i", "a_b_i", "a_lambda",
                "a_w_out", "kv_w", "kv_f_b", "b_w_in", "b_w_out", "f_w_in", "f_conv_w", "f_conv_b",
                "f_w_out", "ln1_g", "ln1_b", "ln2_g", "ln2_b")
BIG = ("a_w_in", "a_w_out", "kv_w", "b_w_in", "b_w_out", "f_w_in", "f_w_out")
OUT_TYPE = ("a_w_out", "b_w_out", "f_w_out")
SMALL_SHARDED = (("meta", 1), ("a_conv_w", 2), ("a_conv_b", 1), ("a_b_r", 1), ("a_b_i", 1), ("a_lambda", 1),
                 ("f_conv_w", 2))
SMALL_REPLICATED = ("a_w_r", "a_w_i", "kv_f_b", "f_conv_b", "ln1_g", "ln1_b", "ln2_g", "ln2_b")


def _train_step(x, loss_target, weights, mom_m, mom_v):
    S = N_SHARDS
    seq, d = x.shape[1], x.shape[2]
    nm = weights["meta"].shape[0]
    la = weights["a_w_in"].shape[0]
    lb = weights["b_w_in"].shape[0]
    depth = la + lb
    dr = weights["a_w_out"].shape[1] * S
    nb, bs = weights["a_w_r"].shape[1], weights["a_w_r"].shape[2]
    per_group = (LANES // math.gcd(bs, LANES))
    gs = per_group * bs
    heads = weights["kv_f_b"].shape[0]
    dff = weights["f_w_out"].shape[1] * S
    nkv = 2 * d + heads
    nkv_s = weights["kv_w"].shape[1]
    nkvp = _round_up(2 * d + LANES, 768) if 2 * d + LANES > 768 else 2 * d + LANES
    alpha = (2 * depth) ** 0.25
    xi, yi, ci = _coords()
    shard = 2 * xi + yi
    core_arr = jnp.reshape(ci, (1,)).astype(jnp.int32)
    shard_arr = jnp.reshape(shard, (1,)).astype(jnp.int32)

    big2d = {k: weights[k].reshape(-1, weights[k].shape[-1]) for k in BIG}
    parts = [cast_halves(big2d[k], "cast_" + k) for k in BIG]
    small_local = [weights[k] for k, _ in SMALL_SHARDED]
    sm_flat = _pack(small_local, 2 * SUBLANES * LANES)
    parts.append(sm_flat.reshape(2, -1, LANES))
    gathered = gather_shards(parts, "gather_weights")
    gw = {}
    for k, g in zip(BIG, gathered[:-1]):
        w = weights[k]
        lyr = w.shape[0] if w.ndim == 3 else 1
        g = g.reshape(S, lyr, w.shape[-2], w.shape[-1])
        if k in OUT_TYPE:
            g = jnp.moveaxis(g, 0, 1).reshape(1, lyr, S * w.shape[-2], w.shape[-1])
        gw[k] = g
    sm_all = gathered[-1].reshape(S, -1)
    small_full = {}
    per_shard = [_unpack(sm_all[s], [a.shape for a in small_local]) for s in range(S)]
    for idx, (k, axis) in enumerate(SMALL_SHARDED):
        small_full[k] = jnp.concatenate([per_shard[s][idx] for s in range(S)], axis=axis)
    kv_cat = jnp.moveaxis(gw["kv_w"][:, 0], 0, 1).reshape(d, S * nkv_s)
    kv_pad = jnp.pad(kv_cat, ((0, 0), (0, nkvp - nkv))).reshape(1, 1, d, nkvp)
    fb_pad = jnp.pad(weights["kv_f_b"], (0, LANES - heads)).reshape(1, LANES)
    wr_g = [_block_diag(weights["a_w_r"][l], per_group) for l in range(la)]
    wi_g = [_block_diag(weights["a_w_i"][l], per_group) for l in range(la)]
    row = lambda v: v.reshape(1, -1)

    h, hb = embed_fwd(small_full["meta"], x[0], "embed")
    saved = []
    kvz = c_pad = ct_pad = None
    _, tp, _ = _attn_geometry(nm + seq)
    t = nm + seq
    for l in range(depth):
        sv = {"hb_in": hb}
        if l < la:
            gr = mm_in(hb, gw["a_w_in"], l, F32, f"a{l}_in")
            rc, rcb = a_conv_fwd(gr, small_full["a_conv_w"][l], row(small_full["a_conv_b"][l]), f"a{l}_conv")
            r_pre, i_pre = mm_bd(rcb, wr_g[l], wi_g[l], f"a{l}_gates")
            hs, gb = a_elem_fwd(gr, rc, r_pre, i_pre, row(small_full["a_b_r"][l]), row(small_full["a_b_i"][l]),
                                row(small_full["a_lambda"][l]), f"a{l}_lru")
            mix = mm_out(gb, gw["a_w_out"], l, f"a{l}_out")
            sv.update(gr=gr, rc=rc, rcb=rcb, r_pre=r_pre, i_pre=i_pre, hs=hs, gb=gb)
        else:
            j = l - la
            if j == 0:
                kvz = mm_in(hb, kv_pad, 0, F32, "kv_proj")
                cum = kv_fwd(kvz, fb_pad, d, "kv_forget")
                c_pad = jnp.pad(cum, ((0, tp - t), (0, 0)))
                ct_pad = jnp.pad(cum[:, :heads].T, ((0, 0), (0, tp - t))).reshape(heads, 1, tp)
                kv_hb = hb
            qg = mm_in(hb, gw["b_w_in"], j, F32, f"b{j}_in")
            o, mob = attn_fwd(qg, kvz, c_pad, ct_pad, d, heads, f"b{j}_attn")
            mix = mm_out(mob, gw["b_w_out"], j, f"b{j}_out")
            sv.update(qg=qg, o=o, mob=mob)
        h1, h1b, xh1, rs1 = ln_fwd(h, mix, row(weights["ln1_g"][l]), row(weights["ln1_b"][l]), alpha, f"ln1_{l}")
        zf = mm_in(h1b, gw["f_w_in"], l, F32, f"f{l}_in")
        ffb = f_elem_fwd(zf, small_full["f_conv_w"][l], row(weights["f_conv_b"][l]), f"f{l}_act")
        ffo = mm_out(ffb, gw["f_w_out"], l, f"f{l}_out")
        h2, h2b, xh2, rs2 = ln_fwd(h1, ffo, row(weights["ln2_g"][l]), row(weights["ln2_b"][l]), alpha, f"ln2_{l}")
        sv.update(h1b=h1b, xh1=xh1, rs1=rs1, zf=zf, ffb=ffb, xh2=xh2, rs2=rs2)
        saved.append(sv)
        h, hb = h2, h2b
    loss11, dy = loss_fwd_bwd(h, loss_target[0], nm, "loss")

    gbuf = {k: lax.empty(gw[k].shape, BF16) for k in BIG if k != "kv_w"}
    g_small = {}
    per_layer = {k: [None] * n for k, n in (
        ("a_conv_w", la), ("a_conv_b", la), ("a_w_r", la), ("a_b_r", la), ("a_w_i", la), ("a_b_i", la),
        ("a_lambda", la), ("f_conv_w", depth), ("f_conv_b", depth), ("ln1_g", depth), ("ln1_b", depth),
        ("ln2_g", depth), ("ln2_b", depth))}
    adds = [(dy, 1.0)]
    dks, dvs, dcs = [], [], []
    for l in reversed(range(depth)):
        sv = saved[l]
        ds2, ds2b, dg2, db2 = ln_bwd(adds, sv["xh2"], sv["rs2"], row(weights["ln2_g"][l]), f"ln2_{l}_bwd")
        per_layer["ln2_g"][l], per_layer["ln2_b"][l] = dg2[0], db2[0]
        dff_v = mm_out_nt(ds2b, gw["f_w_out"], l, f"f{l}_out_dx")
        gbuf["f_w_out"] = mm_tn(sv["ffb"], ds2b, gbuf["f_w_out"], l, f"f{l}_out_dw")
        dzg, dzv, dwg, dwv, dbg, dbv = f_elem_bwd(sv["zf"], dff_v, small_full["f_conv_w"][l],
                                                  row(weights["f_conv_b"][l]), f"f{l}_act_bwd")
        dzb = jnp.concatenate([dzg, dzv], axis=1)
        per_layer["f_conv_w"][l] = jnp.concatenate([dwg, dwv], axis=1)
        per_layer["f_conv_b"][l] = jnp.concatenate([dbg, dbv], axis=1)[0]
        dh1_f = mm_in_nt(dzb, gw["f_w_in"], l, f"f{l}_in_dx")
        gbuf["f_w_in"] = mm_tn(sv["h1b"], dzb, gbuf["f_w_in"], l, f"f{l}_in_dw")
        ds1, ds1b, dg1, db1 = ln_bwd([(ds2, alpha), (dh1_f, 1.0)], sv["xh1"], sv["rs1"],
                                     row(weights["ln1_g"][l]), f"ln1_{l}_bwd")
        per_layer["ln1_g"][l], per_layer["ln1_b"][l] = dg1[0], db1[0]
        if l < la:
            dgv = mm_out_nt(ds1b, gw["a_w_out"], l, f"a{l}_out_dx")
            gbuf["a_w_out"] = mm_tn(sv["gb"], ds1b, gbuf["a_w_out"], l, f"a{l}_out_dw")
            dgate_b, drp_b, dip_b, drc_d, dlam, dbr, dbi = a_elem_bwd(
                dgv, sv["gr"], sv["rc"], sv["r_pre"], sv["i_pre"], sv["hs"], row(small_full["a_b_r"][l]),
                row(small_full["a_b_i"][l]), row(small_full["a_lambda"][l]), f"a{l}_lru_bwd")
            drc_g = mm_bd_nt(drp_b, dip_b, wr_g[l], wi_g[l], f"a{l}_gates_dx")
            dwr_g, dwi_g = mm_bd_tn(sv["rcb"], drp_b, dip_b, gs, f"a{l}_gates_dw")
            drec_b, dcw, dcb = a_conv_bwd(drc_d, drc_g, sv["gr"], small_full["a_conv_w"][l], f"a{l}_conv_bwd")
            per_layer["a_w_r"][l] = _block_diag_extract(dwr_g, per_group, bs)
            per_layer["a_w_i"][l] = _block_diag_extract(dwi_g, per_group, bs)
            per_layer["a_lambda"][l], per_layer["a_b_r"][l], per_layer["a_b_i"][l] = dlam[0], dbr[0], dbi[0]
            per_layer["a_conv_w"][l], per_layer["a_conv_b"][l] = dcw, dcb[0]
            dgr_b = jnp.concatenate([dgate_b, drec_b], axis=1)
            dh_m = mm_in_nt(dgr_b, gw["a_w_in"], l, f"a{l}_in_dx")
            gbuf["a_w_in"] = mm_tn(sv["hb_in"], dgr_b, gbuf["a_w_in"], l, f"a{l}_in_dw")
        else:
            j = l - la
            dmo = mm_out_nt(ds1b, gw["b_w_out"], j, f"b{j}_out_dx")
            gbuf["b_w_out"] = mm_tn(sv["mob"], ds1b, gbuf["b_w_out"], j, f"b{j}_out_dw")
            dq_b, dog_b, dk, dv, dct = attn_bwd(dmo, sv["qg"], kvz, sv["o"], c_pad, ct_pad, d, heads,
                                                f"b{j}_attn_bwd")
            dks.append(dk)
            dvs.append(dv)
            dcs.append(jnp.pad(dct[:, 0, :t].T, ((0, 0), (0, LANES - heads))))
            dqg_b = jnp.concatenate([dq_b, dog_b], axis=1)
            dh_m = mm_in_nt(dqg_b, gw["b_w_in"], j, f"b{j}_in_dx")
            gbuf["b_w_in"] = mm_tn(sv["hb_in"], dqg_b, gbuf["b_w_in"], j, f"b{j}_in_dw")
        adds = [(ds1, alpha), (dh_m, 1.0)]
        if l == la:
            dzf_b, dfb = kv_bwd(dcs, kvz, fb_pad, d, "kv_forget_bwd")
            dk_b = add_cast(dks[0], dks[1], "kv_dk") if lb == 2 else None
            dv_b = add_cast(dvs[0], dvs[1], "kv_dv") if lb == 2 else None
            dz_kv = jnp.concatenate([dk_b, dv_b, dzf_b, jnp.zeros((t, nkvp - 2 * d - LANES), BF16)], axis=1)
            dh_kv = mm_in_nt(dz_kv, kv_pad, 0, "kv_proj_dx")
            kv_gbuf = mm_tn(kv_hb, dz_kv, lax.empty((1, 1, d, nkvp), BF16), 0, "kv_proj_dw")
            g_small["kv_f_b"] = dfb[0, :heads]
            adds.append((dh_kv, 1.0))
    g_meta, g_x = embed_bwd(adds, nm, "embed_bwd")

    big_grads = []
    for k in BIG:
        if k == "kv_w":
            g4 = jnp.moveaxis(kv_gbuf[0, 0, :, :nkv].reshape(d, S, nkv_s), 1, 0)
        elif k in OUT_TYPE:
            _, lyr, kk, nn = gbuf[k].shape
            g4 = jnp.moveaxis(gbuf[k].reshape(lyr, S, kk // S, nn), 1, 0)
        else:
            g4 = gbuf[k]
        rows_total = math.prod(g4.shape[1:-1])
        big_grads.append(g4.reshape(S, 2, rows_total // 2, g4.shape[-1]))

    small_names = list(SMALL_REPLICATED) + [k for k, _ in SMALL_SHARDED]
    g_small["meta"] = g_meta
    for k, vals in per_layer.items():
        g_small[k] = jnp.stack(vals)
    small_shapes = {k: (weights[k].shape if k in SMALL_REPLICATED else g_small[k].shape) for k in small_names}
    sm_g = _pack([g_small[k].reshape(small_shapes[k]) for k in small_names], S * 2 * SUBLANES * LANES)
    big_grads.append(sm_g.reshape(S, 2, -1, LANES))

    from_sib = swap_halves(big_grads, "grad_swap")
    csums = [add_halves(core_arr, g, o, f"chip_sum_{i}") for i, (g, o) in enumerate(zip(big_grads, from_sib))]
    from_chips = chip_all_to_all(csums, "grad_all_to_all")
    finals = [add_four(shard_arr, cs, rv, f"owner_sum_{i}") for i, (cs, rv) in enumerate(zip(csums, from_chips))]
    joined = join_halves(finals, "grad_join")
    sm_red = gather_full(joined[-1], "small_gather").reshape(-1)

    out_g, out_d, out_m, out_v = {}, {}, {}, {}
    for k, gj in zip(BIG, joined[:-1]):
        w2 = big2d[k]
        g2 = gj.reshape(w2.shape)
        dlt, mn, vn = adamw(w2, g2, mom_m[k].reshape(w2.shape), mom_v[k].reshape(w2.shape), "adamw_" + k)
        shp = weights[k].shape
        out_g[k], out_d[k], out_m[k], out_v[k] = g2.reshape(shp), dlt.reshape(shp), mn.reshape(shp), vn.reshape(shp)
    sm_vals = dict(zip(small_names, _unpack(sm_red, [small_shapes[k] for k in small_names])))
    local_small = {}
    for k in SMALL_REPLICATED:
        local_small[k] = sm_vals[k]
    for k, axis in SMALL_SHARDED:
        size = weights[k].shape[axis]
        local_small[k] = lax.dynamic_slice_in_dim(sm_vals[k], shard * size, size, axis=axis)
    mult = SUBLANES * LANES
    pk = lambda src: _pack([src[k] for k in small_names], mult).reshape(-1, LANES)
    dlt, mn, vn = adamw(pk(weights), pk(local_small), pk(mom_m), pk(mom_v), "adamw_small")
    shapes_local = [weights[k].shape for k in small_names]
    for dst, packed in ((out_d, dlt), (out_m, mn), (out_v, vn)):
        for k, val in zip(small_names, _unpack(packed.reshape(-1), shapes_local)):
            dst[k] = val
    for k in small_names:
        out_g[k] = local_small[k]

    loss = lax.psum(loss11[0, 0], ("x", "y", "c"))
    return (loss, g_x[None], *[out_g[k] for k in WEIGHT_ORDER], *[out_d[k] for k in WEIGHT_ORDER],
            *[out_m[k] for k in WEIGHT_ORDER], *[out_v[k] for k in WEIGHT_ORDER])
```

```python
import functools
import math

import jax
import jax.numpy as jnp
from jax import lax
from jax.experimental import pallas as pl
from jax.experimental.pallas import tpu as pltpu

F32 = jnp.float32
BF16 = jnp.bfloat16

LRU_C = 8.0
LN_EPS = 1e-5
ADAM_LR = 0.001
ADAM_B1 = 0.9
ADAM_B2 = 0.999
ADAM_EPS = 1e-08
ADAM_WD = 0.01
ADAM_STEP = 10

LANES = 128
SUBLANES = 8
V7X_VMEM_BYTES = 64 * 1024 * 1024
VMEM_LIMIT = V7X_VMEM_BYTES * 7 // 8
N_SHARDS = 4
GELU_C0 = math.sqrt(2.0 / math.pi)
GELU_C1 = 0.044715
NEG_BIG = -1e30


def _cp(*sem):
    return pltpu.CompilerParams(dimension_semantics=tuple(sem), vmem_limit_bytes=VMEM_LIMIT)


def _tile(n, cap, mult=LANES):
    best = None
    d = mult
    while d <= min(n, cap):
        if n % d == 0:
            best = d
        d += mult
    return n if best is None else best


def _row_block(t):
    if t % 3 == 0 and (t // 3) % 16 == 0:
        return t // 3
    return t


def _round_up(n, m):
    return (n + m - 1) // m * m


def _sigmoid(v):
    return 1.0 / (1.0 + jnp.exp(-v))


def _softplus(v):
    return jnp.maximum(v, 0.0) + jnp.log(1.0 + jnp.exp(-jnp.abs(v)))


def _gelu_parts(v):
    v2 = v * v
    u = GELU_C0 * (v + GELU_C1 * v * v2)
    t = jnp.tanh(u)
    g = 0.5 * v * (1.0 + t)
    dg = 0.5 * (1.0 + t) + 0.5 * v * (1.0 - t * t) * (GELU_C0 * (1.0 + 3.0 * GELU_C1 * v2))
    return g, dg


def _gelu(v):
    u = GELU_C0 * (v + GELU_C1 * v * v * v)
    return 0.5 * v * (1.0 + jnp.tanh(u))


def _neg_expm1(v):
    series = -v * (1.0 + 0.5 * v * (1.0 + (v / 3.0) * (1.0 + 0.25 * v)))
    return jnp.where(v > -0.05, series, 1.0 - jnp.exp(v))


def _shift_down(v, j):
    if j == 0:
        return v
    rows = lax.broadcasted_iota(jnp.int32, v.shape, 0)
    return jnp.where(rows >= j, pltpu.roll(v, j, 0), 0.0)


def _shift_up(v, j):
    if j == 0:
        return v
    n = v.shape[0]
    rows = lax.broadcasted_iota(jnp.int32, v.shape, 0)
    return jnp.where(rows < n - j, pltpu.roll(v, n - j, 0), 0.0)


def _scan_rows(a_ref, b_ref, out_ref, n_rows, width, reverse):
    n_groups = n_rows // SUBLANES
    rows = lax.broadcasted_iota(jnp.int32, (SUBLANES, width), 0)
    edge = 0 if reverse else SUBLANES - 1

    def body(g, carry):
        grp = (n_groups - 1 - g) if reverse else g
        off = pl.multiple_of(grp * SUBLANES, SUBLANES)
        b = b_ref[pl.ds(off, SUBLANES), :]
        a = None if a_ref is None else a_ref[pl.ds(off, SUBLANES), :]
        for d in (1, 2, 4):
            if reverse:
                keep = rows < SUBLANES - d
                sh = SUBLANES - d
            else:
                keep = rows >= d
                sh = d
            b_s = jnp.where(keep, pltpu.roll(b, sh, 0), 0.0)
            if a is None:
                b = b + b_s
            else:
                a_s = jnp.where(keep, pltpu.roll(a, sh, 0), 1.0)
                b = a * b_s + b
                a = a * a_s
        h = b + carry if a is None else b + a * carry
        out_ref[pl.ds(off, SUBLANES), :] = h
        return jnp.sum(jnp.where(rows == edge, h, 0.0), axis=0, keepdims=True)

    lax.fori_loop(0, n_groups, body, jnp.zeros((1, width), F32))


def mm_in(x, w, layer, out_dtype, name):
    t, k = x.shape
    s_n, _, _, ns = w.shape
    tn = _tile(ns, 1408)
    nj = ns // tn
    rb = _row_block(t)

    def body(x_ref, w_ref, o_ref):
        o_ref[...] = jnp.dot(x_ref[...], w_ref[...], preferred_element_type=F32).astype(o_ref.dtype)

    return pl.pallas_call(
        body, name=name, grid=(s_n, nj, t // rb),
        in_specs=[pl.BlockSpec((rb, k), lambda s, j, r: (r, 0)),
                  pl.BlockSpec((None, None, k, tn), lambda s, j, r: (s, layer, 0, j))],
        out_specs=pl.BlockSpec((rb, tn), lambda s, j, r: (r, s * nj + j)),
        out_shape=jax.ShapeDtypeStruct((t, s_n * ns), out_dtype),
        compiler_params=_cp("parallel", "parallel", "parallel"))(x, w)


def mm_out(x, w, layer, name):
    t, k = x.shape
    n = w.shape[3]
    rb = _row_block(t)

    def body(x_ref, w_ref, o_ref):
        o_ref[...] = jnp.dot(x_ref[...], w_ref[...], preferred_element_type=F32)

    return pl.pallas_call(
        body, name=name, grid=(t // rb,),
        in_specs=[pl.BlockSpec((rb, k), lambda r: (r, 0)),
                  pl.BlockSpec((None, None, k, n), lambda r: (0, layer, 0, 0))],
        out_specs=pl.BlockSpec((rb, n), lambda r: (r, 0)),
        out_shape=jax.ShapeDtypeStruct((t, n), F32),
        compiler_params=_cp("parallel"))(x, w)


def mm_in_nt(dy, w, layer, name):
    t, _ = dy.shape
    s_n, _, k, ns = w.shape
    tn = _tile(ns, 1408)
    nj = ns // tn
    rb = _row_block(t)

    def body(dy_ref, w_ref, o_ref):
        @pl.when((pl.program_id(1) == 0) & (pl.program_id(2) == 0))
        def _():
            o_ref[...] = jnp.zeros_like(o_ref)
        o_ref[...] += lax.dot_general(dy_ref[...], w_ref[...], (((1,), (1,)), ((), ())),
                                      preferred_element_type=F32)

    return pl.pallas_call(
        body, name=name, grid=(t // rb, s_n, nj),
        in_specs=[pl.BlockSpec((rb, tn), lambda r, s, j: (r, s * nj + j)),
                  pl.BlockSpec((None, None, k, tn), lambda r, s, j: (s, layer, 0, j))],
        out_specs=pl.BlockSpec((rb, k), lambda r, s, j: (r, 0)),
        out_shape=jax.ShapeDtypeStruct((t, k), F32),
        compiler_params=_cp("parallel", "arbitrary", "arbitrary"))(dy, w)


def mm_out_nt(dy, w, layer, name):
    t, n = dy.shape
    k = w.shape[2]
    rb = _row_block(t)

    def body(dy_ref, w_ref, o_ref):
        o_ref[...] = lax.dot_general(dy_ref[...], w_ref[...], (((1,), (1,)), ((), ())),
                                     preferred_element_type=F32)

    return pl.pallas_call(
        body, name=name, grid=(t // rb,),
        in_specs=[pl.BlockSpec((rb, n), lambda r: (r, 0)),
                  pl.BlockSpec((None, None, k, n), lambda r: (0, layer, 0, 0))],
        out_specs=pl.BlockSpec((rb, k), lambda r: (r, 0)),
        out_shape=jax.ShapeDtypeStruct((t, k), F32),
        compiler_params=_cp("parallel"))(dy, w)


def mm_tn(x, dy, buf, layer, name):
    t, _ = x.shape
    s_n = buf.shape[0]
    kb, nb = buf.shape[2], buf.shape[3]
    tk = _tile(kb, 1408)
    tn = _tile(nb, 1408)
    nkb, nnb = kb // tk, nb // tn

    def body(x_ref, dy_ref, buf_ref, o_ref):
        del buf_ref
        o_ref[...] = lax.dot_general(x_ref[...], dy_ref[...], (((0,), (0,)), ((), ())),
                                     preferred_element_type=F32).astype(o_ref.dtype)

    return pl.pallas_call(
        body, name=name, grid=(s_n, nkb, nnb),
        in_specs=[pl.BlockSpec((t, tk), lambda s, a, b: (0, a)),
                  pl.BlockSpec((t, tn), lambda s, a, b: (0, s * nnb + b)),
                  pl.BlockSpec(memory_space=pl.ANY)],
        out_specs=pl.BlockSpec((None, None, tk, tn), lambda s, a, b: (s, layer, a, b)),
        out_shape=jax.ShapeDtypeStruct(buf.shape, buf.dtype),
        input_output_aliases={2: 0},
        compiler_params=_cp("parallel", "parallel", "parallel"))(x, dy, buf)


def mm_bd(x, wr, wi, name):
    t, _ = x.shape
    g_n, gs, _ = wr.shape
    rb = _row_block(t)

    def body(x_ref, wr_ref, wi_ref, r_ref, i_ref):
        xv = x_ref[...]
        r_ref[...] = jnp.dot(xv, wr_ref[...], preferred_element_type=F32)
        i_ref[...] = jnp.dot(xv, wi_ref[...], preferred_element_type=F32)

    blk = pl.BlockSpec((rb, gs), lambda g, r: (r, g))
    wspec = pl.BlockSpec((None, gs, gs), lambda g, r: (g, 0, 0))
    return pl.pallas_call(
        body, name=name, grid=(g_n, t // rb), in_specs=[blk, wspec, wspec], out_specs=[blk, blk],
        out_shape=[jax.ShapeDtypeStruct((t, g_n * gs), F32)] * 2,
        compiler_params=_cp("parallel", "parallel"))(x, wr, wi)


def mm_bd_nt(dr, di, wr, wi, name):
    t, _ = dr.shape
    g_n, gs, _ = wr.shape
    rb = _row_block(t)
    nt = (((1,), (1,)), ((), ()))

    def body(dr_ref, di_ref, wr_ref, wi_ref, o_ref):
        o_ref[...] = (lax.dot_general(dr_ref[...], wr_ref[...], nt, preferred_element_type=F32)
                      + lax.dot_general(di_ref[...], wi_ref[...], nt, preferred_element_type=F32))

    blk = pl.BlockSpec((rb, gs), lambda g, r: (r, g))
    wspec = pl.BlockSpec((None, gs, gs), lambda g, r: (g, 0, 0))
    return pl.pallas_call(
        body, name=name, grid=(g_n, t // rb), in_specs=[blk, blk, wspec, wspec], out_specs=blk,
        out_shape=jax.ShapeDtypeStruct((t, g_n * gs), F32),
        compiler_params=_cp("parallel", "parallel"))(dr, di, wr, wi)


def mm_bd_tn(x, dr, di, gs, name):
    t, w = x.shape
    g_n = w // gs
    tn_dims = (((0,), (0,)), ((), ()))

    def body(x_ref, dr_ref, di_ref, gr_ref, gi_ref):
        xv = x_ref[...]
        gr_ref[...] = lax.dot_general(xv, dr_ref[...], tn_dims, preferred_element_type=F32)
        gi_ref[...] = lax.dot_general(xv, di_ref[...], tn_dims, preferred_element_type=F32)

    blk = pl.BlockSpec((t, gs), lambda g: (0, g))
    ospec = pl.BlockSpec((None, gs, gs), lambda g: (g, 0, 0))
    return pl.pallas_call(
        body, name=name, grid=(g_n,), in_specs=[blk, blk, blk], out_specs=[ospec, ospec],
        out_shape=[jax.ShapeDtypeStruct((g_n, gs, gs), F32)] * 2,
        compiler_params=_cp("parallel"))(x, dr, di)


def embed_fwd(meta, x2d, name):
    nm, d = meta.shape
    seq = x2d.shape[0]
    t = nm + seq
    cb = _tile(d, 256)

    def body(m_ref, x_ref, h_ref, hb_ref):
        h_ref[pl.ds(0, nm), :] = m_ref[...]
        h_ref[pl.ds(nm, seq), :] = x_ref[...]
        hb_ref[pl.ds(0, nm), :] = m_ref[...].astype(BF16)
        hb_ref[pl.ds(nm, seq), :] = x_ref[...].astype(BF16)

    return pl.pallas_call(
        body, name=name, grid=(d // cb,),
        in_specs=[pl.BlockSpec((nm, cb), lambda j: (0, j)), pl.BlockSpec((seq, cb), lambda j: (0, j))],
        out_specs=[pl.BlockSpec((t, cb), lambda j: (0, j))] * 2,
        out_shape=[jax.ShapeDtypeStruct((t, d), F32), jax.ShapeDtypeStruct((t, d), BF16)],
        compiler_params=_cp("parallel"))(meta, x2d)


def embed_bwd(adds, nm, name):
    t, d = adds[0][0].shape
    seq = t - nm
    cb = _tile(d, 256)
    scales = [s for _, s in adds]
    n = len(adds)

    def body(*refs):
        tot = None
        for r, sc in zip(refs[:n], scales):
            term = r[...] if sc == 1.0 else sc * r[...]
            tot = term if tot is None else tot + term
        gm_ref, gx_ref = refs[n], refs[n + 1]
        gm_ref[...] = tot[0:nm]
        gx_ref[...] = tot[nm:t]

    return pl.pallas_call(
        body, name=name, grid=(d // cb,),
        in_specs=[pl.BlockSpec((t, cb), lambda j: (0, j))] * n,
        out_specs=[pl.BlockSpec((nm, cb), lambda j: (0, j)), pl.BlockSpec((seq, cb), lambda j: (0, j))],
        out_shape=[jax.ShapeDtypeStruct((nm, d), F32), jax.ShapeDtypeStruct((seq, d), F32)],
        compiler_params=_cp("parallel"))(*[a for a, _ in adds])


def loss_fwd_bwd(h, tgt, nm, name):
    t, d = h.shape
    seq = t - nm
    cb = _tile(d, 256)
    inv_d = 1.0 / d

    def body(h_ref, t_ref, loss_ref, dy_ref):
        @pl.when(pl.program_id(0) == 0)
        def _():
            loss_ref[...] = jnp.zeros_like(loss_ref)
        err = h_ref[pl.ds(nm, seq), :] - t_ref[...]
        dy_ref[pl.ds(0, nm), :] = jnp.zeros((nm, cb), F32)
        dy_ref[pl.ds(nm, seq), :] = err * inv_d
        loss_ref[...] += (0.5 * inv_d) * jnp.sum(err * err, keepdims=True)

    return pl.pallas_call(
        body, name=name, grid=(d // cb,),
        in_specs=[pl.BlockSpec((t, cb), lambda j: (0, j)), pl.BlockSpec((seq, cb), lambda j: (0, j))],
        out_specs=[pl.BlockSpec((1, 1), lambda j: (0, 0)), pl.BlockSpec((t, cb), lambda j: (0, j))],
        out_shape=[jax.ShapeDtypeStruct((1, 1), F32), jax.ShapeDtypeStruct((t, d), F32)],
        compiler_params=_cp("arbitrary"))(h, tgt)


def ln_fwd(h, mix, g, b, alpha, name):
    t, d = h.shape
    rb = _row_block(t)

    def body(h_ref, m_ref, g_ref, b_ref, y_ref, yb_ref, xh_ref, rs_ref):
        s = alpha * h_ref[...] + m_ref[...]
        mu = jnp.mean(s, axis=-1, keepdims=True)
        c = s - mu
        var = jnp.mean(c * c, axis=-1, keepdims=True)
        rstd = lax.rsqrt(var + LN_EPS)
        xh = c * rstd
        y = xh * g_ref[...] + b_ref[...]
        y_ref[...] = y
        yb_ref[...] = y.astype(BF16)
        xh_ref[...] = xh
        rs_ref[...] = rstd

    row = pl.BlockSpec((rb, d), lambda r: (r, 0))
    vec = pl.BlockSpec((1, d), lambda r: (0, 0))
    return pl.pallas_call(
        body, name=name, grid=(t // rb,), in_specs=[row, row, vec, vec],
        out_specs=[row, row, row, pl.BlockSpec((rb, 1), lambda r: (r, 0))],
        out_shape=[jax.ShapeDtypeStruct((t, d), F32), jax.ShapeDtypeStruct((t, d), BF16),
                   jax.ShapeDtypeStruct((t, d), F32), jax.ShapeDtypeStruct((t, 1), F32)],
        compiler_params=_cp("parallel"))(h, mix, g, b)


def ln_bwd(adds, xhat, rstd, g, name):
    t, d = xhat.shape
    rb = _row_block(t)
    scales = [s for _, s in adds]
    n = len(adds)

    def body(*refs):
        xh_ref, rs_ref, g_ref = refs[n:n + 3]
        ds_ref, dsb_ref, dg_ref, db_ref = refs[n + 3:]
        dy = None
        for r, sc in zip(refs[:n], scales):
            term = r[...] if sc == 1.0 else sc * r[...]
            dy = term if dy is None else dy + term

        @pl.when(pl.program_id(0) == 0)
        def _():
            dg_ref[...] = jnp.zeros_like(dg_ref)
            db_ref[...] = jnp.zeros_like(db_ref)

        xh = xh_ref[...]
        dxh = dy * g_ref[...]
        m1 = jnp.mean(dxh, axis=-1, keepdims=True)
        m2 = jnp.mean(dxh * xh, axis=-1, keepdims=True)
        ds = rs_ref[...] * (dxh - m1 - xh * m2)
        ds_ref[...] = ds
        dsb_ref[...] = ds.astype(BF16)
        dg_ref[...] += jnp.sum(dy * xh, axis=0, keepdims=True)
        db_ref[...] += jnp.sum(dy, axis=0, keepdims=True)

    row = pl.BlockSpec((rb, d), lambda r: (r, 0))
    vec = pl.BlockSpec((1, d), lambda r: (0, 0))
    return pl.pallas_call(
        body, name=name, grid=(t // rb,),
        in_specs=[row] * n + [row, pl.BlockSpec((rb, 1), lambda r: (r, 0)), vec],
        out_specs=[row, row, vec, vec],
        out_shape=[jax.ShapeDtypeStruct((t, d), F32), jax.ShapeDtypeStruct((t, d), BF16),
                   jax.ShapeDtypeStruct((1, d), F32), jax.ShapeDtypeStruct((1, d), F32)],
        compiler_params=_cp("arbitrary"))(*[a for a, _ in adds], xhat, rstd, g)


def _conv_fwd_val(xv, w_ref, b_ref, width):
    y = b_ref[...]
    for j in range(width):
        y = y + _shift_down(xv, j) * w_ref[pl.ds(width - 1 - j, 1), :]
    return y


def _conv_bwd_val(dout, xv, w_ref, width):
    dx = None
    dws = [None] * width
    for j in range(width):
        k = width - 1 - j
        term = _shift_up(dout, j) * w_ref[pl.ds(k, 1), :]
        dx = term if dx is None else dx + term
        dws[k] = jnp.sum(dout * _shift_down(xv, j), axis=0, keepdims=True)
    return dx, dws, jnp.sum(dout, axis=0, keepdims=True)


def a_conv_fwd(gr, cw, cbias, name):
    t, two_dr = gr.shape
    dr = two_dr // 2
    width = cw.shape[0]
    cb = _tile(dr, 256)
    off = dr // cb

    def body(x_ref, w_ref, b_ref, rc_ref, rcb_ref):
        y = _conv_fwd_val(x_ref[...], w_ref, b_ref, width)
        rc_ref[...] = y
        rcb_ref[...] = y.astype(BF16)

    return pl.pallas_call(
        body, name=name, grid=(dr // cb,),
        in_specs=[pl.BlockSpec((t, cb), lambda j: (0, off + j)),
                  pl.BlockSpec((width, cb), lambda j: (0, j)), pl.BlockSpec((1, cb), lambda j: (0, j))],
        out_specs=[pl.BlockSpec((t, cb), lambda j: (0, j))] * 2,
        out_shape=[jax.ShapeDtypeStruct((t, dr), F32), jax.ShapeDtypeStruct((t, dr), BF16)],
        compiler_params=_cp("parallel"))(gr, cw, cbias)


def a_conv_bwd(drc_a, drc_b, gr, cw, name):
    t, two_dr = gr.shape
    dr = two_dr // 2
    width = cw.shape[0]
    cb = _tile(dr, 256)
    off = dr // cb

    def body(da_ref, db_ref, x_ref, w_ref, dx_ref, dw_ref, dbias_ref):
        dout = da_ref[...] + db_ref[...]
        dx, dws, dbias = _conv_bwd_val(dout, x_ref[...], w_ref, width)
        dx_ref[...] = dx.astype(BF16)
        for k in range(width):
            dw_ref[pl.ds(k, 1), :] = dws[k]
        dbias_ref[...] = dbias

    col = pl.BlockSpec((t, cb), lambda j: (0, j))
    return pl.pallas_call(
        body, name=name, grid=(dr // cb,),
        in_specs=[col, col, pl.BlockSpec((t, cb), lambda j: (0, off + j)),
                  pl.BlockSpec((width, cb), lambda j: (0, j))],
        out_specs=[col, pl.BlockSpec((width, cb), lambda j: (0, j)), pl.BlockSpec((1, cb), lambda j: (0, j))],
        out_shape=[jax.ShapeDtypeStruct((t, dr), BF16), jax.ShapeDtypeStruct((width, dr), F32),
                   jax.ShapeDtypeStruct((1, dr), F32)],
        compiler_params=_cp("parallel"))(drc_a, drc_b, gr, cw)


def _lru_gates(r_pre, i_pre, br, bi, lam):
    r = _sigmoid(r_pre + br)
    i = _sigmoid(i_pre + bi)
    sp = _softplus(-lam)
    la = -LRU_C * r * sp
    a = jnp.exp(la)
    m = jnp.sqrt(_neg_expm1(2.0 * la))
    return r, i, sp, la, a, m


def a_elem_fwd(gr, rc, r_pre, i_pre, br, bi, lam, name):
    t, dr = rc.shape
    cb = LANES

    def body(gate_ref, rc_ref, rp_ref, ip_ref, br_ref, bi_ref, lam_ref, hs_ref, g_ref, a_s, u_s):
        rcv = rc_ref[...]
        _, i, _, _, a, m = _lru_gates(rp_ref[...], ip_ref[...], br_ref[...], bi_ref[...], lam_ref[...])
        a_s[...] = a
        u_s[...] = m * (i * rcv)
        _scan_rows(a_s, u_s, hs_ref, t, cb, reverse=False)
        g_ref[...] = (_gelu(gate_ref[...]) * hs_ref[...]).astype(BF16)

    col = pl.BlockSpec((t, cb), lambda j: (0, j))
    vec = pl.BlockSpec((1, cb), lambda j: (0, j))
    return pl.pallas_call(
        body, name=name, grid=(dr // cb,),
        in_specs=[col, col, col, col, vec, vec, vec],
        out_specs=[col, col],
        out_shape=[jax.ShapeDtypeStruct((t, dr), F32), jax.ShapeDtypeStruct((t, dr), BF16)],
        scratch_shapes=[pltpu.VMEM((t, cb), F32), pltpu.VMEM((t, cb), F32)],
        compiler_params=_cp("parallel"))(gr, rc, r_pre, i_pre, br, bi, lam)


def a_elem_bwd(dg, gr, rc, r_pre, i_pre, hs, br, bi, lam, name):
    t, dr = rc.shape
    cb = LANES

    def body(dg_ref, gate_ref, rc_ref, rp_ref, ip_ref, hs_ref, br_ref, bi_ref, lam_ref,
             dgate_ref, dr_ref, di_ref, drc_ref, dlam_ref, dbr_ref, dbi_ref, a_s, b_s, g_s):
        lamv = lam_ref[...]
        rcv = rc_ref[...]
        hsv = hs_ref[...]
        dgv = dg_ref[...]
        r, i, sp, _, a, m = _lru_gates(rp_ref[...], ip_ref[...], br_ref[...], bi_ref[...], lamv)
        ge, dge = _gelu_parts(gate_ref[...])
        dgate_ref[...] = (dgv * hsv * dge).astype(BF16)
        a_s[...] = _shift_up(a, 1)
        b_s[...] = dgv * ge
        _scan_rows(a_s, b_s, g_s, t, cb, reverse=True)
        gsum = g_s[...]
        da = gsum * _shift_down(hsv, 1)
        dm = gsum * (i * rcv)
        d_i = gsum * m * rcv
        drc_ref[...] = gsum * m * i
        dla = a * da - dm * (a * a) / m
        d_r = (-LRU_C) * sp * dla
        dsp = jnp.sum((-LRU_C) * r * dla, axis=0, keepdims=True)
        dlam_ref[...] = -dsp * _sigmoid(-lamv)
        d_rp = d_r * r * (1.0 - r)
        d_ip = d_i * i * (1.0 - i)
        dr_ref[...] = d_rp.astype(BF16)
        di_ref[...] = d_ip.astype(BF16)
        dbr_ref[...] = jnp.sum(d_rp, axis=0, keepdims=True)
        dbi_ref[...] = jnp.sum(d_ip, axis=0, keepdims=True)

    col = pl.BlockSpec((t, cb), lambda j: (0, j))
    vec = pl.BlockSpec((1, cb), lambda j: (0, j))
    big_b = jax.ShapeDtypeStruct((t, dr), BF16)
    vec_s = jax.ShapeDtypeStruct((1, dr), F32)
    return pl.pallas_call(
        body, name=name, grid=(dr // cb,),
        in_specs=[col, col, col, col, col, col, vec, vec, vec],
        out_specs=[col, col, col, col, vec, vec, vec],
        out_shape=[big_b, big_b, big_b, jax.ShapeDtypeStruct((t, dr), F32), vec_s, vec_s, vec_s],
        scratch_shapes=[pltpu.VMEM((t, cb), F32)] * 3,
        compiler_params=_cp("parallel"))(dg, gr, rc, r_pre, i_pre, hs, br, bi, lam)


def f_elem_fwd(z, cw, cbias, name):
    t, two_f = z.shape
    dff = two_f // 2
    width = cw.shape[0]
    cb = _tile(dff, 256)
    off = dff // cb

    def body(zg_ref, zv_ref, wg_ref, wv_ref, bg_ref, bv_ref, o_ref):
        zcg = _conv_fwd_val(zg_ref[...], wg_ref, bg_ref, width)
        zcv = _conv_fwd_val(zv_ref[...], wv_ref, bv_ref, width)
        o_ref[...] = (_gelu(zcg) * zcv).astype(BF16)

    lo = lambda j: (0, j)
    hi = lambda j: (0, off + j)
    return pl.pallas_call(
        body, name=name, grid=(dff // cb,),
        in_specs=[pl.BlockSpec((t, cb), lo), pl.BlockSpec((t, cb), hi),
                  pl.BlockSpec((width, cb), lo), pl.BlockSpec((width, cb), hi),
                  pl.BlockSpec((1, cb), lo), pl.BlockSpec((1, cb), hi)],
        out_specs=pl.BlockSpec((t, cb), lo),
        out_shape=jax.ShapeDtypeStruct((t, dff), BF16),
        compiler_params=_cp("parallel"))(z, z, cw, cw, cbias, cbias)


def f_elem_bwd(z, dff_g, cw, cbias, name):
    t, two_f = z.shape
    dff = two_f // 2
    width = cw.shape[0]
    cb = _tile(dff, 256)
    off = dff // cb

    def body(zg_ref, zv_ref, d_ref, wg_ref, wv_ref, bg_ref, bv_ref,
             dzg_ref, dzv_ref, dwg_ref, dwv_ref, dbg_ref, dbv_ref):
        zg = zg_ref[...]
        zv = zv_ref[...]
        zcg = _conv_fwd_val(zg, wg_ref, bg_ref, width)
        zcv = _conv_fwd_val(zv, wv_ref, bv_ref, width)
        ge, dge = _gelu_parts(zcg)
        dv = d_ref[...]
        dx, dws, dbias = _conv_bwd_val(dv * zcv * dge, zg, wg_ref, width)
        dzg_ref[...] = dx.astype(BF16)
        for k in range(width):
            dwg_ref[pl.ds(k, 1), :] = dws[k]
        dbg_ref[...] = dbias
        dx, dws, dbias = _conv_bwd_val(dv * ge, zv, wv_ref, width)
        dzv_ref[...] = dx.astype(BF16)
        for k in range(width):
            dwv_ref[pl.ds(k, 1), :] = dws[k]
        dbv_ref[...] = dbias

    lo = lambda j: (0, j)
    hi = lambda j: (0, off + j)
    col = pl.BlockSpec((t, cb), lo)
    wsp = pl.BlockSpec((width, cb), lo)
    vsp = pl.BlockSpec((1, cb), lo)
    return pl.pallas_call(
        body, name=name, grid=(dff // cb,),
        in_specs=[col, pl.BlockSpec((t, cb), hi), col, wsp, pl.BlockSpec((width, cb), hi),
                  vsp, pl.BlockSpec((1, cb), hi)],
        out_specs=[col, col, wsp, wsp, vsp, vsp],
        out_shape=[jax.ShapeDtypeStruct((t, dff), BF16)] * 2
        + [jax.ShapeDtypeStruct((width, dff), F32)] * 2 + [jax.ShapeDtypeStruct((1, dff), F32)] * 2,
        compiler_params=_cp("parallel"))(z, z, dff_g, cw, cw, cbias, cbias)


def kv_fwd(z, fb, d_model, name):
    t, _ = z.shape
    blk = 2 * d_model // LANES

    def body(z_ref, fb_ref, c_ref, lf_s):
        v = z_ref[...] + fb_ref[...]
        lf_s[...] = -_softplus(-v)
        _scan_rows(None, lf_s, c_ref, t, LANES, reverse=False)

    return pl.pallas_call(
        body, name=name, grid=(1,),
        in_specs=[pl.BlockSpec((t, LANES), lambda j: (0, blk)), pl.BlockSpec((1, LANES), lambda j: (0, 0))],
        out_specs=pl.BlockSpec((t, LANES), lambda j: (0, 0)),
        out_shape=jax.ShapeDtypeStruct((t, LANES), F32),
        scratch_shapes=[pltpu.VMEM((t, LANES), F32)],
        compiler_params=_cp("arbitrary"))(z, fb)


def kv_bwd(dcs, z, fb, d_model, name):
    t, _ = z.shape
    blk = 2 * d_model // LANES
    n = len(dcs)

    def body(*refs):
        z_ref, fb_ref, dz_ref, dfb_ref, dc_s, dl_s = refs[n:]
        tot = refs[0][...]
        for r in refs[1:n]:
            tot = tot + r[...]
        dc_s[...] = tot
        _scan_rows(None, dc_s, dl_s, t, LANES, reverse=True)
        v = z_ref[...] + fb_ref[...]
        dz = dl_s[...] * _sigmoid(-v)
        dz_ref[...] = dz.astype(BF16)
        dfb_ref[...] = jnp.sum(dz, axis=0, keepdims=True)

    full = pl.BlockSpec((t, LANES), lambda j: (0, 0))
    return pl.pallas_call(
        body, name=name, grid=(1,),
        in_specs=[full] * n + [pl.BlockSpec((t, LANES), lambda j: (0, blk)),
                               pl.BlockSpec((1, LANES), lambda j: (0, 0))],
        out_specs=[full, pl.BlockSpec((1, LANES), lambda j: (0, 0))],
        out_shape=[jax.ShapeDtypeStruct((t, LANES), BF16), jax.ShapeDtypeStruct((1, LANES), F32)],
        scratch_shapes=[pltpu.VMEM((t, LANES), F32)] * 2,
        compiler_params=_cp("arbitrary"))(*dcs, z, fb)


def add_cast(a, b, name):
    t, d = a.shape
    cb = _tile(d, 512)

    def body(a_ref, b_ref, o_ref):
        o_ref[...] = (a_ref[...] + b_ref[...]).astype(BF16)

    col = pl.BlockSpec((t, cb), lambda j: (0, j))
    return pl.pallas_call(body, name=name, grid=(d // cb,), in_specs=[col, col], out_specs=col,
                          out_shape=jax.ShapeDtypeStruct((t, d), BF16),
                          compiler_params=_cp("parallel"))(a, b)


def _attn_geometry(t):
    nqb = 6 if t > 1024 else 2
    tp = _round_up(t, LANES * nqb)
    return nqb, tp, tp // nqb


def _attn_probs(qs, ks, ccol, crow, j, i, tq, dh, scale):
    r0 = i * tq
    nk = (i + 1) * tq
    lanes = pl.ds(j * dh, dh)
    qi = qs[pl.ds(r0, tq), lanes]
    kk = ks[pl.ds(0, nk), lanes]
    s = lax.dot_general(qi, kk, (((1,), (1,)), ((), ())), preferred_element_type=F32) * scale
    s = s + ccol[r0:r0 + tq] - crow[:, 0:nk]
    rows = lax.broadcasted_iota(jnp.int32, (tq, nk), 0) + r0
    cols = lax.broadcasted_iota(jnp.int32, (tq, nk), 1)
    s = jnp.where(cols <= rows, s, NEG_BIG)
    e = jnp.exp(s - jnp.max(s, axis=1, keepdims=True))
    return e / jnp.sum(e, axis=1, keepdims=True), qi, kk


def _head_col(cs, head):
    lane = lax.broadcasted_iota(jnp.int32, cs.shape, 1)
    return jnp.sum(jnp.where(lane == head, cs, 0.0), axis=1, keepdims=True)


def attn_fwd(qg, z, c_pad, ct_pad, d_model, n_heads, name):
    t = qg.shape[0]
    dh = d_model // n_heads
    hp = LANES // dh
    nqb, tp, tq = _attn_geometry(t)
    nblk = d_model // LANES
    scale = dh ** -0.5

    def body(q_ref, og_ref, k_ref, v_ref, c_ref, ct_ref, o_ref, mo_ref, qs, ks, vs, os_):
        pad = jnp.zeros((tp - t, LANES), BF16)
        for src, dst in ((q_ref, qs), (k_ref, ks), (v_ref, vs)):
            dst[pl.ds(0, t), :] = src[...].astype(BF16)
            dst[pl.ds(t, tp - t), :] = pad
        cs = c_ref[...]
        for j in range(hp):
            ccol = _head_col(cs, pl.program_id(0) * hp + j)
            crow = ct_ref[j]
            for i in range(nqb):
                p, _, _ = _attn_probs(qs, ks, ccol, crow, j, i, tq, dh, scale)
                vv = vs[pl.ds(0, (i + 1) * tq), pl.ds(j * dh, dh)]
                os_[pl.ds(i * tq, tq), pl.ds(j * dh, dh)] = jnp.dot(
                    p.astype(BF16), vv, preferred_element_type=F32)
        o = os_[pl.ds(0, t), :]
        o_ref[...] = o
        mo_ref[...] = (o * _sigmoid(og_ref[...])).astype(BF16)

    col = lambda off: pl.BlockSpec((t, LANES), lambda p: (0, off + p))
    return pl.pallas_call(
        body, name=name, grid=(nblk,),
        in_specs=[col(0), col(nblk), col(0), col(nblk),
                  pl.BlockSpec((tp, LANES), lambda p: (0, 0)),
                  pl.BlockSpec((hp, 1, tp), lambda p: (p, 0, 0))],
        out_specs=[col(0), col(0)],
        out_shape=[jax.ShapeDtypeStruct((t, d_model), F32), jax.ShapeDtypeStruct((t, d_model), BF16)],
        scratch_shapes=[pltpu.VMEM((tp, LANES), BF16)] * 3 + [pltpu.VMEM((tp, LANES), F32)],
        compiler_params=_cp("parallel"))(qg, qg, z, z, c_pad, ct_pad)


def attn_bwd(dmo, qg, z, o, c_pad, ct_pad, d_model, n_heads, name):
    t = qg.shape[0]
    dh = d_model // n_heads
    hp = LANES // dh
    nqb, tp, tq = _attn_geometry(t)
    nblk = d_model // LANES
    scale = dh ** -0.5
    tn_dims = (((0,), (0,)), ((), ()))
    nt_dims = (((1,), (1,)), ((), ()))

    def body(dmo_ref, q_ref, og_ref, k_ref, v_ref, o_ref, c_ref, ct_ref,
             dq_ref, dog_ref, dk_ref, dv_ref, dct_ref, qs, ks, vs, dos, dqs, dks, dvs):
        pad = jnp.zeros((tp - t, LANES), BF16)
        sg = _sigmoid(og_ref[...])
        dmo_v = dmo_ref[...]
        dog_ref[...] = (dmo_v * o_ref[...] * sg * (1.0 - sg)).astype(BF16)
        dos[pl.ds(0, t), :] = (dmo_v * sg).astype(BF16)
        dos[pl.ds(t, tp - t), :] = pad
        for src, dst in ((q_ref, qs), (k_ref, ks), (v_ref, vs)):
            dst[pl.ds(0, t), :] = src[...].astype(BF16)
            dst[pl.ds(t, tp - t), :] = pad
        dks[...] = jnp.zeros_like(dks)
        dvs[...] = jnp.zeros_like(dvs)
        dct_ref[...] = jnp.zeros_like(dct_ref)
        cs = c_ref[...]
        for j in range(hp):
            ccol = _head_col(cs, pl.program_id(0) * hp + j)
            crow = ct_ref[j]
            lanes = pl.ds(j * dh, dh)
            for i in range(nqb):
                nk = (i + 1) * tq
                p, qi, kk = _attn_probs(qs, ks, ccol, crow, j, i, tq, dh, scale)
                vv = vs[pl.ds(0, nk), lanes]
                do_i = dos[pl.ds(i * tq, tq), lanes]
                dp = lax.dot_general(do_i, vv, nt_dims, preferred_element_type=F32)
                ds = p * (dp - jnp.sum(p * dp, axis=1, keepdims=True))
                ds_b = ds.astype(BF16)
                dqs[pl.ds(i * tq, tq), lanes] = jnp.dot(ds_b, kk, preferred_element_type=F32) * scale
                dks[pl.ds(0, nk), lanes] += lax.dot_general(
                    ds_b, qi, tn_dims, preferred_element_type=F32) * scale
                dvs[pl.ds(0, nk), lanes] += lax.dot_general(
                    p.astype(BF16), do_i, tn_dims, preferred_element_type=F32)
                dct_ref[j, :, pl.ds(0, nk)] -= jnp.sum(ds, axis=0, keepdims=True)
        dq_ref[...] = dqs[pl.ds(0, t), :].astype(BF16)
        dk_ref[...] = dks[pl.ds(0, t), :]
        dv_ref[...] = dvs[pl.ds(0, t), :]

    col = lambda off: pl.BlockSpec((t, LANES), lambda p: (0, off + p))
    big = lambda dt: jax.ShapeDtypeStruct((t, d_model), dt)
    return pl.pallas_call(
        body, name=name, grid=(nblk,),
        in_specs=[col(0), col(0), col(nblk), col(0), col(nblk), col(0),
                  pl.BlockSpec((tp, LANES), lambda p: (0, 0)),
                  pl.BlockSpec((hp, 1, tp), lambda p: (p, 0, 0))],
        out_specs=[col(0), col(0), col(0), col(0), pl.BlockSpec((hp, 1, tp), lambda p: (p, 0, 0))],
        out_shape=[big(BF16), big(BF16), big(F32), big(F32),
                   jax.ShapeDtypeStruct((n_heads, 1, tp), F32)],
        scratch_shapes=[pltpu.VMEM((tp, LANES), BF16)] * 4 + [pltpu.VMEM((tp, LANES), F32)] * 3,
        compiler_params=_cp("parallel"))(dmo, qg, qg, z, z, o, c_pad, ct_pad)


def cast_into_slot(shard, w2d, name):
    r, c = w2d.shape
    rh = r // 2
    tr = _tile(rh, 512, 16)
    n = rh // tr

    def body(sh_ref, w_ref, o_ref):
        del sh_ref
        o_ref[...] = w_ref[...].astype(BF16)

    return pl.pallas_call(
        body, name=name,
        grid_spec=pltpu.PrefetchScalarGridSpec(
            num_scalar_prefetch=1, grid=(2, n),
            in_specs=[pl.BlockSpec((tr, c), lambda h, i, sh: (h * n + i, 0))],
            out_specs=pl.BlockSpec((None, None, tr, c), lambda h, i, sh: (sh[0], h, i, 0))),
        out_shape=jax.ShapeDtypeStruct((N_SHARDS, 2, rh, c), BF16),
        compiler_params=_cp("parallel", "parallel"))(shard, w2d)


def add_halves(core, g, other, name):
    s_n, _, rh, c = g.shape
    tr = _tile(rh, 512, 16)

    def body(core_ref, g_ref, o_ref, out_ref):
        del core_ref
        out_ref[...] = (g_ref[...].astype(F32) + o_ref[...].astype(F32)).astype(out_ref.dtype)

    return pl.pallas_call(
        body, name=name,
        grid_spec=pltpu.PrefetchScalarGridSpec(
            num_scalar_prefetch=1, grid=(s_n, rh // tr),
            in_specs=[pl.BlockSpec((None, None, tr, c), lambda s, i, cr: (s, cr[0], i, 0)),
                      pl.BlockSpec((None, tr, c), lambda s, i, cr: (s, i, 0))],
            out_specs=pl.BlockSpec((None, tr, c), lambda s, i, cr: (s, i, 0))),
        out_shape=jax.ShapeDtypeStruct((s_n, rh, c), g.dtype),
        compiler_params=_cp("parallel", "parallel"))(core, g, other)


def add_four(shard_core, csum, recv, name):
    _, rh, c = csum.shape
    tr = _tile(rh, 512, 16)

    def body(sc_ref, a_ref, r_ref, out_ref):
        del sc_ref
        acc = a_ref[...].astype(F32)
        for k in range(3):
            acc = acc + r_ref[k].astype(F32)
        out_ref[...] = acc

    return pl.pallas_call(
        body, name=name,
        grid_spec=pltpu.PrefetchScalarGridSpec(
            num_scalar_prefetch=1, grid=(rh // tr,),
            in_specs=[pl.BlockSpec((None, tr, c), lambda i, sc: (sc[0], i, 0)),
                      pl.BlockSpec((3, tr, c), lambda i, sc: (0, i, 0))],
            out_specs=pl.BlockSpec((None, tr, c), lambda i, sc: (sc[1], i, 0))),
        out_shape=jax.ShapeDtypeStruct((2, rh, c), F32),
        compiler_params=_cp("parallel"))(shard_core, csum, recv)


def adamw(w, g, m, v, name):
    r, c = w.shape
    tr = _tile(r, 512, SUBLANES)
    c1 = 1.0 - ADAM_B1 ** ADAM_STEP
    c2 = 1.0 - ADAM_B2 ** ADAM_STEP

    def body(w_ref, g_ref, m_ref, v_ref, d_ref, mo_ref, vo_ref):
        gv = g_ref[...]
        mn = ADAM_B1 * m_ref[...] + (1.0 - ADAM_B1) * gv
        vn = ADAM_B2 * v_ref[...] + (1.0 - ADAM_B2) * (gv * gv)
        m_hat = mn / c1
        v_hat = vn / c2
        d_ref[...] = -ADAM_LR * (m_hat / (jnp.sqrt(v_hat) + ADAM_EPS) + ADAM_WD * w_ref[...])
        mo_ref[...] = mn
        vo_ref[...] = vn

    blk = pl.BlockSpec((tr, c), lambda i: (i, 0))
    return pl.pallas_call(
        body, name=name, grid=(r // tr,), in_specs=[blk] * 4, out_specs=[blk] * 3,
        out_shape=[jax.ShapeDtypeStruct((r, c), F32)] * 3,
        compiler_params=_cp("parallel"))(w, g, m, v)


def _coords():
    return lax.axis_index("x"), lax.axis_index("y"), lax.axis_index("c")


def _exchange(name, ins, out_shapes, plan, in_place=False):
    n_in = len(ins)
    n_out = len(out_shapes)
    n_rem = len(plan([None] * n_in, [None] * n_out, True))

    def body(*refs):
        in_refs = refs[:n_in]
        out_refs = refs[n_in:n_in + n_out]
        send_sems, recv_sems = refs[n_in + n_out:]
        remote = plan(list(in_refs), list(out_refs), False)
        copies = [pltpu.make_async_remote_copy(
            src_ref=src, dst_ref=dst, send_sem=send_sems.at[q], recv_sem=recv_sems.at[q],
            device_id=peer, device_id_type=pl.DeviceIdType.MESH)
            for q, (src, dst, peer, _) in enumerate(remote)]
        waited = set()
        for q, (_, _, _, after) in enumerate(remote):
            if after is not None and after not in waited:
                copies[after].wait_recv()
                waited.add(after)
            copies[q].start()
        for q, cp in enumerate(copies):
            if q not in waited:
                cp.wait_recv()
        for cp in copies:
            cp.wait_send()

    hbm = pl.BlockSpec(memory_space=pl.ANY)
    return pl.pallas_call(
        body, name=name, in_specs=[hbm] * n_in, out_specs=[hbm] * n_out, out_shape=out_shapes,
        input_output_aliases={i: i for i in range(n_in)} if in_place else {},
        scratch_shapes=[pltpu.SemaphoreType.DMA((n_rem,)), pltpu.SemaphoreType.DMA((n_rem,))],
        compiler_params=pltpu.CompilerParams(has_side_effects=True))(*ins)


def _other_chips(x, y, c):
    return [((1 - x, y, c), 2 * (1 - x) + y), ((x, 1 - y, c), 2 * x + 1 - y),
            ((1 - x, 1 - y, c), 2 * (1 - x) + 1 - y)]


def gather_shards(bufs, name):
    n = len(bufs)

    def plan(ins, outs, count_only):
        if count_only:
            return [None] * (6 * n)
        x, y, c = _coords()
        me = 2 * x + y
        chips = _other_chips(x, y, c)
        remote = []
        for i in range(n):
            mine = outs[i].at[me, c]
            for peer, _ in chips:
                remote.append((mine, mine, peer, None))
        for i in range(n):
            for k, (_, src_shard) in enumerate(chips):
                slab = outs[i].at[src_shard, c]
                remote.append((slab, slab, (x, y, 1 - c), 3 * i + k))
        return remote

    shapes = [jax.ShapeDtypeStruct(b.shape, b.dtype) for b in bufs]
    return _exchange(name, bufs, shapes, plan, in_place=True)


def swap_halves(grads, name):
    n = len(grads)
    shapes = [jax.ShapeDtypeStruct((g.shape[0],) + g.shape[2:], g.dtype) for g in grads]

    def plan(ins, outs, count_only):
        if count_only:
            return [None] * (n * N_SHARDS)
        x, y, c = _coords()
        remote = []
        for i in range(n):
            for s in range(N_SHARDS):
                remote.append((ins[i].at[s, 1 - c], outs[i].at[s], (x, y, 1 - c), None))
        return remote

    return _exchange(name, grads, shapes, plan)


def chip_all_to_all(csums, name):
    n = len(csums)
    shapes = [jax.ShapeDtypeStruct((3,) + g.shape[1:], g.dtype) for g in csums]

    def plan(ins, outs, count_only):
        if count_only:
            return [None] * (3 * n)
        x, y, c = _coords()
        remote = []
        for i in range(n):
            for k, (peer, shard) in enumerate(_other_chips(x, y, c)):
                remote.append((ins[i].at[shard], outs[i].at[k], peer, None))
        return remote

    return _exchange(name, csums, shapes, plan)


def join_halves(bufs, name):
    n = len(bufs)

    def plan(ins, outs, count_only):
        if count_only:
            return [None] * n
        x, y, c = _coords()
        return [(outs[i].at[c], outs[i].at[c], (x, y, 1 - c), None) for i in range(n)]

    shapes = [jax.ShapeDtypeStruct(b.shape, b.dtype) for b in bufs]
    return _exchange(name, bufs, shapes, plan, in_place=True)


def gather_full(buf, name):
    def plan(ins, outs, count_only):
        if count_only:
            return [None] * 3
        x, y, c = _coords()
        mine = outs[0].at[2 * x + y]
        return [(mine, mine, peer, None) for peer, _ in _other_chips(x, y, c)]

    return _exchange(name, [buf], [jax.ShapeDtypeStruct(buf.shape, buf.dtype)], plan, in_place=True)[0]


def _pack(arrays, multiple):
    flat = jnp.concatenate([a.reshape(-1) for a in arrays])
    n = flat.shape[0]
    return jnp.pad(flat, (0, _round_up(n, multiple) - n))


def _unpack(flat, shapes):
    out, pos = [], 0
    for shp in shapes:
        n = math.prod(shp)
        out.append(flat[pos:pos + n].reshape(shp))
        pos += n
    return out


def _block_diag(w, per_group):
    nb, bs, _ = w.shape
    g = nb // per_group
    w4 = w.reshape(g, per_group, bs, bs)
    eye = jnp.eye(per_group, dtype=w.dtype)
    full = w4[:, :, :, None, :] * eye[None, :, None, :, None]
    return full.reshape(g, per_group * bs, per_group * bs).astype(BF16)


def _block_diag_extract(full, per_group, bs):
    g = full.shape[0]
    f5 = full.reshape(g, per_group, bs, per_group, bs)
    idx = jnp.arange(per_group)
    picked = f5[:, idx, :, idx, :]
    return jnp.moveaxis(picked, 0, 1).reshape(g * per_group, bs, bs)


def kernel(x, meta, a_w_in, a_conv_w, a_conv_b, a_w_r, a_b_r, a_w_i, a_b_i, a_lambda, a_w_out, kv_w, kv_f_b, b_w_in, b_w_out, f_w_in, f_conv_w, f_conv_b, f_w_out, ln1_g, ln1_b, ln2_g, ln2_b, loss_target, m_meta, m_a_w_in, m_a_conv_w, m_a_conv_b, m_a_w_r, m_a_b_r, m_a_w_i, m_a_b_i, m_a_lambda, m_a_w_out, m_kv_w, m_kv_f_b, m_b_w_in, m_b_w_out, m_f_w_in, m_f_conv_w, m_f_conv_b, m_f_w_out, m_ln1_g, m_ln1_b, m_ln2_g, m_ln2_b, v_meta, v_a_w_in, v_a_conv_w, v_a_conv_b, v_a_w_r, v_a_b_r, v_a_w_i, v_a_b_i, v_a_lambda, v_a_w_out, v_kv_w, v_kv_f_b, v_b_w_in, v_b_w_out, v_f_w_in, v_f_conv_w, v_f_conv_b, v_f_w_out, v_ln1_g, v_ln1_b, v_ln2_g, v_ln2_b):
    weights = dict(meta=meta, a_w_in=a_w_in, a_conv_w=a_conv_w, a_conv_b=a_conv_b, a_w_r=a_w_r, a_b_r=a_b_r,
                   a_w_i=a_w_i, a_b_i=a_b_i, a_lambda=a_lambda, a_w_out=a_w_out, kv_w=kv_w, kv_f_b=kv_f_b,
                   b_w_in=b_w_in, b_w_out=b_w_out, f_w_in=f_w_in, f_conv_w=f_conv_w, f_conv_b=f_conv_b,
                   f_w_out=f_w_out, ln1_g=ln1_g, ln1_b=ln1_b, ln2_g=ln2_g, ln2_b=ln2_b)
    mom_m = dict(meta=m_meta, a_w_in=m_a_w_in, a_conv_w=m_a_conv_w, a_conv_b=m_a_conv_b, a_w_r=m_a_w_r,
                 a_b_r=m_a_b_r, a_w_i=m_a_w_i, a_b_i=m_a_b_i, a_lambda=m_a_lambda, a_w_out=m_a_w_out,
                 kv_w=m_kv_w, kv_f_b=m_kv_f_b, b_w_in=m_b_w_in, b_w_out=m_b_w_out, f_w_in=m_f_w_in,
                 f_conv_w=m_f_conv_w, f_conv_b=m_f_conv_b, f_w_out=m_f_w_out, ln1_g=m_ln1_g, ln1_b=m_ln1_b,
                 ln2_g=m_ln2_g, ln2_b=m_ln2_b)
    mom_v = dict(meta=v_meta, a_w_in=v_a_w_in, a_conv_w=v_a_conv_w, a_conv_b=v_a_conv_b, a_w_r=v_a_w_r,
                 a_b_r=v_a_b_r, a_w_i=v_a_w_i, a_b_i=v_a_b_i, a_lambda=v_a_lambda, a_w_out=v_a_w_out,
                 kv_w=v_kv_w, kv_f_b=v_kv_f_b, b_w_in=v_b_w_in, b_w_out=v_b_w_out, f_w_in=v_f_w_in,
                 f_conv_w=v_f_conv_w, f_conv_b=v_f_conv_b, f_w_out=v_f_w_out, ln1_g=v_ln1_g, ln1_b=v_ln1_b,
                 ln2_g=v_ln2_g, ln2_b=v_ln2_b)
    return _train_step(x, loss_target, weights, mom_m, mom_v)


WEIGHT_ORDER = ("meta", "a_w_in", "a_conv_w", "a_conv_b", "a_w_r", "a_b_r", "a_w_i", "a_b_i", "a_lambda",
                "a_w_out", "kv_w", "kv_f_b", "b_w_in", "b_w_out", "f_w_in", "f_conv_w", "f_conv_b",
                "f_w_out", "ln1_g", "ln1_b", "ln2_g", "ln2_b")
BIG = ("a_w_in", "a_w_out", "kv_w", "b_w_in", "b_w_out", "f_w_in", "f_w_out")
OUT_TYPE = ("a_w_out", "b_w_out", "f_w_out")
SMALL_SHARDED = (("meta", 1), ("a_conv_w", 2), ("a_conv_b", 1), ("a_b_r", 1), ("a_b_i", 1), ("a_lambda", 1),
                 ("f_conv_w", 2))
SMALL_REPLICATED = ("a_w_r", "a_w_i", "kv_f_b", "f_conv_b", "ln1_g", "ln1_b", "ln2_g", "ln2_b")


def _train_step(x, loss_target, weights, mom_m, mom_v):
    S = N_SHARDS
    seq, d = x.shape[1], x.shape[2]
    nm = weights["meta"].shape[0]
    la = weights["a_w_in"].shape[0]
    lb = weights["b_w_in"].shape[0]
    depth = la + lb
    dr = weights["a_w_out"].shape[1] * S
    nb, bs = weights["a_w_r"].shape[1], weights["a_w_r"].shape[2]
    per_group = (LANES // math.gcd(bs, LANES))
    gs = per_group * bs
    heads = weights["kv_f_b"].shape[0]
    dff = weights["f_w_out"].shape[1] * S
    nkv = 2 * d + heads
    nkv_s = weights["kv_w"].shape[1]
    nkvp = _round_up(2 * d + LANES, 768) if 2 * d + LANES > 768 else 2 * d + LANES
    alpha = (2 * depth) ** 0.25
    xi, yi, ci = _coords()
    shard = 2 * xi + yi
    core_arr = jnp.reshape(ci, (1,)).astype(jnp.int32)
    shard_arr = jnp.reshape(shard, (1,)).astype(jnp.int32)
    shard_core_arr = jnp.stack([shard, ci]).astype(jnp.int32)

    big2d = {k: weights[k].reshape(-1, weights[k].shape[-1]) for k in BIG}
    parts = [cast_into_slot(shard_arr, big2d[k], "cast_" + k) for k in BIG]
    small_local = [weights[k] for k, _ in SMALL_SHARDED]
    sm_flat = _pack(small_local, 2 * SUBLANES * LANES).reshape(1, 2, -1, LANES)
    parts.append(lax.dynamic_update_slice_in_dim(lax.empty((S,) + sm_flat.shape[1:], F32), sm_flat, shard, axis=0))
    gathered = gather_shards(parts, "gather_weights")
    gw = {}
    for k, g in zip(BIG, gathered[:-1]):
        w = weights[k]
        lyr = w.shape[0] if w.ndim == 3 else 1
        g = g.reshape(S, lyr, w.shape[-2], w.shape[-1])
        if k in OUT_TYPE:
            g = jnp.moveaxis(g, 0, 1).reshape(1, lyr, S * w.shape[-2], w.shape[-1])
        gw[k] = g
    sm_all = gathered[-1].reshape(S, -1)
    small_full = {}
    per_shard = [_unpack(sm_all[s], [a.shape for a in small_local]) for s in range(S)]
    for idx, (k, axis) in enumerate(SMALL_SHARDED):
        small_full[k] = jnp.concatenate([per_shard[s][idx] for s in range(S)], axis=axis)
    kv_cat = jnp.moveaxis(gw["kv_w"][:, 0], 0, 1).reshape(d, S * nkv_s)
    kv_pad = jnp.pad(kv_cat, ((0, 0), (0, nkvp - nkv))).reshape(1, 1, d, nkvp)
    fb_pad = jnp.pad(weights["kv_f_b"], (0, LANES - heads)).reshape(1, LANES)
    wr_g = [_block_diag(weights["a_w_r"][l], per_group) for l in range(la)]
    wi_g = [_block_diag(weights["a_w_i"][l], per_group) for l in range(la)]
    row = lambda v: v.reshape(1, -1)

    h, hb = embed_fwd(small_full["meta"], x[0], "embed")
    saved = []
    kvz = c_pad = ct_pad = None
    _, tp, _ = _attn_geometry(nm + seq)
    t = nm + seq
    for l in range(depth):
        sv = {"hb_in": hb}
        if l < la:
            gr = mm_in(hb, gw["a_w_in"], l, F32, f"a{l}_in")
            rc, rcb = a_conv_fwd(gr, small_full["a_conv_w"][l], row(small_full["a_conv_b"][l]), f"a{l}_conv")
            r_pre, i_pre = mm_bd(rcb, wr_g[l], wi_g[l], f"a{l}_gates")
            hs, gb = a_elem_fwd(gr, rc, r_pre, i_pre, row(small_full["a_b_r"][l]), row(small_full["a_b_i"][l]),
                                row(small_full["a_lambda"][l]), f"a{l}_lru")
            mix = mm_out(gb, gw["a_w_out"], l, f"a{l}_out")
            sv.update(gr=gr, rc=rc, rcb=rcb, r_pre=r_pre, i_pre=i_pre, hs=hs, gb=gb)
        else:
            j = l - la
            if j == 0:
                kvz = mm_in(hb, kv_pad, 0, F32, "kv_proj")
                cum = kv_fwd(kvz, fb_pad, d, "kv_forget")
                c_pad = jnp.pad(cum, ((0, tp - t), (0, 0)))
                ct_pad = jnp.pad(cum[:, :heads].T, ((0, 0), (0, tp - t))).reshape(heads, 1, tp)
                kv_hb = hb
            qg = mm_in(hb, gw["b_w_in"], j, F32, f"b{j}_in")
            o, mob = attn_fwd(qg, kvz, c_pad, ct_pad, d, heads, f"b{j}_attn")
            mix = mm_out(mob, gw["b_w_out"], j, f"b{j}_out")
            sv.update(qg=qg, o=o, mob=mob)
        h1, h1b, xh1, rs1 = ln_fwd(h, mix, row(weights["ln1_g"][l]), row(weights["ln1_b"][l]), alpha, f"ln1_{l}")
        zf = mm_in(h1b, gw["f_w_in"], l, F32, f"f{l}_in")
        ffb = f_elem_fwd(zf, small_full["f_conv_w"][l], row(weights["f_conv_b"][l]), f"f{l}_act")
        ffo = mm_out(ffb, gw["f_w_out"], l, f"f{l}_out")
        h2, h2b, xh2, rs2 = ln_fwd(h1, ffo, row(weights["ln2_g"][l]), row(weights["ln2_b"][l]), alpha, f"ln2_{l}")
        sv.update(h1b=h1b, xh1=xh1, rs1=rs1, zf=zf, ffb=ffb, xh2=xh2, rs2=rs2)
        saved.append(sv)
        h, hb = h2, h2b
    loss11, dy = loss_fwd_bwd(h, loss_target[0], nm, "loss")

    gbuf = {k: lax.empty(gw[k].shape, BF16) for k in BIG if k != "kv_w"}
    g_small = {}
    per_layer = {k: [None] * n for k, n in (
        ("a_conv_w", la), ("a_conv_b", la), ("a_w_r", la), ("a_b_r", la), ("a_w_i", la), ("a_b_i", la),
        ("a_lambda", la), ("f_conv_w", depth), ("f_conv_b", depth), ("ln1_g", depth), ("ln1_b", depth),
        ("ln2_g", depth), ("ln2_b", depth))}
    adds = [(dy, 1.0)]
    dks, dvs, dcs = [], [], []
    for l in reversed(range(depth)):
        sv = saved[l]
        ds2, ds2b, dg2, db2 = ln_bwd(adds, sv["xh2"], sv["rs2"], row(weights["ln2_g"][l]), f"ln2_{l}_bwd")
        per_layer["ln2_g"][l], per_layer["ln2_b"][l] = dg2[0], db2[0]
        dff_v = mm_out_nt(ds2b, gw["f_w_out"], l, f"f{l}_out_dx")
        gbuf["f_w_out"] = mm_tn(sv["ffb"], ds2b, gbuf["f_w_out"], l, f"f{l}_out_dw")
        dzg, dzv, dwg, dwv, dbg, dbv = f_elem_bwd(sv["zf"], dff_v, small_full["f_conv_w"][l],
                                                  row(weights["f_conv_b"][l]), f"f{l}_act_bwd")
        dzb = jnp.concatenate([dzg, dzv], axis=1)
        per_layer["f_conv_w"][l] = jnp.concatenate([dwg, dwv], axis=1)
        per_layer["f_conv_b"][l] = jnp.concatenate([dbg, dbv], axis=1)[0]
        dh1_f = mm_in_nt(dzb, gw["f_w_in"], l, f"f{l}_in_dx")
        gbuf["f_w_in"] = mm_tn(sv["h1b"], dzb, gbuf["f_w_in"], l, f"f{l}_in_dw")
        ds1, ds1b, dg1, db1 = ln_bwd([(ds2, alpha), (dh1_f, 1.0)], sv["xh1"], sv["rs1"],
                                     row(weights["ln1_g"][l]), f"ln1_{l}_bwd")
        per_layer["ln1_g"][l], per_layer["ln1_b"][l] = dg1[0], db1[0]
        if l < la:
            dgv = mm_out_nt(ds1b, gw["a_w_out"], l, f"a{l}_out_dx")
            gbuf["a_w_out"] = mm_tn(sv["gb"], ds1b, gbuf["a_w_out"], l, f"a{l}_out_dw")
            dgate_b, drp_b, dip_b, drc_d, dlam, dbr, dbi = a_elem_bwd(
                dgv, sv["gr"], sv["rc"], sv["r_pre"], sv["i_pre"], sv["hs"], row(small_full["a_b_r"][l]),
                row(small_full["a_b_i"][l]), row(small_full["a_lambda"][l]), f"a{l}_lru_bwd")
            drc_g = mm_bd_nt(drp_b, dip_b, wr_g[l], wi_g[l], f"a{l}_gates_dx")
            dwr_g, dwi_g = mm_bd_tn(sv["rcb"], drp_b, dip_b, gs, f"a{l}_gates_dw")
            drec_b, dcw, dcb = a_conv_bwd(drc_d, drc_g, sv["gr"], small_full["a_conv_w"][l], f"a{l}_conv_bwd")
            per_layer["a_w_r"][l] = _block_diag_extract(dwr_g, per_group, bs)
            per_layer["a_w_i"][l] = _block_diag_extract(dwi_g, per_group, bs)
            per_layer["a_lambda"][l], per_layer["a_b_r"][l], per_layer["a_b_i"][l] = dlam[0], dbr[0], dbi[0]
            per_layer["a_conv_w"][l], per_layer["a_conv_b"][l] = dcw, dcb[0]
            dgr_b = jnp.concatenate([dgate_b, drec_b], axis=1)
            dh_m = mm_in_nt(dgr_b, gw["a_w_in"], l, f"a{l}_in_dx")
            gbuf["a_w_in"] = mm_tn(sv["hb_in"], dgr_b, gbuf["a_w_in"], l, f"a{l}_in_dw")
        else:
            j = l - la
            dmo = mm_out_nt(ds1b, gw["b_w_out"], j, f"b{j}_out_dx")
            gbuf["b_w_out"] = mm_tn(sv["mob"], ds1b, gbuf["b_w_out"], j, f"b{j}_out_dw")
            dq_b, dog_b, dk, dv, dct = attn_bwd(dmo, sv["qg"], kvz, sv["o"], c_pad, ct_pad, d, heads,
                                                f"b{j}_attn_bwd")
            dks.append(dk)
            dvs.append(dv)
            dcs.append(jnp.pad(dct[:, 0, :t].T, ((0, 0), (0, LANES - heads))))
            dqg_b = jnp.concatenate([dq_b, dog_b], axis=1)
            dh_m = mm_in_nt(dqg_b, gw["b_w_in"], j, f"b{j}_in_dx")
            gbuf["b_w_in"] = mm_tn(sv["hb_in"], dqg_b, gbuf["b_w_in"], j, f"b{j}_in_dw")
        adds = [(ds1, alpha), (dh_m, 1.0)]
        if l == la:
            dzf_b, dfb = kv_bwd(dcs, kvz, fb_pad, d, "kv_forget_bwd")
            dk_b = add_cast(dks[0], dks[1], "kv_dk") if lb == 2 else None
            dv_b = add_cast(dvs[0], dvs[1], "kv_dv") if lb == 2 else None
            dz_kv = jnp.concatenate([dk_b, dv_b, dzf_b, jnp.zeros((t, nkvp - 2 * d - LANES), BF16)], axis=1)
            dh_kv = mm_in_nt(dz_kv, kv_pad, 0, "kv_proj_dx")
            kv_gbuf = mm_tn(kv_hb, dz_kv, lax.empty((1, 1, d, nkvp), BF16), 0, "kv_proj_dw")
            g_small["kv_f_b"] = dfb[0, :heads]
            adds.append((dh_kv, 1.0))
    g_meta, g_x = embed_bwd(adds, nm, "embed_bwd")

    big_grads = []
    for k in BIG:
        if k == "kv_w":
            g4 = jnp.moveaxis(kv_gbuf[0, 0, :, :nkv].reshape(d, S, nkv_s), 1, 0)
        elif k in OUT_TYPE:
            _, lyr, kk, nn = gbuf[k].shape
            g4 = jnp.moveaxis(gbuf[k].reshape(lyr, S, kk // S, nn), 1, 0)
        else:
            g4 = gbuf[k]
        rows_total = math.prod(g4.shape[1:-1])
        big_grads.append(g4.reshape(S, 2, rows_total // 2, g4.shape[-1]))

    small_names = list(SMALL_REPLICATED) + [k for k, _ in SMALL_SHARDED]
    g_small["meta"] = g_meta
    for k, vals in per_layer.items():
        g_small[k] = jnp.stack(vals)
    small_shapes = {k: (weights[k].shape if k in SMALL_REPLICATED else g_small[k].shape) for k in small_names}
    sm_g = _pack([g_small[k].reshape(small_shapes[k]) for k in small_names], S * 2 * SUBLANES * LANES)
    big_grads.append(sm_g.reshape(S, 2, -1, LANES))

    from_sib = swap_halves(big_grads, "grad_swap")
    csums = [add_halves(core_arr, g, o, f"chip_sum_{i}") for i, (g, o) in enumerate(zip(big_grads, from_sib))]
    from_chips = chip_all_to_all(csums, "grad_all_to_all")
    finals = [add_four(shard_core_arr, cs, rv, f"owner_sum_{i}") for i, (cs, rv) in enumerate(zip(csums, from_chips))]
    joined = join_halves(finals, "grad_join")
    sm_slot = lax.dynamic_update_slice_in_dim(lax.empty((S,) + joined[-1].shape, F32), joined[-1][None], shard, axis=0)
    sm_red = gather_full(sm_slot, "small_gather").reshape(-1)

    out_g, out_d, out_m, out_v = {}, {}, {}, {}
    for k, gj in zip(BIG, joined[:-1]):
        w2 = big2d[k]
        g2 = gj.reshape(w2.shape)
        dlt, mn, vn = adamw(w2, g2, mom_m[k].reshape(w2.shape), mom_v[k].reshape(w2.shape), "adamw_" + k)
        shp = weights[k].shape
        out_g[k], out_d[k], out_m[k], out_v[k] = g2.reshape(shp), dlt.reshape(shp), mn.reshape(shp), vn.reshape(shp)
    sm_vals = dict(zip(small_names, _unpack(sm_red, [small_shapes[k] for k in small_names])))
    local_small = {}
    for k in SMALL_REPLICATED:
        local_small[k] = sm_vals[k]
    for k, axis in SMALL_SHARDED:
        size = weights[k].shape[axis]
        local_small[k] = lax.dynamic_slice_in_dim(sm_vals[k], shard * size, size, axis=axis)
    mult = 512 * LANES
    pk = lambda src: _pack([src[k] for k in small_names], mult).reshape(-1, LANES)
    dlt, mn, vn = adamw(pk(weights), pk(local_small), pk(mom_m), pk(mom_v), "adamw_small")
    shapes_local = [weights[k].shape for k in small_names]
    for dst, packed in ((out_d, dlt), (out_m, mn), (out_v, vn)):
        for k, val in zip(small_names, _unpack(packed.reshape(-1), shapes_local)):
            dst[k] = val
    for k in small_names:
        out_g[k] = local_small[k]

    loss = lax.psum(loss11[0, 0], ("x", "y", "c"))
    return (loss, g_x[None], *[out_g[k] for k in WEIGHT_ORDER], *[out_d[k] for k in WEIGHT_ORDER],
            *[out_m[k] for k in WEIGHT_ORDER], *[out_v[k] for k in WEIGHT_ORDER])
```

```python
import functools
import math

import jax
import jax.numpy as jnp
from jax import lax
from jax.experimental import pallas as pl
from jax.experimental.pallas import tpu as pltpu

F32 = jnp.float32
BF16 = jnp.bfloat16

LRU_C = 8.0
LN_EPS = 1e-5
ADAM_LR = 0.001
ADAM_B1 = 0.9
ADAM_B2 = 0.999
ADAM_EPS = 1e-08
ADAM_WD = 0.01
ADAM_STEP = 10

LANES = 128
SUBLANES = 8
V7X_VMEM_BYTES = 64 * 1024 * 1024
VMEM_LIMIT = V7X_VMEM_BYTES * 7 // 8
N_SHARDS = 4
GELU_C0 = math.sqrt(2.0 / math.pi)
GELU_C1 = 0.044715
NEG_BIG = -1e30


def _cp(*sem):
    return pltpu.CompilerParams(dimension_semantics=tuple(sem), vmem_limit_bytes=VMEM_LIMIT)


def _tile(n, cap, mult=LANES):
    best = None
    d = mult
    while d <= min(n, cap):
        if n % d == 0:
            best = d
        d += mult
    return n if best is None else best


def _row_block(t):
    if t % 3 == 0 and (t // 3) % 16 == 0:
        return t // 3
    return t


def _round_up(n, m):
    return (n + m - 1) // m * m


def _sigmoid(v):
    return 1.0 / (1.0 + jnp.exp(-v))


def _softplus(v):
    return jnp.maximum(v, 0.0) + jnp.log(1.0 + jnp.exp(-jnp.abs(v)))


def _gelu_parts(v):
    v2 = v * v
    u = GELU_C0 * (v + GELU_C1 * v * v2)
    t = jnp.tanh(u)
    g = 0.5 * v * (1.0 + t)
    dg = 0.5 * (1.0 + t) + 0.5 * v * (1.0 - t * t) * (GELU_C0 * (1.0 + 3.0 * GELU_C1 * v2))
    return g, dg


def _gelu(v):
    u = GELU_C0 * (v + GELU_C1 * v * v * v)
    return 0.5 * v * (1.0 + jnp.tanh(u))


def _neg_expm1(v):
    series = -v * (1.0 + 0.5 * v * (1.0 + (v / 3.0) * (1.0 + 0.25 * v)))
    return jnp.where(v > -0.05, series, 1.0 - jnp.exp(v))


def _shift_down(v, j):
    if j == 0:
        return v
    rows = lax.broadcasted_iota(jnp.int32, v.shape, 0)
    return jnp.where(rows >= j, pltpu.roll(v, j, 0), 0.0)


def _shift_up(v, j):
    if j == 0:
        return v
    n = v.shape[0]
    rows = lax.broadcasted_iota(jnp.int32, v.shape, 0)
    return jnp.where(rows < n - j, pltpu.roll(v, n - j, 0), 0.0)


def _scan_rows(a_ref, b_ref, out_ref, n_rows, width, reverse):
    n_groups = n_rows // SUBLANES
    rows = lax.broadcasted_iota(jnp.int32, (SUBLANES, width), 0)
    edge = 0 if reverse else SUBLANES - 1

    def body(g, carry):
        grp = (n_groups - 1 - g) if reverse else g
        off = pl.multiple_of(grp * SUBLANES, SUBLANES)
        b = b_ref[pl.ds(off, SUBLANES), :]
        a = None if a_ref is None else a_ref[pl.ds(off, SUBLANES), :]
        for d in (1, 2, 4):
            if reverse:
                keep = rows < SUBLANES - d
                sh = SUBLANES - d
            else:
                keep = rows >= d
                sh = d
            b_s = jnp.where(keep, pltpu.roll(b, sh, 0), 0.0)
            if a is None:
                b = b + b_s
            else:
                a_s = jnp.where(keep, pltpu.roll(a, sh, 0), 1.0)
                b = a * b_s + b
                a = a * a_s
        h = b + carry if a is None else b + a * carry
        out_ref[pl.ds(off, SUBLANES), :] = h
        return jnp.sum(jnp.where(rows == edge, h, 0.0), axis=0, keepdims=True)

    lax.fori_loop(0, n_groups, body, jnp.zeros((1, width), F32), unroll=2)


def mm_in(x, w, out_dtype, name):
    t, k = x.shape
    s_n, _, ns = w.shape
    tn = _tile(ns, 1408)
    nj = ns // tn
    rb = _row_block(t)

    def body(x_ref, w_ref, o_ref):
        o_ref[...] = jnp.dot(x_ref[...], w_ref[...], preferred_element_type=F32).astype(o_ref.dtype)

    return pl.pallas_call(
        body, name=name, grid=(s_n, nj, t // rb),
        in_specs=[pl.BlockSpec((rb, k), lambda s, j, r: (r, 0)),
                  pl.BlockSpec((None, k, tn), lambda s, j, r: (s, 0, j))],
        out_specs=pl.BlockSpec((rb, tn), lambda s, j, r: (r, s * nj + j)),
        out_shape=jax.ShapeDtypeStruct((t, s_n * ns), out_dtype),
        compiler_params=_cp("parallel", "parallel", "parallel"))(x, w)


def mm_out(x, w, name):
    t, k = x.shape
    n = w.shape[1]
    rb = _row_block(t)

    def body(x_ref, w_ref, o_ref):
        o_ref[...] = jnp.dot(x_ref[...], w_ref[...], preferred_element_type=F32)

    return pl.pallas_call(
        body, name=name, grid=(t // rb,),
        in_specs=[pl.BlockSpec((rb, k), lambda r: (r, 0)), pl.BlockSpec((k, n), lambda r: (0, 0))],
        out_specs=pl.BlockSpec((rb, n), lambda r: (r, 0)),
        out_shape=jax.ShapeDtypeStruct((t, n), F32),
        compiler_params=_cp("parallel"))(x, w)


def mm_in_nt(dy, w, name):
    t, _ = dy.shape
    s_n, k, ns = w.shape
    tn = _tile(ns, 1408)
    nj = ns // tn
    rb = _row_block(t)

    def body(dy_ref, w_ref, o_ref):
        @pl.when((pl.program_id(1) == 0) & (pl.program_id(2) == 0))
        def _():
            o_ref[...] = jnp.zeros_like(o_ref)
        o_ref[...] += lax.dot_general(dy_ref[...], w_ref[...], (((1,), (1,)), ((), ())),
                                      preferred_element_type=F32)

    return pl.pallas_call(
        body, name=name, grid=(t // rb, s_n, nj),
        in_specs=[pl.BlockSpec((rb, tn), lambda r, s, j: (r, s * nj + j)),
                  pl.BlockSpec((None, k, tn), lambda r, s, j: (s, 0, j))],
        out_specs=pl.BlockSpec((rb, k), lambda r, s, j: (r, 0)),
        out_shape=jax.ShapeDtypeStruct((t, k), F32),
        compiler_params=_cp("parallel", "arbitrary", "arbitrary"))(dy, w)


def mm_out_nt(dy, w, name):
    t, n = dy.shape
    k = w.shape[0]
    rb = _row_block(t)

    def body(dy_ref, w_ref, o_ref):
        o_ref[...] = lax.dot_general(dy_ref[...], w_ref[...], (((1,), (1,)), ((), ())),
                                     preferred_element_type=F32)

    return pl.pallas_call(
        body, name=name, grid=(t // rb,),
        in_specs=[pl.BlockSpec((rb, n), lambda r: (r, 0)), pl.BlockSpec((k, n), lambda r: (0, 0))],
        out_specs=pl.BlockSpec((rb, k), lambda r: (r, 0)),
        out_shape=jax.ShapeDtypeStruct((t, k), F32),
        compiler_params=_cp("parallel"))(dy, w)


def mm_tn(x, dy, s_n, name):
    t, kb = x.shape
    nb = dy.shape[1] // s_n
    tk = _tile(kb, 1408)
    tn = _tile(nb, 1408)
    nkb, nnb = kb // tk, nb // tn

    def body(x_ref, dy_ref, o_ref):
        o_ref[...] = lax.dot_general(x_ref[...], dy_ref[...], (((0,), (0,)), ((), ())),
                                     preferred_element_type=F32).astype(o_ref.dtype)

    return pl.pallas_call(
        body, name=name, grid=(s_n, nkb, nnb),
        in_specs=[pl.BlockSpec((t, tk), lambda s, a, b: (0, a)),
                  pl.BlockSpec((t, tn), lambda s, a, b: (0, s * nnb + b))],
        out_specs=pl.BlockSpec((None, tk, tn), lambda s, a, b: (s, a, b)),
        out_shape=jax.ShapeDtypeStruct((s_n, kb, nb), BF16),
        compiler_params=_cp("parallel", "parallel", "parallel"))(x, dy)


def mm_bd(x, wr, wi, name):
    t, _ = x.shape
    g_n, gs, _ = wr.shape
    rb = _row_block(t)

    def body(x_ref, wr_ref, wi_ref, r_ref, i_ref):
        xv = x_ref[...]
        r_ref[...] = jnp.dot(xv, wr_ref[...], preferred_element_type=F32)
        i_ref[...] = jnp.dot(xv, wi_ref[...], preferred_element_type=F32)

    blk = pl.BlockSpec((rb, gs), lambda g, r: (r, g))
    wspec = pl.BlockSpec((None, gs, gs), lambda g, r: (g, 0, 0))
    return pl.pallas_call(
        body, name=name, grid=(g_n, t // rb), in_specs=[blk, wspec, wspec], out_specs=[blk, blk],
        out_shape=[jax.ShapeDtypeStruct((t, g_n * gs), F32)] * 2,
        compiler_params=_cp("parallel", "parallel"))(x, wr, wi)


def mm_bd_nt(dr, di, wr, wi, name):
    t, _ = dr.shape
    g_n, gs, _ = wr.shape
    rb = _row_block(t)
    nt = (((1,), (1,)), ((), ()))

    def body(dr_ref, di_ref, wr_ref, wi_ref, o_ref):
        o_ref[...] = (lax.dot_general(dr_ref[...], wr_ref[...], nt, preferred_element_type=F32)
                      + lax.dot_general(di_ref[...], wi_ref[...], nt, preferred_element_type=F32))

    blk = pl.BlockSpec((rb, gs), lambda g, r: (r, g))
    wspec = pl.BlockSpec((None, gs, gs), lambda g, r: (g, 0, 0))
    return pl.pallas_call(
        body, name=name, grid=(g_n, t // rb), in_specs=[blk, blk, wspec, wspec], out_specs=blk,
        out_shape=jax.ShapeDtypeStruct((t, g_n * gs), F32),
        compiler_params=_cp("parallel", "parallel"))(dr, di, wr, wi)


def mm_bd_tn(x, dr, di, gs, name):
    t, w = x.shape
    g_n = w // gs
    tn_dims = (((0,), (0,)), ((), ()))

    def body(x_ref, dr_ref, di_ref, gr_ref, gi_ref):
        xv = x_ref[...]
        gr_ref[...] = lax.dot_general(xv, dr_ref[...], tn_dims, preferred_element_type=F32)
        gi_ref[...] = lax.dot_general(xv, di_ref[...], tn_dims, preferred_element_type=F32)

    blk = pl.BlockSpec((t, gs), lambda g: (0, g))
    ospec = pl.BlockSpec((None, gs, gs), lambda g: (g, 0, 0))
    return pl.pallas_call(
        body, name=name, grid=(g_n,), in_specs=[blk, blk, blk], out_specs=[ospec, ospec],
        out_shape=[jax.ShapeDtypeStruct((g_n, gs, gs), F32)] * 2,
        compiler_params=_cp("parallel"))(x, dr, di)


def embed_fwd(meta, x2d, name):
    nm, d = meta.shape
    seq = x2d.shape[0]
    t = nm + seq
    cb = _tile(d, 256)

    def body(m_ref, x_ref, h_ref, hb_ref):
        h_ref[pl.ds(0, nm), :] = m_ref[...]
        h_ref[pl.ds(nm, seq), :] = x_ref[...]
        hb_ref[pl.ds(0, nm), :] = m_ref[...].astype(BF16)
        hb_ref[pl.ds(nm, seq), :] = x_ref[...].astype(BF16)

    return pl.pallas_call(
        body, name=name, grid=(d // cb,),
        in_specs=[pl.BlockSpec((nm, cb), lambda j: (0, j)), pl.BlockSpec((seq, cb), lambda j: (0, j))],
        out_specs=[pl.BlockSpec((t, cb), lambda j: (0, j))] * 2,
        out_shape=[jax.ShapeDtypeStruct((t, d), F32), jax.ShapeDtypeStruct((t, d), BF16)],
        compiler_params=_cp("parallel"))(meta, x2d)


def embed_bwd(adds, nm, name):
    t, d = adds[0][0].shape
    seq = t - nm
    cb = _tile(d, 256)
    scales = [s for _, s in adds]
    n = len(adds)

    def body(*refs):
        tot = None
        for r, sc in zip(refs[:n], scales):
            term = r[...] if sc == 1.0 else sc * r[...]
            tot = term if tot is None else tot + term
        gm_ref, gx_ref = refs[n], refs[n + 1]
        gm_ref[...] = tot[0:nm]
        gx_ref[...] = tot[nm:t]

    return pl.pallas_call(
        body, name=name, grid=(d // cb,),
        in_specs=[pl.BlockSpec((t, cb), lambda j: (0, j))] * n,
        out_specs=[pl.BlockSpec((nm, cb), lambda j: (0, j)), pl.BlockSpec((seq, cb), lambda j: (0, j))],
        out_shape=[jax.ShapeDtypeStruct((nm, d), F32), jax.ShapeDtypeStruct((seq, d), F32)],
        compiler_params=_cp("parallel"))(*[a for a, _ in adds])


def loss_fwd_bwd(h, tgt, nm, name):
    t, d = h.shape
    seq = t - nm
    cb = _tile(d, 256)
    inv_d = 1.0 / d

    def body(h_ref, t_ref, loss_ref, dy_ref):
        @pl.when(pl.program_id(0) == 0)
        def _():
            loss_ref[...] = jnp.zeros_like(loss_ref)
        err = h_ref[pl.ds(nm, seq), :] - t_ref[...]
        dy_ref[pl.ds(0, nm), :] = jnp.zeros((nm, cb), F32)
        dy_ref[pl.ds(nm, seq), :] = err * inv_d
        loss_ref[...] += (0.5 * inv_d) * jnp.sum(err * err, keepdims=True)

    return pl.pallas_call(
        body, name=name, grid=(d // cb,),
        in_specs=[pl.BlockSpec((t, cb), lambda j: (0, j)), pl.BlockSpec((seq, cb), lambda j: (0, j))],
        out_specs=[pl.BlockSpec((1, 1), lambda j: (0, 0)), pl.BlockSpec((t, cb), lambda j: (0, j))],
        out_shape=[jax.ShapeDtypeStruct((1, 1), F32), jax.ShapeDtypeStruct((t, d), F32)],
        compiler_params=_cp("arbitrary"))(h, tgt)


def ln_fwd(h, mix, g, b, alpha, name):
    t, d = h.shape
    rb = _row_block(t)

    def body(h_ref, m_ref, g_ref, b_ref, y_ref, yb_ref, xh_ref, rs_ref):
        s = alpha * h_ref[...] + m_ref[...]
        mu = jnp.mean(s, axis=-1, keepdims=True)
        c = s - mu
        var = jnp.mean(c * c, axis=-1, keepdims=True)
        rstd = lax.rsqrt(var + LN_EPS)
        xh = c * rstd
        y = xh * g_ref[...] + b_ref[...]
        y_ref[...] = y
        yb_ref[...] = y.astype(BF16)
        xh_ref[...] = xh
        rs_ref[...] = rstd

    row = pl.BlockSpec((rb, d), lambda r: (r, 0))
    vec = pl.BlockSpec((1, d), lambda r: (0, 0))
    return pl.pallas_call(
        body, name=name, grid=(t // rb,), in_specs=[row, row, vec, vec],
        out_specs=[row, row, row, pl.BlockSpec((rb, 1), lambda r: (r, 0))],
        out_shape=[jax.ShapeDtypeStruct((t, d), F32), jax.ShapeDtypeStruct((t, d), BF16),
                   jax.ShapeDtypeStruct((t, d), F32), jax.ShapeDtypeStruct((t, 1), F32)],
        compiler_params=_cp("parallel"))(h, mix, g, b)


def ln_bwd(adds, xhat, rstd, g, name):
    t, d = xhat.shape
    rb = _row_block(t)
    scales = [s for _, s in adds]
    n = len(adds)

    def body(*refs):
        xh_ref, rs_ref, g_ref = refs[n:n + 3]
        ds_ref, dsb_ref, dg_ref, db_ref = refs[n + 3:]
        dy = None
        for r, sc in zip(refs[:n], scales):
            term = r[...] if sc == 1.0 else sc * r[...]
            dy = term if dy is None else dy + term

        @pl.when(pl.program_id(0) == 0)
        def _():
            dg_ref[...] = jnp.zeros_like(dg_ref)
            db_ref[...] = jnp.zeros_like(db_ref)

        xh = xh_ref[...]
        dxh = dy * g_ref[...]
        m1 = jnp.mean(dxh, axis=-1, keepdims=True)
        m2 = jnp.mean(dxh * xh, axis=-1, keepdims=True)
        ds = rs_ref[...] * (dxh - m1 - xh * m2)
        ds_ref[...] = ds
        dsb_ref[...] = ds.astype(BF16)
        dg_ref[...] += jnp.sum(dy * xh, axis=0, keepdims=True)
        db_ref[...] += jnp.sum(dy, axis=0, keepdims=True)

    row = pl.BlockSpec((rb, d), lambda r: (r, 0))
    vec = pl.BlockSpec((1, d), lambda r: (0, 0))
    return pl.pallas_call(
        body, name=name, grid=(t // rb,),
        in_specs=[row] * n + [row, pl.BlockSpec((rb, 1), lambda r: (r, 0)), vec],
        out_specs=[row, row, vec, vec],
        out_shape=[jax.ShapeDtypeStruct((t, d), F32), jax.ShapeDtypeStruct((t, d), BF16),
                   jax.ShapeDtypeStruct((1, d), F32), jax.ShapeDtypeStruct((1, d), F32)],
        compiler_params=_cp("arbitrary"))(*[a for a, _ in adds], xhat, rstd, g)


def _conv_fwd_val(xv, w_ref, b_ref, width):
    y = b_ref[...]
    for j in range(width):
        y = y + _shift_down(xv, j) * w_ref[pl.ds(width - 1 - j, 1), :]
    return y


def _conv_bwd_val(dout, xv, w_ref, width):
    dx = None
    dws = [None] * width
    for j in range(width):
        k = width - 1 - j
        term = _shift_up(dout, j) * w_ref[pl.ds(k, 1), :]
        dx = term if dx is None else dx + term
        dws[k] = jnp.sum(dout * _shift_down(xv, j), axis=0, keepdims=True)
    return dx, dws, jnp.sum(dout, axis=0, keepdims=True)


def a_conv_fwd(gr, cw, cbias, name):
    t, two_dr = gr.shape
    dr = two_dr // 2
    width = cw.shape[0]
    cb = _tile(dr, 256)
    off = dr // cb

    def body(x_ref, w_ref, b_ref, rc_ref, rcb_ref):
        y = _conv_fwd_val(x_ref[...], w_ref, b_ref, width)
        rc_ref[...] = y
        rcb_ref[...] = y.astype(BF16)

    return pl.pallas_call(
        body, name=name, grid=(dr // cb,),
        in_specs=[pl.BlockSpec((t, cb), lambda j: (0, off + j)),
                  pl.BlockSpec((width, cb), lambda j: (0, j)), pl.BlockSpec((1, cb), lambda j: (0, j))],
        out_specs=[pl.BlockSpec((t, cb), lambda j: (0, j))] * 2,
        out_shape=[jax.ShapeDtypeStruct((t, dr), F32), jax.ShapeDtypeStruct((t, dr), BF16)],
        compiler_params=_cp("parallel"))(gr, cw, cbias)


def a_conv_bwd(drc_a, drc_b, gr, cw, name):
    t, two_dr = gr.shape
    dr = two_dr // 2
    width = cw.shape[0]
    cb = _tile(dr, 256)
    off = dr // cb

    def body(da_ref, db_ref, x_ref, w_ref, dx_ref, dw_ref, dbias_ref):
        dout = da_ref[...] + db_ref[...]
        dx, dws, dbias = _conv_bwd_val(dout, x_ref[...], w_ref, width)
        dx_ref[...] = dx.astype(BF16)
        for k in range(width):
            dw_ref[pl.ds(k, 1), :] = dws[k]
        dbias_ref[...] = dbias

    col = pl.BlockSpec((t, cb), lambda j: (0, j))
    return pl.pallas_call(
        body, name=name, grid=(dr // cb,),
        in_specs=[col, col, pl.BlockSpec((t, cb), lambda j: (0, off + j)),
                  pl.BlockSpec((width, cb), lambda j: (0, j))],
        out_specs=[col, pl.BlockSpec((width, cb), lambda j: (0, j)), pl.BlockSpec((1, cb), lambda j: (0, j))],
        out_shape=[jax.ShapeDtypeStruct((t, dr), BF16), jax.ShapeDtypeStruct((width, dr), F32),
                   jax.ShapeDtypeStruct((1, dr), F32)],
        compiler_params=_cp("parallel"))(drc_a, drc_b, gr, cw)


def _lru_gates(r_pre, i_pre, br, bi, lam):
    r = _sigmoid(r_pre + br)
    i = _sigmoid(i_pre + bi)
    sp = _softplus(-lam)
    la = -LRU_C * r * sp
    a = jnp.exp(la)
    m = jnp.sqrt(_neg_expm1(2.0 * la))
    return r, i, sp, la, a, m


def a_elem_fwd(gr, rc, r_pre, i_pre, br, bi, lam, name):
    t, dr = rc.shape
    cb = LANES

    def body(gate_ref, rc_ref, rp_ref, ip_ref, br_ref, bi_ref, lam_ref, hs_ref, g_ref, a_s, u_s):
        rcv = rc_ref[...]
        _, i, _, _, a, m = _lru_gates(rp_ref[...], ip_ref[...], br_ref[...], bi_ref[...], lam_ref[...])
        a_s[...] = a
        u_s[...] = m * (i * rcv)
        _scan_rows(a_s, u_s, hs_ref, t, cb, reverse=False)
        g_ref[...] = (_gelu(gate_ref[...]) * hs_ref[...]).astype(BF16)

    col = pl.BlockSpec((t, cb), lambda j: (0, j))
    vec = pl.BlockSpec((1, cb), lambda j: (0, j))
    return pl.pallas_call(
        body, name=name, grid=(dr // cb,),
        in_specs=[col, col, col, col, vec, vec, vec],
        out_specs=[col, col],
        out_shape=[jax.ShapeDtypeStruct((t, dr), F32), jax.ShapeDtypeStruct((t, dr), BF16)],
        scratch_shapes=[pltpu.VMEM((t, cb), F32), pltpu.VMEM((t, cb), F32)],
        compiler_params=_cp("parallel"))(gr, rc, r_pre, i_pre, br, bi, lam)


def a_elem_bwd(dg, gr, rc, r_pre, i_pre, hs, br, bi, lam, name):
    t, dr = rc.shape
    cb = LANES

    def body(dg_ref, gate_ref, rc_ref, rp_ref, ip_ref, hs_ref, br_ref, bi_ref, lam_ref,
             dgate_ref, dr_ref, di_ref, drc_ref, dlam_ref, dbr_ref, dbi_ref, a_s, b_s, g_s):
        lamv = lam_ref[...]
        rcv = rc_ref[...]
        hsv = hs_ref[...]
        dgv = dg_ref[...]
        r, i, sp, _, a, m = _lru_gates(rp_ref[...], ip_ref[...], br_ref[...], bi_ref[...], lamv)
        ge, dge = _gelu_parts(gate_ref[...])
        dgate_ref[...] = (dgv * hsv * dge).astype(BF16)
        a_s[...] = _shift_up(a, 1)
        b_s[...] = dgv * ge
        _scan_rows(a_s, b_s, g_s, t, cb, reverse=True)
        gsum = g_s[...]
        da = gsum * _shift_down(hsv, 1)
        dm = gsum * (i * rcv)
        d_i = gsum * m * rcv
        drc_ref[...] = gsum * m * i
        dla = a * da - dm * (a * a) / m
        d_r = (-LRU_C) * sp * dla
        dsp = jnp.sum((-LRU_C) * r * dla, axis=0, keepdims=True)
        dlam_ref[...] = -dsp * _sigmoid(-lamv)
        d_rp = d_r * r * (1.0 - r)
        d_ip = d_i * i * (1.0 - i)
        dr_ref[...] = d_rp.astype(BF16)
        di_ref[...] = d_ip.astype(BF16)
        dbr_ref[...] = jnp.sum(d_rp, axis=0, keepdims=True)
        dbi_ref[...] = jnp.sum(d_ip, axis=0, keepdims=True)

    col = pl.BlockSpec((t, cb), lambda j: (0, j))
    vec = pl.BlockSpec((1, cb), lambda j: (0, j))
    big_b = jax.ShapeDtypeStruct((t, dr), BF16)
    vec_s = jax.ShapeDtypeStruct((1, dr), F32)
    return pl.pallas_call(
        body, name=name, grid=(dr // cb,),
        in_specs=[col, col, col, col, col, col, vec, vec, vec],
        out_specs=[col, col, col, col, vec, vec, vec],
        out_shape=[big_b, big_b, big_b, jax.ShapeDtypeStruct((t, dr), F32), vec_s, vec_s, vec_s],
        scratch_shapes=[pltpu.VMEM((t, cb), F32)] * 3,
        compiler_params=_cp("parallel"))(dg, gr, rc, r_pre, i_pre, hs, br, bi, lam)


def f_elem_fwd(z, cw, cbias, name):
    t, two_f = z.shape
    dff = two_f // 2
    width = cw.shape[0]
    cb = _tile(dff, 256)
    off = dff // cb

    def body(zg_ref, zv_ref, wg_ref, wv_ref, bg_ref, bv_ref, o_ref):
        zcg = _conv_fwd_val(zg_ref[...], wg_ref, bg_ref, width)
        zcv = _conv_fwd_val(zv_ref[...], wv_ref, bv_ref, width)
        o_ref[...] = (_gelu(zcg) * zcv).astype(BF16)

    lo = lambda j: (0, j)
    hi = lambda j: (0, off + j)
    return pl.pallas_call(
        body, name=name, grid=(dff // cb,),
        in_specs=[pl.BlockSpec((t, cb), lo), pl.BlockSpec((t, cb), hi),
                  pl.BlockSpec((width, cb), lo), pl.BlockSpec((width, cb), hi),
                  pl.BlockSpec((1, cb), lo), pl.BlockSpec((1, cb), hi)],
        out_specs=pl.BlockSpec((t, cb), lo),
        out_shape=jax.ShapeDtypeStruct((t, dff), BF16),
        compiler_params=_cp("parallel"))(z, z, cw, cw, cbias, cbias)


def f_elem_bwd(z, dff_g, cw, cbias, name):
    t, two_f = z.shape
    dff = two_f // 2
    width = cw.shape[0]
    cb = _tile(dff, 256)
    off = dff // cb

    def body(zg_ref, zv_ref, d_ref, wg_ref, wv_ref, bg_ref, bv_ref,
             dzg_ref, dzv_ref, dwg_ref, dwv_ref, dbg_ref, dbv_ref):
        zg = zg_ref[...]
        zv = zv_ref[...]
        zcg = _conv_fwd_val(zg, wg_ref, bg_ref, width)
        zcv = _conv_fwd_val(zv, wv_ref, bv_ref, width)
        ge, dge = _gelu_parts(zcg)
        dv = d_ref[...]
        dx, dws, dbias = _conv_bwd_val(dv * zcv * dge, zg, wg_ref, width)
        dzg_ref[...] = dx.astype(BF16)
        for k in range(width):
            dwg_ref[pl.ds(k, 1), :] = dws[k]
        dbg_ref[...] = dbias
        dx, dws, dbias = _conv_bwd_val(dv * ge, zv, wv_ref, width)
        dzv_ref[...] = dx.astype(BF16)
        for k in range(width):
            dwv_ref[pl.ds(k, 1), :] = dws[k]
        dbv_ref[...] = dbias

    lo = lambda j: (0, j)
    hi = lambda j: (0, off + j)
    col = pl.BlockSpec((t, cb), lo)
    wsp = pl.BlockSpec((width, cb), lo)
    vsp = pl.BlockSpec((1, cb), lo)
    return pl.pallas_call(
        body, name=name, grid=(dff // cb,),
        in_specs=[col, pl.BlockSpec((t, cb), hi), col, wsp, pl.BlockSpec((width, cb), hi),
                  vsp, pl.BlockSpec((1, cb), hi)],
        out_specs=[col, col, wsp, wsp, vsp, vsp],
        out_shape=[jax.ShapeDtypeStruct((t, dff), BF16)] * 2
        + [jax.ShapeDtypeStruct((width, dff), F32)] * 2 + [jax.ShapeDtypeStruct((1, dff), F32)] * 2,
        compiler_params=_cp("parallel"))(z, z, dff_g, cw, cw, cbias, cbias)


def kv_fwd(z, fb, d_model, name):
    t, _ = z.shape
    blk = 2 * d_model // LANES

    def body(z_ref, fb_ref, c_ref, lf_s):
        v = z_ref[...] + fb_ref[...]
        lf_s[...] = -_softplus(-v)
        _scan_rows(None, lf_s, c_ref, t, LANES, reverse=False)

    return pl.pallas_call(
        body, name=name, grid=(1,),
        in_specs=[pl.BlockSpec((t, LANES), lambda j: (0, blk)), pl.BlockSpec((1, LANES), lambda j: (0, 0))],
        out_specs=pl.BlockSpec((t, LANES), lambda j: (0, 0)),
        out_shape=jax.ShapeDtypeStruct((t, LANES), F32),
        scratch_shapes=[pltpu.VMEM((t, LANES), F32)],
        compiler_params=_cp("arbitrary"))(z, fb)


def kv_bwd(dcs, z, fb, d_model, name):
    t, _ = z.shape
    blk = 2 * d_model // LANES
    n = len(dcs)

    def body(*refs):
        z_ref, fb_ref, dz_ref, dfb_ref, dc_s, dl_s = refs[n:]
        tot = refs[0][...]
        for r in refs[1:n]:
            tot = tot + r[...]
        dc_s[...] = tot
        _scan_rows(None, dc_s, dl_s, t, LANES, reverse=True)
        v = z_ref[...] + fb_ref[...]
        dz = dl_s[...] * _sigmoid(-v)
        dz_ref[...] = dz.astype(BF16)
        dfb_ref[...] = jnp.sum(dz, axis=0, keepdims=True)

    full = pl.BlockSpec((t, LANES), lambda j: (0, 0))
    return pl.pallas_call(
        body, name=name, grid=(1,),
        in_specs=[full] * n + [pl.BlockSpec((t, LANES), lambda j: (0, blk)),
                               pl.BlockSpec((1, LANES), lambda j: (0, 0))],
        out_specs=[full, pl.BlockSpec((1, LANES), lambda j: (0, 0))],
        out_shape=[jax.ShapeDtypeStruct((t, LANES), BF16), jax.ShapeDtypeStruct((1, LANES), F32)],
        scratch_shapes=[pltpu.VMEM((t, LANES), F32)] * 2,
        compiler_params=_cp("arbitrary"))(*dcs, z, fb)


def add_cast(a, b, name):
    t, d = a.shape
    cb = _tile(d, 512)

    def body(a_ref, b_ref, o_ref):
        o_ref[...] = (a_ref[...] + b_ref[...]).astype(BF16)

    col = pl.BlockSpec((t, cb), lambda j: (0, j))
    return pl.pallas_call(body, name=name, grid=(d // cb,), in_specs=[col, col], out_specs=col,
                          out_shape=jax.ShapeDtypeStruct((t, d), BF16),
                          compiler_params=_cp("parallel"))(a, b)


def _attn_geometry(t):
    nqb = 6 if t > 1024 else 2
    tp = _round_up(t, LANES * nqb)
    return nqb, tp, tp // nqb


def _attn_scales(dh):
    scale = dh ** -0.5
    if math.log2(scale).is_integer():
        return scale, 1.0
    return 1.0, scale


def _attn_pieces(qs, ks, crow, j, i, tq, dh, s_mul):
    r0 = i * tq
    lanes = pl.ds(j * dh, dh)
    qi = qs[pl.ds(r0, tq), lanes]
    spans = ([(0, r0)] if i > 0 else []) + [(r0, tq)]
    logits = []
    for k0, n in spans:
        s = lax.dot_general(qi, ks[pl.ds(k0, n), lanes], (((1,), (1,)), ((), ())),
                            preferred_element_type=F32)
        if s_mul != 1.0:
            s = s * s_mul
        s = s - crow[:, k0:k0 + n]
        if k0 == r0:
            rows = lax.broadcasted_iota(jnp.int32, (tq, tq), 0)
            cols = lax.broadcasted_iota(jnp.int32, (tq, tq), 1)
            s = jnp.where(cols <= rows, s, NEG_BIG)
        logits.append(s)
    mx = jnp.max(logits[0], axis=1, keepdims=True)
    for s in logits[1:]:
        mx = jnp.maximum(mx, jnp.max(s, axis=1, keepdims=True))
    es = [jnp.exp(s - mx) for s in logits]
    tot = jnp.sum(es[0], axis=1, keepdims=True)
    for e in es[1:]:
        tot = tot + jnp.sum(e, axis=1, keepdims=True)
    inv = 1.0 / tot
    return [(k0, n, e * inv) for (k0, n), e in zip(spans, es)], qi


def attn_fwd(qg, z, ct_pad, d_model, n_heads, name):
    t = qg.shape[0]
    dh = d_model // n_heads
    hp = LANES // dh
    nqb, tp, tq = _attn_geometry(t)
    nblk = d_model // LANES
    q_mul, s_mul = _attn_scales(dh)

    def body(q_ref, og_ref, k_ref, v_ref, ct_ref, o_ref, mo_ref, qs, ks, vs, os_):
        pad = jnp.zeros((tp - t, LANES), BF16)
        qs[pl.ds(0, t), :] = (q_ref[...] * q_mul).astype(BF16)
        qs[pl.ds(t, tp - t), :] = pad
        for src, dst in ((k_ref, ks), (v_ref, vs)):
            dst[pl.ds(0, t), :] = src[...].astype(BF16)
            dst[pl.ds(t, tp - t), :] = pad
        for j in range(hp):
            crow = ct_ref[j]
            lanes = pl.ds(j * dh, dh)
            for i in range(nqb):
                pieces, _ = _attn_pieces(qs, ks, crow, j, i, tq, dh, s_mul)
                acc = None
                for k0, n, p in pieces:
                    part = jnp.dot(p.astype(BF16), vs[pl.ds(k0, n), lanes], preferred_element_type=F32)
                    acc = part if acc is None else acc + part
                os_[pl.ds(i * tq, tq), lanes] = acc
        o = os_[pl.ds(0, t), :]
        o_ref[...] = o
        mo_ref[...] = (o * _sigmoid(og_ref[...])).astype(BF16)

    col = lambda off: pl.BlockSpec((t, LANES), lambda p: (0, off + p))
    return pl.pallas_call(
        body, name=name, grid=(nblk,),
        in_specs=[col(0), col(nblk), col(0), col(nblk), pl.BlockSpec((hp, 1, tp), lambda p: (p, 0, 0))],
        out_specs=[col(0), col(0)],
        out_shape=[jax.ShapeDtypeStruct((t, d_model), F32), jax.ShapeDtypeStruct((t, d_model), BF16)],
        scratch_shapes=[pltpu.VMEM((tp, LANES), BF16)] * 3 + [pltpu.VMEM((tp, LANES), F32)],
        compiler_params=_cp("parallel"))(qg, qg, z, z, ct_pad)


def attn_bwd(dmo, qg, z, o, ct_pad, d_model, n_heads, name):
    t = qg.shape[0]
    dh = d_model // n_heads
    hp = LANES // dh
    nqb, tp, tq = _attn_geometry(t)
    nblk = d_model // LANES
    q_mul, s_mul = _attn_scales(dh)
    scale = dh ** -0.5
    tn_dims = (((0,), (0,)), ((), ()))
    nt_dims = (((1,), (1,)), ((), ()))

    def body(dmo_ref, q_ref, og_ref, k_ref, v_ref, o_ref, ct_ref,
             dq_ref, dog_ref, dk_ref, dv_ref, dct_ref, qs, ks, vs, dos, dqs, dks, dvs):
        pad = jnp.zeros((tp - t, LANES), BF16)
        sg = _sigmoid(og_ref[...])
        dmo_v = dmo_ref[...]
        dog_ref[...] = (dmo_v * o_ref[...] * sg * (1.0 - sg)).astype(BF16)
        dos[pl.ds(0, t), :] = (dmo_v * sg).astype(BF16)
        dos[pl.ds(t, tp - t), :] = pad
        qs[pl.ds(0, t), :] = (q_ref[...] * q_mul).astype(BF16)
        qs[pl.ds(t, tp - t), :] = pad
        for src, dst in ((k_ref, ks), (v_ref, vs)):
            dst[pl.ds(0, t), :] = src[...].astype(BF16)
            dst[pl.ds(t, tp - t), :] = pad
        dks[...] = jnp.zeros_like(dks)
        dvs[...] = jnp.zeros_like(dvs)
        dct_ref[...] = jnp.zeros_like(dct_ref)
        for j in range(hp):
            crow = ct_ref[j]
            lanes = pl.ds(j * dh, dh)
            for i in range(nqb):
                pieces, qi = _attn_pieces(qs, ks, crow, j, i, tq, dh, s_mul)
                do_i = dos[pl.ds(i * tq, tq), lanes]
                dps = [lax.dot_general(do_i, vs[pl.ds(k0, n), lanes], nt_dims, preferred_element_type=F32)
                       for k0, n, _ in pieces]
                row = None
                for (_, _, p), dp in zip(pieces, dps):
                    part = jnp.sum(p * dp, axis=1, keepdims=True)
                    row = part if row is None else row + part
                dq_i = None
                for (k0, n, p), dp in zip(pieces, dps):
                    ds = p * (dp - row)
                    ds_b = ds.astype(BF16)
                    keys = pl.ds(k0, n)
                    part = jnp.dot(ds_b, ks[keys, lanes], preferred_element_type=F32)
                    dq_i = part if dq_i is None else dq_i + part
                    dks[keys, lanes] += lax.dot_general(ds_b, qi, tn_dims, preferred_element_type=F32) * s_mul
                    dvs[keys, lanes] += lax.dot_general(p.astype(BF16), do_i, tn_dims,
                                                        preferred_element_type=F32)
                    dct_ref[j, :, keys] -= jnp.sum(ds, axis=0, keepdims=True)
                dqs[pl.ds(i * tq, tq), lanes] = dq_i * scale
        dq_ref[...] = dqs[pl.ds(0, t), :].astype(BF16)
        dk_ref[...] = dks[pl.ds(0, t), :]
        dv_ref[...] = dvs[pl.ds(0, t), :]

    col = lambda off: pl.BlockSpec((t, LANES), lambda p: (0, off + p))
    big = lambda dt: jax.ShapeDtypeStruct((t, d_model), dt)
    return pl.pallas_call(
        body, name=name, grid=(nblk,),
        in_specs=[col(0), col(0), col(nblk), col(0), col(nblk), col(0),
                  pl.BlockSpec((hp, 1, tp), lambda p: (p, 0, 0))],
        out_specs=[col(0), col(0), col(0), col(0), pl.BlockSpec((hp, 1, tp), lambda p: (p, 0, 0))],
        out_shape=[big(BF16), big(BF16), big(F32), big(F32),
                   jax.ShapeDtypeStruct((n_heads, 1, tp), F32)],
        scratch_shapes=[pltpu.VMEM((tp, LANES), BF16)] * 4 + [pltpu.VMEM((tp, LANES), F32)] * 3,
        compiler_params=_cp("parallel"))(dmo, qg, qg, z, z, o, ct_pad)


def cast_into_slot(shard, w2d, name):
    r, c = w2d.shape
    rh = r // 2
    tr = _tile(rh, 512, 16)
    n = rh // tr

    def body(sh_ref, w_ref, o_ref):
        del sh_ref
        o_ref[...] = w_ref[...].astype(BF16)

    return pl.pallas_call(
        body, name=name,
        grid_spec=pltpu.PrefetchScalarGridSpec(
            num_scalar_prefetch=1, grid=(2, n),
            in_specs=[pl.BlockSpec((tr, c), lambda h, i, sh: (h * n + i, 0))],
            out_specs=pl.BlockSpec((None, None, tr, c), lambda h, i, sh: (sh[0], h, i, 0))),
        out_shape=jax.ShapeDtypeStruct((N_SHARDS, 2, rh, c), BF16),
        compiler_params=_cp("parallel", "parallel"))(shard, w2d)


def add_halves(core, g, other, name):
    s_n, _, rh, c = g.shape
    tr = _tile(rh, 512, 16)

    def body(core_ref, g_ref, o_ref, out_ref):
        del core_ref
        out_ref[...] = (g_ref[...].astype(F32) + o_ref[...].astype(F32)).astype(out_ref.dtype)

    return pl.pallas_call(
        body, name=name,
        grid_spec=pltpu.PrefetchScalarGridSpec(
            num_scalar_prefetch=1, grid=(s_n, rh // tr),
            in_specs=[pl.BlockSpec((None, None, tr, c), lambda s, i, cr: (s, cr[0], i, 0)),
                      pl.BlockSpec((None, tr, c), lambda s, i, cr: (s, i, 0))],
            out_specs=pl.BlockSpec((None, tr, c), lambda s, i, cr: (s, i, 0))),
        out_shape=jax.ShapeDtypeStruct((s_n, rh, c), g.dtype),
        compiler_params=_cp("parallel", "parallel"))(core, g, other)


def add_four(shard_core, csum, recv, buf, layer, name):
    _, rh, c = csum.shape
    tr = _tile(rh, 512, 16)

    def body(sc_ref, a_ref, r_ref, buf_ref, out_ref):
        del sc_ref, buf_ref
        acc = a_ref[...].astype(F32)
        for k in range(3):
            acc = acc + r_ref[k].astype(F32)
        out_ref[...] = acc

    return pl.pallas_call(
        body, name=name,
        grid_spec=pltpu.PrefetchScalarGridSpec(
            num_scalar_prefetch=1, grid=(rh // tr,),
            in_specs=[pl.BlockSpec((None, tr, c), lambda i, sc: (sc[0], i, 0)),
                      pl.BlockSpec((3, tr, c), lambda i, sc: (0, i, 0)),
                      pl.BlockSpec(memory_space=pl.ANY)],
            out_specs=pl.BlockSpec((None, None, tr, c), lambda i, sc: (layer, sc[1], i, 0))),
        out_shape=jax.ShapeDtypeStruct(buf.shape, F32),
        input_output_aliases={3: 0},
        compiler_params=_cp("parallel"))(shard_core, csum, recv, buf)


def adamw(w, g, m, v, name):
    r, c = w.shape
    tr = _tile(r, 512, SUBLANES)
    c1 = 1.0 - ADAM_B1 ** ADAM_STEP
    c2 = 1.0 - ADAM_B2 ** ADAM_STEP

    def body(w_ref, g_ref, m_ref, v_ref, d_ref, mo_ref, vo_ref):
        gv = g_ref[...]
        mn = ADAM_B1 * m_ref[...] + (1.0 - ADAM_B1) * gv
        vn = ADAM_B2 * v_ref[...] + (1.0 - ADAM_B2) * (gv * gv)
        m_hat = mn / c1
        v_hat = vn / c2
        d_ref[...] = -ADAM_LR * (m_hat / (jnp.sqrt(v_hat) + ADAM_EPS) + ADAM_WD * w_ref[...])
        mo_ref[...] = mn
        vo_ref[...] = vn

    blk = pl.BlockSpec((tr, c), lambda i: (i, 0))
    return pl.pallas_call(
        body, name=name, grid=(r // tr,), in_specs=[blk] * 4, out_specs=[blk] * 3,
        out_shape=[jax.ShapeDtypeStruct((r, c), F32)] * 3,
        compiler_params=_cp("parallel"))(w, g, m, v)


def _coords():
    return lax.axis_index("x"), lax.axis_index("y"), lax.axis_index("c")


def _exchange(name, ins, out_shapes, plan, in_place=False):
    n_in = len(ins)
    n_out = len(out_shapes)
    n_rem = len(plan([None] * n_in, [None] * n_out, True))

    def body(*refs):
        in_refs = refs[:n_in]
        out_refs = refs[n_in:n_in + n_out]
        send_sems, recv_sems = refs[n_in + n_out:]
        remote = plan(list(in_refs), list(out_refs), False)
        copies = [pltpu.make_async_remote_copy(
            src_ref=src, dst_ref=dst, send_sem=send_sems.at[q], recv_sem=recv_sems.at[q],
            device_id=peer, device_id_type=pl.DeviceIdType.MESH)
            for q, (src, dst, peer, _) in enumerate(remote)]
        waited = set()
        for q, (_, _, _, after) in enumerate(remote):
            if after is not None and after not in waited:
                copies[after].wait_recv()
                waited.add(after)
            copies[q].start()
        for q, cp in enumerate(copies):
            if q not in waited:
                cp.wait_recv()
        for cp in copies:
            cp.wait_send()

    hbm = pl.BlockSpec(memory_space=pl.ANY)
    return pl.pallas_call(
        body, name=name, in_specs=[hbm] * n_in, out_specs=[hbm] * n_out, out_shape=out_shapes,
        input_output_aliases={i: i for i in range(n_in)} if in_place else {},
        scratch_shapes=[pltpu.SemaphoreType.DMA((n_rem,)), pltpu.SemaphoreType.DMA((n_rem,))],
        compiler_params=pltpu.CompilerParams(has_side_effects=True))(*ins)


def _other_chips(x, y, c):
    return [((1 - x, y, c), 2 * (1 - x) + y), ((x, 1 - y, c), 2 * x + 1 - y),
            ((1 - x, 1 - y, c), 2 * (1 - x) + 1 - y)]


def gather_shards(bufs, name):
    n = len(bufs)

    def plan(ins, outs, count_only):
        if count_only:
            return [None] * (6 * n)
        x, y, c = _coords()
        me = 2 * x + y
        chips = _other_chips(x, y, c)
        remote = []
        for i in range(n):
            mine = outs[i].at[me, c]
            for peer, _ in chips:
                remote.append((mine, mine, peer, None))
        for i in range(n):
            for k, (_, src_shard) in enumerate(chips):
                slab = outs[i].at[src_shard, c]
                remote.append((slab, slab, (x, y, 1 - c), 3 * i + k))
        return remote

    shapes = [jax.ShapeDtypeStruct(b.shape, b.dtype) for b in bufs]
    return _exchange(name, bufs, shapes, plan, in_place=True)


def swap_halves(grads, name):
    n = len(grads)
    shapes = [jax.ShapeDtypeStruct((g.shape[0],) + g.shape[2:], g.dtype) for g in grads]

    def plan(ins, outs, count_only):
        if count_only:
            return [None] * (n * N_SHARDS)
        x, y, c = _coords()
        remote = []
        for i in range(n):
            for s in range(N_SHARDS):
                remote.append((ins[i].at[s, 1 - c], outs[i].at[s], (x, y, 1 - c), None))
        return remote

    return _exchange(name, grads, shapes, plan)


def chip_all_to_all(csums, name):
    n = len(csums)
    shapes = [jax.ShapeDtypeStruct((3,) + g.shape[1:], g.dtype) for g in csums]

    def plan(ins, outs, count_only):
        if count_only:
            return [None] * (3 * n)
        x, y, c = _coords()
        remote = []
        for i in range(n):
            for k, (peer, shard) in enumerate(_other_chips(x, y, c)):
                remote.append((ins[i].at[shard], outs[i].at[k], peer, None))
        return remote

    return _exchange(name, csums, shapes, plan)


def join_halves(bufs, name):
    slots = [(i, l) for i, b in enumerate(bufs) for l in range(b.shape[0])]

    def plan(ins, outs, count_only):
        if count_only:
            return [None] * len(slots)
        x, y, c = _coords()
        return [(outs[i].at[l, c], outs[i].at[l, c], (x, y, 1 - c), None) for i, l in slots]

    shapes = [jax.ShapeDtypeStruct(b.shape, b.dtype) for b in bufs]
    return _exchange(name, bufs, shapes, plan, in_place=True)


def gather_full(buf, name):
    def plan(ins, outs, count_only):
        if count_only:
            return [None] * 3
        x, y, c = _coords()
        mine = outs[0].at[2 * x + y]
        return [(mine, mine, peer, None) for peer, _ in _other_chips(x, y, c)]

    return _exchange(name, [buf], [jax.ShapeDtypeStruct(buf.shape, buf.dtype)], plan, in_place=True)[0]


def _pack(arrays, multiple):
    flat = jnp.concatenate([a.reshape(-1) for a in arrays])
    n = flat.shape[0]
    return jnp.pad(flat, (0, _round_up(n, multiple) - n))


def _unpack(flat, shapes):
    out, pos = [], 0
    for shp in shapes:
        n = math.prod(shp)
        out.append(flat[pos:pos + n].reshape(shp))
        pos += n
    return out


def _block_diag(w, per_group):
    nb, bs, _ = w.shape
    g = nb // per_group
    w4 = w.reshape(g, per_group, bs, bs)
    eye = jnp.eye(per_group, dtype=w.dtype)
    full = w4[:, :, :, None, :] * eye[None, :, None, :, None]
    return full.reshape(g, per_group * bs, per_group * bs).astype(BF16)


def _block_diag_extract(full, per_group, bs):
    g = full.shape[0]
    f5 = full.reshape(g, per_group, bs, per_group, bs)
    idx = jnp.arange(per_group)
    picked = f5[:, idx, :, idx, :]
    return jnp.moveaxis(picked, 0, 1).reshape(g * per_group, bs, bs)


def kernel(x, meta, a_w_in, a_conv_w, a_conv_b, a_w_r, a_b_r, a_w_i, a_b_i, a_lambda, a_w_out, kv_w, kv_f_b, b_w_in, b_w_out, f_w_in, f_conv_w, f_conv_b, f_w_out, ln1_g, ln1_b, ln2_g, ln2_b, loss_target, m_meta, m_a_w_in, m_a_conv_w, m_a_conv_b, m_a_w_r, m_a_b_r, m_a_w_i, m_a_b_i, m_a_lambda, m_a_w_out, m_kv_w, m_kv_f_b, m_b_w_in, m_b_w_out, m_f_w_in, m_f_conv_w, m_f_conv_b, m_f_w_out, m_ln1_g, m_ln1_b, m_ln2_g, m_ln2_b, v_meta, v_a_w_in, v_a_conv_w, v_a_conv_b, v_a_w_r, v_a_b_r, v_a_w_i, v_a_b_i, v_a_lambda, v_a_w_out, v_kv_w, v_kv_f_b, v_b_w_in, v_b_w_out, v_f_w_in, v_f_conv_w, v_f_conv_b, v_f_w_out, v_ln1_g, v_ln1_b, v_ln2_g, v_ln2_b):
    weights = dict(meta=meta, a_w_in=a_w_in, a_conv_w=a_conv_w, a_conv_b=a_conv_b, a_w_r=a_w_r, a_b_r=a_b_r,
                   a_w_i=a_w_i, a_b_i=a_b_i, a_lambda=a_lambda, a_w_out=a_w_out, kv_w=kv_w, kv_f_b=kv_f_b,
                   b_w_in=b_w_in, b_w_out=b_w_out, f_w_in=f_w_in, f_conv_w=f_conv_w, f_conv_b=f_conv_b,
                   f_w_out=f_w_out, ln1_g=ln1_g, ln1_b=ln1_b, ln2_g=ln2_g, ln2_b=ln2_b)
    mom_m = dict(meta=m_meta, a_w_in=m_a_w_in, a_conv_w=m_a_conv_w, a_conv_b=m_a_conv_b, a_w_r=m_a_w_r,
                 a_b_r=m_a_b_r, a_w_i=m_a_w_i, a_b_i=m_a_b_i, a_lambda=m_a_lambda, a_w_out=m_a_w_out,
                 kv_w=m_kv_w, kv_f_b=m_kv_f_b, b_w_in=m_b_w_in, b_w_out=m_b_w_out, f_w_in=m_f_w_in,
                 f_conv_w=m_f_conv_w, f_conv_b=m_f_conv_b, f_w_out=m_f_w_out, ln1_g=m_ln1_g, ln1_b=m_ln1_b,
                 ln2_g=m_ln2_g, ln2_b=m_ln2_b)
    mom_v = dict(meta=v_meta, a_w_in=v_a_w_in, a_conv_w=v_a_conv_w, a_conv_b=v_a_conv_b, a_w_r=v_a_w_r,
                 a_b_r=v_a_b_r, a_w_i=v_a_w_i, a_b_i=v_a_b_i, a_lambda=v_a_lambda, a_w_out=v_a_w_out,
                 kv_w=v_kv_w, kv_f_b=v_kv_f_b, b_w_in=v_b_w_in, b_w_out=v_b_w_out, f_w_in=v_f_w_in,
                 f_conv_w=v_f_conv_w, f_conv_b=v_f_conv_b, f_w_out=v_f_w_out, ln1_g=v_ln1_g, ln1_b=v_ln1_b,
                 ln2_g=v_ln2_g, ln2_b=v_ln2_b)
    return _train_step(x, loss_target, weights, mom_m, mom_v)


WEIGHT_ORDER = ("meta", "a_w_in", "a_conv_w", "a_conv_b", "a_w_r", "a_b_r", "a_w_i", "a_b_i", "a_lambda",
                "a_w_out", "kv_w", "kv_f_b", "b_w_in", "b_w_out", "f_w_in", "f_conv_w", "f_conv_b",
                "f_w_out", "ln1_g", "ln1_b", "ln2_g", "ln2_b")
BIG = ("a_w_in", "a_w_out", "kv_w", "b_w_in", "b_w_out", "f_w_in", "f_w_out")
OUT_TYPE = ("a_w_out", "b_w_out", "f_w_out")
SMALL_SHARDED = (("meta", 1), ("a_conv_w", 2), ("a_conv_b", 1), ("a_b_r", 1), ("a_b_i", 1), ("a_lambda", 1),
                 ("f_conv_w", 2))
SMALL_REPLICATED = ("a_w_r", "a_w_i", "kv_f_b", "f_conv_b", "ln1_g", "ln1_b", "ln2_g", "ln2_b")


def _train_step(x, loss_target, weights, mom_m, mom_v):
    S = N_SHARDS
    seq, d = x.shape[1], x.shape[2]
    nm = weights["meta"].shape[0]
    la = weights["a_w_in"].shape[0]
    lb = weights["b_w_in"].shape[0]
    depth = la + lb
    dr = weights["a_w_out"].shape[1] * S
    nb, bs = weights["a_w_r"].shape[1], weights["a_w_r"].shape[2]
    per_group = (LANES // math.gcd(bs, LANES))
    gs = per_group * bs
    heads = weights["kv_f_b"].shape[0]
    dff = weights["f_w_out"].shape[1] * S
    nkv = 2 * d + heads
    nkv_s = weights["kv_w"].shape[1]
    nkvp = _round_up(2 * d + LANES, 768) if 2 * d + LANES > 768 else 2 * d + LANES
    alpha = (2 * depth) ** 0.25
    xi, yi, ci = _coords()
    shard = 2 * xi + yi
    core_arr = jnp.reshape(ci, (1,)).astype(jnp.int32)
    shard_arr = jnp.reshape(shard, (1,)).astype(jnp.int32)
    shard_core_arr = jnp.stack([shard, ci]).astype(jnp.int32)

    def layer_keys(l):
        if l < la:
            mixer = [("a_w_in", l), ("a_w_out", l)]
        else:
            mixer = ([("kv_w", 0)] if l == la else []) + [("b_w_in", l - la), ("b_w_out", l - la)]
        return mixer + [("f_w_in", l), ("f_w_out", l)]

    keys = [kl for l in range(depth) for kl in layer_keys(l)]
    local2d = {(k, i): (weights[k][i] if weights[k].ndim == 3 else weights[k]) for k, i in keys}
    parts = [cast_into_slot(shard_arr, local2d[kl], f"cast_{kl[0]}{kl[1]}") for kl in keys]
    small_local = [weights[k] for k, _ in SMALL_SHARDED]
    sm_flat = _pack(small_local, 2 * SUBLANES * LANES).reshape(1, 2, -1, LANES)
    parts.append(lax.dynamic_update_slice_in_dim(lax.empty((S,) + sm_flat.shape[1:], F32), sm_flat, shard, axis=0))
    gathered = gather_shards(parts, "gather_weights")
    gw = {}
    for kl, g in zip(keys, gathered[:-1]):
        rows, cols = local2d[kl].shape
        gw[kl] = g.reshape(S * rows, cols) if kl[0] in OUT_TYPE else g.reshape(S, rows, cols)
    sm_all = gathered[-1].reshape(S, -1)
    small_full = {}
    per_shard = [_unpack(sm_all[s], [a.shape for a in small_local]) for s in range(S)]
    for idx, (k, axis) in enumerate(SMALL_SHARDED):
        small_full[k] = jnp.concatenate([per_shard[s][idx] for s in range(S)], axis=axis)
    kv_cat = jnp.moveaxis(gw[("kv_w", 0)], 0, 1).reshape(d, S * nkv_s)
    kv_pad = jnp.pad(kv_cat, ((0, 0), (0, nkvp - nkv))).reshape(1, d, nkvp)
    fb_pad = jnp.pad(weights["kv_f_b"], (0, LANES - heads)).reshape(1, LANES)
    wr_g = [_block_diag(weights["a_w_r"][l], per_group) for l in range(la)]
    wi_g = [_block_diag(weights["a_w_i"][l], per_group) for l in range(la)]
    row = lambda v: v.reshape(1, -1)

    h, hb = embed_fwd(small_full["meta"], x[0], "embed")
    saved = []
    kvz = ct_pad = None
    _, tp, _ = _attn_geometry(nm + seq)
    t = nm + seq
    for l in range(depth):
        sv = {"hb_in": hb}
        if l < la:
            gr = mm_in(hb, gw[("a_w_in", l)], F32, f"a{l}_in")
            rc, rcb = a_conv_fwd(gr, small_full["a_conv_w"][l], row(small_full["a_conv_b"][l]), f"a{l}_conv")
            r_pre, i_pre = mm_bd(rcb, wr_g[l], wi_g[l], f"a{l}_gates")
            hs, gb = a_elem_fwd(gr, rc, r_pre, i_pre, row(small_full["a_b_r"][l]), row(small_full["a_b_i"][l]),
                                row(small_full["a_lambda"][l]), f"a{l}_lru")
            mix = mm_out(gb, gw[("a_w_out", l)], f"a{l}_out")
            sv.update(gr=gr, rc=rc, rcb=rcb, r_pre=r_pre, i_pre=i_pre, hs=hs, gb=gb)
        else:
            j = l - la
            if j == 0:
                kvz = mm_in(hb, kv_pad, F32, "kv_proj")
                cum = kv_fwd(kvz, fb_pad, d, "kv_forget")
                ct_pad = jnp.pad(cum[:, :heads].T, ((0, 0), (0, tp - t))).reshape(heads, 1, tp)
                kv_hb = hb
            qg = mm_in(hb, gw[("b_w_in", j)], F32, f"b{j}_in")
            o, mob = attn_fwd(qg, kvz, ct_pad, d, heads, f"b{j}_attn")
            mix = mm_out(mob, gw[("b_w_out", j)], f"b{j}_out")
            sv.update(qg=qg, o=o, mob=mob)
        h1, h1b, xh1, rs1 = ln_fwd(h, mix, row(weights["ln1_g"][l]), row(weights["ln1_b"][l]), alpha, f"ln1_{l}")
        zf = mm_in(h1b, gw[("f_w_in", l)], F32, f"f{l}_in")
        ffb = f_elem_fwd(zf, small_full["f_conv_w"][l], row(weights["f_conv_b"][l]), f"f{l}_act")
        ffo = mm_out(ffb, gw[("f_w_out", l)], f"f{l}_out")
        h2, h2b, xh2, rs2 = ln_fwd(h1, ffo, row(weights["ln2_g"][l]), row(weights["ln2_b"][l]), alpha, f"ln2_{l}")
        sv.update(h1b=h1b, xh1=xh1, rs1=rs1, zf=zf, ffb=ffb, xh2=xh2, rs2=rs2)
        saved.append(sv)
        h, hb = h2, h2b
    loss11, dy = loss_fwd_bwd(h, loss_target[0], nm, "loss")

    grads = {}

    def by_owner(kl, g3):
        rows, cols = local2d[kl].shape
        grads[kl] = g3.reshape(S, 2, rows // 2, cols)

    g_small = {}
    per_layer = {k: [None] * n for k, n in (
        ("a_conv_w", la), ("a_conv_b", la), ("a_w_r", la), ("a_b_r", la), ("a_w_i", la), ("a_b_i", la),
        ("a_lambda", la), ("f_conv_w", depth), ("f_conv_b", depth), ("ln1_g", depth), ("ln1_b", depth),
        ("ln2_g", depth), ("ln2_b", depth))}
    adds = [(dy, 1.0)]
    dks, dvs, dcs = [], [], []
    for l in reversed(range(depth)):
        sv = saved[l]
        ds2, ds2b, dg2, db2 = ln_bwd(adds, sv["xh2"], sv["rs2"], row(weights["ln2_g"][l]), f"ln2_{l}_bwd")
        per_layer["ln2_g"][l], per_layer["ln2_b"][l] = dg2[0], db2[0]
        dff_v = mm_out_nt(ds2b, gw[("f_w_out", l)], f"f{l}_out_dx")
        by_owner(("f_w_out", l), mm_tn(sv["ffb"], ds2b, 1, f"f{l}_out_dw"))
        dzg, dzv, dwg, dwv, dbg, dbv = f_elem_bwd(sv["zf"], dff_v, small_full["f_conv_w"][l],
                                                  row(weights["f_conv_b"][l]), f"f{l}_act_bwd")
        dzb = jnp.concatenate([dzg, dzv], axis=1)
        per_layer["f_conv_w"][l] = jnp.concatenate([dwg, dwv], axis=1)
        per_layer["f_conv_b"][l] = jnp.concatenate([dbg, dbv], axis=1)[0]
        dh1_f = mm_in_nt(dzb, gw[("f_w_in", l)], f"f{l}_in_dx")
        by_owner(("f_w_in", l), mm_tn(sv["h1b"], dzb, S, f"f{l}_in_dw"))
        ds1, ds1b, dg1, db1 = ln_bwd([(ds2, alpha), (dh1_f, 1.0)], sv["xh1"], sv["rs1"],
                                     row(weights["ln1_g"][l]), f"ln1_{l}_bwd")
        per_layer["ln1_g"][l], per_layer["ln1_b"][l] = dg1[0], db1[0]
        if l < la:
            dgv = mm_out_nt(ds1b, gw[("a_w_out", l)], f"a{l}_out_dx")
            by_owner(("a_w_out", l), mm_tn(sv["gb"], ds1b, 1, f"a{l}_out_dw"))
            dgate_b, drp_b, dip_b, drc_d, dlam, dbr, dbi = a_elem_bwd(
                dgv, sv["gr"], sv["rc"], sv["r_pre"], sv["i_pre"], sv["hs"], row(small_full["a_b_r"][l]),
                row(small_full["a_b_i"][l]), row(small_full["a_lambda"][l]), f"a{l}_lru_bwd")
            drc_g = mm_bd_nt(drp_b, dip_b, wr_g[l], wi_g[l], f"a{l}_gates_dx")
            dwr_g, dwi_g = mm_bd_tn(sv["rcb"], drp_b, dip_b, gs, f"a{l}_gates_dw")
            drec_b, dcw, dcb = a_conv_bwd(drc_d, drc_g, sv["gr"], small_full["a_conv_w"][l], f"a{l}_conv_bwd")
            per_layer["a_w_r"][l] = _block_diag_extract(dwr_g, per_group, bs)
            per_layer["a_w_i"][l] = _block_diag_extract(dwi_g, per_group, bs)
            per_layer["a_lambda"][l], per_layer["a_b_r"][l], per_layer["a_b_i"][l] = dlam[0], dbr[0], dbi[0]
            per_layer["a_conv_w"][l], per_layer["a_conv_b"][l] = dcw, dcb[0]
            dgr_b = jnp.concatenate([dgate_b, drec_b], axis=1)
            dh_m = mm_in_nt(dgr_b, gw[("a_w_in", l)], f"a{l}_in_dx")
            by_owner(("a_w_in", l), mm_tn(sv["hb_in"], dgr_b, S, f"a{l}_in_dw"))
        else:
            j = l - la
            dmo = mm_out_nt(ds1b, gw[("b_w_out", j)], f"b{j}_out_dx")
            by_owner(("b_w_out", j), mm_tn(sv["mob"], ds1b, 1, f"b{j}_out_dw"))
            dq_b, dog_b, dk, dv, dct = attn_bwd(dmo, sv["qg"], kvz, sv["o"], ct_pad, d, heads, f"b{j}_attn_bwd")
            dks.append(dk)
            dvs.append(dv)
            dcs.append(jnp.pad(dct[:, 0, :t].T, ((0, 0), (0, LANES - heads))))
            dqg_b = jnp.concatenate([dq_b, dog_b], axis=1)
            dh_m = mm_in_nt(dqg_b, gw[("b_w_in", j)], f"b{j}_in_dx")
            by_owner(("b_w_in", j), mm_tn(sv["hb_in"], dqg_b, S, f"b{j}_in_dw"))
        adds = [(ds1, alpha), (dh_m, 1.0)]
        if l == la:
            dzf_b, dfb = kv_bwd(dcs, kvz, fb_pad, d, "kv_forget_bwd")
            dk_b = add_cast(dks[0], dks[1], "kv_dk") if lb == 2 else None
            dv_b = add_cast(dvs[0], dvs[1], "kv_dv") if lb == 2 else None
            dz_kv = jnp.concatenate([dk_b, dv_b, dzf_b, jnp.zeros((t, nkvp - 2 * d - LANES), BF16)], axis=1)
            dh_kv = mm_in_nt(dz_kv, kv_pad, "kv_proj_dx")
            kv_dw = mm_tn(kv_hb, dz_kv, 1, "kv_proj_dw")
            by_owner(("kv_w", 0), jnp.moveaxis(kv_dw[0, :, :nkv].reshape(d, S, nkv_s), 1, 0))
            g_small["kv_f_b"] = dfb[0, :heads]
            adds.append((dh_kv, 1.0))
    g_meta, g_x = embed_bwd(adds, nm, "embed_bwd")

    big_grads = [grads[kl] for kl in keys]

    small_names = list(SMALL_REPLICATED) + [k for k, _ in SMALL_SHARDED]
    g_small["meta"] = g_meta
    for k, vals in per_layer.items():
        g_small[k] = jnp.stack(vals)
    small_shapes = {k: (weights[k].shape if k in SMALL_REPLICATED else g_small[k].shape) for k in small_names}
    sm_g = _pack([g_small[k].reshape(small_shapes[k]) for k in small_names], S * 2 * SUBLANES * LANES)
    big_grads.append(sm_g.reshape(S, 2, -1, LANES))

    from_sib = swap_halves(big_grads, "grad_swap")
    csums = [add_halves(core_arr, g, o, f"chip_sum_{i}") for i, (g, o) in enumerate(zip(big_grads, from_sib))]
    from_chips = chip_all_to_all(csums, "grad_all_to_all")
    fin = {}
    for kl, g in zip(keys, big_grads):
        n_stack = weights[kl[0]].shape[0] if weights[kl[0]].ndim == 3 else 1
        fin.setdefault(kl[0], lax.empty((n_stack,) + g.shape[1:], F32))
    fin["small"] = lax.empty((1,) + big_grads[-1].shape[1:], F32)
    for i, (kl, cs, rv) in enumerate(zip(keys + [("small", 0)], csums, from_chips)):
        fin[kl[0]] = add_four(shard_core_arr, cs, rv, fin[kl[0]], kl[1], f"owner_sum_{i}")
    names = list(BIG) + ["small"]
    joined = dict(zip(names, join_halves([fin[k] for k in names], "grad_join")))
    sm_slot = lax.dynamic_update_slice_in_dim(lax.empty((S,) + joined["small"].shape[1:], F32), joined["small"],
                                              shard, axis=0)
    sm_red = gather_full(sm_slot, "small_gather").reshape(-1)

    out_g, out_d, out_m, out_v = {}, {}, {}, {}
    for k in BIG:
        w2 = weights[k].reshape(-1, weights[k].shape[-1])
        g2 = joined[k].reshape(w2.shape)
        dlt, mn, vn = adamw(w2, g2, mom_m[k].reshape(w2.shape), mom_v[k].reshape(w2.shape), "adamw_" + k)
        shp = weights[k].shape
        out_g[k], out_d[k], out_m[k], out_v[k] = g2.reshape(shp), dlt.reshape(shp), mn.reshape(shp), vn.reshape(shp)
    sm_vals = dict(zip(small_names, _unpack(sm_red, [small_shapes[k] for k in small_names])))
    local_small = {}
    for k in SMALL_REPLICATED:
        local_small[k] = sm_vals[k]
    for k, axis in SMALL_SHARDED:
        size = weights[k].shape[axis]
        local_small[k] = lax.dynamic_slice_in_dim(sm_vals[k], shard * size, size, axis=axis)
    mult = 512 * LANES
    pk = lambda src: _pack([src[k] for k in small_names], mult).reshape(-1, LANES)
    dlt, mn, vn = adamw(pk(weights), pk(local_small), pk(mom_m), pk(mom_v), "adamw_small")
    shapes_local = [weights[k].shape for k in small_names]
    for dst, packed in ((out_d, dlt), (out_m, mn), (out_v, vn)):
        for k, val in zip(small_names, _unpack(packed.reshape(-1), shapes_local)):
            dst[k] = val
    for k in small_names:
        out_g[k] = local_small[k]

    loss = lax.psum(loss11[0, 0], ("x", "y", "c"))
    return (loss, g_x[None], *[out_g[k] for k in WEIGHT_ORDER], *[out_d[k] for k in WEIGHT_ORDER],
            *[out_m[k] for k in WEIGHT_ORDER], *[out_v[k] for k in WEIGHT_ORDER])
```

```python
import functools
import math

import jax
import jax.numpy as jnp
from jax import lax
from jax.experimental import pallas as pl
from jax.experimental.pallas import tpu as pltpu

F32 = jnp.float32
BF16 = jnp.bfloat16

LRU_C = 8.0
LN_EPS = 1e-5
ADAM_LR = 0.001
ADAM_B1 = 0.9
ADAM_B2 = 0.999
ADAM_EPS = 1e-08
ADAM_WD = 0.01
ADAM_STEP = 10

LANES = 128
SUBLANES = 8
V7X_VMEM_BYTES = 64 * 1024 * 1024
VMEM_LIMIT = V7X_VMEM_BYTES * 7 // 8
N_SHARDS = 4
GELU_C0 = math.sqrt(2.0 / math.pi)
GELU_C1 = 0.044715
NEG_BIG = -1e30


def _cp(*sem):
    return pltpu.CompilerParams(dimension_semantics=tuple(sem), vmem_limit_bytes=VMEM_LIMIT)


def _tile(n, cap, mult=LANES):
    best = None
    d = mult
    while d <= min(n, cap):
        if n % d == 0:
            best = d
        d += mult
    return n if best is None else best


def _row_block(t):
    if t % 3 == 0 and (t // 3) % 16 == 0:
        return t // 3
    return t


def _round_up(n, m):
    return (n + m - 1) // m * m


def _sigmoid(v):
    return 1.0 / (1.0 + jnp.exp(-v))


def _softplus(v):
    return jnp.maximum(v, 0.0) + jnp.log(1.0 + jnp.exp(-jnp.abs(v)))


def _gelu_parts(v):
    v2 = v * v
    u = GELU_C0 * (v + GELU_C1 * v * v2)
    t = jnp.tanh(u)
    g = 0.5 * v * (1.0 + t)
    dg = 0.5 * (1.0 + t) + 0.5 * v * (1.0 - t * t) * (GELU_C0 * (1.0 + 3.0 * GELU_C1 * v2))
    return g, dg


def _gelu(v):
    u = GELU_C0 * (v + GELU_C1 * v * v * v)
    return 0.5 * v * (1.0 + jnp.tanh(u))


def _neg_expm1(v):
    series = -v * (1.0 + 0.5 * v * (1.0 + (v / 3.0) * (1.0 + 0.25 * v)))
    return jnp.where(v > -0.05, series, 1.0 - jnp.exp(v))


def _shift_down(v, j):
    if j == 0:
        return v
    rows = lax.broadcasted_iota(jnp.int32, v.shape, 0)
    return jnp.where(rows >= j, pltpu.roll(v, j, 0), 0.0)


def _shift_up(v, j):
    if j == 0:
        return v
    n = v.shape[0]
    rows = lax.broadcasted_iota(jnp.int32, v.shape, 0)
    return jnp.where(rows < n - j, pltpu.roll(v, n - j, 0), 0.0)


def _scan_rows(a_ref, b_ref, out_ref, n_rows, width, reverse):
    n_groups = n_rows // SUBLANES
    rows = lax.broadcasted_iota(jnp.int32, (SUBLANES, width), 0)
    edge = 0 if reverse else SUBLANES - 1

    def body(g, carry):
        grp = (n_groups - 1 - g) if reverse else g
        off = pl.multiple_of(grp * SUBLANES, SUBLANES)
        b = b_ref[pl.ds(off, SUBLANES), :]
        a = None if a_ref is None else a_ref[pl.ds(off, SUBLANES), :]
        for d in (1, 2, 4):
            if reverse:
                keep = rows < SUBLANES - d
                sh = SUBLANES - d
            else:
                keep = rows >= d
                sh = d
            b_s = jnp.where(keep, pltpu.roll(b, sh, 0), 0.0)
            if a is None:
                b = b + b_s
            else:
                a_s = jnp.where(keep, pltpu.roll(a, sh, 0), 1.0)
                b = a * b_s + b
                a = a * a_s
        h = b + carry if a is None else b + a * carry
        out_ref[pl.ds(off, SUBLANES), :] = h
        return jnp.sum(jnp.where(rows == edge, h, 0.0), axis=0, keepdims=True)

    lax.fori_loop(0, n_groups, body, jnp.zeros((1, width), F32), unroll=2)


def mm_in(x, w, out_dtype, name):
    t, k = x.shape
    s_n, _, ns = w.shape
    tn = _tile(ns, 1408)
    nj = ns // tn
    rb = _row_block(t)

    def body(x_ref, w_ref, o_ref):
        o_ref[...] = jnp.dot(x_ref[...], w_ref[...], preferred_element_type=F32).astype(o_ref.dtype)

    return pl.pallas_call(
        body, name=name, grid=(s_n, nj, t // rb),
        in_specs=[pl.BlockSpec((rb, k), lambda s, j, r: (r, 0)),
                  pl.BlockSpec((None, k, tn), lambda s, j, r: (s, 0, j))],
        out_specs=pl.BlockSpec((rb, tn), lambda s, j, r: (r, s * nj + j)),
        out_shape=jax.ShapeDtypeStruct((t, s_n * ns), out_dtype),
        compiler_params=_cp("parallel", "parallel", "parallel"))(x, w)


def mm_out(x, w, name):
    t, k = x.shape
    n = w.shape[1]
    rb = _row_block(t)

    def body(x_ref, w_ref, o_ref):
        o_ref[...] = jnp.dot(x_ref[...], w_ref[...], preferred_element_type=F32)

    return pl.pallas_call(
        body, name=name, grid=(t // rb,),
        in_specs=[pl.BlockSpec((rb, k), lambda r: (r, 0)), pl.BlockSpec((k, n), lambda r: (0, 0))],
        out_specs=pl.BlockSpec((rb, n), lambda r: (r, 0)),
        out_shape=jax.ShapeDtypeStruct((t, n), F32),
        compiler_params=_cp("parallel"))(x, w)


def mm_in_nt(dy, w, name):
    t, _ = dy.shape
    s_n, k, ns = w.shape
    tn = _tile(ns, 1408)
    nj = ns // tn
    rb = _row_block(t)

    def body(dy_ref, w_ref, o_ref):
        @pl.when((pl.program_id(1) == 0) & (pl.program_id(2) == 0))
        def _():
            o_ref[...] = jnp.zeros_like(o_ref)
        o_ref[...] += lax.dot_general(dy_ref[...], w_ref[...], (((1,), (1,)), ((), ())),
                                      preferred_element_type=F32)

    return pl.pallas_call(
        body, name=name, grid=(t // rb, s_n, nj),
        in_specs=[pl.BlockSpec((rb, tn), lambda r, s, j: (r, s * nj + j)),
                  pl.BlockSpec((None, k, tn), lambda r, s, j: (s, 0, j))],
        out_specs=pl.BlockSpec((rb, k), lambda r, s, j: (r, 0)),
        out_shape=jax.ShapeDtypeStruct((t, k), F32),
        compiler_params=_cp("parallel", "arbitrary", "arbitrary"))(dy, w)


def mm_out_nt(dy, w, name):
    t, n = dy.shape
    k = w.shape[0]
    rb = _row_block(t)

    def body(dy_ref, w_ref, o_ref):
        o_ref[...] = lax.dot_general(dy_ref[...], w_ref[...], (((1,), (1,)), ((), ())),
                                     preferred_element_type=F32)

    return pl.pallas_call(
        body, name=name, grid=(t // rb,),
        in_specs=[pl.BlockSpec((rb, n), lambda r: (r, 0)), pl.BlockSpec((k, n), lambda r: (0, 0))],
        out_specs=pl.BlockSpec((rb, k), lambda r: (r, 0)),
        out_shape=jax.ShapeDtypeStruct((t, k), F32),
        compiler_params=_cp("parallel"))(dy, w)


def mm_tn(x, dy, s_n, name):
    t, kb = x.shape
    nb = dy.shape[1] // s_n
    tk = _tile(kb, 1408)
    tn = _tile(nb, 1408)
    nkb, nnb = kb // tk, nb // tn

    def body(x_ref, dy_ref, o_ref):
        o_ref[...] = lax.dot_general(x_ref[...], dy_ref[...], (((0,), (0,)), ((), ())),
                                     preferred_element_type=F32).astype(o_ref.dtype)

    return pl.pallas_call(
        body, name=name, grid=(s_n, nkb, nnb),
        in_specs=[pl.BlockSpec((t, tk), lambda s, a, b: (0, a)),
                  pl.BlockSpec((t, tn), lambda s, a, b: (0, s * nnb + b))],
        out_specs=pl.BlockSpec((None, tk, tn), lambda s, a, b: (s, a, b)),
        out_shape=jax.ShapeDtypeStruct((s_n, kb, nb), BF16),
        compiler_params=_cp("parallel", "parallel", "parallel"))(x, dy)


def mm_bd(x, wr, wi, name):
    t, _ = x.shape
    g_n, gs, _ = wr.shape
    rb = _row_block(t)

    def body(x_ref, wr_ref, wi_ref, r_ref, i_ref):
        xv = x_ref[...]
        r_ref[...] = jnp.dot(xv, wr_ref[...], preferred_element_type=F32)
        i_ref[...] = jnp.dot(xv, wi_ref[...], preferred_element_type=F32)

    blk = pl.BlockSpec((rb, gs), lambda g, r: (r, g))
    wspec = pl.BlockSpec((None, gs, gs), lambda g, r: (g, 0, 0))
    return pl.pallas_call(
        body, name=name, grid=(g_n, t // rb), in_specs=[blk, wspec, wspec], out_specs=[blk, blk],
        out_shape=[jax.ShapeDtypeStruct((t, g_n * gs), F32)] * 2,
        compiler_params=_cp("parallel", "parallel"))(x, wr, wi)


def mm_bd_nt(dr, di, wr, wi, name):
    t, _ = dr.shape
    g_n, gs, _ = wr.shape
    rb = _row_block(t)
    nt = (((1,), (1,)), ((), ()))

    def body(dr_ref, di_ref, wr_ref, wi_ref, o_ref):
        o_ref[...] = (lax.dot_general(dr_ref[...], wr_ref[...], nt, preferred_element_type=F32)
                      + lax.dot_general(di_ref[...], wi_ref[...], nt, preferred_element_type=F32))

    blk = pl.BlockSpec((rb, gs), lambda g, r: (r, g))
    wspec = pl.BlockSpec((None, gs, gs), lambda g, r: (g, 0, 0))
    return pl.pallas_call(
        body, name=name, grid=(g_n, t // rb), in_specs=[blk, blk, wspec, wspec], out_specs=blk,
        out_shape=jax.ShapeDtypeStruct((t, g_n * gs), F32),
        compiler_params=_cp("parallel", "parallel"))(dr, di, wr, wi)


def mm_bd_tn(x, dr, di, gs, name):
    t, w = x.shape
    g_n = w // gs
    tn_dims = (((0,), (0,)), ((), ()))

    def body(x_ref, dr_ref, di_ref, gr_ref, gi_ref):
        xv = x_ref[...]
        gr_ref[...] = lax.dot_general(xv, dr_ref[...], tn_dims, preferred_element_type=F32)
        gi_ref[...] = lax.dot_general(xv, di_ref[...], tn_dims, preferred_element_type=F32)

    blk = pl.BlockSpec((t, gs), lambda g: (0, g))
    ospec = pl.BlockSpec((None, gs, gs), lambda g: (g, 0, 0))
    return pl.pallas_call(
        body, name=name, grid=(g_n,), in_specs=[blk, blk, blk], out_specs=[ospec, ospec],
        out_shape=[jax.ShapeDtypeStruct((g_n, gs, gs), F32)] * 2,
        compiler_params=_cp("parallel"))(x, dr, di)


def embed_fwd(meta, x2d, name):
    nm, d = meta.shape
    seq = x2d.shape[0]
    t = nm + seq
    cb = _tile(d, 256)

    def body(m_ref, x_ref, h_ref, hb_ref):
        h_ref[pl.ds(0, nm), :] = m_ref[...]
        h_ref[pl.ds(nm, seq), :] = x_ref[...]
        hb_ref[pl.ds(0, nm), :] = m_ref[...].astype(BF16)
        hb_ref[pl.ds(nm, seq), :] = x_ref[...].astype(BF16)

    return pl.pallas_call(
        body, name=name, grid=(d // cb,),
        in_specs=[pl.BlockSpec((nm, cb), lambda j: (0, j)), pl.BlockSpec((seq, cb), lambda j: (0, j))],
        out_specs=[pl.BlockSpec((t, cb), lambda j: (0, j))] * 2,
        out_shape=[jax.ShapeDtypeStruct((t, d), F32), jax.ShapeDtypeStruct((t, d), BF16)],
        compiler_params=_cp("parallel"))(meta, x2d)


def embed_bwd(adds, nm, name):
    t, d = adds[0][0].shape
    seq = t - nm
    cb = _tile(d, 256)
    scales = [s for _, s in adds]
    n = len(adds)

    def body(*refs):
        tot = None
        for r, sc in zip(refs[:n], scales):
            term = r[...] if sc == 1.0 else sc * r[...]
            tot = term if tot is None else tot + term
        gm_ref, gx_ref = refs[n], refs[n + 1]
        gm_ref[...] = tot[0:nm]
        gx_ref[...] = tot[nm:t]

    return pl.pallas_call(
        body, name=name, grid=(d // cb,),
        in_specs=[pl.BlockSpec((t, cb), lambda j: (0, j))] * n,
        out_specs=[pl.BlockSpec((nm, cb), lambda j: (0, j)), pl.BlockSpec((seq, cb), lambda j: (0, j))],
        out_shape=[jax.ShapeDtypeStruct((nm, d), F32), jax.ShapeDtypeStruct((seq, d), F32)],
        compiler_params=_cp("parallel"))(*[a for a, _ in adds])


def loss_fwd_bwd(h, tgt, nm, name):
    t, d = h.shape
    seq = t - nm
    cb = _tile(d, 256)
    inv_d = 1.0 / d

    def body(h_ref, t_ref, loss_ref, dy_ref):
        @pl.when(pl.program_id(0) == 0)
        def _():
            loss_ref[...] = jnp.zeros_like(loss_ref)
        err = h_ref[pl.ds(nm, seq), :] - t_ref[...]
        dy_ref[pl.ds(0, nm), :] = jnp.zeros((nm, cb), F32)
        dy_ref[pl.ds(nm, seq), :] = err * inv_d
        loss_ref[...] += (0.5 * inv_d) * jnp.sum(err * err, keepdims=True)

    return pl.pallas_call(
        body, name=name, grid=(d // cb,),
        in_specs=[pl.BlockSpec((t, cb), lambda j: (0, j)), pl.BlockSpec((seq, cb), lambda j: (0, j))],
        out_specs=[pl.BlockSpec((1, 1), lambda j: (0, 0)), pl.BlockSpec((t, cb), lambda j: (0, j))],
        out_shape=[jax.ShapeDtypeStruct((1, 1), F32), jax.ShapeDtypeStruct((t, d), F32)],
        compiler_params=_cp("arbitrary"))(h, tgt)


def ln_fwd(h, mix, g, b, alpha, name):
    t, d = h.shape
    rb = _row_block(t)

    def body(h_ref, m_ref, g_ref, b_ref, y_ref, yb_ref, xh_ref, rs_ref):
        s = alpha * h_ref[...] + m_ref[...]
        mu = jnp.mean(s, axis=-1, keepdims=True)
        c = s - mu
        var = jnp.mean(c * c, axis=-1, keepdims=True)
        rstd = lax.rsqrt(var + LN_EPS)
        xh = c * rstd
        y = xh * g_ref[...] + b_ref[...]
        y_ref[...] = y
        yb_ref[...] = y.astype(BF16)
        xh_ref[...] = xh
        rs_ref[...] = rstd

    row = pl.BlockSpec((rb, d), lambda r: (r, 0))
    vec = pl.BlockSpec((1, d), lambda r: (0, 0))
    return pl.pallas_call(
        body, name=name, grid=(t // rb,), in_specs=[row, row, vec, vec],
        out_specs=[row, row, row, pl.BlockSpec((rb, 1), lambda r: (r, 0))],
        out_shape=[jax.ShapeDtypeStruct((t, d), F32), jax.ShapeDtypeStruct((t, d), BF16),
                   jax.ShapeDtypeStruct((t, d), F32), jax.ShapeDtypeStruct((t, 1), F32)],
        compiler_params=_cp("parallel"))(h, mix, g, b)


def ln_bwd(adds, xhat, rstd, g, name):
    t, d = xhat.shape
    rb = _row_block(t)
    scales = [s for _, s in adds]
    n = len(adds)

    def body(*refs):
        xh_ref, rs_ref, g_ref = refs[n:n + 3]
        ds_ref, dsb_ref, dg_ref, db_ref = refs[n + 3:]
        dy = None
        for r, sc in zip(refs[:n], scales):
            term = r[...] if sc == 1.0 else sc * r[...]
            dy = term if dy is None else dy + term

        @pl.when(pl.program_id(0) == 0)
        def _():
            dg_ref[...] = jnp.zeros_like(dg_ref)
            db_ref[...] = jnp.zeros_like(db_ref)

        xh = xh_ref[...]
        dxh = dy * g_ref[...]
        m1 = jnp.mean(dxh, axis=-1, keepdims=True)
        m2 = jnp.mean(dxh * xh, axis=-1, keepdims=True)
        ds = rs_ref[...] * (dxh - m1 - xh * m2)
        ds_ref[...] = ds
        dsb_ref[...] = ds.astype(BF16)
        dg_ref[...] += jnp.sum(dy * xh, axis=0, keepdims=True)
        db_ref[...] += jnp.sum(dy, axis=0, keepdims=True)

    row = pl.BlockSpec((rb, d), lambda r: (r, 0))
    vec = pl.BlockSpec((1, d), lambda r: (0, 0))
    return pl.pallas_call(
        body, name=name, grid=(t // rb,),
        in_specs=[row] * n + [row, pl.BlockSpec((rb, 1), lambda r: (r, 0)), vec],
        out_specs=[row, row, vec, vec],
        out_shape=[jax.ShapeDtypeStruct((t, d), F32), jax.ShapeDtypeStruct((t, d), BF16),
                   jax.ShapeDtypeStruct((1, d), F32), jax.ShapeDtypeStruct((1, d), F32)],
        compiler_params=_cp("arbitrary"))(*[a for a, _ in adds], xhat, rstd, g)


def _conv_fwd_val(xv, w_ref, b_ref, width):
    y = b_ref[...]
    for j in range(width):
        y = y + _shift_down(xv, j) * w_ref[pl.ds(width - 1 - j, 1), :]
    return y


def _conv_bwd_val(dout, xv, w_ref, width):
    dx = None
    dws = [None] * width
    for j in range(width):
        k = width - 1 - j
        term = _shift_up(dout, j) * w_ref[pl.ds(k, 1), :]
        dx = term if dx is None else dx + term
        dws[k] = jnp.sum(dout * _shift_down(xv, j), axis=0, keepdims=True)
    return dx, dws, jnp.sum(dout, axis=0, keepdims=True)


def a_conv_fwd(gr, cw, cbias, name):
    t, two_dr = gr.shape
    dr = two_dr // 2
    width = cw.shape[0]
    cb = _tile(dr, 256)
    off = dr // cb

    def body(x_ref, w_ref, b_ref, rc_ref, rcb_ref):
        y = _conv_fwd_val(x_ref[...], w_ref, b_ref, width)
        rc_ref[...] = y
        rcb_ref[...] = y.astype(BF16)

    return pl.pallas_call(
        body, name=name, grid=(dr // cb,),
        in_specs=[pl.BlockSpec((t, cb), lambda j: (0, off + j)),
                  pl.BlockSpec((width, cb), lambda j: (0, j)), pl.BlockSpec((1, cb), lambda j: (0, j))],
        out_specs=[pl.BlockSpec((t, cb), lambda j: (0, j))] * 2,
        out_shape=[jax.ShapeDtypeStruct((t, dr), F32), jax.ShapeDtypeStruct((t, dr), BF16)],
        compiler_params=_cp("parallel"))(gr, cw, cbias)


def a_conv_bwd(drc_a, drc_b, gr, cw, name):
    t, two_dr = gr.shape
    dr = two_dr // 2
    width = cw.shape[0]
    cb = _tile(dr, 256)
    off = dr // cb

    def body(da_ref, db_ref, x_ref, w_ref, dx_ref, dw_ref, dbias_ref):
        dout = da_ref[...] + db_ref[...]
        dx, dws, dbias = _conv_bwd_val(dout, x_ref[...], w_ref, width)
        dx_ref[...] = dx.astype(BF16)
        for k in range(width):
            dw_ref[pl.ds(k, 1), :] = dws[k]
        dbias_ref[...] = dbias

    col = pl.BlockSpec((t, cb), lambda j: (0, j))
    return pl.pallas_call(
        body, name=name, grid=(dr // cb,),
        in_specs=[col, col, pl.BlockSpec((t, cb), lambda j: (0, off + j)),
                  pl.BlockSpec((width, cb), lambda j: (0, j))],
        out_specs=[col, pl.BlockSpec((width, cb), lambda j: (0, j)), pl.BlockSpec((1, cb), lambda j: (0, j))],
        out_shape=[jax.ShapeDtypeStruct((t, dr), BF16), jax.ShapeDtypeStruct((width, dr), F32),
                   jax.ShapeDtypeStruct((1, dr), F32)],
        compiler_params=_cp("parallel"))(drc_a, drc_b, gr, cw)


def _lru_gates(r_pre, i_pre, br, bi, lam):
    r = _sigmoid(r_pre + br)
    i = _sigmoid(i_pre + bi)
    sp = _softplus(-lam)
    la = -LRU_C * r * sp
    a = jnp.exp(la)
    m = jnp.sqrt(_neg_expm1(2.0 * la))
    return r, i, sp, la, a, m


def a_elem_fwd(gr, rc, r_pre, i_pre, br, bi, lam, name):
    t, dr = rc.shape
    cb = LANES

    def body(gate_ref, rc_ref, rp_ref, ip_ref, br_ref, bi_ref, lam_ref, hs_ref, g_ref, a_s, u_s):
        rcv = rc_ref[...]
        _, i, _, _, a, m = _lru_gates(rp_ref[...], ip_ref[...], br_ref[...], bi_ref[...], lam_ref[...])
        a_s[...] = a
        u_s[...] = m * (i * rcv)
        _scan_rows(a_s, u_s, hs_ref, t, cb, reverse=False)
        g_ref[...] = (_gelu(gate_ref[...]) * hs_ref[...]).astype(BF16)

    col = pl.BlockSpec((t, cb), lambda j: (0, j))
    vec = pl.BlockSpec((1, cb), lambda j: (0, j))
    return pl.pallas_call(
        body, name=name, grid=(dr // cb,),
        in_specs=[col, col, col, col, vec, vec, vec],
        out_specs=[col, col],
        out_shape=[jax.ShapeDtypeStruct((t, dr), F32), jax.ShapeDtypeStruct((t, dr), BF16)],
        scratch_shapes=[pltpu.VMEM((t, cb), F32), pltpu.VMEM((t, cb), F32)],
        compiler_params=_cp("parallel"))(gr, rc, r_pre, i_pre, br, bi, lam)


def a_elem_bwd(dg, gr, rc, r_pre, i_pre, hs, br, bi, lam, name):
    t, dr = rc.shape
    cb = LANES

    def body(dg_ref, gate_ref, rc_ref, rp_ref, ip_ref, hs_ref, br_ref, bi_ref, lam_ref,
             dgate_ref, dr_ref, di_ref, drc_ref, dlam_ref, dbr_ref, dbi_ref, a_s, b_s, g_s):
        lamv = lam_ref[...]
        rcv = rc_ref[...]
        hsv = hs_ref[...]
        dgv = dg_ref[...]
        r, i, sp, _, a, m = _lru_gates(rp_ref[...], ip_ref[...], br_ref[...], bi_ref[...], lamv)
        ge, dge = _gelu_parts(gate_ref[...])
        dgate_ref[...] = (dgv * hsv * dge).astype(BF16)
        a_s[...] = _shift_up(a, 1)
        b_s[...] = dgv * ge
        _scan_rows(a_s, b_s, g_s, t, cb, reverse=True)
        gsum = g_s[...]
        da = gsum * _shift_down(hsv, 1)
        dm = gsum * (i * rcv)
        d_i = gsum * m * rcv
        drc_ref[...] = gsum * m * i
        dla = a * da - dm * (a * a) / m
        d_r = (-LRU_C) * sp * dla
        dsp = jnp.sum((-LRU_C) * r * dla, axis=0, keepdims=True)
        dlam_ref[...] = -dsp * _sigmoid(-lamv)
        d_rp = d_r * r * (1.0 - r)
        d_ip = d_i * i * (1.0 - i)
        dr_ref[...] = d_rp.astype(BF16)
        di_ref[...] = d_ip.astype(BF16)
        dbr_ref[...] = jnp.sum(d_rp, axis=0, keepdims=True)
        dbi_ref[...] = jnp.sum(d_ip, axis=0, keepdims=True)

    col = pl.BlockSpec((t, cb), lambda j: (0, j))
    vec = pl.BlockSpec((1, cb), lambda j: (0, j))
    big_b = jax.ShapeDtypeStruct((t, dr), BF16)
    vec_s = jax.ShapeDtypeStruct((1, dr), F32)
    return pl.pallas_call(
        body, name=name, grid=(dr // cb,),
        in_specs=[col, col, col, col, col, col, vec, vec, vec],
        out_specs=[col, col, col, col, vec, vec, vec],
        out_shape=[big_b, big_b, big_b, jax.ShapeDtypeStruct((t, dr), F32), vec_s, vec_s, vec_s],
        scratch_shapes=[pltpu.VMEM((t, cb), F32)] * 3,
        compiler_params=_cp("parallel"))(dg, gr, rc, r_pre, i_pre, hs, br, bi, lam)


def f_elem_fwd(z, cw, cbias, name):
    t, two_f = z.shape
    dff = two_f // 2
    width = cw.shape[0]
    cb = _tile(dff, 256)
    off = dff // cb

    def body(zg_ref, zv_ref, wg_ref, wv_ref, bg_ref, bv_ref, o_ref):
        zcg = _conv_fwd_val(zg_ref[...], wg_ref, bg_ref, width)
        zcv = _conv_fwd_val(zv_ref[...], wv_ref, bv_ref, width)
        o_ref[...] = (_gelu(zcg) * zcv).astype(BF16)

    lo = lambda j: (0, j)
    hi = lambda j: (0, off + j)
    return pl.pallas_call(
        body, name=name, grid=(dff // cb,),
        in_specs=[pl.BlockSpec((t, cb), lo), pl.BlockSpec((t, cb), hi),
                  pl.BlockSpec((width, cb), lo), pl.BlockSpec((width, cb), hi),
                  pl.BlockSpec((1, cb), lo), pl.BlockSpec((1, cb), hi)],
        out_specs=pl.BlockSpec((t, cb), lo),
        out_shape=jax.ShapeDtypeStruct((t, dff), BF16),
        compiler_params=_cp("parallel"))(z, z, cw, cw, cbias, cbias)


def f_elem_bwd(z, dff_g, cw, cbias, name):
    t, two_f = z.shape
    dff = two_f // 2
    width = cw.shape[0]
    cb = _tile(dff, 256)
    off = dff // cb

    def body(zg_ref, zv_ref, d_ref, wg_ref, wv_ref, bg_ref, bv_ref,
             dzg_ref, dzv_ref, dwg_ref, dwv_ref, dbg_ref, dbv_ref):
        zg = zg_ref[...]
        zv = zv_ref[...]
        zcg = _conv_fwd_val(zg, wg_ref, bg_ref, width)
        zcv = _conv_fwd_val(zv, wv_ref, bv_ref, width)
        ge, dge = _gelu_parts(zcg)
        dv = d_ref[...]
        dx, dws, dbias = _conv_bwd_val(dv * zcv * dge, zg, wg_ref, width)
        dzg_ref[...] = dx.astype(BF16)
        for k in range(width):
            dwg_ref[pl.ds(k, 1), :] = dws[k]
        dbg_ref[...] = dbias
        dx, dws, dbias = _conv_bwd_val(dv * ge, zv, wv_ref, width)
        dzv_ref[...] = dx.astype(BF16)
        for k in range(width):
            dwv_ref[pl.ds(k, 1), :] = dws[k]
        dbv_ref[...] = dbias

    lo = lambda j: (0, j)
    hi = lambda j: (0, off + j)
    col = pl.BlockSpec((t, cb), lo)
    wsp = pl.BlockSpec((width, cb), lo)
    vsp = pl.BlockSpec((1, cb), lo)
    return pl.pallas_call(
        body, name=name, grid=(dff // cb,),
        in_specs=[col, pl.BlockSpec((t, cb), hi), col, wsp, pl.BlockSpec((width, cb), hi),
                  vsp, pl.BlockSpec((1, cb), hi)],
        out_specs=[col, col, wsp, wsp, vsp, vsp],
        out_shape=[jax.ShapeDtypeStruct((t, dff), BF16)] * 2
        + [jax.ShapeDtypeStruct((width, dff), F32)] * 2 + [jax.ShapeDtypeStruct((1, dff), F32)] * 2,
        compiler_params=_cp("parallel"))(z, z, dff_g, cw, cw, cbias, cbias)


def kv_fwd(z, fb, d_model, name):
    t, _ = z.shape
    blk = 2 * d_model // LANES

    def body(z_ref, fb_ref, c_ref, lf_s):
        v = z_ref[...] + fb_ref[...]
        lf_s[...] = -_softplus(-v)
        _scan_rows(None, lf_s, c_ref, t, LANES, reverse=False)

    return pl.pallas_call(
        body, name=name, grid=(1,),
        in_specs=[pl.BlockSpec((t, LANES), lambda j: (0, blk)), pl.BlockSpec((1, LANES), lambda j: (0, 0))],
        out_specs=pl.BlockSpec((t, LANES), lambda j: (0, 0)),
        out_shape=jax.ShapeDtypeStruct((t, LANES), F32),
        scratch_shapes=[pltpu.VMEM((t, LANES), F32)],
        compiler_params=_cp("arbitrary"))(z, fb)


def kv_bwd(dcs, z, fb, d_model, name):
    t, _ = z.shape
    blk = 2 * d_model // LANES
    n = len(dcs)

    def body(*refs):
        z_ref, fb_ref, dz_ref, dfb_ref, dc_s, dl_s = refs[n:]
        tot = refs[0][...]
        for r in refs[1:n]:
            tot = tot + r[...]
        dc_s[...] = tot
        _scan_rows(None, dc_s, dl_s, t, LANES, reverse=True)
        v = z_ref[...] + fb_ref[...]
        dz = dl_s[...] * _sigmoid(-v)
        dz_ref[...] = dz.astype(BF16)
        dfb_ref[...] = jnp.sum(dz, axis=0, keepdims=True)

    full = pl.BlockSpec((t, LANES), lambda j: (0, 0))
    return pl.pallas_call(
        body, name=name, grid=(1,),
        in_specs=[full] * n + [pl.BlockSpec((t, LANES), lambda j: (0, blk)),
                               pl.BlockSpec((1, LANES), lambda j: (0, 0))],
        out_specs=[full, pl.BlockSpec((1, LANES), lambda j: (0, 0))],
        out_shape=[jax.ShapeDtypeStruct((t, LANES), BF16), jax.ShapeDtypeStruct((1, LANES), F32)],
        scratch_shapes=[pltpu.VMEM((t, LANES), F32)] * 2,
        compiler_params=_cp("arbitrary"))(*dcs, z, fb)


def add_cast(a, b, name):
    t, d = a.shape
    cb = _tile(d, 512)

    def body(a_ref, b_ref, o_ref):
        o_ref[...] = (a_ref[...] + b_ref[...]).astype(BF16)

    col = pl.BlockSpec((t, cb), lambda j: (0, j))
    return pl.pallas_call(body, name=name, grid=(d // cb,), in_specs=[col, col], out_specs=col,
                          out_shape=jax.ShapeDtypeStruct((t, d), BF16),
                          compiler_params=_cp("parallel"))(a, b)


def _attn_geometry(t):
    nqb = 6 if t > 1024 else 2
    tp = _round_up(t, LANES * nqb)
    return nqb, tp, tp // nqb


def _attn_scales(dh):
    scale = dh ** -0.5
    if math.log2(scale).is_integer():
        return scale, 1.0
    return 1.0, scale


def _attn_pieces(qs, ks, crow, j, i, tq, dh, s_mul):
    r0 = i * tq
    lanes = pl.ds(j * dh, dh)
    qi = qs[pl.ds(r0, tq), lanes]
    spans = ([(0, r0)] if i > 0 else []) + [(r0, tq)]
    logits = []
    for k0, n in spans:
        s = lax.dot_general(qi, ks[pl.ds(k0, n), lanes], (((1,), (1,)), ((), ())),
                            preferred_element_type=F32)
        if s_mul != 1.0:
            s = s * s_mul
        s = s - crow[:, k0:k0 + n]
        if k0 == r0:
            rows = lax.broadcasted_iota(jnp.int32, (tq, tq), 0)
            cols = lax.broadcasted_iota(jnp.int32, (tq, tq), 1)
            s = jnp.where(cols <= rows, s, NEG_BIG)
        logits.append(s)
    mx = jnp.max(logits[0], axis=1, keepdims=True)
    for s in logits[1:]:
        mx = jnp.maximum(mx, jnp.max(s, axis=1, keepdims=True))
    es = [jnp.exp(s - mx) for s in logits]
    tot = jnp.sum(es[0], axis=1, keepdims=True)
    for e in es[1:]:
        tot = tot + jnp.sum(e, axis=1, keepdims=True)
    inv = 1.0 / tot
    return [(k0, n, e * inv) for (k0, n), e in zip(spans, es)], qi


def attn_fwd(qg, z, ct_pad, d_model, n_heads, name):
    t = qg.shape[0]
    dh = d_model // n_heads
    hp = LANES // dh
    nqb, tp, tq = _attn_geometry(t)
    nblk = d_model // LANES
    q_mul, s_mul = _attn_scales(dh)

    def body(q_ref, og_ref, k_ref, v_ref, ct_ref, o_ref, mo_ref, qs, ks, vs, os_):
        pad = jnp.zeros((tp - t, LANES), BF16)
        qs[pl.ds(0, t), :] = (q_ref[...] * q_mul).astype(BF16)
        qs[pl.ds(t, tp - t), :] = pad
        for src, dst in ((k_ref, ks), (v_ref, vs)):
            dst[pl.ds(0, t), :] = src[...].astype(BF16)
            dst[pl.ds(t, tp - t), :] = pad
        for j in range(hp):
            crow = ct_ref[j]
            lanes = pl.ds(j * dh, dh)
            for i in range(nqb):
                pieces, _ = _attn_pieces(qs, ks, crow, j, i, tq, dh, s_mul)
                acc = None
                for k0, n, p in pieces:
                    part = jnp.dot(p.astype(BF16), vs[pl.ds(k0, n), lanes], preferred_element_type=F32)
                    acc = part if acc is None else acc + part
                os_[pl.ds(i * tq, tq), lanes] = acc
        o = os_[pl.ds(0, t), :]
        o_ref[...] = o
        mo_ref[...] = (o * _sigmoid(og_ref[...])).astype(BF16)

    col = lambda off: pl.BlockSpec((t, LANES), lambda p: (0, off + p))
    return pl.pallas_call(
        body, name=name, grid=(nblk,),
        in_specs=[col(0), col(nblk), col(0), col(nblk), pl.BlockSpec((hp, 1, tp), lambda p: (p, 0, 0))],
        out_specs=[col(0), col(0)],
        out_shape=[jax.ShapeDtypeStruct((t, d_model), F32), jax.ShapeDtypeStruct((t, d_model), BF16)],
        scratch_shapes=[pltpu.VMEM((tp, LANES), BF16)] * 3 + [pltpu.VMEM((tp, LANES), F32)],
        compiler_params=_cp("parallel"))(qg, qg, z, z, ct_pad)


def attn_bwd(dmo, qg, z, o, ct_pad, d_model, n_heads, name):
    t = qg.shape[0]
    dh = d_model // n_heads
    hp = LANES // dh
    nqb, tp, tq = _attn_geometry(t)
    nblk = d_model // LANES
    q_mul, s_mul = _attn_scales(dh)
    scale = dh ** -0.5
    tn_dims = (((0,), (0,)), ((), ()))
    nt_dims = (((1,), (1,)), ((), ()))

    def body(dmo_ref, q_ref, og_ref, k_ref, v_ref, o_ref, ct_ref,
             dq_ref, dog_ref, dk_ref, dv_ref, dct_ref, qs, ks, vs, dos, dqs, dks, dvs):
        pad = jnp.zeros((tp - t, LANES), BF16)
        sg = _sigmoid(og_ref[...])
        dmo_v = dmo_ref[...]
        dog_ref[...] = (dmo_v * o_ref[...] * sg * (1.0 - sg)).astype(BF16)
        dos[pl.ds(0, t), :] = (dmo_v * sg).astype(BF16)
        dos[pl.ds(t, tp - t), :] = pad
        qs[pl.ds(0, t), :] = (q_ref[...] * q_mul).astype(BF16)
        qs[pl.ds(t, tp - t), :] = pad
        for src, dst in ((k_ref, ks), (v_ref, vs)):
            dst[pl.ds(0, t), :] = src[...].astype(BF16)
            dst[pl.ds(t, tp - t), :] = pad
        dks[...] = jnp.zeros_like(dks)
        dvs[...] = jnp.zeros_like(dvs)
        dct_ref[...] = jnp.zeros_like(dct_ref)
        for j in range(hp):
            crow = ct_ref[j]
            lanes = pl.ds(j * dh, dh)
            for i in range(nqb):
                pieces, qi = _attn_pieces(qs, ks, crow, j, i, tq, dh, s_mul)
                do_i = dos[pl.ds(i * tq, tq), lanes]
                dps = [lax.dot_general(do_i, vs[pl.ds(k0, n), lanes], nt_dims, preferred_element_type=F32)
                       for k0, n, _ in pieces]
                row = None
                for (_, _, p), dp in zip(pieces, dps):
                    part = jnp.sum(p * dp, axis=1, keepdims=True)
                    row = part if row is None else row + part
                dq_i = None
                for (k0, n, p), dp in zip(pieces, dps):
                    ds = p * (dp - row)
                    ds_b = ds.astype(BF16)
                    keys = pl.ds(k0, n)
                    part = jnp.dot(ds_b, ks[keys, lanes], preferred_element_type=F32)
                    dq_i = part if dq_i is None else dq_i + part
                    dks[keys, lanes] += lax.dot_general(ds_b, qi, tn_dims, preferred_element_type=F32) * s_mul
                    dvs[keys, lanes] += lax.dot_general(p.astype(BF16), do_i, tn_dims,
                                                        preferred_element_type=F32)
                    dct_ref[j, :, keys] -= jnp.sum(ds, axis=0, keepdims=True)
                dqs[pl.ds(i * tq, tq), lanes] = dq_i * scale
        dq_ref[...] = dqs[pl.ds(0, t), :].astype(BF16)
        dk_ref[...] = dks[pl.ds(0, t), :]
        dv_ref[...] = dvs[pl.ds(0, t), :]

    col = lambda off: pl.BlockSpec((t, LANES), lambda p: (0, off + p))
    big = lambda dt: jax.ShapeDtypeStruct((t, d_model), dt)
    return pl.pallas_call(
        body, name=name, grid=(nblk,),
        in_specs=[col(0), col(0), col(nblk), col(0), col(nblk), col(0),
                  pl.BlockSpec((hp, 1, tp), lambda p: (p, 0, 0))],
        out_specs=[col(0), col(0), col(0), col(0), pl.BlockSpec((hp, 1, tp), lambda p: (p, 0, 0))],
        out_shape=[big(BF16), big(BF16), big(F32), big(F32),
                   jax.ShapeDtypeStruct((n_heads, 1, tp), F32)],
        scratch_shapes=[pltpu.VMEM((tp, LANES), BF16)] * 4 + [pltpu.VMEM((tp, LANES), F32)] * 3,
        compiler_params=_cp("parallel"))(dmo, qg, qg, z, z, o, ct_pad)


def cast_into_slot(shard, w2d, name):
    r, c = w2d.shape
    rh = r // 2
    tr = _tile(rh, 512, 16)
    n = rh // tr

    def body(sh_ref, w_ref, o_ref):
        del sh_ref
        o_ref[...] = w_ref[...].astype(BF16)

    return pl.pallas_call(
        body, name=name,
        grid_spec=pltpu.PrefetchScalarGridSpec(
            num_scalar_prefetch=1, grid=(2, n),
            in_specs=[pl.BlockSpec((tr, c), lambda h, i, sh: (h * n + i, 0))],
            out_specs=pl.BlockSpec((None, None, tr, c), lambda h, i, sh: (sh[0], h, i, 0))),
        out_shape=jax.ShapeDtypeStruct((N_SHARDS, 2, rh, c), BF16),
        compiler_params=_cp("parallel", "parallel"))(shard, w2d)


def add_halves(core, g, other, name):
    s_n, _, rh, c = g.shape
    tr = _tile(rh, 512, 16)

    def body(core_ref, g_ref, o_ref, out_ref):
        del core_ref
        out_ref[...] = (g_ref[...].astype(F32) + o_ref[...].astype(F32)).astype(out_ref.dtype)

    return pl.pallas_call(
        body, name=name,
        grid_spec=pltpu.PrefetchScalarGridSpec(
            num_scalar_prefetch=1, grid=(s_n, rh // tr),
            in_specs=[pl.BlockSpec((None, None, tr, c), lambda s, i, cr: (s, cr[0], i, 0)),
                      pl.BlockSpec((None, tr, c), lambda s, i, cr: (s, i, 0))],
            out_specs=pl.BlockSpec((None, tr, c), lambda s, i, cr: (s, i, 0))),
        out_shape=jax.ShapeDtypeStruct((s_n, rh, c), g.dtype),
        compiler_params=_cp("parallel", "parallel"))(core, g, other)


def add_four(shard_core, csum, recv, buf, layer, name):
    _, rh, c = csum.shape
    tr = _tile(rh, 512, 16)

    def body(sc_ref, a_ref, r_ref, buf_ref, out_ref):
        del sc_ref, buf_ref
        acc = a_ref[...].astype(F32)
        for k in range(3):
            acc = acc + r_ref[k].astype(F32)
        out_ref[...] = acc

    return pl.pallas_call(
        body, name=name,
        grid_spec=pltpu.PrefetchScalarGridSpec(
            num_scalar_prefetch=1, grid=(rh // tr,),
            in_specs=[pl.BlockSpec((None, tr, c), lambda i, sc: (sc[0], i, 0)),
                      pl.BlockSpec((3, tr, c), lambda i, sc: (0, i, 0)),
                      pl.BlockSpec(memory_space=pl.ANY)],
            out_specs=pl.BlockSpec((None, None, tr, c), lambda i, sc: (layer, sc[1], i, 0))),
        out_shape=jax.ShapeDtypeStruct(buf.shape, F32),
        input_output_aliases={3: 0},
        compiler_params=_cp("parallel"))(shard_core, csum, recv, buf)


def adamw(w, g, m, v, name):
    r, c = w.shape
    tr = _tile(r, 512, SUBLANES)
    c1 = 1.0 - ADAM_B1 ** ADAM_STEP
    c2 = 1.0 - ADAM_B2 ** ADAM_STEP

    def body(w_ref, g_ref, m_ref, v_ref, d_ref, mo_ref, vo_ref):
        gv = g_ref[...]
        mn = ADAM_B1 * m_ref[...] + (1.0 - ADAM_B1) * gv
        vn = ADAM_B2 * v_ref[...] + (1.0 - ADAM_B2) * (gv * gv)
        m_hat = mn / c1
        v_hat = vn / c2
        d_ref[...] = -ADAM_LR * (m_hat / (jnp.sqrt(v_hat) + ADAM_EPS) + ADAM_WD * w_ref[...])
        mo_ref[...] = mn
        vo_ref[...] = vn

    blk = pl.BlockSpec((tr, c), lambda i: (i, 0))
    return pl.pallas_call(
        body, name=name, grid=(r // tr,), in_specs=[blk] * 4, out_specs=[blk] * 3,
        out_shape=[jax.ShapeDtypeStruct((r, c), F32)] * 3,
        compiler_params=_cp("parallel"))(w, g, m, v)


def _coords():
    return lax.axis_index("x"), lax.axis_index("y"), lax.axis_index("c")


def _exchange(name, ins, out_shapes, plan, in_place=False):
    n_in = len(ins)
    n_out = len(out_shapes)
    n_rem = len(plan([None] * n_in, [None] * n_out, True))

    def body(*refs):
        in_refs = refs[:n_in]
        out_refs = refs[n_in:n_in + n_out]
        send_sems, recv_sems = refs[n_in + n_out:]
        remote = plan(list(in_refs), list(out_refs), False)
        copies = [pltpu.make_async_remote_copy(
            src_ref=src, dst_ref=dst, send_sem=send_sems.at[q], recv_sem=recv_sems.at[q],
            device_id=peer, device_id_type=pl.DeviceIdType.MESH)
            for q, (src, dst, peer, _) in enumerate(remote)]
        waited = set()
        for q, (_, _, _, after) in enumerate(remote):
            if after is not None and after not in waited:
                copies[after].wait_recv()
                waited.add(after)
            copies[q].start()
        for q, cp in enumerate(copies):
            if q not in waited:
                cp.wait_recv()
        for cp in copies:
            cp.wait_send()

    hbm = pl.BlockSpec(memory_space=pl.ANY)
    return pl.pallas_call(
        body, name=name, in_specs=[hbm] * n_in, out_specs=[hbm] * n_out, out_shape=out_shapes,
        input_output_aliases={i: i for i in range(n_in)} if in_place else {},
        scratch_shapes=[pltpu.SemaphoreType.DMA((n_rem,)), pltpu.SemaphoreType.DMA((n_rem,))],
        compiler_params=pltpu.CompilerParams(has_side_effects=True))(*ins)


def _split_start(name, groups, plan):
    flat = [a for grp in groups for a in grp]
    n, n_grp = len(flat), len(groups)
    counts = [len(plan(g, [None] * len(grp), True)) for g, grp in enumerate(groups)]

    def body(*refs):
        ins, sems, token = refs[:n], refs[n:n + 2 * n_grp], refs[-1]
        pos = 0
        for g, grp in enumerate(groups):
            arrs = list(ins[pos:pos + len(grp)])
            pos += len(grp)
            for q, (src, dst, peer) in enumerate(plan(g, arrs, False)):
                pltpu.make_async_remote_copy(
                    src_ref=src, dst_ref=dst, send_sem=sems[2 * g].at[q], recv_sem=sems[2 * g + 1].at[q],
                    device_id=peer, device_id_type=pl.DeviceIdType.MESH).start()
        token[...] = jnp.zeros_like(token)

    hbm = pl.BlockSpec(memory_space=pltpu.HBM)
    sem = pl.BlockSpec(memory_space=pltpu.SEMAPHORE)
    outs = pl.pallas_call(
        body, name=name,
        out_shape=[pltpu.SemaphoreType.DMA((cnt,)) for cnt in counts for _ in range(2)]
        + [pltpu.HBM(a.shape, a.dtype) for a in flat] + [jax.ShapeDtypeStruct((SUBLANES, LANES), F32)],
        in_specs=[hbm] * n, out_specs=[sem] * (2 * n_grp) + [hbm] * n + [pl.BlockSpec(memory_space=pltpu.VMEM)],
        input_output_aliases={i: 2 * n_grp + i for i in range(n)},
        compiler_params=pltpu.CompilerParams(has_side_effects=pltpu.SideEffectType.DATAFLOW_SIDE_EFFECTING),
    )(*[pltpu.with_memory_space_constraint(a, pltpu.HBM) for a in flat])
    started, pos = [], 2 * n_grp
    for g, grp in enumerate(groups):
        started.append((outs[2 * g], outs[2 * g + 1], list(outs[pos:pos + len(grp)])))
        pos += len(grp)
    return started, outs[-1]


def _split_wait(name, started, after, plan_g):
    send_sems, recv_sems, arrs = started
    n = len(arrs)

    def body(*refs):
        ins, ssem, rsem = list(refs[:n]), refs[n], refs[n + 1]
        for q, (src, dst, peer) in enumerate(plan_g(ins, False)):
            cp = pltpu.make_async_remote_copy(
                src_ref=src, dst_ref=dst, send_sem=ssem.at[q], recv_sem=rsem.at[q],
                device_id=peer, device_id_type=pl.DeviceIdType.MESH)
            cp.wait_send()
            cp.wait_recv()

    hbm = pl.BlockSpec(memory_space=pltpu.HBM)
    sem = pl.BlockSpec(memory_space=pltpu.SEMAPHORE)
    return pl.pallas_call(
        body, name=name, out_shape=[pltpu.HBM(a.shape, a.dtype) for a in arrs],
        in_specs=[hbm] * n + [sem, sem, pl.BlockSpec(memory_space=pl.ANY)], out_specs=[hbm] * n,
        input_output_aliases={i: i for i in range(n)},
        compiler_params=pltpu.CompilerParams(has_side_effects=pltpu.SideEffectType.DATAFLOW_SIDE_EFFECTING),
    )(*arrs, send_sems, recv_sems, after)


def _gather_ici_plan(arrs, count_only):
    if count_only:
        return [None] * (3 * len(arrs))
    x, y, c = _coords()
    pushes = []
    for a in arrs:
        mine = a.at[2 * x + y, c]
        pushes += [(mine, mine, peer) for peer, _ in _other_chips(x, y, c)]
    return pushes


def _all_to_all_plan(arrs, count_only):
    half = len(arrs) // 2
    if count_only:
        return [None] * (3 * half)
    x, y, c = _coords()
    pushes = []
    for src, land in zip(arrs[:half], arrs[half:]):
        pushes += [(src.at[shard], land.at[k], peer) for k, (peer, shard) in enumerate(_other_chips(x, y, c))]
    return pushes


def forward_to_sibling(bufs, name):
    n = len(bufs)

    def plan(ins, outs, count_only):
        if count_only:
            return [None] * (3 * n)
        x, y, c = _coords()
        pushes = []
        for i in range(n):
            for _, src_shard in _other_chips(x, y, c):
                slab = outs[i].at[src_shard, c]
                pushes.append((slab, slab, (x, y, 1 - c), None))
        return pushes

    shapes = [jax.ShapeDtypeStruct(b.shape, b.dtype) for b in bufs]
    return _exchange(name, bufs, shapes, plan, in_place=True)


def _other_chips(x, y, c):
    return [((1 - x, y, c), 2 * (1 - x) + y), ((x, 1 - y, c), 2 * x + 1 - y),
            ((1 - x, 1 - y, c), 2 * (1 - x) + 1 - y)]


def gather_shards(bufs, name):
    n = len(bufs)

    def plan(ins, outs, count_only):
        if count_only:
            return [None] * (6 * n)
        x, y, c = _coords()
        me = 2 * x + y
        chips = _other_chips(x, y, c)
        remote = []
        for i in range(n):
            mine = outs[i].at[me, c]
            for peer, _ in chips:
                remote.append((mine, mine, peer, None))
        for i in range(n):
            for k, (_, src_shard) in enumerate(chips):
                slab = outs[i].at[src_shard, c]
                remote.append((slab, slab, (x, y, 1 - c), 3 * i + k))
        return remote

    shapes = [jax.ShapeDtypeStruct(b.shape, b.dtype) for b in bufs]
    return _exchange(name, bufs, shapes, plan, in_place=True)


def swap_halves(grads, name):
    n = len(grads)
    shapes = [jax.ShapeDtypeStruct((g.shape[0],) + g.shape[2:], g.dtype) for g in grads]

    def plan(ins, outs, count_only):
        if count_only:
            return [None] * (n * N_SHARDS)
        x, y, c = _coords()
        remote = []
        for i in range(n):
            for s in range(N_SHARDS):
                remote.append((ins[i].at[s, 1 - c], outs[i].at[s], (x, y, 1 - c), None))
        return remote

    return _exchange(name, grads, shapes, plan)


def chip_all_to_all(csums, name):
    n = len(csums)
    shapes = [jax.ShapeDtypeStruct((3,) + g.shape[1:], g.dtype) for g in csums]

    def plan(ins, outs, count_only):
        if count_only:
            return [None] * (3 * n)
        x, y, c = _coords()
        remote = []
        for i in range(n):
            for k, (peer, shard) in enumerate(_other_chips(x, y, c)):
                remote.append((ins[i].at[shard], outs[i].at[k], peer, None))
        return remote

    return _exchange(name, csums, shapes, plan)


def join_halves(bufs, name):
    slots = [(i, l) for i, b in enumerate(bufs) for l in range(b.shape[0])]

    def plan(ins, outs, count_only):
        if count_only:
            return [None] * len(slots)
        x, y, c = _coords()
        return [(outs[i].at[l, c], outs[i].at[l, c], (x, y, 1 - c), None) for i, l in slots]

    shapes = [jax.ShapeDtypeStruct(b.shape, b.dtype) for b in bufs]
    return _exchange(name, bufs, shapes, plan, in_place=True)


def gather_full(buf, name):
    def plan(ins, outs, count_only):
        if count_only:
            return [None] * 3
        x, y, c = _coords()
        mine = outs[0].at[2 * x + y]
        return [(mine, mine, peer, None) for peer, _ in _other_chips(x, y, c)]

    return _exchange(name, [buf], [jax.ShapeDtypeStruct(buf.shape, buf.dtype)], plan, in_place=True)[0]


def _pack(arrays, multiple):
    flat = jnp.concatenate([a.reshape(-1) for a in arrays])
    n = flat.shape[0]
    return jnp.pad(flat, (0, _round_up(n, multiple) - n))


def _unpack(flat, shapes):
    out, pos = [], 0
    for shp in shapes:
        n = math.prod(shp)
        out.append(flat[pos:pos + n].reshape(shp))
        pos += n
    return out


def _block_diag(w, per_group):
    nb, bs, _ = w.shape
    g = nb // per_group
    w4 = w.reshape(g, per_group, bs, bs)
    eye = jnp.eye(per_group, dtype=w.dtype)
    full = w4[:, :, :, None, :] * eye[None, :, None, :, None]
    return full.reshape(g, per_group * bs, per_group * bs).astype(BF16)


def _block_diag_extract(full, per_group, bs):
    g = full.shape[0]
    f5 = full.reshape(g, per_group, bs, per_group, bs)
    idx = jnp.arange(per_group)
    picked = f5[:, idx, :, idx, :]
    return jnp.moveaxis(picked, 0, 1).reshape(g * per_group, bs, bs)


def kernel(x, meta, a_w_in, a_conv_w, a_conv_b, a_w_r, a_b_r, a_w_i, a_b_i, a_lambda, a_w_out, kv_w, kv_f_b, b_w_in, b_w_out, f_w_in, f_conv_w, f_conv_b, f_w_out, ln1_g, ln1_b, ln2_g, ln2_b, loss_target, m_meta, m_a_w_in, m_a_conv_w, m_a_conv_b, m_a_w_r, m_a_b_r, m_a_w_i, m_a_b_i, m_a_lambda, m_a_w_out, m_kv_w, m_kv_f_b, m_b_w_in, m_b_w_out, m_f_w_in, m_f_conv_w, m_f_conv_b, m_f_w_out, m_ln1_g, m_ln1_b, m_ln2_g, m_ln2_b, v_meta, v_a_w_in, v_a_conv_w, v_a_conv_b, v_a_w_r, v_a_b_r, v_a_w_i, v_a_b_i, v_a_lambda, v_a_w_out, v_kv_w, v_kv_f_b, v_b_w_in, v_b_w_out, v_f_w_in, v_f_conv_w, v_f_conv_b, v_f_w_out, v_ln1_g, v_ln1_b, v_ln2_g, v_ln2_b):
    weights = dict(meta=meta, a_w_in=a_w_in, a_conv_w=a_conv_w, a_conv_b=a_conv_b, a_w_r=a_w_r, a_b_r=a_b_r,
                   a_w_i=a_w_i, a_b_i=a_b_i, a_lambda=a_lambda, a_w_out=a_w_out, kv_w=kv_w, kv_f_b=kv_f_b,
                   b_w_in=b_w_in, b_w_out=b_w_out, f_w_in=f_w_in, f_conv_w=f_conv_w, f_conv_b=f_conv_b,
                   f_w_out=f_w_out, ln1_g=ln1_g, ln1_b=ln1_b, ln2_g=ln2_g, ln2_b=ln2_b)
    mom_m = dict(meta=m_meta, a_w_in=m_a_w_in, a_conv_w=m_a_conv_w, a_conv_b=m_a_conv_b, a_w_r=m_a_w_r,
                 a_b_r=m_a_b_r, a_w_i=m_a_w_i, a_b_i=m_a_b_i, a_lambda=m_a_lambda, a_w_out=m_a_w_out,
                 kv_w=m_kv_w, kv_f_b=m_kv_f_b, b_w_in=m_b_w_in, b_w_out=m_b_w_out, f_w_in=m_f_w_in,
                 f_conv_w=m_f_conv_w, f_conv_b=m_f_conv_b, f_w_out=m_f_w_out, ln1_g=m_ln1_g, ln1_b=m_ln1_b,
                 ln2_g=m_ln2_g, ln2_b=m_ln2_b)
    mom_v = dict(meta=v_meta, a_w_in=v_a_w_in, a_conv_w=v_a_conv_w, a_conv_b=v_a_conv_b, a_w_r=v_a_w_r,
                 a_b_r=v_a_b_r, a_w_i=v_a_w_i, a_b_i=v_a_b_i, a_lambda=v_a_lambda, a_w_out=v_a_w_out,
                 kv_w=v_kv_w, kv_f_b=v_kv_f_b, b_w_in=v_b_w_in, b_w_out=v_b_w_out, f_w_in=v_f_w_in,
                 f_conv_w=v_f_conv_w, f_conv_b=v_f_conv_b, f_w_out=v_f_w_out, ln1_g=v_ln1_g, ln1_b=v_ln1_b,
                 ln2_g=v_ln2_g, ln2_b=v_ln2_b)
    return _train_step(x, loss_target, weights, mom_m, mom_v)


WEIGHT_ORDER = ("meta", "a_w_in", "a_conv_w", "a_conv_b", "a_w_r", "a_b_r", "a_w_i", "a_b_i", "a_lambda",
                "a_w_out", "kv_w", "kv_f_b", "b_w_in", "b_w_out", "f_w_in", "f_conv_w", "f_conv_b",
                "f_w_out", "ln1_g", "ln1_b", "ln2_g", "ln2_b")
BIG = ("a_w_in", "a_w_out", "kv_w", "b_w_in", "b_w_out", "f_w_in", "f_w_out")
OUT_TYPE = ("a_w_out", "b_w_out", "f_w_out")
SMALL_SHARDED = (("meta", 1), ("a_conv_w", 2), ("a_conv_b", 1), ("a_b_r", 1), ("a_b_i", 1), ("a_lambda", 1),
                 ("f_conv_w", 2))
SMALL_REPLICATED = ("a_w_r", "a_w_i", "kv_f_b", "f_conv_b", "ln1_g", "ln1_b", "ln2_g", "ln2_b")


def _train_step(x, loss_target, weights, mom_m, mom_v):
    S = N_SHARDS
    seq, d = x.shape[1], x.shape[2]
    nm = weights["meta"].shape[0]
    la = weights["a_w_in"].shape[0]
    lb = weights["b_w_in"].shape[0]
    depth = la + lb
    dr = weights["a_w_out"].shape[1] * S
    nb, bs = weights["a_w_r"].shape[1], weights["a_w_r"].shape[2]
    per_group = (LANES // math.gcd(bs, LANES))
    gs = per_group * bs
    heads = weights["kv_f_b"].shape[0]
    dff = weights["f_w_out"].shape[1] * S
    nkv = 2 * d + heads
    nkv_s = weights["kv_w"].shape[1]
    nkvp = _round_up(2 * d + LANES, 768) if 2 * d + LANES > 768 else 2 * d + LANES
    alpha = (2 * depth) ** 0.25
    xi, yi, ci = _coords()
    shard = 2 * xi + yi
    core_arr = jnp.reshape(ci, (1,)).astype(jnp.int32)
    shard_arr = jnp.reshape(shard, (1,)).astype(jnp.int32)
    shard_core_arr = jnp.stack([shard, ci]).astype(jnp.int32)

    def mixer_keys(l):
        if l < la:
            return [("a_w_in", l), ("a_w_out", l)]
        return ([("kv_w", 0)] if l == la else []) + [("b_w_in", l - la), ("b_w_out", l - la)]

    def ffn_keys(l):
        return [("f_w_in", l), ("f_w_out", l)]

    groups = [mixer_keys(0), ffn_keys(0)] + [mixer_keys(l) + ffn_keys(l) for l in range(1, depth)]
    keys = [kl for grp in groups for kl in grp]
    local2d = {(k, i): (weights[k][i] if weights[k].ndim == 3 else weights[k]) for k, i in keys}
    small_local = [weights[k] for k, _ in SMALL_SHARDED]
    sm_flat = _pack(small_local, 2 * SUBLANES * LANES).reshape(1, 2, -1, LANES)
    sm_slot = lax.dynamic_update_slice_in_dim(lax.empty((S,) + sm_flat.shape[1:], F32), sm_flat, shard, axis=0)
    parts = [[cast_into_slot(shard_arr, local2d[kl], f"cast_{kl[0]}{kl[1]}") for kl in grp] for grp in groups]
    parts[0].append(sm_slot)
    in_flight, start_token = _split_start("gather_start", parts, lambda g, refs, cnt: _gather_ici_plan(refs, cnt))
    gw = {}

    def fetch(g, after):
        arrs = _split_wait(f"gather_wait_{g}", in_flight[g], after, _gather_ici_plan)
        arrs = forward_to_sibling(arrs, f"gather_fwd_{g}")
        for kl, a in zip(groups[g], arrs):
            rows, cols = local2d[kl].shape
            gw[kl] = a.reshape(S * rows, cols) if kl[0] in OUT_TYPE else a.reshape(S, rows, cols)
        return arrs

    sm_all = fetch(0, start_token)[-1].reshape(S, -1)
    small_full = {}
    per_shard = [_unpack(sm_all[s], [a.shape for a in small_local]) for s in range(S)]
    for idx, (k, axis) in enumerate(SMALL_SHARDED):
        small_full[k] = jnp.concatenate([per_shard[s][idx] for s in range(S)], axis=axis)
    fb_pad = jnp.pad(weights["kv_f_b"], (0, LANES - heads)).reshape(1, LANES)
    wr_g = [_block_diag(weights["a_w_r"][l], per_group) for l in range(la)]
    wi_g = [_block_diag(weights["a_w_i"][l], per_group) for l in range(la)]
    row = lambda v: v.reshape(1, -1)

    h, hb = embed_fwd(small_full["meta"], x[0], "embed")
    saved = []
    kvz = ct_pad = None
    _, tp, _ = _attn_geometry(nm + seq)
    t = nm + seq
    for l in range(depth):
        sv = {"hb_in": hb}
        if l > 0:
            fetch(l + 1, hb)
        if l < la:
            gr = mm_in(hb, gw[("a_w_in", l)], F32, f"a{l}_in")
            rc, rcb = a_conv_fwd(gr, small_full["a_conv_w"][l], row(small_full["a_conv_b"][l]), f"a{l}_conv")
            r_pre, i_pre = mm_bd(rcb, wr_g[l], wi_g[l], f"a{l}_gates")
            hs, gb = a_elem_fwd(gr, rc, r_pre, i_pre, row(small_full["a_b_r"][l]), row(small_full["a_b_i"][l]),
                                row(small_full["a_lambda"][l]), f"a{l}_lru")
            mix = mm_out(gb, gw[("a_w_out", l)], f"a{l}_out")
            sv.update(gr=gr, rc=rc, rcb=rcb, r_pre=r_pre, i_pre=i_pre, hs=hs, gb=gb)
        else:
            j = l - la
            if j == 0:
                kv_cat = jnp.moveaxis(gw[("kv_w", 0)], 0, 1).reshape(d, S * nkv_s)
                kv_pad = jnp.pad(kv_cat, ((0, 0), (0, nkvp - nkv))).reshape(1, d, nkvp)
                kvz = mm_in(hb, kv_pad, F32, "kv_proj")
                cum = kv_fwd(kvz, fb_pad, d, "kv_forget")
                ct_pad = jnp.pad(cum[:, :heads].T, ((0, 0), (0, tp - t))).reshape(heads, 1, tp)
                kv_hb = hb
            qg = mm_in(hb, gw[("b_w_in", j)], F32, f"b{j}_in")
            o, mob = attn_fwd(qg, kvz, ct_pad, d, heads, f"b{j}_attn")
            mix = mm_out(mob, gw[("b_w_out", j)], f"b{j}_out")
            sv.update(qg=qg, o=o, mob=mob)
        h1, h1b, xh1, rs1 = ln_fwd(h, mix, row(weights["ln1_g"][l]), row(weights["ln1_b"][l]), alpha, f"ln1_{l}")
        if l == 0:
            fetch(1, h1b)
        zf = mm_in(h1b, gw[("f_w_in", l)], F32, f"f{l}_in")
        ffb = f_elem_fwd(zf, small_full["f_conv_w"][l], row(weights["f_conv_b"][l]), f"f{l}_act")
        ffo = mm_out(ffb, gw[("f_w_out", l)], f"f{l}_out")
        h2, h2b, xh2, rs2 = ln_fwd(h1, ffo, row(weights["ln2_g"][l]), row(weights["ln2_b"][l]), alpha, f"ln2_{l}")
        sv.update(h1b=h1b, xh1=xh1, rs1=rs1, zf=zf, ffb=ffb, xh2=xh2, rs2=rs2)
        saved.append(sv)
        h, hb = h2, h2b
    loss11, dy = loss_fwd_bwd(h, loss_target[0], nm, "loss")

    grads = {}

    def by_owner(kl, g3):
        rows, cols = local2d[kl].shape
        grads[kl] = g3.reshape(S, 2, rows // 2, cols)

    reducing = []

    def send_grads(g, names, arrays):
        from_sib = swap_halves(arrays, f"grad_swap_{g}")
        csums = [add_halves(core_arr, a, o, f"chip_sum_{g}_{i}") for i, (a, o) in enumerate(zip(arrays, from_sib))]
        lands = [lax.empty((3,) + cs.shape[1:], cs.dtype) for cs in csums]
        started, _ = _split_start(f"grad_a2a_start_{g}", [csums + lands],
                                  lambda _, refs, cnt: _all_to_all_plan(refs, cnt))
        reducing.append((names, started[0]))

    g_small = {}
    per_layer = {k: [None] * n for k, n in (
        ("a_conv_w", la), ("a_conv_b", la), ("a_w_r", la), ("a_b_r", la), ("a_w_i", la), ("a_b_i", la),
        ("a_lambda", la), ("f_conv_w", depth), ("f_conv_b", depth), ("ln1_g", depth), ("ln1_b", depth),
        ("ln2_g", depth), ("ln2_b", depth))}
    adds = [(dy, 1.0)]
    dks, dvs, dcs = [], [], []
    for l in reversed(range(depth)):
        sv = saved[l]
        ds2, ds2b, dg2, db2 = ln_bwd(adds, sv["xh2"], sv["rs2"], row(weights["ln2_g"][l]), f"ln2_{l}_bwd")
        per_layer["ln2_g"][l], per_layer["ln2_b"][l] = dg2[0], db2[0]
        dff_v = mm_out_nt(ds2b, gw[("f_w_out", l)], f"f{l}_out_dx")
        by_owner(("f_w_out", l), mm_tn(sv["ffb"], ds2b, 1, f"f{l}_out_dw"))
        dzg, dzv, dwg, dwv, dbg, dbv = f_elem_bwd(sv["zf"], dff_v, small_full["f_conv_w"][l],
                                                  row(weights["f_conv_b"][l]), f"f{l}_act_bwd")
        dzb = jnp.concatenate([dzg, dzv], axis=1)
        per_layer["f_conv_w"][l] = jnp.concatenate([dwg, dwv], axis=1)
        per_layer["f_conv_b"][l] = jnp.concatenate([dbg, dbv], axis=1)[0]
        dh1_f = mm_in_nt(dzb, gw[("f_w_in", l)], f"f{l}_in_dx")
        by_owner(("f_w_in", l), mm_tn(sv["h1b"], dzb, S, f"f{l}_in_dw"))
        if l == 0:
            send_grads(1, groups[1], [grads[kl] for kl in groups[1]])
        ds1, ds1b, dg1, db1 = ln_bwd([(ds2, alpha), (dh1_f, 1.0)], sv["xh1"], sv["rs1"],
                                     row(weights["ln1_g"][l]), f"ln1_{l}_bwd")
        per_layer["ln1_g"][l], per_layer["ln1_b"][l] = dg1[0], db1[0]
        if l < la:
            dgv = mm_out_nt(ds1b, gw[("a_w_out", l)], f"a{l}_out_dx")
            by_owner(("a_w_out", l), mm_tn(sv["gb"], ds1b, 1, f"a{l}_out_dw"))
            dgate_b, drp_b, dip_b, drc_d, dlam, dbr, dbi = a_elem_bwd(
                dgv, sv["gr"], sv["rc"], sv["r_pre"], sv["i_pre"], sv["hs"], row(small_full["a_b_r"][l]),
                row(small_full["a_b_i"][l]), row(small_full["a_lambda"][l]), f"a{l}_lru_bwd")
            drc_g = mm_bd_nt(drp_b, dip_b, wr_g[l], wi_g[l], f"a{l}_gates_dx")
            dwr_g, dwi_g = mm_bd_tn(sv["rcb"], drp_b, dip_b, gs, f"a{l}_gates_dw")
            drec_b, dcw, dcb = a_conv_bwd(drc_d, drc_g, sv["gr"], small_full["a_conv_w"][l], f"a{l}_conv_bwd")
            per_layer["a_w_r"][l] = _block_diag_extract(dwr_g, per_group, bs)
            per_layer["a_w_i"][l] = _block_diag_extract(dwi_g, per_group, bs)
            per_layer["a_lambda"][l], per_layer["a_b_r"][l], per_layer["a_b_i"][l] = dlam[0], dbr[0], dbi[0]
            per_layer["a_conv_w"][l], per_layer["a_conv_b"][l] = dcw, dcb[0]
            dgr_b = jnp.concatenate([dgate_b, drec_b], axis=1)
            dh_m = mm_in_nt(dgr_b, gw[("a_w_in", l)], f"a{l}_in_dx")
            by_owner(("a_w_in", l), mm_tn(sv["hb_in"], dgr_b, S, f"a{l}_in_dw"))
        else:
            j = l - la
            dmo = mm_out_nt(ds1b, gw[("b_w_out", j)], f"b{j}_out_dx")
            by_owner(("b_w_out", j), mm_tn(sv["mob"], ds1b, 1, f"b{j}_out_dw"))
            dq_b, dog_b, dk, dv, dct = attn_bwd(dmo, sv["qg"], kvz, sv["o"], ct_pad, d, heads, f"b{j}_attn_bwd")
            dks.append(dk)
            dvs.append(dv)
            dcs.append(jnp.pad(dct[:, 0, :t].T, ((0, 0), (0, LANES - heads))))
            dqg_b = jnp.concatenate([dq_b, dog_b], axis=1)
            dh_m = mm_in_nt(dqg_b, gw[("b_w_in", j)], f"b{j}_in_dx")
            by_owner(("b_w_in", j), mm_tn(sv["hb_in"], dqg_b, S, f"b{j}_in_dw"))
        adds = [(ds1, alpha), (dh_m, 1.0)]
        if l == la:
            dzf_b, dfb = kv_bwd(dcs, kvz, fb_pad, d, "kv_forget_bwd")
            dk_b = add_cast(dks[0], dks[1], "kv_dk") if lb == 2 else None
            dv_b = add_cast(dvs[0], dvs[1], "kv_dv") if lb == 2 else None
            dz_kv = jnp.concatenate([dk_b, dv_b, dzf_b, jnp.zeros((t, nkvp - 2 * d - LANES), BF16)], axis=1)
            dh_kv = mm_in_nt(dz_kv, kv_pad, "kv_proj_dx")
            kv_dw = mm_tn(kv_hb, dz_kv, 1, "kv_proj_dw")
            by_owner(("kv_w", 0), jnp.moveaxis(kv_dw[0, :, :nkv].reshape(d, S, nkv_s), 1, 0))
            g_small["kv_f_b"] = dfb[0, :heads]
            adds.append((dh_kv, 1.0))
        if l > 0:
            send_grads(l + 1, groups[l + 1], [grads[kl] for kl in groups[l + 1]])
    g_meta, g_x = embed_bwd(adds, nm, "embed_bwd")

    small_names = list(SMALL_REPLICATED) + [k for k, _ in SMALL_SHARDED]
    g_small["meta"] = g_meta
    for k, vals in per_layer.items():
        g_small[k] = jnp.stack(vals)
    small_shapes = {k: (weights[k].shape if k in SMALL_REPLICATED else g_small[k].shape) for k in small_names}
    sm_g = _pack([g_small[k].reshape(small_shapes[k]) for k in small_names], S * 2 * SUBLANES * LANES)
    sm_g = sm_g.reshape(S, 2, -1, LANES)
    send_grads(0, groups[0] + [("small", 0)], [grads[kl] for kl in groups[0]] + [sm_g])

    fin = {"small": lax.empty((1,) + sm_g.shape[1:], F32)}
    for kl in keys:
        n_stack = weights[kl[0]].shape[0] if weights[kl[0]].ndim == 3 else 1
        rows, cols = local2d[kl].shape
        fin.setdefault(kl[0], lax.empty((n_stack, 2, rows // 2, cols), F32))
    for g, (names_g, started) in enumerate(reducing):
        arrs = _split_wait(f"grad_a2a_wait_{g}", started, g_x, _all_to_all_plan)
        half = len(names_g)
        for i, (kl, cs, rv) in enumerate(zip(names_g, arrs[:half], arrs[half:])):
            fin[kl[0]] = add_four(shard_core_arr, cs, rv, fin[kl[0]], kl[1], f"owner_sum_{g}_{i}")
    names = list(BIG) + ["small"]
    joined = dict(zip(names, join_halves([fin[k] for k in names], "grad_join")))
    sm_slot = lax.dynamic_update_slice_in_dim(lax.empty((S,) + joined["small"].shape[1:], F32), joined["small"],
                                              shard, axis=0)
    sm_red = gather_full(sm_slot, "small_gather").reshape(-1)

    out_g, out_d, out_m, out_v = {}, {}, {}, {}
    for k in BIG:
        w2 = weights[k].reshape(-1, weights[k].shape[-1])
        g2 = joined[k].reshape(w2.shape)
        dlt, mn, vn = adamw(w2, g2, mom_m[k].reshape(w2.shape), mom_v[k].reshape(w2.shape), "adamw_" + k)
        shp = weights[k].shape
        out_g[k], out_d[k], out_m[k], out_v[k] = g2.reshape(shp), dlt.reshape(shp), mn.reshape(shp), vn.reshape(shp)
    sm_vals = dict(zip(small_names, _unpack(sm_red, [small_shapes[k] for k in small_names])))
    local_small = {}
    for k in SMALL_REPLICATED:
        local_small[k] = sm_vals[k]
    for k, axis in SMALL_SHARDED:
        size = weights[k].shape[axis]
        local_small[k] = lax.dynamic_slice_in_dim(sm_vals[k], shard * size, size, axis=axis)
    mult = 512 * LANES
    pk = lambda src: _pack([src[k] for k in small_names], mult).reshape(-1, LANES)
    dlt, mn, vn = adamw(pk(weights), pk(local_small), pk(mom_m), pk(mom_v), "adamw_small")
    shapes_local = [weights[k].shape for k in small_names]
    for dst, packed in ((out_d, dlt), (out_m, mn), (out_v, vn)):
        for k, val in zip(small_names, _unpack(packed.reshape(-1), shapes_local)):
            dst[k] = val
    for k in small_names:
        out_g[k] = local_small[k]

    loss = lax.psum(loss11[0, 0], ("x", "y", "c"))
    return (loss, g_x[None], *[out_g[k] for k in WEIGHT_ORDER], *[out_d[k] for k in WEIGHT_ORDER],
            *[out_m[k] for k in WEIGHT_ORDER], *[out_v[k] for k in WEIGHT_ORDER])
```

```python
import functools
import math

import jax
import jax.numpy as jnp
from jax import lax
from jax.experimental import pallas as pl
from jax.experimental.pallas import tpu as pltpu

F32 = jnp.float32
BF16 = jnp.bfloat16

LRU_C = 8.0
LN_EPS = 1e-5
ADAM_LR = 0.001
ADAM_B1 = 0.9
ADAM_B2 = 0.999
ADAM_EPS = 1e-08
ADAM_WD = 0.01
ADAM_STEP = 10

LANES = 128
SUBLANES = 8
V7X_VMEM_BYTES = 64 * 1024 * 1024
VMEM_LIMIT = V7X_VMEM_BYTES * 7 // 8
N_SHARDS = 4
GELU_C0 = math.sqrt(2.0 / math.pi)
GELU_C1 = 0.044715
NEG_BIG = -1e30


def _cp(*sem):
    return pltpu.CompilerParams(dimension_semantics=tuple(sem), vmem_limit_bytes=VMEM_LIMIT)


def _tile(n, cap, mult=LANES):
    best = None
    d = mult
    while d <= min(n, cap):
        if n % d == 0:
            best = d
        d += mult
    return n if best is None else best


def _row_block(t):
    if t % 3 == 0 and (t // 3) % 16 == 0:
        return t // 3
    return t


def _round_up(n, m):
    return (n + m - 1) // m * m


def _sigmoid(v):
    return 1.0 / (1.0 + jnp.exp(-v))


def _softplus(v):
    return jnp.maximum(v, 0.0) + jnp.log(1.0 + jnp.exp(-jnp.abs(v)))


def _gelu_parts(v):
    v2 = v * v
    u = GELU_C0 * (v + GELU_C1 * v * v2)
    t = jnp.tanh(u)
    g = 0.5 * v * (1.0 + t)
    dg = 0.5 * (1.0 + t) + 0.5 * v * (1.0 - t * t) * (GELU_C0 * (1.0 + 3.0 * GELU_C1 * v2))
    return g, dg


def _gelu(v):
    u = GELU_C0 * (v + GELU_C1 * v * v * v)
    return 0.5 * v * (1.0 + jnp.tanh(u))


def _neg_expm1(v):
    series = -v * (1.0 + 0.5 * v * (1.0 + (v / 3.0) * (1.0 + 0.25 * v)))
    return jnp.where(v > -0.05, series, 1.0 - jnp.exp(v))


def _shift_down(v, j):
    if j == 0:
        return v
    rows = lax.broadcasted_iota(jnp.int32, v.shape, 0)
    return jnp.where(rows >= j, pltpu.roll(v, j, 0), 0.0)


def _shift_up(v, j):
    if j == 0:
        return v
    n = v.shape[0]
    rows = lax.broadcasted_iota(jnp.int32, v.shape, 0)
    return jnp.where(rows < n - j, pltpu.roll(v, n - j, 0), 0.0)


def _scan_rows(a_ref, b_ref, out_ref, n_rows, width, reverse):
    n_groups = n_rows // SUBLANES
    rows = lax.broadcasted_iota(jnp.int32, (SUBLANES, width), 0)
    edge = 0 if reverse else SUBLANES - 1

    def body(g, carry):
        grp = (n_groups - 1 - g) if reverse else g
        off = pl.multiple_of(grp * SUBLANES, SUBLANES)
        b = b_ref[pl.ds(off, SUBLANES), :]
        a = None if a_ref is None else a_ref[pl.ds(off, SUBLANES), :]
        for d in (1, 2, 4):
            if reverse:
                keep = rows < SUBLANES - d
                sh = SUBLANES - d
            else:
                keep = rows >= d
                sh = d
            b_s = jnp.where(keep, pltpu.roll(b, sh, 0), 0.0)
            if a is None:
                b = b + b_s
            else:
                a_s = jnp.where(keep, pltpu.roll(a, sh, 0), 1.0)
                b = a * b_s + b
                a = a * a_s
        h = b + carry if a is None else b + a * carry
        out_ref[pl.ds(off, SUBLANES), :] = h
        return jnp.sum(jnp.where(rows == edge, h, 0.0), axis=0, keepdims=True)

    lax.fori_loop(0, n_groups, body, jnp.zeros((1, width), F32), unroll=2)


def mm_in(x, w, out_dtype, name):
    t, k = x.shape
    s_n, _, ns = w.shape
    tn = _tile(ns, 1408)
    nj = ns // tn
    rb = _row_block(t)

    def body(x_ref, w_ref, o_ref):
        o_ref[...] = jnp.dot(x_ref[...], w_ref[...], preferred_element_type=F32).astype(o_ref.dtype)

    return pl.pallas_call(
        body, name=name, grid=(s_n, nj, t // rb),
        in_specs=[pl.BlockSpec((rb, k), lambda s, j, r: (r, 0)),
                  pl.BlockSpec((None, k, tn), lambda s, j, r: (s, 0, j))],
        out_specs=pl.BlockSpec((rb, tn), lambda s, j, r: (r, s * nj + j)),
        out_shape=jax.ShapeDtypeStruct((t, s_n * ns), out_dtype),
        compiler_params=_cp("parallel", "parallel", "parallel"))(x, w)


def mm_out(x, w, name):
    t, k = x.shape
    n = w.shape[1]
    rb = _row_block(t)

    def body(x_ref, w_ref, o_ref):
        o_ref[...] = jnp.dot(x_ref[...], w_ref[...], preferred_element_type=F32)

    return pl.pallas_call(
        body, name=name, grid=(t // rb,),
        in_specs=[pl.BlockSpec((rb, k), lambda r: (r, 0)), pl.BlockSpec((k, n), lambda r: (0, 0))],
        out_specs=pl.BlockSpec((rb, n), lambda r: (r, 0)),
        out_shape=jax.ShapeDtypeStruct((t, n), F32),
        compiler_params=_cp("parallel"))(x, w)


def mm_in_nt(dy, w, name):
    t, _ = dy.shape
    s_n, k, ns = w.shape
    tn = _tile(ns, 1408)
    nj = ns // tn
    rb = _row_block(t)

    def body(dy_ref, w_ref, o_ref):
        @pl.when((pl.program_id(1) == 0) & (pl.program_id(2) == 0))
        def _():
            o_ref[...] = jnp.zeros_like(o_ref)
        o_ref[...] += lax.dot_general(dy_ref[...], w_ref[...], (((1,), (1,)), ((), ())),
                                      preferred_element_type=F32)

    return pl.pallas_call(
        body, name=name, grid=(t // rb, s_n, nj),
        in_specs=[pl.BlockSpec((rb, tn), lambda r, s, j: (r, s * nj + j)),
                  pl.BlockSpec((None, k, tn), lambda r, s, j: (s, 0, j))],
        out_specs=pl.BlockSpec((rb, k), lambda r, s, j: (r, 0)),
        out_shape=jax.ShapeDtypeStruct((t, k), F32),
        compiler_params=_cp("parallel", "arbitrary", "arbitrary"))(dy, w)


def mm_out_nt(dy, w, name):
    t, n = dy.shape
    k = w.shape[0]
    rb = _row_block(t)

    def body(dy_ref, w_ref, o_ref):
        o_ref[...] = lax.dot_general(dy_ref[...], w_ref[...], (((1,), (1,)), ((), ())),
                                     preferred_element_type=F32)

    return pl.pallas_call(
        body, name=name, grid=(t // rb,),
        in_specs=[pl.BlockSpec((rb, n), lambda r: (r, 0)), pl.BlockSpec((k, n), lambda r: (0, 0))],
        out_specs=pl.BlockSpec((rb, k), lambda r: (r, 0)),
        out_shape=jax.ShapeDtypeStruct((t, k), F32),
        compiler_params=_cp("parallel"))(dy, w)


def mm_tn(x, dy, s_n, name):
    t, kb = x.shape
    nb = dy.shape[1] // s_n
    tk = _tile(kb, 1408)
    tn = _tile(nb, 1408)
    nkb, nnb = kb // tk, nb // tn

    def body(x_ref, dy_ref, o_ref):
        o_ref[...] = lax.dot_general(x_ref[...], dy_ref[...], (((0,), (0,)), ((), ())),
                                     preferred_element_type=F32).astype(o_ref.dtype)

    return pl.pallas_call(
        body, name=name, grid=(s_n, nkb, nnb),
        in_specs=[pl.BlockSpec((t, tk), lambda s, a, b: (0, a)),
                  pl.BlockSpec((t, tn), lambda s, a, b: (0, s * nnb + b))],
        out_specs=pl.BlockSpec((None, tk, tn), lambda s, a, b: (s, a, b)),
        out_shape=jax.ShapeDtypeStruct((s_n, kb, nb), BF16),
        compiler_params=_cp("parallel", "parallel", "parallel"))(x, dy)


def mm_bd(x, wr, wi, name):
    t, _ = x.shape
    g_n, gs, _ = wr.shape
    rb = _row_block(t)

    def body(x_ref, wr_ref, wi_ref, r_ref, i_ref):
        xv = x_ref[...]
        r_ref[...] = jnp.dot(xv, wr_ref[...], preferred_element_type=F32)
        i_ref[...] = jnp.dot(xv, wi_ref[...], preferred_element_type=F32)

    blk = pl.BlockSpec((rb, gs), lambda g, r: (r, g))
    wspec = pl.BlockSpec((None, gs, gs), lambda g, r: (g, 0, 0))
    return pl.pallas_call(
        body, name=name, grid=(g_n, t // rb), in_specs=[blk, wspec, wspec], out_specs=[blk, blk],
        out_shape=[jax.ShapeDtypeStruct((t, g_n * gs), F32)] * 2,
        compiler_params=_cp("parallel", "parallel"))(x, wr, wi)


def mm_bd_nt(dr, di, wr, wi, name):
    t, _ = dr.shape
    g_n, gs, _ = wr.shape
    rb = _row_block(t)
    nt = (((1,), (1,)), ((), ()))

    def body(dr_ref, di_ref, wr_ref, wi_ref, o_ref):
        o_ref[...] = (lax.dot_general(dr_ref[...], wr_ref[...], nt, preferred_element_type=F32)
                      + lax.dot_general(di_ref[...], wi_ref[...], nt, preferred_element_type=F32))

    blk = pl.BlockSpec((rb, gs), lambda g, r: (r, g))
    wspec = pl.BlockSpec((None, gs, gs), lambda g, r: (g, 0, 0))
    return pl.pallas_call(
        body, name=name, grid=(g_n, t // rb), in_specs=[blk, blk, wspec, wspec], out_specs=blk,
        out_shape=jax.ShapeDtypeStruct((t, g_n * gs), F32),
        compiler_params=_cp("parallel", "parallel"))(dr, di, wr, wi)


def mm_bd_tn(x, dr, di, gs, name):
    t, w = x.shape
    g_n = w // gs
    tn_dims = (((0,), (0,)), ((), ()))

    def body(x_ref, dr_ref, di_ref, gr_ref, gi_ref):
        xv = x_ref[...]
        gr_ref[...] = lax.dot_general(xv, dr_ref[...], tn_dims, preferred_element_type=F32)
        gi_ref[...] = lax.dot_general(xv, di_ref[...], tn_dims, preferred_element_type=F32)

    blk = pl.BlockSpec((t, gs), lambda g: (0, g))
    ospec = pl.BlockSpec((None, gs, gs), lambda g: (g, 0, 0))
    return pl.pallas_call(
        body, name=name, grid=(g_n,), in_specs=[blk, blk, blk], out_specs=[ospec, ospec],
        out_shape=[jax.ShapeDtypeStruct((g_n, gs, gs), F32)] * 2,
        compiler_params=_cp("parallel"))(x, dr, di)


def embed_fwd(meta, x2d, name):
    nm, d = meta.shape
    seq = x2d.shape[0]
    t = nm + seq
    cb = _tile(d, 256)

    def body(m_ref, x_ref, h_ref, hb_ref):
        h_ref[pl.ds(0, nm), :] = m_ref[...]
        h_ref[pl.ds(nm, seq), :] = x_ref[...]
        hb_ref[pl.ds(0, nm), :] = m_ref[...].astype(BF16)
        hb_ref[pl.ds(nm, seq), :] = x_ref[...].astype(BF16)

    return pl.pallas_call(
        body, name=name, grid=(d // cb,),
        in_specs=[pl.BlockSpec((nm, cb), lambda j: (0, j)), pl.BlockSpec((seq, cb), lambda j: (0, j))],
        out_specs=[pl.BlockSpec((t, cb), lambda j: (0, j))] * 2,
        out_shape=[jax.ShapeDtypeStruct((t, d), F32), jax.ShapeDtypeStruct((t, d), BF16)],
        compiler_params=_cp("parallel"))(meta, x2d)


def embed_bwd(adds, nm, name):
    t, d = adds[0][0].shape
    seq = t - nm
    cb = _tile(d, 256)
    scales = [s for _, s in adds]
    n = len(adds)

    def body(*refs):
        tot = None
        for r, sc in zip(refs[:n], scales):
            term = r[...] if sc == 1.0 else sc * r[...]
            tot = term if tot is None else tot + term
        gm_ref, gx_ref = refs[n], refs[n + 1]
        gm_ref[...] = tot[0:nm]
        gx_ref[...] = tot[nm:t]

    return pl.pallas_call(
        body, name=name, grid=(d // cb,),
        in_specs=[pl.BlockSpec((t, cb), lambda j: (0, j))] * n,
        out_specs=[pl.BlockSpec((nm, cb), lambda j: (0, j)), pl.BlockSpec((seq, cb), lambda j: (0, j))],
        out_shape=[jax.ShapeDtypeStruct((nm, d), F32), jax.ShapeDtypeStruct((seq, d), F32)],
        compiler_params=_cp("parallel"))(*[a for a, _ in adds])


def loss_fwd_bwd(h, tgt, nm, name):
    t, d = h.shape
    seq = t - nm
    cb = _tile(d, 256)
    inv_d = 1.0 / d

    def body(h_ref, t_ref, loss_ref, dy_ref):
        @pl.when(pl.program_id(0) == 0)
        def _():
            loss_ref[...] = jnp.zeros_like(loss_ref)
        err = h_ref[pl.ds(nm, seq), :] - t_ref[...]
        dy_ref[pl.ds(0, nm), :] = jnp.zeros((nm, cb), F32)
        dy_ref[pl.ds(nm, seq), :] = err * inv_d
        loss_ref[...] += (0.5 * inv_d) * jnp.sum(err * err, keepdims=True)

    return pl.pallas_call(
        body, name=name, grid=(d // cb,),
        in_specs=[pl.BlockSpec((t, cb), lambda j: (0, j)), pl.BlockSpec((seq, cb), lambda j: (0, j))],
        out_specs=[pl.BlockSpec((1, 1), lambda j: (0, 0)), pl.BlockSpec((t, cb), lambda j: (0, j))],
        out_shape=[jax.ShapeDtypeStruct((1, 1), F32), jax.ShapeDtypeStruct((t, d), F32)],
        compiler_params=_cp("arbitrary"))(h, tgt)


def ln_fwd(h, mix, g, b, alpha, name):
    t, d = h.shape
    rb = _row_block(t)

    def body(h_ref, m_ref, g_ref, b_ref, y_ref, yb_ref, xh_ref, rs_ref):
        s = alpha * h_ref[...] + m_ref[...]
        mu = jnp.mean(s, axis=-1, keepdims=True)
        c = s - mu
        var = jnp.mean(c * c, axis=-1, keepdims=True)
        rstd = lax.rsqrt(var + LN_EPS)
        xh = c * rstd
        y = xh * g_ref[...] + b_ref[...]
        y_ref[...] = y
        yb_ref[...] = y.astype(BF16)
        xh_ref[...] = xh
        rs_ref[...] = rstd

    row = pl.BlockSpec((rb, d), lambda r: (r, 0))
    vec = pl.BlockSpec((1, d), lambda r: (0, 0))
    return pl.pallas_call(
        body, name=name, grid=(t // rb,), in_specs=[row, row, vec, vec],
        out_specs=[row, row, row, pl.BlockSpec((rb, 1), lambda r: (r, 0))],
        out_shape=[jax.ShapeDtypeStruct((t, d), F32), jax.ShapeDtypeStruct((t, d), BF16),
                   jax.ShapeDtypeStruct((t, d), F32), jax.ShapeDtypeStruct((t, 1), F32)],
        compiler_params=_cp("parallel"))(h, mix, g, b)


def ln_bwd(adds, xhat, rstd, g, name):
    t, d = xhat.shape
    rb = _row_block(t)
    scales = [s for _, s in adds]
    n = len(adds)

    def body(*refs):
        xh_ref, rs_ref, g_ref = refs[n:n + 3]
        ds_ref, dsb_ref, dg_ref, db_ref = refs[n + 3:]
        dy = None
        for r, sc in zip(refs[:n], scales):
            term = r[...] if sc == 1.0 else sc * r[...]
            dy = term if dy is None else dy + term

        @pl.when(pl.program_id(0) == 0)
        def _():
            dg_ref[...] = jnp.zeros_like(dg_ref)
            db_ref[...] = jnp.zeros_like(db_ref)

        xh = xh_ref[...]
        dxh = dy * g_ref[...]
        m1 = jnp.mean(dxh, axis=-1, keepdims=True)
        m2 = jnp.mean(dxh * xh, axis=-1, keepdims=True)
        ds = rs_ref[...] * (dxh - m1 - xh * m2)
        ds_ref[...] = ds
        dsb_ref[...] = ds.astype(BF16)
        dg_ref[...] += jnp.sum(dy * xh, axis=0, keepdims=True)
        db_ref[...] += jnp.sum(dy, axis=0, keepdims=True)

    row = pl.BlockSpec((rb, d), lambda r: (r, 0))
    vec = pl.BlockSpec((1, d), lambda r: (0, 0))
    return pl.pallas_call(
        body, name=name, grid=(t // rb,),
        in_specs=[row] * n + [row, pl.BlockSpec((rb, 1), lambda r: (r, 0)), vec],
        out_specs=[row, row, vec, vec],
        out_shape=[jax.ShapeDtypeStruct((t, d), F32), jax.ShapeDtypeStruct((t, d), BF16),
                   jax.ShapeDtypeStruct((1, d), F32), jax.ShapeDtypeStruct((1, d), F32)],
        compiler_params=_cp("arbitrary"))(*[a for a, _ in adds], xhat, rstd, g)


def _conv_fwd_val(xv, w_ref, b_ref, width):
    y = b_ref[...]
    for j in range(width):
        y = y + _shift_down(xv, j) * w_ref[pl.ds(width - 1 - j, 1), :]
    return y


def _conv_bwd_val(dout, xv, w_ref, width):
    dx = None
    dws = [None] * width
    for j in range(width):
        k = width - 1 - j
        term = _shift_up(dout, j) * w_ref[pl.ds(k, 1), :]
        dx = term if dx is None else dx + term
        dws[k] = jnp.sum(dout * _shift_down(xv, j), axis=0, keepdims=True)
    return dx, dws, jnp.sum(dout, axis=0, keepdims=True)


def a_conv_fwd(gr, cw, cbias, name):
    t, two_dr = gr.shape
    dr = two_dr // 2
    width = cw.shape[0]
    cb = _tile(dr, 256)
    off = dr // cb

    def body(x_ref, w_ref, b_ref, rc_ref, rcb_ref):
        y = _conv_fwd_val(x_ref[...], w_ref, b_ref, width)
        rc_ref[...] = y
        rcb_ref[...] = y.astype(BF16)

    return pl.pallas_call(
        body, name=name, grid=(dr // cb,),
        in_specs=[pl.BlockSpec((t, cb), lambda j: (0, off + j)),
                  pl.BlockSpec((width, cb), lambda j: (0, j)), pl.BlockSpec((1, cb), lambda j: (0, j))],
        out_specs=[pl.BlockSpec((t, cb), lambda j: (0, j))] * 2,
        out_shape=[jax.ShapeDtypeStruct((t, dr), F32), jax.ShapeDtypeStruct((t, dr), BF16)],
        compiler_params=_cp("parallel"))(gr, cw, cbias)


def a_conv_bwd(drc_a, drc_b, gr, cw, name):
    t, two_dr = gr.shape
    dr = two_dr // 2
    width = cw.shape[0]
    cb = _tile(dr, 256)
    off = dr // cb

    def body(da_ref, db_ref, x_ref, w_ref, dx_ref, dw_ref, dbias_ref):
        dout = da_ref[...] + db_ref[...]
        dx, dws, dbias = _conv_bwd_val(dout, x_ref[...], w_ref, width)
        dx_ref[...] = dx.astype(BF16)
        for k in range(width):
            dw_ref[pl.ds(k, 1), :] = dws[k]
        dbias_ref[...] = dbias

    col = pl.BlockSpec((t, cb), lambda j: (0, j))
    return pl.pallas_call(
        body, name=name, grid=(dr // cb,),
        in_specs=[col, col, pl.BlockSpec((t, cb), lambda j: (0, off + j)),
                  pl.BlockSpec((width, cb), lambda j: (0, j))],
        out_specs=[col, pl.BlockSpec((width, cb), lambda j: (0, j)), pl.BlockSpec((1, cb), lambda j: (0, j))],
        out_shape=[jax.ShapeDtypeStruct((t, dr), BF16), jax.ShapeDtypeStruct((width, dr), F32),
                   jax.ShapeDtypeStruct((1, dr), F32)],
        compiler_params=_cp("parallel"))(drc_a, drc_b, gr, cw)


def _lru_gates(r_pre, i_pre, br, bi, lam):
    r = _sigmoid(r_pre + br)
    i = _sigmoid(i_pre + bi)
    sp = _softplus(-lam)
    la = -LRU_C * r * sp
    a = jnp.exp(la)
    m = jnp.sqrt(_neg_expm1(2.0 * la))
    return r, i, sp, la, a, m


def a_elem_fwd(gr, rc, r_pre, i_pre, br, bi, lam, name):
    t, dr = rc.shape
    cb = LANES

    def body(gate_ref, rc_ref, rp_ref, ip_ref, br_ref, bi_ref, lam_ref, hs_ref, g_ref, a_s, u_s):
        rcv = rc_ref[...]
        _, i, _, _, a, m = _lru_gates(rp_ref[...], ip_ref[...], br_ref[...], bi_ref[...], lam_ref[...])
        a_s[...] = a
        u_s[...] = m * (i * rcv)
        _scan_rows(a_s, u_s, hs_ref, t, cb, reverse=False)
        g_ref[...] = (_gelu(gate_ref[...]) * hs_ref[...]).astype(BF16)

    col = pl.BlockSpec((t, cb), lambda j: (0, j))
    vec = pl.BlockSpec((1, cb), lambda j: (0, j))
    return pl.pallas_call(
        body, name=name, grid=(dr // cb,),
        in_specs=[col, col, col, col, vec, vec, vec],
        out_specs=[col, col],
        out_shape=[jax.ShapeDtypeStruct((t, dr), F32), jax.ShapeDtypeStruct((t, dr), BF16)],
        scratch_shapes=[pltpu.VMEM((t, cb), F32), pltpu.VMEM((t, cb), F32)],
        compiler_params=_cp("parallel"))(gr, rc, r_pre, i_pre, br, bi, lam)


def a_elem_bwd(dg, gr, rc, r_pre, i_pre, hs, br, bi, lam, name):
    t, dr = rc.shape
    cb = LANES

    def body(dg_ref, gate_ref, rc_ref, rp_ref, ip_ref, hs_ref, br_ref, bi_ref, lam_ref,
             dgate_ref, dr_ref, di_ref, drc_ref, dlam_ref, dbr_ref, dbi_ref, a_s, b_s, g_s):
        lamv = lam_ref[...]
        rcv = rc_ref[...]
        hsv = hs_ref[...]
        dgv = dg_ref[...]
        r, i, sp, _, a, m = _lru_gates(rp_ref[...], ip_ref[...], br_ref[...], bi_ref[...], lamv)
        ge, dge = _gelu_parts(gate_ref[...])
        dgate_ref[...] = (dgv * hsv * dge).astype(BF16)
        a_s[...] = _shift_up(a, 1)
        b_s[...] = dgv * ge
        _scan_rows(a_s, b_s, g_s, t, cb, reverse=True)
        gsum = g_s[...]
        da = gsum * _shift_down(hsv, 1)
        dm = gsum * (i * rcv)
        d_i = gsum * m * rcv
        drc_ref[...] = gsum * m * i
        dla = a * da - dm * (a * a) / m
        d_r = (-LRU_C) * sp * dla
        dsp = jnp.sum((-LRU_C) * r * dla, axis=0, keepdims=True)
        dlam_ref[...] = -dsp * _sigmoid(-lamv)
        d_rp = d_r * r * (1.0 - r)
        d_ip = d_i * i * (1.0 - i)
        dr_ref[...] = d_rp.astype(BF16)
        di_ref[...] = d_ip.astype(BF16)
        dbr_ref[...] = jnp.sum(d_rp, axis=0, keepdims=True)
        dbi_ref[...] = jnp.sum(d_ip, axis=0, keepdims=True)

    col = pl.BlockSpec((t, cb), lambda j: (0, j))
    vec = pl.BlockSpec((1, cb), lambda j: (0, j))
    big_b = jax.ShapeDtypeStruct((t, dr), BF16)
    vec_s = jax.ShapeDtypeStruct((1, dr), F32)
    return pl.pallas_call(
        body, name=name, grid=(dr // cb,),
        in_specs=[col, col, col, col, col, col, vec, vec, vec],
        out_specs=[col, col, col, col, vec, vec, vec],
        out_shape=[big_b, big_b, big_b, jax.ShapeDtypeStruct((t, dr), F32), vec_s, vec_s, vec_s],
        scratch_shapes=[pltpu.VMEM((t, cb), F32)] * 3,
        compiler_params=_cp("parallel"))(dg, gr, rc, r_pre, i_pre, hs, br, bi, lam)


def f_elem_fwd(z, cw, cbias, name):
    t, two_f = z.shape
    dff = two_f // 2
    width = cw.shape[0]
    cb = _tile(dff, 256)
    off = dff // cb

    def body(zg_ref, zv_ref, wg_ref, wv_ref, bg_ref, bv_ref, o_ref):
        zcg = _conv_fwd_val(zg_ref[...], wg_ref, bg_ref, width)
        zcv = _conv_fwd_val(zv_ref[...], wv_ref, bv_ref, width)
        o_ref[...] = (_gelu(zcg) * zcv).astype(BF16)

    lo = lambda j: (0, j)
    hi = lambda j: (0, off + j)
    return pl.pallas_call(
        body, name=name, grid=(dff // cb,),
        in_specs=[pl.BlockSpec((t, cb), lo), pl.BlockSpec((t, cb), hi),
                  pl.BlockSpec((width, cb), lo), pl.BlockSpec((width, cb), hi),
                  pl.BlockSpec((1, cb), lo), pl.BlockSpec((1, cb), hi)],
        out_specs=pl.BlockSpec((t, cb), lo),
        out_shape=jax.ShapeDtypeStruct((t, dff), BF16),
        compiler_params=_cp("parallel"))(z, z, cw, cw, cbias, cbias)


def f_elem_bwd(z, dff_g, cw, cbias, name):
    t, two_f = z.shape
    dff = two_f // 2
    width = cw.shape[0]
    cb = _tile(dff, 256)
    off = dff // cb

    def body(zg_ref, zv_ref, d_ref, wg_ref, wv_ref, bg_ref, bv_ref,
             dzg_ref, dzv_ref, dwg_ref, dwv_ref, dbg_ref, dbv_ref):
        zg = zg_ref[...]
        zv = zv_ref[...]
        zcg = _conv_fwd_val(zg, wg_ref, bg_ref, width)
        zcv = _conv_fwd_val(zv, wv_ref, bv_ref, width)
        ge, dge = _gelu_parts(zcg)
        dv = d_ref[...]
        dx, dws, dbias = _conv_bwd_val(dv * zcv * dge, zg, wg_ref, width)
        dzg_ref[...] = dx.astype(BF16)
        for k in range(width):
            dwg_ref[pl.ds(k, 1), :] = dws[k]
        dbg_ref[...] = dbias
        dx, dws, dbias = _conv_bwd_val(dv * ge, zv, wv_ref, width)
        dzv_ref[...] = dx.astype(BF16)
        for k in range(width):
            dwv_ref[pl.ds(k, 1), :] = dws[k]
        dbv_ref[...] = dbias

    lo = lambda j: (0, j)
    hi = lambda j: (0, off + j)
    col = pl.BlockSpec((t, cb), lo)
    wsp = pl.BlockSpec((width, cb), lo)
    vsp = pl.BlockSpec((1, cb), lo)
    return pl.pallas_call(
        body, name=name, grid=(dff // cb,),
        in_specs=[col, pl.BlockSpec((t, cb), hi), col, wsp, pl.BlockSpec((width, cb), hi),
                  vsp, pl.BlockSpec((1, cb), hi)],
        out_specs=[col, col, wsp, wsp, vsp, vsp],
        out_shape=[jax.ShapeDtypeStruct((t, dff), BF16)] * 2
        + [jax.ShapeDtypeStruct((width, dff), F32)] * 2 + [jax.ShapeDtypeStruct((1, dff), F32)] * 2,
        compiler_params=_cp("parallel"))(z, z, dff_g, cw, cw, cbias, cbias)


def kv_fwd(z, fb, d_model, name):
    t, _ = z.shape
    blk = 2 * d_model // LANES

    def body(z_ref, fb_ref, c_ref, lf_s):
        v = z_ref[...] + fb_ref[...]
        lf_s[...] = -_softplus(-v)
        _scan_rows(None, lf_s, c_ref, t, LANES, reverse=False)

    return pl.pallas_call(
        body, name=name, grid=(1,),
        in_specs=[pl.BlockSpec((t, LANES), lambda j: (0, blk)), pl.BlockSpec((1, LANES), lambda j: (0, 0))],
        out_specs=pl.BlockSpec((t, LANES), lambda j: (0, 0)),
        out_shape=jax.ShapeDtypeStruct((t, LANES), F32),
        scratch_shapes=[pltpu.VMEM((t, LANES), F32)],
        compiler_params=_cp("arbitrary"))(z, fb)


def kv_bwd(dcs, z, fb, d_model, name):
    t, _ = z.shape
    blk = 2 * d_model // LANES
    n = len(dcs)

    def body(*refs):
        z_ref, fb_ref, dz_ref, dfb_ref, dc_s, dl_s = refs[n:]
        tot = refs[0][...]
        for r in refs[1:n]:
            tot = tot + r[...]
        dc_s[...] = tot
        _scan_rows(None, dc_s, dl_s, t, LANES, reverse=True)
        v = z_ref[...] + fb_ref[...]
        dz = dl_s[...] * _sigmoid(-v)
        dz_ref[...] = dz.astype(BF16)
        dfb_ref[...] = jnp.sum(dz, axis=0, keepdims=True)

    full = pl.BlockSpec((t, LANES), lambda j: (0, 0))
    return pl.pallas_call(
        body, name=name, grid=(1,),
        in_specs=[full] * n + [pl.BlockSpec((t, LANES), lambda j: (0, blk)),
                               pl.BlockSpec((1, LANES), lambda j: (0, 0))],
        out_specs=[full, pl.BlockSpec((1, LANES), lambda j: (0, 0))],
        out_shape=[jax.ShapeDtypeStruct((t, LANES), BF16), jax.ShapeDtypeStruct((1, LANES), F32)],
        scratch_shapes=[pltpu.VMEM((t, LANES), F32)] * 2,
        compiler_params=_cp("arbitrary"))(*dcs, z, fb)


def add_cast(a, b, name):
    t, d = a.shape
    cb = _tile(d, 512)

    def body(a_ref, b_ref, o_ref):
        o_ref[...] = (a_ref[...] + b_ref[...]).astype(BF16)

    col = pl.BlockSpec((t, cb), lambda j: (0, j))
    return pl.pallas_call(body, name=name, grid=(d // cb,), in_specs=[col, col], out_specs=col,
                          out_shape=jax.ShapeDtypeStruct((t, d), BF16),
                          compiler_params=_cp("parallel"))(a, b)


def _attn_geometry(t):
    nqb = 6 if t > 1024 else 2
    tp = _round_up(t, LANES * nqb)
    return nqb, tp, tp // nqb


def _attn_scales(dh):
    scale = dh ** -0.5
    if math.log2(scale).is_integer():
        return scale, 1.0
    return 1.0, scale


def _attn_pieces(qs, ks, crow, j, i, tq, dh, s_mul):
    r0 = i * tq
    lanes = pl.ds(j * dh, dh)
    qi = qs[pl.ds(r0, tq), lanes]
    spans = ([(0, r0)] if i > 0 else []) + [(r0, tq)]
    logits = []
    for k0, n in spans:
        s = lax.dot_general(qi, ks[pl.ds(k0, n), lanes], (((1,), (1,)), ((), ())),
                            preferred_element_type=F32)
        if s_mul != 1.0:
            s = s * s_mul
        s = s - crow[:, k0:k0 + n]
        if k0 == r0:
            rows = lax.broadcasted_iota(jnp.int32, (tq, tq), 0)
            cols = lax.broadcasted_iota(jnp.int32, (tq, tq), 1)
            s = jnp.where(cols <= rows, s, NEG_BIG)
        logits.append(s)
    mx = jnp.max(logits[0], axis=1, keepdims=True)
    for s in logits[1:]:
        mx = jnp.maximum(mx, jnp.max(s, axis=1, keepdims=True))
    es = [jnp.exp(s - mx) for s in logits]
    tot = jnp.sum(es[0], axis=1, keepdims=True)
    for e in es[1:]:
        tot = tot + jnp.sum(e, axis=1, keepdims=True)
    inv = 1.0 / tot
    return [(k0, n, e * inv) for (k0, n), e in zip(spans, es)], qi


def attn_fwd(qg, z, ct_pad, d_model, n_heads, name):
    t = qg.shape[0]
    dh = d_model // n_heads
    hp = LANES // dh
    nqb, tp, tq = _attn_geometry(t)
    nblk = d_model // LANES
    q_mul, s_mul = _attn_scales(dh)

    def body(q_ref, og_ref, k_ref, v_ref, ct_ref, o_ref, mo_ref, qs, ks, vs, os_):
        pad = jnp.zeros((tp - t, LANES), BF16)
        qs[pl.ds(0, t), :] = (q_ref[...] * q_mul).astype(BF16)
        qs[pl.ds(t, tp - t), :] = pad
        for src, dst in ((k_ref, ks), (v_ref, vs)):
            dst[pl.ds(0, t), :] = src[...].astype(BF16)
            dst[pl.ds(t, tp - t), :] = pad
        for j in range(hp):
            crow = ct_ref[j]
            lanes = pl.ds(j * dh, dh)
            for i in range(nqb):
                pieces, _ = _attn_pieces(qs, ks, crow, j, i, tq, dh, s_mul)
                acc = None
                for k0, n, p in pieces:
                    part = jnp.dot(p.astype(BF16), vs[pl.ds(k0, n), lanes], preferred_element_type=F32)
                    acc = part if acc is None else acc + part
                os_[pl.ds(i * tq, tq), lanes] = acc
        o = os_[pl.ds(0, t), :]
        o_ref[...] = o
        mo_ref[...] = (o * _sigmoid(og_ref[...])).astype(BF16)

    col = lambda off: pl.BlockSpec((t, LANES), lambda p: (0, off + p))
    return pl.pallas_call(
        body, name=name, grid=(nblk,),
        in_specs=[col(0), col(nblk), col(0), col(nblk), pl.BlockSpec((hp, 1, tp), lambda p: (p, 0, 0))],
        out_specs=[col(0), col(0)],
        out_shape=[jax.ShapeDtypeStruct((t, d_model), F32), jax.ShapeDtypeStruct((t, d_model), BF16)],
        scratch_shapes=[pltpu.VMEM((tp, LANES), BF16)] * 3 + [pltpu.VMEM((tp, LANES), F32)],
        compiler_params=_cp("parallel"))(qg, qg, z, z, ct_pad)


def attn_bwd(dmo, qg, z, o, ct_pad, d_model, n_heads, name):
    t = qg.shape[0]
    dh = d_model // n_heads
    hp = LANES // dh
    nqb, tp, tq = _attn_geometry(t)
    nblk = d_model // LANES
    q_mul, s_mul = _attn_scales(dh)
    scale = dh ** -0.5
    tn_dims = (((0,), (0,)), ((), ()))
    nt_dims = (((1,), (1,)), ((), ()))

    def body(dmo_ref, q_ref, og_ref, k_ref, v_ref, o_ref, ct_ref,
             dq_ref, dog_ref, dk_ref, dv_ref, dct_ref, qs, ks, vs, dos, dqs, dks, dvs):
        pad = jnp.zeros((tp - t, LANES), BF16)
        sg = _sigmoid(og_ref[...])
        dmo_v = dmo_ref[...]
        dog_ref[...] = (dmo_v * o_ref[...] * sg * (1.0 - sg)).astype(BF16)
        dos[pl.ds(0, t), :] = (dmo_v * sg).astype(BF16)
        dos[pl.ds(t, tp - t), :] = pad
        qs[pl.ds(0, t), :] = (q_ref[...] * q_mul).astype(BF16)
        qs[pl.ds(t, tp - t), :] = pad
        for src, dst in ((k_ref, ks), (v_ref, vs)):
            dst[pl.ds(0, t), :] = src[...].astype(BF16)
            dst[pl.ds(t, tp - t), :] = pad
        dks[...] = jnp.zeros_like(dks)
        dvs[...] = jnp.zeros_like(dvs)
        dct_ref[...] = jnp.zeros_like(dct_ref)
        for j in range(hp):
            crow = ct_ref[j]
            lanes = pl.ds(j * dh, dh)
            for i in range(nqb):
                pieces, qi = _attn_pieces(qs, ks, crow, j, i, tq, dh, s_mul)
                do_i = dos[pl.ds(i * tq, tq), lanes]
                dps = [lax.dot_general(do_i, vs[pl.ds(k0, n), lanes], nt_dims, preferred_element_type=F32)
                       for k0, n, _ in pieces]
                row = None
                for (_, _, p), dp in zip(pieces, dps):
                    part = jnp.sum(p * dp, axis=1, keepdims=True)
                    row = part if row is None else row + part
                dq_i = None
                for (k0, n, p), dp in zip(pieces, dps):
                    ds = p * (dp - row)
                    ds_b = ds.astype(BF16)
                    keys = pl.ds(k0, n)
                    part = jnp.dot(ds_b, ks[keys, lanes], preferred_element_type=F32)
                    dq_i = part if dq_i is None else dq_i + part
                    dks[keys, lanes] += lax.dot_general(ds_b, qi, tn_dims, preferred_element_type=F32) * s_mul
                    dvs[keys, lanes] += lax.dot_general(p.astype(BF16), do_i, tn_dims,
                                                        preferred_element_type=F32)
                    dct_ref[j, :, keys] -= jnp.sum(ds, axis=0, keepdims=True)
                dqs[pl.ds(i * tq, tq), lanes] = dq_i * scale
        dq_ref[...] = dqs[pl.ds(0, t), :].astype(BF16)
        dk_ref[...] = dks[pl.ds(0, t), :]
        dv_ref[...] = dvs[pl.ds(0, t), :]

    col = lambda off: pl.BlockSpec((t, LANES), lambda p: (0, off + p))
    big = lambda dt: jax.ShapeDtypeStruct((t, d_model), dt)
    return pl.pallas_call(
        body, name=name, grid=(nblk,),
        in_specs=[col(0), col(0), col(nblk), col(0), col(nblk), col(0),
                  pl.BlockSpec((hp, 1, tp), lambda p: (p, 0, 0))],
        out_specs=[col(0), col(0), col(0), col(0), pl.BlockSpec((hp, 1, tp), lambda p: (p, 0, 0))],
        out_shape=[big(BF16), big(BF16), big(F32), big(F32),
                   jax.ShapeDtypeStruct((n_heads, 1, tp), F32)],
        scratch_shapes=[pltpu.VMEM((tp, LANES), BF16)] * 4 + [pltpu.VMEM((tp, LANES), F32)] * 3,
        compiler_params=_cp("parallel"))(dmo, qg, qg, z, z, o, ct_pad)


def cast_into_slot(shard, w2d, name):
    r, c = w2d.shape
    rh = r // 2
    tr = _tile(rh, 512, 16)
    n = rh // tr

    def body(sh_ref, w_ref, o_ref):
        del sh_ref
        o_ref[...] = w_ref[...].astype(BF16)

    return pl.pallas_call(
        body, name=name,
        grid_spec=pltpu.PrefetchScalarGridSpec(
            num_scalar_prefetch=1, grid=(2, n),
            in_specs=[pl.BlockSpec((tr, c), lambda h, i, sh: (h * n + i, 0))],
            out_specs=pl.BlockSpec((None, None, tr, c), lambda h, i, sh: (sh[0], h, i, 0))),
        out_shape=jax.ShapeDtypeStruct((N_SHARDS, 2, rh, c), BF16),
        compiler_params=_cp("parallel", "parallel"))(shard, w2d)


def add_halves(core, g, other, name):
    s_n, _, rh, c = g.shape
    tr = _tile(rh, 512, 16)

    def body(core_ref, g_ref, o_ref, out_ref):
        del core_ref
        out_ref[...] = (g_ref[...].astype(F32) + o_ref[...].astype(F32)).astype(out_ref.dtype)

    return pl.pallas_call(
        body, name=name,
        grid_spec=pltpu.PrefetchScalarGridSpec(
            num_scalar_prefetch=1, grid=(s_n, rh // tr),
            in_specs=[pl.BlockSpec((None, None, tr, c), lambda s, i, cr: (s, cr[0], i, 0)),
                      pl.BlockSpec((None, tr, c), lambda s, i, cr: (s, i, 0))],
            out_specs=pl.BlockSpec((None, tr, c), lambda s, i, cr: (s, i, 0))),
        out_shape=jax.ShapeDtypeStruct((s_n, rh, c), g.dtype),
        compiler_params=_cp("parallel", "parallel"))(core, g, other)


def add_four(shard_core, csum, recv, buf, layer, name):
    _, rh, c = csum.shape
    tr = _tile(rh, 512, 16)

    def body(sc_ref, a_ref, r_ref, buf_ref, out_ref):
        del sc_ref, buf_ref
        acc = a_ref[...].astype(F32)
        for k in range(3):
            acc = acc + r_ref[k].astype(F32)
        out_ref[...] = acc

    return pl.pallas_call(
        body, name=name,
        grid_spec=pltpu.PrefetchScalarGridSpec(
            num_scalar_prefetch=1, grid=(rh // tr,),
            in_specs=[pl.BlockSpec((None, tr, c), lambda i, sc: (sc[0], i, 0)),
                      pl.BlockSpec((3, tr, c), lambda i, sc: (0, i, 0)),
                      pl.BlockSpec(memory_space=pl.ANY)],
            out_specs=pl.BlockSpec((None, None, tr, c), lambda i, sc: (layer, sc[1], i, 0))),
        out_shape=jax.ShapeDtypeStruct(buf.shape, F32),
        input_output_aliases={3: 0},
        compiler_params=_cp("parallel"))(shard_core, csum, recv, buf)


def adamw(w, g, m, v, name):
    r, c = w.shape
    tr = _tile(r, 512, SUBLANES)
    c1 = 1.0 - ADAM_B1 ** ADAM_STEP
    c2 = 1.0 - ADAM_B2 ** ADAM_STEP

    def body(w_ref, g_ref, m_ref, v_ref, d_ref, mo_ref, vo_ref):
        gv = g_ref[...]
        mn = ADAM_B1 * m_ref[...] + (1.0 - ADAM_B1) * gv
        vn = ADAM_B2 * v_ref[...] + (1.0 - ADAM_B2) * (gv * gv)
        m_hat = mn / c1
        v_hat = vn / c2
        d_ref[...] = -ADAM_LR * (m_hat / (jnp.sqrt(v_hat) + ADAM_EPS) + ADAM_WD * w_ref[...])
        mo_ref[...] = mn
        vo_ref[...] = vn

    blk = pl.BlockSpec((tr, c), lambda i: (i, 0))
    return pl.pallas_call(
        body, name=name, grid=(r // tr,), in_specs=[blk] * 4, out_specs=[blk] * 3,
        out_shape=[jax.ShapeDtypeStruct((r, c), F32)] * 3,
        compiler_params=_cp("parallel"))(w, g, m, v)


def _coords():
    return lax.axis_index("x"), lax.axis_index("y"), lax.axis_index("c")


def _exchange(name, ins, out_shapes, plan, in_place=False):
    n_in = len(ins)
    n_out = len(out_shapes)
    n_rem = len(plan([None] * n_in, [None] * n_out, True))

    def body(*refs):
        in_refs = refs[:n_in]
        out_refs = refs[n_in:n_in + n_out]
        send_sems, recv_sems = refs[n_in + n_out:]
        remote = plan(list(in_refs), list(out_refs), False)
        copies = [pltpu.make_async_remote_copy(
            src_ref=src, dst_ref=dst, send_sem=send_sems.at[q], recv_sem=recv_sems.at[q],
            device_id=peer, device_id_type=pl.DeviceIdType.MESH)
            for q, (src, dst, peer, _) in enumerate(remote)]
        waited = set()
        for q, (_, _, _, after) in enumerate(remote):
            if after is not None and after not in waited:
                copies[after].wait_recv()
                waited.add(after)
            copies[q].start()
        for q, cp in enumerate(copies):
            if q not in waited:
                cp.wait_recv()
        for cp in copies:
            cp.wait_send()

    hbm = pl.BlockSpec(memory_space=pl.ANY)
    return pl.pallas_call(
        body, name=name, in_specs=[hbm] * n_in, out_specs=[hbm] * n_out, out_shape=out_shapes,
        input_output_aliases={i: i for i in range(n_in)} if in_place else {},
        scratch_shapes=[pltpu.SemaphoreType.DMA((n_rem,)), pltpu.SemaphoreType.DMA((n_rem,))],
        compiler_params=pltpu.CompilerParams(has_side_effects=True))(*ins)


def _split_start(name, groups, plan):
    flat = [a for grp in groups for a in grp]
    n, n_grp = len(flat), len(groups)
    counts = [len(plan(g, [None] * len(grp), True)) for g, grp in enumerate(groups)]

    def body(*refs):
        ins, sems, token = refs[:n], refs[n:n + 2 * n_grp], refs[-1]
        pos = 0
        for g, grp in enumerate(groups):
            arrs = list(ins[pos:pos + len(grp)])
            pos += len(grp)
            for q, (src, dst, peer) in enumerate(plan(g, arrs, False)):
                pltpu.make_async_remote_copy(
                    src_ref=src, dst_ref=dst, send_sem=sems[2 * g].at[q], recv_sem=sems[2 * g + 1].at[q],
                    device_id=peer, device_id_type=pl.DeviceIdType.MESH).start()
        token[...] = jnp.zeros_like(token)

    hbm = pl.BlockSpec(memory_space=pltpu.HBM)
    sem = pl.BlockSpec(memory_space=pltpu.SEMAPHORE)
    outs = pl.pallas_call(
        body, name=name,
        out_shape=[pltpu.SemaphoreType.DMA((cnt,)) for cnt in counts for _ in range(2)]
        + [pltpu.HBM(a.shape, a.dtype) for a in flat] + [jax.ShapeDtypeStruct((SUBLANES, LANES), F32)],
        in_specs=[hbm] * n, out_specs=[sem] * (2 * n_grp) + [hbm] * n + [pl.BlockSpec(memory_space=pltpu.VMEM)],
        input_output_aliases={i: 2 * n_grp + i for i in range(n)},
        compiler_params=pltpu.CompilerParams(has_side_effects=pltpu.SideEffectType.DATAFLOW_SIDE_EFFECTING),
    )(*[pltpu.with_memory_space_constraint(a, pltpu.HBM) for a in flat])
    started, pos = [], 2 * n_grp
    for g, grp in enumerate(groups):
        started.append((outs[2 * g], outs[2 * g + 1], list(outs[pos:pos + len(grp)])))
        pos += len(grp)
    return started, outs[-1]


def _split_wait(name, started, after, plan_g):
    send_sems, recv_sems, arrs = started
    n = len(arrs)

    def body(*refs):
        ins, ssem, rsem = list(refs[:n]), refs[n], refs[n + 1]
        for q, (src, dst, peer) in enumerate(plan_g(ins, False)):
            cp = pltpu.make_async_remote_copy(
                src_ref=src, dst_ref=dst, send_sem=ssem.at[q], recv_sem=rsem.at[q],
                device_id=peer, device_id_type=pl.DeviceIdType.MESH)
            cp.wait_send()
            cp.wait_recv()

    hbm = pl.BlockSpec(memory_space=pltpu.HBM)
    sem = pl.BlockSpec(memory_space=pltpu.SEMAPHORE)
    return pl.pallas_call(
        body, name=name, out_shape=[pltpu.HBM(a.shape, a.dtype) for a in arrs],
        in_specs=[hbm] * n + [sem, sem, pl.BlockSpec(memory_space=pl.ANY)], out_specs=[hbm] * n,
        input_output_aliases={i: i for i in range(n)},
        compiler_params=pltpu.CompilerParams(has_side_effects=pltpu.SideEffectType.DATAFLOW_SIDE_EFFECTING),
    )(*arrs, send_sems, recv_sems, after)


def _gather_ici_plan(arrs, count_only):
    if count_only:
        return [None] * (3 * len(arrs))
    x, y, c = _coords()
    pushes = []
    for a in arrs:
        mine = a.at[2 * x + y, c]
        pushes += [(mine, mine, peer) for peer, _ in _other_chips(x, y, c)]
    return pushes


def _all_to_all_plan(arrs, count_only):
    half = len(arrs) // 2
    if count_only:
        return [None] * (3 * half)
    x, y, c = _coords()
    pushes = []
    for src, land in zip(arrs[:half], arrs[half:]):
        pushes += [(src.at[shard], land.at[k], peer) for k, (peer, shard) in enumerate(_other_chips(x, y, c))]
    return pushes


def forward_to_sibling(bufs, name):
    n = len(bufs)

    def plan(ins, outs, count_only):
        if count_only:
            return [None] * (3 * n)
        x, y, c = _coords()
        pushes = []
        for i in range(n):
            for _, src_shard in _other_chips(x, y, c):
                slab = outs[i].at[src_shard, c]
                pushes.append((slab, slab, (x, y, 1 - c), None))
        return pushes

    shapes = [jax.ShapeDtypeStruct(b.shape, b.dtype) for b in bufs]
    return _exchange(name, bufs, shapes, plan, in_place=True)


def _other_chips(x, y, c):
    return [((1 - x, y, c), 2 * (1 - x) + y), ((x, 1 - y, c), 2 * x + 1 - y),
            ((1 - x, 1 - y, c), 2 * (1 - x) + 1 - y)]


def gather_shards(bufs, name):
    n = len(bufs)

    def plan(ins, outs, count_only):
        if count_only:
            return [None] * (6 * n)
        x, y, c = _coords()
        me = 2 * x + y
        chips = _other_chips(x, y, c)
        remote = []
        for i in range(n):
            mine = outs[i].at[me, c]
            for peer, _ in chips:
                remote.append((mine, mine, peer, None))
        for i in range(n):
            for k, (_, src_shard) in enumerate(chips):
                slab = outs[i].at[src_shard, c]
                remote.append((slab, slab, (x, y, 1 - c), 3 * i + k))
        return remote

    shapes = [jax.ShapeDtypeStruct(b.shape, b.dtype) for b in bufs]
    return _exchange(name, bufs, shapes, plan, in_place=True)


def swap_halves(grads, name):
    n = len(grads)
    shapes = [jax.ShapeDtypeStruct((g.shape[0],) + g.shape[2:], g.dtype) for g in grads]

    def plan(ins, outs, count_only):
        if count_only:
            return [None] * (n * N_SHARDS)
        x, y, c = _coords()
        remote = []
        for i in range(n):
            for s in range(N_SHARDS):
                remote.append((ins[i].at[s, 1 - c], outs[i].at[s], (x, y, 1 - c), None))
        return remote

    return _exchange(name, grads, shapes, plan)


def chip_all_to_all(csums, name):
    n = len(csums)
    shapes = [jax.ShapeDtypeStruct((3,) + g.shape[1:], g.dtype) for g in csums]

    def plan(ins, outs, count_only):
        if count_only:
            return [None] * (3 * n)
        x, y, c = _coords()
        remote = []
        for i in range(n):
            for k, (peer, shard) in enumerate(_other_chips(x, y, c)):
                remote.append((ins[i].at[shard], outs[i].at[k], peer, None))
        return remote

    return _exchange(name, csums, shapes, plan)


def join_halves(bufs, name):
    slots = [(i, l) for i, b in enumerate(bufs) for l in range(b.shape[0])]

    def plan(ins, outs, count_only):
        if count_only:
            return [None] * len(slots)
        x, y, c = _coords()
        return [(outs[i].at[l, c], outs[i].at[l, c], (x, y, 1 - c), None) for i, l in slots]

    shapes = [jax.ShapeDtypeStruct(b.shape, b.dtype) for b in bufs]
    return _exchange(name, bufs, shapes, plan, in_place=True)


def gather_full(buf, name):
    def plan(ins, outs, count_only):
        if count_only:
            return [None] * 3
        x, y, c = _coords()
        mine = outs[0].at[2 * x + y]
        return [(mine, mine, peer, None) for peer, _ in _other_chips(x, y, c)]

    return _exchange(name, [buf], [jax.ShapeDtypeStruct(buf.shape, buf.dtype)], plan, in_place=True)[0]


def _pack(arrays, multiple):
    flat = jnp.concatenate([a.reshape(-1) for a in arrays])
    n = flat.shape[0]
    return jnp.pad(flat, (0, _round_up(n, multiple) - n))


def _unpack(flat, shapes):
    out, pos = [], 0
    for shp in shapes:
        n = math.prod(shp)
        out.append(flat[pos:pos + n].reshape(shp))
        pos += n
    return out


def _block_diag(w, per_group):
    nb, bs, _ = w.shape
    g = nb // per_group
    w4 = w.reshape(g, per_group, bs, bs)
    eye = jnp.eye(per_group, dtype=w.dtype)
    full = w4[:, :, :, None, :] * eye[None, :, None, :, None]
    return full.reshape(g, per_group * bs, per_group * bs).astype(BF16)


def _block_diag_extract(full, per_group, bs):
    g = full.shape[0]
    f5 = full.reshape(g, per_group, bs, per_group, bs)
    idx = jnp.arange(per_group)
    picked = f5[:, idx, :, idx, :]
    return jnp.moveaxis(picked, 0, 1).reshape(g * per_group, bs, bs)


def kernel(x, meta, a_w_in, a_conv_w, a_conv_b, a_w_r, a_b_r, a_w_i, a_b_i, a_lambda, a_w_out, kv_w, kv_f_b, b_w_in, b_w_out, f_w_in, f_conv_w, f_conv_b, f_w_out, ln1_g, ln1_b, ln2_g, ln2_b, loss_target, m_meta, m_a_w_in, m_a_conv_w, m_a_conv_b, m_a_w_r, m_a_b_r, m_a_w_i, m_a_b_i, m_a_lambda, m_a_w_out, m_kv_w, m_kv_f_b, m_b_w_in, m_b_w_out, m_f_w_in, m_f_conv_w, m_f_conv_b, m_f_w_out, m_ln1_g, m_ln1_b, m_ln2_g, m_ln2_b, v_meta, v_a_w_in, v_a_conv_w, v_a_conv_b, v_a_w_r, v_a_b_r, v_a_w_i, v_a_b_i, v_a_lambda, v_a_w_out, v_kv_w, v_kv_f_b, v_b_w_in, v_b_w_out, v_f_w_in, v_f_conv_w, v_f_conv_b, v_f_w_out, v_ln1_g, v_ln1_b, v_ln2_g, v_ln2_b):
    weights = dict(meta=meta, a_w_in=a_w_in, a_conv_w=a_conv_w, a_conv_b=a_conv_b, a_w_r=a_w_r, a_b_r=a_b_r,
                   a_w_i=a_w_i, a_b_i=a_b_i, a_lambda=a_lambda, a_w_out=a_w_out, kv_w=kv_w, kv_f_b=kv_f_b,
                   b_w_in=b_w_in, b_w_out=b_w_out, f_w_in=f_w_in, f_conv_w=f_conv_w, f_conv_b=f_conv_b,
                   f_w_out=f_w_out, ln1_g=ln1_g, ln1_b=ln1_b, ln2_g=ln2_g, ln2_b=ln2_b)
    mom_m = dict(meta=m_meta, a_w_in=m_a_w_in, a_conv_w=m_a_conv_w, a_conv_b=m_a_conv_b, a_w_r=m_a_w_r,
                 a_b_r=m_a_b_r, a_w_i=m_a_w_i, a_b_i=m_a_b_i, a_lambda=m_a_lambda, a_w_out=m_a_w_out,
                 kv_w=m_kv_w, kv_f_b=m_kv_f_b, b_w_in=m_b_w_in, b_w_out=m_b_w_out, f_w_in=m_f_w_in,
                 f_conv_w=m_f_conv_w, f_conv_b=m_f_conv_b, f_w_out=m_f_w_out, ln1_g=m_ln1_g, ln1_b=m_ln1_b,
                 ln2_g=m_ln2_g, ln2_b=m_ln2_b)
    mom_v = dict(meta=v_meta, a_w_in=v_a_w_in, a_conv_w=v_a_conv_w, a_conv_b=v_a_conv_b, a_w_r=v_a_w_r,
                 a_b_r=v_a_b_r, a_w_i=v_a_w_i, a_b_i=v_a_b_i, a_lambda=v_a_lambda, a_w_out=v_a_w_out,
                 kv_w=v_kv_w, kv_f_b=v_kv_f_b, b_w_in=v_b_w_in, b_w_out=v_b_w_out, f_w_in=v_f_w_in,
                 f_conv_w=v_f_conv_w, f_conv_b=v_f_conv_b, f_w_out=v_f_w_out, ln1_g=v_ln1_g, ln1_b=v_ln1_b,
                 ln2_g=v_ln2_g, ln2_b=v_ln2_b)
    return _train_step(x, loss_target, weights, mom_m, mom_v)


WEIGHT_ORDER = ("meta", "a_w_in", "a_conv_w", "a_conv_b", "a_w_r", "a_b_r", "a_w_i", "a_b_i", "a_lambda",
                "a_w_out", "kv_w", "kv_f_b", "b_w_in", "b_w_out", "f_w_in", "f_conv_w", "f_conv_b",
                "f_w_out", "ln1_g", "ln1_b", "ln2_g", "ln2_b")
BIG = ("a_w_in", "a_w_out", "kv_w", "b_w_in", "b_w_out", "f_w_in", "f_w_out")
OUT_TYPE = ("a_w_out", "b_w_out", "f_w_out")
SMALL_SHARDED = (("meta", 1), ("a_conv_w", 2), ("a_conv_b", 1), ("a_b_r", 1), ("a_b_i", 1), ("a_lambda", 1),
                 ("f_conv_w", 2))
SMALL_REPLICATED = ("a_w_r", "a_w_i", "kv_f_b", "f_conv_b", "ln1_g", "ln1_b", "ln2_g", "ln2_b")


def _train_step(x, loss_target, weights, mom_m, mom_v):
    S = N_SHARDS
    seq, d = x.shape[1], x.shape[2]
    nm = weights["meta"].shape[0]
    la = weights["a_w_in"].shape[0]
    lb = weights["b_w_in"].shape[0]
    depth = la + lb
    dr = weights["a_w_out"].shape[1] * S
    nb, bs = weights["a_w_r"].shape[1], weights["a_w_r"].shape[2]
    per_group = (LANES // math.gcd(bs, LANES))
    gs = per_group * bs
    heads = weights["kv_f_b"].shape[0]
    dff = weights["f_w_out"].shape[1] * S
    nkv = 2 * d + heads
    nkv_s = weights["kv_w"].shape[1]
    nkvp = _round_up(2 * d + LANES, 768) if 2 * d + LANES > 768 else 2 * d + LANES
    alpha = (2 * depth) ** 0.25
    xi, yi, ci = _coords()
    shard = 2 * xi + yi
    core_arr = jnp.reshape(ci, (1,)).astype(jnp.int32)
    shard_arr = jnp.reshape(shard, (1,)).astype(jnp.int32)
    shard_core_arr = jnp.stack([shard, ci]).astype(jnp.int32)

    def mixer_keys(l):
        if l < la:
            return [("a_w_in", l), ("a_w_out", l)]
        return ([("kv_w", 0)] if l == la else []) + [("b_w_in", l - la), ("b_w_out", l - la)]

    def ffn_keys(l):
        return [("f_w_in", l), ("f_w_out", l)]

    groups = [mixer_keys(0), ffn_keys(0)] + [mixer_keys(l) + ffn_keys(l) for l in range(1, depth)]
    keys = [kl for grp in groups for kl in grp]
    local2d = {(k, i): (weights[k][i] if weights[k].ndim == 3 else weights[k]) for k, i in keys}
    small_local = [weights[k] for k, _ in SMALL_SHARDED]
    sm_flat = _pack(small_local, 2 * SUBLANES * LANES).reshape(1, 2, -1, LANES)
    sm_slot = lax.dynamic_update_slice_in_dim(lax.empty((S,) + sm_flat.shape[1:], F32), sm_flat, shard, axis=0)
    parts = [[cast_into_slot(shard_arr, local2d[kl], f"cast_{kl[0]}{kl[1]}") for kl in grp] for grp in groups]
    parts[0].append(sm_slot)
    in_flight, start_token = _split_start("gather_start", parts, lambda g, refs, cnt: _gather_ici_plan(refs, cnt))
    gw = {}

    def fetch(g, after):
        arrs = _split_wait(f"gather_wait_{g}", in_flight[g], after, _gather_ici_plan)
        arrs = forward_to_sibling(arrs, f"gather_fwd_{g}")
        for kl, a in zip(groups[g], arrs):
            rows, cols = local2d[kl].shape
            gw[kl] = a.reshape(S * rows, cols) if kl[0] in OUT_TYPE else a.reshape(S, rows, cols)
        return arrs

    sm_all = fetch(0, start_token)[-1].reshape(S, -1)
    small_full = {}
    per_shard = [_unpack(sm_all[s], [a.shape for a in small_local]) for s in range(S)]
    for idx, (k, axis) in enumerate(SMALL_SHARDED):
        small_full[k] = jnp.concatenate([per_shard[s][idx] for s in range(S)], axis=axis)
    fb_pad = jnp.pad(weights["kv_f_b"], (0, LANES - heads)).reshape(1, LANES)
    wr_g = [_block_diag(weights["a_w_r"][l], per_group) for l in range(la)]
    wi_g = [_block_diag(weights["a_w_i"][l], per_group) for l in range(la)]
    row = lambda v: v.reshape(1, -1)

    h, hb = embed_fwd(small_full["meta"], x[0], "embed")
    saved = []
    kvz = ct_pad = None
    _, tp, _ = _attn_geometry(nm + seq)
    t = nm + seq
    for l in range(depth):
        sv = {"hb_in": hb}
        if l > 0:
            fetch(l + 1, hb)
        if l < la:
            gr = mm_in(hb, gw[("a_w_in", l)], F32, f"a{l}_in")
            rc, rcb = a_conv_fwd(gr, small_full["a_conv_w"][l], row(small_full["a_conv_b"][l]), f"a{l}_conv")
            r_pre, i_pre = mm_bd(rcb, wr_g[l], wi_g[l], f"a{l}_gates")
            hs, gb = a_elem_fwd(gr, rc, r_pre, i_pre, row(small_full["a_b_r"][l]), row(small_full["a_b_i"][l]),
                                row(small_full["a_lambda"][l]), f"a{l}_lru")
            mix = mm_out(gb, gw[("a_w_out", l)], f"a{l}_out")
            sv.update(gr=gr, rc=rc, rcb=rcb, r_pre=r_pre, i_pre=i_pre, hs=hs, gb=gb)
        else:
            j = l - la
            if j == 0:
                kv_cat = jnp.moveaxis(gw[("kv_w", 0)], 0, 1).reshape(d, S * nkv_s)
                kv_pad = jnp.pad(kv_cat, ((0, 0), (0, nkvp - nkv))).reshape(1, d, nkvp)
                kvz = mm_in(hb, kv_pad, F32, "kv_proj")
                cum = kv_fwd(kvz, fb_pad, d, "kv_forget")
                ct_pad = jnp.pad(cum[:, :heads].T, ((0, 0), (0, tp - t))).reshape(heads, 1, tp)
                kv_hb = hb
            qg = mm_in(hb, gw[("b_w_in", j)], F32, f"b{j}_in")
            o, mob = attn_fwd(qg, kvz, ct_pad, d, heads, f"b{j}_attn")
            mix = mm_out(mob, gw[("b_w_out", j)], f"b{j}_out")
            sv.update(qg=qg, o=o, mob=mob)
        h1, h1b, xh1, rs1 = ln_fwd(h, mix, row(weights["ln1_g"][l]), row(weights["ln1_b"][l]), alpha, f"ln1_{l}")
        if l == 0:
            fetch(1, h1b)
        zf = mm_in(h1b, gw[("f_w_in", l)], F32, f"f{l}_in")
        ffb = f_elem_fwd(zf, small_full["f_conv_w"][l], row(weights["f_conv_b"][l]), f"f{l}_act")
        ffo = mm_out(ffb, gw[("f_w_out", l)], f"f{l}_out")
        h2, h2b, xh2, rs2 = ln_fwd(h1, ffo, row(weights["ln2_g"][l]), row(weights["ln2_b"][l]), alpha, f"ln2_{l}")
        sv.update(h1b=h1b, xh1=xh1, rs1=rs1, zf=zf, ffb=ffb, xh2=xh2, rs2=rs2)
        saved.append(sv)
        h, hb = h2, h2b
    loss11, dy = loss_fwd_bwd(h, loss_target[0], nm, "loss")

    grads = {}

    def by_owner(kl, g3):
        rows, cols = local2d[kl].shape
        grads[kl] = g3.reshape(S, 2, rows // 2, cols)

    reducing = []

    def send_grads(g, names, arrays):
        from_sib = swap_halves(arrays, f"grad_swap_{g}")
        csums = [add_halves(core_arr, a, o, f"chip_sum_{g}_{i}") for i, (a, o) in enumerate(zip(arrays, from_sib))]
        lands = [lax.empty((3,) + cs.shape[1:], cs.dtype) for cs in csums]
        started, token = _split_start(f"grad_a2a_start_{g}", [csums + lands],
                                      lambda _, refs, cnt: _all_to_all_plan(refs, cnt))
        reducing.append((names, started[0]))
        return token[0:1, 0:1]

    def after_start(vec, zero):
        return vec if zero is None else vec + zero

    pin = None

    g_small = {}
    per_layer = {k: [None] * n for k, n in (
        ("a_conv_w", la), ("a_conv_b", la), ("a_w_r", la), ("a_b_r", la), ("a_w_i", la), ("a_b_i", la),
        ("a_lambda", la), ("f_conv_w", depth), ("f_conv_b", depth), ("ln1_g", depth), ("ln1_b", depth),
        ("ln2_g", depth), ("ln2_b", depth))}
    adds = [(dy, 1.0)]
    dks, dvs, dcs = [], [], []
    for l in reversed(range(depth)):
        sv = saved[l]
        ds2, ds2b, dg2, db2 = ln_bwd(adds, sv["xh2"], sv["rs2"], after_start(row(weights["ln2_g"][l]), pin),
                                     f"ln2_{l}_bwd")
        pin = None
        per_layer["ln2_g"][l], per_layer["ln2_b"][l] = dg2[0], db2[0]
        dff_v = mm_out_nt(ds2b, gw[("f_w_out", l)], f"f{l}_out_dx")
        by_owner(("f_w_out", l), mm_tn(sv["ffb"], ds2b, 1, f"f{l}_out_dw"))
        dzg, dzv, dwg, dwv, dbg, dbv = f_elem_bwd(sv["zf"], dff_v, small_full["f_conv_w"][l],
                                                  row(weights["f_conv_b"][l]), f"f{l}_act_bwd")
        dzb = jnp.concatenate([dzg, dzv], axis=1)
        per_layer["f_conv_w"][l] = jnp.concatenate([dwg, dwv], axis=1)
        per_layer["f_conv_b"][l] = jnp.concatenate([dbg, dbv], axis=1)[0]
        dh1_f = mm_in_nt(dzb, gw[("f_w_in", l)], f"f{l}_in_dx")
        by_owner(("f_w_in", l), mm_tn(sv["h1b"], dzb, S, f"f{l}_in_dw"))
        if l == 0:
            pin = send_grads(1, groups[1], [grads[kl] for kl in groups[1]])
        ds1, ds1b, dg1, db1 = ln_bwd([(ds2, alpha), (dh1_f, 1.0)], sv["xh1"], sv["rs1"],
                                     after_start(row(weights["ln1_g"][l]), pin), f"ln1_{l}_bwd")
        pin = None
        per_layer["ln1_g"][l], per_layer["ln1_b"][l] = dg1[0], db1[0]
        if l < la:
            dgv = mm_out_nt(ds1b, gw[("a_w_out", l)], f"a{l}_out_dx")
            by_owner(("a_w_out", l), mm_tn(sv["gb"], ds1b, 1, f"a{l}_out_dw"))
            dgate_b, drp_b, dip_b, drc_d, dlam, dbr, dbi = a_elem_bwd(
                dgv, sv["gr"], sv["rc"], sv["r_pre"], sv["i_pre"], sv["hs"], row(small_full["a_b_r"][l]),
                row(small_full["a_b_i"][l]), row(small_full["a_lambda"][l]), f"a{l}_lru_bwd")
            drc_g = mm_bd_nt(drp_b, dip_b, wr_g[l], wi_g[l], f"a{l}_gates_dx")
            dwr_g, dwi_g = mm_bd_tn(sv["rcb"], drp_b, dip_b, gs, f"a{l}_gates_dw")
            drec_b, dcw, dcb = a_conv_bwd(drc_d, drc_g, sv["gr"], small_full["a_conv_w"][l], f"a{l}_conv_bwd")
            per_layer["a_w_r"][l] = _block_diag_extract(dwr_g, per_group, bs)
            per_layer["a_w_i"][l] = _block_diag_extract(dwi_g, per_group, bs)
            per_layer["a_lambda"][l], per_layer["a_b_r"][l], per_layer["a_b_i"][l] = dlam[0], dbr[0], dbi[0]
            per_layer["a_conv_w"][l], per_layer["a_conv_b"][l] = dcw, dcb[0]
            dgr_b = jnp.concatenate([dgate_b, drec_b], axis=1)
            dh_m = mm_in_nt(dgr_b, gw[("a_w_in", l)], f"a{l}_in_dx")
            by_owner(("a_w_in", l), mm_tn(sv["hb_in"], dgr_b, S, f"a{l}_in_dw"))
        else:
            j = l - la
            dmo = mm_out_nt(ds1b, gw[("b_w_out", j)], f"b{j}_out_dx")
            by_owner(("b_w_out", j), mm_tn(sv["mob"], ds1b, 1, f"b{j}_out_dw"))
            dq_b, dog_b, dk, dv, dct = attn_bwd(dmo, sv["qg"], kvz, sv["o"], ct_pad, d, heads, f"b{j}_attn_bwd")
            dks.append(dk)
            dvs.append(dv)
            dcs.append(jnp.pad(dct[:, 0, :t].T, ((0, 0), (0, LANES - heads))))
            dqg_b = jnp.concatenate([dq_b, dog_b], axis=1)
            dh_m = mm_in_nt(dqg_b, gw[("b_w_in", j)], f"b{j}_in_dx")
            by_owner(("b_w_in", j), mm_tn(sv["hb_in"], dqg_b, S, f"b{j}_in_dw"))
        adds = [(ds1, alpha), (dh_m, 1.0)]
        if l == la:
            dzf_b, dfb = kv_bwd(dcs, kvz, fb_pad, d, "kv_forget_bwd")
            dk_b = add_cast(dks[0], dks[1], "kv_dk") if lb == 2 else None
            dv_b = add_cast(dvs[0], dvs[1], "kv_dv") if lb == 2 else None
            dz_kv = jnp.concatenate([dk_b, dv_b, dzf_b, jnp.zeros((t, nkvp - 2 * d - LANES), BF16)], axis=1)
            dh_kv = mm_in_nt(dz_kv, kv_pad, "kv_proj_dx")
            kv_dw = mm_tn(kv_hb, dz_kv, 1, "kv_proj_dw")
            by_owner(("kv_w", 0), jnp.moveaxis(kv_dw[0, :, :nkv].reshape(d, S, nkv_s), 1, 0))
            g_small["kv_f_b"] = dfb[0, :heads]
            adds.append((dh_kv, 1.0))
        if l > 0:
            pin = send_grads(l + 1, groups[l + 1], [grads[kl] for kl in groups[l + 1]])
    g_meta, g_x = embed_bwd(adds, nm, "embed_bwd")

    small_names = list(SMALL_REPLICATED) + [k for k, _ in SMALL_SHARDED]
    g_small["meta"] = g_meta
    for k, vals in per_layer.items():
        g_small[k] = jnp.stack(vals)
    small_shapes = {k: (weights[k].shape if k in SMALL_REPLICATED else g_small[k].shape) for k in small_names}
    sm_g = _pack([g_small[k].reshape(small_shapes[k]) for k in small_names], S * 2 * SUBLANES * LANES)
    sm_g = sm_g.reshape(S, 2, -1, LANES)
    send_grads(0, groups[0] + [("small", 0)], [grads[kl] for kl in groups[0]] + [sm_g])

    fin = {"small": lax.empty((1,) + sm_g.shape[1:], F32)}
    for kl in keys:
        n_stack = weights[kl[0]].shape[0] if weights[kl[0]].ndim == 3 else 1
        rows, cols = local2d[kl].shape
        fin.setdefault(kl[0], lax.empty((n_stack, 2, rows // 2, cols), F32))
    for g, (names_g, started) in enumerate(reducing):
        arrs = _split_wait(f"grad_a2a_wait_{g}", started, g_x, _all_to_all_plan)
        half = len(names_g)
        for i, (kl, cs, rv) in enumerate(zip(names_g, arrs[:half], arrs[half:])):
            fin[kl[0]] = add_four(shard_core_arr, cs, rv, fin[kl[0]], kl[1], f"owner_sum_{g}_{i}")
    names = list(BIG) + ["small"]
    joined = dict(zip(names, join_halves([fin[k] for k in names], "grad_join")))
    sm_slot = lax.dynamic_update_slice_in_dim(lax.empty((S,) + joined["small"].shape[1:], F32), joined["small"],
                                              shard, axis=0)
    sm_red = gather_full(sm_slot, "small_gather").reshape(-1)

    out_g, out_d, out_m, out_v = {}, {}, {}, {}
    for k in BIG:
        w2 = weights[k].reshape(-1, weights[k].shape[-1])
        g2 = joined[k].reshape(w2.shape)
        dlt, mn, vn = adamw(w2, g2, mom_m[k].reshape(w2.shape), mom_v[k].reshape(w2.shape), "adamw_" + k)
        shp = weights[k].shape
        out_g[k], out_d[k], out_m[k], out_v[k] = g2.reshape(shp), dlt.reshape(shp), mn.reshape(shp), vn.reshape(shp)
    sm_vals = dict(zip(small_names, _unpack(sm_red, [small_shapes[k] for k in small_names])))
    local_small = {}
    for k in SMALL_REPLICATED:
        local_small[k] = sm_vals[k]
    for k, axis in SMALL_SHARDED:
        size = weights[k].shape[axis]
        local_small[k] = lax.dynamic_slice_in_dim(sm_vals[k], shard * size, size, axis=axis)
    mult = 512 * LANES
    pk = lambda src: _pack([src[k] for k in small_names], mult).reshape(-1, LANES)
    dlt, mn, vn = adamw(pk(weights), pk(local_small), pk(mom_m), pk(mom_v), "adamw_small")
    shapes_local = [weights[k].shape for k in small_names]
    for dst, packed in ((out_d, dlt), (out_m, mn), (out_v, vn)):
        for k, val in zip(small_names, _unpack(packed.reshape(-1), shapes_local)):
            dst[k] = val
    for k in small_names:
        out_g[k] = local_small[k]

    loss = lax.psum(loss11[0, 0], ("x", "y", "c"))
    return (loss, g_x[None], *[out_g[k] for k in WEIGHT_ORDER], *[out_d[k] for k in WEIGHT_ORDER],
            *[out_m[k] for k in WEIGHT_ORDER], *[out_v[k] for k in WEIGHT_ORDER])
```

```python
import functools
import math

import jax
import jax.numpy as jnp
from jax import lax
from jax.experimental import pallas as pl
from jax.experimental.pallas import tpu as pltpu

F32 = jnp.float32
BF16 = jnp.bfloat16

LRU_C = 8.0
LN_EPS = 1e-5
ADAM_LR = 0.001
ADAM_B1 = 0.9
ADAM_B2 = 0.999
ADAM_EPS = 1e-08
ADAM_WD = 0.01
ADAM_STEP = 10

LANES = 128
SUBLANES = 8
V7X_VMEM_BYTES = 64 * 1024 * 1024
VMEM_LIMIT = V7X_VMEM_BYTES * 7 // 8
N_SHARDS = 4
GELU_C0 = math.sqrt(2.0 / math.pi)
GELU_C1 = 0.044715
NEG_BIG = -1e30


def _cp(*sem):
    return pltpu.CompilerParams(dimension_semantics=tuple(sem), vmem_limit_bytes=VMEM_LIMIT)


def _tile(n, cap, mult=LANES):
    best = None
    d = mult
    while d <= min(n, cap):
        if n % d == 0:
            best = d
        d += mult
    return n if best is None else best


def _row_block(t):
    if t % 3 == 0 and (t // 3) % 16 == 0:
        return t // 3
    return t


def _round_up(n, m):
    return (n + m - 1) // m * m


def _sigmoid(v):
    return 1.0 / (1.0 + jnp.exp(-v))


def _softplus(v):
    return jnp.maximum(v, 0.0) + jnp.log(1.0 + jnp.exp(-jnp.abs(v)))


def _gelu_parts(v):
    v2 = v * v
    u = GELU_C0 * (v + GELU_C1 * v * v2)
    t = jnp.tanh(u)
    g = 0.5 * v * (1.0 + t)
    dg = 0.5 * (1.0 + t) + 0.5 * v * (1.0 - t * t) * (GELU_C0 * (1.0 + 3.0 * GELU_C1 * v2))
    return g, dg


def _gelu(v):
    u = GELU_C0 * (v + GELU_C1 * v * v * v)
    return 0.5 * v * (1.0 + jnp.tanh(u))


def _neg_expm1(v):
    series = -v * (1.0 + 0.5 * v * (1.0 + (v / 3.0) * (1.0 + 0.25 * v)))
    return jnp.where(v > -0.05, series, 1.0 - jnp.exp(v))


def _shift_down(v, j):
    if j == 0:
        return v
    rows = lax.broadcasted_iota(jnp.int32, v.shape, 0)
    return jnp.where(rows >= j, pltpu.roll(v, j, 0), 0.0)


def _shift_up(v, j):
    if j == 0:
        return v
    n = v.shape[0]
    rows = lax.broadcasted_iota(jnp.int32, v.shape, 0)
    return jnp.where(rows < n - j, pltpu.roll(v, n - j, 0), 0.0)


def _scan_rows(a_ref, b_ref, out_ref, n_rows, width, reverse):
    n_groups = n_rows // SUBLANES
    rows = lax.broadcasted_iota(jnp.int32, (SUBLANES, width), 0)
    edge = 0 if reverse else SUBLANES - 1

    def body(g, carry):
        grp = (n_groups - 1 - g) if reverse else g
        off = pl.multiple_of(grp * SUBLANES, SUBLANES)
        b = b_ref[pl.ds(off, SUBLANES), :]
        a = None if a_ref is None else a_ref[pl.ds(off, SUBLANES), :]
        for d in (1, 2, 4):
            if reverse:
                keep = rows < SUBLANES - d
                sh = SUBLANES - d
            else:
                keep = rows >= d
                sh = d
            b_s = jnp.where(keep, pltpu.roll(b, sh, 0), 0.0)
            if a is None:
                b = b + b_s
            else:
                a_s = jnp.where(keep, pltpu.roll(a, sh, 0), 1.0)
                b = a * b_s + b
                a = a * a_s
        h = b + carry if a is None else b + a * carry
        out_ref[pl.ds(off, SUBLANES), :] = h
        return jnp.sum(jnp.where(rows == edge, h, 0.0), axis=0, keepdims=True)

    lax.fori_loop(0, n_groups, body, jnp.zeros((1, width), F32), unroll=2)


def mm_in(x, w, out_dtype, name):
    t, k = x.shape
    s_n, _, ns = w.shape
    tn = _tile(ns, 1408)
    nj = ns // tn
    rb = _row_block(t)

    def body(x_ref, w_ref, o_ref):
        o_ref[...] = jnp.dot(x_ref[...], w_ref[...], preferred_element_type=F32).astype(o_ref.dtype)

    return pl.pallas_call(
        body, name=name, grid=(s_n, nj, t // rb),
        in_specs=[pl.BlockSpec((rb, k), lambda s, j, r: (r, 0)),
                  pl.BlockSpec((None, k, tn), lambda s, j, r: (s, 0, j))],
        out_specs=pl.BlockSpec((rb, tn), lambda s, j, r: (r, s * nj + j)),
        out_shape=jax.ShapeDtypeStruct((t, s_n * ns), out_dtype),
        compiler_params=_cp("parallel", "parallel", "parallel"))(x, w)


def mm_out(x, w, name):
    t, k = x.shape
    n = w.shape[1]
    rb = _row_block(t)

    def body(x_ref, w_ref, o_ref):
        o_ref[...] = jnp.dot(x_ref[...], w_ref[...], preferred_element_type=F32)

    return pl.pallas_call(
        body, name=name, grid=(t // rb,),
        in_specs=[pl.BlockSpec((rb, k), lambda r: (r, 0)), pl.BlockSpec((k, n), lambda r: (0, 0))],
        out_specs=pl.BlockSpec((rb, n), lambda r: (r, 0)),
        out_shape=jax.ShapeDtypeStruct((t, n), F32),
        compiler_params=_cp("parallel"))(x, w)


def mm_in_nt(dy, w, name):
    t, _ = dy.shape
    s_n, k, ns = w.shape
    tn = _tile(ns, 1408)
    nj = ns // tn
    rb = _row_block(t)

    def body(dy_ref, w_ref, o_ref):
        @pl.when((pl.program_id(1) == 0) & (pl.program_id(2) == 0))
        def _():
            o_ref[...] = jnp.zeros_like(o_ref)
        o_ref[...] += lax.dot_general(dy_ref[...], w_ref[...], (((1,), (1,)), ((), ())),
                                      preferred_element_type=F32)

    return pl.pallas_call(
        body, name=name, grid=(t // rb, s_n, nj),
        in_specs=[pl.BlockSpec((rb, tn), lambda r, s, j: (r, s * nj + j)),
                  pl.BlockSpec((None, k, tn), lambda r, s, j: (s, 0, j))],
        out_specs=pl.BlockSpec((rb, k), lambda r, s, j: (r, 0)),
        out_shape=jax.ShapeDtypeStruct((t, k), F32),
        compiler_params=_cp("parallel", "arbitrary", "arbitrary"))(dy, w)


def mm_out_nt(dy, w, name):
    t, n = dy.shape
    k = w.shape[0]
    rb = _row_block(t)

    def body(dy_ref, w_ref, o_ref):
        o_ref[...] = lax.dot_general(dy_ref[...], w_ref[...], (((1,), (1,)), ((), ())),
                                     preferred_element_type=F32)

    return pl.pallas_call(
        body, name=name, grid=(t // rb,),
        in_specs=[pl.BlockSpec((rb, n), lambda r: (r, 0)), pl.BlockSpec((k, n), lambda r: (0, 0))],
        out_specs=pl.BlockSpec((rb, k), lambda r: (r, 0)),
        out_shape=jax.ShapeDtypeStruct((t, k), F32),
        compiler_params=_cp("parallel"))(dy, w)


def mm_tn(x, dy, s_n, name):
    t, kb = x.shape
    nb = dy.shape[1] // s_n
    tk = _tile(kb, 1408)
    tn = _tile(nb, 1408)
    nkb, nnb = kb // tk, nb // tn

    def body(x_ref, dy_ref, o_ref):
        o_ref[...] = lax.dot_general(x_ref[...], dy_ref[...], (((0,), (0,)), ((), ())),
                                     preferred_element_type=F32).astype(o_ref.dtype)

    return pl.pallas_call(
        body, name=name, grid=(s_n, nkb, nnb),
        in_specs=[pl.BlockSpec((t, tk), lambda s, a, b: (0, a)),
                  pl.BlockSpec((t, tn), lambda s, a, b: (0, s * nnb + b))],
        out_specs=pl.BlockSpec((None, tk, tn), lambda s, a, b: (s, a, b)),
        out_shape=jax.ShapeDtypeStruct((s_n, kb, nb), BF16),
        compiler_params=_cp("parallel", "parallel", "parallel"))(x, dy)


def mm_bd(x, wr, wi, name):
    t, _ = x.shape
    g_n, gs, _ = wr.shape
    rb = _row_block(t)

    def body(x_ref, wr_ref, wi_ref, r_ref, i_ref):
        xv = x_ref[...]
        r_ref[...] = jnp.dot(xv, wr_ref[...], preferred_element_type=F32)
        i_ref[...] = jnp.dot(xv, wi_ref[...], preferred_element_type=F32)

    blk = pl.BlockSpec((rb, gs), lambda g, r: (r, g))
    wspec = pl.BlockSpec((None, gs, gs), lambda g, r: (g, 0, 0))
    return pl.pallas_call(
        body, name=name, grid=(g_n, t // rb), in_specs=[blk, wspec, wspec], out_specs=[blk, blk],
        out_shape=[jax.ShapeDtypeStruct((t, g_n * gs), F32)] * 2,
        compiler_params=_cp("parallel", "parallel"))(x, wr, wi)


def mm_bd_nt(dr, di, wr, wi, name):
    t, _ = dr.shape
    g_n, gs, _ = wr.shape
    rb = _row_block(t)
    nt = (((1,), (1,)), ((), ()))

    def body(dr_ref, di_ref, wr_ref, wi_ref, o_ref):
        o_ref[...] = (lax.dot_general(dr_ref[...], wr_ref[...], nt, preferred_element_type=F32)
                      + lax.dot_general(di_ref[...], wi_ref[...], nt, preferred_element_type=F32))

    blk = pl.BlockSpec((rb, gs), lambda g, r: (r, g))
    wspec = pl.BlockSpec((None, gs, gs), lambda g, r: (g, 0, 0))
    return pl.pallas_call(
        body, name=name, grid=(g_n, t // rb), in_specs=[blk, blk, wspec, wspec], out_specs=blk,
        out_shape=jax.ShapeDtypeStruct((t, g_n * gs), F32),
        compiler_params=_cp("parallel", "parallel"))(dr, di, wr, wi)


def mm_bd_tn(x, dr, di, gs, name):
    t, w = x.shape
    g_n = w // gs
    tn_dims = (((0,), (0,)), ((), ()))

    def body(x_ref, dr_ref, di_ref, gr_ref, gi_ref):
        xv = x_ref[...]
        gr_ref[...] = lax.dot_general(xv, dr_ref[...], tn_dims, preferred_element_type=F32)
        gi_ref[...] = lax.dot_general(xv, di_ref[...], tn_dims, preferred_element_type=F32)

    blk = pl.BlockSpec((t, gs), lambda g: (0, g))
    ospec = pl.BlockSpec((None, gs, gs), lambda g: (g, 0, 0))
    return pl.pallas_call(
        body, name=name, grid=(g_n,), in_specs=[blk, blk, blk], out_specs=[ospec, ospec],
        out_shape=[jax.ShapeDtypeStruct((g_n, gs, gs), F32)] * 2,
        compiler_params=_cp("parallel"))(x, dr, di)


def embed_fwd(meta, x2d, name):
    nm, d = meta.shape
    seq = x2d.shape[0]
    t = nm + seq
    cb = _tile(d, 256)

    def body(m_ref, x_ref, h_ref, hb_ref):
        h_ref[pl.ds(0, nm), :] = m_ref[...]
        h_ref[pl.ds(nm, seq), :] = x_ref[...]
        hb_ref[pl.ds(0, nm), :] = m_ref[...].astype(BF16)
        hb_ref[pl.ds(nm, seq), :] = x_ref[...].astype(BF16)

    return pl.pallas_call(
        body, name=name, grid=(d // cb,),
        in_specs=[pl.BlockSpec((nm, cb), lambda j: (0, j)), pl.BlockSpec((seq, cb), lambda j: (0, j))],
        out_specs=[pl.BlockSpec((t, cb), lambda j: (0, j))] * 2,
        out_shape=[jax.ShapeDtypeStruct((t, d), F32), jax.ShapeDtypeStruct((t, d), BF16)],
        compiler_params=_cp("parallel"))(meta, x2d)


def embed_bwd(adds, nm, name):
    t, d = adds[0][0].shape
    seq = t - nm
    cb = _tile(d, 256)
    scales = [s for _, s in adds]
    n = len(adds)

    def body(*refs):
        tot = None
        for r, sc in zip(refs[:n], scales):
            term = r[...] if sc == 1.0 else sc * r[...]
            tot = term if tot is None else tot + term
        gm_ref, gx_ref = refs[n], refs[n + 1]
        gm_ref[...] = tot[0:nm]
        gx_ref[...] = tot[nm:t]

    return pl.pallas_call(
        body, name=name, grid=(d // cb,),
        in_specs=[pl.BlockSpec((t, cb), lambda j: (0, j))] * n,
        out_specs=[pl.BlockSpec((nm, cb), lambda j: (0, j)), pl.BlockSpec((seq, cb), lambda j: (0, j))],
        out_shape=[jax.ShapeDtypeStruct((nm, d), F32), jax.ShapeDtypeStruct((seq, d), F32)],
        compiler_params=_cp("parallel"))(*[a for a, _ in adds])


def loss_fwd_bwd(h, tgt, nm, name):
    t, d = h.shape
    seq = t - nm
    cb = _tile(d, 256)
    inv_d = 1.0 / d

    def body(h_ref, t_ref, loss_ref, dy_ref):
        @pl.when(pl.program_id(0) == 0)
        def _():
            loss_ref[...] = jnp.zeros_like(loss_ref)
        err = h_ref[pl.ds(nm, seq), :] - t_ref[...]
        dy_ref[pl.ds(0, nm), :] = jnp.zeros((nm, cb), F32)
        dy_ref[pl.ds(nm, seq), :] = err * inv_d
        loss_ref[...] += (0.5 * inv_d) * jnp.sum(err * err, keepdims=True)

    return pl.pallas_call(
        body, name=name, grid=(d // cb,),
        in_specs=[pl.BlockSpec((t, cb), lambda j: (0, j)), pl.BlockSpec((seq, cb), lambda j: (0, j))],
        out_specs=[pl.BlockSpec((1, 1), lambda j: (0, 0)), pl.BlockSpec((t, cb), lambda j: (0, j))],
        out_shape=[jax.ShapeDtypeStruct((1, 1), F32), jax.ShapeDtypeStruct((t, d), F32)],
        compiler_params=_cp("arbitrary"))(h, tgt)


def ln_fwd(h, mix, g, b, alpha, name):
    t, d = h.shape
    rb = _row_block(t)

    def body(h_ref, m_ref, g_ref, b_ref, y_ref, yb_ref, xh_ref, rs_ref):
        s = alpha * h_ref[...] + m_ref[...]
        mu = jnp.mean(s, axis=-1, keepdims=True)
        c = s - mu
        var = jnp.mean(c * c, axis=-1, keepdims=True)
        rstd = lax.rsqrt(var + LN_EPS)
        xh = c * rstd
        y = xh * g_ref[...] + b_ref[...]
        y_ref[...] = y
        yb_ref[...] = y.astype(BF16)
        xh_ref[...] = xh
        rs_ref[...] = rstd

    row = pl.BlockSpec((rb, d), lambda r: (r, 0))
    vec = pl.BlockSpec((1, d), lambda r: (0, 0))
    return pl.pallas_call(
        body, name=name, grid=(t // rb,), in_specs=[row, row, vec, vec],
        out_specs=[row, row, row, pl.BlockSpec((rb, 1), lambda r: (r, 0))],
        out_shape=[jax.ShapeDtypeStruct((t, d), F32), jax.ShapeDtypeStruct((t, d), BF16),
                   jax.ShapeDtypeStruct((t, d), F32), jax.ShapeDtypeStruct((t, 1), F32)],
        compiler_params=_cp("parallel"))(h, mix, g, b)


def ln_bwd(adds, xhat, rstd, g, name):
    t, d = xhat.shape
    rb = _row_block(t)
    scales = [s for _, s in adds]
    n = len(adds)

    def body(*refs):
        xh_ref, rs_ref, g_ref = refs[n:n + 3]
        ds_ref, dsb_ref, dg_ref, db_ref = refs[n + 3:]
        dy = None
        for r, sc in zip(refs[:n], scales):
            term = r[...] if sc == 1.0 else sc * r[...]
            dy = term if dy is None else dy + term

        @pl.when(pl.program_id(0) == 0)
        def _():
            dg_ref[...] = jnp.zeros_like(dg_ref)
            db_ref[...] = jnp.zeros_like(db_ref)

        xh = xh_ref[...]
        dxh = dy * g_ref[...]
        m1 = jnp.mean(dxh, axis=-1, keepdims=True)
        m2 = jnp.mean(dxh * xh, axis=-1, keepdims=True)
        ds = rs_ref[...] * (dxh - m1 - xh * m2)
        ds_ref[...] = ds
        dsb_ref[...] = ds.astype(BF16)
        dg_ref[...] += jnp.sum(dy * xh, axis=0, keepdims=True)
        db_ref[...] += jnp.sum(dy, axis=0, keepdims=True)

    row = pl.BlockSpec((rb, d), lambda r: (r, 0))
    vec = pl.BlockSpec((1, d), lambda r: (0, 0))
    return pl.pallas_call(
        body, name=name, grid=(t // rb,),
        in_specs=[row] * n + [row, pl.BlockSpec((rb, 1), lambda r: (r, 0)), vec],
        out_specs=[row, row, vec, vec],
        out_shape=[jax.ShapeDtypeStruct((t, d), F32), jax.ShapeDtypeStruct((t, d), BF16),
                   jax.ShapeDtypeStruct((1, d), F32), jax.ShapeDtypeStruct((1, d), F32)],
        compiler_params=_cp("arbitrary"))(*[a for a, _ in adds], xhat, rstd, g)


def _conv_fwd_val(xv, w_ref, b_ref, width):
    y = b_ref[...]
    for j in range(width):
        y = y + _shift_down(xv, j) * w_ref[pl.ds(width - 1 - j, 1), :]
    return y


def _conv_bwd_val(dout, xv, w_ref, width):
    dx = None
    dws = [None] * width
    for j in range(width):
        k = width - 1 - j
        up = _shift_up(dout, j)
        term = up * w_ref[pl.ds(k, 1), :]
        dx = term if dx is None else dx + term
        dws[k] = jnp.sum(up * xv, axis=0, keepdims=True)
    return dx, dws, jnp.sum(dout, axis=0, keepdims=True)


def a_conv_fwd(gr, cw, cbias, name):
    t, two_dr = gr.shape
    dr = two_dr // 2
    width = cw.shape[0]
    cb = _tile(dr, 256)
    off = dr // cb

    def body(x_ref, w_ref, b_ref, rc_ref, rcb_ref):
        y = _conv_fwd_val(x_ref[...], w_ref, b_ref, width)
        rc_ref[...] = y
        rcb_ref[...] = y.astype(BF16)

    return pl.pallas_call(
        body, name=name, grid=(dr // cb,),
        in_specs=[pl.BlockSpec((t, cb), lambda j: (0, off + j)),
                  pl.BlockSpec((width, cb), lambda j: (0, j)), pl.BlockSpec((1, cb), lambda j: (0, j))],
        out_specs=[pl.BlockSpec((t, cb), lambda j: (0, j))] * 2,
        out_shape=[jax.ShapeDtypeStruct((t, dr), F32), jax.ShapeDtypeStruct((t, dr), BF16)],
        compiler_params=_cp("parallel"))(gr, cw, cbias)


def a_conv_bwd(drc_a, drc_b, gr, cw, name):
    t, two_dr = gr.shape
    dr = two_dr // 2
    width = cw.shape[0]
    cb = _tile(dr, 256)
    off = dr // cb

    def body(da_ref, db_ref, x_ref, w_ref, dx_ref, dw_ref, dbias_ref):
        dout = da_ref[...] + db_ref[...]
        dx, dws, dbias = _conv_bwd_val(dout, x_ref[...], w_ref, width)
        dx_ref[...] = dx.astype(BF16)
        for k in range(width):
            dw_ref[pl.ds(k, 1), :] = dws[k]
        dbias_ref[...] = dbias

    col = pl.BlockSpec((t, cb), lambda j: (0, j))
    return pl.pallas_call(
        body, name=name, grid=(dr // cb,),
        in_specs=[col, col, pl.BlockSpec((t, cb), lambda j: (0, off + j)),
                  pl.BlockSpec((width, cb), lambda j: (0, j))],
        out_specs=[col, pl.BlockSpec((width, cb), lambda j: (0, j)), pl.BlockSpec((1, cb), lambda j: (0, j))],
        out_shape=[jax.ShapeDtypeStruct((t, dr), BF16), jax.ShapeDtypeStruct((width, dr), F32),
                   jax.ShapeDtypeStruct((1, dr), F32)],
        compiler_params=_cp("parallel"))(drc_a, drc_b, gr, cw)


def _lru_gates(r_pre, i_pre, br, bi, lam):
    r = _sigmoid(r_pre + br)
    i = _sigmoid(i_pre + bi)
    sp = _softplus(-lam)
    la = -LRU_C * r * sp
    a = jnp.exp(la)
    m = jnp.sqrt(_neg_expm1(2.0 * la))
    return r, i, sp, la, a, m


def a_elem_fwd(gr, rc, r_pre, i_pre, br, bi, lam, name):
    t, dr = rc.shape
    cb = _tile(dr, 2 * LANES)
    rb = _row_block(t)
    chunks = [pl.ds(r * rb, rb) for r in range(t // rb)]

    def body(gate_ref, rc_ref, rp_ref, ip_ref, br_ref, bi_ref, lam_ref, hs_ref, g_ref, a_s, u_s):
        for rows in chunks:
            _, i, _, _, a, m = _lru_gates(rp_ref[rows, :], ip_ref[rows, :], br_ref[...], bi_ref[...], lam_ref[...])
            a_s[rows, :] = a
            u_s[rows, :] = m * (i * rc_ref[rows, :])
        _scan_rows(a_s, u_s, hs_ref, t, cb, reverse=False)
        for rows in chunks:
            g_ref[rows, :] = (_gelu(gate_ref[rows, :]) * hs_ref[rows, :]).astype(BF16)

    col = pl.BlockSpec((t, cb), lambda j: (0, j))
    vec = pl.BlockSpec((1, cb), lambda j: (0, j))
    return pl.pallas_call(
        body, name=name, grid=(dr // cb,),
        in_specs=[col, col, col, col, vec, vec, vec],
        out_specs=[col, col],
        out_shape=[jax.ShapeDtypeStruct((t, dr), F32), jax.ShapeDtypeStruct((t, dr), BF16)],
        scratch_shapes=[pltpu.VMEM((t, cb), F32), pltpu.VMEM((t, cb), F32)],
        compiler_params=_cp("parallel"))(gr, rc, r_pre, i_pre, br, bi, lam)


def a_elem_bwd(dg, gr, rc, r_pre, i_pre, hs, br, bi, lam, name):
    t, dr = rc.shape
    cb = _tile(dr, 2 * LANES)
    rb = _row_block(t)
    chunks = [pl.ds(r * rb, rb) for r in range(t // rb)]

    def body(dg_ref, gate_ref, rc_ref, rp_ref, ip_ref, hs_ref, br_ref, bi_ref, lam_ref,
             dgate_ref, dr_ref, di_ref, drc_ref, dlam_ref, dbr_ref, dbi_ref, a_s, b_s, g_s, hp_s):
        lamv = lam_ref[...]
        gates = lambda rows: _lru_gates(rp_ref[rows, :], ip_ref[rows, :], br_ref[...], bi_ref[...], lamv)
        for rows in chunks:
            a_s[rows, :] = gates(rows)[4]
            ge, dge = _gelu_parts(gate_ref[rows, :])
            dgv = dg_ref[rows, :]
            dgate_ref[rows, :] = (dgv * hs_ref[rows, :] * dge).astype(BF16)
            b_s[rows, :] = dgv * ge
        a_s[...] = _shift_up(a_s[...], 1)
        hp_s[...] = _shift_down(hs_ref[...], 1)
        _scan_rows(a_s, b_s, g_s, t, cb, reverse=True)
        dsp = dbr = dbi = jnp.zeros((1, cb), F32)
        for rows in chunks:
            r, i, sp, _, a, m = gates(rows)
            rcv = rc_ref[rows, :]
            gsum = g_s[rows, :]
            da = gsum * hp_s[rows, :]
            dm = gsum * (i * rcv)
            d_i = gsum * m * rcv
            drc_ref[rows, :] = gsum * m * i
            dla = a * da - dm * (a * a) / m
            d_r = (-LRU_C) * sp * dla
            dsp = dsp + jnp.sum((-LRU_C) * r * dla, axis=0, keepdims=True)
            d_rp = d_r * r * (1.0 - r)
            d_ip = d_i * i * (1.0 - i)
            dr_ref[rows, :] = d_rp.astype(BF16)
            di_ref[rows, :] = d_ip.astype(BF16)
            dbr = dbr + jnp.sum(d_rp, axis=0, keepdims=True)
            dbi = dbi + jnp.sum(d_ip, axis=0, keepdims=True)
        dlam_ref[...] = -dsp * _sigmoid(-lamv)
        dbr_ref[...] = dbr
        dbi_ref[...] = dbi

    col = pl.BlockSpec((t, cb), lambda j: (0, j))
    vec = pl.BlockSpec((1, cb), lambda j: (0, j))
    big_b = jax.ShapeDtypeStruct((t, dr), BF16)
    vec_s = jax.ShapeDtypeStruct((1, dr), F32)
    return pl.pallas_call(
        body, name=name, grid=(dr // cb,),
        in_specs=[col, col, col, col, col, col, vec, vec, vec],
        out_specs=[col, col, col, col, vec, vec, vec],
        out_shape=[big_b, big_b, big_b, jax.ShapeDtypeStruct((t, dr), F32), vec_s, vec_s, vec_s],
        scratch_shapes=[pltpu.VMEM((t, cb), F32)] * 4,
        compiler_params=_cp("parallel"))(dg, gr, rc, r_pre, i_pre, hs, br, bi, lam)


def f_elem_fwd(z, cw, cbias, name):
    t, two_f = z.shape
    dff = two_f // 2
    width = cw.shape[0]
    cb = _tile(dff, 256)
    off = dff // cb

    def body(zg_ref, zv_ref, wg_ref, wv_ref, bg_ref, bv_ref, o_ref):
        zcg = _conv_fwd_val(zg_ref[...], wg_ref, bg_ref, width)
        zcv = _conv_fwd_val(zv_ref[...], wv_ref, bv_ref, width)
        o_ref[...] = (_gelu(zcg) * zcv).astype(BF16)

    lo = lambda j: (0, j)
    hi = lambda j: (0, off + j)
    return pl.pallas_call(
        body, name=name, grid=(dff // cb,),
        in_specs=[pl.BlockSpec((t, cb), lo), pl.BlockSpec((t, cb), hi),
                  pl.BlockSpec((width, cb), lo), pl.BlockSpec((width, cb), hi),
                  pl.BlockSpec((1, cb), lo), pl.BlockSpec((1, cb), hi)],
        out_specs=pl.BlockSpec((t, cb), lo),
        out_shape=jax.ShapeDtypeStruct((t, dff), BF16),
        compiler_params=_cp("parallel"))(z, z, cw, cw, cbias, cbias)


def f_elem_bwd(z, dff_g, cw, cbias, name):
    t, two_f = z.shape
    dff = two_f // 2
    width = cw.shape[0]
    cb = _tile(dff, 256)
    off = dff // cb

    def body(zg_ref, zv_ref, d_ref, wg_ref, wv_ref, bg_ref, bv_ref, dz_ref, dw_ref, db_ref, dval_s):
        def emit(dout, xv, w_ref):
            dx, dws, dbias = _conv_bwd_val(dout, xv, w_ref, width)
            dz_ref[...] = dx.astype(BF16)
            for k in range(width):
                dw_ref[pl.ds(k, 1), :] = dws[k]
            db_ref[...] = dbias

        @pl.when(pl.program_id(1) == 0)
        def _():
            zg = zg_ref[...]
            zcg = _conv_fwd_val(zg, wg_ref, bg_ref, width)
            zcv = _conv_fwd_val(zv_ref[...], wv_ref, bv_ref, width)
            ge, dge = _gelu_parts(zcg)
            dv = d_ref[...]
            dval_s[...] = dv * ge
            emit(dv * zcv * dge, zg, wg_ref)

        @pl.when(pl.program_id(1) == 1)
        def _():
            emit(dval_s[...], zv_ref[...], wv_ref)

    lo = lambda j, h: (0, j)
    hi = lambda j, h: (0, off + j)
    out = lambda j, h: (0, h * off + j)
    return pl.pallas_call(
        body, name=name, grid=(dff // cb, 2),
        in_specs=[pl.BlockSpec((t, cb), lo), pl.BlockSpec((t, cb), hi), pl.BlockSpec((t, cb), lo),
                  pl.BlockSpec((width, cb), lo), pl.BlockSpec((width, cb), hi),
                  pl.BlockSpec((1, cb), lo), pl.BlockSpec((1, cb), hi)],
        out_specs=[pl.BlockSpec((t, cb), out), pl.BlockSpec((width, cb), out), pl.BlockSpec((1, cb), out)],
        out_shape=[jax.ShapeDtypeStruct((t, two_f), BF16), jax.ShapeDtypeStruct((width, two_f), F32),
                   jax.ShapeDtypeStruct((1, two_f), F32)],
        scratch_shapes=[pltpu.VMEM((t, cb), F32)],
        compiler_params=_cp("parallel", "arbitrary"))(z, z, dff_g, cw, cw, cbias, cbias)


def kv_fwd(z, fb, d_model, name):
    t, _ = z.shape
    blk = 2 * d_model // LANES

    def body(z_ref, fb_ref, c_ref, lf_s):
        v = z_ref[...] + fb_ref[...]
        lf_s[...] = -_softplus(-v)
        _scan_rows(None, lf_s, c_ref, t, LANES, reverse=False)

    return pl.pallas_call(
        body, name=name, grid=(1,),
        in_specs=[pl.BlockSpec((t, LANES), lambda j: (0, blk)), pl.BlockSpec((1, LANES), lambda j: (0, 0))],
        out_specs=pl.BlockSpec((t, LANES), lambda j: (0, 0)),
        out_shape=jax.ShapeDtypeStruct((t, LANES), F32),
        scratch_shapes=[pltpu.VMEM((t, LANES), F32)],
        compiler_params=_cp("arbitrary"))(z, fb)


def kv_bwd(dcs, z, fb, d_model, name):
    t, _ = z.shape
    blk = 2 * d_model // LANES
    n = len(dcs)

    def body(*refs):
        z_ref, fb_ref, dz_ref, dfb_ref, dc_s, dl_s = refs[n:]
        tot = refs[0][...]
        for r in refs[1:n]:
            tot = tot + r[...]
        dc_s[...] = tot
        _scan_rows(None, dc_s, dl_s, t, LANES, reverse=True)
        v = z_ref[...] + fb_ref[...]
        dz = dl_s[...] * _sigmoid(-v)
        dz_ref[...] = dz.astype(BF16)
        dfb_ref[...] = jnp.sum(dz, axis=0, keepdims=True)

    full = pl.BlockSpec((t, LANES), lambda j: (0, 0))
    return pl.pallas_call(
        body, name=name, grid=(1,),
        in_specs=[full] * n + [pl.BlockSpec((t, LANES), lambda j: (0, blk)),
                               pl.BlockSpec((1, LANES), lambda j: (0, 0))],
        out_specs=[full, pl.BlockSpec((1, LANES), lambda j: (0, 0))],
        out_shape=[jax.ShapeDtypeStruct((t, LANES), BF16), jax.ShapeDtypeStruct((1, LANES), F32)],
        scratch_shapes=[pltpu.VMEM((t, LANES), F32)] * 2,
        compiler_params=_cp("arbitrary"))(*dcs, z, fb)


def add_cast(a, b, name):
    t, d = a.shape
    cb = _tile(d, 512)

    def body(a_ref, b_ref, o_ref):
        o_ref[...] = (a_ref[...] + b_ref[...]).astype(BF16)

    col = pl.BlockSpec((t, cb), lambda j: (0, j))
    return pl.pallas_call(body, name=name, grid=(d // cb,), in_specs=[col, col], out_specs=col,
                          out_shape=jax.ShapeDtypeStruct((t, d), BF16),
                          compiler_params=_cp("parallel"))(a, b)


def _attn_geometry(t):
    nqb = 6 if t > 1024 else 2
    tp = _round_up(t, LANES * nqb)
    return nqb, tp, tp // nqb


def _attn_scales(dh):
    scale = dh ** -0.5
    if math.log2(scale).is_integer():
        return scale, 1.0
    return 1.0, scale


def _attn_pieces(qs, ks, crow, j, i, tq, dh, s_mul):
    r0 = i * tq
    lanes = pl.ds(j * dh, dh)
    qi = qs[pl.ds(r0, tq), lanes]
    spans = ([(0, r0)] if i > 0 else []) + [(r0, tq)]
    logits = []
    for k0, n in spans:
        s = lax.dot_general(qi, ks[pl.ds(k0, n), lanes], (((1,), (1,)), ((), ())),
                            preferred_element_type=F32)
        if s_mul != 1.0:
            s = s * s_mul
        s = s - crow[:, k0:k0 + n]
        if k0 == r0:
            rows = lax.broadcasted_iota(jnp.int32, (tq, tq), 0)
            cols = lax.broadcasted_iota(jnp.int32, (tq, tq), 1)
            s = jnp.where(cols <= rows, s, NEG_BIG)
        logits.append(s)
    mx = jnp.max(logits[0], axis=1, keepdims=True)
    for s in logits[1:]:
        mx = jnp.maximum(mx, jnp.max(s, axis=1, keepdims=True))
    es = [jnp.exp(s - mx) for s in logits]
    tot = jnp.sum(es[0], axis=1, keepdims=True)
    for e in es[1:]:
        tot = tot + jnp.sum(e, axis=1, keepdims=True)
    inv = 1.0 / tot
    return [(k0, n, e * inv) for (k0, n), e in zip(spans, es)], qi


def attn_fwd(qg, z, ct_pad, d_model, n_heads, name):
    t = qg.shape[0]
    dh = d_model // n_heads
    hp = LANES // dh
    nqb, tp, tq = _attn_geometry(t)
    nblk = d_model // LANES
    q_mul, s_mul = _attn_scales(dh)

    def body(q_ref, og_ref, k_ref, v_ref, ct_ref, o_ref, mo_ref, qs, ks, vs, os_):
        pad = jnp.zeros((tp - t, LANES), BF16)
        qs[pl.ds(0, t), :] = (q_ref[...] * q_mul).astype(BF16)
        qs[pl.ds(t, tp - t), :] = pad
        for src, dst in ((k_ref, ks), (v_ref, vs)):
            dst[pl.ds(0, t), :] = src[...].astype(BF16)
            dst[pl.ds(t, tp - t), :] = pad
        for j in range(hp):
            crow = ct_ref[j]
            lanes = pl.ds(j * dh, dh)
            for i in range(nqb):
                pieces, _ = _attn_pieces(qs, ks, crow, j, i, tq, dh, s_mul)
                acc = None
                for k0, n, p in pieces:
                    part = jnp.dot(p.astype(BF16), vs[pl.ds(k0, n), lanes], preferred_element_type=F32)
                    acc = part if acc is None else acc + part
                os_[pl.ds(i * tq, tq), lanes] = acc
        o = os_[pl.ds(0, t), :]
        o_ref[...] = o
        mo_ref[...] = (o * _sigmoid(og_ref[...])).astype(BF16)

    col = lambda off: pl.BlockSpec((t, LANES), lambda p: (0, off + p))
    return pl.pallas_call(
        body, name=name, grid=(nblk,),
        in_specs=[col(0), col(nblk), col(0), col(nblk), pl.BlockSpec((hp, 1, tp), lambda p: (p, 0, 0))],
        out_specs=[col(0), col(0)],
        out_shape=[jax.ShapeDtypeStruct((t, d_model), F32), jax.ShapeDtypeStruct((t, d_model), BF16)],
        scratch_shapes=[pltpu.VMEM((tp, LANES), BF16)] * 3 + [pltpu.VMEM((tp, LANES), F32)],
        compiler_params=_cp("parallel"))(qg, qg, z, z, ct_pad)


def attn_bwd(dmo, qg, z, o, ct_pad, d_model, n_heads, name):
    t = qg.shape[0]
    dh = d_model // n_heads
    hp = LANES // dh
    nqb, tp, tq = _attn_geometry(t)
    nblk = d_model // LANES
    q_mul, s_mul = _attn_scales(dh)
    scale = dh ** -0.5
    tn_dims = (((0,), (0,)), ((), ()))
    nt_dims = (((1,), (1,)), ((), ()))

    def body(dmo_ref, q_ref, og_ref, k_ref, v_ref, o_ref, ct_ref,
             dq_ref, dog_ref, dk_ref, dv_ref, dct_ref, qs, ks, vs, dos, dqs, dks, dvs):
        pad = jnp.zeros((tp - t, LANES), BF16)
        sg = _sigmoid(og_ref[...])
        dmo_v = dmo_ref[...]
        dog_ref[...] = (dmo_v * o_ref[...] * sg * (1.0 - sg)).astype(BF16)
        dos[pl.ds(0, t), :] = (dmo_v * sg).astype(BF16)
        dos[pl.ds(t, tp - t), :] = pad
        qs[pl.ds(0, t), :] = (q_ref[...] * q_mul).astype(BF16)
        qs[pl.ds(t, tp - t), :] = pad
        for src, dst in ((k_ref, ks), (v_ref, vs)):
            dst[pl.ds(0, t), :] = src[...].astype(BF16)
            dst[pl.ds(t, tp - t), :] = pad
        dks[...] = jnp.zeros_like(dks)
        dvs[...] = jnp.zeros_like(dvs)
        dct_ref[...] = jnp.zeros_like(dct_ref)
        for j in range(hp):
            crow = ct_ref[j]
            lanes = pl.ds(j * dh, dh)
            for i in range(nqb):
                pieces, qi = _attn_pieces(qs, ks, crow, j, i, tq, dh, s_mul)
                do_i = dos[pl.ds(i * tq, tq), lanes]
                dps = [lax.dot_general(do_i, vs[pl.ds(k0, n), lanes], nt_dims, preferred_element_type=F32)
                       for k0, n, _ in pieces]
                row = None
                for (_, _, p), dp in zip(pieces, dps):
                    part = jnp.sum(p * dp, axis=1, keepdims=True)
                    row = part if row is None else row + part
                dq_i = None
                for (k0, n, p), dp in zip(pieces, dps):
                    ds = p * (dp - row)
                    ds_b = ds.astype(BF16)
                    keys = pl.ds(k0, n)
                    part = jnp.dot(ds_b, ks[keys, lanes], preferred_element_type=F32)
                    dq_i = part if dq_i is None else dq_i + part
                    dks[keys, lanes] += lax.dot_general(ds_b, qi, tn_dims, preferred_element_type=F32) * s_mul
                    dvs[keys, lanes] += lax.dot_general(p.astype(BF16), do_i, tn_dims,
                                                        preferred_element_type=F32)
                    dct_ref[j, :, keys] -= jnp.sum(ds, axis=0, keepdims=True)
                dqs[pl.ds(i * tq, tq), lanes] = dq_i * scale
        dq_ref[...] = dqs[pl.ds(0, t), :].astype(BF16)
        dk_ref[...] = dks[pl.ds(0, t), :]
        dv_ref[...] = dvs[pl.ds(0, t), :]

    col = lambda off: pl.BlockSpec((t, LANES), lambda p: (0, off + p))
    big = lambda dt: jax.ShapeDtypeStruct((t, d_model), dt)
    return pl.pallas_call(
        body, name=name, grid=(nblk,),
        in_specs=[col(0), col(0), col(nblk), col(0), col(nblk), col(0),
                  pl.BlockSpec((hp, 1, tp), lambda p: (p, 0, 0))],
        out_specs=[col(0), col(0), col(0), col(0), pl.BlockSpec((hp, 1, tp), lambda p: (p, 0, 0))],
        out_shape=[big(BF16), big(BF16), big(F32), big(F32),
                   jax.ShapeDtypeStruct((n_heads, 1, tp), F32)],
        scratch_shapes=[pltpu.VMEM((tp, LANES), BF16)] * 4 + [pltpu.VMEM((tp, LANES), F32)] * 3,
        compiler_params=_cp("parallel"))(dmo, qg, qg, z, z, o, ct_pad)


def cast_into_slot(shard, w2d, name):
    r, c = w2d.shape
    rh = r // 2
    tr = _tile(rh, 512, 16)
    n = rh // tr

    def body(sh_ref, w_ref, o_ref):
        del sh_ref
        o_ref[...] = w_ref[...].astype(BF16)

    return pl.pallas_call(
        body, name=name,
        grid_spec=pltpu.PrefetchScalarGridSpec(
            num_scalar_prefetch=1, grid=(2, n),
            in_specs=[pl.BlockSpec((tr, c), lambda h, i, sh: (h * n + i, 0))],
            out_specs=pl.BlockSpec((None, None, tr, c), lambda h, i, sh: (sh[0], h, i, 0))),
        out_shape=jax.ShapeDtypeStruct((N_SHARDS, 2, rh, c), BF16),
        compiler_params=_cp("parallel", "parallel"))(shard, w2d)


def add_halves(core, g, other, name):
    s_n, _, rh, c = g.shape
    tr = _tile(rh, 512, 16)

    def body(core_ref, g_ref, o_ref, out_ref):
        del core_ref
        out_ref[...] = (g_ref[...].astype(F32) + o_ref[...].astype(F32)).astype(out_ref.dtype)

    return pl.pallas_call(
        body, name=name,
        grid_spec=pltpu.PrefetchScalarGridSpec(
            num_scalar_prefetch=1, grid=(s_n, rh // tr),
            in_specs=[pl.BlockSpec((None, None, tr, c), lambda s, i, cr: (s, cr[0], i, 0)),
                      pl.BlockSpec((None, tr, c), lambda s, i, cr: (s, i, 0))],
            out_specs=pl.BlockSpec((None, tr, c), lambda s, i, cr: (s, i, 0))),
        out_shape=jax.ShapeDtypeStruct((s_n, rh, c), g.dtype),
        compiler_params=_cp("parallel", "parallel"))(core, g, other)


def add_four(shard_core, csum, recv, buf, layer, name):
    _, rh, c = csum.shape
    tr = _tile(rh, 512, 16)

    def body(sc_ref, a_ref, r_ref, buf_ref, out_ref):
        del sc_ref, buf_ref
        acc = a_ref[...].astype(F32)
        for k in range(3):
            acc = acc + r_ref[k].astype(F32)
        out_ref[...] = acc

    return pl.pallas_call(
        body, name=name,
        grid_spec=pltpu.PrefetchScalarGridSpec(
            num_scalar_prefetch=1, grid=(rh // tr,),
            in_specs=[pl.BlockSpec((None, tr, c), lambda i, sc: (sc[0], i, 0)),
                      pl.BlockSpec((3, tr, c), lambda i, sc: (0, i, 0)),
                      pl.BlockSpec(memory_space=pl.ANY)],
            out_specs=pl.BlockSpec((None, None, tr, c), lambda i, sc: (layer, sc[1], i, 0))),
        out_shape=jax.ShapeDtypeStruct(buf.shape, F32),
        input_output_aliases={3: 0},
        compiler_params=_cp("parallel"))(shard_core, csum, recv, buf)


def adamw(w, g, m, v, name):
    r, c = w.shape
    tr = _tile(r, 512, SUBLANES)
    c1 = 1.0 - ADAM_B1 ** ADAM_STEP
    c2 = 1.0 - ADAM_B2 ** ADAM_STEP

    def body(w_ref, g_ref, m_ref, v_ref, d_ref, mo_ref, vo_ref):
        gv = g_ref[...]
        mn = ADAM_B1 * m_ref[...] + (1.0 - ADAM_B1) * gv
        vn = ADAM_B2 * v_ref[...] + (1.0 - ADAM_B2) * (gv * gv)
        m_hat = mn / c1
        v_hat = vn / c2
        d_ref[...] = -ADAM_LR * (m_hat / (jnp.sqrt(v_hat) + ADAM_EPS) + ADAM_WD * w_ref[...])
        mo_ref[...] = mn
        vo_ref[...] = vn

    blk = pl.BlockSpec((tr, c), lambda i: (i, 0))
    return pl.pallas_call(
        body, name=name, grid=(r // tr,), in_specs=[blk] * 4, out_specs=[blk] * 3,
        out_shape=[jax.ShapeDtypeStruct((r, c), F32)] * 3,
        compiler_params=_cp("parallel"))(w, g, m, v)


def _coords():
    return lax.axis_index("x"), lax.axis_index("y"), lax.axis_index("c")


def _exchange(name, ins, out_shapes, plan, in_place=False):
    n_in = len(ins)
    n_out = len(out_shapes)
    n_rem = len(plan([None] * n_in, [None] * n_out, True))

    def body(*refs):
        in_refs = refs[:n_in]
        out_refs = refs[n_in:n_in + n_out]
        send_sems, recv_sems = refs[n_in + n_out:]
        remote = plan(list(in_refs), list(out_refs), False)
        copies = [pltpu.make_async_remote_copy(
            src_ref=src, dst_ref=dst, send_sem=send_sems.at[q], recv_sem=recv_sems.at[q],
            device_id=peer, device_id_type=pl.DeviceIdType.MESH)
            for q, (src, dst, peer, _) in enumerate(remote)]
        waited = set()
        for q, (_, _, _, after) in enumerate(remote):
            if after is not None and after not in waited:
                copies[after].wait_recv()
                waited.add(after)
            copies[q].start()
        for q, cp in enumerate(copies):
            if q not in waited:
                cp.wait_recv()
        for cp in copies:
            cp.wait_send()

    hbm = pl.BlockSpec(memory_space=pl.ANY)
    return pl.pallas_call(
        body, name=name, in_specs=[hbm] * n_in, out_specs=[hbm] * n_out, out_shape=out_shapes,
        input_output_aliases={i: i for i in range(n_in)} if in_place else {},
        scratch_shapes=[pltpu.SemaphoreType.DMA((n_rem,)), pltpu.SemaphoreType.DMA((n_rem,))],
        compiler_params=pltpu.CompilerParams(has_side_effects=True))(*ins)


def _split_start(name, groups, plan):
    flat = [a for grp in groups for a in grp]
    n, n_grp = len(flat), len(groups)
    counts = [len(plan(g, [None] * len(grp), True)) for g, grp in enumerate(groups)]

    def body(*refs):
        ins, sems, token = refs[:n], refs[n:n + 2 * n_grp], refs[-1]
        pos = 0
        for g, grp in enumerate(groups):
            arrs = list(ins[pos:pos + len(grp)])
            pos += len(grp)
            for q, (src, dst, peer) in enumerate(plan(g, arrs, False)):
                pltpu.make_async_remote_copy(
                    src_ref=src, dst_ref=dst, send_sem=sems[2 * g].at[q], recv_sem=sems[2 * g + 1].at[q],
                    device_id=peer, device_id_type=pl.DeviceIdType.MESH).start()
        token[...] = jnp.zeros_like(token)

    hbm = pl.BlockSpec(memory_space=pltpu.HBM)
    sem = pl.BlockSpec(memory_space=pltpu.SEMAPHORE)
    outs = pl.pallas_call(
        body, name=name,
        out_shape=[pltpu.SemaphoreType.DMA((cnt,)) for cnt in counts for _ in range(2)]
        + [pltpu.HBM(a.shape, a.dtype) for a in flat] + [jax.ShapeDtypeStruct((SUBLANES, LANES), F32)],
        in_specs=[hbm] * n, out_specs=[sem] * (2 * n_grp) + [hbm] * n + [pl.BlockSpec(memory_space=pltpu.VMEM)],
        input_output_aliases={i: 2 * n_grp + i for i in range(n)},
        compiler_params=pltpu.CompilerParams(has_side_effects=pltpu.SideEffectType.DATAFLOW_SIDE_EFFECTING),
    )(*[pltpu.with_memory_space_constraint(a, pltpu.HBM) for a in flat])
    started, pos = [], 2 * n_grp
    for g, grp in enumerate(groups):
        started.append((outs[2 * g], outs[2 * g + 1], list(outs[pos:pos + len(grp)])))
        pos += len(grp)
    return started, outs[-1]


def _split_wait(name, started, after, plan_g):
    send_sems, recv_sems, arrs = started
    n = len(arrs)

    def body(*refs):
        ins, ssem, rsem = list(refs[:n]), refs[n], refs[n + 1]
        for q, (src, dst, peer) in enumerate(plan_g(ins, False)):
            cp = pltpu.make_async_remote_copy(
                src_ref=src, dst_ref=dst, send_sem=ssem.at[q], recv_sem=rsem.at[q],
                device_id=peer, device_id_type=pl.DeviceIdType.MESH)
            cp.wait_send()
            cp.wait_recv()

    hbm = pl.BlockSpec(memory_space=pltpu.HBM)
    sem = pl.BlockSpec(memory_space=pltpu.SEMAPHORE)
    return pl.pallas_call(
        body, name=name, out_shape=[pltpu.HBM(a.shape, a.dtype) for a in arrs],
        in_specs=[hbm] * n + [sem, sem, pl.BlockSpec(memory_space=pl.ANY)], out_specs=[hbm] * n,
        input_output_aliases={i: i for i in range(n)},
        compiler_params=pltpu.CompilerParams(has_side_effects=pltpu.SideEffectType.DATAFLOW_SIDE_EFFECTING),
    )(*arrs, send_sems, recv_sems, after)


def _gather_ici_plan(arrs, count_only):
    if count_only:
        return [None] * (3 * len(arrs))
    x, y, c = _coords()
    pushes = []
    for a in arrs:
        mine = a.at[2 * x + y, c]
        pushes += [(mine, mine, peer) for peer, _ in _other_chips(x, y, c)]
    return pushes


def _all_to_all_plan(arrs, count_only):
    half = len(arrs) // 2
    if count_only:
        return [None] * (3 * half)
    x, y, c = _coords()
    pushes = []
    for src, land in zip(arrs[:half], arrs[half:]):
        pushes += [(src.at[shard], land.at[k], peer) for k, (peer, shard) in enumerate(_other_chips(x, y, c))]
    return pushes


def forward_to_sibling(bufs, name):
    n = len(bufs)

    def plan(ins, outs, count_only):
        if count_only:
            return [None] * (3 * n)
        x, y, c = _coords()
        pushes = []
        for i in range(n):
            for _, src_shard in _other_chips(x, y, c):
                slab = outs[i].at[src_shard, c]
                pushes.append((slab, slab, (x, y, 1 - c), None))
        return pushes

    shapes = [jax.ShapeDtypeStruct(b.shape, b.dtype) for b in bufs]
    return _exchange(name, bufs, shapes, plan, in_place=True)


def _other_chips(x, y, c):
    return [((1 - x, y, c), 2 * (1 - x) + y), ((x, 1 - y, c), 2 * x + 1 - y),
            ((1 - x, 1 - y, c), 2 * (1 - x) + 1 - y)]


def swap_halves(grads, name):
    n = len(grads)
    shapes = [jax.ShapeDtypeStruct((g.shape[0],) + g.shape[2:], g.dtype) for g in grads]

    def plan(ins, outs, count_only):
        if count_only:
            return [None] * (n * N_SHARDS)
        x, y, c = _coords()
        remote = []
        for i in range(n):
            for s in range(N_SHARDS):
                remote.append((ins[i].at[s, 1 - c], outs[i].at[s], (x, y, 1 - c), None))
        return remote

    return _exchange(name, grads, shapes, plan)


def join_halves(bufs, name):
    slots = [(i, l) for i, b in enumerate(bufs) for l in range(b.shape[0])]

    def plan(ins, outs, count_only):
        if count_only:
            return [None] * len(slots)
        x, y, c = _coords()
        return [(outs[i].at[l, c], outs[i].at[l, c], (x, y, 1 - c), None) for i, l in slots]

    shapes = [jax.ShapeDtypeStruct(b.shape, b.dtype) for b in bufs]
    return _exchange(name, bufs, shapes, plan, in_place=True)


def gather_full(buf, name):
    def plan(ins, outs, count_only):
        if count_only:
            return [None] * 3
        x, y, c = _coords()
        mine = outs[0].at[2 * x + y]
        return [(mine, mine, peer, None) for peer, _ in _other_chips(x, y, c)]

    return _exchange(name, [buf], [jax.ShapeDtypeStruct(buf.shape, buf.dtype)], plan, in_place=True)[0]


def _pack(arrays, multiple):
    flat = jnp.concatenate([a.reshape(-1) for a in arrays])
    n = flat.shape[0]
    return jnp.pad(flat, (0, _round_up(n, multiple) - n))


def _unpack(flat, shapes):
    out, pos = [], 0
    for shp in shapes:
        n = math.prod(shp)
        out.append(flat[pos:pos + n].reshape(shp))
        pos += n
    return out


def _block_diag(w, per_group):
    nb, bs, _ = w.shape
    g = nb // per_group
    w4 = w.reshape(g, per_group, bs, bs)
    eye = jnp.eye(per_group, dtype=w.dtype)
    full = w4[:, :, :, None, :] * eye[None, :, None, :, None]
    return full.reshape(g, per_group * bs, per_group * bs).astype(BF16)


def _block_diag_extract(full, per_group, bs):
    g = full.shape[0]
    f5 = full.reshape(g, per_group, bs, per_group, bs)
    idx = jnp.arange(per_group)
    picked = f5[:, idx, :, idx, :]
    return jnp.moveaxis(picked, 0, 1).reshape(g * per_group, bs, bs)


def kernel(x, meta, a_w_in, a_conv_w, a_conv_b, a_w_r, a_b_r, a_w_i, a_b_i, a_lambda, a_w_out, kv_w, kv_f_b, b_w_in, b_w_out, f_w_in, f_conv_w, f_conv_b, f_w_out, ln1_g, ln1_b, ln2_g, ln2_b, loss_target, m_meta, m_a_w_in, m_a_conv_w, m_a_conv_b, m_a_w_r, m_a_b_r, m_a_w_i, m_a_b_i, m_a_lambda, m_a_w_out, m_kv_w, m_kv_f_b, m_b_w_in, m_b_w_out, m_f_w_in, m_f_conv_w, m_f_conv_b, m_f_w_out, m_ln1_g, m_ln1_b, m_ln2_g, m_ln2_b, v_meta, v_a_w_in, v_a_conv_w, v_a_conv_b, v_a_w_r, v_a_b_r, v_a_w_i, v_a_b_i, v_a_lambda, v_a_w_out, v_kv_w, v_kv_f_b, v_b_w_in, v_b_w_out, v_f_w_in, v_f_conv_w, v_f_conv_b, v_f_w_out, v_ln1_g, v_ln1_b, v_ln2_g, v_ln2_b):
    weights = dict(meta=meta, a_w_in=a_w_in, a_conv_w=a_conv_w, a_conv_b=a_conv_b, a_w_r=a_w_r, a_b_r=a_b_r,
                   a_w_i=a_w_i, a_b_i=a_b_i, a_lambda=a_lambda, a_w_out=a_w_out, kv_w=kv_w, kv_f_b=kv_f_b,
                   b_w_in=b_w_in, b_w_out=b_w_out, f_w_in=f_w_in, f_conv_w=f_conv_w, f_conv_b=f_conv_b,
                   f_w_out=f_w_out, ln1_g=ln1_g, ln1_b=ln1_b, ln2_g=ln2_g, ln2_b=ln2_b)
    mom_m = dict(meta=m_meta, a_w_in=m_a_w_in, a_conv_w=m_a_conv_w, a_conv_b=m_a_conv_b, a_w_r=m_a_w_r,
                 a_b_r=m_a_b_r, a_w_i=m_a_w_i, a_b_i=m_a_b_i, a_lambda=m_a_lambda, a_w_out=m_a_w_out,
                 kv_w=m_kv_w, kv_f_b=m_kv_f_b, b_w_in=m_b_w_in, b_w_out=m_b_w_out, f_w_in=m_f_w_in,
                 f_conv_w=m_f_conv_w, f_conv_b=m_f_conv_b, f_w_out=m_f_w_out, ln1_g=m_ln1_g, ln1_b=m_ln1_b,
                 ln2_g=m_ln2_g, ln2_b=m_ln2_b)
    mom_v = dict(meta=v_meta, a_w_in=v_a_w_in, a_conv_w=v_a_conv_w, a_conv_b=v_a_conv_b, a_w_r=v_a_w_r,
                 a_b_r=v_a_b_r, a_w_i=v_a_w_i, a_b_i=v_a_b_i, a_lambda=v_a_lambda, a_w_out=v_a_w_out,
                 kv_w=v_kv_w, kv_f_b=v_kv_f_b, b_w_in=v_b_w_in, b_w_out=v_b_w_out, f_w_in=v_f_w_in,
                 f_conv_w=v_f_conv_w, f_conv_b=v_f_conv_b, f_w_out=v_f_w_out, ln1_g=v_ln1_g, ln1_b=v_ln1_b,
                 ln2_g=v_ln2_g, ln2_b=v_ln2_b)
    return _train_step(x, loss_target, weights, mom_m, mom_v)


WEIGHT_ORDER = ("meta", "a_w_in", "a_conv_w", "a_conv_b", "a_w_r", "a_b_r", "a_w_i", "a_b_i", "a_lambda",
                "a_w_out", "kv_w", "kv_f_b", "b_w_in", "b_w_out", "f_w_in", "f_conv_w", "f_conv_b",
                "f_w_out", "ln1_g", "ln1_b", "ln2_g", "ln2_b")
BIG = ("a_w_in", "a_w_out", "kv_w", "b_w_in", "b_w_out", "f_w_in", "f_w_out")
OUT_TYPE = ("a_w_out", "b_w_out", "f_w_out")
SMALL_SHARDED = (("meta", 1), ("a_conv_w", 2), ("a_conv_b", 1), ("a_b_r", 1), ("a_b_i", 1), ("a_lambda", 1),
                 ("f_conv_w", 2))
SMALL_REPLICATED = ("a_w_r", "a_w_i", "kv_f_b", "f_conv_b", "ln1_g", "ln1_b", "ln2_g", "ln2_b")


def _train_step(x, loss_target, weights, mom_m, mom_v):
    S = N_SHARDS
    seq, d = x.shape[1], x.shape[2]
    nm = weights["meta"].shape[0]
    la = weights["a_w_in"].shape[0]
    lb = weights["b_w_in"].shape[0]
    depth = la + lb
    dr = weights["a_w_out"].shape[1] * S
    nb, bs = weights["a_w_r"].shape[1], weights["a_w_r"].shape[2]
    per_group = (LANES // math.gcd(bs, LANES))
    gs = per_group * bs
    heads = weights["kv_f_b"].shape[0]
    dff = weights["f_w_out"].shape[1] * S
    nkv = 2 * d + heads
    nkv_s = weights["kv_w"].shape[1]
    nkvp = _round_up(2 * d + LANES, 768) if 2 * d + LANES > 768 else 2 * d + LANES
    alpha = (2 * depth) ** 0.25
    xi, yi, ci = _coords()
    shard = 2 * xi + yi
    core_arr = jnp.reshape(ci, (1,)).astype(jnp.int32)
    shard_arr = jnp.reshape(shard, (1,)).astype(jnp.int32)
    shard_core_arr = jnp.stack([shard, ci]).astype(jnp.int32)

    def mixer_keys(l):
        if l < la:
            return [("a_w_in", l), ("a_w_out", l)]
        return ([("kv_w", 0)] if l == la else []) + [("b_w_in", l - la), ("b_w_out", l - la)]

    def ffn_keys(l):
        return [("f_w_in", l), ("f_w_out", l)]

    groups = [mixer_keys(0), ffn_keys(0)] + [mixer_keys(l) + ffn_keys(l) for l in range(1, depth)]
    keys = [kl for grp in groups for kl in grp]
    local2d = {(k, i): (weights[k][i] if weights[k].ndim == 3 else weights[k]) for k, i in keys}
    small_local = [weights[k] for k, _ in SMALL_SHARDED]
    sm_flat = _pack(small_local, 2 * SUBLANES * LANES).reshape(1, 2, -1, LANES)
    sm_slot = lax.dynamic_update_slice_in_dim(lax.empty((S,) + sm_flat.shape[1:], F32), sm_flat, shard, axis=0)
    parts = [[cast_into_slot(shard_arr, local2d[kl], f"cast_{kl[0]}{kl[1]}") for kl in grp] for grp in groups]
    parts[0].append(sm_slot)
    in_flight, start_token = _split_start("gather_start", parts, lambda g, refs, cnt: _gather_ici_plan(refs, cnt))
    gw = {}

    def fetch(g, after):
        arrs = _split_wait(f"gather_wait_{g}", in_flight[g], after, _gather_ici_plan)
        arrs = forward_to_sibling(arrs, f"gather_fwd_{g}")
        for kl, a in zip(groups[g], arrs):
            rows, cols = local2d[kl].shape
            gw[kl] = a.reshape(S * rows, cols) if kl[0] in OUT_TYPE else a.reshape(S, rows, cols)
        return arrs

    sm_all = fetch(0, start_token)[-1].reshape(S, -1)
    small_full = {}
    per_shard = [_unpack(sm_all[s], [a.shape for a in small_local]) for s in range(S)]
    for idx, (k, axis) in enumerate(SMALL_SHARDED):
        small_full[k] = jnp.concatenate([per_shard[s][idx] for s in range(S)], axis=axis)
    fb_pad = jnp.pad(weights["kv_f_b"], (0, LANES - heads)).reshape(1, LANES)
    wr_g = [_block_diag(weights["a_w_r"][l], per_group) for l in range(la)]
    wi_g = [_block_diag(weights["a_w_i"][l], per_group) for l in range(la)]
    row = lambda v: v.reshape(1, -1)

    h, hb = embed_fwd(small_full["meta"], x[0], "embed")
    saved = []
    kvz = ct_pad = None
    _, tp, _ = _attn_geometry(nm + seq)
    t = nm + seq
    for l in range(depth):
        sv = {"hb_in": hb}
        if l > 0:
            fetch(l + 1, hb)
        if l < la:
            gr = mm_in(hb, gw[("a_w_in", l)], F32, f"a{l}_in")
            rc, rcb = a_conv_fwd(gr, small_full["a_conv_w"][l], row(small_full["a_conv_b"][l]), f"a{l}_conv")
            r_pre, i_pre = mm_bd(rcb, wr_g[l], wi_g[l], f"a{l}_gates")
            hs, gb = a_elem_fwd(gr, rc, r_pre, i_pre, row(small_full["a_b_r"][l]), row(small_full["a_b_i"][l]),
                                row(small_full["a_lambda"][l]), f"a{l}_lru")
            mix = mm_out(gb, gw[("a_w_out", l)], f"a{l}_out")
            sv.update(gr=gr, rc=rc, rcb=rcb, r_pre=r_pre, i_pre=i_pre, hs=hs, gb=gb)
        else:
            j = l - la
            if j == 0:
                kv_cat = jnp.moveaxis(gw[("kv_w", 0)], 0, 1).reshape(d, S * nkv_s)
                kv_pad = jnp.pad(kv_cat, ((0, 0), (0, nkvp - nkv))).reshape(1, d, nkvp)
                kvz = mm_in(hb, kv_pad, F32, "kv_proj")
                cum = kv_fwd(kvz, fb_pad, d, "kv_forget")
                ct_pad = jnp.pad(cum[:, :heads].T, ((0, 0), (0, tp - t))).reshape(heads, 1, tp)
                kv_hb = hb
            qg = mm_in(hb, gw[("b_w_in", j)], F32, f"b{j}_in")
            o, mob = attn_fwd(qg, kvz, ct_pad, d, heads, f"b{j}_attn")
            mix = mm_out(mob, gw[("b_w_out", j)], f"b{j}_out")
            sv.update(qg=qg, o=o, mob=mob)
        h1, h1b, xh1, rs1 = ln_fwd(h, mix, row(weights["ln1_g"][l]), row(weights["ln1_b"][l]), alpha, f"ln1_{l}")
        if l == 0:
            fetch(1, h1b)
        zf = mm_in(h1b, gw[("f_w_in", l)], F32, f"f{l}_in")
        ffb = f_elem_fwd(zf, small_full["f_conv_w"][l], row(weights["f_conv_b"][l]), f"f{l}_act")
        ffo = mm_out(ffb, gw[("f_w_out", l)], f"f{l}_out")
        h2, h2b, xh2, rs2 = ln_fwd(h1, ffo, row(weights["ln2_g"][l]), row(weights["ln2_b"][l]), alpha, f"ln2_{l}")
        sv.update(h1b=h1b, xh1=xh1, rs1=rs1, zf=zf, ffb=ffb, xh2=xh2, rs2=rs2)
        saved.append(sv)
        h, hb = h2, h2b
    loss11, dy = loss_fwd_bwd(h, loss_target[0], nm, "loss")

    grads = {}

    def by_owner(kl, g3):
        rows, cols = local2d[kl].shape
        grads[kl] = g3.reshape(S, 2, rows // 2, cols)

    reducing = []

    def send_grads(g, names, arrays):
        from_sib = swap_halves(arrays, f"grad_swap_{g}")
        csums = [add_halves(core_arr, a, o, f"chip_sum_{g}_{i}") for i, (a, o) in enumerate(zip(arrays, from_sib))]
        lands = [lax.empty((3,) + cs.shape[1:], cs.dtype) for cs in csums]
        started, token = _split_start(f"grad_a2a_start_{g}", [csums + lands],
                                      lambda _, refs, cnt: _all_to_all_plan(refs, cnt))
        reducing.append((names, started[0]))
        return token[0:1, 0:1]

    def after_start(vec, zero):
        return vec if zero is None else vec + zero

    pin = None

    g_small = {}
    per_layer = {k: [None] * n for k, n in (
        ("a_conv_w", la), ("a_conv_b", la), ("a_w_r", la), ("a_b_r", la), ("a_w_i", la), ("a_b_i", la),
        ("a_lambda", la), ("f_conv_w", depth), ("f_conv_b", depth), ("ln1_g", depth), ("ln1_b", depth),
        ("ln2_g", depth), ("ln2_b", depth))}
    adds = [(dy, 1.0)]
    dks, dvs, dcs = [], [], []
    for l in reversed(range(depth)):
        sv = saved[l]
        ds2, ds2b, dg2, db2 = ln_bwd(adds, sv["xh2"], sv["rs2"], after_start(row(weights["ln2_g"][l]), pin),
                                     f"ln2_{l}_bwd")
        pin = None
        per_layer["ln2_g"][l], per_layer["ln2_b"][l] = dg2[0], db2[0]
        dff_v = mm_out_nt(ds2b, gw[("f_w_out", l)], f"f{l}_out_dx")
        by_owner(("f_w_out", l), mm_tn(sv["ffb"], ds2b, 1, f"f{l}_out_dw"))
        dzb, dfw, dfb_l = f_elem_bwd(sv["zf"], dff_v, small_full["f_conv_w"][l], row(weights["f_conv_b"][l]),
                                     f"f{l}_act_bwd")
        per_layer["f_conv_w"][l], per_layer["f_conv_b"][l] = dfw, dfb_l[0]
        dh1_f = mm_in_nt(dzb, gw[("f_w_in", l)], f"f{l}_in_dx")
        by_owner(("f_w_in", l), mm_tn(sv["h1b"], dzb, S, f"f{l}_in_dw"))
        if l == 0:
            pin = send_grads(1, groups[1], [grads[kl] for kl in groups[1]])
        ds1, ds1b, dg1, db1 = ln_bwd([(ds2, alpha), (dh1_f, 1.0)], sv["xh1"], sv["rs1"],
                                     after_start(row(weights["ln1_g"][l]), pin), f"ln1_{l}_bwd")
        pin = None
        per_layer["ln1_g"][l], per_layer["ln1_b"][l] = dg1[0], db1[0]
        if l < la:
            dgv = mm_out_nt(ds1b, gw[("a_w_out", l)], f"a{l}_out_dx")
            by_owner(("a_w_out", l), mm_tn(sv["gb"], ds1b, 1, f"a{l}_out_dw"))
            dgate_b, drp_b, dip_b, drc_d, dlam, dbr, dbi = a_elem_bwd(
                dgv, sv["gr"], sv["rc"], sv["r_pre"], sv["i_pre"], sv["hs"], row(small_full["a_b_r"][l]),
                row(small_full["a_b_i"][l]), row(small_full["a_lambda"][l]), f"a{l}_lru_bwd")
            drc_g = mm_bd_nt(drp_b, dip_b, wr_g[l], wi_g[l], f"a{l}_gates_dx")
            dwr_g, dwi_g = mm_bd_tn(sv["rcb"], drp_b, dip_b, gs, f"a{l}_gates_dw")
            drec_b, dcw, dcb = a_conv_bwd(drc_d, drc_g, sv["gr"], small_full["a_conv_w"][l], f"a{l}_conv_bwd")
            per_layer["a_w_r"][l] = _block_diag_extract(dwr_g, per_group, bs)
            per_layer["a_w_i"][l] = _block_diag_extract(dwi_g, per_group, bs)
            per_layer["a_lambda"][l], per_layer["a_b_r"][l], per_layer["a_b_i"][l] = dlam[0], dbr[0], dbi[0]
            per_layer["a_conv_w"][l], per_layer["a_conv_b"][l] = dcw, dcb[0]
            dgr_b = jnp.concatenate([dgate_b, drec_b], axis=1)
            dh_m = mm_in_nt(dgr_b, gw[("a_w_in", l)], f"a{l}_in_dx")
            by_owner(("a_w_in", l), mm_tn(sv["hb_in"], dgr_b, S, f"a{l}_in_dw"))
        else:
            j = l - la
            dmo = mm_out_nt(ds1b, gw[("b_w_out", j)], f"b{j}_out_dx")
            by_owner(("b_w_out", j), mm_tn(sv["mob"], ds1b, 1, f"b{j}_out_dw"))
            dq_b, dog_b, dk, dv, dct = attn_bwd(dmo, sv["qg"], kvz, sv["o"], ct_pad, d, heads, f"b{j}_attn_bwd")
            dks.append(dk)
            dvs.append(dv)
            dcs.append(jnp.pad(dct[:, 0, :t].T, ((0, 0), (0, LANES - heads))))
            dqg_b = jnp.concatenate([dq_b, dog_b], axis=1)
            dh_m = mm_in_nt(dqg_b, gw[("b_w_in", j)], f"b{j}_in_dx")
            by_owner(("b_w_in", j), mm_tn(sv["hb_in"], dqg_b, S, f"b{j}_in_dw"))
        adds = [(ds1, alpha), (dh_m, 1.0)]
        if l == la:
            dzf_b, dfb = kv_bwd(dcs, kvz, fb_pad, d, "kv_forget_bwd")
            dk_b = add_cast(dks[0], dks[1], "kv_dk") if lb == 2 else None
            dv_b = add_cast(dvs[0], dvs[1], "kv_dv") if lb == 2 else None
            dz_kv = jnp.concatenate([dk_b, dv_b, dzf_b, jnp.zeros((t, nkvp - 2 * d - LANES), BF16)], axis=1)
            dh_kv = mm_in_nt(dz_kv, kv_pad, "kv_proj_dx")
            kv_dw = mm_tn(kv_hb, dz_kv, 1, "kv_proj_dw")
            by_owner(("kv_w", 0), jnp.moveaxis(kv_dw[0, :, :nkv].reshape(d, S, nkv_s), 1, 0))
            g_small["kv_f_b"] = dfb[0, :heads]
            adds.append((dh_kv, 1.0))
        if l > 0:
            pin = send_grads(l + 1, groups[l + 1], [grads[kl] for kl in groups[l + 1]])
    g_meta, g_x = embed_bwd(adds, nm, "embed_bwd")

    small_names = list(SMALL_REPLICATED) + [k for k, _ in SMALL_SHARDED]
    g_small["meta"] = g_meta
    for k, vals in per_layer.items():
        g_small[k] = jnp.stack(vals)
    small_shapes = {k: (weights[k].shape if k in SMALL_REPLICATED else g_small[k].shape) for k in small_names}
    sm_g = _pack([g_small[k].reshape(small_shapes[k]) for k in small_names], S * 2 * SUBLANES * LANES)
    sm_g = sm_g.reshape(S, 2, -1, LANES)
    send_grads(0, groups[0] + [("small", 0)], [grads[kl] for kl in groups[0]] + [sm_g])

    fin = {"small": lax.empty((1,) + sm_g.shape[1:], F32)}
    for kl in keys:
        n_stack = weights[kl[0]].shape[0] if weights[kl[0]].ndim == 3 else 1
        rows, cols = local2d[kl].shape
        fin.setdefault(kl[0], lax.empty((n_stack, 2, rows // 2, cols), F32))
    for g, (names_g, started) in enumerate(reducing):
        arrs = _split_wait(f"grad_a2a_wait_{g}", started, g_x, _all_to_all_plan)
        half = len(names_g)
        for i, (kl, cs, rv) in enumerate(zip(names_g, arrs[:half], arrs[half:])):
            fin[kl[0]] = add_four(shard_core_arr, cs, rv, fin[kl[0]], kl[1], f"owner_sum_{g}_{i}")
    names = list(BIG) + ["small"]
    joined = dict(zip(names, join_halves([fin[k] for k in names], "grad_join")))
    sm_slot = lax.dynamic_update_slice_in_dim(lax.empty((S,) + joined["small"].shape[1:], F32), joined["small"],
                                              shard, axis=0)
    sm_red = gather_full(sm_slot, "small_gather").reshape(-1)

    out_g, out_d, out_m, out_v = {}, {}, {}, {}
    for k in BIG:
        w2 = weights[k].reshape(-1, weights[k].shape[-1])
        g2 = joined[k].reshape(w2.shape)
        dlt, mn, vn = adamw(w2, g2, mom_m[k].reshape(w2.shape), mom_v[k].reshape(w2.shape), "adamw_" + k)
        shp = weights[k].shape
        out_g[k], out_d[k], out_m[k], out_v[k] = g2.reshape(shp), dlt.reshape(shp), mn.reshape(shp), vn.reshape(shp)
    sm_vals = dict(zip(small_names, _unpack(sm_red, [small_shapes[k] for k in small_names])))
    local_small = {}
    for k in SMALL_REPLICATED:
        local_small[k] = sm_vals[k]
    for k, axis in SMALL_SHARDED:
        size = weights[k].shape[axis]
        local_small[k] = lax.dynamic_slice_in_dim(sm_vals[k], shard * size, size, axis=axis)
    for k in small_names:
        shp = weights[k].shape
        two_d = (-1, shp[-1]) if len(shp) > 1 else (1, -1)
        dlt, mn, vn = adamw(weights[k].reshape(two_d), local_small[k].reshape(two_d), mom_m[k].reshape(two_d),
                            mom_v[k].reshape(two_d), "adamw_" + k)
        out_g[k], out_d[k], out_m[k], out_v[k] = local_small[k], dlt.reshape(shp), mn.reshape(shp), vn.reshape(shp)

    loss = lax.psum(loss11[0, 0], ("x", "y", "c"))
    return (loss, g_x[None], *[out_g[k] for k in WEIGHT_ORDER], *[out_d[k] for k in WEIGHT_ORDER],
            *[out_m[k] for k in WEIGHT_ORDER], *[out_v[k] for k in WEIGHT_ORDER])
```

```python
import functools
import math

import jax
import jax.numpy as jnp
from jax import lax
from jax.experimental import pallas as pl
from jax.experimental.pallas import tpu as pltpu

F32 = jnp.float32
BF16 = jnp.bfloat16

LRU_C = 8.0
LN_EPS = 1e-5
ADAM_LR = 0.001
ADAM_B1 = 0.9
ADAM_B2 = 0.999
ADAM_EPS = 1e-08
ADAM_WD = 0.01
ADAM_STEP = 10

LANES = 128
SUBLANES = 8
V7X_VMEM_BYTES = 64 * 1024 * 1024
VMEM_LIMIT = V7X_VMEM_BYTES * 7 // 8
N_SHARDS = 4
GELU_C0 = math.sqrt(2.0 / math.pi)
GELU_C1 = 0.044715
NEG_BIG = -1e30


def _cp(*sem):
    return pltpu.CompilerParams(dimension_semantics=tuple(sem), vmem_limit_bytes=VMEM_LIMIT)


def _tile(n, cap, mult=LANES):
    best = None
    d = mult
    while d <= min(n, cap):
        if n % d == 0:
            best = d
        d += mult
    return n if best is None else best


def _row_block(t):
    if t % 3 == 0 and (t // 3) % 16 == 0:
        return t // 3
    return t


def _round_up(n, m):
    return (n + m - 1) // m * m


def _sigmoid(v):
    return 1.0 / (1.0 + jnp.exp(-v))


def _softplus(v):
    return jnp.maximum(v, 0.0) + jnp.log(1.0 + jnp.exp(-jnp.abs(v)))


def _gelu_parts(v):
    v2 = v * v
    u = GELU_C0 * (v + GELU_C1 * v * v2)
    t = jnp.tanh(u)
    g = 0.5 * v * (1.0 + t)
    dg = 0.5 * (1.0 + t) + 0.5 * v * (1.0 - t * t) * (GELU_C0 * (1.0 + 3.0 * GELU_C1 * v2))
    return g, dg


def _gelu(v):
    u = GELU_C0 * (v + GELU_C1 * v * v * v)
    return 0.5 * v * (1.0 + jnp.tanh(u))


def _neg_expm1(v):
    series = -v * (1.0 + 0.5 * v * (1.0 + (v / 3.0) * (1.0 + 0.25 * v)))
    return jnp.where(v > -0.05, series, 1.0 - jnp.exp(v))


def _shift_down(v, j):
    if j == 0:
        return v
    rows = lax.broadcasted_iota(jnp.int32, v.shape, 0)
    return jnp.where(rows >= j, pltpu.roll(v, j, 0), 0.0)


def _shift_up(v, j):
    if j == 0:
        return v
    n = v.shape[0]
    rows = lax.broadcasted_iota(jnp.int32, v.shape, 0)
    return jnp.where(rows < n - j, pltpu.roll(v, n - j, 0), 0.0)


def _scan_rows(a_ref, b_ref, out_ref, n_rows, width, reverse):
    n_groups = n_rows // SUBLANES
    rows = lax.broadcasted_iota(jnp.int32, (SUBLANES, width), 0)
    edge = 0 if reverse else SUBLANES - 1

    def body(g, carry):
        grp = (n_groups - 1 - g) if reverse else g
        off = pl.multiple_of(grp * SUBLANES, SUBLANES)
        b = b_ref[pl.ds(off, SUBLANES), :]
        a = None if a_ref is None else a_ref[pl.ds(off, SUBLANES), :]
        for d in (1, 2, 4):
            if reverse:
                keep = rows < SUBLANES - d
                sh = SUBLANES - d
            else:
                keep = rows >= d
                sh = d
            b_s = jnp.where(keep, pltpu.roll(b, sh, 0), 0.0)
            if a is None:
                b = b + b_s
            else:
                a_s = jnp.where(keep, pltpu.roll(a, sh, 0), 1.0)
                b = a * b_s + b
                a = a * a_s
        h = b + carry if a is None else b + a * carry
        out_ref[pl.ds(off, SUBLANES), :] = h
        return jnp.sum(jnp.where(rows == edge, h, 0.0), axis=0, keepdims=True)

    lax.fori_loop(0, n_groups, body, jnp.zeros((1, width), F32), unroll=2)


def mm_in(x, w, out_dtype, name):
    t, k = x.shape
    s_n, _, ns = w.shape
    tn = _tile(ns, 1408)
    nj = ns // tn
    rb = _row_block(t)

    def body(x_ref, w_ref, o_ref):
        o_ref[...] = jnp.dot(x_ref[...], w_ref[...], preferred_element_type=F32).astype(o_ref.dtype)

    return pl.pallas_call(
        body, name=name, grid=(s_n, nj, t // rb),
        in_specs=[pl.BlockSpec((rb, k), lambda s, j, r: (r, 0)),
                  pl.BlockSpec((None, k, tn), lambda s, j, r: (s, 0, j))],
        out_specs=pl.BlockSpec((rb, tn), lambda s, j, r: (r, s * nj + j)),
        out_shape=jax.ShapeDtypeStruct((t, s_n * ns), out_dtype),
        compiler_params=_cp("parallel", "parallel", "parallel"))(x, w)


def mm_out(x, w, name):
    t, k = x.shape
    n = w.shape[1]
    rb = _row_block(t)

    def body(x_ref, w_ref, o_ref):
        o_ref[...] = jnp.dot(x_ref[...], w_ref[...], preferred_element_type=F32)

    return pl.pallas_call(
        body, name=name, grid=(t // rb,),
        in_specs=[pl.BlockSpec((rb, k), lambda r: (r, 0)), pl.BlockSpec((k, n), lambda r: (0, 0))],
        out_specs=pl.BlockSpec((rb, n), lambda r: (r, 0)),
        out_shape=jax.ShapeDtypeStruct((t, n), F32),
        compiler_params=_cp("parallel"))(x, w)


def mm_in_nt(dy, w, name):
    t, _ = dy.shape
    s_n, k, ns = w.shape
    tn = _tile(ns, 1408)
    nj = ns // tn
    rb = _row_block(t)

    def body(dy_ref, w_ref, o_ref):
        @pl.when((pl.program_id(1) == 0) & (pl.program_id(2) == 0))
        def _():
            o_ref[...] = jnp.zeros_like(o_ref)
        o_ref[...] += lax.dot_general(dy_ref[...], w_ref[...], (((1,), (1,)), ((), ())),
                                      preferred_element_type=F32)

    return pl.pallas_call(
        body, name=name, grid=(t // rb, s_n, nj),
        in_specs=[pl.BlockSpec((rb, tn), lambda r, s, j: (r, s * nj + j)),
                  pl.BlockSpec((None, k, tn), lambda r, s, j: (s, 0, j))],
        out_specs=pl.BlockSpec((rb, k), lambda r, s, j: (r, 0)),
        out_shape=jax.ShapeDtypeStruct((t, k), F32),
        compiler_params=_cp("parallel", "arbitrary", "arbitrary"))(dy, w)


def mm_out_nt(dy, w, name):
    t, n = dy.shape
    k = w.shape[0]
    rb = _row_block(t)

    def body(dy_ref, w_ref, o_ref):
        o_ref[...] = lax.dot_general(dy_ref[...], w_ref[...], (((1,), (1,)), ((), ())),
                                     preferred_element_type=F32)

    return pl.pallas_call(
        body, name=name, grid=(t // rb,),
        in_specs=[pl.BlockSpec((rb, n), lambda r: (r, 0)), pl.BlockSpec((k, n), lambda r: (0, 0))],
        out_specs=pl.BlockSpec((rb, k), lambda r: (r, 0)),
        out_shape=jax.ShapeDtypeStruct((t, k), F32),
        compiler_params=_cp("parallel"))(dy, w)


def mm_tn(x, dy, s_n, name):
    t, kb = x.shape
    nb = dy.shape[1] // s_n
    tk = _tile(kb, 1408)
    tn = _tile(nb, 1408)
    nkb, nnb = kb // tk, nb // tn

    def body(x_ref, dy_ref, o_ref):
        o_ref[...] = lax.dot_general(x_ref[...], dy_ref[...], (((0,), (0,)), ((), ())),
                                     preferred_element_type=F32).astype(o_ref.dtype)

    return pl.pallas_call(
        body, name=name, grid=(s_n, nkb, nnb),
        in_specs=[pl.BlockSpec((t, tk), lambda s, a, b: (0, a)),
                  pl.BlockSpec((t, tn), lambda s, a, b: (0, s * nnb + b))],
        out_specs=pl.BlockSpec((None, tk, tn), lambda s, a, b: (s, a, b)),
        out_shape=jax.ShapeDtypeStruct((s_n, kb, nb), BF16),
        compiler_params=_cp("parallel", "parallel", "parallel"))(x, dy)


def mm_bd(x, wr, wi, name):
    t, _ = x.shape
    g_n, gs, _ = wr.shape
    rb = _row_block(t)

    def body(x_ref, wr_ref, wi_ref, r_ref, i_ref):
        xv = x_ref[...]
        r_ref[...] = jnp.dot(xv, wr_ref[...], preferred_element_type=F32)
        i_ref[...] = jnp.dot(xv, wi_ref[...], preferred_element_type=F32)

    blk = pl.BlockSpec((rb, gs), lambda g, r: (r, g))
    wspec = pl.BlockSpec((None, gs, gs), lambda g, r: (g, 0, 0))
    return pl.pallas_call(
        body, name=name, grid=(g_n, t // rb), in_specs=[blk, wspec, wspec], out_specs=[blk, blk],
        out_shape=[jax.ShapeDtypeStruct((t, g_n * gs), F32)] * 2,
        compiler_params=_cp("parallel", "parallel"))(x, wr, wi)


def mm_bd_nt(dr, di, wr, wi, name):
    t, _ = dr.shape
    g_n, gs, _ = wr.shape
    rb = _row_block(t)
    nt = (((1,), (1,)), ((), ()))

    def body(dr_ref, di_ref, wr_ref, wi_ref, o_ref):
        o_ref[...] = (lax.dot_general(dr_ref[...], wr_ref[...], nt, preferred_element_type=F32)
                      + lax.dot_general(di_ref[...], wi_ref[...], nt, preferred_element_type=F32))

    blk = pl.BlockSpec((rb, gs), lambda g, r: (r, g))
    wspec = pl.BlockSpec((None, gs, gs), lambda g, r: (g, 0, 0))
    return pl.pallas_call(
        body, name=name, grid=(g_n, t // rb), in_specs=[blk, blk, wspec, wspec], out_specs=blk,
        out_shape=jax.ShapeDtypeStruct((t, g_n * gs), F32),
        compiler_params=_cp("parallel", "parallel"))(dr, di, wr, wi)


def mm_bd_tn(x, dr, di, gs, name):
    t, w = x.shape
    g_n = w // gs
    tn_dims = (((0,), (0,)), ((), ()))

    def body(x_ref, dr_ref, di_ref, gr_ref, gi_ref):
        xv = x_ref[...]
        gr_ref[...] = lax.dot_general(xv, dr_ref[...], tn_dims, preferred_element_type=F32)
        gi_ref[...] = lax.dot_general(xv, di_ref[...], tn_dims, preferred_element_type=F32)

    blk = pl.BlockSpec((t, gs), lambda g: (0, g))
    ospec = pl.BlockSpec((None, gs, gs), lambda g: (g, 0, 0))
    return pl.pallas_call(
        body, name=name, grid=(g_n,), in_specs=[blk, blk, blk], out_specs=[ospec, ospec],
        out_shape=[jax.ShapeDtypeStruct((g_n, gs, gs), F32)] * 2,
        compiler_params=_cp("parallel"))(x, dr, di)


def embed_fwd(meta, x2d, name):
    nm, d = meta.shape
    seq = x2d.shape[0]
    t = nm + seq
    cb = _tile(d, 256)

    def body(m_ref, x_ref, h_ref, hb_ref):
        h_ref[pl.ds(0, nm), :] = m_ref[...]
        h_ref[pl.ds(nm, seq), :] = x_ref[...]
        hb_ref[pl.ds(0, nm), :] = m_ref[...].astype(BF16)
        hb_ref[pl.ds(nm, seq), :] = x_ref[...].astype(BF16)

    return pl.pallas_call(
        body, name=name, grid=(d // cb,),
        in_specs=[pl.BlockSpec((nm, cb), lambda j: (0, j)), pl.BlockSpec((seq, cb), lambda j: (0, j))],
        out_specs=[pl.BlockSpec((t, cb), lambda j: (0, j))] * 2,
        out_shape=[jax.ShapeDtypeStruct((t, d), F32), jax.ShapeDtypeStruct((t, d), BF16)],
        compiler_params=_cp("parallel"))(meta, x2d)


def embed_bwd(adds, nm, name):
    t, d = adds[0][0].shape
    seq = t - nm
    cb = _tile(d, 256)
    scales = [s for _, s in adds]
    n = len(adds)

    def body(*refs):
        tot = None
        for r, sc in zip(refs[:n], scales):
            term = r[...] if sc == 1.0 else sc * r[...]
            tot = term if tot is None else tot + term
        gm_ref, gx_ref = refs[n], refs[n + 1]
        gm_ref[...] = tot[0:nm]
        gx_ref[...] = tot[nm:t]

    return pl.pallas_call(
        body, name=name, grid=(d // cb,),
        in_specs=[pl.BlockSpec((t, cb), lambda j: (0, j))] * n,
        out_specs=[pl.BlockSpec((nm, cb), lambda j: (0, j)), pl.BlockSpec((seq, cb), lambda j: (0, j))],
        out_shape=[jax.ShapeDtypeStruct((nm, d), F32), jax.ShapeDtypeStruct((seq, d), F32)],
        compiler_params=_cp("parallel"))(*[a for a, _ in adds])


def loss_fwd_bwd(h, tgt, nm, name):
    t, d = h.shape
    seq = t - nm
    cb = _tile(d, 256)
    inv_d = 1.0 / d

    def body(h_ref, t_ref, loss_ref, dy_ref):
        @pl.when(pl.program_id(0) == 0)
        def _():
            loss_ref[...] = jnp.zeros_like(loss_ref)
        err = h_ref[pl.ds(nm, seq), :] - t_ref[...]
        dy_ref[pl.ds(0, nm), :] = jnp.zeros((nm, cb), F32)
        dy_ref[pl.ds(nm, seq), :] = err * inv_d
        loss_ref[...] += (0.5 * inv_d) * jnp.sum(err * err, keepdims=True)

    return pl.pallas_call(
        body, name=name, grid=(d // cb,),
        in_specs=[pl.BlockSpec((t, cb), lambda j: (0, j)), pl.BlockSpec((seq, cb), lambda j: (0, j))],
        out_specs=[pl.BlockSpec((1, 1), lambda j: (0, 0)), pl.BlockSpec((t, cb), lambda j: (0, j))],
        out_shape=[jax.ShapeDtypeStruct((1, 1), F32), jax.ShapeDtypeStruct((t, d), F32)],
        compiler_params=_cp("arbitrary"))(h, tgt)


def ln_fwd(h, mix, g, b, alpha, name):
    t, d = h.shape
    rb = _row_block(t)

    def body(h_ref, m_ref, g_ref, b_ref, y_ref, yb_ref, xh_ref, rs_ref):
        s = alpha * h_ref[...] + m_ref[...]
        mu = jnp.mean(s, axis=-1, keepdims=True)
        c = s - mu
        var = jnp.mean(c * c, axis=-1, keepdims=True)
        rstd = lax.rsqrt(var + LN_EPS)
        xh = c * rstd
        y = xh * g_ref[...] + b_ref[...]
        y_ref[...] = y
        yb_ref[...] = y.astype(BF16)
        xh_ref[...] = xh
        rs_ref[...] = rstd

    row = pl.BlockSpec((rb, d), lambda r: (r, 0))
    vec = pl.BlockSpec((1, d), lambda r: (0, 0))
    return pl.pallas_call(
        body, name=name, grid=(t // rb,), in_specs=[row, row, vec, vec],
        out_specs=[row, row, row, pl.BlockSpec((rb, 1), lambda r: (r, 0))],
        out_shape=[jax.ShapeDtypeStruct((t, d), F32), jax.ShapeDtypeStruct((t, d), BF16),
                   jax.ShapeDtypeStruct((t, d), F32), jax.ShapeDtypeStruct((t, 1), F32)],
        compiler_params=_cp("parallel"))(h, mix, g, b)


def ln_bwd(adds, xhat, rstd, g, name):
    t, d = xhat.shape
    rb = _row_block(t)
    scales = [s for _, s in adds]
    n = len(adds)

    def body(*refs):
        xh_ref, rs_ref, g_ref = refs[n:n + 3]
        ds_ref, dsb_ref, dg_ref, db_ref = refs[n + 3:]
        dy = None
        for r, sc in zip(refs[:n], scales):
            term = r[...] if sc == 1.0 else sc * r[...]
            dy = term if dy is None else dy + term

        @pl.when(pl.program_id(0) == 0)
        def _():
            dg_ref[...] = jnp.zeros_like(dg_ref)
            db_ref[...] = jnp.zeros_like(db_ref)

        xh = xh_ref[...]
        dxh = dy * g_ref[...]
        m1 = jnp.mean(dxh, axis=-1, keepdims=True)
        m2 = jnp.mean(dxh * xh, axis=-1, keepdims=True)
        ds = rs_ref[...] * (dxh - m1 - xh * m2)
        ds_ref[...] = ds
        dsb_ref[...] = ds.astype(BF16)
        dg_ref[...] += jnp.sum(dy * xh, axis=0, keepdims=True)
        db_ref[...] += jnp.sum(dy, axis=0, keepdims=True)

    row = pl.BlockSpec((rb, d), lambda r: (r, 0))
    vec = pl.BlockSpec((1, d), lambda r: (0, 0))
    return pl.pallas_call(
        body, name=name, grid=(t // rb,),
        in_specs=[row] * n + [row, pl.BlockSpec((rb, 1), lambda r: (r, 0)), vec],
        out_specs=[row, row, vec, vec],
        out_shape=[jax.ShapeDtypeStruct((t, d), F32), jax.ShapeDtypeStruct((t, d), BF16),
                   jax.ShapeDtypeStruct((1, d), F32), jax.ShapeDtypeStruct((1, d), F32)],
        compiler_params=_cp("arbitrary"))(*[a for a, _ in adds], xhat, rstd, g)


def _conv_fwd_val(xv, w_ref, b_ref, width):
    y = b_ref[...]
    for j in range(width):
        y = y + _shift_down(xv, j) * w_ref[pl.ds(width - 1 - j, 1), :]
    return y


def _conv_bwd_val(dout, xv, w_ref, width):
    dx = None
    dws = [None] * width
    for j in range(width):
        k = width - 1 - j
        up = _shift_up(dout, j)
        term = up * w_ref[pl.ds(k, 1), :]
        dx = term if dx is None else dx + term
        dws[k] = jnp.sum(up * xv, axis=0, keepdims=True)
    return dx, dws, jnp.sum(dout, axis=0, keepdims=True)


def a_conv_fwd(gr, cw, cbias, name):
    t, two_dr = gr.shape
    dr = two_dr // 2
    width = cw.shape[0]
    cb = _tile(dr, 256)
    off = dr // cb

    def body(x_ref, w_ref, b_ref, rc_ref, rcb_ref):
        y = _conv_fwd_val(x_ref[...], w_ref, b_ref, width)
        rc_ref[...] = y
        rcb_ref[...] = y.astype(BF16)

    return pl.pallas_call(
        body, name=name, grid=(dr // cb,),
        in_specs=[pl.BlockSpec((t, cb), lambda j: (0, off + j)),
                  pl.BlockSpec((width, cb), lambda j: (0, j)), pl.BlockSpec((1, cb), lambda j: (0, j))],
        out_specs=[pl.BlockSpec((t, cb), lambda j: (0, j))] * 2,
        out_shape=[jax.ShapeDtypeStruct((t, dr), F32), jax.ShapeDtypeStruct((t, dr), BF16)],
        compiler_params=_cp("parallel"))(gr, cw, cbias)


def a_conv_bwd(drc_a, drc_b, gr, cw, dgr, name):
    t, two_dr = gr.shape
    dr = two_dr // 2
    width = cw.shape[0]
    cb = _tile(dr, 256)
    off = dr // cb

    def body(da_ref, db_ref, x_ref, w_ref, dgr_in, dx_ref, dw_ref, dbias_ref):
        del dgr_in
        dout = da_ref[...] + db_ref[...]
        dx, dws, dbias = _conv_bwd_val(dout, x_ref[...], w_ref, width)
        dx_ref[...] = dx.astype(BF16)
        for k in range(width):
            dw_ref[pl.ds(k, 1), :] = dws[k]
        dbias_ref[...] = dbias

    col = pl.BlockSpec((t, cb), lambda j: (0, j))
    return pl.pallas_call(
        body, name=name, grid=(dr // cb,),
        in_specs=[col, col, pl.BlockSpec((t, cb), lambda j: (0, off + j)),
                  pl.BlockSpec((width, cb), lambda j: (0, j)), pl.BlockSpec(memory_space=pl.ANY)],
        out_specs=[pl.BlockSpec((t, cb), lambda j: (0, off + j)), pl.BlockSpec((width, cb), lambda j: (0, j)),
                   pl.BlockSpec((1, cb), lambda j: (0, j))],
        out_shape=[jax.ShapeDtypeStruct((t, two_dr), BF16), jax.ShapeDtypeStruct((width, dr), F32),
                   jax.ShapeDtypeStruct((1, dr), F32)],
        input_output_aliases={4: 0},
        compiler_params=_cp("parallel"))(drc_a, drc_b, gr, cw, dgr)


def _lru_gates(r_pre, i_pre, br, bi, lam):
    r = _sigmoid(r_pre + br)
    i = _sigmoid(i_pre + bi)
    sp = _softplus(-lam)
    la = -LRU_C * r * sp
    a = jnp.exp(la)
    m = jnp.sqrt(_neg_expm1(2.0 * la))
    return r, i, sp, la, a, m


def a_elem_fwd(gr, rc, r_pre, i_pre, br, bi, lam, name):
    t, dr = rc.shape
    cb = _tile(dr, 2 * LANES)
    rb = _row_block(t)
    chunks = [pl.ds(r * rb, rb) for r in range(t // rb)]

    def body(gate_ref, rc_ref, rp_ref, ip_ref, br_ref, bi_ref, lam_ref, hs_ref, g_ref, a_s, u_s):
        for rows in chunks:
            _, i, _, _, a, m = _lru_gates(rp_ref[rows, :], ip_ref[rows, :], br_ref[...], bi_ref[...], lam_ref[...])
            a_s[rows, :] = a
            u_s[rows, :] = m * (i * rc_ref[rows, :])
        _scan_rows(a_s, u_s, hs_ref, t, cb, reverse=False)
        for rows in chunks:
            g_ref[rows, :] = (_gelu(gate_ref[rows, :]) * hs_ref[rows, :]).astype(BF16)

    col = pl.BlockSpec((t, cb), lambda j: (0, j))
    vec = pl.BlockSpec((1, cb), lambda j: (0, j))
    return pl.pallas_call(
        body, name=name, grid=(dr // cb,),
        in_specs=[col, col, col, col, vec, vec, vec],
        out_specs=[col, col],
        out_shape=[jax.ShapeDtypeStruct((t, dr), F32), jax.ShapeDtypeStruct((t, dr), BF16)],
        scratch_shapes=[pltpu.VMEM((t, cb), F32), pltpu.VMEM((t, cb), F32)],
        compiler_params=_cp("parallel"))(gr, rc, r_pre, i_pre, br, bi, lam)


def a_elem_bwd(dg, gr, rc, r_pre, i_pre, hs, br, bi, lam, name):
    t, dr = rc.shape
    cb = _tile(dr, 2 * LANES)
    rb = _row_block(t)
    chunks = [pl.ds(r * rb, rb) for r in range(t // rb)]

    def body(dg_ref, gate_ref, rc_ref, rp_ref, ip_ref, hs_ref, br_ref, bi_ref, lam_ref,
             dgate_ref, dr_ref, di_ref, drc_ref, dlam_ref, dbr_ref, dbi_ref, a_s, b_s, g_s, hp_s):
        lamv = lam_ref[...]
        gates = lambda rows: _lru_gates(rp_ref[rows, :], ip_ref[rows, :], br_ref[...], bi_ref[...], lamv)
        for rows in chunks:
            a_s[rows, :] = gates(rows)[4]
            ge, dge = _gelu_parts(gate_ref[rows, :])
            dgv = dg_ref[rows, :]
            dgate_ref[rows, :] = (dgv * hs_ref[rows, :] * dge).astype(BF16)
            b_s[rows, :] = dgv * ge
        a_s[...] = _shift_up(a_s[...], 1)
        hp_s[...] = _shift_down(hs_ref[...], 1)
        _scan_rows(a_s, b_s, g_s, t, cb, reverse=True)
        dsp = dbr = dbi = jnp.zeros((1, cb), F32)
        for rows in chunks:
            r, i, sp, _, a, m = gates(rows)
            rcv = rc_ref[rows, :]
            gsum = g_s[rows, :]
            da = gsum * hp_s[rows, :]
            dm = gsum * (i * rcv)
            d_i = gsum * m * rcv
            drc_ref[rows, :] = gsum * m * i
            dla = a * da - dm * (a * a) / m
            d_r = (-LRU_C) * sp * dla
            dsp = dsp + jnp.sum((-LRU_C) * r * dla, axis=0, keepdims=True)
            d_rp = d_r * r * (1.0 - r)
            d_ip = d_i * i * (1.0 - i)
            dr_ref[rows, :] = d_rp.astype(BF16)
            di_ref[rows, :] = d_ip.astype(BF16)
            dbr = dbr + jnp.sum(d_rp, axis=0, keepdims=True)
            dbi = dbi + jnp.sum(d_ip, axis=0, keepdims=True)
        dlam_ref[...] = -dsp * _sigmoid(-lamv)
        dbr_ref[...] = dbr
        dbi_ref[...] = dbi

    col = pl.BlockSpec((t, cb), lambda j: (0, j))
    vec = pl.BlockSpec((1, cb), lambda j: (0, j))
    big_b = jax.ShapeDtypeStruct((t, dr), BF16)
    vec_s = jax.ShapeDtypeStruct((1, dr), F32)
    return pl.pallas_call(
        body, name=name, grid=(dr // cb,),
        in_specs=[col, col, col, col, col, col, vec, vec, vec],
        out_specs=[col, col, col, col, vec, vec, vec],
        out_shape=[jax.ShapeDtypeStruct((t, 2 * dr), BF16), big_b, big_b, jax.ShapeDtypeStruct((t, dr), F32),
                   vec_s, vec_s, vec_s],
        scratch_shapes=[pltpu.VMEM((t, cb), F32)] * 4,
        compiler_params=_cp("parallel"))(dg, gr, rc, r_pre, i_pre, hs, br, bi, lam)


def f_elem_fwd(z, cw, cbias, name):
    t, two_f = z.shape
    dff = two_f // 2
    width = cw.shape[0]
    cb = _tile(dff, 256)
    off = dff // cb

    def body(zg_ref, zv_ref, wg_ref, wv_ref, bg_ref, bv_ref, o_ref):
        zcg = _conv_fwd_val(zg_ref[...], wg_ref, bg_ref, width)
        zcv = _conv_fwd_val(zv_ref[...], wv_ref, bv_ref, width)
        o_ref[...] = (_gelu(zcg) * zcv).astype(BF16)

    lo = lambda j: (0, j)
    hi = lambda j: (0, off + j)
    return pl.pallas_call(
        body, name=name, grid=(dff // cb,),
        in_specs=[pl.BlockSpec((t, cb), lo), pl.BlockSpec((t, cb), hi),
                  pl.BlockSpec((width, cb), lo), pl.BlockSpec((width, cb), hi),
                  pl.BlockSpec((1, cb), lo), pl.BlockSpec((1, cb), hi)],
        out_specs=pl.BlockSpec((t, cb), lo),
        out_shape=jax.ShapeDtypeStruct((t, dff), BF16),
        compiler_params=_cp("parallel"))(z, z, cw, cw, cbias, cbias)


def f_elem_bwd(z, dff_g, cw, cbias, name):
    t, two_f = z.shape
    dff = two_f // 2
    width = cw.shape[0]
    cb = _tile(dff, 256)
    off = dff // cb
    lo = lambda j: (0, j)
    hi = lambda j: (0, off + j)
    col, wsp, vsp = (lambda m: pl.BlockSpec((t, cb), m)), (lambda m: pl.BlockSpec((width, cb), m)), (
        lambda m: pl.BlockSpec((1, cb), m))
    shapes = [jax.ShapeDtypeStruct((t, two_f), BF16), jax.ShapeDtypeStruct((width, two_f), F32),
              jax.ShapeDtypeStruct((1, two_f), F32)]

    def emit(dout, xv, w_ref, dz_ref, dw_ref, db_ref):
        dx, dws, dbias = _conv_bwd_val(dout, xv, w_ref, width)
        dz_ref[...] = dx.astype(BF16)
        for k in range(width):
            dw_ref[pl.ds(k, 1), :] = dws[k]
        db_ref[...] = dbias

    def gate_body(zg_ref, zv_ref, d_ref, wg_ref, wv_ref, bg_ref, bv_ref, dz_ref, dw_ref, db_ref, dval_ref):
        zg = zg_ref[...]
        zcg = _conv_fwd_val(zg, wg_ref, bg_ref, width)
        zcv = _conv_fwd_val(zv_ref[...], wv_ref, bv_ref, width)
        ge, dge = _gelu_parts(zcg)
        dv = d_ref[...]
        dval_ref[...] = dv * ge
        emit(dv * zcv * dge, zg, wg_ref, dz_ref, dw_ref, db_ref)

    dz, dw, db, dval = pl.pallas_call(
        gate_body, name=name, grid=(off,),
        in_specs=[col(lo), col(hi), col(lo), wsp(lo), wsp(hi), vsp(lo), vsp(hi)],
        out_specs=[col(lo), wsp(lo), vsp(lo), col(lo)],
        out_shape=shapes + [jax.ShapeDtypeStruct((t, dff), F32)],
        compiler_params=_cp("parallel"))(z, z, dff_g, cw, cw, cbias, cbias)

    def val_body(dval_ref, zv_ref, wv_ref, dz_in, dw_in, db_in, dz_ref, dw_ref, db_ref):
        del dz_in, dw_in, db_in
        emit(dval_ref[...], zv_ref[...], wv_ref, dz_ref, dw_ref, db_ref)

    hbm = pl.BlockSpec(memory_space=pl.ANY)
    return pl.pallas_call(
        val_body, name=name + "_val", grid=(off,),
        in_specs=[col(lo), col(hi), wsp(hi), hbm, hbm, hbm],
        out_specs=[col(hi), wsp(hi), vsp(hi)], out_shape=shapes,
        input_output_aliases={3: 0, 4: 1, 5: 2},
        compiler_params=_cp("parallel"))(dval, z, cw, dz, dw, db)


def kv_fwd(z, fb, d_model, name):
    t, _ = z.shape
    blk = 2 * d_model // LANES

    def body(z_ref, fb_ref, c_ref, lf_s):
        v = z_ref[...] + fb_ref[...]
        lf_s[...] = -_softplus(-v)
        _scan_rows(None, lf_s, c_ref, t, LANES, reverse=False)

    return pl.pallas_call(
        body, name=name, grid=(1,),
        in_specs=[pl.BlockSpec((t, LANES), lambda j: (0, blk)), pl.BlockSpec((1, LANES), lambda j: (0, 0))],
        out_specs=pl.BlockSpec((t, LANES), lambda j: (0, 0)),
        out_shape=jax.ShapeDtypeStruct((t, LANES), F32),
        scratch_shapes=[pltpu.VMEM((t, LANES), F32)],
        compiler_params=_cp("arbitrary"))(z, fb)


def kv_bwd(dcs, z, fb, d_model, name):
    t, _ = z.shape
    blk = 2 * d_model // LANES
    n = len(dcs)

    def body(*refs):
        z_ref, fb_ref, dz_ref, dfb_ref, dc_s, dl_s = refs[n:]
        tot = refs[0][...]
        for r in refs[1:n]:
            tot = tot + r[...]
        dc_s[...] = tot
        _scan_rows(None, dc_s, dl_s, t, LANES, reverse=True)
        v = z_ref[...] + fb_ref[...]
        dz = dl_s[...] * _sigmoid(-v)
        dz_ref[...] = dz.astype(BF16)
        dfb_ref[...] = jnp.sum(dz, axis=0, keepdims=True)

    full = pl.BlockSpec((t, LANES), lambda j: (0, 0))
    return pl.pallas_call(
        body, name=name, grid=(1,),
        in_specs=[full] * n + [pl.BlockSpec((t, LANES), lambda j: (0, blk)),
                               pl.BlockSpec((1, LANES), lambda j: (0, 0))],
        out_specs=[full, pl.BlockSpec((1, LANES), lambda j: (0, 0))],
        out_shape=[jax.ShapeDtypeStruct((t, LANES), BF16), jax.ShapeDtypeStruct((1, LANES), F32)],
        scratch_shapes=[pltpu.VMEM((t, LANES), F32)] * 2,
        compiler_params=_cp("arbitrary"))(*dcs, z, fb)


def add_cast(a, b, name):
    t, d = a.shape
    cb = _tile(d, 512)

    def body(a_ref, b_ref, o_ref):
        o_ref[...] = (a_ref[...] + b_ref[...]).astype(BF16)

    col = pl.BlockSpec((t, cb), lambda j: (0, j))
    return pl.pallas_call(body, name=name, grid=(d // cb,), in_specs=[col, col], out_specs=col,
                          out_shape=jax.ShapeDtypeStruct((t, d), BF16),
                          compiler_params=_cp("parallel"))(a, b)


def _attn_geometry(t):
    nqb = 6 if t > 1024 else 2
    tp = _round_up(t, LANES * nqb)
    return nqb, tp, tp // nqb


def _attn_scales(dh):
    scale = dh ** -0.5
    if math.log2(scale).is_integer():
        return scale, 1.0
    return 1.0, scale


def _attn_pieces(qs, ks, crow, j, i, tq, dh, s_mul):
    r0 = i * tq
    lanes = pl.ds(j * dh, dh)
    qi = qs[pl.ds(r0, tq), lanes]
    spans = ([(0, r0)] if i > 0 else []) + [(r0, tq)]
    logits = []
    for k0, n in spans:
        s = lax.dot_general(qi, ks[pl.ds(k0, n), lanes], (((1,), (1,)), ((), ())),
                            preferred_element_type=F32)
        if s_mul != 1.0:
            s = s * s_mul
        s = s - crow[:, k0:k0 + n]
        if k0 == r0:
            rows = lax.broadcasted_iota(jnp.int32, (tq, tq), 0)
            cols = lax.broadcasted_iota(jnp.int32, (tq, tq), 1)
            s = jnp.where(cols <= rows, s, NEG_BIG)
        logits.append(s)
    mx = jnp.max(logits[0], axis=1, keepdims=True)
    for s in logits[1:]:
        mx = jnp.maximum(mx, jnp.max(s, axis=1, keepdims=True))
    es = [jnp.exp(s - mx) for s in logits]
    tot = jnp.sum(es[0], axis=1, keepdims=True)
    for e in es[1:]:
        tot = tot + jnp.sum(e, axis=1, keepdims=True)
    inv = 1.0 / tot
    return [(k0, n, e * inv) for (k0, n), e in zip(spans, es)], qi


def attn_fwd(qg, z, ct_pad, d_model, n_heads, name):
    t = qg.shape[0]
    dh = d_model // n_heads
    hp = LANES // dh
    nqb, tp, tq = _attn_geometry(t)
    nblk = d_model // LANES
    q_mul, s_mul = _attn_scales(dh)

    def body(q_ref, og_ref, k_ref, v_ref, ct_ref, o_ref, mo_ref, qs, ks, vs, os_):
        pad = jnp.zeros((tp - t, LANES), BF16)
        qs[pl.ds(0, t), :] = (q_ref[...] * q_mul).astype(BF16)
        qs[pl.ds(t, tp - t), :] = pad
        for src, dst in ((k_ref, ks), (v_ref, vs)):
            dst[pl.ds(0, t), :] = src[...].astype(BF16)
            dst[pl.ds(t, tp - t), :] = pad
        for j in range(hp):
            crow = ct_ref[j]
            lanes = pl.ds(j * dh, dh)
            for i in range(nqb):
                pieces, _ = _attn_pieces(qs, ks, crow, j, i, tq, dh, s_mul)
                acc = None
                for k0, n, p in pieces:
                    part = jnp.dot(p.astype(BF16), vs[pl.ds(k0, n), lanes], preferred_element_type=F32)
                    acc = part if acc is None else acc + part
                os_[pl.ds(i * tq, tq), lanes] = acc
        o = os_[pl.ds(0, t), :]
        o_ref[...] = o
        mo_ref[...] = (o * _sigmoid(og_ref[...])).astype(BF16)

    col = lambda off: pl.BlockSpec((t, LANES), lambda p: (0, off + p))
    return pl.pallas_call(
        body, name=name, grid=(nblk,),
        in_specs=[col(0), col(nblk), col(0), col(nblk), pl.BlockSpec((hp, 1, tp), lambda p: (p, 0, 0))],
        out_specs=[col(0), col(0)],
        out_shape=[jax.ShapeDtypeStruct((t, d_model), F32), jax.ShapeDtypeStruct((t, d_model), BF16)],
        scratch_shapes=[pltpu.VMEM((tp, LANES), BF16)] * 3 + [pltpu.VMEM((tp, LANES), F32)],
        compiler_params=_cp("parallel"))(qg, qg, z, z, ct_pad)


def attn_bwd(dmo, qg, z, o, ct_pad, d_model, n_heads, name):
    t = qg.shape[0]
    dh = d_model // n_heads
    hp = LANES // dh
    nqb, tp, tq = _attn_geometry(t)
    nblk = d_model // LANES
    q_mul, s_mul = _attn_scales(dh)
    scale = dh ** -0.5
    tn_dims = (((0,), (0,)), ((), ()))
    nt_dims = (((1,), (1,)), ((), ()))

    def body(dmo_ref, q_ref, og_ref, k_ref, v_ref, o_ref, ct_ref,
             dq_ref, dog_ref, dk_ref, dv_ref, dct_ref, qs, ks, vs, dos, dqs, dks, dvs):
        pad = jnp.zeros((tp - t, LANES), BF16)
        sg = _sigmoid(og_ref[...])
        dmo_v = dmo_ref[...]
        dog_ref[...] = (dmo_v * o_ref[...] * sg * (1.0 - sg)).astype(BF16)
        dos[pl.ds(0, t), :] = (dmo_v * sg).astype(BF16)
        dos[pl.ds(t, tp - t), :] = pad
        qs[pl.ds(0, t), :] = (q_ref[...] * q_mul).astype(BF16)
        qs[pl.ds(t, tp - t), :] = pad
        for src, dst in ((k_ref, ks), (v_ref, vs)):
            dst[pl.ds(0, t), :] = src[...].astype(BF16)
            dst[pl.ds(t, tp - t), :] = pad
        dks[...] = jnp.zeros_like(dks)
        dvs[...] = jnp.zeros_like(dvs)
        dct_ref[...] = jnp.zeros_like(dct_ref)
        for j in range(hp):
            crow = ct_ref[j]
            lanes = pl.ds(j * dh, dh)
            for i in range(nqb):
                pieces, qi = _attn_pieces(qs, ks, crow, j, i, tq, dh, s_mul)
                do_i = dos[pl.ds(i * tq, tq), lanes]
                dps = [lax.dot_general(do_i, vs[pl.ds(k0, n), lanes], nt_dims, preferred_element_type=F32)
                       for k0, n, _ in pieces]
                row = None
                for (_, _, p), dp in zip(pieces, dps):
                    part = jnp.sum(p * dp, axis=1, keepdims=True)
                    row = part if row is None else row + part
                dq_i = None
                for (k0, n, p), dp in zip(pieces, dps):
                    ds = p * (dp - row)
                    ds_b = ds.astype(BF16)
                    keys = pl.ds(k0, n)
                    part = jnp.dot(ds_b, ks[keys, lanes], preferred_element_type=F32)
                    dq_i = part if dq_i is None else dq_i + part
                    dks[keys, lanes] += lax.dot_general(ds_b, qi, tn_dims, preferred_element_type=F32) * s_mul
                    dvs[keys, lanes] += lax.dot_general(p.astype(BF16), do_i, tn_dims,
                                                        preferred_element_type=F32)
                    dct_ref[j, :, keys] -= jnp.sum(ds, axis=0, keepdims=True)
                dqs[pl.ds(i * tq, tq), lanes] = dq_i * scale
        dq_ref[...] = dqs[pl.ds(0, t), :].astype(BF16)
        dk_ref[...] = dks[pl.ds(0, t), :]
        dv_ref[...] = dvs[pl.ds(0, t), :]

    col = lambda off: pl.BlockSpec((t, LANES), lambda p: (0, off + p))
    big = lambda dt: jax.ShapeDtypeStruct((t, d_model), dt)
    return pl.pallas_call(
        body, name=name, grid=(nblk,),
        in_specs=[col(0), col(0), col(nblk), col(0), col(nblk), col(0),
                  pl.BlockSpec((hp, 1, tp), lambda p: (p, 0, 0))],
        out_specs=[col(0), col(0), col(0), col(0), pl.BlockSpec((hp, 1, tp), lambda p: (p, 0, 0))],
        out_shape=[big(BF16), big(BF16), big(F32), big(F32),
                   jax.ShapeDtypeStruct((n_heads, 1, tp), F32)],
        scratch_shapes=[pltpu.VMEM((tp, LANES), BF16)] * 4 + [pltpu.VMEM((tp, LANES), F32)] * 3,
        compiler_params=_cp("parallel"))(dmo, qg, qg, z, z, o, ct_pad)


def cast_into_slot(shard, w2d, name):
    r, c = w2d.shape
    rh = r // 2
    tr = _tile(rh, 512, 16)
    n = rh // tr

    def body(sh_ref, w_ref, o_ref):
        del sh_ref
        o_ref[...] = w_ref[...].astype(BF16)

    return pl.pallas_call(
        body, name=name,
        grid_spec=pltpu.PrefetchScalarGridSpec(
            num_scalar_prefetch=1, grid=(2, n),
            in_specs=[pl.BlockSpec((tr, c), lambda h, i, sh: (h * n + i, 0))],
            out_specs=pl.BlockSpec((None, None, tr, c), lambda h, i, sh: (sh[0], h, i, 0))),
        out_shape=jax.ShapeDtypeStruct((N_SHARDS, 2, rh, c), BF16),
        compiler_params=_cp("parallel", "parallel"))(shard, w2d)


def add_halves(core, g, other, name):
    s_n, _, rh, c = g.shape
    tr = _tile(rh, 512, 16)

    def body(core_ref, g_ref, o_ref, out_ref):
        del core_ref
        out_ref[...] = (g_ref[...].astype(F32) + o_ref[...].astype(F32)).astype(out_ref.dtype)

    return pl.pallas_call(
        body, name=name,
        grid_spec=pltpu.PrefetchScalarGridSpec(
            num_scalar_prefetch=1, grid=(s_n, rh // tr),
            in_specs=[pl.BlockSpec((None, None, tr, c), lambda s, i, cr: (s, cr[0], i, 0)),
                      pl.BlockSpec((None, tr, c), lambda s, i, cr: (s, i, 0))],
            out_specs=pl.BlockSpec((None, tr, c), lambda s, i, cr: (s, i, 0))),
        out_shape=jax.ShapeDtypeStruct((s_n, rh, c), g.dtype),
        compiler_params=_cp("parallel", "parallel"))(core, g, other)


def add_four(shard_core, csum, recv, buf, layer, name):
    _, rh, c = csum.shape
    tr = _tile(rh, 512, 16)

    def body(sc_ref, a_ref, r_ref, buf_ref, out_ref):
        del sc_ref, buf_ref
        acc = a_ref[...].astype(F32)
        for k in range(3):
            acc = acc + r_ref[k].astype(F32)
        out_ref[...] = acc

    return pl.pallas_call(
        body, name=name,
        grid_spec=pltpu.PrefetchScalarGridSpec(
            num_scalar_prefetch=1, grid=(rh // tr,),
            in_specs=[pl.BlockSpec((None, tr, c), lambda i, sc: (sc[0], i, 0)),
                      pl.BlockSpec((3, tr, c), lambda i, sc: (0, i, 0)),
                      pl.BlockSpec(memory_space=pl.ANY)],
            out_specs=pl.BlockSpec((None, None, tr, c), lambda i, sc: (layer, sc[1], i, 0))),
        out_shape=jax.ShapeDtypeStruct(buf.shape, F32),
        input_output_aliases={3: 0},
        compiler_params=_cp("parallel"))(shard_core, csum, recv, buf)


def adamw(w, g, m, v, name):
    r, c = w.shape
    tr = _tile(r, 512, SUBLANES)
    c1 = 1.0 - ADAM_B1 ** ADAM_STEP
    c2 = 1.0 - ADAM_B2 ** ADAM_STEP

    def body(w_ref, g_ref, m_ref, v_ref, d_ref, mo_ref, vo_ref):
        gv = g_ref[...]
        mn = ADAM_B1 * m_ref[...] + (1.0 - ADAM_B1) * gv
        vn = ADAM_B2 * v_ref[...] + (1.0 - ADAM_B2) * (gv * gv)
        m_hat = mn / c1
        v_hat = vn / c2
        d_ref[...] = -ADAM_LR * (m_hat / (jnp.sqrt(v_hat) + ADAM_EPS) + ADAM_WD * w_ref[...])
        mo_ref[...] = mn
        vo_ref[...] = vn

    blk = pl.BlockSpec((tr, c), lambda i: (i, 0))
    return pl.pallas_call(
        body, name=name, grid=(r // tr,), in_specs=[blk] * 4, out_specs=[blk] * 3,
        out_shape=[jax.ShapeDtypeStruct((r, c), F32)] * 3,
        compiler_params=_cp("parallel"))(w, g, m, v)


def _coords():
    return lax.axis_index("x"), lax.axis_index("y"), lax.axis_index("c")


def _exchange(name, ins, out_shapes, plan, in_place=False):
    n_in = len(ins)
    n_out = len(out_shapes)
    n_rem = len(plan([None] * n_in, [None] * n_out, True))

    def body(*refs):
        in_refs = refs[:n_in]
        out_refs = refs[n_in:n_in + n_out]
        send_sems, recv_sems = refs[n_in + n_out:]
        remote = plan(list(in_refs), list(out_refs), False)
        copies = [pltpu.make_async_remote_copy(
            src_ref=src, dst_ref=dst, send_sem=send_sems.at[q], recv_sem=recv_sems.at[q],
            device_id=peer, device_id_type=pl.DeviceIdType.MESH)
            for q, (src, dst, peer, _) in enumerate(remote)]
        waited = set()
        for q, (_, _, _, after) in enumerate(remote):
            if after is not None and after not in waited:
                copies[after].wait_recv()
                waited.add(after)
            copies[q].start()
        for q, cp in enumerate(copies):
            if q not in waited:
                cp.wait_recv()
        for cp in copies:
            cp.wait_send()

    hbm = pl.BlockSpec(memory_space=pl.ANY)
    return pl.pallas_call(
        body, name=name, in_specs=[hbm] * n_in, out_specs=[hbm] * n_out, out_shape=out_shapes,
        input_output_aliases={i: i for i in range(n_in)} if in_place else {},
        scratch_shapes=[pltpu.SemaphoreType.DMA((n_rem,)), pltpu.SemaphoreType.DMA((n_rem,))],
        compiler_params=pltpu.CompilerParams(has_side_effects=True))(*ins)


def _split_start(name, groups, plan):
    flat = [a for grp in groups for a in grp]
    n, n_grp = len(flat), len(groups)
    counts = [len(plan(g, [None] * len(grp), True)) for g, grp in enumerate(groups)]

    def body(*refs):
        ins, sems, token = refs[:n], refs[n:n + 2 * n_grp], refs[-1]
        pos = 0
        for g, grp in enumerate(groups):
            arrs = list(ins[pos:pos + len(grp)])
            pos += len(grp)
            for q, (src, dst, peer) in enumerate(plan(g, arrs, False)):
                pltpu.make_async_remote_copy(
                    src_ref=src, dst_ref=dst, send_sem=sems[2 * g].at[q], recv_sem=sems[2 * g + 1].at[q],
                    device_id=peer, device_id_type=pl.DeviceIdType.MESH).start()
        token[...] = jnp.zeros_like(token)

    hbm = pl.BlockSpec(memory_space=pltpu.HBM)
    sem = pl.BlockSpec(memory_space=pltpu.SEMAPHORE)
    outs = pl.pallas_call(
        body, name=name,
        out_shape=[pltpu.SemaphoreType.DMA((cnt,)) for cnt in counts for _ in range(2)]
        + [pltpu.HBM(a.shape, a.dtype) for a in flat] + [jax.ShapeDtypeStruct((SUBLANES, LANES), F32)],
        in_specs=[hbm] * n, out_specs=[sem] * (2 * n_grp) + [hbm] * n + [pl.BlockSpec(memory_space=pltpu.VMEM)],
        input_output_aliases={i: 2 * n_grp + i for i in range(n)},
        compiler_params=pltpu.CompilerParams(has_side_effects=pltpu.SideEffectType.DATAFLOW_SIDE_EFFECTING),
    )(*[pltpu.with_memory_space_constraint(a, pltpu.HBM) for a in flat])
    started, pos = [], 2 * n_grp
    for g, grp in enumerate(groups):
        started.append((outs[2 * g], outs[2 * g + 1], list(outs[pos:pos + len(grp)])))
        pos += len(grp)
    return started, outs[-1]


def _split_wait(name, started, after, plan_g):
    send_sems, recv_sems, arrs = started
    n = len(arrs)

    def body(*refs):
        ins, ssem, rsem = list(refs[:n]), refs[n], refs[n + 1]
        for q, (src, dst, peer) in enumerate(plan_g(ins, False)):
            cp = pltpu.make_async_remote_copy(
                src_ref=src, dst_ref=dst, send_sem=ssem.at[q], recv_sem=rsem.at[q],
                device_id=peer, device_id_type=pl.DeviceIdType.MESH)
            cp.wait_send()
            cp.wait_recv()

    hbm = pl.BlockSpec(memory_space=pltpu.HBM)
    sem = pl.BlockSpec(memory_space=pltpu.SEMAPHORE)
    return pl.pallas_call(
        body, name=name, out_shape=[pltpu.HBM(a.shape, a.dtype) for a in arrs],
        in_specs=[hbm] * n + [sem, sem, pl.BlockSpec(memory_space=pl.ANY)], out_specs=[hbm] * n,
        input_output_aliases={i: i for i in range(n)},
        compiler_params=pltpu.CompilerParams(has_side_effects=pltpu.SideEffectType.DATAFLOW_SIDE_EFFECTING),
    )(*arrs, send_sems, recv_sems, after)


def _gather_ici_plan(arrs, count_only):
    if count_only:
        return [None] * (3 * len(arrs))
    x, y, c = _coords()
    pushes = []
    for a in arrs:
        mine = a.at[2 * x + y, c]
        pushes += [(mine, mine, peer) for peer, _ in _other_chips(x, y, c)]
    return pushes


def _all_to_all_plan(arrs, count_only):
    half = len(arrs) // 2
    if count_only:
        return [None] * (3 * half)
    x, y, c = _coords()
    pushes = []
    for src, land in zip(arrs[:half], arrs[half:]):
        pushes += [(src.at[shard], land.at[k], peer) for k, (peer, shard) in enumerate(_other_chips(x, y, c))]
    return pushes


def forward_to_sibling(bufs, name):
    n = len(bufs)

    def plan(ins, outs, count_only):
        if count_only:
            return [None] * (3 * n)
        x, y, c = _coords()
        pushes = []
        for i in range(n):
            for _, src_shard in _other_chips(x, y, c):
                slab = outs[i].at[src_shard, c]
                pushes.append((slab, slab, (x, y, 1 - c), None))
        return pushes

    shapes = [jax.ShapeDtypeStruct(b.shape, b.dtype) for b in bufs]
    return _exchange(name, bufs, shapes, plan, in_place=True)


def _other_chips(x, y, c):
    return [((1 - x, y, c), 2 * (1 - x) + y), ((x, 1 - y, c), 2 * x + 1 - y),
            ((1 - x, 1 - y, c), 2 * (1 - x) + 1 - y)]


def swap_halves(grads, name):
    n = len(grads)
    shapes = [jax.ShapeDtypeStruct((g.shape[0],) + g.shape[2:], g.dtype) for g in grads]

    def plan(ins, outs, count_only):
        if count_only:
            return [None] * (n * N_SHARDS)
        x, y, c = _coords()
        remote = []
        for i in range(n):
            for s in range(N_SHARDS):
                remote.append((ins[i].at[s, 1 - c], outs[i].at[s], (x, y, 1 - c), None))
        return remote

    return _exchange(name, grads, shapes, plan)


def join_halves(bufs, name):
    slots = [(i, l) for i, b in enumerate(bufs) for l in range(b.shape[0])]

    def plan(ins, outs, count_only):
        if count_only:
            return [None] * len(slots)
        x, y, c = _coords()
        return [(outs[i].at[l, c], outs[i].at[l, c], (x, y, 1 - c), None) for i, l in slots]

    shapes = [jax.ShapeDtypeStruct(b.shape, b.dtype) for b in bufs]
    return _exchange(name, bufs, shapes, plan, in_place=True)


def gather_full(buf, name):
    def plan(ins, outs, count_only):
        if count_only:
            return [None] * 3
        x, y, c = _coords()
        mine = outs[0].at[2 * x + y]
        return [(mine, mine, peer, None) for peer, _ in _other_chips(x, y, c)]

    return _exchange(name, [buf], [jax.ShapeDtypeStruct(buf.shape, buf.dtype)], plan, in_place=True)[0]


def _pack(arrays, multiple):
    flat = jnp.concatenate([a.reshape(-1) for a in arrays])
    n = flat.shape[0]
    return jnp.pad(flat, (0, _round_up(n, multiple) - n))


def _unpack(flat, shapes):
    out, pos = [], 0
    for shp in shapes:
        n = math.prod(shp)
        out.append(flat[pos:pos + n].reshape(shp))
        pos += n
    return out


def _block_diag(w, per_group):
    nb, bs, _ = w.shape
    g = nb // per_group
    w4 = w.reshape(g, per_group, bs, bs)
    eye = jnp.eye(per_group, dtype=w.dtype)
    full = w4[:, :, :, None, :] * eye[None, :, None, :, None]
    return full.reshape(g, per_group * bs, per_group * bs).astype(BF16)


def _block_diag_extract(full, per_group, bs):
    g = full.shape[0]
    f5 = full.reshape(g, per_group, bs, per_group, bs)
    idx = jnp.arange(per_group)
    picked = f5[:, idx, :, idx, :]
    return jnp.moveaxis(picked, 0, 1).reshape(g * per_group, bs, bs)


def kernel(x, meta, a_w_in, a_conv_w, a_conv_b, a_w_r, a_b_r, a_w_i, a_b_i, a_lambda, a_w_out, kv_w, kv_f_b, b_w_in, b_w_out, f_w_in, f_conv_w, f_conv_b, f_w_out, ln1_g, ln1_b, ln2_g, ln2_b, loss_target, m_meta, m_a_w_in, m_a_conv_w, m_a_conv_b, m_a_w_r, m_a_b_r, m_a_w_i, m_a_b_i, m_a_lambda, m_a_w_out, m_kv_w, m_kv_f_b, m_b_w_in, m_b_w_out, m_f_w_in, m_f_conv_w, m_f_conv_b, m_f_w_out, m_ln1_g, m_ln1_b, m_ln2_g, m_ln2_b, v_meta, v_a_w_in, v_a_conv_w, v_a_conv_b, v_a_w_r, v_a_b_r, v_a_w_i, v_a_b_i, v_a_lambda, v_a_w_out, v_kv_w, v_kv_f_b, v_b_w_in, v_b_w_out, v_f_w_in, v_f_conv_w, v_f_conv_b, v_f_w_out, v_ln1_g, v_ln1_b, v_ln2_g, v_ln2_b):
    weights = dict(meta=meta, a_w_in=a_w_in, a_conv_w=a_conv_w, a_conv_b=a_conv_b, a_w_r=a_w_r, a_b_r=a_b_r,
                   a_w_i=a_w_i, a_b_i=a_b_i, a_lambda=a_lambda, a_w_out=a_w_out, kv_w=kv_w, kv_f_b=kv_f_b,
                   b_w_in=b_w_in, b_w_out=b_w_out, f_w_in=f_w_in, f_conv_w=f_conv_w, f_conv_b=f_conv_b,
                   f_w_out=f_w_out, ln1_g=ln1_g, ln1_b=ln1_b, ln2_g=ln2_g, ln2_b=ln2_b)
    mom_m = dict(meta=m_meta, a_w_in=m_a_w_in, a_conv_w=m_a_conv_w, a_conv_b=m_a_conv_b, a_w_r=m_a_w_r,
                 a_b_r=m_a_b_r, a_w_i=m_a_w_i, a_b_i=m_a_b_i, a_lambda=m_a_lambda, a_w_out=m_a_w_out,
                 kv_w=m_kv_w, kv_f_b=m_kv_f_b, b_w_in=m_b_w_in, b_w_out=m_b_w_out, f_w_in=m_f_w_in,
                 f_conv_w=m_f_conv_w, f_conv_b=m_f_conv_b, f_w_out=m_f_w_out, ln1_g=m_ln1_g, ln1_b=m_ln1_b,
                 ln2_g=m_ln2_g, ln2_b=m_ln2_b)
    mom_v = dict(meta=v_meta, a_w_in=v_a_w_in, a_conv_w=v_a_conv_w, a_conv_b=v_a_conv_b, a_w_r=v_a_w_r,
                 a_b_r=v_a_b_r, a_w_i=v_a_w_i, a_b_i=v_a_b_i, a_lambda=v_a_lambda, a_w_out=v_a_w_out,
                 kv_w=v_kv_w, kv_f_b=v_kv_f_b, b_w_in=v_b_w_in, b_w_out=v_b_w_out, f_w_in=v_f_w_in,
                 f_conv_w=v_f_conv_w, f_conv_b=v_f_conv_b, f_w_out=v_f_w_out, ln1_g=v_ln1_g, ln1_b=v_ln1_b,
                 ln2_g=v_ln2_g, ln2_b=v_ln2_b)
    return _train_step(x, loss_target, weights, mom_m, mom_v)


WEIGHT_ORDER = ("meta", "a_w_in", "a_conv_w", "a_conv_b", "a_w_r", "a_b_r", "a_w_i", "a_b_i", "a_lambda",
                "a_w_out", "kv_w", "kv_f_b", "b_w_in", "b_w_out", "f_w_in", "f_conv_w", "f_conv_b",
                "f_w_out", "ln1_g", "ln1_b", "ln2_g", "ln2_b")
BIG = ("a_w_in", "a_w_out", "kv_w", "b_w_in", "b_w_out", "f_w_in", "f_w_out")
OUT_TYPE = ("a_w_out", "b_w_out", "f_w_out")
SMALL_SHARDED = (("meta", 1), ("a_conv_w", 2), ("a_conv_b", 1), ("a_b_r", 1), ("a_b_i", 1), ("a_lambda", 1),
                 ("f_conv_w", 2))
SMALL_REPLICATED = ("a_w_r", "a_w_i", "kv_f_b", "f_conv_b", "ln1_g", "ln1_b", "ln2_g", "ln2_b")


def _train_step(x, loss_target, weights, mom_m, mom_v):
    S = N_SHARDS
    seq, d = x.shape[1], x.shape[2]
    nm = weights["meta"].shape[0]
    la = weights["a_w_in"].shape[0]
    lb = weights["b_w_in"].shape[0]
    depth = la + lb
    dr = weights["a_w_out"].shape[1] * S
    nb, bs = weights["a_w_r"].shape[1], weights["a_w_r"].shape[2]
    per_group = (LANES // math.gcd(bs, LANES))
    gs = per_group * bs
    heads = weights["kv_f_b"].shape[0]
    dff = weights["f_w_out"].shape[1] * S
    nkv = 2 * d + heads
    nkv_s = weights["kv_w"].shape[1]
    nkvp = _round_up(2 * d + LANES, 768) if 2 * d + LANES > 768 else 2 * d + LANES
    alpha = (2 * depth) ** 0.25
    xi, yi, ci = _coords()
    shard = 2 * xi + yi
    core_arr = jnp.reshape(ci, (1,)).astype(jnp.int32)
    shard_arr = jnp.reshape(shard, (1,)).astype(jnp.int32)
    shard_core_arr = jnp.stack([shard, ci]).astype(jnp.int32)

    def mixer_keys(l):
        if l < la:
            return [("a_w_in", l), ("a_w_out", l)]
        return ([("kv_w", 0)] if l == la else []) + [("b_w_in", l - la), ("b_w_out", l - la)]

    def ffn_keys(l):
        return [("f_w_in", l), ("f_w_out", l)]

    groups = [mixer_keys(0), ffn_keys(0)] + [mixer_keys(l) + ffn_keys(l) for l in range(1, depth)]
    keys = [kl for grp in groups for kl in grp]
    local2d = {(k, i): (weights[k][i] if weights[k].ndim == 3 else weights[k]) for k, i in keys}
    small_local = [weights[k] for k, _ in SMALL_SHARDED]
    sm_flat = _pack(small_local, 2 * SUBLANES * LANES).reshape(1, 2, -1, LANES)
    sm_slot = lax.dynamic_update_slice_in_dim(lax.empty((S,) + sm_flat.shape[1:], F32), sm_flat, shard, axis=0)
    parts = [[cast_into_slot(shard_arr, local2d[kl], f"cast_{kl[0]}{kl[1]}") for kl in grp] for grp in groups]
    parts[0].append(sm_slot)
    in_flight, start_token = _split_start("gather_start", parts, lambda g, refs, cnt: _gather_ici_plan(refs, cnt))
    gw = {}

    def fetch(g, after):
        arrs = _split_wait(f"gather_wait_{g}", in_flight[g], after, _gather_ici_plan)
        arrs = forward_to_sibling(arrs, f"gather_fwd_{g}")
        for kl, a in zip(groups[g], arrs):
            rows, cols = local2d[kl].shape
            gw[kl] = a.reshape(S * rows, cols) if kl[0] in OUT_TYPE else a.reshape(S, rows, cols)
        return arrs

    sm_all = fetch(0, start_token)[-1].reshape(S, -1)
    small_full = {}
    per_shard = [_unpack(sm_all[s], [a.shape for a in small_local]) for s in range(S)]
    for idx, (k, axis) in enumerate(SMALL_SHARDED):
        small_full[k] = jnp.concatenate([per_shard[s][idx] for s in range(S)], axis=axis)
    fb_pad = jnp.pad(weights["kv_f_b"], (0, LANES - heads)).reshape(1, LANES)
    wr_g = [_block_diag(weights["a_w_r"][l], per_group) for l in range(la)]
    wi_g = [_block_diag(weights["a_w_i"][l], per_group) for l in range(la)]
    row = lambda v: v.reshape(1, -1)

    h, hb = embed_fwd(small_full["meta"], x[0], "embed")
    saved = []
    kvz = ct_pad = None
    _, tp, _ = _attn_geometry(nm + seq)
    t = nm + seq
    for l in range(depth):
        sv = {"hb_in": hb}
        if l > 0:
            fetch(l + 1, hb)
        if l < la:
            gr = mm_in(hb, gw[("a_w_in", l)], F32, f"a{l}_in")
            rc, rcb = a_conv_fwd(gr, small_full["a_conv_w"][l], row(small_full["a_conv_b"][l]), f"a{l}_conv")
            r_pre, i_pre = mm_bd(rcb, wr_g[l], wi_g[l], f"a{l}_gates")
            hs, gb = a_elem_fwd(gr, rc, r_pre, i_pre, row(small_full["a_b_r"][l]), row(small_full["a_b_i"][l]),
                                row(small_full["a_lambda"][l]), f"a{l}_lru")
            mix = mm_out(gb, gw[("a_w_out", l)], f"a{l}_out")
            sv.update(gr=gr, rc=rc, rcb=rcb, r_pre=r_pre, i_pre=i_pre, hs=hs, gb=gb)
        else:
            j = l - la
            if j == 0:
                kv_cat = jnp.moveaxis(gw[("kv_w", 0)], 0, 1).reshape(d, S * nkv_s)
                kv_pad = jnp.pad(kv_cat, ((0, 0), (0, nkvp - nkv))).reshape(1, d, nkvp)
                kvz = mm_in(hb, kv_pad, F32, "kv_proj")
                cum = kv_fwd(kvz, fb_pad, d, "kv_forget")
                ct_pad = jnp.pad(cum[:, :heads].T, ((0, 0), (0, tp - t))).reshape(heads, 1, tp)
                kv_hb = hb
            qg = mm_in(hb, gw[("b_w_in", j)], F32, f"b{j}_in")
            o, mob = attn_fwd(qg, kvz, ct_pad, d, heads, f"b{j}_attn")
            mix = mm_out(mob, gw[("b_w_out", j)], f"b{j}_out")
            sv.update(qg=qg, o=o, mob=mob)
        h1, h1b, xh1, rs1 = ln_fwd(h, mix, row(weights["ln1_g"][l]), row(weights["ln1_b"][l]), alpha, f"ln1_{l}")
        if l == 0:
            fetch(1, h1b)
        zf = mm_in(h1b, gw[("f_w_in", l)], F32, f"f{l}_in")
        ffb = f_elem_fwd(zf, small_full["f_conv_w"][l], row(weights["f_conv_b"][l]), f"f{l}_act")
        ffo = mm_out(ffb, gw[("f_w_out", l)], f"f{l}_out")
        h2, h2b, xh2, rs2 = ln_fwd(h1, ffo, row(weights["ln2_g"][l]), row(weights["ln2_b"][l]), alpha, f"ln2_{l}")
        sv.update(h1b=h1b, xh1=xh1, rs1=rs1, zf=zf, ffb=ffb, xh2=xh2, rs2=rs2)
        saved.append(sv)
        h, hb = h2, h2b
    loss11, dy = loss_fwd_bwd(h, loss_target[0], nm, "loss")

    grads = {}

    def by_owner(kl, g3):
        rows, cols = local2d[kl].shape
        grads[kl] = g3.reshape(S, 2, rows // 2, cols)

    reducing = []

    def send_grads(g, names, arrays):
        from_sib = swap_halves(arrays, f"grad_swap_{g}")
        csums = [add_halves(core_arr, a, o, f"chip_sum_{g}_{i}") for i, (a, o) in enumerate(zip(arrays, from_sib))]
        lands = [lax.empty((3,) + cs.shape[1:], cs.dtype) for cs in csums]
        started, token = _split_start(f"grad_a2a_start_{g}", [csums + lands],
                                      lambda _, refs, cnt: _all_to_all_plan(refs, cnt))
        reducing.append((names, started[0]))
        return token[0:1, 0:1]

    def after_start(vec, zero):
        return vec if zero is None else vec + zero

    pin = None

    g_small = {}
    per_layer = {k: [None] * n for k, n in (
        ("a_conv_w", la), ("a_conv_b", la), ("a_w_r", la), ("a_b_r", la), ("a_w_i", la), ("a_b_i", la),
        ("a_lambda", la), ("f_conv_w", depth), ("f_conv_b", depth), ("ln1_g", depth), ("ln1_b", depth),
        ("ln2_g", depth), ("ln2_b", depth))}
    adds = [(dy, 1.0)]
    dks, dvs, dcs = [], [], []
    for l in reversed(range(depth)):
        sv = saved[l]
        ds2, ds2b, dg2, db2 = ln_bwd(adds, sv["xh2"], sv["rs2"], after_start(row(weights["ln2_g"][l]), pin),
                                     f"ln2_{l}_bwd")
        pin = None
        per_layer["ln2_g"][l], per_layer["ln2_b"][l] = dg2[0], db2[0]
        dff_v = mm_out_nt(ds2b, gw[("f_w_out", l)], f"f{l}_out_dx")
        by_owner(("f_w_out", l), mm_tn(sv["ffb"], ds2b, 1, f"f{l}_out_dw"))
        dzb, dfw, dfb_l = f_elem_bwd(sv["zf"], dff_v, small_full["f_conv_w"][l], row(weights["f_conv_b"][l]),
                                     f"f{l}_act_bwd")
        per_layer["f_conv_w"][l], per_layer["f_conv_b"][l] = dfw, dfb_l[0]
        dh1_f = mm_in_nt(dzb, gw[("f_w_in", l)], f"f{l}_in_dx")
        by_owner(("f_w_in", l), mm_tn(sv["h1b"], dzb, S, f"f{l}_in_dw"))
        if l == 0:
            pin = send_grads(1, groups[1], [grads[kl] for kl in groups[1]])
        ds1, ds1b, dg1, db1 = ln_bwd([(ds2, alpha), (dh1_f, 1.0)], sv["xh1"], sv["rs1"],
                                     after_start(row(weights["ln1_g"][l]), pin), f"ln1_{l}_bwd")
        pin = None
        per_layer["ln1_g"][l], per_layer["ln1_b"][l] = dg1[0], db1[0]
        if l < la:
            dgv = mm_out_nt(ds1b, gw[("a_w_out", l)], f"a{l}_out_dx")
            by_owner(("a_w_out", l), mm_tn(sv["gb"], ds1b, 1, f"a{l}_out_dw"))
            dgate_b, drp_b, dip_b, drc_d, dlam, dbr, dbi = a_elem_bwd(
                dgv, sv["gr"], sv["rc"], sv["r_pre"], sv["i_pre"], sv["hs"], row(small_full["a_b_r"][l]),
                row(small_full["a_b_i"][l]), row(small_full["a_lambda"][l]), f"a{l}_lru_bwd")
            drc_g = mm_bd_nt(drp_b, dip_b, wr_g[l], wi_g[l], f"a{l}_gates_dx")
            dwr_g, dwi_g = mm_bd_tn(sv["rcb"], drp_b, dip_b, gs, f"a{l}_gates_dw")
            dgr_b, dcw, dcb = a_conv_bwd(drc_d, drc_g, sv["gr"], small_full["a_conv_w"][l], dgate_b,
                                         f"a{l}_conv_bwd")
            per_layer["a_w_r"][l] = _block_diag_extract(dwr_g, per_group, bs)
            per_layer["a_w_i"][l] = _block_diag_extract(dwi_g, per_group, bs)
            per_layer["a_lambda"][l], per_layer["a_b_r"][l], per_layer["a_b_i"][l] = dlam[0], dbr[0], dbi[0]
            per_layer["a_conv_w"][l], per_layer["a_conv_b"][l] = dcw, dcb[0]
            dh_m = mm_in_nt(dgr_b, gw[("a_w_in", l)], f"a{l}_in_dx")
            by_owner(("a_w_in", l), mm_tn(sv["hb_in"], dgr_b, S, f"a{l}_in_dw"))
        else:
            j = l - la
            dmo = mm_out_nt(ds1b, gw[("b_w_out", j)], f"b{j}_out_dx")
            by_owner(("b_w_out", j), mm_tn(sv["mob"], ds1b, 1, f"b{j}_out_dw"))
            dq_b, dog_b, dk, dv, dct = attn_bwd(dmo, sv["qg"], kvz, sv["o"], ct_pad, d, heads, f"b{j}_attn_bwd")
            dks.append(dk)
            dvs.append(dv)
            dcs.append(jnp.pad(dct[:, 0, :t].T, ((0, 0), (0, LANES - heads))))
            dqg_b = jnp.concatenate([dq_b, dog_b], axis=1)
            dh_m = mm_in_nt(dqg_b, gw[("b_w_in", j)], f"b{j}_in_dx")
            by_owner(("b_w_in", j), mm_tn(sv["hb_in"], dqg_b, S, f"b{j}_in_dw"))
        adds = [(ds1, alpha), (dh_m, 1.0)]
        if l == la:
            dzf_b, dfb = kv_bwd(dcs, kvz, fb_pad, d, "kv_forget_bwd")
            dk_b = add_cast(dks[0], dks[1], "kv_dk") if lb == 2 else None
            dv_b = add_cast(dvs[0], dvs[1], "kv_dv") if lb == 2 else None
            dz_kv = jnp.concatenate([dk_b, dv_b, dzf_b, jnp.zeros((t, nkvp - 2 * d - LANES), BF16)], axis=1)
            dh_kv = mm_in_nt(dz_kv, kv_pad, "kv_proj_dx")
            kv_dw = mm_tn(kv_hb, dz_kv, 1, "kv_proj_dw")
            by_owner(("kv_w", 0), jnp.moveaxis(kv_dw[0, :, :nkv].reshape(d, S, nkv_s), 1, 0))
            g_small["kv_f_b"] = dfb[0, :heads]
            adds.append((dh_kv, 1.0))
        if l > 0:
            pin = send_grads(l + 1, groups[l + 1], [grads[kl] for kl in groups[l + 1]])
    g_meta, g_x = embed_bwd(adds, nm, "embed_bwd")

    small_names = list(SMALL_REPLICATED) + [k for k, _ in SMALL_SHARDED]
    g_small["meta"] = g_meta
    for k, vals in per_layer.items():
        g_small[k] = jnp.stack(vals)
    small_shapes = {k: (weights[k].shape if k in SMALL_REPLICATED else g_small[k].shape) for k in small_names}
    sm_g = _pack([g_small[k].reshape(small_shapes[k]) for k in small_names], S * 2 * SUBLANES * LANES)
    sm_g = sm_g.reshape(S, 2, -1, LANES)
    send_grads(0, groups[0] + [("small", 0)], [grads[kl] for kl in groups[0]] + [sm_g])

    fin = {"small": lax.empty((1,) + sm_g.shape[1:], F32)}
    for kl in keys:
        n_stack = weights[kl[0]].shape[0] if weights[kl[0]].ndim == 3 else 1
        rows, cols = local2d[kl].shape
        fin.setdefault(kl[0], lax.empty((n_stack, 2, rows // 2, cols), F32))
    for g, (names_g, started) in enumerate(reducing):
        arrs = _split_wait(f"grad_a2a_wait_{g}", started, g_x, _all_to_all_plan)
        half = len(names_g)
        for i, (kl, cs, rv) in enumerate(zip(names_g, arrs[:half], arrs[half:])):
            fin[kl[0]] = add_four(shard_core_arr, cs, rv, fin[kl[0]], kl[1], f"owner_sum_{g}_{i}")
    names = list(BIG) + ["small"]
    joined = dict(zip(names, join_halves([fin[k] for k in names], "grad_join")))
    sm_slot = lax.dynamic_update_slice_in_dim(lax.empty((S,) + joined["small"].shape[1:], F32), joined["small"],
                                              shard, axis=0)
    sm_red = gather_full(sm_slot, "small_gather").reshape(-1)

    out_g, out_d, out_m, out_v = {}, {}, {}, {}
    for k in BIG:
        w2 = weights[k].reshape(-1, weights[k].shape[-1])
        g2 = joined[k].reshape(w2.shape)
        dlt, mn, vn = adamw(w2, g2, mom_m[k].reshape(w2.shape), mom_v[k].reshape(w2.shape), "adamw_" + k)
        shp = weights[k].shape
        out_g[k], out_d[k], out_m[k], out_v[k] = g2.reshape(shp), dlt.reshape(shp), mn.reshape(shp), vn.reshape(shp)
    sm_vals = dict(zip(small_names, _unpack(sm_red, [small_shapes[k] for k in small_names])))
    local_small = {}
    for k in SMALL_REPLICATED:
        local_small[k] = sm_vals[k]
    for k, axis in SMALL_SHARDED:
        size = weights[k].shape[axis]
        local_small[k] = lax.dynamic_slice_in_dim(sm_vals[k], shard * size, size, axis=axis)
    for k in small_names:
        shp = weights[k].shape
        two_d = (-1, shp[-1]) if len(shp) > 1 else (1, -1)
        dlt, mn, vn = adamw(weights[k].reshape(two_d), local_small[k].reshape(two_d), mom_m[k].reshape(two_d),
                            mom_v[k].reshape(two_d), "adamw_" + k)
        out_g[k], out_d[k], out_m[k], out_v[k] = local_small[k], dlt.reshape(shp), mn.reshape(shp), vn.reshape(shp)

    loss = lax.psum(loss11[0, 0], ("x", "y", "c"))
    return (loss, g_x[None], *[out_g[k] for k in WEIGHT_ORDER], *[out_d[k] for k in WEIGHT_ORDER],
            *[out_m[k] for k in WEIGHT_ORDER], *[out_v[k] for k in WEIGHT_ORDER])
```

```python
import functools
import math

import jax
import jax.numpy as jnp
from jax import lax
from jax.experimental import pallas as pl
from jax.experimental.pallas import tpu as pltpu

F32 = jnp.float32
BF16 = jnp.bfloat16

LRU_C = 8.0
LN_EPS = 1e-5
ADAM_LR = 0.001
ADAM_B1 = 0.9
ADAM_B2 = 0.999
ADAM_EPS = 1e-08
ADAM_WD = 0.01
ADAM_STEP = 10

LANES = 128
SUBLANES = 8
V7X_VMEM_BYTES = 64 * 1024 * 1024
VMEM_LIMIT = V7X_VMEM_BYTES * 7 // 8
N_SHARDS = 4
GELU_C0 = math.sqrt(2.0 / math.pi)
GELU_C1 = 0.044715
NEG_BIG = -1e30


def _cp(*sem):
    return pltpu.CompilerParams(dimension_semantics=tuple(sem), vmem_limit_bytes=VMEM_LIMIT)


def _tile(n, cap, mult=LANES):
    best = None
    d = mult
    while d <= min(n, cap):
        if n % d == 0:
            best = d
        d += mult
    return n if best is None else best


def _row_block(t):
    if t % 3 == 0 and (t // 3) % 16 == 0:
        return t // 3
    return t


def _round_up(n, m):
    return (n + m - 1) // m * m


def _sigmoid(v):
    return 1.0 / (1.0 + jnp.exp(-v))


def _softplus(v):
    return jnp.maximum(v, 0.0) + jnp.log(1.0 + jnp.exp(-jnp.abs(v)))


def _gelu_parts(v):
    v2 = v * v
    u = GELU_C0 * (v + GELU_C1 * v * v2)
    t = jnp.tanh(u)
    g = 0.5 * v * (1.0 + t)
    dg = 0.5 * (1.0 + t) + 0.5 * v * (1.0 - t * t) * (GELU_C0 * (1.0 + 3.0 * GELU_C1 * v2))
    return g, dg


def _gelu(v):
    u = GELU_C0 * (v + GELU_C1 * v * v * v)
    return 0.5 * v * (1.0 + jnp.tanh(u))


def _neg_expm1(v):
    series = -v * (1.0 + 0.5 * v * (1.0 + (v / 3.0) * (1.0 + 0.25 * v)))
    return jnp.where(v > -0.05, series, 1.0 - jnp.exp(v))


def _shift_down(v, j):
    if j == 0:
        return v
    rows = lax.broadcasted_iota(jnp.int32, v.shape, 0)
    return jnp.where(rows >= j, pltpu.roll(v, j, 0), 0.0)


def _shift_up(v, j):
    if j == 0:
        return v
    n = v.shape[0]
    rows = lax.broadcasted_iota(jnp.int32, v.shape, 0)
    return jnp.where(rows < n - j, pltpu.roll(v, n - j, 0), 0.0)


def _scan_rows(a_ref, b_ref, out_ref, n_rows, width, reverse):
    n_groups = n_rows // SUBLANES
    rows = lax.broadcasted_iota(jnp.int32, (SUBLANES, width), 0)
    edge = 0 if reverse else SUBLANES - 1

    def body(g, carry):
        grp = (n_groups - 1 - g) if reverse else g
        off = pl.multiple_of(grp * SUBLANES, SUBLANES)
        b = b_ref[pl.ds(off, SUBLANES), :]
        a = None if a_ref is None else a_ref[pl.ds(off, SUBLANES), :]
        for d in (1, 2, 4):
            if reverse:
                keep = rows < SUBLANES - d
                sh = SUBLANES - d
            else:
                keep = rows >= d
                sh = d
            b_s = jnp.where(keep, pltpu.roll(b, sh, 0), 0.0)
            if a is None:
                b = b + b_s
            else:
                a_s = jnp.where(keep, pltpu.roll(a, sh, 0), 1.0)
                b = a * b_s + b
                a = a * a_s
        h = b + carry if a is None else b + a * carry
        out_ref[pl.ds(off, SUBLANES), :] = h
        return jnp.sum(jnp.where(rows == edge, h, 0.0), axis=0, keepdims=True)

    lax.fori_loop(0, n_groups, body, jnp.zeros((1, width), F32), unroll=2)


def mm_in(x, w, out_dtype, name):
    t, k = x.shape
    s_n, _, ns = w.shape
    tn = _tile(ns, 1408)
    nj = ns // tn
    rb = _row_block(t)

    def body(x_ref, w_ref, o_ref):
        o_ref[...] = jnp.dot(x_ref[...], w_ref[...], preferred_element_type=F32).astype(o_ref.dtype)

    return pl.pallas_call(
        body, name=name, grid=(s_n, nj, t // rb),
        in_specs=[pl.BlockSpec((rb, k), lambda s, j, r: (r, 0)),
                  pl.BlockSpec((None, k, tn), lambda s, j, r: (s, 0, j))],
        out_specs=pl.BlockSpec((rb, tn), lambda s, j, r: (r, s * nj + j)),
        out_shape=jax.ShapeDtypeStruct((t, s_n * ns), out_dtype),
        compiler_params=_cp("parallel", "parallel", "parallel"))(x, w)


def mm_out(x, w, name):
    t, k = x.shape
    n = w.shape[1]
    rb = _row_block(t)

    def body(x_ref, w_ref, o_ref):
        o_ref[...] = jnp.dot(x_ref[...], w_ref[...], preferred_element_type=F32)

    return pl.pallas_call(
        body, name=name, grid=(t // rb,),
        in_specs=[pl.BlockSpec((rb, k), lambda r: (r, 0)), pl.BlockSpec((k, n), lambda r: (0, 0))],
        out_specs=pl.BlockSpec((rb, n), lambda r: (r, 0)),
        out_shape=jax.ShapeDtypeStruct((t, n), F32),
        compiler_params=_cp("parallel"))(x, w)


def _part_map(p, per_part, nj, lead):
    def index(*grid):
        s, j = grid[-2], grid[-1]
        mine = s // per_part == p
        col = jnp.where(mine, (s - p * per_part) * nj + j, 0)
        return (grid[0], col) if lead else (0, col)
    return index


def mm_in_nt(dy_parts, w, name):
    n_parts = len(dy_parts)
    t = dy_parts[0].shape[0]
    s_n, k, ns = w.shape
    per_part = s_n // n_parts
    tn = _tile(ns, 1408)
    nj = ns // tn
    rb = _row_block(t)

    def body(*refs):
        w_ref, o_ref = refs[n_parts:]

        @pl.when((pl.program_id(1) == 0) & (pl.program_id(2) == 0))
        def _():
            o_ref[...] = jnp.zeros_like(o_ref)

        for p in range(n_parts):
            @pl.when(pl.program_id(1) // per_part == p)
            def _():
                o_ref[...] += lax.dot_general(refs[p][...], w_ref[...], (((1,), (1,)), ((), ())),
                                              preferred_element_type=F32)

    return pl.pallas_call(
        body, name=name, grid=(t // rb, s_n, nj),
        in_specs=[pl.BlockSpec((rb, tn), _part_map(p, per_part, nj, True)) for p in range(n_parts)]
        + [pl.BlockSpec((None, k, tn), lambda r, s, j: (s, 0, j))],
        out_specs=pl.BlockSpec((rb, k), lambda r, s, j: (r, 0)),
        out_shape=jax.ShapeDtypeStruct((t, k), F32),
        compiler_params=_cp("parallel", "arbitrary", "arbitrary"))(*dy_parts, w)


def mm_out_nt(dy, w, name):
    t, n = dy.shape
    k = w.shape[0]
    rb = _row_block(t)

    def body(dy_ref, w_ref, o_ref):
        o_ref[...] = lax.dot_general(dy_ref[...], w_ref[...], (((1,), (1,)), ((), ())),
                                     preferred_element_type=F32)

    return pl.pallas_call(
        body, name=name, grid=(t // rb,),
        in_specs=[pl.BlockSpec((rb, n), lambda r: (r, 0)), pl.BlockSpec((k, n), lambda r: (0, 0))],
        out_specs=pl.BlockSpec((rb, k), lambda r: (r, 0)),
        out_shape=jax.ShapeDtypeStruct((t, k), F32),
        compiler_params=_cp("parallel"))(dy, w)


def mm_tn(x, dy_parts, s_n, name):
    n_parts = len(dy_parts)
    t, kb = x.shape
    nb = dy_parts[0].shape[1] * n_parts // s_n
    per_part = max(s_n // n_parts, 1)
    tk = _tile(kb, 1408)
    tn = _tile(nb, 1408)
    nkb, nnb = kb // tk, nb // tn
    tn_dims = (((0,), (0,)), ((), ()))

    def body(*refs):
        x_ref, o_ref = refs[0], refs[-1]
        for p in range(n_parts):
            @pl.when(pl.program_id(1) // per_part == p)
            def _():
                o_ref[...] = lax.dot_general(x_ref[...], refs[1 + p][...], tn_dims,
                                             preferred_element_type=F32).astype(o_ref.dtype)

    return pl.pallas_call(
        body, name=name, grid=(nkb, s_n, nnb),
        in_specs=[pl.BlockSpec((t, tk), lambda a, s, b: (0, a))]
        + [pl.BlockSpec((t, tn), _part_map(p, per_part, nnb, False)) for p in range(n_parts)],
        out_specs=pl.BlockSpec((None, tk, tn), lambda a, s, b: (s, a, b)),
        out_shape=jax.ShapeDtypeStruct((s_n, kb, nb), BF16),
        compiler_params=_cp("parallel", "parallel", "parallel"))(x, *dy_parts)


def mm_bd(x, wr, wi, name):
    t, _ = x.shape
    g_n, gs, _ = wr.shape
    rb = _row_block(t)

    def body(x_ref, wr_ref, wi_ref, r_ref, i_ref):
        xv = x_ref[...]
        r_ref[...] = jnp.dot(xv, wr_ref[...], preferred_element_type=F32)
        i_ref[...] = jnp.dot(xv, wi_ref[...], preferred_element_type=F32)

    blk = pl.BlockSpec((rb, gs), lambda g, r: (r, g))
    wspec = pl.BlockSpec((None, gs, gs), lambda g, r: (g, 0, 0))
    return pl.pallas_call(
        body, name=name, grid=(g_n, t // rb), in_specs=[blk, wspec, wspec], out_specs=[blk, blk],
        out_shape=[jax.ShapeDtypeStruct((t, g_n * gs), F32)] * 2,
        compiler_params=_cp("parallel", "parallel"))(x, wr, wi)


def mm_bd_nt(dr, di, wr, wi, name):
    t, _ = dr.shape
    g_n, gs, _ = wr.shape
    rb = _row_block(t)
    nt = (((1,), (1,)), ((), ()))

    def body(dr_ref, di_ref, wr_ref, wi_ref, o_ref):
        o_ref[...] = (lax.dot_general(dr_ref[...], wr_ref[...], nt, preferred_element_type=F32)
                      + lax.dot_general(di_ref[...], wi_ref[...], nt, preferred_element_type=F32))

    blk = pl.BlockSpec((rb, gs), lambda g, r: (r, g))
    wspec = pl.BlockSpec((None, gs, gs), lambda g, r: (g, 0, 0))
    return pl.pallas_call(
        body, name=name, grid=(g_n, t // rb), in_specs=[blk, blk, wspec, wspec], out_specs=blk,
        out_shape=jax.ShapeDtypeStruct((t, g_n * gs), F32),
        compiler_params=_cp("parallel", "parallel"))(dr, di, wr, wi)


def mm_bd_tn(x, dr, di, gs, name):
    t, w = x.shape
    g_n = w // gs
    tn_dims = (((0,), (0,)), ((), ()))

    def body(x_ref, dr_ref, di_ref, gr_ref, gi_ref):
        xv = x_ref[...]
        gr_ref[...] = lax.dot_general(xv, dr_ref[...], tn_dims, preferred_element_type=F32)
        gi_ref[...] = lax.dot_general(xv, di_ref[...], tn_dims, preferred_element_type=F32)

    blk = pl.BlockSpec((t, gs), lambda g: (0, g))
    ospec = pl.BlockSpec((None, gs, gs), lambda g: (g, 0, 0))
    return pl.pallas_call(
        body, name=name, grid=(g_n,), in_specs=[blk, blk, blk], out_specs=[ospec, ospec],
        out_shape=[jax.ShapeDtypeStruct((g_n, gs, gs), F32)] * 2,
        compiler_params=_cp("parallel"))(x, dr, di)


def embed_fwd(meta, x2d, name):
    nm, d = meta.shape
    seq = x2d.shape[0]
    t = nm + seq
    cb = _tile(d, 256)

    def body(m_ref, x_ref, h_ref, hb_ref):
        h_ref[pl.ds(0, nm), :] = m_ref[...]
        h_ref[pl.ds(nm, seq), :] = x_ref[...]
        hb_ref[pl.ds(0, nm), :] = m_ref[...].astype(BF16)
        hb_ref[pl.ds(nm, seq), :] = x_ref[...].astype(BF16)

    return pl.pallas_call(
        body, name=name, grid=(d // cb,),
        in_specs=[pl.BlockSpec((nm, cb), lambda j: (0, j)), pl.BlockSpec((seq, cb), lambda j: (0, j))],
        out_specs=[pl.BlockSpec((t, cb), lambda j: (0, j))] * 2,
        out_shape=[jax.ShapeDtypeStruct((t, d), F32), jax.ShapeDtypeStruct((t, d), BF16)],
        compiler_params=_cp("parallel"))(meta, x2d)


def embed_bwd(adds, nm, name):
    t, d = adds[0][0].shape
    seq = t - nm
    cb = _tile(d, 256)
    scales = [s for _, s in adds]
    n = len(adds)

    def body(*refs):
        tot = None
        for r, sc in zip(refs[:n], scales):
            term = r[...] if sc == 1.0 else sc * r[...]
            tot = term if tot is None else tot + term
        gm_ref, gx_ref = refs[n], refs[n + 1]
        gm_ref[...] = tot[0:nm]
        gx_ref[...] = tot[nm:t]

    return pl.pallas_call(
        body, name=name, grid=(d // cb,),
        in_specs=[pl.BlockSpec((t, cb), lambda j: (0, j))] * n,
        out_specs=[pl.BlockSpec((nm, cb), lambda j: (0, j)), pl.BlockSpec((seq, cb), lambda j: (0, j))],
        out_shape=[jax.ShapeDtypeStruct((nm, d), F32), jax.ShapeDtypeStruct((seq, d), F32)],
        compiler_params=_cp("parallel"))(*[a for a, _ in adds])


def loss_fwd_bwd(h, tgt, nm, name):
    t, d = h.shape
    seq = t - nm
    cb = _tile(d, 256)
    inv_d = 1.0 / d

    def body(h_ref, t_ref, loss_ref, dy_ref):
        @pl.when(pl.program_id(0) == 0)
        def _():
            loss_ref[...] = jnp.zeros_like(loss_ref)
        err = h_ref[pl.ds(nm, seq), :] - t_ref[...]
        dy_ref[pl.ds(0, nm), :] = jnp.zeros((nm, cb), F32)
        dy_ref[pl.ds(nm, seq), :] = err * inv_d
        loss_ref[...] += (0.5 * inv_d) * jnp.sum(err * err, keepdims=True)

    return pl.pallas_call(
        body, name=name, grid=(d // cb,),
        in_specs=[pl.BlockSpec((t, cb), lambda j: (0, j)), pl.BlockSpec((seq, cb), lambda j: (0, j))],
        out_specs=[pl.BlockSpec((1, 1), lambda j: (0, 0)), pl.BlockSpec((t, cb), lambda j: (0, j))],
        out_shape=[jax.ShapeDtypeStruct((1, 1), F32), jax.ShapeDtypeStruct((t, d), F32)],
        compiler_params=_cp("arbitrary"))(h, tgt)


def ln_fwd(h, mix, g, b, alpha, name):
    t, d = h.shape
    rb = _row_block(t)

    def body(h_ref, m_ref, g_ref, b_ref, y_ref, yb_ref, xh_ref, rs_ref):
        s = alpha * h_ref[...] + m_ref[...]
        mu = jnp.mean(s, axis=-1, keepdims=True)
        c = s - mu
        var = jnp.mean(c * c, axis=-1, keepdims=True)
        rstd = lax.rsqrt(var + LN_EPS)
        xh = c * rstd
        y = xh * g_ref[...] + b_ref[...]
        y_ref[...] = y
        yb_ref[...] = y.astype(BF16)
        xh_ref[...] = xh
        rs_ref[...] = rstd

    row = pl.BlockSpec((rb, d), lambda r: (r, 0))
    vec = pl.BlockSpec((1, d), lambda r: (0, 0))
    return pl.pallas_call(
        body, name=name, grid=(t // rb,), in_specs=[row, row, vec, vec],
        out_specs=[row, row, row, pl.BlockSpec((rb, 1), lambda r: (r, 0))],
        out_shape=[jax.ShapeDtypeStruct((t, d), F32), jax.ShapeDtypeStruct((t, d), BF16),
                   jax.ShapeDtypeStruct((t, d), F32), jax.ShapeDtypeStruct((t, 1), F32)],
        compiler_params=_cp("parallel"))(h, mix, g, b)


def ln_bwd(adds, xhat, rstd, g, name):
    t, d = xhat.shape
    rb = _row_block(t)
    scales = [s for _, s in adds]
    n = len(adds)

    def body(*refs):
        xh_ref, rs_ref, g_ref = refs[n:n + 3]
        ds_ref, dsb_ref, dg_ref, db_ref = refs[n + 3:]
        dy = None
        for r, sc in zip(refs[:n], scales):
            term = r[...] if sc == 1.0 else sc * r[...]
            dy = term if dy is None else dy + term

        @pl.when(pl.program_id(0) == 0)
        def _():
            dg_ref[...] = jnp.zeros_like(dg_ref)
            db_ref[...] = jnp.zeros_like(db_ref)

        xh = xh_ref[...]
        dxh = dy * g_ref[...]
        m1 = jnp.mean(dxh, axis=-1, keepdims=True)
        m2 = jnp.mean(dxh * xh, axis=-1, keepdims=True)
        ds = rs_ref[...] * (dxh - m1 - xh * m2)
        ds_ref[...] = ds
        dsb_ref[...] = ds.astype(BF16)
        dg_ref[...] += jnp.sum(dy * xh, axis=0, keepdims=True)
        db_ref[...] += jnp.sum(dy, axis=0, keepdims=True)

    row = pl.BlockSpec((rb, d), lambda r: (r, 0))
    vec = pl.BlockSpec((1, d), lambda r: (0, 0))
    return pl.pallas_call(
        body, name=name, grid=(t // rb,),
        in_specs=[row] * n + [row, pl.BlockSpec((rb, 1), lambda r: (r, 0)), vec],
        out_specs=[row, row, vec, vec],
        out_shape=[jax.ShapeDtypeStruct((t, d), F32), jax.ShapeDtypeStruct((t, d), BF16),
                   jax.ShapeDtypeStruct((1, d), F32), jax.ShapeDtypeStruct((1, d), F32)],
        compiler_params=_cp("arbitrary"))(*[a for a, _ in adds], xhat, rstd, g)


def _conv_fwd_val(xv, w_ref, b_ref, width):
    y = b_ref[...]
    for j in range(width):
        y = y + _shift_down(xv, j) * w_ref[pl.ds(width - 1 - j, 1), :]
    return y


def _conv_bwd_val(dout, xv, w_ref, width):
    dx = None
    dws = [None] * width
    for j in range(width):
        k = width - 1 - j
        term = _shift_up(dout, j) * w_ref[pl.ds(k, 1), :]
        dx = term if dx is None else dx + term
        dws[k] = jnp.sum(dout * _shift_down(xv, j), axis=0, keepdims=True)
    return dx, dws, jnp.sum(dout, axis=0, keepdims=True)


def a_conv_fwd(gr, cw, cbias, name):
    t, two_dr = gr.shape
    dr = two_dr // 2
    width = cw.shape[0]
    cb = _tile(dr, 256)
    off = dr // cb

    def body(x_ref, w_ref, b_ref, rc_ref, rcb_ref):
        y = _conv_fwd_val(x_ref[...], w_ref, b_ref, width)
        rc_ref[...] = y
        rcb_ref[...] = y.astype(BF16)

    return pl.pallas_call(
        body, name=name, grid=(dr // cb,),
        in_specs=[pl.BlockSpec((t, cb), lambda j: (0, off + j)),
                  pl.BlockSpec((width, cb), lambda j: (0, j)), pl.BlockSpec((1, cb), lambda j: (0, j))],
        out_specs=[pl.BlockSpec((t, cb), lambda j: (0, j))] * 2,
        out_shape=[jax.ShapeDtypeStruct((t, dr), F32), jax.ShapeDtypeStruct((t, dr), BF16)],
        compiler_params=_cp("parallel"))(gr, cw, cbias)


def a_conv_bwd(drc_a, drc_b, gr, cw, dgr, name):
    t, two_dr = gr.shape
    dr = two_dr // 2
    width = cw.shape[0]
    cb = _tile(dr, 256)
    off = dr // cb

    def body(da_ref, db_ref, x_ref, w_ref, dgr_in, dx_ref, dw_ref, dbias_ref):
        del dgr_in
        dout = da_ref[...] + db_ref[...]
        dx, dws, dbias = _conv_bwd_val(dout, x_ref[...], w_ref, width)
        dx_ref[...] = dx.astype(BF16)
        for k in range(width):
            dw_ref[pl.ds(k, 1), :] = dws[k]
        dbias_ref[...] = dbias

    col = pl.BlockSpec((t, cb), lambda j: (0, j))
    return pl.pallas_call(
        body, name=name, grid=(dr // cb,),
        in_specs=[col, col, pl.BlockSpec((t, cb), lambda j: (0, off + j)),
                  pl.BlockSpec((width, cb), lambda j: (0, j)), pl.BlockSpec(memory_space=pl.ANY)],
        out_specs=[pl.BlockSpec((t, cb), lambda j: (0, off + j)), pl.BlockSpec((width, cb), lambda j: (0, j)),
                   pl.BlockSpec((1, cb), lambda j: (0, j))],
        out_shape=[jax.ShapeDtypeStruct((t, two_dr), BF16), jax.ShapeDtypeStruct((width, dr), F32),
                   jax.ShapeDtypeStruct((1, dr), F32)],
        input_output_aliases={4: 0},
        compiler_params=_cp("parallel"))(drc_a, drc_b, gr, cw, dgr)


def _lru_gates(r_pre, i_pre, br, bi, lam):
    r = _sigmoid(r_pre + br)
    i = _sigmoid(i_pre + bi)
    sp = _softplus(-lam)
    la = -LRU_C * r * sp
    a = jnp.exp(la)
    m = jnp.sqrt(_neg_expm1(2.0 * la))
    return r, i, sp, la, a, m


def a_elem_fwd(gr, rc, r_pre, i_pre, br, bi, lam, name):
    t, dr = rc.shape
    cb = _tile(dr, 2 * LANES)
    rb = _row_block(t)
    chunks = [pl.ds(r * rb, rb) for r in range(t // rb)]

    def body(gate_ref, rc_ref, rp_ref, ip_ref, br_ref, bi_ref, lam_ref, hs_ref, g_ref, a_s, u_s):
        for rows in chunks:
            _, i, _, _, a, m = _lru_gates(rp_ref[rows, :], ip_ref[rows, :], br_ref[...], bi_ref[...], lam_ref[...])
            a_s[rows, :] = a
            u_s[rows, :] = m * (i * rc_ref[rows, :])
        _scan_rows(a_s, u_s, hs_ref, t, cb, reverse=False)
        for rows in chunks:
            g_ref[rows, :] = (_gelu(gate_ref[rows, :]) * hs_ref[rows, :]).astype(BF16)

    col = pl.BlockSpec((t, cb), lambda j: (0, j))
    vec = pl.BlockSpec((1, cb), lambda j: (0, j))
    return pl.pallas_call(
        body, name=name, grid=(dr // cb,),
        in_specs=[col, col, col, col, vec, vec, vec],
        out_specs=[col, col],
        out_shape=[jax.ShapeDtypeStruct((t, dr), F32), jax.ShapeDtypeStruct((t, dr), BF16)],
        scratch_shapes=[pltpu.VMEM((t, cb), F32), pltpu.VMEM((t, cb), F32)],
        compiler_params=_cp("parallel"))(gr, rc, r_pre, i_pre, br, bi, lam)


def a_elem_bwd(dg, gr, rc, r_pre, i_pre, hs, br, bi, lam, name):
    t, dr = rc.shape
    cb = _tile(dr, 2 * LANES)
    rb = _row_block(t)
    chunks = [pl.ds(r * rb, rb) for r in range(t // rb)]

    def body(dg_ref, gate_ref, rc_ref, rp_ref, ip_ref, hs_ref, br_ref, bi_ref, lam_ref,
             dgate_ref, dr_ref, di_ref, drc_ref, dlam_ref, dbr_ref, dbi_ref, a_s, b_s, g_s, hp_s):
        lamv = lam_ref[...]
        gates = lambda rows: _lru_gates(rp_ref[rows, :], ip_ref[rows, :], br_ref[...], bi_ref[...], lamv)
        for rows in chunks:
            a_s[rows, :] = gates(rows)[4]
            ge, dge = _gelu_parts(gate_ref[rows, :])
            dgv = dg_ref[rows, :]
            dgate_ref[rows, :] = (dgv * hs_ref[rows, :] * dge).astype(BF16)
            b_s[rows, :] = dgv * ge
        a_s[...] = _shift_up(a_s[...], 1)
        hp_s[...] = _shift_down(hs_ref[...], 1)
        _scan_rows(a_s, b_s, g_s, t, cb, reverse=True)
        dsp = dbr = dbi = jnp.zeros((1, cb), F32)
        for rows in chunks:
            r, i, sp, _, a, m = gates(rows)
            rcv = rc_ref[rows, :]
            gsum = g_s[rows, :]
            da = gsum * hp_s[rows, :]
            dm = gsum * (i * rcv)
            d_i = gsum * m * rcv
            drc_ref[rows, :] = gsum * m * i
            dla = a * da - dm * (a * a) / m
            d_r = (-LRU_C) * sp * dla
            dsp = dsp + jnp.sum((-LRU_C) * r * dla, axis=0, keepdims=True)
            d_rp = d_r * r * (1.0 - r)
            d_ip = d_i * i * (1.0 - i)
            dr_ref[rows, :] = d_rp.astype(BF16)
            di_ref[rows, :] = d_ip.astype(BF16)
            dbr = dbr + jnp.sum(d_rp, axis=0, keepdims=True)
            dbi = dbi + jnp.sum(d_ip, axis=0, keepdims=True)
        dlam_ref[...] = -dsp * _sigmoid(-lamv)
        dbr_ref[...] = dbr
        dbi_ref[...] = dbi

    col = pl.BlockSpec((t, cb), lambda j: (0, j))
    vec = pl.BlockSpec((1, cb), lambda j: (0, j))
    big_b = jax.ShapeDtypeStruct((t, dr), BF16)
    vec_s = jax.ShapeDtypeStruct((1, dr), F32)
    return pl.pallas_call(
        body, name=name, grid=(dr // cb,),
        in_specs=[col, col, col, col, col, col, vec, vec, vec],
        out_specs=[col, col, col, col, vec, vec, vec],
        out_shape=[jax.ShapeDtypeStruct((t, 2 * dr), BF16), big_b, big_b, jax.ShapeDtypeStruct((t, dr), F32),
                   vec_s, vec_s, vec_s],
        scratch_shapes=[pltpu.VMEM((t, cb), F32)] * 4,
        compiler_params=_cp("parallel"))(dg, gr, rc, r_pre, i_pre, hs, br, bi, lam)


def f_elem_fwd(z, cw, cbias, name):
    t, two_f = z.shape
    dff = two_f // 2
    width = cw.shape[0]
    cb = _tile(dff, 256)
    off = dff // cb

    def body(zg_ref, zv_ref, wg_ref, wv_ref, bg_ref, bv_ref, o_ref):
        zcg = _conv_fwd_val(zg_ref[...], wg_ref, bg_ref, width)
        zcv = _conv_fwd_val(zv_ref[...], wv_ref, bv_ref, width)
        o_ref[...] = (_gelu(zcg) * zcv).astype(BF16)

    lo = lambda j: (0, j)
    hi = lambda j: (0, off + j)
    return pl.pallas_call(
        body, name=name, grid=(dff // cb,),
        in_specs=[pl.BlockSpec((t, cb), lo), pl.BlockSpec((t, cb), hi),
                  pl.BlockSpec((width, cb), lo), pl.BlockSpec((width, cb), hi),
                  pl.BlockSpec((1, cb), lo), pl.BlockSpec((1, cb), hi)],
        out_specs=pl.BlockSpec((t, cb), lo),
        out_shape=jax.ShapeDtypeStruct((t, dff), BF16),
        compiler_params=_cp("parallel"))(z, z, cw, cw, cbias, cbias)


def f_elem_bwd(z, dff_g, cw, cbias, name):
    t, two_f = z.shape
    dff = two_f // 2
    width = cw.shape[0]
    cb = _tile(dff, 256)
    off = dff // cb

    def body(zg_ref, zv_ref, d_ref, wg_ref, wv_ref, bg_ref, bv_ref,
             dzg_ref, dzv_ref, dwg_ref, dwv_ref, dbg_ref, dbv_ref):
        zg = zg_ref[...]
        zv = zv_ref[...]
        zcg = _conv_fwd_val(zg, wg_ref, bg_ref, width)
        zcv = _conv_fwd_val(zv, wv_ref, bv_ref, width)
        ge, dge = _gelu_parts(zcg)
        dv = d_ref[...]
        dx, dws, dbias = _conv_bwd_val(dv * zcv * dge, zg, wg_ref, width)
        dzg_ref[...] = dx.astype(BF16)
        for k in range(width):
            dwg_ref[pl.ds(k, 1), :] = dws[k]
        dbg_ref[...] = dbias
        dx, dws, dbias = _conv_bwd_val(dv * ge, zv, wv_ref, width)
        dzv_ref[...] = dx.astype(BF16)
        for k in range(width):
            dwv_ref[pl.ds(k, 1), :] = dws[k]
        dbv_ref[...] = dbias

    lo = lambda j: (0, j)
    hi = lambda j: (0, off + j)
    col = pl.BlockSpec((t, cb), lo)
    wsp = pl.BlockSpec((width, cb), lo)
    vsp = pl.BlockSpec((1, cb), lo)
    return pl.pallas_call(
        body, name=name, grid=(dff // cb,),
        in_specs=[col, pl.BlockSpec((t, cb), hi), col, wsp, pl.BlockSpec((width, cb), hi),
                  vsp, pl.BlockSpec((1, cb), hi)],
        out_specs=[col, col, wsp, wsp, vsp, vsp],
        out_shape=[jax.ShapeDtypeStruct((t, dff), BF16)] * 2
        + [jax.ShapeDtypeStruct((width, dff), F32)] * 2 + [jax.ShapeDtypeStruct((1, dff), F32)] * 2,
        compiler_params=_cp("parallel"))(z, z, dff_g, cw, cw, cbias, cbias)


def kv_fwd(z, fb, d_model, name):
    t, _ = z.shape
    blk = 2 * d_model // LANES

    def body(z_ref, fb_ref, c_ref, lf_s):
        v = z_ref[...] + fb_ref[...]
        lf_s[...] = -_softplus(-v)
        _scan_rows(None, lf_s, c_ref, t, LANES, reverse=False)

    return pl.pallas_call(
        body, name=name, grid=(1,),
        in_specs=[pl.BlockSpec((t, LANES), lambda j: (0, blk)), pl.BlockSpec((1, LANES), lambda j: (0, 0))],
        out_specs=pl.BlockSpec((t, LANES), lambda j: (0, 0)),
        out_shape=jax.ShapeDtypeStruct((t, LANES), F32),
        scratch_shapes=[pltpu.VMEM((t, LANES), F32)],
        compiler_params=_cp("arbitrary"))(z, fb)


def kv_bwd(dcs, z, fb, d_model, name):
    t, _ = z.shape
    blk = 2 * d_model // LANES
    n = len(dcs)

    def body(*refs):
        z_ref, fb_ref, dz_ref, dfb_ref, dc_s, dl_s = refs[n:]
        tot = refs[0][...]
        for r in refs[1:n]:
            tot = tot + r[...]
        dc_s[...] = tot
        _scan_rows(None, dc_s, dl_s, t, LANES, reverse=True)
        v = z_ref[...] + fb_ref[...]
        dz = dl_s[...] * _sigmoid(-v)
        dz_ref[...] = dz.astype(BF16)
        dfb_ref[...] = jnp.sum(dz, axis=0, keepdims=True)

    full = pl.BlockSpec((t, LANES), lambda j: (0, 0))
    return pl.pallas_call(
        body, name=name, grid=(1,),
        in_specs=[full] * n + [pl.BlockSpec((t, LANES), lambda j: (0, blk)),
                               pl.BlockSpec((1, LANES), lambda j: (0, 0))],
        out_specs=[full, pl.BlockSpec((1, LANES), lambda j: (0, 0))],
        out_shape=[jax.ShapeDtypeStruct((t, LANES), BF16), jax.ShapeDtypeStruct((1, LANES), F32)],
        scratch_shapes=[pltpu.VMEM((t, LANES), F32)] * 2,
        compiler_params=_cp("arbitrary"))(*dcs, z, fb)


def add_cast(a, b, name):
    t, d = a.shape
    cb = _tile(d, 512)

    def body(a_ref, b_ref, o_ref):
        o_ref[...] = (a_ref[...] + b_ref[...]).astype(BF16)

    col = pl.BlockSpec((t, cb), lambda j: (0, j))
    return pl.pallas_call(body, name=name, grid=(d // cb,), in_specs=[col, col], out_specs=col,
                          out_shape=jax.ShapeDtypeStruct((t, d), BF16),
                          compiler_params=_cp("parallel"))(a, b)


def _attn_geometry(t):
    nqb = 6 if t > 1024 else 2
    tp = _round_up(t, LANES * nqb)
    return nqb, tp, tp // nqb


def _attn_scales(dh):
    scale = dh ** -0.5
    if math.log2(scale).is_integer():
        return scale, 1.0
    return 1.0, scale


def _attn_pieces(qs, ks, crow, j, i, tq, dh, s_mul):
    r0 = i * tq
    lanes = pl.ds(j * dh, dh)
    qi = qs[pl.ds(r0, tq), lanes]
    spans = ([(0, r0)] if i > 0 else []) + [(r0, tq)]
    logits = []
    for k0, n in spans:
        s = lax.dot_general(qi, ks[pl.ds(k0, n), lanes], (((1,), (1,)), ((), ())),
                            preferred_element_type=F32)
        if s_mul != 1.0:
            s = s * s_mul
        s = s - crow[:, k0:k0 + n]
        if k0 == r0:
            rows = lax.broadcasted_iota(jnp.int32, (tq, tq), 0)
            cols = lax.broadcasted_iota(jnp.int32, (tq, tq), 1)
            s = jnp.where(cols <= rows, s, NEG_BIG)
        logits.append(s)
    mx = jnp.max(logits[0], axis=1, keepdims=True)
    for s in logits[1:]:
        mx = jnp.maximum(mx, jnp.max(s, axis=1, keepdims=True))
    es = [jnp.exp(s - mx) for s in logits]
    tot = jnp.sum(es[0], axis=1, keepdims=True)
    for e in es[1:]:
        tot = tot + jnp.sum(e, axis=1, keepdims=True)
    inv = 1.0 / tot
    return [(k0, n, e * inv) for (k0, n), e in zip(spans, es)], qi


def attn_fwd(qg, z, ct_pad, d_model, n_heads, name):
    t = qg.shape[0]
    dh = d_model // n_heads
    hp = LANES // dh
    nqb, tp, tq = _attn_geometry(t)
    nblk = d_model // LANES
    q_mul, s_mul = _attn_scales(dh)

    def body(q_ref, og_ref, k_ref, v_ref, ct_ref, o_ref, mo_ref, qs, ks, vs, os_):
        pad = jnp.zeros((tp - t, LANES), BF16)
        qs[pl.ds(0, t), :] = (q_ref[...] * q_mul).astype(BF16)
        qs[pl.ds(t, tp - t), :] = pad
        for src, dst in ((k_ref, ks), (v_ref, vs)):
            dst[pl.ds(0, t), :] = src[...].astype(BF16)
            dst[pl.ds(t, tp - t), :] = pad
        for j in range(hp):
            crow = ct_ref[j]
            lanes = pl.ds(j * dh, dh)
            for i in range(nqb):
                pieces, _ = _attn_pieces(qs, ks, crow, j, i, tq, dh, s_mul)
                acc = None
                for k0, n, p in pieces:
                    part = jnp.dot(p.astype(BF16), vs[pl.ds(k0, n), lanes], preferred_element_type=F32)
                    acc = part if acc is None else acc + part
                os_[pl.ds(i * tq, tq), lanes] = acc
        o = os_[pl.ds(0, t), :]
        o_ref[...] = o
        mo_ref[...] = (o * _sigmoid(og_ref[...])).astype(BF16)

    col = lambda off: pl.BlockSpec((t, LANES), lambda p: (0, off + p))
    return pl.pallas_call(
        body, name=name, grid=(nblk,),
        in_specs=[col(0), col(nblk), col(0), col(nblk), pl.BlockSpec((hp, 1, tp), lambda p: (p, 0, 0))],
        out_specs=[col(0), col(0)],
        out_shape=[jax.ShapeDtypeStruct((t, d_model), F32), jax.ShapeDtypeStruct((t, d_model), BF16)],
        scratch_shapes=[pltpu.VMEM((tp, LANES), BF16)] * 3 + [pltpu.VMEM((tp, LANES), F32)],
        compiler_params=_cp("parallel"))(qg, qg, z, z, ct_pad)


def attn_bwd(dmo, qg, z, o, ct_pad, d_model, n_heads, name):
    t = qg.shape[0]
    dh = d_model // n_heads
    hp = LANES // dh
    nqb, tp, tq = _attn_geometry(t)
    nblk = d_model // LANES
    q_mul, s_mul = _attn_scales(dh)
    scale = dh ** -0.5
    tn_dims = (((0,), (0,)), ((), ()))
    nt_dims = (((1,), (1,)), ((), ()))

    def body(dmo_ref, q_ref, og_ref, k_ref, v_ref, o_ref, ct_ref,
             dq_ref, dog_ref, dk_ref, dv_ref, dct_ref, qs, ks, vs, dos, dqs, dks, dvs):
        pad = jnp.zeros((tp - t, LANES), BF16)
        sg = _sigmoid(og_ref[...])
        dmo_v = dmo_ref[...]
        dog_ref[...] = (dmo_v * o_ref[...] * sg * (1.0 - sg)).astype(BF16)
        dos[pl.ds(0, t), :] = (dmo_v * sg).astype(BF16)
        dos[pl.ds(t, tp - t), :] = pad
        qs[pl.ds(0, t), :] = (q_ref[...] * q_mul).astype(BF16)
        qs[pl.ds(t, tp - t), :] = pad
        for src, dst in ((k_ref, ks), (v_ref, vs)):
            dst[pl.ds(0, t), :] = src[...].astype(BF16)
            dst[pl.ds(t, tp - t), :] = pad
        dks[...] = jnp.zeros_like(dks)
        dvs[...] = jnp.zeros_like(dvs)
        dct_ref[...] = jnp.zeros_like(dct_ref)
        for j in range(hp):
            crow = ct_ref[j]
            lanes = pl.ds(j * dh, dh)
            for i in range(nqb):
                pieces, qi = _attn_pieces(qs, ks, crow, j, i, tq, dh, s_mul)
                do_i = dos[pl.ds(i * tq, tq), lanes]
                dps = [lax.dot_general(do_i, vs[pl.ds(k0, n), lanes], nt_dims, preferred_element_type=F32)
                       for k0, n, _ in pieces]
                row = None
                for (_, _, p), dp in zip(pieces, dps):
                    part = jnp.sum(p * dp, axis=1, keepdims=True)
                    row = part if row is None else row + part
                dq_i = None
                for (k0, n, p), dp in zip(pieces, dps):
                    ds = p * (dp - row)
                    ds_b = ds.astype(BF16)
                    keys = pl.ds(k0, n)
                    part = jnp.dot(ds_b, ks[keys, lanes], preferred_element_type=F32)
                    dq_i = part if dq_i is None else dq_i + part
                    dks[keys, lanes] += lax.dot_general(ds_b, qi, tn_dims, preferred_element_type=F32) * s_mul
                    dvs[keys, lanes] += lax.dot_general(p.astype(BF16), do_i, tn_dims,
                                                        preferred_element_type=F32)
                    dct_ref[j, :, keys] -= jnp.sum(ds, axis=0, keepdims=True)
                dqs[pl.ds(i * tq, tq), lanes] = dq_i * scale
        dq_ref[...] = dqs[pl.ds(0, t), :].astype(BF16)
        dk_ref[...] = dks[pl.ds(0, t), :]
        dv_ref[...] = dvs[pl.ds(0, t), :]

    col = lambda off: pl.BlockSpec((t, LANES), lambda p: (0, off + p))
    big = lambda dt: jax.ShapeDtypeStruct((t, d_model), dt)
    return pl.pallas_call(
        body, name=name, grid=(nblk,),
        in_specs=[col(0), col(0), col(nblk), col(0), col(nblk), col(0),
                  pl.BlockSpec((hp, 1, tp), lambda p: (p, 0, 0))],
        out_specs=[col(0), col(0), col(0), col(0), pl.BlockSpec((hp, 1, tp), lambda p: (p, 0, 0))],
        out_shape=[big(BF16), big(BF16), big(F32), big(F32),
                   jax.ShapeDtypeStruct((n_heads, 1, tp), F32)],
        scratch_shapes=[pltpu.VMEM((tp, LANES), BF16)] * 4 + [pltpu.VMEM((tp, LANES), F32)] * 3,
        compiler_params=_cp("parallel"))(dmo, qg, qg, z, z, o, ct_pad)


def cast_into_slot(shard, w2d, name):
    r, c = w2d.shape
    rh = r // 2
    tr = _tile(rh, 512, 16)
    n = rh // tr

    def body(sh_ref, w_ref, o_ref):
        del sh_ref
        o_ref[...] = w_ref[...].astype(BF16)

    return pl.pallas_call(
        body, name=name,
        grid_spec=pltpu.PrefetchScalarGridSpec(
            num_scalar_prefetch=1, grid=(2, n),
            in_specs=[pl.BlockSpec((tr, c), lambda h, i, sh: (h * n + i, 0))],
            out_specs=pl.BlockSpec((None, None, tr, c), lambda h, i, sh: (sh[0], h, i, 0))),
        out_shape=jax.ShapeDtypeStruct((N_SHARDS, 2, rh, c), BF16),
        compiler_params=_cp("parallel", "parallel"))(shard, w2d)


def add_halves(core, g, other, name):
    s_n, _, rh, c = g.shape
    tr = _tile(rh, 512, 16)

    def body(core_ref, g_ref, o_ref, out_ref):
        del core_ref
        out_ref[...] = (g_ref[...].astype(F32) + o_ref[...].astype(F32)).astype(out_ref.dtype)

    return pl.pallas_call(
        body, name=name,
        grid_spec=pltpu.PrefetchScalarGridSpec(
            num_scalar_prefetch=1, grid=(s_n, rh // tr),
            in_specs=[pl.BlockSpec((None, None, tr, c), lambda s, i, cr: (s, cr[0], i, 0)),
                      pl.BlockSpec((None, tr, c), lambda s, i, cr: (s, i, 0))],
            out_specs=pl.BlockSpec((None, tr, c), lambda s, i, cr: (s, i, 0))),
        out_shape=jax.ShapeDtypeStruct((s_n, rh, c), g.dtype),
        compiler_params=_cp("parallel", "parallel"))(core, g, other)


def add_four(shard_core, csum, recv, buf, layer, name):
    _, rh, c = csum.shape
    tr = _tile(rh, 512, 16)

    def body(sc_ref, a_ref, r_ref, buf_ref, out_ref):
        del sc_ref, buf_ref
        acc = a_ref[...].astype(F32)
        for k in range(3):
            acc = acc + r_ref[k].astype(F32)
        out_ref[...] = acc

    return pl.pallas_call(
        body, name=name,
        grid_spec=pltpu.PrefetchScalarGridSpec(
            num_scalar_prefetch=1, grid=(rh // tr,),
            in_specs=[pl.BlockSpec((None, tr, c), lambda i, sc: (sc[0], i, 0)),
                      pl.BlockSpec((3, tr, c), lambda i, sc: (0, i, 0)),
                      pl.BlockSpec(memory_space=pl.ANY)],
            out_specs=pl.BlockSpec((None, None, tr, c), lambda i, sc: (layer, sc[1], i, 0))),
        out_shape=jax.ShapeDtypeStruct(buf.shape, F32),
        input_output_aliases={3: 0},
        compiler_params=_cp("parallel"))(shard_core, csum, recv, buf)


def adamw(w, g, m, v, name):
    r, c = w.shape
    tr = _tile(r, 512, SUBLANES)
    c1 = 1.0 - ADAM_B1 ** ADAM_STEP
    c2 = 1.0 - ADAM_B2 ** ADAM_STEP

    def body(w_ref, g_ref, m_ref, v_ref, d_ref, mo_ref, vo_ref):
        gv = g_ref[...]
        mn = ADAM_B1 * m_ref[...] + (1.0 - ADAM_B1) * gv
        vn = ADAM_B2 * v_ref[...] + (1.0 - ADAM_B2) * (gv * gv)
        m_hat = mn / c1
        v_hat = vn / c2
        d_ref[...] = -ADAM_LR * (m_hat / (jnp.sqrt(v_hat) + ADAM_EPS) + ADAM_WD * w_ref[...])
        mo_ref[...] = mn
        vo_ref[...] = vn

    blk = pl.BlockSpec((tr, c), lambda i: (i, 0))
    return pl.pallas_call(
        body, name=name, grid=(r // tr,), in_specs=[blk] * 4, out_specs=[blk] * 3,
        out_shape=[jax.ShapeDtypeStruct((r, c), F32)] * 3,
        compiler_params=_cp("parallel"))(w, g, m, v)


def _coords():
    return lax.axis_index("x"), lax.axis_index("y"), lax.axis_index("c")


def _exchange(name, ins, out_shapes, plan, in_place=False):
    n_in = len(ins)
    n_out = len(out_shapes)
    n_rem = len(plan([None] * n_in, [None] * n_out, True))

    def body(*refs):
        in_refs = refs[:n_in]
        out_refs = refs[n_in:n_in + n_out]
        send_sems, recv_sems = refs[n_in + n_out:]
        remote = plan(list(in_refs), list(out_refs), False)
        copies = [pltpu.make_async_remote_copy(
            src_ref=src, dst_ref=dst, send_sem=send_sems.at[q], recv_sem=recv_sems.at[q],
            device_id=peer, device_id_type=pl.DeviceIdType.MESH)
            for q, (src, dst, peer, _) in enumerate(remote)]
        waited = set()
        for q, (_, _, _, after) in enumerate(remote):
            if after is not None and after not in waited:
                copies[after].wait_recv()
                waited.add(after)
            copies[q].start()
        for q, cp in enumerate(copies):
            if q not in waited:
                cp.wait_recv()
        for cp in copies:
            cp.wait_send()

    hbm = pl.BlockSpec(memory_space=pl.ANY)
    return pl.pallas_call(
        body, name=name, in_specs=[hbm] * n_in, out_specs=[hbm] * n_out, out_shape=out_shapes,
        input_output_aliases={i: i for i in range(n_in)} if in_place else {},
        scratch_shapes=[pltpu.SemaphoreType.DMA((n_rem,)), pltpu.SemaphoreType.DMA((n_rem,))],
        compiler_params=pltpu.CompilerParams(has_side_effects=True))(*ins)


def _split_start(name, groups, plan):
    flat = [a for grp in groups for a in grp]
    n, n_grp = len(flat), len(groups)
    counts = [len(plan(g, [None] * len(grp), True)) for g, grp in enumerate(groups)]

    def body(*refs):
        ins, sems, token = refs[:n], refs[n:n + 2 * n_grp], refs[-1]
        pos = 0
        for g, grp in enumerate(groups):
            arrs = list(ins[pos:pos + len(grp)])
            pos += len(grp)
            for q, (src, dst, peer) in enumerate(plan(g, arrs, False)):
                pltpu.make_async_remote_copy(
                    src_ref=src, dst_ref=dst, send_sem=sems[2 * g].at[q], recv_sem=sems[2 * g + 1].at[q],
                    device_id=peer, device_id_type=pl.DeviceIdType.MESH).start()
        token[...] = jnp.zeros_like(token)

    hbm = pl.BlockSpec(memory_space=pltpu.HBM)
    sem = pl.BlockSpec(memory_space=pltpu.SEMAPHORE)
    outs = pl.pallas_call(
        body, name=name,
        out_shape=[pltpu.SemaphoreType.DMA((cnt,)) for cnt in counts for _ in range(2)]
        + [pltpu.HBM(a.shape, a.dtype) for a in flat] + [jax.ShapeDtypeStruct((SUBLANES, LANES), F32)],
        in_specs=[hbm] * n, out_specs=[sem] * (2 * n_grp) + [hbm] * n + [pl.BlockSpec(memory_space=pltpu.VMEM)],
        input_output_aliases={i: 2 * n_grp + i for i in range(n)},
        compiler_params=pltpu.CompilerParams(has_side_effects=pltpu.SideEffectType.DATAFLOW_SIDE_EFFECTING),
    )(*[pltpu.with_memory_space_constraint(a, pltpu.HBM) for a in flat])
    started, pos = [], 2 * n_grp
    for g, grp in enumerate(groups):
        started.append((outs[2 * g], outs[2 * g + 1], list(outs[pos:pos + len(grp)])))
        pos += len(grp)
    return started, outs[-1]


def _split_wait(name, started, after, plan_g):
    send_sems, recv_sems, arrs = started
    n = len(arrs)

    def body(*refs):
        ins, ssem, rsem = list(refs[:n]), refs[n], refs[n + 1]
        for q, (src, dst, peer) in enumerate(plan_g(ins, False)):
            cp = pltpu.make_async_remote_copy(
                src_ref=src, dst_ref=dst, send_sem=ssem.at[q], recv_sem=rsem.at[q],
                device_id=peer, device_id_type=pl.DeviceIdType.MESH)
            cp.wait_send()
            cp.wait_recv()

    hbm = pl.BlockSpec(memory_space=pltpu.HBM)
    sem = pl.BlockSpec(memory_space=pltpu.SEMAPHORE)
    return pl.pallas_call(
        body, name=name, out_shape=[pltpu.HBM(a.shape, a.dtype) for a in arrs],
        in_specs=[hbm] * n + [sem, sem, pl.BlockSpec(memory_space=pl.ANY)], out_specs=[hbm] * n,
        input_output_aliases={i: i for i in range(n)},
        compiler_params=pltpu.CompilerParams(has_side_effects=pltpu.SideEffectType.DATAFLOW_SIDE_EFFECTING),
    )(*arrs, send_sems, recv_sems, after)


def _gather_ici_plan(arrs, count_only):
    if count_only:
        return [None] * (3 * len(arrs))
    x, y, c = _coords()
    pushes = []
    for a in arrs:
        mine = a.at[2 * x + y, c]
        pushes += [(mine, mine, peer) for peer, _ in _other_chips(x, y, c)]
    return pushes


def _all_to_all_plan(arrs, count_only):
    half = len(arrs) // 2
    if count_only:
        return [None] * (3 * half)
    x, y, c = _coords()
    pushes = []
    for src, land in zip(arrs[:half], arrs[half:]):
        pushes += [(src.at[shard], land.at[k], peer) for k, (peer, shard) in enumerate(_other_chips(x, y, c))]
    return pushes


def forward_to_sibling(bufs, name):
    n = len(bufs)

    def plan(ins, outs, count_only):
        if count_only:
            return [None] * (3 * n)
        x, y, c = _coords()
        pushes = []
        for i in range(n):
            for _, src_shard in _other_chips(x, y, c):
                slab = outs[i].at[src_shard, c]
                pushes.append((slab, slab, (x, y, 1 - c), None))
        return pushes

    shapes = [jax.ShapeDtypeStruct(b.shape, b.dtype) for b in bufs]
    return _exchange(name, bufs, shapes, plan, in_place=True)


def _other_chips(x, y, c):
    return [((1 - x, y, c), 2 * (1 - x) + y), ((x, 1 - y, c), 2 * x + 1 - y),
            ((1 - x, 1 - y, c), 2 * (1 - x) + 1 - y)]


def swap_halves(grads, name):
    n = len(grads)
    shapes = [jax.ShapeDtypeStruct((g.shape[0],) + g.shape[2:], g.dtype) for g in grads]

    def plan(ins, outs, count_only):
        if count_only:
            return [None] * (n * N_SHARDS)
        x, y, c = _coords()
        remote = []
        for i in range(n):
            for s in range(N_SHARDS):
                remote.append((ins[i].at[s, 1 - c], outs[i].at[s], (x, y, 1 - c), None))
        return remote

    return _exchange(name, grads, shapes, plan)


def join_halves(bufs, name):
    slots = [(i, l) for i, b in enumerate(bufs) for l in range(b.shape[0])]

    def plan(ins, outs, count_only):
        if count_only:
            return [None] * len(slots)
        x, y, c = _coords()
        return [(outs[i].at[l, c], outs[i].at[l, c], (x, y, 1 - c), None) for i, l in slots]

    shapes = [jax.ShapeDtypeStruct(b.shape, b.dtype) for b in bufs]
    return _exchange(name, bufs, shapes, plan, in_place=True)


def gather_full(buf, name):
    def plan(ins, outs, count_only):
        if count_only:
            return [None] * 3
        x, y, c = _coords()
        mine = outs[0].at[2 * x + y]
        return [(mine, mine, peer, None) for peer, _ in _other_chips(x, y, c)]

    return _exchange(name, [buf], [jax.ShapeDtypeStruct(buf.shape, buf.dtype)], plan, in_place=True)[0]


def _pack(arrays, multiple):
    flat = jnp.concatenate([a.reshape(-1) for a in arrays])
    n = flat.shape[0]
    return jnp.pad(flat, (0, _round_up(n, multiple) - n))


def _unpack(flat, shapes):
    out, pos = [], 0
    for shp in shapes:
        n = math.prod(shp)
        out.append(flat[pos:pos + n].reshape(shp))
        pos += n
    return out


def _block_diag(w, per_group):
    nb, bs, _ = w.shape
    g = nb // per_group
    w4 = w.reshape(g, per_group, bs, bs)
    eye = jnp.eye(per_group, dtype=w.dtype)
    full = w4[:, :, :, None, :] * eye[None, :, None, :, None]
    return full.reshape(g, per_group * bs, per_group * bs).astype(BF16)


def _block_diag_extract(full, per_group, bs):
    g = full.shape[0]
    f5 = full.reshape(g, per_group, bs, per_group, bs)
    idx = jnp.arange(per_group)
    picked = f5[:, idx, :, idx, :]
    return jnp.moveaxis(picked, 0, 1).reshape(g * per_group, bs, bs)


def kernel(x, meta, a_w_in, a_conv_w, a_conv_b, a_w_r, a_b_r, a_w_i, a_b_i, a_lambda, a_w_out, kv_w, kv_f_b, b_w_in, b_w_out, f_w_in, f_conv_w, f_conv_b, f_w_out, ln1_g, ln1_b, ln2_g, ln2_b, loss_target, m_meta, m_a_w_in, m_a_conv_w, m_a_conv_b, m_a_w_r, m_a_b_r, m_a_w_i, m_a_b_i, m_a_lambda, m_a_w_out, m_kv_w, m_kv_f_b, m_b_w_in, m_b_w_out, m_f_w_in, m_f_conv_w, m_f_conv_b, m_f_w_out, m_ln1_g, m_ln1_b, m_ln2_g, m_ln2_b, v_meta, v_a_w_in, v_a_conv_w, v_a_conv_b, v_a_w_r, v_a_b_r, v_a_w_i, v_a_b_i, v_a_lambda, v_a_w_out, v_kv_w, v_kv_f_b, v_b_w_in, v_b_w_out, v_f_w_in, v_f_conv_w, v_f_conv_b, v_f_w_out, v_ln1_g, v_ln1_b, v_ln2_g, v_ln2_b):
    weights = dict(meta=meta, a_w_in=a_w_in, a_conv_w=a_conv_w, a_conv_b=a_conv_b, a_w_r=a_w_r, a_b_r=a_b_r,
                   a_w_i=a_w_i, a_b_i=a_b_i, a_lambda=a_lambda, a_w_out=a_w_out, kv_w=kv_w, kv_f_b=kv_f_b,
                   b_w_in=b_w_in, b_w_out=b_w_out, f_w_in=f_w_in, f_conv_w=f_conv_w, f_conv_b=f_conv_b,
                   f_w_out=f_w_out, ln1_g=ln1_g, ln1_b=ln1_b, ln2_g=ln2_g, ln2_b=ln2_b)
    mom_m = dict(meta=m_meta, a_w_in=m_a_w_in, a_conv_w=m_a_conv_w, a_conv_b=m_a_conv_b, a_w_r=m_a_w_r,
                 a_b_r=m_a_b_r, a_w_i=m_a_w_i, a_b_i=m_a_b_i, a_lambda=m_a_lambda, a_w_out=m_a_w_out,
                 kv_w=m_kv_w, kv_f_b=m_kv_f_b, b_w_in=m_b_w_in, b_w_out=m_b_w_out, f_w_in=m_f_w_in,
                 f_conv_w=m_f_conv_w, f_conv_b=m_f_conv_b, f_w_out=m_f_w_out, ln1_g=m_ln1_g, ln1_b=m_ln1_b,
                 ln2_g=m_ln2_g, ln2_b=m_ln2_b)
    mom_v = dict(meta=v_meta, a_w_in=v_a_w_in, a_conv_w=v_a_conv_w, a_conv_b=v_a_conv_b, a_w_r=v_a_w_r,
                 a_b_r=v_a_b_r, a_w_i=v_a_w_i, a_b_i=v_a_b_i, a_lambda=v_a_lambda, a_w_out=v_a_w_out,
                 kv_w=v_kv_w, kv_f_b=v_kv_f_b, b_w_in=v_b_w_in, b_w_out=v_b_w_out, f_w_in=v_f_w_in,
                 f_conv_w=v_f_conv_w, f_conv_b=v_f_conv_b, f_w_out=v_f_w_out, ln1_g=v_ln1_g, ln1_b=v_ln1_b,
                 ln2_g=v_ln2_g, ln2_b=v_ln2_b)
    return _train_step(x, loss_target, weights, mom_m, mom_v)


WEIGHT_ORDER = ("meta", "a_w_in", "a_conv_w", "a_conv_b", "a_w_r", "a_b_r", "a_w_i", "a_b_i", "a_lambda",
                "a_w_out", "kv_w", "kv_f_b", "b_w_in", "b_w_out", "f_w_in", "f_conv_w", "f_conv_b",
                "f_w_out", "ln1_g", "ln1_b", "ln2_g", "ln2_b")
BIG = ("a_w_in", "a_w_out", "kv_w", "b_w_in", "b_w_out", "f_w_in", "f_w_out")
OUT_TYPE = ("a_w_out", "b_w_out", "f_w_out")
SMALL_SHARDED = (("meta", 1), ("a_conv_w", 2), ("a_conv_b", 1), ("a_b_r", 1), ("a_b_i", 1), ("a_lambda", 1),
                 ("f_conv_w", 2))
SMALL_REPLICATED = ("a_w_r", "a_w_i", "kv_f_b", "f_conv_b", "ln1_g", "ln1_b", "ln2_g", "ln2_b")


def _train_step(x, loss_target, weights, mom_m, mom_v):
    S = N_SHARDS
    seq, d = x.shape[1], x.shape[2]
    nm = weights["meta"].shape[0]
    la = weights["a_w_in"].shape[0]
    lb = weights["b_w_in"].shape[0]
    depth = la + lb
    dr = weights["a_w_out"].shape[1] * S
    nb, bs = weights["a_w_r"].shape[1], weights["a_w_r"].shape[2]
    per_group = (LANES // math.gcd(bs, LANES))
    gs = per_group * bs
    heads = weights["kv_f_b"].shape[0]
    dff = weights["f_w_out"].shape[1] * S
    nkv = 2 * d + heads
    nkv_s = weights["kv_w"].shape[1]
    nkvp = _round_up(2 * d + LANES, 768) if 2 * d + LANES > 768 else 2 * d + LANES
    alpha = (2 * depth) ** 0.25
    xi, yi, ci = _coords()
    shard = 2 * xi + yi
    core_arr = jnp.reshape(ci, (1,)).astype(jnp.int32)
    shard_arr = jnp.reshape(shard, (1,)).astype(jnp.int32)
    shard_core_arr = jnp.stack([shard, ci]).astype(jnp.int32)

    def mixer_keys(l):
        if l < la:
            return [("a_w_in", l), ("a_w_out", l)]
        return ([("kv_w", 0)] if l == la else []) + [("b_w_in", l - la), ("b_w_out", l - la)]

    def ffn_keys(l):
        return [("f_w_in", l), ("f_w_out", l)]

    groups = [mixer_keys(0), ffn_keys(0)] + [mixer_keys(l) + ffn_keys(l) for l in range(1, depth)]
    keys = [kl for grp in groups for kl in grp]
    local2d = {(k, i): (weights[k][i] if weights[k].ndim == 3 else weights[k]) for k, i in keys}
    small_local = [weights[k] for k, _ in SMALL_SHARDED]
    sm_flat = _pack(small_local, 2 * SUBLANES * LANES).reshape(1, 2, -1, LANES)
    sm_slot = lax.dynamic_update_slice_in_dim(lax.empty((S,) + sm_flat.shape[1:], F32), sm_flat, shard, axis=0)
    parts = [[cast_into_slot(shard_arr, local2d[kl], f"cast_{kl[0]}{kl[1]}") for kl in grp] for grp in groups]
    parts[0].append(sm_slot)
    in_flight, start_token = _split_start("gather_start", parts, lambda g, refs, cnt: _gather_ici_plan(refs, cnt))
    gw = {}

    def fetch(g, after):
        arrs = _split_wait(f"gather_wait_{g}", in_flight[g], after, _gather_ici_plan)
        arrs = forward_to_sibling(arrs, f"gather_fwd_{g}")
        for kl, a in zip(groups[g], arrs):
            rows, cols = local2d[kl].shape
            gw[kl] = a.reshape(S * rows, cols) if kl[0] in OUT_TYPE else a.reshape(S, rows, cols)
        return arrs

    sm_all = fetch(0, start_token)[-1].reshape(S, -1)
    small_full = {}
    per_shard = [_unpack(sm_all[s], [a.shape for a in small_local]) for s in range(S)]
    for idx, (k, axis) in enumerate(SMALL_SHARDED):
        small_full[k] = jnp.concatenate([per_shard[s][idx] for s in range(S)], axis=axis)
    fb_pad = jnp.pad(weights["kv_f_b"], (0, LANES - heads)).reshape(1, LANES)
    wr_g = [_block_diag(weights["a_w_r"][l], per_group) for l in range(la)]
    wi_g = [_block_diag(weights["a_w_i"][l], per_group) for l in range(la)]
    row = lambda v: v.reshape(1, -1)

    h, hb = embed_fwd(small_full["meta"], x[0], "embed")
    saved = []
    kvz = ct_pad = None
    _, tp, _ = _attn_geometry(nm + seq)
    t = nm + seq
    for l in range(depth):
        sv = {"hb_in": hb}
        if l > 0:
            fetch(l + 1, hb)
        if l < la:
            gr = mm_in(hb, gw[("a_w_in", l)], F32, f"a{l}_in")
            rc, rcb = a_conv_fwd(gr, small_full["a_conv_w"][l], row(small_full["a_conv_b"][l]), f"a{l}_conv")
            r_pre, i_pre = mm_bd(rcb, wr_g[l], wi_g[l], f"a{l}_gates")
            hs, gb = a_elem_fwd(gr, rc, r_pre, i_pre, row(small_full["a_b_r"][l]), row(small_full["a_b_i"][l]),
                                row(small_full["a_lambda"][l]), f"a{l}_lru")
            mix = mm_out(gb, gw[("a_w_out", l)], f"a{l}_out")
            sv.update(gr=gr, rc=rc, rcb=rcb, r_pre=r_pre, i_pre=i_pre, hs=hs, gb=gb)
        else:
            j = l - la
            if j == 0:
                kv_cat = jnp.moveaxis(gw[("kv_w", 0)], 0, 1).reshape(d, S * nkv_s)
                kv_pad = jnp.pad(kv_cat, ((0, 0), (0, nkvp - nkv))).reshape(1, d, nkvp)
                kvz = mm_in(hb, kv_pad, F32, "kv_proj")
                cum = kv_fwd(kvz, fb_pad, d, "kv_forget")
                ct_pad = jnp.pad(cum[:, :heads].T, ((0, 0), (0, tp - t))).reshape(heads, 1, tp)
                kv_hb = hb
            qg = mm_in(hb, gw[("b_w_in", j)], F32, f"b{j}_in")
            o, mob = attn_fwd(qg, kvz, ct_pad, d, heads, f"b{j}_attn")
            mix = mm_out(mob, gw[("b_w_out", j)], f"b{j}_out")
            sv.update(qg=qg, o=o, mob=mob)
        h1, h1b, xh1, rs1 = ln_fwd(h, mix, row(weights["ln1_g"][l]), row(weights["ln1_b"][l]), alpha, f"ln1_{l}")
        if l == 0:
            fetch(1, h1b)
        zf = mm_in(h1b, gw[("f_w_in", l)], F32, f"f{l}_in")
        ffb = f_elem_fwd(zf, small_full["f_conv_w"][l], row(weights["f_conv_b"][l]), f"f{l}_act")
        ffo = mm_out(ffb, gw[("f_w_out", l)], f"f{l}_out")
        h2, h2b, xh2, rs2 = ln_fwd(h1, ffo, row(weights["ln2_g"][l]), row(weights["ln2_b"][l]), alpha, f"ln2_{l}")
        sv.update(h1b=h1b, xh1=xh1, rs1=rs1, zf=zf, ffb=ffb, xh2=xh2, rs2=rs2)
        saved.append(sv)
        h, hb = h2, h2b
    loss11, dy = loss_fwd_bwd(h, loss_target[0], nm, "loss")

    grads = {}

    def by_owner(kl, g3):
        rows, cols = local2d[kl].shape
        grads[kl] = g3.reshape(S, 2, rows // 2, cols)

    reducing = []

    def send_grads(g, names, arrays):
        from_sib = swap_halves(arrays, f"grad_swap_{g}")
        csums = [add_halves(core_arr, a, o, f"chip_sum_{g}_{i}") for i, (a, o) in enumerate(zip(arrays, from_sib))]
        lands = [lax.empty((3,) + cs.shape[1:], cs.dtype) for cs in csums]
        started, token = _split_start(f"grad_a2a_start_{g}", [csums + lands],
                                      lambda _, refs, cnt: _all_to_all_plan(refs, cnt))
        reducing.append((names, started[0]))
        return token[0:1, 0:1]

    def after_start(vec, zero):
        return vec if zero is None else vec + zero

    pin = None

    g_small = {}
    per_layer = {k: [None] * n for k, n in (
        ("a_conv_w", la), ("a_conv_b", la), ("a_w_r", la), ("a_b_r", la), ("a_w_i", la), ("a_b_i", la),
        ("a_lambda", la), ("f_conv_w", depth), ("f_conv_b", depth), ("ln1_g", depth), ("ln1_b", depth),
        ("ln2_g", depth), ("ln2_b", depth))}
    adds = [(dy, 1.0)]
    dks, dvs, dcs = [], [], []
    for l in reversed(range(depth)):
        sv = saved[l]
        ds2, ds2b, dg2, db2 = ln_bwd(adds, sv["xh2"], sv["rs2"], after_start(row(weights["ln2_g"][l]), pin),
                                     f"ln2_{l}_bwd")
        pin = None
        per_layer["ln2_g"][l], per_layer["ln2_b"][l] = dg2[0], db2[0]
        dff_v = mm_out_nt(ds2b, gw[("f_w_out", l)], f"f{l}_out_dx")
        by_owner(("f_w_out", l), mm_tn(sv["ffb"], [ds2b], 1, f"f{l}_out_dw"))
        dzg, dzv, dwg, dwv, dbg, dbv = f_elem_bwd(sv["zf"], dff_v, small_full["f_conv_w"][l],
                                                  row(weights["f_conv_b"][l]), f"f{l}_act_bwd")
        per_layer["f_conv_w"][l] = jnp.concatenate([dwg, dwv], axis=1)
        per_layer["f_conv_b"][l] = jnp.concatenate([dbg, dbv], axis=1)[0]
        dh1_f = mm_in_nt([dzg, dzv], gw[("f_w_in", l)], f"f{l}_in_dx")
        by_owner(("f_w_in", l), mm_tn(sv["h1b"], [dzg, dzv], S, f"f{l}_in_dw"))
        if l == 0:
            pin = send_grads(1, groups[1], [grads[kl] for kl in groups[1]])
        ds1, ds1b, dg1, db1 = ln_bwd([(ds2, alpha), (dh1_f, 1.0)], sv["xh1"], sv["rs1"],
                                     after_start(row(weights["ln1_g"][l]), pin), f"ln1_{l}_bwd")
        pin = None
        per_layer["ln1_g"][l], per_layer["ln1_b"][l] = dg1[0], db1[0]
        if l < la:
            dgv = mm_out_nt(ds1b, gw[("a_w_out", l)], f"a{l}_out_dx")
            by_owner(("a_w_out", l), mm_tn(sv["gb"], [ds1b], 1, f"a{l}_out_dw"))
            dgate_b, drp_b, dip_b, drc_d, dlam, dbr, dbi = a_elem_bwd(
                dgv, sv["gr"], sv["rc"], sv["r_pre"], sv["i_pre"], sv["hs"], row(small_full["a_b_r"][l]),
                row(small_full["a_b_i"][l]), row(small_full["a_lambda"][l]), f"a{l}_lru_bwd")
            drc_g = mm_bd_nt(drp_b, dip_b, wr_g[l], wi_g[l], f"a{l}_gates_dx")
            dwr_g, dwi_g = mm_bd_tn(sv["rcb"], drp_b, dip_b, gs, f"a{l}_gates_dw")
            dgr_b, dcw, dcb = a_conv_bwd(drc_d, drc_g, sv["gr"], small_full["a_conv_w"][l], dgate_b,
                                         f"a{l}_conv_bwd")
            per_layer["a_w_r"][l] = _block_diag_extract(dwr_g, per_group, bs)
            per_layer["a_w_i"][l] = _block_diag_extract(dwi_g, per_group, bs)
            per_layer["a_lambda"][l], per_layer["a_b_r"][l], per_layer["a_b_i"][l] = dlam[0], dbr[0], dbi[0]
            per_layer["a_conv_w"][l], per_layer["a_conv_b"][l] = dcw, dcb[0]
            dh_m = mm_in_nt([dgr_b], gw[("a_w_in", l)], f"a{l}_in_dx")
            by_owner(("a_w_in", l), mm_tn(sv["hb_in"], [dgr_b], S, f"a{l}_in_dw"))
        else:
            j = l - la
            dmo = mm_out_nt(ds1b, gw[("b_w_out", j)], f"b{j}_out_dx")
            by_owner(("b_w_out", j), mm_tn(sv["mob"], [ds1b], 1, f"b{j}_out_dw"))
            dq_b, dog_b, dk, dv, dct = attn_bwd(dmo, sv["qg"], kvz, sv["o"], ct_pad, d, heads, f"b{j}_attn_bwd")
            dks.append(dk)
            dvs.append(dv)
            dcs.append(jnp.pad(dct[:, 0, :t].T, ((0, 0), (0, LANES - heads))))
            dh_m = mm_in_nt([dq_b, dog_b], gw[("b_w_in", j)], f"b{j}_in_dx")
            by_owner(("b_w_in", j), mm_tn(sv["hb_in"], [dq_b, dog_b], S, f"b{j}_in_dw"))
        adds = [(ds1, alpha), (dh_m, 1.0)]
        if l == la:
            dzf_b, dfb = kv_bwd(dcs, kvz, fb_pad, d, "kv_forget_bwd")
            dk_b = add_cast(dks[0], dks[1], "kv_dk") if lb == 2 else None
            dv_b = add_cast(dvs[0], dvs[1], "kv_dv") if lb == 2 else None
            dz_kv = jnp.concatenate([dk_b, dv_b, dzf_b, jnp.zeros((t, nkvp - 2 * d - LANES), BF16)], axis=1)
            dh_kv = mm_in_nt([dz_kv], kv_pad, "kv_proj_dx")
            kv_dw = mm_tn(kv_hb, [dz_kv], 1, "kv_proj_dw")
            by_owner(("kv_w", 0), jnp.moveaxis(kv_dw[0, :, :nkv].reshape(d, S, nkv_s), 1, 0))
            g_small["kv_f_b"] = dfb[0, :heads]
            adds.append((dh_kv, 1.0))
        if l > 0:
            pin = send_grads(l + 1, groups[l + 1], [grads[kl] for kl in groups[l + 1]])
    g_meta, g_x = embed_bwd(adds, nm, "embed_bwd")

    small_names = list(SMALL_REPLICATED) + [k for k, _ in SMALL_SHARDED]
    g_small["meta"] = g_meta
    for k, vals in per_layer.items():
        g_small[k] = jnp.stack(vals)
    small_shapes = {k: (weights[k].shape if k in SMALL_REPLICATED else g_small[k].shape) for k in small_names}
    sm_g = _pack([g_small[k].reshape(small_shapes[k]) for k in small_names], S * 2 * SUBLANES * LANES)
    sm_g = sm_g.reshape(S, 2, -1, LANES)
    send_grads(0, groups[0] + [("small", 0)], [grads[kl] for kl in groups[0]] + [sm_g])

    fin = {"small": lax.empty((1,) + sm_g.shape[1:], F32)}
    for kl in keys:
        n_stack = weights[kl[0]].shape[0] if weights[kl[0]].ndim == 3 else 1
        rows, cols = local2d[kl].shape
        fin.setdefault(kl[0], lax.empty((n_stack, 2, rows // 2, cols), F32))
    for g, (names_g, started) in enumerate(reducing):
        arrs = _split_wait(f"grad_a2a_wait_{g}", started, g_x, _all_to_all_plan)
        half = len(names_g)
        for i, (kl, cs, rv) in enumerate(zip(names_g, arrs[:half], arrs[half:])):
            fin[kl[0]] = add_four(shard_core_arr, cs, rv, fin[kl[0]], kl[1], f"owner_sum_{g}_{i}")
    names = list(BIG) + ["small"]
    joined = dict(zip(names, join_halves([fin[k] for k in names], "grad_join")))
    sm_slot = lax.dynamic_update_slice_in_dim(lax.empty((S,) + joined["small"].shape[1:], F32), joined["small"],
                                              shard, axis=0)
    sm_red = gather_full(sm_slot, "small_gather").reshape(-1)

    out_g, out_d, out_m, out_v = {}, {}, {}, {}
    for k in BIG:
        w2 = weights[k].reshape(-1, weights[k].shape[-1])
        g2 = joined[k].reshape(w2.shape)
        dlt, mn, vn = adamw(w2, g2, mom_m[k].reshape(w2.shape), mom_v[k].reshape(w2.shape), "adamw_" + k)
        shp = weights[k].shape
        out_g[k], out_d[k], out_m[k], out_v[k] = g2.reshape(shp), dlt.reshape(shp), mn.reshape(shp), vn.reshape(shp)
    sm_vals = dict(zip(small_names, _unpack(sm_red, [small_shapes[k] for k in small_names])))
    local_small = {}
    for k in SMALL_REPLICATED:
        local_small[k] = sm_vals[k]
    for k, axis in SMALL_SHARDED:
        size = weights[k].shape[axis]
        local_small[k] = lax.dynamic_slice_in_dim(sm_vals[k], shard * size, size, axis=axis)
    for k in small_names:
        shp = weights[k].shape
        two_d = (-1, shp[-1]) if len(shp) > 1 else (1, -1)
        dlt, mn, vn = adamw(weights[k].reshape(two_d), local_small[k].reshape(two_d), mom_m[k].reshape(two_d),
                            mom_v[k].reshape(two_d), "adamw_" + k)
        out_g[k], out_d[k], out_m[k], out_v[k] = local_small[k], dlt.reshape(shp), mn.reshape(shp), vn.reshape(shp)

    loss = lax.psum(loss11[0, 0], ("x", "y", "c"))
    return (loss, g_x[None], *[out_g[k] for k in WEIGHT_ORDER], *[out_d[k] for k in WEIGHT_ORDER],
            *[out_m[k] for k in WEIGHT_ORDER], *[out_v[k] for k in WEIGHT_ORDER])
```

```python
import functools
import math

import jax
import jax.numpy as jnp
from jax import lax
from jax.experimental import pallas as pl
from jax.experimental.pallas import tpu as pltpu

F32 = jnp.float32
BF16 = jnp.bfloat16

LRU_C = 8.0
LN_EPS = 1e-5
ADAM_LR = 0.001
ADAM_B1 = 0.9
ADAM_B2 = 0.999
ADAM_EPS = 1e-08
ADAM_WD = 0.01
ADAM_STEP = 10

LANES = 128
SUBLANES = 8
V7X_VMEM_BYTES = 64 * 1024 * 1024
VMEM_LIMIT = V7X_VMEM_BYTES * 7 // 8
N_SHARDS = 4
GELU_C0 = math.sqrt(2.0 / math.pi)
GELU_C1 = 0.044715
NEG_BIG = -1e30


def _cp(*sem):
    return pltpu.CompilerParams(dimension_semantics=tuple(sem), vmem_limit_bytes=VMEM_LIMIT)


def _tile(n, cap, mult=LANES):
    best = None
    d = mult
    while d <= min(n, cap):
        if n % d == 0:
            best = d
        d += mult
    return n if best is None else best


def _row_block(t):
    if t % 3 == 0 and (t // 3) % 16 == 0:
        return t // 3
    return t


def _round_up(n, m):
    return (n + m - 1) // m * m


def _sigmoid(v):
    return 1.0 / (1.0 + jnp.exp(-v))


def _softplus(v):
    return jnp.maximum(v, 0.0) + jnp.log(1.0 + jnp.exp(-jnp.abs(v)))


def _gelu_parts(v):
    v2 = v * v
    u = GELU_C0 * (v + GELU_C1 * v * v2)
    t = jnp.tanh(u)
    g = 0.5 * v * (1.0 + t)
    dg = 0.5 * (1.0 + t) + 0.5 * v * (1.0 - t * t) * (GELU_C0 * (1.0 + 3.0 * GELU_C1 * v2))
    return g, dg


def _gelu(v):
    u = GELU_C0 * (v + GELU_C1 * v * v * v)
    return 0.5 * v * (1.0 + jnp.tanh(u))


def _neg_expm1(v):
    series = -v * (1.0 + 0.5 * v * (1.0 + (v / 3.0) * (1.0 + 0.25 * v)))
    return jnp.where(v > -0.05, series, 1.0 - jnp.exp(v))


def _shift_down(v, j):
    if j == 0:
        return v
    rows = lax.broadcasted_iota(jnp.int32, v.shape, 0)
    return jnp.where(rows >= j, pltpu.roll(v, j, 0), 0.0)


def _shift_up(v, j):
    if j == 0:
        return v
    n = v.shape[0]
    rows = lax.broadcasted_iota(jnp.int32, v.shape, 0)
    return jnp.where(rows < n - j, pltpu.roll(v, n - j, 0), 0.0)


def _scan_rows(a_ref, b_ref, out_ref, n_rows, width, reverse):
    n_groups = n_rows // SUBLANES
    rows = lax.broadcasted_iota(jnp.int32, (SUBLANES, width), 0)
    edge = 0 if reverse else SUBLANES - 1

    def body(g, carry):
        grp = (n_groups - 1 - g) if reverse else g
        off = pl.multiple_of(grp * SUBLANES, SUBLANES)
        b = b_ref[pl.ds(off, SUBLANES), :]
        a = None if a_ref is None else a_ref[pl.ds(off, SUBLANES), :]
        for d in (1, 2, 4):
            if reverse:
                keep = rows < SUBLANES - d
                sh = SUBLANES - d
            else:
                keep = rows >= d
                sh = d
            b_s = jnp.where(keep, pltpu.roll(b, sh, 0), 0.0)
            if a is None:
                b = b + b_s
            else:
                a_s = jnp.where(keep, pltpu.roll(a, sh, 0), 1.0)
                b = a * b_s + b
                a = a * a_s
        h = b + carry if a is None else b + a * carry
        out_ref[pl.ds(off, SUBLANES), :] = h
        return jnp.sum(jnp.where(rows == edge, h, 0.0), axis=0, keepdims=True)

    lax.fori_loop(0, n_groups, body, jnp.zeros((1, width), F32), unroll=2)


def mm_in(x, w, out_dtype, name):
    t, k = x.shape
    s_n, _, ns = w.shape
    tn = _tile(ns, 1408)
    nj = ns // tn
    rb = _row_block(t)

    def body(x_ref, w_ref, o_ref):
        o_ref[...] = jnp.dot(x_ref[...], w_ref[...], preferred_element_type=F32).astype(o_ref.dtype)

    return pl.pallas_call(
        body, name=name, grid=(s_n, nj, t // rb),
        in_specs=[pl.BlockSpec((rb, k), lambda s, j, r: (r, 0)),
                  pl.BlockSpec((None, k, tn), lambda s, j, r: (s, 0, j))],
        out_specs=pl.BlockSpec((rb, tn), lambda s, j, r: (r, s * nj + j)),
        out_shape=jax.ShapeDtypeStruct((t, s_n * ns), out_dtype),
        compiler_params=_cp("parallel", "parallel", "parallel"))(x, w)


def _part_map(p, per_part, nj, lead):
    def index(*grid):
        s, j = grid[-2], grid[-1]
        mine = s // per_part == p
        col = jnp.where(mine, (s - p * per_part) * nj + j, 0)
        return (grid[0], col) if lead else (0, col)
    return index


def mm_in_nt(dy_parts, w, name):
    n_parts = len(dy_parts)
    t = dy_parts[0].shape[0]
    s_n, k, ns = w.shape
    per_part = s_n // n_parts
    tn = _tile(ns, 1408)
    nj = ns // tn
    rb = _row_block(t)

    def body(*refs):
        w_ref, o_ref = refs[n_parts:]

        @pl.when((pl.program_id(1) == 0) & (pl.program_id(2) == 0))
        def _():
            o_ref[...] = jnp.zeros_like(o_ref)

        for p in range(n_parts):
            @pl.when(pl.program_id(1) // per_part == p)
            def _():
                o_ref[...] += lax.dot_general(refs[p][...], w_ref[...], (((1,), (1,)), ((), ())),
                                              preferred_element_type=F32)

    return pl.pallas_call(
        body, name=name, grid=(t // rb, s_n, nj),
        in_specs=[pl.BlockSpec((rb, tn), _part_map(p, per_part, nj, True)) for p in range(n_parts)]
        + [pl.BlockSpec((None, k, tn), lambda r, s, j: (s, 0, j))],
        out_specs=pl.BlockSpec((rb, k), lambda r, s, j: (r, 0)),
        out_shape=jax.ShapeDtypeStruct((t, k), F32),
        compiler_params=_cp("parallel", "arbitrary", "arbitrary"))(*dy_parts, w)


def mm_out_nt(dy, w, name):
    t, n = dy.shape
    k = w.shape[0]
    rb = _row_block(t)

    def body(dy_ref, w_ref, o_ref):
        o_ref[...] = lax.dot_general(dy_ref[...], w_ref[...], (((1,), (1,)), ((), ())),
                                     preferred_element_type=F32)

    return pl.pallas_call(
        body, name=name, grid=(t // rb,),
        in_specs=[pl.BlockSpec((rb, n), lambda r: (r, 0)), pl.BlockSpec((k, n), lambda r: (0, 0))],
        out_specs=pl.BlockSpec((rb, k), lambda r: (r, 0)),
        out_shape=jax.ShapeDtypeStruct((t, k), F32),
        compiler_params=_cp("parallel"))(dy, w)


def mm_tn(x, dy_parts, s_n, name):
    n_parts = len(dy_parts)
    t, kb = x.shape
    nb = dy_parts[0].shape[1] * n_parts // s_n
    per_part = max(s_n // n_parts, 1)
    tk = _tile(kb, 1408)
    tn = _tile(nb, 1408)
    nkb, nnb = kb // tk, nb // tn
    tn_dims = (((0,), (0,)), ((), ()))

    def body(*refs):
        x_ref, o_ref = refs[0], refs[-1]
        for p in range(n_parts):
            @pl.when(pl.program_id(1) // per_part == p)
            def _():
                o_ref[...] = lax.dot_general(x_ref[...], refs[1 + p][...], tn_dims,
                                             preferred_element_type=F32).astype(o_ref.dtype)

    return pl.pallas_call(
        body, name=name, grid=(nkb, s_n, nnb),
        in_specs=[pl.BlockSpec((t, tk), lambda a, s, b: (0, a))]
        + [pl.BlockSpec((t, tn), _part_map(p, per_part, nnb, False)) for p in range(n_parts)],
        out_specs=pl.BlockSpec((None, tk, tn), lambda a, s, b: (s, a, b)),
        out_shape=jax.ShapeDtypeStruct((s_n, kb, nb), BF16),
        compiler_params=_cp("parallel", "parallel", "parallel"))(x, *dy_parts)


def mm_bd(x, wr, wi, name):
    t, _ = x.shape
    g_n, gs, _ = wr.shape
    rb = _row_block(t)

    def body(x_ref, wr_ref, wi_ref, r_ref, i_ref):
        xv = x_ref[...]
        r_ref[...] = jnp.dot(xv, wr_ref[...], preferred_element_type=F32)
        i_ref[...] = jnp.dot(xv, wi_ref[...], preferred_element_type=F32)

    blk = pl.BlockSpec((rb, gs), lambda g, r: (r, g))
    wspec = pl.BlockSpec((None, gs, gs), lambda g, r: (g, 0, 0))
    return pl.pallas_call(
        body, name=name, grid=(g_n, t // rb), in_specs=[blk, wspec, wspec], out_specs=[blk, blk],
        out_shape=[jax.ShapeDtypeStruct((t, g_n * gs), F32)] * 2,
        compiler_params=_cp("parallel", "parallel"))(x, wr, wi)


def mm_bd_nt(dr, di, wr, wi, name):
    t, _ = dr.shape
    g_n, gs, _ = wr.shape
    rb = _row_block(t)
    nt = (((1,), (1,)), ((), ()))

    def body(dr_ref, di_ref, wr_ref, wi_ref, o_ref):
        o_ref[...] = (lax.dot_general(dr_ref[...], wr_ref[...], nt, preferred_element_type=F32)
                      + lax.dot_general(di_ref[...], wi_ref[...], nt, preferred_element_type=F32))

    blk = pl.BlockSpec((rb, gs), lambda g, r: (r, g))
    wspec = pl.BlockSpec((None, gs, gs), lambda g, r: (g, 0, 0))
    return pl.pallas_call(
        body, name=name, grid=(g_n, t // rb), in_specs=[blk, blk, wspec, wspec], out_specs=blk,
        out_shape=jax.ShapeDtypeStruct((t, g_n * gs), F32),
        compiler_params=_cp("parallel", "parallel"))(dr, di, wr, wi)


def mm_bd_tn(x, dr, di, gs, name):
    t, w = x.shape
    g_n = w // gs
    tn_dims = (((0,), (0,)), ((), ()))

    def body(x_ref, dr_ref, di_ref, gr_ref, gi_ref):
        xv = x_ref[...]
        gr_ref[...] = lax.dot_general(xv, dr_ref[...], tn_dims, preferred_element_type=F32)
        gi_ref[...] = lax.dot_general(xv, di_ref[...], tn_dims, preferred_element_type=F32)

    blk = pl.BlockSpec((t, gs), lambda g: (0, g))
    ospec = pl.BlockSpec((None, gs, gs), lambda g: (g, 0, 0))
    return pl.pallas_call(
        body, name=name, grid=(g_n,), in_specs=[blk, blk, blk], out_specs=[ospec, ospec],
        out_shape=[jax.ShapeDtypeStruct((g_n, gs, gs), F32)] * 2,
        compiler_params=_cp("parallel"))(x, dr, di)


def embed_fwd(meta, x2d, name):
    nm, d = meta.shape
    seq = x2d.shape[0]
    t = nm + seq
    cb = _tile(d, 256)

    def body(m_ref, x_ref, h_ref, hb_ref):
        h_ref[pl.ds(0, nm), :] = m_ref[...]
        h_ref[pl.ds(nm, seq), :] = x_ref[...]
        hb_ref[pl.ds(0, nm), :] = m_ref[...].astype(BF16)
        hb_ref[pl.ds(nm, seq), :] = x_ref[...].astype(BF16)

    return pl.pallas_call(
        body, name=name, grid=(d // cb,),
        in_specs=[pl.BlockSpec((nm, cb), lambda j: (0, j)), pl.BlockSpec((seq, cb), lambda j: (0, j))],
        out_specs=[pl.BlockSpec((t, cb), lambda j: (0, j))] * 2,
        out_shape=[jax.ShapeDtypeStruct((t, d), F32), jax.ShapeDtypeStruct((t, d), BF16)],
        compiler_params=_cp("parallel"))(meta, x2d)


def embed_bwd(adds, nm, name):
    t, d = adds[0][0].shape
    seq = t - nm
    cb = _tile(d, 256)
    scales = [s for _, s in adds]
    n = len(adds)

    def body(*refs):
        tot = None
        for r, sc in zip(refs[:n], scales):
            term = r[...] if sc == 1.0 else sc * r[...]
            tot = term if tot is None else tot + term
        gm_ref, gx_ref = refs[n], refs[n + 1]
        gm_ref[...] = tot[0:nm]
        gx_ref[...] = tot[nm:t]

    return pl.pallas_call(
        body, name=name, grid=(d // cb,),
        in_specs=[pl.BlockSpec((t, cb), lambda j: (0, j))] * n,
        out_specs=[pl.BlockSpec((nm, cb), lambda j: (0, j)), pl.BlockSpec((seq, cb), lambda j: (0, j))],
        out_shape=[jax.ShapeDtypeStruct((nm, d), F32), jax.ShapeDtypeStruct((seq, d), F32)],
        compiler_params=_cp("parallel"))(*[a for a, _ in adds])


def loss_fwd_bwd(h, tgt, nm, name):
    t, d = h.shape
    seq = t - nm
    cb = _tile(d, 256)
    inv_d = 1.0 / d

    def body(h_ref, t_ref, loss_ref, dy_ref):
        @pl.when(pl.program_id(0) == 0)
        def _():
            loss_ref[...] = jnp.zeros_like(loss_ref)
        err = h_ref[pl.ds(nm, seq), :] - t_ref[...]
        dy_ref[pl.ds(0, nm), :] = jnp.zeros((nm, cb), F32)
        dy_ref[pl.ds(nm, seq), :] = err * inv_d
        loss_ref[...] += (0.5 * inv_d) * jnp.sum(err * err, keepdims=True)

    return pl.pallas_call(
        body, name=name, grid=(d // cb,),
        in_specs=[pl.BlockSpec((t, cb), lambda j: (0, j)), pl.BlockSpec((seq, cb), lambda j: (0, j))],
        out_specs=[pl.BlockSpec((1, 1), lambda j: (0, 0)), pl.BlockSpec((t, cb), lambda j: (0, j))],
        out_shape=[jax.ShapeDtypeStruct((1, 1), F32), jax.ShapeDtypeStruct((t, d), F32)],
        compiler_params=_cp("arbitrary"))(h, tgt)


def mm_out_ln(x, w, h, g, b, alpha, name):
    t, d = h.shape
    k = x.shape[1]
    rb = _row_block(t)

    def body(x_ref, w_ref, h_ref, g_ref, b_ref, y_ref, yb_ref, xh_ref, rs_ref):
        s = alpha * h_ref[...] + jnp.dot(x_ref[...], w_ref[...], preferred_element_type=F32)
        mu = jnp.mean(s, axis=-1, keepdims=True)
        c = s - mu
        var = jnp.mean(c * c, axis=-1, keepdims=True)
        rstd = lax.rsqrt(var + LN_EPS)
        xh = c * rstd
        y = xh * g_ref[...] + b_ref[...]
        y_ref[...] = y
        yb_ref[...] = y.astype(BF16)
        xh_ref[...] = xh
        rs_ref[...] = rstd

    row = pl.BlockSpec((rb, d), lambda r: (r, 0))
    vec = pl.BlockSpec((1, d), lambda r: (0, 0))
    return pl.pallas_call(
        body, name=name, grid=(t // rb,),
        in_specs=[pl.BlockSpec((rb, k), lambda r: (r, 0)), pl.BlockSpec((k, d), lambda r: (0, 0)), row, vec, vec],
        out_specs=[row, row, row, pl.BlockSpec((rb, 1), lambda r: (r, 0))],
        out_shape=[jax.ShapeDtypeStruct((t, d), F32), jax.ShapeDtypeStruct((t, d), BF16),
                   jax.ShapeDtypeStruct((t, d), F32), jax.ShapeDtypeStruct((t, 1), F32)],
        compiler_params=_cp("parallel"))(x, w, h, g, b)


def ln_bwd(adds, xhat, rstd, g, name):
    t, d = xhat.shape
    rb = _row_block(t)
    scales = [s for _, s in adds]
    n = len(adds)

    def body(*refs):
        xh_ref, rs_ref, g_ref = refs[n:n + 3]
        ds_ref, dsb_ref, dg_ref, db_ref = refs[n + 3:]
        dy = None
        for r, sc in zip(refs[:n], scales):
            term = r[...] if sc == 1.0 else sc * r[...]
            dy = term if dy is None else dy + term

        @pl.when(pl.program_id(0) == 0)
        def _():
            dg_ref[...] = jnp.zeros_like(dg_ref)
            db_ref[...] = jnp.zeros_like(db_ref)

        xh = xh_ref[...]
        dxh = dy * g_ref[...]
        m1 = jnp.mean(dxh, axis=-1, keepdims=True)
        m2 = jnp.mean(dxh * xh, axis=-1, keepdims=True)
        ds = rs_ref[...] * (dxh - m1 - xh * m2)
        ds_ref[...] = ds
        dsb_ref[...] = ds.astype(BF16)
        dg_ref[...] += jnp.sum(dy * xh, axis=0, keepdims=True)
        db_ref[...] += jnp.sum(dy, axis=0, keepdims=True)

    row = pl.BlockSpec((rb, d), lambda r: (r, 0))
    vec = pl.BlockSpec((1, d), lambda r: (0, 0))
    return pl.pallas_call(
        body, name=name, grid=(t // rb,),
        in_specs=[row] * n + [row, pl.BlockSpec((rb, 1), lambda r: (r, 0)), vec],
        out_specs=[row, row, vec, vec],
        out_shape=[jax.ShapeDtypeStruct((t, d), F32), jax.ShapeDtypeStruct((t, d), BF16),
                   jax.ShapeDtypeStruct((1, d), F32), jax.ShapeDtypeStruct((1, d), F32)],
        compiler_params=_cp("arbitrary"))(*[a for a, _ in adds], xhat, rstd, g)


def _conv_fwd_val(xv, w_ref, b_ref, width):
    y = b_ref[...]
    for j in range(width):
        y = y + _shift_down(xv, j) * w_ref[pl.ds(width - 1 - j, 1), :]
    return y


def _conv_bwd_val(dout, xv, w_ref, width):
    dx = None
    dws = [None] * width
    for j in range(width):
        k = width - 1 - j
        term = _shift_up(dout, j) * w_ref[pl.ds(k, 1), :]
        dx = term if dx is None else dx + term
        dws[k] = jnp.sum(dout * _shift_down(xv, j), axis=0, keepdims=True)
    return dx, dws, jnp.sum(dout, axis=0, keepdims=True)


def a_conv_fwd(gr, cw, cbias, name):
    t, two_dr = gr.shape
    dr = two_dr // 2
    width = cw.shape[0]
    cb = _tile(dr, 256)
    off = dr // cb

    def body(x_ref, w_ref, b_ref, rc_ref, rcb_ref):
        y = _conv_fwd_val(x_ref[...], w_ref, b_ref, width)
        rc_ref[...] = y
        rcb_ref[...] = y.astype(BF16)

    return pl.pallas_call(
        body, name=name, grid=(dr // cb,),
        in_specs=[pl.BlockSpec((t, cb), lambda j: (0, off + j)),
                  pl.BlockSpec((width, cb), lambda j: (0, j)), pl.BlockSpec((1, cb), lambda j: (0, j))],
        out_specs=[pl.BlockSpec((t, cb), lambda j: (0, j))] * 2,
        out_shape=[jax.ShapeDtypeStruct((t, dr), F32), jax.ShapeDtypeStruct((t, dr), BF16)],
        compiler_params=_cp("parallel"))(gr, cw, cbias)


def a_conv_bwd(drc_a, drc_b, gr, cw, dgr, name):
    t, two_dr = gr.shape
    dr = two_dr // 2
    width = cw.shape[0]
    cb = _tile(dr, 256)
    off = dr // cb

    def body(da_ref, db_ref, x_ref, w_ref, dgr_in, dx_ref, dw_ref, dbias_ref):
        del dgr_in
        dout = da_ref[...] + db_ref[...]
        dx, dws, dbias = _conv_bwd_val(dout, x_ref[...], w_ref, width)
        dx_ref[...] = dx.astype(BF16)
        for k in range(width):
            dw_ref[pl.ds(k, 1), :] = dws[k]
        dbias_ref[...] = dbias

    col = pl.BlockSpec((t, cb), lambda j: (0, j))
    return pl.pallas_call(
        body, name=name, grid=(dr // cb,),
        in_specs=[col, col, pl.BlockSpec((t, cb), lambda j: (0, off + j)),
                  pl.BlockSpec((width, cb), lambda j: (0, j)), pl.BlockSpec(memory_space=pl.ANY)],
        out_specs=[pl.BlockSpec((t, cb), lambda j: (0, off + j)), pl.BlockSpec((width, cb), lambda j: (0, j)),
                   pl.BlockSpec((1, cb), lambda j: (0, j))],
        out_shape=[jax.ShapeDtypeStruct((t, two_dr), BF16), jax.ShapeDtypeStruct((width, dr), F32),
                   jax.ShapeDtypeStruct((1, dr), F32)],
        input_output_aliases={4: 0},
        compiler_params=_cp("parallel"))(drc_a, drc_b, gr, cw, dgr)


def _lru_gates(r_pre, i_pre, br, bi, lam):
    r = _sigmoid(r_pre + br)
    i = _sigmoid(i_pre + bi)
    sp = _softplus(-lam)
    la = -LRU_C * r * sp
    a = jnp.exp(la)
    m = jnp.sqrt(_neg_expm1(2.0 * la))
    return r, i, sp, la, a, m


def a_elem_fwd(gr, rc, r_pre, i_pre, br, bi, lam, name):
    t, dr = rc.shape
    cb = _tile(dr, 2 * LANES)
    rb = _row_block(t)
    chunks = [pl.ds(r * rb, rb) for r in range(t // rb)]

    def body(gate_ref, rc_ref, rp_ref, ip_ref, br_ref, bi_ref, lam_ref, hs_ref, g_ref, a_s, u_s):
        for rows in chunks:
            _, i, _, _, a, m = _lru_gates(rp_ref[rows, :], ip_ref[rows, :], br_ref[...], bi_ref[...], lam_ref[...])
            a_s[rows, :] = a
            u_s[rows, :] = m * (i * rc_ref[rows, :])
        _scan_rows(a_s, u_s, hs_ref, t, cb, reverse=False)
        for rows in chunks:
            g_ref[rows, :] = (_gelu(gate_ref[rows, :]) * hs_ref[rows, :]).astype(BF16)

    col = pl.BlockSpec((t, cb), lambda j: (0, j))
    vec = pl.BlockSpec((1, cb), lambda j: (0, j))
    return pl.pallas_call(
        body, name=name, grid=(dr // cb,),
        in_specs=[col, col, col, col, vec, vec, vec],
        out_specs=[col, col],
        out_shape=[jax.ShapeDtypeStruct((t, dr), F32), jax.ShapeDtypeStruct((t, dr), BF16)],
        scratch_shapes=[pltpu.VMEM((t, cb), F32), pltpu.VMEM((t, cb), F32)],
        compiler_params=_cp("parallel"))(gr, rc, r_pre, i_pre, br, bi, lam)


def a_elem_bwd(dg, gr, rc, r_pre, i_pre, hs, br, bi, lam, name):
    t, dr = rc.shape
    cb = _tile(dr, 2 * LANES)
    rb = _row_block(t)
    chunks = [pl.ds(r * rb, rb) for r in range(t // rb)]

    def body(dg_ref, gate_ref, rc_ref, rp_ref, ip_ref, hs_ref, br_ref, bi_ref, lam_ref,
             dgate_ref, dr_ref, di_ref, drc_ref, dlam_ref, dbr_ref, dbi_ref, a_s, b_s, g_s, hp_s):
        lamv = lam_ref[...]
        gates = lambda rows: _lru_gates(rp_ref[rows, :], ip_ref[rows, :], br_ref[...], bi_ref[...], lamv)
        for rows in chunks:
            a_s[rows, :] = gates(rows)[4]
            ge, dge = _gelu_parts(gate_ref[rows, :])
            dgv = dg_ref[rows, :]
            dgate_ref[rows, :] = (dgv * hs_ref[rows, :] * dge).astype(BF16)
            b_s[rows, :] = dgv * ge
        a_s[...] = _shift_up(a_s[...], 1)
        hp_s[...] = _shift_down(hs_ref[...], 1)
        _scan_rows(a_s, b_s, g_s, t, cb, reverse=True)
        dsp = dbr = dbi = jnp.zeros((1, cb), F32)
        for rows in chunks:
            r, i, sp, _, a, m = gates(rows)
            rcv = rc_ref[rows, :]
            gsum = g_s[rows, :]
            da = gsum * hp_s[rows, :]
            dm = gsum * (i * rcv)
            d_i = gsum * m * rcv
            drc_ref[rows, :] = gsum * m * i
            dla = a * da - dm * (a * a) / m
            d_r = (-LRU_C) * sp * dla
            dsp = dsp + jnp.sum((-LRU_C) * r * dla, axis=0, keepdims=True)
            d_rp = d_r * r * (1.0 - r)
            d_ip = d_i * i * (1.0 - i)
            dr_ref[rows, :] = d_rp.astype(BF16)
            di_ref[rows, :] = d_ip.astype(BF16)
            dbr = dbr + jnp.sum(d_rp, axis=0, keepdims=True)
            dbi = dbi + jnp.sum(d_ip, axis=0, keepdims=True)
        dlam_ref[...] = -dsp * _sigmoid(-lamv)
        dbr_ref[...] = dbr
        dbi_ref[...] = dbi

    col = pl.BlockSpec((t, cb), lambda j: (0, j))
    vec = pl.BlockSpec((1, cb), lambda j: (0, j))
    big_b = jax.ShapeDtypeStruct((t, dr), BF16)
    vec_s = jax.ShapeDtypeStruct((1, dr), F32)
    return pl.pallas_call(
        body, name=name, grid=(dr // cb,),
        in_specs=[col, col, col, col, col, col, vec, vec, vec],
        out_specs=[col, col, col, col, vec, vec, vec],
        out_shape=[jax.ShapeDtypeStruct((t, 2 * dr), BF16), big_b, big_b, jax.ShapeDtypeStruct((t, dr), F32),
                   vec_s, vec_s, vec_s],
        scratch_shapes=[pltpu.VMEM((t, cb), F32)] * 4,
        compiler_params=_cp("parallel"))(dg, gr, rc, r_pre, i_pre, hs, br, bi, lam)


def f_elem_fwd(z, cw, cbias, name):
    t, two_f = z.shape
    dff = two_f // 2
    width = cw.shape[0]
    cb = _tile(dff, 256)
    off = dff // cb

    def body(zg_ref, zv_ref, wg_ref, wv_ref, bg_ref, bv_ref, o_ref):
        zcg = _conv_fwd_val(zg_ref[...], wg_ref, bg_ref, width)
        zcv = _conv_fwd_val(zv_ref[...], wv_ref, bv_ref, width)
        o_ref[...] = (_gelu(zcg) * zcv).astype(BF16)

    lo = lambda j: (0, j)
    hi = lambda j: (0, off + j)
    return pl.pallas_call(
        body, name=name, grid=(dff // cb,),
        in_specs=[pl.BlockSpec((t, cb), lo), pl.BlockSpec((t, cb), hi),
                  pl.BlockSpec((width, cb), lo), pl.BlockSpec((width, cb), hi),
                  pl.BlockSpec((1, cb), lo), pl.BlockSpec((1, cb), hi)],
        out_specs=pl.BlockSpec((t, cb), lo),
        out_shape=jax.ShapeDtypeStruct((t, dff), BF16),
        compiler_params=_cp("parallel"))(z, z, cw, cw, cbias, cbias)


def f_elem_bwd(z, dff_g, cw, cbias, name):
    t, two_f = z.shape
    dff = two_f // 2
    width = cw.shape[0]
    cb = _tile(dff, 256)
    off = dff // cb

    def body(zg_ref, zv_ref, d_ref, wg_ref, wv_ref, bg_ref, bv_ref,
             dzg_ref, dzv_ref, dwg_ref, dwv_ref, dbg_ref, dbv_ref):
        zg = zg_ref[...]
        zv = zv_ref[...]
        zcg = _conv_fwd_val(zg, wg_ref, bg_ref, width)
        zcv = _conv_fwd_val(zv, wv_ref, bv_ref, width)
        ge, dge = _gelu_parts(zcg)
        dv = d_ref[...]
        dx, dws, dbias = _conv_bwd_val(dv * zcv * dge, zg, wg_ref, width)
        dzg_ref[...] = dx.astype(BF16)
        for k in range(width):
            dwg_ref[pl.ds(k, 1), :] = dws[k]
        dbg_ref[...] = dbias
        dx, dws, dbias = _conv_bwd_val(dv * ge, zv, wv_ref, width)
        dzv_ref[...] = dx.astype(BF16)
        for k in range(width):
            dwv_ref[pl.ds(k, 1), :] = dws[k]
        dbv_ref[...] = dbias

    lo = lambda j: (0, j)
    hi = lambda j: (0, off + j)
    col = pl.BlockSpec((t, cb), lo)
    wsp = pl.BlockSpec((width, cb), lo)
    vsp = pl.BlockSpec((1, cb), lo)
    return pl.pallas_call(
        body, name=name, grid=(dff // cb,),
        in_specs=[col, pl.BlockSpec((t, cb), hi), col, wsp, pl.BlockSpec((width, cb), hi),
                  vsp, pl.BlockSpec((1, cb), hi)],
        out_specs=[col, col, wsp, wsp, vsp, vsp],
        out_shape=[jax.ShapeDtypeStruct((t, dff), BF16)] * 2
        + [jax.ShapeDtypeStruct((width, dff), F32)] * 2 + [jax.ShapeDtypeStruct((1, dff), F32)] * 2,
        compiler_params=_cp("parallel"))(z, z, dff_g, cw, cw, cbias, cbias)


def kv_fwd(z, fb, d_model, name):
    t, _ = z.shape
    blk = 2 * d_model // LANES

    def body(z_ref, fb_ref, c_ref, lf_s):
        v = z_ref[...] + fb_ref[...]
        lf_s[...] = -_softplus(-v)
        _scan_rows(None, lf_s, c_ref, t, LANES, reverse=False)

    return pl.pallas_call(
        body, name=name, grid=(1,),
        in_specs=[pl.BlockSpec((t, LANES), lambda j: (0, blk)), pl.BlockSpec((1, LANES), lambda j: (0, 0))],
        out_specs=pl.BlockSpec((t, LANES), lambda j: (0, 0)),
        out_shape=jax.ShapeDtypeStruct((t, LANES), F32),
        scratch_shapes=[pltpu.VMEM((t, LANES), F32)],
        compiler_params=_cp("arbitrary"))(z, fb)


def kv_bwd(dcs, z, fb, d_model, name):
    t, _ = z.shape
    blk = 2 * d_model // LANES
    n = len(dcs)

    def body(*refs):
        z_ref, fb_ref, dz_ref, dfb_ref, dc_s, dl_s = refs[n:]
        tot = refs[0][...]
        for r in refs[1:n]:
            tot = tot + r[...]
        dc_s[...] = tot
        _scan_rows(None, dc_s, dl_s, t, LANES, reverse=True)
        v = z_ref[...] + fb_ref[...]
        dz = dl_s[...] * _sigmoid(-v)
        dz_ref[...] = dz.astype(BF16)
        dfb_ref[...] = jnp.sum(dz, axis=0, keepdims=True)

    full = pl.BlockSpec((t, LANES), lambda j: (0, 0))
    return pl.pallas_call(
        body, name=name, grid=(1,),
        in_specs=[full] * n + [pl.BlockSpec((t, LANES), lambda j: (0, blk)),
                               pl.BlockSpec((1, LANES), lambda j: (0, 0))],
        out_specs=[full, pl.BlockSpec((1, LANES), lambda j: (0, 0))],
        out_shape=[jax.ShapeDtypeStruct((t, LANES), BF16), jax.ShapeDtypeStruct((1, LANES), F32)],
        scratch_shapes=[pltpu.VMEM((t, LANES), F32)] * 2,
        compiler_params=_cp("arbitrary"))(*dcs, z, fb)


def add_cast(a, b, name):
    t, d = a.shape
    cb = _tile(d, 512)

    def body(a_ref, b_ref, o_ref):
        o_ref[...] = (a_ref[...] + b_ref[...]).astype(BF16)

    col = pl.BlockSpec((t, cb), lambda j: (0, j))
    return pl.pallas_call(body, name=name, grid=(d // cb,), in_specs=[col, col], out_specs=col,
                          out_shape=jax.ShapeDtypeStruct((t, d), BF16),
                          compiler_params=_cp("parallel"))(a, b)


def _attn_geometry(t):
    nqb = 6 if t > 1024 else 2
    tp = _round_up(t, LANES * nqb)
    return nqb, tp, tp // nqb


def _attn_scales(dh):
    scale = dh ** -0.5
    if math.log2(scale).is_integer():
        return scale, 1.0
    return 1.0, scale


def _attn_pieces(qs, ks, crow, j, i, tq, dh, s_mul):
    r0 = i * tq
    lanes = pl.ds(j * dh, dh)
    qi = qs[pl.ds(r0, tq), lanes]
    spans = ([(0, r0)] if i > 0 else []) + [(r0, tq)]
    logits = []
    for k0, n in spans:
        s = lax.dot_general(qi, ks[pl.ds(k0, n), lanes], (((1,), (1,)), ((), ())),
                            preferred_element_type=F32)
        if s_mul != 1.0:
            s = s * s_mul
        s = s - crow[:, k0:k0 + n]
        if k0 == r0:
            rows = lax.broadcasted_iota(jnp.int32, (tq, tq), 0)
            cols = lax.broadcasted_iota(jnp.int32, (tq, tq), 1)
            s = jnp.where(cols <= rows, s, NEG_BIG)
        logits.append(s)
    mx = jnp.max(logits[0], axis=1, keepdims=True)
    for s in logits[1:]:
        mx = jnp.maximum(mx, jnp.max(s, axis=1, keepdims=True))
    es = [jnp.exp(s - mx) for s in logits]
    tot = jnp.sum(es[0], axis=1, keepdims=True)
    for e in es[1:]:
        tot = tot + jnp.sum(e, axis=1, keepdims=True)
    inv = 1.0 / tot
    return [(k0, n, e * inv) for (k0, n), e in zip(spans, es)], qi


def attn_fwd(qg, z, ct_pad, d_model, n_heads, name):
    t = qg.shape[0]
    dh = d_model // n_heads
    hp = LANES // dh
    nqb, tp, tq = _attn_geometry(t)
    nblk = d_model // LANES
    q_mul, s_mul = _attn_scales(dh)

    def body(q_ref, og_ref, k_ref, v_ref, ct_ref, o_ref, mo_ref, qs, ks, vs, os_):
        pad = jnp.zeros((tp - t, LANES), BF16)
        qs[pl.ds(0, t), :] = (q_ref[...] * q_mul).astype(BF16)
        qs[pl.ds(t, tp - t), :] = pad
        for src, dst in ((k_ref, ks), (v_ref, vs)):
            dst[pl.ds(0, t), :] = src[...].astype(BF16)
            dst[pl.ds(t, tp - t), :] = pad
        for j in range(hp):
            crow = ct_ref[j]
            lanes = pl.ds(j * dh, dh)
            for i in range(nqb):
                pieces, _ = _attn_pieces(qs, ks, crow, j, i, tq, dh, s_mul)
                acc = None
                for k0, n, p in pieces:
                    part = jnp.dot(p.astype(BF16), vs[pl.ds(k0, n), lanes], preferred_element_type=F32)
                    acc = part if acc is None else acc + part
                os_[pl.ds(i * tq, tq), lanes] = acc
        o = os_[pl.ds(0, t), :]
        o_ref[...] = o
        mo_ref[...] = (o * _sigmoid(og_ref[...])).astype(BF16)

    col = lambda off: pl.BlockSpec((t, LANES), lambda p: (0, off + p))
    return pl.pallas_call(
        body, name=name, grid=(nblk,),
        in_specs=[col(0), col(nblk), col(0), col(nblk), pl.BlockSpec((hp, 1, tp), lambda p: (p, 0, 0))],
        out_specs=[col(0), col(0)],
        out_shape=[jax.ShapeDtypeStruct((t, d_model), F32), jax.ShapeDtypeStruct((t, d_model), BF16)],
        scratch_shapes=[pltpu.VMEM((tp, LANES), BF16)] * 3 + [pltpu.VMEM((tp, LANES), F32)],
        compiler_params=_cp("parallel"))(qg, qg, z, z, ct_pad)


def attn_bwd(dmo, qg, z, o, ct_pad, d_model, n_heads, name):
    t = qg.shape[0]
    dh = d_model // n_heads
    hp = LANES // dh
    nqb, tp, tq = _attn_geometry(t)
    nblk = d_model // LANES
    q_mul, s_mul = _attn_scales(dh)
    scale = dh ** -0.5
    tn_dims = (((0,), (0,)), ((), ()))
    nt_dims = (((1,), (1,)), ((), ()))

    def body(dmo_ref, q_ref, og_ref, k_ref, v_ref, o_ref, ct_ref,
             dq_ref, dog_ref, dk_ref, dv_ref, dct_ref, qs, ks, vs, dos, dqs, dks, dvs):
        pad = jnp.zeros((tp - t, LANES), BF16)
        sg = _sigmoid(og_ref[...])
        dmo_v = dmo_ref[...]
        dog_ref[...] = (dmo_v * o_ref[...] * sg * (1.0 - sg)).astype(BF16)
        dos[pl.ds(0, t), :] = (dmo_v * sg).astype(BF16)
        dos[pl.ds(t, tp - t), :] = pad
        qs[pl.ds(0, t), :] = (q_ref[...] * q_mul).astype(BF16)
        qs[pl.ds(t, tp - t), :] = pad
        for src, dst in ((k_ref, ks), (v_ref, vs)):
            dst[pl.ds(0, t), :] = src[...].astype(BF16)
            dst[pl.ds(t, tp - t), :] = pad
        dks[...] = jnp.zeros_like(dks)
        dvs[...] = jnp.zeros_like(dvs)
        dct_ref[...] = jnp.zeros_like(dct_ref)
        for j in range(hp):
            crow = ct_ref[j]
            lanes = pl.ds(j * dh, dh)
            for i in range(nqb):
                pieces, qi = _attn_pieces(qs, ks, crow, j, i, tq, dh, s_mul)
                do_i = dos[pl.ds(i * tq, tq), lanes]
                dps = [lax.dot_general(do_i, vs[pl.ds(k0, n), lanes], nt_dims, preferred_element_type=F32)
                       for k0, n, _ in pieces]
                row = None
                for (_, _, p), dp in zip(pieces, dps):
                    part = jnp.sum(p * dp, axis=1, keepdims=True)
                    row = part if row is None else row + part
                dq_i = None
                for (k0, n, p), dp in zip(pieces, dps):
                    ds = p * (dp - row)
                    ds_b = ds.astype(BF16)
                    keys = pl.ds(k0, n)
                    part = jnp.dot(ds_b, ks[keys, lanes], preferred_element_type=F32)
                    dq_i = part if dq_i is None else dq_i + part
                    dks[keys, lanes] += lax.dot_general(ds_b, qi, tn_dims, preferred_element_type=F32) * s_mul
                    dvs[keys, lanes] += lax.dot_general(p.astype(BF16), do_i, tn_dims,
                                                        preferred_element_type=F32)
                    dct_ref[j, :, keys] -= jnp.sum(ds, axis=0, keepdims=True)
                dqs[pl.ds(i * tq, tq), lanes] = dq_i * scale
        dq_ref[...] = dqs[pl.ds(0, t), :].astype(BF16)
        dk_ref[...] = dks[pl.ds(0, t), :]
        dv_ref[...] = dvs[pl.ds(0, t), :]

    col = lambda off: pl.BlockSpec((t, LANES), lambda p: (0, off + p))
    big = lambda dt: jax.ShapeDtypeStruct((t, d_model), dt)
    return pl.pallas_call(
        body, name=name, grid=(nblk,),
        in_specs=[col(0), col(0), col(nblk), col(0), col(nblk), col(0),
                  pl.BlockSpec((hp, 1, tp), lambda p: (p, 0, 0))],
        out_specs=[col(0), col(0), col(0), col(0), pl.BlockSpec((hp, 1, tp), lambda p: (p, 0, 0))],
        out_shape=[big(BF16), big(BF16), big(F32), big(F32),
                   jax.ShapeDtypeStruct((n_heads, 1, tp), F32)],
        scratch_shapes=[pltpu.VMEM((tp, LANES), BF16)] * 4 + [pltpu.VMEM((tp, LANES), F32)] * 3,
        compiler_params=_cp("parallel"))(dmo, qg, qg, z, z, o, ct_pad)


def cast_into_slot(shard, w2d, name):
    r, c = w2d.shape
    rh = r // 2
    tr = _tile(rh, 512, 16)
    n = rh // tr

    def body(sh_ref, w_ref, o_ref):
        del sh_ref
        o_ref[...] = w_ref[...].astype(BF16)

    return pl.pallas_call(
        body, name=name,
        grid_spec=pltpu.PrefetchScalarGridSpec(
            num_scalar_prefetch=1, grid=(2, n),
            in_specs=[pl.BlockSpec((tr, c), lambda h, i, sh: (h * n + i, 0))],
            out_specs=pl.BlockSpec((None, None, tr, c), lambda h, i, sh: (sh[0], h, i, 0))),
        out_shape=jax.ShapeDtypeStruct((N_SHARDS, 2, rh, c), BF16),
        compiler_params=_cp("parallel", "parallel"))(shard, w2d)


def add_halves(core, g, other, name):
    s_n, _, rh, c = g.shape
    tr = _tile(rh, 512, 16)

    def body(core_ref, g_ref, o_ref, out_ref):
        del core_ref
        out_ref[...] = (g_ref[...].astype(F32) + o_ref[...].astype(F32)).astype(out_ref.dtype)

    return pl.pallas_call(
        body, name=name,
        grid_spec=pltpu.PrefetchScalarGridSpec(
            num_scalar_prefetch=1, grid=(s_n, rh // tr),
            in_specs=[pl.BlockSpec((None, None, tr, c), lambda s, i, cr: (s, cr[0], i, 0)),
                      pl.BlockSpec((None, tr, c), lambda s, i, cr: (s, i, 0))],
            out_specs=pl.BlockSpec((None, tr, c), lambda s, i, cr: (s, i, 0))),
        out_shape=jax.ShapeDtypeStruct((s_n, rh, c), g.dtype),
        compiler_params=_cp("parallel", "parallel"))(core, g, other)


def add_four(shard_core, csum, recv, buf, layer, name):
    _, rh, c = csum.shape
    tr = _tile(rh, 512, 16)
    slot = (lambda sc: sc[0]) if layer is None else (lambda sc: layer)

    def body(sc_ref, a_ref, r_ref, buf_ref, out_ref):
        del sc_ref, buf_ref
        acc = a_ref[...].astype(F32)
        for k in range(3):
            acc = acc + r_ref[k].astype(F32)
        out_ref[...] = acc

    return pl.pallas_call(
        body, name=name,
        grid_spec=pltpu.PrefetchScalarGridSpec(
            num_scalar_prefetch=1, grid=(rh // tr,),
            in_specs=[pl.BlockSpec((None, tr, c), lambda i, sc: (sc[0], i, 0)),
                      pl.BlockSpec((3, tr, c), lambda i, sc: (0, i, 0)),
                      pl.BlockSpec(memory_space=pl.ANY)],
            out_specs=pl.BlockSpec((None, None, tr, c), lambda i, sc: (slot(sc), sc[1], i, 0))),
        out_shape=jax.ShapeDtypeStruct(buf.shape, F32),
        input_output_aliases={3: 0},
        compiler_params=_cp("parallel"))(shard_core, csum, recv, buf)


def adamw(w, g, m, v, name):
    r, c = w.shape
    tr = _tile(r, 512, SUBLANES)
    c1 = 1.0 - ADAM_B1 ** ADAM_STEP
    c2 = 1.0 - ADAM_B2 ** ADAM_STEP

    def body(w_ref, g_ref, m_ref, v_ref, d_ref, mo_ref, vo_ref):
        gv = g_ref[...]
        mn = ADAM_B1 * m_ref[...] + (1.0 - ADAM_B1) * gv
        vn = ADAM_B2 * v_ref[...] + (1.0 - ADAM_B2) * (gv * gv)
        m_hat = mn / c1
        v_hat = vn / c2
        d_ref[...] = -ADAM_LR * (m_hat / (jnp.sqrt(v_hat) + ADAM_EPS) + ADAM_WD * w_ref[...])
        mo_ref[...] = mn
        vo_ref[...] = vn

    blk = pl.BlockSpec((tr, c), lambda i: (i, 0))
    return pl.pallas_call(
        body, name=name, grid=(r // tr,), in_specs=[blk] * 4, out_specs=[blk] * 3,
        out_shape=[jax.ShapeDtypeStruct((r, c), F32)] * 3,
        compiler_params=_cp("parallel"))(w, g, m, v)


def _coords():
    return lax.axis_index("x"), lax.axis_index("y"), lax.axis_index("c")


def _exchange(name, ins, out_shapes, plan, in_place=False):
    n_in = len(ins)
    n_out = len(out_shapes)
    n_rem = len(plan([None] * n_in, [None] * n_out, True))

    def body(*refs):
        in_refs = refs[:n_in]
        out_refs = refs[n_in:n_in + n_out]
        send_sems, recv_sems = refs[n_in + n_out:]
        remote = plan(list(in_refs), list(out_refs), False)
        copies = [pltpu.make_async_remote_copy(
            src_ref=src, dst_ref=dst, send_sem=send_sems.at[q], recv_sem=recv_sems.at[q],
            device_id=peer, device_id_type=pl.DeviceIdType.MESH)
            for q, (src, dst, peer, _) in enumerate(remote)]
        waited = set()
        for q, (_, _, _, after) in enumerate(remote):
            if after is not None and after not in waited:
                copies[after].wait_recv()
                waited.add(after)
            copies[q].start()
        for q, cp in enumerate(copies):
            if q not in waited:
                cp.wait_recv()
        for cp in copies:
            cp.wait_send()

    hbm = pl.BlockSpec(memory_space=pl.ANY)
    return pl.pallas_call(
        body, name=name, in_specs=[hbm] * n_in, out_specs=[hbm] * n_out, out_shape=out_shapes,
        input_output_aliases={i: i for i in range(n_in)} if in_place else {},
        scratch_shapes=[pltpu.SemaphoreType.DMA((n_rem,)), pltpu.SemaphoreType.DMA((n_rem,))],
        compiler_params=pltpu.CompilerParams(has_side_effects=True))(*ins)


def _split_start(name, groups, plan):
    flat = [a for grp in groups for a in grp]
    n, n_grp = len(flat), len(groups)
    counts = [len(plan(g, [None] * len(grp), True)) for g, grp in enumerate(groups)]

    def body(*refs):
        ins, sems, token = refs[:n], refs[n:n + 2 * n_grp], refs[-1]
        pos = 0
        for g, grp in enumerate(groups):
            arrs = list(ins[pos:pos + len(grp)])
            pos += len(grp)
            for q, (src, dst, peer) in enumerate(plan(g, arrs, False)):
                pltpu.make_async_remote_copy(
                    src_ref=src, dst_ref=dst, send_sem=sems[2 * g].at[q], recv_sem=sems[2 * g + 1].at[q],
                    device_id=peer, device_id_type=pl.DeviceIdType.MESH).start()
        token[...] = jnp.zeros_like(token)

    hbm = pl.BlockSpec(memory_space=pltpu.HBM)
    sem = pl.BlockSpec(memory_space=pltpu.SEMAPHORE)
    outs = pl.pallas_call(
        body, name=name,
        out_shape=[pltpu.SemaphoreType.DMA((cnt,)) for cnt in counts for _ in range(2)]
        + [pltpu.HBM(a.shape, a.dtype) for a in flat] + [jax.ShapeDtypeStruct((SUBLANES, LANES), F32)],
        in_specs=[hbm] * n, out_specs=[sem] * (2 * n_grp) + [hbm] * n + [pl.BlockSpec(memory_space=pltpu.VMEM)],
        input_output_aliases={i: 2 * n_grp + i for i in range(n)},
        compiler_params=pltpu.CompilerParams(has_side_effects=pltpu.SideEffectType.DATAFLOW_SIDE_EFFECTING),
    )(*[pltpu.with_memory_space_constraint(a, pltpu.HBM) for a in flat])
    started, pos = [], 2 * n_grp
    for g, grp in enumerate(groups):
        started.append((outs[2 * g], outs[2 * g + 1], list(outs[pos:pos + len(grp)])))
        pos += len(grp)
    return started, outs[-1]


def _split_wait(name, started, after, plan_g):
    send_sems, recv_sems, arrs = started
    n = len(arrs)

    def body(*refs):
        ins, ssem, rsem = list(refs[:n]), refs[n], refs[n + 1]
        for q, (src, dst, peer) in enumerate(plan_g(ins, False)):
            cp = pltpu.make_async_remote_copy(
                src_ref=src, dst_ref=dst, send_sem=ssem.at[q], recv_sem=rsem.at[q],
                device_id=peer, device_id_type=pl.DeviceIdType.MESH)
            cp.wait_send()
            cp.wait_recv()

    hbm = pl.BlockSpec(memory_space=pltpu.HBM)
    sem = pl.BlockSpec(memory_space=pltpu.SEMAPHORE)
    return pl.pallas_call(
        body, name=name, out_shape=[pltpu.HBM(a.shape, a.dtype) for a in arrs],
        in_specs=[hbm] * n + [sem, sem, pl.BlockSpec(memory_space=pl.ANY)], out_specs=[hbm] * n,
        input_output_aliases={i: i for i in range(n)},
        compiler_params=pltpu.CompilerParams(has_side_effects=pltpu.SideEffectType.DATAFLOW_SIDE_EFFECTING),
    )(*arrs, send_sems, recv_sems, after)


def _gather_ici_plan(arrs, count_only):
    if count_only:
        return [None] * (3 * len(arrs))
    x, y, c = _coords()
    pushes = []
    for a in arrs:
        mine = a.at[2 * x + y, c]
        pushes += [(mine, mine, peer) for peer, _ in _other_chips(x, y, c)]
    return pushes


def _all_to_all_plan(arrs, count_only):
    half = len(arrs) // 2
    if count_only:
        return [None] * (3 * half)
    x, y, c = _coords()
    pushes = []
    for src, land in zip(arrs[:half], arrs[half:]):
        pushes += [(src.at[shard], land.at[k], peer) for k, (peer, shard) in enumerate(_other_chips(x, y, c))]
    return pushes


def forward_to_sibling(bufs, name):
    n = len(bufs)

    def plan(ins, outs, count_only):
        if count_only:
            return [None] * (3 * n)
        x, y, c = _coords()
        pushes = []
        for i in range(n):
            for _, src_shard in _other_chips(x, y, c):
                slab = outs[i].at[src_shard, c]
                pushes.append((slab, slab, (x, y, 1 - c), None))
        return pushes

    shapes = [jax.ShapeDtypeStruct(b.shape, b.dtype) for b in bufs]
    return _exchange(name, bufs, shapes, plan, in_place=True)


def _other_chips(x, y, c):
    return [((1 - x, y, c), 2 * (1 - x) + y), ((x, 1 - y, c), 2 * x + 1 - y),
            ((1 - x, 1 - y, c), 2 * (1 - x) + 1 - y)]


def swap_halves(grads, name):
    n = len(grads)
    shapes = [jax.ShapeDtypeStruct((g.shape[0],) + g.shape[2:], g.dtype) for g in grads]

    def plan(ins, outs, count_only):
        if count_only:
            return [None] * (n * N_SHARDS)
        x, y, c = _coords()
        remote = []
        for i in range(n):
            for s in range(N_SHARDS):
                remote.append((ins[i].at[s, 1 - c], outs[i].at[s], (x, y, 1 - c), None))
        return remote

    return _exchange(name, grads, shapes, plan)


def join_halves(bufs, everywhere, name):
    slots = [(i, l) for i, b in enumerate(bufs) for l in range(b.shape[0])]
    n = len(bufs)

    def plan(ins, outs, count_only):
        if count_only:
            return [None] * (len(slots) + 7)
        x, y, c = _coords()
        pushes = [(outs[i].at[l, c], outs[i].at[l, c], (x, y, 1 - c), None) for i, l in slots]
        mine = outs[n].at[2 * x + y, c]
        for flips in range(1, 8):
            peer = (1 - x if flips & 4 else x, 1 - y if flips & 2 else y, 1 - c if flips & 1 else c)
            pushes.append((mine, mine, peer, None))
        return pushes

    arrs = list(bufs) + [everywhere]
    shapes = [jax.ShapeDtypeStruct(b.shape, b.dtype) for b in arrs]
    return _exchange(name, arrs, shapes, plan, in_place=True)


def _pack(arrays, multiple):
    flat = jnp.concatenate([a.reshape(-1) for a in arrays])
    n = flat.shape[0]
    return jnp.pad(flat, (0, _round_up(n, multiple) - n))


def _unpack(flat, shapes):
    out, pos = [], 0
    for shp in shapes:
        n = math.prod(shp)
        out.append(flat[pos:pos + n].reshape(shp))
        pos += n
    return out


def _block_diag(w, per_group):
    nb, bs, _ = w.shape
    g = nb // per_group
    w4 = w.reshape(g, per_group, bs, bs)
    eye = jnp.eye(per_group, dtype=w.dtype)
    full = w4[:, :, :, None, :] * eye[None, :, None, :, None]
    return full.reshape(g, per_group * bs, per_group * bs).astype(BF16)


def _block_diag_extract(full, per_group, bs):
    g = full.shape[0]
    f5 = full.reshape(g, per_group, bs, per_group, bs)
    idx = jnp.arange(per_group)
    picked = f5[:, idx, :, idx, :]
    return jnp.moveaxis(picked, 0, 1).reshape(g * per_group, bs, bs)


def kernel(x, meta, a_w_in, a_conv_w, a_conv_b, a_w_r, a_b_r, a_w_i, a_b_i, a_lambda, a_w_out, kv_w, kv_f_b, b_w_in, b_w_out, f_w_in, f_conv_w, f_conv_b, f_w_out, ln1_g, ln1_b, ln2_g, ln2_b, loss_target, m_meta, m_a_w_in, m_a_conv_w, m_a_conv_b, m_a_w_r, m_a_b_r, m_a_w_i, m_a_b_i, m_a_lambda, m_a_w_out, m_kv_w, m_kv_f_b, m_b_w_in, m_b_w_out, m_f_w_in, m_f_conv_w, m_f_conv_b, m_f_w_out, m_ln1_g, m_ln1_b, m_ln2_g, m_ln2_b, v_meta, v_a_w_in, v_a_conv_w, v_a_conv_b, v_a_w_r, v_a_b_r, v_a_w_i, v_a_b_i, v_a_lambda, v_a_w_out, v_kv_w, v_kv_f_b, v_b_w_in, v_b_w_out, v_f_w_in, v_f_conv_w, v_f_conv_b, v_f_w_out, v_ln1_g, v_ln1_b, v_ln2_g, v_ln2_b):
    weights = dict(meta=meta, a_w_in=a_w_in, a_conv_w=a_conv_w, a_conv_b=a_conv_b, a_w_r=a_w_r, a_b_r=a_b_r,
                   a_w_i=a_w_i, a_b_i=a_b_i, a_lambda=a_lambda, a_w_out=a_w_out, kv_w=kv_w, kv_f_b=kv_f_b,
                   b_w_in=b_w_in, b_w_out=b_w_out, f_w_in=f_w_in, f_conv_w=f_conv_w, f_conv_b=f_conv_b,
                   f_w_out=f_w_out, ln1_g=ln1_g, ln1_b=ln1_b, ln2_g=ln2_g, ln2_b=ln2_b)
    mom_m = dict(meta=m_meta, a_w_in=m_a_w_in, a_conv_w=m_a_conv_w, a_conv_b=m_a_conv_b, a_w_r=m_a_w_r,
                 a_b_r=m_a_b_r, a_w_i=m_a_w_i, a_b_i=m_a_b_i, a_lambda=m_a_lambda, a_w_out=m_a_w_out,
                 kv_w=m_kv_w, kv_f_b=m_kv_f_b, b_w_in=m_b_w_in, b_w_out=m_b_w_out, f_w_in=m_f_w_in,
                 f_conv_w=m_f_conv_w, f_conv_b=m_f_conv_b, f_w_out=m_f_w_out, ln1_g=m_ln1_g, ln1_b=m_ln1_b,
                 ln2_g=m_ln2_g, ln2_b=m_ln2_b)
    mom_v = dict(meta=v_meta, a_w_in=v_a_w_in, a_conv_w=v_a_conv_w, a_conv_b=v_a_conv_b, a_w_r=v_a_w_r,
                 a_b_r=v_a_b_r, a_w_i=v_a_w_i, a_b_i=v_a_b_i, a_lambda=v_a_lambda, a_w_out=v_a_w_out,
                 kv_w=v_kv_w, kv_f_b=v_kv_f_b, b_w_in=v_b_w_in, b_w_out=v_b_w_out, f_w_in=v_f_w_in,
                 f_conv_w=v_f_conv_w, f_conv_b=v_f_conv_b, f_w_out=v_f_w_out, ln1_g=v_ln1_g, ln1_b=v_ln1_b,
                 ln2_g=v_ln2_g, ln2_b=v_ln2_b)
    return _train_step(x, loss_target, weights, mom_m, mom_v)


WEIGHT_ORDER = ("meta", "a_w_in", "a_conv_w", "a_conv_b", "a_w_r", "a_b_r", "a_w_i", "a_b_i", "a_lambda",
                "a_w_out", "kv_w", "kv_f_b", "b_w_in", "b_w_out", "f_w_in", "f_conv_w", "f_conv_b",
                "f_w_out", "ln1_g", "ln1_b", "ln2_g", "ln2_b")
BIG = ("a_w_in", "a_w_out", "kv_w", "b_w_in", "b_w_out", "f_w_in", "f_w_out")
OUT_TYPE = ("a_w_out", "b_w_out", "f_w_out")
SMALL_SHARDED = (("meta", 1), ("a_conv_w", 2), ("a_conv_b", 1), ("a_b_r", 1), ("a_b_i", 1), ("a_lambda", 1),
                 ("f_conv_w", 2))
SMALL_REPLICATED = ("a_w_r", "a_w_i", "kv_f_b", "f_conv_b", "ln1_g", "ln1_b", "ln2_g", "ln2_b")


def _train_step(x, loss_target, weights, mom_m, mom_v):
    S = N_SHARDS
    seq, d = x.shape[1], x.shape[2]
    nm = weights["meta"].shape[0]
    la = weights["a_w_in"].shape[0]
    lb = weights["b_w_in"].shape[0]
    depth = la + lb
    dr = weights["a_w_out"].shape[1] * S
    nb, bs = weights["a_w_r"].shape[1], weights["a_w_r"].shape[2]
    per_group = (LANES // math.gcd(bs, LANES))
    gs = per_group * bs
    heads = weights["kv_f_b"].shape[0]
    dff = weights["f_w_out"].shape[1] * S
    nkv = 2 * d + heads
    nkv_s = weights["kv_w"].shape[1]
    nkvp = _round_up(2 * d + LANES, 768) if 2 * d + LANES > 768 else 2 * d + LANES
    alpha = (2 * depth) ** 0.25
    xi, yi, ci = _coords()
    shard = 2 * xi + yi
    core_arr = jnp.reshape(ci, (1,)).astype(jnp.int32)
    shard_arr = jnp.reshape(shard, (1,)).astype(jnp.int32)
    shard_core_arr = jnp.stack([shard, ci]).astype(jnp.int32)

    def mixer_keys(l):
        if l < la:
            return [("a_w_in", l), ("a_w_out", l)]
        return ([("kv_w", 0)] if l == la else []) + [("b_w_in", l - la), ("b_w_out", l - la)]

    def ffn_keys(l):
        return [("f_w_in", l), ("f_w_out", l)]

    assert la >= 1
    groups = [mixer_keys(0)[:1], mixer_keys(0)[1:] + ffn_keys(0)]
    groups += [mixer_keys(l) + ffn_keys(l) for l in range(1, depth)]
    keys = [kl for grp in groups for kl in grp]
    local2d = {(k, i): (weights[k][i] if weights[k].ndim == 3 else weights[k]) for k, i in keys}
    small_local = [weights[k] for k, _ in SMALL_SHARDED]
    sm_flat = _pack(small_local, 2 * SUBLANES * LANES).reshape(1, 2, -1, LANES)
    sm_slot = lax.dynamic_update_slice_in_dim(lax.empty((S,) + sm_flat.shape[1:], F32), sm_flat, shard, axis=0)
    parts = [[cast_into_slot(shard_arr, local2d[kl], f"cast_{kl[0]}{kl[1]}") for kl in grp] for grp in groups]
    parts[0].append(sm_slot)
    in_flight, start_token = _split_start("gather_start", parts, lambda g, refs, cnt: _gather_ici_plan(refs, cnt))
    gw = {}

    def fetch(g, after):
        arrs = _split_wait(f"gather_wait_{g}", in_flight[g], after, _gather_ici_plan)
        arrs = forward_to_sibling(arrs, f"gather_fwd_{g}")
        for kl, a in zip(groups[g], arrs):
            rows, cols = local2d[kl].shape
            gw[kl] = a.reshape(S * rows, cols) if kl[0] in OUT_TYPE else a.reshape(S, rows, cols)
        return arrs

    sm_all = fetch(0, start_token)[-1].reshape(S, -1)
    small_full = {}
    per_shard = [_unpack(sm_all[s], [a.shape for a in small_local]) for s in range(S)]
    for idx, (k, axis) in enumerate(SMALL_SHARDED):
        small_full[k] = jnp.concatenate([per_shard[s][idx] for s in range(S)], axis=axis)
    fb_pad = jnp.pad(weights["kv_f_b"], (0, LANES - heads)).reshape(1, LANES)
    wr_g = [_block_diag(weights["a_w_r"][l], per_group) for l in range(la)]
    wi_g = [_block_diag(weights["a_w_i"][l], per_group) for l in range(la)]
    row = lambda v: v.reshape(1, -1)

    h, hb = embed_fwd(small_full["meta"], x[0], "embed")
    saved = []
    kvz = ct_pad = None
    _, tp, _ = _attn_geometry(nm + seq)
    t = nm + seq
    for l in range(depth):
        sv = {"hb_in": hb}
        if l > 0:
            fetch(l + 1, hb)
        if l < la:
            gr = mm_in(hb, gw[("a_w_in", l)], F32, f"a{l}_in")
            rc, rcb = a_conv_fwd(gr, small_full["a_conv_w"][l], row(small_full["a_conv_b"][l]), f"a{l}_conv")
            r_pre, i_pre = mm_bd(rcb, wr_g[l], wi_g[l], f"a{l}_gates")
            hs, gb = a_elem_fwd(gr, rc, r_pre, i_pre, row(small_full["a_b_r"][l]), row(small_full["a_b_i"][l]),
                                row(small_full["a_lambda"][l]), f"a{l}_lru")
            if l == 0:
                fetch(1, gb)
            mixed, w_mix = gb, gw[("a_w_out", l)]
            sv.update(gr=gr, rc=rc, rcb=rcb, r_pre=r_pre, i_pre=i_pre, hs=hs, gb=gb)
        else:
            j = l - la
            if j == 0:
                kv_cat = jnp.moveaxis(gw[("kv_w", 0)], 0, 1).reshape(d, S * nkv_s)
                kv_pad = jnp.pad(kv_cat, ((0, 0), (0, nkvp - nkv))).reshape(1, d, nkvp)
                kvz = mm_in(hb, kv_pad, F32, "kv_proj")
                cum = kv_fwd(kvz, fb_pad, d, "kv_forget")
                ct_pad = jnp.pad(cum[:, :heads].T, ((0, 0), (0, tp - t))).reshape(heads, 1, tp)
                kv_hb = hb
            qg = mm_in(hb, gw[("b_w_in", j)], F32, f"b{j}_in")
            o, mob = attn_fwd(qg, kvz, ct_pad, d, heads, f"b{j}_attn")
            mixed, w_mix = mob, gw[("b_w_out", j)]
            sv.update(qg=qg, o=o, mob=mob)
        h1, h1b, xh1, rs1 = mm_out_ln(mixed, w_mix, h, row(weights["ln1_g"][l]), row(weights["ln1_b"][l]), alpha,
                                      f"mix{l}_out_ln1")
        zf = mm_in(h1b, gw[("f_w_in", l)], F32, f"f{l}_in")
        ffb = f_elem_fwd(zf, small_full["f_conv_w"][l], row(weights["f_conv_b"][l]), f"f{l}_act")
        h2, h2b, xh2, rs2 = mm_out_ln(ffb, gw[("f_w_out", l)], h1, row(weights["ln2_g"][l]), row(weights["ln2_b"][l]),
                                      alpha, f"f{l}_out_ln2")
        sv.update(h1b=h1b, xh1=xh1, rs1=rs1, zf=zf, ffb=ffb, xh2=xh2, rs2=rs2)
        saved.append(sv)
        h, hb = h2, h2b
    loss11, dy = loss_fwd_bwd(h, loss_target[0], nm, "loss")

    grads = {}

    def by_owner(kl, g3):
        rows, cols = local2d[kl].shape
        grads[kl] = g3.reshape(S, 2, rows // 2, cols)

    reducing = []

    def send_grads(g, names, arrays):
        from_sib = swap_halves(arrays, f"grad_swap_{g}")
        csums = [add_halves(core_arr, a, o, f"chip_sum_{g}_{i}") for i, (a, o) in enumerate(zip(arrays, from_sib))]
        lands = [lax.empty((3,) + cs.shape[1:], cs.dtype) for cs in csums]
        started, token = _split_start(f"grad_a2a_start_{g}", [csums + lands],
                                      lambda _, refs, cnt: _all_to_all_plan(refs, cnt))
        reducing.append((names, started[0]))
        return token[0:1, 0:1]

    def after_start(vec, zero):
        return vec if zero is None else vec + zero

    pin = None

    g_small = {}
    per_layer = {k: [None] * n for k, n in (
        ("a_conv_w", la), ("a_conv_b", la), ("a_w_r", la), ("a_b_r", la), ("a_w_i", la), ("a_b_i", la),
        ("a_lambda", la), ("f_conv_w", depth), ("f_conv_b", depth), ("ln1_g", depth), ("ln1_b", depth),
        ("ln2_g", depth), ("ln2_b", depth))}
    adds = [(dy, 1.0)]
    dks, dvs, dcs = [], [], []
    for l in reversed(range(depth)):
        sv = saved[l]
        ds2, ds2b, dg2, db2 = ln_bwd(adds, sv["xh2"], sv["rs2"], after_start(row(weights["ln2_g"][l]), pin),
                                     f"ln2_{l}_bwd")
        pin = None
        per_layer["ln2_g"][l], per_layer["ln2_b"][l] = dg2[0], db2[0]
        dff_v = mm_out_nt(ds2b, gw[("f_w_out", l)], f"f{l}_out_dx")
        by_owner(("f_w_out", l), mm_tn(sv["ffb"], [ds2b], 1, f"f{l}_out_dw"))
        dzg, dzv, dwg, dwv, dbg, dbv = f_elem_bwd(sv["zf"], dff_v, small_full["f_conv_w"][l],
                                                  row(weights["f_conv_b"][l]), f"f{l}_act_bwd")
        per_layer["f_conv_w"][l] = jnp.concatenate([dwg, dwv], axis=1)
        per_layer["f_conv_b"][l] = jnp.concatenate([dbg, dbv], axis=1)[0]
        dh1_f = mm_in_nt([dzg, dzv], gw[("f_w_in", l)], f"f{l}_in_dx")
        by_owner(("f_w_in", l), mm_tn(sv["h1b"], [dzg, dzv], S, f"f{l}_in_dw"))
        ds1, ds1b, dg1, db1 = ln_bwd([(ds2, alpha), (dh1_f, 1.0)], sv["xh1"], sv["rs1"],
                                     row(weights["ln1_g"][l]), f"ln1_{l}_bwd")
        per_layer["ln1_g"][l], per_layer["ln1_b"][l] = dg1[0], db1[0]
        if l < la:
            dgv = mm_out_nt(ds1b, gw[("a_w_out", l)], f"a{l}_out_dx")
            by_owner(("a_w_out", l), mm_tn(sv["gb"], [ds1b], 1, f"a{l}_out_dw"))
            if l == 0:
                pin = send_grads(1, groups[1], [grads[kl] for kl in groups[1]])
            dgate_b, drp_b, dip_b, drc_d, dlam, dbr, dbi = a_elem_bwd(
                dgv, sv["gr"], sv["rc"], sv["r_pre"], sv["i_pre"], sv["hs"], row(small_full["a_b_r"][l]),
                row(small_full["a_b_i"][l]), after_start(row(small_full["a_lambda"][l]), pin), f"a{l}_lru_bwd")
            pin = None
            drc_g = mm_bd_nt(drp_b, dip_b, wr_g[l], wi_g[l], f"a{l}_gates_dx")
            dwr_g, dwi_g = mm_bd_tn(sv["rcb"], drp_b, dip_b, gs, f"a{l}_gates_dw")
            dgr_b, dcw, dcb = a_conv_bwd(drc_d, drc_g, sv["gr"], small_full["a_conv_w"][l], dgate_b,
                                         f"a{l}_conv_bwd")
            per_layer["a_w_r"][l] = _block_diag_extract(dwr_g, per_group, bs)
            per_layer["a_w_i"][l] = _block_diag_extract(dwi_g, per_group, bs)
            per_layer["a_lambda"][l], per_layer["a_b_r"][l], per_layer["a_b_i"][l] = dlam[0], dbr[0], dbi[0]
            per_layer["a_conv_w"][l], per_layer["a_conv_b"][l] = dcw, dcb[0]
            dh_m = mm_in_nt([dgr_b], gw[("a_w_in", l)], f"a{l}_in_dx")
            by_owner(("a_w_in", l), mm_tn(sv["hb_in"], [dgr_b], S, f"a{l}_in_dw"))
        else:
            j = l - la
            dmo = mm_out_nt(ds1b, gw[("b_w_out", j)], f"b{j}_out_dx")
            by_owner(("b_w_out", j), mm_tn(sv["mob"], [ds1b], 1, f"b{j}_out_dw"))
            dq_b, dog_b, dk, dv, dct = attn_bwd(dmo, sv["qg"], kvz, sv["o"], ct_pad, d, heads, f"b{j}_attn_bwd")
            dks.append(dk)
            dvs.append(dv)
            dcs.append(jnp.pad(dct[:, 0, :t].T, ((0, 0), (0, LANES - heads))))
            dh_m = mm_in_nt([dq_b, dog_b], gw[("b_w_in", j)], f"b{j}_in_dx")
            by_owner(("b_w_in", j), mm_tn(sv["hb_in"], [dq_b, dog_b], S, f"b{j}_in_dw"))
        adds = [(ds1, alpha), (dh_m, 1.0)]
        if l == la:
            dzf_b, dfb = kv_bwd(dcs, kvz, fb_pad, d, "kv_forget_bwd")
            dk_b = add_cast(dks[0], dks[1], "kv_dk") if lb == 2 else None
            dv_b = add_cast(dvs[0], dvs[1], "kv_dv") if lb == 2 else None
            dz_kv = jnp.concatenate([dk_b, dv_b, dzf_b, jnp.zeros((t, nkvp - 2 * d - LANES), BF16)], axis=1)
            dh_kv = mm_in_nt([dz_kv], kv_pad, "kv_proj_dx")
            kv_dw = mm_tn(kv_hb, [dz_kv], 1, "kv_proj_dw")
            by_owner(("kv_w", 0), jnp.moveaxis(kv_dw[0, :, :nkv].reshape(d, S, nkv_s), 1, 0))
            g_small["kv_f_b"] = dfb[0, :heads]
            adds.append((dh_kv, 1.0))
        if l > 0:
            pin = send_grads(l + 1, groups[l + 1], [grads[kl] for kl in groups[l + 1]])
    g_meta, g_x = embed_bwd(adds, nm, "embed_bwd")

    small_names = list(SMALL_REPLICATED) + [k for k, _ in SMALL_SHARDED]
    g_small["meta"] = g_meta
    for k, vals in per_layer.items():
        g_small[k] = jnp.stack(vals)
    small_shapes = {k: (weights[k].shape if k in SMALL_REPLICATED else g_small[k].shape) for k in small_names}
    sm_g = _pack([g_small[k].reshape(small_shapes[k]) for k in small_names], S * 2 * SUBLANES * LANES)
    sm_g = sm_g.reshape(S, 2, -1, LANES)
    send_grads(0, groups[0] + [("small", 0)], [grads[kl] for kl in groups[0]] + [sm_g])

    fin = {"small": lax.empty(sm_g.shape, F32)}
    for kl in keys:
        n_stack = weights[kl[0]].shape[0] if weights[kl[0]].ndim == 3 else 1
        rows, cols = local2d[kl].shape
        fin.setdefault(kl[0], lax.empty((n_stack, 2, rows // 2, cols), F32))
    for g, (names_g, started) in enumerate(reducing):
        arrs = _split_wait(f"grad_a2a_wait_{g}", started, g_x, _all_to_all_plan)
        half = len(names_g)
        for i, (kl, cs, rv) in enumerate(zip(names_g, arrs[:half], arrs[half:])):
            fin[kl[0]] = add_four(shard_core_arr, cs, rv, fin[kl[0]], None if kl[0] == "small" else kl[1],
                                  f"owner_sum_{g}_{i}")
    names = list(BIG) + ["small"]
    joined = dict(zip(names, join_halves([fin[k] for k in BIG], fin["small"], "grad_join")))
    sm_red = joined["small"].reshape(-1)

    out_g, out_d, out_m, out_v = {}, {}, {}, {}
    for k in BIG:
        w2 = weights[k].reshape(-1, weights[k].shape[-1])
        g2 = joined[k].reshape(w2.shape)
        dlt, mn, vn = adamw(w2, g2, mom_m[k].reshape(w2.shape), mom_v[k].reshape(w2.shape), "adamw_" + k)
        shp = weights[k].shape
        out_g[k], out_d[k], out_m[k], out_v[k] = g2.reshape(shp), dlt.reshape(shp), mn.reshape(shp), vn.reshape(shp)
    sm_vals = dict(zip(small_names, _unpack(sm_red, [small_shapes[k] for k in small_names])))
    local_small = {}
    for k in SMALL_REPLICATED:
        local_small[k] = sm_vals[k]
    for k, axis in SMALL_SHARDED:
        size = weights[k].shape[axis]
        local_small[k] = lax.dynamic_slice_in_dim(sm_vals[k], shard * size, size, axis=axis)
    for k in small_names:
        shp = weights[k].shape
        two_d = (-1, shp[-1]) if len(shp) > 1 else (1, -1)
        dlt, mn, vn = adamw(weights[k].reshape(two_d), local_small[k].reshape(two_d), mom_m[k].reshape(two_d),
                            mom_v[k].reshape(two_d), "adamw_" + k)
        out_g[k], out_d[k], out_m[k], out_v[k] = local_small[k], dlt.reshape(shp), mn.reshape(shp), vn.reshape(shp)

    loss = lax.psum(loss11[0, 0], ("x", "y", "c"))
    return (loss, g_x[None], *[out_g[k] for k in WEIGHT_ORDER], *[out_d[k] for k in WEIGHT_ORDER],
            *[out_m[k] for k in WEIGHT_ORDER], *[out_v[k] for k in WEIGHT_ORDER])
```

```python
import functools
import math

import jax
import jax.numpy as jnp
from jax import lax
from jax.experimental import pallas as pl
from jax.experimental.pallas import tpu as pltpu

F32 = jnp.float32
BF16 = jnp.bfloat16

LRU_C = 8.0
LN_EPS = 1e-5
ADAM_LR = 0.001
ADAM_B1 = 0.9
ADAM_B2 = 0.999
ADAM_EPS = 1e-08
ADAM_WD = 0.01
ADAM_STEP = 10

LANES = 128
SUBLANES = 8
V7X_VMEM_BYTES = 64 * 1024 * 1024
VMEM_LIMIT = V7X_VMEM_BYTES * 7 // 8
N_SHARDS = 4
GELU_C0 = math.sqrt(2.0 / math.pi)
GELU_C1 = 0.044715
NEG_BIG = -1e30


def _cp(*sem):
    return pltpu.CompilerParams(dimension_semantics=tuple(sem), vmem_limit_bytes=VMEM_LIMIT)


def _tile(n, cap, mult=LANES):
    best = None
    d = mult
    while d <= min(n, cap):
        if n % d == 0:
            best = d
        d += mult
    return n if best is None else best


def _row_block(t):
    if t % 3 == 0 and (t // 3) % 16 == 0:
        return t // 3
    return t


def _round_up(n, m):
    return (n + m - 1) // m * m


def _sigmoid(v):
    return 1.0 / (1.0 + jnp.exp(-v))


def _softplus(v):
    return jnp.maximum(v, 0.0) + jnp.log(1.0 + jnp.exp(-jnp.abs(v)))


def _gelu_parts(v):
    v2 = v * v
    u = GELU_C0 * (v + GELU_C1 * v * v2)
    t = jnp.tanh(u)
    g = 0.5 * v * (1.0 + t)
    dg = 0.5 * (1.0 + t) + 0.5 * v * (1.0 - t * t) * (GELU_C0 * (1.0 + 3.0 * GELU_C1 * v2))
    return g, dg


def _gelu(v):
    u = GELU_C0 * (v + GELU_C1 * v * v * v)
    return 0.5 * v * (1.0 + jnp.tanh(u))


def _neg_expm1(v):
    series = -v * (1.0 + 0.5 * v * (1.0 + (v / 3.0) * (1.0 + 0.25 * v)))
    return jnp.where(v > -0.05, series, 1.0 - jnp.exp(v))


def _shift_down(v, j):
    if j == 0:
        return v
    rows = lax.broadcasted_iota(jnp.int32, v.shape, 0)
    return jnp.where(rows >= j, pltpu.roll(v, j, 0), 0.0)


def _shift_up(v, j):
    if j == 0:
        return v
    n = v.shape[0]
    rows = lax.broadcasted_iota(jnp.int32, v.shape, 0)
    return jnp.where(rows < n - j, pltpu.roll(v, n - j, 0), 0.0)


def _scan_rows(a_ref, b_ref, out_ref, n_rows, width, reverse):
    n_groups = n_rows // SUBLANES
    rows = lax.broadcasted_iota(jnp.int32, (SUBLANES, width), 0)
    edge = 0 if reverse else SUBLANES - 1

    def body(g, carry):
        grp = (n_groups - 1 - g) if reverse else g
        off = pl.multiple_of(grp * SUBLANES, SUBLANES)
        b = b_ref[pl.ds(off, SUBLANES), :]
        a = None if a_ref is None else a_ref[pl.ds(off, SUBLANES), :]
        for d in (1, 2, 4):
            if reverse:
                keep = rows < SUBLANES - d
                sh = SUBLANES - d
            else:
                keep = rows >= d
                sh = d
            b_s = jnp.where(keep, pltpu.roll(b, sh, 0), 0.0)
            if a is None:
                b = b + b_s
            else:
                a_s = jnp.where(keep, pltpu.roll(a, sh, 0), 1.0)
                b = a * b_s + b
                a = a * a_s
        h = b + carry if a is None else b + a * carry
        out_ref[pl.ds(off, SUBLANES), :] = h
        return jnp.sum(jnp.where(rows == edge, h, 0.0), axis=0, keepdims=True)

    lax.fori_loop(0, n_groups, body, jnp.zeros((1, width), F32), unroll=2)


def mm_in(x, w, out_dtype, name):
    t, k = x.shape
    s_n, _, ns = w.shape
    tn = _tile(ns, 1408)
    nj = ns // tn
    rb = _row_block(t)

    def body(x_ref, w_ref, o_ref):
        o_ref[...] = jnp.dot(x_ref[...], w_ref[...], preferred_element_type=F32).astype(o_ref.dtype)

    return pl.pallas_call(
        body, name=name, grid=(s_n, nj, t // rb),
        in_specs=[pl.BlockSpec((rb, k), lambda s, j, r: (r, 0)),
                  pl.BlockSpec((None, k, tn), lambda s, j, r: (s, 0, j))],
        out_specs=pl.BlockSpec((rb, tn), lambda s, j, r: (r, s * nj + j)),
        out_shape=jax.ShapeDtypeStruct((t, s_n * ns), out_dtype),
        compiler_params=_cp("parallel", "parallel", "parallel"))(x, w)


def _part_map(p, per_part, nj, lead):
    def index(*grid):
        s, j = grid[-2], grid[-1]
        mine = s // per_part == p
        col = jnp.where(mine, (s - p * per_part) * nj + j, 0)
        return (grid[0], col) if lead else (0, col)
    return index


def mm_in_nt(dy_parts, w, name):
    n_parts = len(dy_parts)
    t = dy_parts[0].shape[0]
    s_n, k, ns = w.shape
    per_part = s_n // n_parts
    tn = _tile(ns, 1408)
    nj = ns // tn
    rb = _row_block(t)

    def body(*refs):
        w_ref, o_ref = refs[n_parts:]

        @pl.when((pl.program_id(1) == 0) & (pl.program_id(2) == 0))
        def _():
            o_ref[...] = jnp.zeros_like(o_ref)

        for p in range(n_parts):
            @pl.when(pl.program_id(1) // per_part == p)
            def _():
                o_ref[...] += lax.dot_general(refs[p][...], w_ref[...], (((1,), (1,)), ((), ())),
                                              preferred_element_type=F32)

    return pl.pallas_call(
        body, name=name, grid=(t // rb, s_n, nj),
        in_specs=[pl.BlockSpec((rb, tn), _part_map(p, per_part, nj, True)) for p in range(n_parts)]
        + [pl.BlockSpec((None, k, tn), lambda r, s, j: (s, 0, j))],
        out_specs=pl.BlockSpec((rb, k), lambda r, s, j: (r, 0)),
        out_shape=jax.ShapeDtypeStruct((t, k), F32),
        compiler_params=_cp("parallel", "arbitrary", "arbitrary"))(*dy_parts, w)


def mm_out_nt(dy, w, name):
    t, n = dy.shape
    k = w.shape[0]
    rb = _row_block(t)

    def body(dy_ref, w_ref, o_ref):
        o_ref[...] = lax.dot_general(dy_ref[...], w_ref[...], (((1,), (1,)), ((), ())),
                                     preferred_element_type=F32)

    return pl.pallas_call(
        body, name=name, grid=(t // rb,),
        in_specs=[pl.BlockSpec((rb, n), lambda r: (r, 0)), pl.BlockSpec((k, n), lambda r: (0, 0))],
        out_specs=pl.BlockSpec((rb, k), lambda r: (r, 0)),
        out_shape=jax.ShapeDtypeStruct((t, k), F32),
        compiler_params=_cp("parallel"))(dy, w)


def mm_tn(x, dy_parts, s_n, name):
    n_parts = len(dy_parts)
    t, kb = x.shape
    nb = dy_parts[0].shape[1] * n_parts // s_n
    per_part = max(s_n // n_parts, 1)
    tk = _tile(kb, 1408)
    tn = _tile(nb, 1408)
    nkb, nnb = kb // tk, nb // tn
    tn_dims = (((0,), (0,)), ((), ()))

    def body(*refs):
        x_ref, o_ref = refs[0], refs[-1]
        for p in range(n_parts):
            @pl.when(pl.program_id(1) // per_part == p)
            def _():
                o_ref[...] = lax.dot_general(x_ref[...], refs[1 + p][...], tn_dims,
                                             preferred_element_type=F32).astype(o_ref.dtype)

    return pl.pallas_call(
        body, name=name, grid=(nkb, s_n, nnb),
        in_specs=[pl.BlockSpec((t, tk), lambda a, s, b: (0, a))]
        + [pl.BlockSpec((t, tn), _part_map(p, per_part, nnb, False)) for p in range(n_parts)],
        out_specs=pl.BlockSpec((None, tk, tn), lambda a, s, b: (s, a, b)),
        out_shape=jax.ShapeDtypeStruct((s_n, kb, nb), BF16),
        compiler_params=_cp("parallel", "parallel", "parallel"))(x, *dy_parts)


def mm_bd(x, wr, wi, name):
    t, _ = x.shape
    g_n, gs, _ = wr.shape
    rb = _row_block(t)

    def body(x_ref, wr_ref, wi_ref, r_ref, i_ref):
        xv = x_ref[...]
        r_ref[...] = jnp.dot(xv, wr_ref[...], preferred_element_type=F32)
        i_ref[...] = jnp.dot(xv, wi_ref[...], preferred_element_type=F32)

    blk = pl.BlockSpec((rb, gs), lambda g, r: (r, g))
    wspec = pl.BlockSpec((None, gs, gs), lambda g, r: (g, 0, 0))
    return pl.pallas_call(
        body, name=name, grid=(g_n, t // rb), in_specs=[blk, wspec, wspec], out_specs=[blk, blk],
        out_shape=[jax.ShapeDtypeStruct((t, g_n * gs), F32)] * 2,
        compiler_params=_cp("parallel", "parallel"))(x, wr, wi)


def mm_bd_nt(dr, di, wr, wi, name):
    t, _ = dr.shape
    g_n, gs, _ = wr.shape
    rb = _row_block(t)
    nt = (((1,), (1,)), ((), ()))

    def body(dr_ref, di_ref, wr_ref, wi_ref, o_ref):
        o_ref[...] = (lax.dot_general(dr_ref[...], wr_ref[...], nt, preferred_element_type=F32)
                      + lax.dot_general(di_ref[...], wi_ref[...], nt, preferred_element_type=F32))

    blk = pl.BlockSpec((rb, gs), lambda g, r: (r, g))
    wspec = pl.BlockSpec((None, gs, gs), lambda g, r: (g, 0, 0))
    return pl.pallas_call(
        body, name=name, grid=(g_n, t // rb), in_specs=[blk, blk, wspec, wspec], out_specs=blk,
        out_shape=jax.ShapeDtypeStruct((t, g_n * gs), F32),
        compiler_params=_cp("parallel", "parallel"))(dr, di, wr, wi)


def mm_bd_tn(x, dr, di, gs, name):
    t, w = x.shape
    g_n = w // gs
    tn_dims = (((0,), (0,)), ((), ()))

    def body(x_ref, dr_ref, di_ref, gr_ref, gi_ref):
        xv = x_ref[...]
        gr_ref[...] = lax.dot_general(xv, dr_ref[...], tn_dims, preferred_element_type=F32)
        gi_ref[...] = lax.dot_general(xv, di_ref[...], tn_dims, preferred_element_type=F32)

    blk = pl.BlockSpec((t, gs), lambda g: (0, g))
    ospec = pl.BlockSpec((None, gs, gs), lambda g: (g, 0, 0))
    return pl.pallas_call(
        body, name=name, grid=(g_n,), in_specs=[blk, blk, blk], out_specs=[ospec, ospec],
        out_shape=[jax.ShapeDtypeStruct((g_n, gs, gs), F32)] * 2,
        compiler_params=_cp("parallel"))(x, dr, di)


def embed_fwd(meta, x2d, name):
    nm, d = meta.shape
    seq = x2d.shape[0]
    t = nm + seq
    cb = _tile(d, 256)

    def body(m_ref, x_ref, h_ref, hb_ref):
        h_ref[pl.ds(0, nm), :] = m_ref[...]
        h_ref[pl.ds(nm, seq), :] = x_ref[...]
        hb_ref[pl.ds(0, nm), :] = m_ref[...].astype(BF16)
        hb_ref[pl.ds(nm, seq), :] = x_ref[...].astype(BF16)

    return pl.pallas_call(
        body, name=name, grid=(d // cb,),
        in_specs=[pl.BlockSpec((nm, cb), lambda j: (0, j)), pl.BlockSpec((seq, cb), lambda j: (0, j))],
        out_specs=[pl.BlockSpec((t, cb), lambda j: (0, j))] * 2,
        out_shape=[jax.ShapeDtypeStruct((t, d), F32), jax.ShapeDtypeStruct((t, d), BF16)],
        compiler_params=_cp("parallel"))(meta, x2d)


def embed_bwd(adds, nm, name):
    t, d = adds[0][0].shape
    seq = t - nm
    cb = _tile(d, 256)
    scales = [s for _, s in adds]
    n = len(adds)

    def body(*refs):
        tot = None
        for r, sc in zip(refs[:n], scales):
            term = r[...] if sc == 1.0 else sc * r[...]
            tot = term if tot is None else tot + term
        gm_ref, gx_ref = refs[n], refs[n + 1]
        gm_ref[...] = tot[0:nm]
        gx_ref[...] = tot[nm:t]

    return pl.pallas_call(
        body, name=name, grid=(d // cb,),
        in_specs=[pl.BlockSpec((t, cb), lambda j: (0, j))] * n,
        out_specs=[pl.BlockSpec((nm, cb), lambda j: (0, j)), pl.BlockSpec((seq, cb), lambda j: (0, j))],
        out_shape=[jax.ShapeDtypeStruct((nm, d), F32), jax.ShapeDtypeStruct((seq, d), F32)],
        compiler_params=_cp("parallel"))(*[a for a, _ in adds])


def loss_fwd_bwd(h, tgt, nm, name):
    t, d = h.shape
    seq = t - nm
    cb = _tile(d, 256)
    inv_d = 1.0 / d

    def body(h_ref, t_ref, loss_ref, dy_ref):
        @pl.when(pl.program_id(0) == 0)
        def _():
            loss_ref[...] = jnp.zeros_like(loss_ref)
        err = h_ref[pl.ds(nm, seq), :] - t_ref[...]
        dy_ref[pl.ds(0, nm), :] = jnp.zeros((nm, cb), F32)
        dy_ref[pl.ds(nm, seq), :] = err * inv_d
        loss_ref[...] += (0.5 * inv_d) * jnp.sum(err * err, keepdims=True)

    return pl.pallas_call(
        body, name=name, grid=(d // cb,),
        in_specs=[pl.BlockSpec((t, cb), lambda j: (0, j)), pl.BlockSpec((seq, cb), lambda j: (0, j))],
        out_specs=[pl.BlockSpec((1, 1), lambda j: (0, 0)), pl.BlockSpec((t, cb), lambda j: (0, j))],
        out_shape=[jax.ShapeDtypeStruct((1, 1), F32), jax.ShapeDtypeStruct((t, d), F32)],
        compiler_params=_cp("arbitrary"))(h, tgt)


def mm_out_ln(x, w, h, g, b, alpha, name):
    t, d = h.shape
    k = x.shape[1]
    rb = _row_block(t)

    def body(x_ref, w_ref, h_ref, g_ref, b_ref, y_ref, yb_ref, xh_ref, rs_ref):
        s = alpha * h_ref[...] + jnp.dot(x_ref[...], w_ref[...], preferred_element_type=F32)
        mu = jnp.mean(s, axis=-1, keepdims=True)
        c = s - mu
        var = jnp.mean(c * c, axis=-1, keepdims=True)
        rstd = lax.rsqrt(var + LN_EPS)
        xh = c * rstd
        y = xh * g_ref[...] + b_ref[...]
        y_ref[...] = y
        yb_ref[...] = y.astype(BF16)
        xh_ref[...] = xh
        rs_ref[...] = rstd

    row = pl.BlockSpec((rb, d), lambda r: (r, 0))
    vec = pl.BlockSpec((1, d), lambda r: (0, 0))
    return pl.pallas_call(
        body, name=name, grid=(t // rb,),
        in_specs=[pl.BlockSpec((rb, k), lambda r: (r, 0)), pl.BlockSpec((k, d), lambda r: (0, 0)), row, vec, vec],
        out_specs=[row, row, row, pl.BlockSpec((rb, 1), lambda r: (r, 0))],
        out_shape=[jax.ShapeDtypeStruct((t, d), F32), jax.ShapeDtypeStruct((t, d), BF16),
                   jax.ShapeDtypeStruct((t, d), F32), jax.ShapeDtypeStruct((t, 1), F32)],
        compiler_params=_cp("parallel"))(x, w, h, g, b)


def ln_bwd(adds, xhat, rstd, g, name):
    t, d = xhat.shape
    rb = _row_block(t)
    scales = [s for _, s in adds]
    n = len(adds)

    def body(*refs):
        xh_ref, rs_ref, g_ref = refs[n:n + 3]
        ds_ref, dsb_ref, dg_ref, db_ref = refs[n + 3:]
        dy = None
        for r, sc in zip(refs[:n], scales):
            term = r[...] if sc == 1.0 else sc * r[...]
            dy = term if dy is None else dy + term

        @pl.when(pl.program_id(0) == 0)
        def _():
            dg_ref[...] = jnp.zeros_like(dg_ref)
            db_ref[...] = jnp.zeros_like(db_ref)

        xh = xh_ref[...]
        dxh = dy * g_ref[...]
        m1 = jnp.mean(dxh, axis=-1, keepdims=True)
        m2 = jnp.mean(dxh * xh, axis=-1, keepdims=True)
        ds = rs_ref[...] * (dxh - m1 - xh * m2)
        ds_ref[...] = ds
        dsb_ref[...] = ds.astype(BF16)
        dg_ref[...] += jnp.sum(dy * xh, axis=0, keepdims=True)
        db_ref[...] += jnp.sum(dy, axis=0, keepdims=True)

    row = pl.BlockSpec((rb, d), lambda r: (r, 0))
    vec = pl.BlockSpec((1, d), lambda r: (0, 0))
    return pl.pallas_call(
        body, name=name, grid=(t // rb,),
        in_specs=[row] * n + [row, pl.BlockSpec((rb, 1), lambda r: (r, 0)), vec],
        out_specs=[row, row, vec, vec],
        out_shape=[jax.ShapeDtypeStruct((t, d), F32), jax.ShapeDtypeStruct((t, d), BF16),
                   jax.ShapeDtypeStruct((1, d), F32), jax.ShapeDtypeStruct((1, d), F32)],
        compiler_params=_cp("arbitrary"))(*[a for a, _ in adds], xhat, rstd, g)


def _conv_fwd_val(xv, w_ref, b_ref, width):
    y = b_ref[...]
    for j in range(width):
        y = y + _shift_down(xv, j) * w_ref[pl.ds(width - 1 - j, 1), :]
    return y


def _conv_bwd_val(dout, xv, w_ref, width):
    dx = None
    dws = [None] * width
    for j in range(width):
        k = width - 1 - j
        term = _shift_up(dout, j) * w_ref[pl.ds(k, 1), :]
        dx = term if dx is None else dx + term
        dws[k] = jnp.sum(dout * _shift_down(xv, j), axis=0, keepdims=True)
    return dx, dws, jnp.sum(dout, axis=0, keepdims=True)


def a_conv_fwd(gr, cw, cbias, name):
    t, two_dr = gr.shape
    dr = two_dr // 2
    width = cw.shape[0]
    cb = _tile(dr, 256)
    off = dr // cb

    def body(x_ref, w_ref, b_ref, rc_ref, rcb_ref):
        y = _conv_fwd_val(x_ref[...], w_ref, b_ref, width)
        rc_ref[...] = y
        rcb_ref[...] = y.astype(BF16)

    return pl.pallas_call(
        body, name=name, grid=(dr // cb,),
        in_specs=[pl.BlockSpec((t, cb), lambda j: (0, off + j)),
                  pl.BlockSpec((width, cb), lambda j: (0, j)), pl.BlockSpec((1, cb), lambda j: (0, j))],
        out_specs=[pl.BlockSpec((t, cb), lambda j: (0, j))] * 2,
        out_shape=[jax.ShapeDtypeStruct((t, dr), F32), jax.ShapeDtypeStruct((t, dr), BF16)],
        compiler_params=_cp("parallel"))(gr, cw, cbias)


def a_conv_bwd(drc_a, drc_b, gr, cw, dgr, name):
    t, two_dr = gr.shape
    dr = two_dr // 2
    width = cw.shape[0]
    cb = _tile(dr, 256)
    off = dr // cb

    def body(da_ref, db_ref, x_ref, w_ref, dgr_in, dx_ref, dw_ref, dbias_ref):
        del dgr_in
        dout = da_ref[...] + db_ref[...]
        dx, dws, dbias = _conv_bwd_val(dout, x_ref[...], w_ref, width)
        dx_ref[...] = dx.astype(BF16)
        for k in range(width):
            dw_ref[pl.ds(k, 1), :] = dws[k]
        dbias_ref[...] = dbias

    col = pl.BlockSpec((t, cb), lambda j: (0, j))
    return pl.pallas_call(
        body, name=name, grid=(dr // cb,),
        in_specs=[col, col, pl.BlockSpec((t, cb), lambda j: (0, off + j)),
                  pl.BlockSpec((width, cb), lambda j: (0, j)), pl.BlockSpec(memory_space=pl.ANY)],
        out_specs=[pl.BlockSpec((t, cb), lambda j: (0, off + j)), pl.BlockSpec((width, cb), lambda j: (0, j)),
                   pl.BlockSpec((1, cb), lambda j: (0, j))],
        out_shape=[jax.ShapeDtypeStruct((t, two_dr), BF16), jax.ShapeDtypeStruct((width, dr), F32),
                   jax.ShapeDtypeStruct((1, dr), F32)],
        input_output_aliases={4: 0},
        compiler_params=_cp("parallel"))(drc_a, drc_b, gr, cw, dgr)


def _lru_gates(r_pre, i_pre, br, bi, lam):
    r = _sigmoid(r_pre + br)
    i = _sigmoid(i_pre + bi)
    sp = _softplus(-lam)
    la = -LRU_C * r * sp
    a = jnp.exp(la)
    m = jnp.sqrt(_neg_expm1(2.0 * la))
    return r, i, sp, la, a, m


def a_elem_fwd(gr, rc, r_pre, i_pre, br, bi, lam, name):
    t, dr = rc.shape
    cb = _tile(dr, 2 * LANES)
    rb = _row_block(t)
    chunks = [pl.ds(r * rb, rb) for r in range(t // rb)]

    def body(gate_ref, rc_ref, rp_ref, ip_ref, br_ref, bi_ref, lam_ref, hs_ref, g_ref, a_s, u_s):
        for rows in chunks:
            _, i, _, _, a, m = _lru_gates(rp_ref[rows, :], ip_ref[rows, :], br_ref[...], bi_ref[...], lam_ref[...])
            a_s[rows, :] = a
            u_s[rows, :] = m * (i * rc_ref[rows, :])
        _scan_rows(a_s, u_s, hs_ref, t, cb, reverse=False)
        for rows in chunks:
            g_ref[rows, :] = (_gelu(gate_ref[rows, :]) * hs_ref[rows, :]).astype(BF16)

    col = pl.BlockSpec((t, cb), lambda j: (0, j))
    vec = pl.BlockSpec((1, cb), lambda j: (0, j))
    return pl.pallas_call(
        body, name=name, grid=(dr // cb,),
        in_specs=[col, col, col, col, vec, vec, vec],
        out_specs=[col, col],
        out_shape=[jax.ShapeDtypeStruct((t, dr), F32), jax.ShapeDtypeStruct((t, dr), BF16)],
        scratch_shapes=[pltpu.VMEM((t, cb), F32), pltpu.VMEM((t, cb), F32)],
        compiler_params=_cp("parallel"))(gr, rc, r_pre, i_pre, br, bi, lam)


def a_elem_bwd(dg, gr, rc, r_pre, i_pre, hs, br, bi, lam, name):
    t, dr = rc.shape
    cb = _tile(dr, 2 * LANES)
    rb = _row_block(t)
    chunks = [pl.ds(r * rb, rb) for r in range(t // rb)]

    def body(dg_ref, gate_ref, rc_ref, rp_ref, ip_ref, hs_ref, br_ref, bi_ref, lam_ref,
             dgate_ref, dr_ref, di_ref, drc_ref, dlam_ref, dbr_ref, dbi_ref, a_s, b_s, g_s, hp_s):
        lamv = lam_ref[...]
        gates = lambda rows: _lru_gates(rp_ref[rows, :], ip_ref[rows, :], br_ref[...], bi_ref[...], lamv)
        for rows in chunks:
            a_s[rows, :] = gates(rows)[4]
            ge, dge = _gelu_parts(gate_ref[rows, :])
            dgv = dg_ref[rows, :]
            dgate_ref[rows, :] = (dgv * hs_ref[rows, :] * dge).astype(BF16)
            b_s[rows, :] = dgv * ge
        a_s[...] = _shift_up(a_s[...], 1)
        hp_s[...] = _shift_down(hs_ref[...], 1)
        _scan_rows(a_s, b_s, g_s, t, cb, reverse=True)
        dsp = dbr = dbi = jnp.zeros((1, cb), F32)
        for rows in chunks:
            r, i, sp, _, a, m = gates(rows)
            rcv = rc_ref[rows, :]
            gsum = g_s[rows, :]
            da = gsum * hp_s[rows, :]
            dm = gsum * (i * rcv)
            d_i = gsum * m * rcv
            drc_ref[rows, :] = gsum * m * i
            dla = a * da - dm * (a * a) / m
            d_r = (-LRU_C) * sp * dla
            dsp = dsp + jnp.sum((-LRU_C) * r * dla, axis=0, keepdims=True)
            d_rp = d_r * r * (1.0 - r)
            d_ip = d_i * i * (1.0 - i)
            dr_ref[rows, :] = d_rp.astype(BF16)
            di_ref[rows, :] = d_ip.astype(BF16)
            dbr = dbr + jnp.sum(d_rp, axis=0, keepdims=True)
            dbi = dbi + jnp.sum(d_ip, axis=0, keepdims=True)
        dlam_ref[...] = -dsp * _sigmoid(-lamv)
        dbr_ref[...] = dbr
        dbi_ref[...] = dbi

    col = pl.BlockSpec((t, cb), lambda j: (0, j))
    vec = pl.BlockSpec((1, cb), lambda j: (0, j))
    big_b = jax.ShapeDtypeStruct((t, dr), BF16)
    vec_s = jax.ShapeDtypeStruct((1, dr), F32)
    return pl.pallas_call(
        body, name=name, grid=(dr // cb,),
        in_specs=[col, col, col, col, col, col, vec, vec, vec],
        out_specs=[col, col, col, col, vec, vec, vec],
        out_shape=[jax.ShapeDtypeStruct((t, 2 * dr), BF16), big_b, big_b, jax.ShapeDtypeStruct((t, dr), F32),
                   vec_s, vec_s, vec_s],
        scratch_shapes=[pltpu.VMEM((t, cb), F32)] * 4,
        compiler_params=_cp("parallel"))(dg, gr, rc, r_pre, i_pre, hs, br, bi, lam)


def f_elem_fwd(z, cw, cbias, name):
    t, two_f = z.shape
    dff = two_f // 2
    width = cw.shape[0]
    cb = _tile(dff, 256)
    off = dff // cb

    def body(zg_ref, zv_ref, wg_ref, wv_ref, bg_ref, bv_ref, o_ref):
        zcg = _conv_fwd_val(zg_ref[...], wg_ref, bg_ref, width)
        zcv = _conv_fwd_val(zv_ref[...], wv_ref, bv_ref, width)
        o_ref[...] = (_gelu(zcg) * zcv).astype(BF16)

    lo = lambda j: (0, j)
    hi = lambda j: (0, off + j)
    return pl.pallas_call(
        body, name=name, grid=(dff // cb,),
        in_specs=[pl.BlockSpec((t, cb), lo), pl.BlockSpec((t, cb), hi),
                  pl.BlockSpec((width, cb), lo), pl.BlockSpec((width, cb), hi),
                  pl.BlockSpec((1, cb), lo), pl.BlockSpec((1, cb), hi)],
        out_specs=pl.BlockSpec((t, cb), lo),
        out_shape=jax.ShapeDtypeStruct((t, dff), BF16),
        compiler_params=_cp("parallel"))(z, z, cw, cw, cbias, cbias)


def f_elem_bwd(z, dff_g, cw, cbias, name):
    t, two_f = z.shape
    dff = two_f // 2
    width = cw.shape[0]
    cb = _tile(dff, 256)
    off = dff // cb

    def body(zg_ref, zv_ref, d_ref, wg_ref, wv_ref, bg_ref, bv_ref,
             dzg_ref, dzv_ref, dwg_ref, dwv_ref, dbg_ref, dbv_ref):
        zg = zg_ref[...]
        zv = zv_ref[...]
        zcg = _conv_fwd_val(zg, wg_ref, bg_ref, width)
        zcv = _conv_fwd_val(zv, wv_ref, bv_ref, width)
        ge, dge = _gelu_parts(zcg)
        dv = d_ref[...]
        dx, dws, dbias = _conv_bwd_val(dv * zcv * dge, zg, wg_ref, width)
        dzg_ref[...] = dx.astype(BF16)
        for k in range(width):
            dwg_ref[pl.ds(k, 1), :] = dws[k]
        dbg_ref[...] = dbias
        dx, dws, dbias = _conv_bwd_val(dv * ge, zv, wv_ref, width)
        dzv_ref[...] = dx.astype(BF16)
        for k in range(width):
            dwv_ref[pl.ds(k, 1), :] = dws[k]
        dbv_ref[...] = dbias

    lo = lambda j: (0, j)
    hi = lambda j: (0, off + j)
    col = pl.BlockSpec((t, cb), lo)
    wsp = pl.BlockSpec((width, cb), lo)
    vsp = pl.BlockSpec((1, cb), lo)
    return pl.pallas_call(
        body, name=name, grid=(dff // cb,),
        in_specs=[col, pl.BlockSpec((t, cb), hi), col, wsp, pl.BlockSpec((width, cb), hi),
                  vsp, pl.BlockSpec((1, cb), hi)],
        out_specs=[col, col, wsp, wsp, vsp, vsp],
        out_shape=[jax.ShapeDtypeStruct((t, dff), BF16)] * 2
        + [jax.ShapeDtypeStruct((width, dff), F32)] * 2 + [jax.ShapeDtypeStruct((1, dff), F32)] * 2,
        compiler_params=_cp("parallel"))(z, z, dff_g, cw, cw, cbias, cbias)


def kv_fwd(z, fb, d_model, name):
    t, _ = z.shape
    blk = 2 * d_model // LANES

    def body(z_ref, fb_ref, c_ref, lf_s):
        v = z_ref[...] + fb_ref[...]
        lf_s[...] = -_softplus(-v)
        _scan_rows(None, lf_s, c_ref, t, LANES, reverse=False)

    return pl.pallas_call(
        body, name=name, grid=(1,),
        in_specs=[pl.BlockSpec((t, LANES), lambda j: (0, blk)), pl.BlockSpec((1, LANES), lambda j: (0, 0))],
        out_specs=pl.BlockSpec((t, LANES), lambda j: (0, 0)),
        out_shape=jax.ShapeDtypeStruct((t, LANES), F32),
        scratch_shapes=[pltpu.VMEM((t, LANES), F32)],
        compiler_params=_cp("arbitrary"))(z, fb)


def kv_bwd(dcs, z, fb, d_model, name):
    t, _ = z.shape
    blk = 2 * d_model // LANES
    n = len(dcs)

    def body(*refs):
        z_ref, fb_ref, dz_ref, dfb_ref, dc_s, dl_s = refs[n:]
        tot = refs[0][...]
        for r in refs[1:n]:
            tot = tot + r[...]
        dc_s[...] = tot
        _scan_rows(None, dc_s, dl_s, t, LANES, reverse=True)
        v = z_ref[...] + fb_ref[...]
        dz = dl_s[...] * _sigmoid(-v)
        dz_ref[...] = dz.astype(BF16)
        dfb_ref[...] = jnp.sum(dz, axis=0, keepdims=True)

    full = pl.BlockSpec((t, LANES), lambda j: (0, 0))
    return pl.pallas_call(
        body, name=name, grid=(1,),
        in_specs=[full] * n + [pl.BlockSpec((t, LANES), lambda j: (0, blk)),
                               pl.BlockSpec((1, LANES), lambda j: (0, 0))],
        out_specs=[full, pl.BlockSpec((1, LANES), lambda j: (0, 0))],
        out_shape=[jax.ShapeDtypeStruct((t, LANES), BF16), jax.ShapeDtypeStruct((1, LANES), F32)],
        scratch_shapes=[pltpu.VMEM((t, LANES), F32)] * 2,
        compiler_params=_cp("arbitrary"))(*dcs, z, fb)


def add_cast(a, b, name):
    t, d = a.shape
    cb = _tile(d, 512)

    def body(a_ref, b_ref, o_ref):
        o_ref[...] = (a_ref[...] + b_ref[...]).astype(BF16)

    col = pl.BlockSpec((t, cb), lambda j: (0, j))
    return pl.pallas_call(body, name=name, grid=(d // cb,), in_specs=[col, col], out_specs=col,
                          out_shape=jax.ShapeDtypeStruct((t, d), BF16),
                          compiler_params=_cp("parallel"))(a, b)


def _attn_geometry(t):
    nqb = 6 if t > 1024 else 2
    tp = _round_up(t, LANES * nqb)
    return nqb, tp, tp // nqb


def _attn_scales(dh):
    scale = dh ** -0.5
    if math.log2(scale).is_integer():
        return scale, 1.0
    return 1.0, scale


def _attn_pieces(qs, ks, crow, j, i, tq, dh, s_mul):
    r0 = i * tq
    lanes = pl.ds(j * dh, dh)
    qi = qs[pl.ds(r0, tq), lanes]
    spans = ([(0, r0)] if i > 0 else []) + [(r0, tq)]
    logits = []
    for k0, n in spans:
        s = lax.dot_general(qi, ks[pl.ds(k0, n), lanes], (((1,), (1,)), ((), ())),
                            preferred_element_type=F32)
        if s_mul != 1.0:
            s = s * s_mul
        s = s - crow[:, k0:k0 + n]
        if k0 == r0:
            rows = lax.broadcasted_iota(jnp.int32, (tq, tq), 0)
            cols = lax.broadcasted_iota(jnp.int32, (tq, tq), 1)
            s = jnp.where(cols <= rows, s, NEG_BIG)
        logits.append(s)
    mx = jnp.max(logits[0], axis=1, keepdims=True)
    for s in logits[1:]:
        mx = jnp.maximum(mx, jnp.max(s, axis=1, keepdims=True))
    es = [jnp.exp(s - mx) for s in logits]
    tot = jnp.sum(es[0], axis=1, keepdims=True)
    for e in es[1:]:
        tot = tot + jnp.sum(e, axis=1, keepdims=True)
    inv = 1.0 / tot
    return [(k0, n, e * inv) for (k0, n), e in zip(spans, es)], qi


def attn_fwd(qg, z, ct_pad, d_model, n_heads, name):
    t = qg.shape[0]
    dh = d_model // n_heads
    hp = LANES // dh
    nqb, tp, tq = _attn_geometry(t)
    nblk = d_model // LANES
    q_mul, s_mul = _attn_scales(dh)

    def body(q_ref, og_ref, k_ref, v_ref, ct_ref, o_ref, mo_ref, qs, ks, vs, os_):
        pad = jnp.zeros((tp - t, LANES), BF16)
        qs[pl.ds(0, t), :] = (q_ref[...] * q_mul).astype(BF16)
        qs[pl.ds(t, tp - t), :] = pad
        for src, dst in ((k_ref, ks), (v_ref, vs)):
            dst[pl.ds(0, t), :] = src[...].astype(BF16)
            dst[pl.ds(t, tp - t), :] = pad
        for j in range(hp):
            crow = ct_ref[j]
            lanes = pl.ds(j * dh, dh)
            for i in range(nqb):
                pieces, _ = _attn_pieces(qs, ks, crow, j, i, tq, dh, s_mul)
                acc = None
                for k0, n, p in pieces:
                    part = jnp.dot(p.astype(BF16), vs[pl.ds(k0, n), lanes], preferred_element_type=F32)
                    acc = part if acc is None else acc + part
                os_[pl.ds(i * tq, tq), lanes] = acc
        o = os_[pl.ds(0, t), :]
        o_ref[...] = o
        mo_ref[...] = (o * _sigmoid(og_ref[...])).astype(BF16)

    col = lambda off: pl.BlockSpec((t, LANES), lambda p: (0, off + p))
    return pl.pallas_call(
        body, name=name, grid=(nblk,),
        in_specs=[col(0), col(nblk), col(0), col(nblk), pl.BlockSpec((hp, 1, tp), lambda p: (p, 0, 0))],
        out_specs=[col(0), col(0)],
        out_shape=[jax.ShapeDtypeStruct((t, d_model), F32), jax.ShapeDtypeStruct((t, d_model), BF16)],
        scratch_shapes=[pltpu.VMEM((tp, LANES), BF16)] * 3 + [pltpu.VMEM((tp, LANES), F32)],
        compiler_params=_cp("parallel"))(qg, qg, z, z, ct_pad)


def attn_bwd(dmo, qg, z, o, ct_pad, d_model, n_heads, name):
    t = qg.shape[0]
    dh = d_model // n_heads
    hp = LANES // dh
    nqb, tp, tq = _attn_geometry(t)
    nblk = d_model // LANES
    q_mul, s_mul = _attn_scales(dh)
    scale = dh ** -0.5
    tn_dims = (((0,), (0,)), ((), ()))
    nt_dims = (((1,), (1,)), ((), ()))

    def body(dmo_ref, q_ref, og_ref, k_ref, v_ref, o_ref, ct_ref,
             dq_ref, dog_ref, dk_ref, dv_ref, dct_ref, qs, ks, vs, dos, dqs, dks, dvs):
        pad = jnp.zeros((tp - t, LANES), BF16)
        sg = _sigmoid(og_ref[...])
        dmo_v = dmo_ref[...]
        dog_ref[...] = (dmo_v * o_ref[...] * sg * (1.0 - sg)).astype(BF16)
        dos[pl.ds(0, t), :] = (dmo_v * sg).astype(BF16)
        dos[pl.ds(t, tp - t), :] = pad
        qs[pl.ds(0, t), :] = (q_ref[...] * q_mul).astype(BF16)
        qs[pl.ds(t, tp - t), :] = pad
        for src, dst in ((k_ref, ks), (v_ref, vs)):
            dst[pl.ds(0, t), :] = src[...].astype(BF16)
            dst[pl.ds(t, tp - t), :] = pad
        dks[...] = jnp.zeros_like(dks)
        dvs[...] = jnp.zeros_like(dvs)
        dct_ref[...] = jnp.zeros_like(dct_ref)
        for j in range(hp):
            crow = ct_ref[j]
            lanes = pl.ds(j * dh, dh)
            for i in range(nqb):
                pieces, qi = _attn_pieces(qs, ks, crow, j, i, tq, dh, s_mul)
                do_i = dos[pl.ds(i * tq, tq), lanes]
                dps = [lax.dot_general(do_i, vs[pl.ds(k0, n), lanes], nt_dims, preferred_element_type=F32)
                       for k0, n, _ in pieces]
                row = None
                for (_, _, p), dp in zip(pieces, dps):
                    part = jnp.sum(p * dp, axis=1, keepdims=True)
                    row = part if row is None else row + part
                dq_i = None
                for (k0, n, p), dp in zip(pieces, dps):
                    ds = p * (dp - row)
                    ds_b = ds.astype(BF16)
                    keys = pl.ds(k0, n)
                    part = jnp.dot(ds_b, ks[keys, lanes], preferred_element_type=F32)
                    dq_i = part if dq_i is None else dq_i + part
                    dks[keys, lanes] += lax.dot_general(ds_b, qi, tn_dims, preferred_element_type=F32) * s_mul
                    dvs[keys, lanes] += lax.dot_general(p.astype(BF16), do_i, tn_dims,
                                                        preferred_element_type=F32)
                    dct_ref[j, :, keys] -= jnp.sum(ds, axis=0, keepdims=True)
                dqs[pl.ds(i * tq, tq), lanes] = dq_i * scale
        dq_ref[...] = dqs[pl.ds(0, t), :].astype(BF16)
        dk_ref[...] = dks[pl.ds(0, t), :]
        dv_ref[...] = dvs[pl.ds(0, t), :]

    col = lambda off: pl.BlockSpec((t, LANES), lambda p: (0, off + p))
    big = lambda dt: jax.ShapeDtypeStruct((t, d_model), dt)
    return pl.pallas_call(
        body, name=name, grid=(nblk,),
        in_specs=[col(0), col(0), col(nblk), col(0), col(nblk), col(0),
                  pl.BlockSpec((hp, 1, tp), lambda p: (p, 0, 0))],
        out_specs=[col(0), col(0), col(0), col(0), pl.BlockSpec((hp, 1, tp), lambda p: (p, 0, 0))],
        out_shape=[big(BF16), big(BF16), big(F32), big(F32),
                   jax.ShapeDtypeStruct((n_heads, 1, tp), F32)],
        scratch_shapes=[pltpu.VMEM((tp, LANES), BF16)] * 4 + [pltpu.VMEM((tp, LANES), F32)] * 3,
        compiler_params=_cp("parallel"))(dmo, qg, qg, z, z, o, ct_pad)


def cast_into_slot(shard, w, index, name):
    r, c = w.shape[-2:]
    rh = r // 2
    tr = _tile(rh, 512, 16)
    n = rh // tr
    if w.ndim == 3:
        w_spec = pl.BlockSpec((None, tr, c), lambda h, i, sh: (index, h * n + i, 0))
    else:
        w_spec = pl.BlockSpec((tr, c), lambda h, i, sh: (h * n + i, 0))

    def body(sh_ref, w_ref, o_ref):
        del sh_ref
        o_ref[...] = w_ref[...].astype(BF16)

    return pl.pallas_call(
        body, name=name,
        grid_spec=pltpu.PrefetchScalarGridSpec(
            num_scalar_prefetch=1, grid=(2, n), in_specs=[w_spec],
            out_specs=pl.BlockSpec((None, None, tr, c), lambda h, i, sh: (sh[0], h, i, 0))),
        out_shape=jax.ShapeDtypeStruct((N_SHARDS, 2, rh, c), BF16),
        compiler_params=_cp("parallel", "parallel"))(shard, w)


def add_halves(core, g, other, name):
    s_n, _, rh, c = g.shape
    tr = _tile(rh, 512, 16)

    def body(core_ref, g_ref, o_ref, out_ref):
        del core_ref
        out_ref[...] = (g_ref[...].astype(F32) + o_ref[...].astype(F32)).astype(out_ref.dtype)

    return pl.pallas_call(
        body, name=name,
        grid_spec=pltpu.PrefetchScalarGridSpec(
            num_scalar_prefetch=1, grid=(s_n, rh // tr),
            in_specs=[pl.BlockSpec((None, None, tr, c), lambda s, i, cr: (s, cr[0], i, 0)),
                      pl.BlockSpec((None, tr, c), lambda s, i, cr: (s, i, 0))],
            out_specs=pl.BlockSpec((None, tr, c), lambda s, i, cr: (s, i, 0))),
        out_shape=jax.ShapeDtypeStruct((s_n, rh, c), g.dtype),
        compiler_params=_cp("parallel", "parallel"))(core, g, other)


def add_four(shard_core, csum, recv, buf, layer, name):
    _, rh, c = csum.shape
    tr = _tile(rh, 512, 16)
    slot = (lambda sc: sc[0]) if layer is None else (lambda sc: layer)

    def body(sc_ref, a_ref, r_ref, buf_ref, out_ref):
        del sc_ref, buf_ref
        acc = a_ref[...].astype(F32)
        for k in range(3):
            acc = acc + r_ref[k].astype(F32)
        out_ref[...] = acc

    return pl.pallas_call(
        body, name=name,
        grid_spec=pltpu.PrefetchScalarGridSpec(
            num_scalar_prefetch=1, grid=(rh // tr,),
            in_specs=[pl.BlockSpec((None, tr, c), lambda i, sc: (sc[0], i, 0)),
                      pl.BlockSpec((3, tr, c), lambda i, sc: (0, i, 0)),
                      pl.BlockSpec(memory_space=pl.ANY)],
            out_specs=pl.BlockSpec((None, None, tr, c), lambda i, sc: (slot(sc), sc[1], i, 0))),
        out_shape=jax.ShapeDtypeStruct(buf.shape, F32),
        input_output_aliases={3: 0},
        compiler_params=_cp("parallel"))(shard_core, csum, recv, buf)


def adamw(w, g, m, v, name, emit_g=False):
    r, c = w.shape
    tr = _tile(r, 512, SUBLANES)
    c1 = 1.0 - ADAM_B1 ** ADAM_STEP
    c2 = 1.0 - ADAM_B2 ** ADAM_STEP
    n_out = 4 if emit_g else 3

    def body(w_ref, g_ref, m_ref, v_ref, d_ref, mo_ref, vo_ref, *go_ref):
        gv = g_ref[...]
        if emit_g:
            go_ref[0][...] = gv
        mn = ADAM_B1 * m_ref[...] + (1.0 - ADAM_B1) * gv
        vn = ADAM_B2 * v_ref[...] + (1.0 - ADAM_B2) * (gv * gv)
        m_hat = mn / c1
        v_hat = vn / c2
        d_ref[...] = -ADAM_LR * (m_hat / (jnp.sqrt(v_hat) + ADAM_EPS) + ADAM_WD * w_ref[...])
        mo_ref[...] = mn
        vo_ref[...] = vn

    blk = pl.BlockSpec((tr, c), lambda i: (i, 0))
    return pl.pallas_call(
        body, name=name, grid=(r // tr,), in_specs=[blk] * 4, out_specs=[blk] * n_out,
        out_shape=[jax.ShapeDtypeStruct((r, c), F32)] * n_out,
        compiler_params=_cp("parallel"))(w, g, m, v)


def _coords():
    return lax.axis_index("x"), lax.axis_index("y"), lax.axis_index("c")


def _exchange(name, ins, out_shapes, plan, in_place=False):
    n_in = len(ins)
    n_out = len(out_shapes)
    n_rem = len(plan([None] * n_in, [None] * n_out, True))

    def body(*refs):
        in_refs = refs[:n_in]
        out_refs = refs[n_in:n_in + n_out]
        send_sems, recv_sems = refs[n_in + n_out:]
        copies = [pltpu.make_async_remote_copy(
            src_ref=src, dst_ref=dst, send_sem=send_sems.at[q], recv_sem=recv_sems.at[q],
            device_id=peer, device_id_type=pl.DeviceIdType.MESH)
            for q, (src, dst, peer) in enumerate(plan(list(in_refs), list(out_refs), False))]
        for cp in copies:
            cp.start()
        for cp in copies:
            cp.wait_recv()
        for cp in copies:
            cp.wait_send()

    hbm = pl.BlockSpec(memory_space=pl.ANY)
    return pl.pallas_call(
        body, name=name, in_specs=[hbm] * n_in, out_specs=[hbm] * n_out, out_shape=out_shapes,
        input_output_aliases={i: i for i in range(n_in)} if in_place else {},
        scratch_shapes=[pltpu.SemaphoreType.DMA((n_rem,)), pltpu.SemaphoreType.DMA((n_rem,))],
        compiler_params=pltpu.CompilerParams(has_side_effects=True))(*ins)


def _split_start(name, groups, plan):
    flat = [a for grp in groups for a in grp]
    n, n_grp = len(flat), len(groups)
    counts = [len(plan(g, [None] * len(grp), True)) for g, grp in enumerate(groups)]

    def body(*refs):
        ins, sems, token = refs[:n], refs[n:n + 2 * n_grp], refs[-1]
        pos = 0
        for g, grp in enumerate(groups):
            arrs = list(ins[pos:pos + len(grp)])
            pos += len(grp)
            for q, (src, dst, peer) in enumerate(plan(g, arrs, False)):
                pltpu.make_async_remote_copy(
                    src_ref=src, dst_ref=dst, send_sem=sems[2 * g].at[q], recv_sem=sems[2 * g + 1].at[q],
                    device_id=peer, device_id_type=pl.DeviceIdType.MESH).start()
        token[...] = jnp.zeros_like(token)

    hbm = pl.BlockSpec(memory_space=pltpu.HBM)
    sem = pl.BlockSpec(memory_space=pltpu.SEMAPHORE)
    outs = pl.pallas_call(
        body, name=name,
        out_shape=[pltpu.SemaphoreType.DMA((cnt,)) for cnt in counts for _ in range(2)]
        + [pltpu.HBM(a.shape, a.dtype) for a in flat] + [jax.ShapeDtypeStruct((SUBLANES, LANES), F32)],
        in_specs=[hbm] * n, out_specs=[sem] * (2 * n_grp) + [hbm] * n + [pl.BlockSpec(memory_space=pltpu.VMEM)],
        input_output_aliases={i: 2 * n_grp + i for i in range(n)},
        compiler_params=pltpu.CompilerParams(has_side_effects=pltpu.SideEffectType.DATAFLOW_SIDE_EFFECTING),
    )(*[pltpu.with_memory_space_constraint(a, pltpu.HBM) for a in flat])
    started, pos = [], 2 * n_grp
    for g, grp in enumerate(groups):
        started.append((outs[2 * g], outs[2 * g + 1], list(outs[pos:pos + len(grp)])))
        pos += len(grp)
    return started, outs[-1]


def _split_wait(name, started, after, plan_g):
    send_sems, recv_sems, arrs = started
    n = len(arrs)

    def body(*refs):
        ins, ssem, rsem = list(refs[:n]), refs[n], refs[n + 1]
        for q, (src, dst, peer) in enumerate(plan_g(ins, False)):
            cp = pltpu.make_async_remote_copy(
                src_ref=src, dst_ref=dst, send_sem=ssem.at[q], recv_sem=rsem.at[q],
                device_id=peer, device_id_type=pl.DeviceIdType.MESH)
            cp.wait_send()
            cp.wait_recv()

    hbm = pl.BlockSpec(memory_space=pltpu.HBM)
    sem = pl.BlockSpec(memory_space=pltpu.SEMAPHORE)
    return pl.pallas_call(
        body, name=name, out_shape=[pltpu.HBM(a.shape, a.dtype) for a in arrs],
        in_specs=[hbm] * n + [sem, sem, pl.BlockSpec(memory_space=pl.ANY)], out_specs=[hbm] * n,
        input_output_aliases={i: i for i in range(n)},
        compiler_params=pltpu.CompilerParams(has_side_effects=pltpu.SideEffectType.DATAFLOW_SIDE_EFFECTING),
    )(*arrs, send_sems, recv_sems, after)


def _gather_ici_plan(arrs, count_only):
    if count_only:
        return [None] * (3 * len(arrs))
    x, y, c = _coords()
    pushes = []
    for a in arrs:
        mine = a.at[2 * x + y, c]
        pushes += [(mine, mine, peer) for peer, _ in _other_chips(x, y, c)]
    return pushes


def _all_to_all_plan(arrs, count_only):
    half = len(arrs) // 2
    if count_only:
        return [None] * (3 * half)
    x, y, c = _coords()
    pushes = []
    for src, land in zip(arrs[:half], arrs[half:]):
        pushes += [(src.at[shard], land.at[k], peer) for k, (peer, shard) in enumerate(_other_chips(x, y, c))]
    return pushes


def forward_to_sibling(bufs, name):
    n = len(bufs)

    def plan(ins, outs, count_only):
        if count_only:
            return [None] * (3 * n)
        x, y, c = _coords()
        pushes = []
        for i in range(n):
            for _, src_shard in _other_chips(x, y, c):
                slab = outs[i].at[src_shard, c]
                pushes.append((slab, slab, (x, y, 1 - c)))
        return pushes

    shapes = [jax.ShapeDtypeStruct(b.shape, b.dtype) for b in bufs]
    return _exchange(name, bufs, shapes, plan, in_place=True)


def _other_chips(x, y, c):
    return [((1 - x, y, c), 2 * (1 - x) + y), ((x, 1 - y, c), 2 * x + 1 - y),
            ((1 - x, 1 - y, c), 2 * (1 - x) + 1 - y)]


def swap_halves(grads, name):
    n = len(grads)
    shapes = [jax.ShapeDtypeStruct((g.shape[0],) + g.shape[2:], g.dtype) for g in grads]

    def plan(ins, outs, count_only):
        if count_only:
            return [None] * (n * N_SHARDS)
        x, y, c = _coords()
        remote = []
        for i in range(n):
            for s in range(N_SHARDS):
                remote.append((ins[i].at[s, 1 - c], outs[i].at[s], (x, y, 1 - c)))
        return remote

    return _exchange(name, grads, shapes, plan)


def join_halves(bufs, everywhere, name):
    slots = [(i, l) for i, b in enumerate(bufs) for l in range(b.shape[0])]
    n = len(bufs)

    def plan(ins, outs, count_only):
        if count_only:
            return [None] * (len(slots) + 7)
        x, y, c = _coords()
        pushes = [(outs[i].at[l, c], outs[i].at[l, c], (x, y, 1 - c)) for i, l in slots]
        mine = outs[n].at[2 * x + y, c]
        for flips in range(1, 8):
            peer = (1 - x if flips & 4 else x, 1 - y if flips & 2 else y, 1 - c if flips & 1 else c)
            pushes.append((mine, mine, peer))
        return pushes

    arrs = list(bufs) + [everywhere]
    shapes = [jax.ShapeDtypeStruct(b.shape, b.dtype) for b in arrs]
    return _exchange(name, arrs, shapes, plan, in_place=True)


def _pack(arrays, multiple):
    flat = jnp.concatenate([a.reshape(-1) for a in arrays])
    n = flat.shape[0]
    return jnp.pad(flat, (0, _round_up(n, multiple) - n))


def _unpack(flat, shapes):
    out, pos = [], 0
    for shp in shapes:
        n = math.prod(shp)
        out.append(flat[pos:pos + n].reshape(shp))
        pos += n
    return out


def _block_diag(w, per_group):
    nb, bs, _ = w.shape
    g = nb // per_group
    w4 = w.reshape(g, per_group, bs, bs)
    eye = jnp.eye(per_group, dtype=w.dtype)
    full = w4[:, :, :, None, :] * eye[None, :, None, :, None]
    return full.reshape(g, per_group * bs, per_group * bs).astype(BF16)


def _block_diag_extract(full, per_group, bs):
    g = full.shape[0]
    f5 = full.reshape(g, per_group, bs, per_group, bs)
    idx = jnp.arange(per_group)
    picked = f5[:, idx, :, idx, :]
    return jnp.moveaxis(picked, 0, 1).reshape(g * per_group, bs, bs)


def kernel(x, meta, a_w_in, a_conv_w, a_conv_b, a_w_r, a_b_r, a_w_i, a_b_i, a_lambda, a_w_out, kv_w, kv_f_b, b_w_in, b_w_out, f_w_in, f_conv_w, f_conv_b, f_w_out, ln1_g, ln1_b, ln2_g, ln2_b, loss_target, m_meta, m_a_w_in, m_a_conv_w, m_a_conv_b, m_a_w_r, m_a_b_r, m_a_w_i, m_a_b_i, m_a_lambda, m_a_w_out, m_kv_w, m_kv_f_b, m_b_w_in, m_b_w_out, m_f_w_in, m_f_conv_w, m_f_conv_b, m_f_w_out, m_ln1_g, m_ln1_b, m_ln2_g, m_ln2_b, v_meta, v_a_w_in, v_a_conv_w, v_a_conv_b, v_a_w_r, v_a_b_r, v_a_w_i, v_a_b_i, v_a_lambda, v_a_w_out, v_kv_w, v_kv_f_b, v_b_w_in, v_b_w_out, v_f_w_in, v_f_conv_w, v_f_conv_b, v_f_w_out, v_ln1_g, v_ln1_b, v_ln2_g, v_ln2_b):
    weights = dict(meta=meta, a_w_in=a_w_in, a_conv_w=a_conv_w, a_conv_b=a_conv_b, a_w_r=a_w_r, a_b_r=a_b_r,
                   a_w_i=a_w_i, a_b_i=a_b_i, a_lambda=a_lambda, a_w_out=a_w_out, kv_w=kv_w, kv_f_b=kv_f_b,
                   b_w_in=b_w_in, b_w_out=b_w_out, f_w_in=f_w_in, f_conv_w=f_conv_w, f_conv_b=f_conv_b,
                   f_w_out=f_w_out, ln1_g=ln1_g, ln1_b=ln1_b, ln2_g=ln2_g, ln2_b=ln2_b)
    mom_m = dict(meta=m_meta, a_w_in=m_a_w_in, a_conv_w=m_a_conv_w, a_conv_b=m_a_conv_b, a_w_r=m_a_w_r,
                 a_b_r=m_a_b_r, a_w_i=m_a_w_i, a_b_i=m_a_b_i, a_lambda=m_a_lambda, a_w_out=m_a_w_out,
                 kv_w=m_kv_w, kv_f_b=m_kv_f_b, b_w_in=m_b_w_in, b_w_out=m_b_w_out, f_w_in=m_f_w_in,
                 f_conv_w=m_f_conv_w, f_conv_b=m_f_conv_b, f_w_out=m_f_w_out, ln1_g=m_ln1_g, ln1_b=m_ln1_b,
                 ln2_g=m_ln2_g, ln2_b=m_ln2_b)
    mom_v = dict(meta=v_meta, a_w_in=v_a_w_in, a_conv_w=v_a_conv_w, a_conv_b=v_a_conv_b, a_w_r=v_a_w_r,
                 a_b_r=v_a_b_r, a_w_i=v_a_w_i, a_b_i=v_a_b_i, a_lambda=v_a_lambda, a_w_out=v_a_w_out,
                 kv_w=v_kv_w, kv_f_b=v_kv_f_b, b_w_in=v_b_w_in, b_w_out=v_b_w_out, f_w_in=v_f_w_in,
                 f_conv_w=v_f_conv_w, f_conv_b=v_f_conv_b, f_w_out=v_f_w_out, ln1_g=v_ln1_g, ln1_b=v_ln1_b,
                 ln2_g=v_ln2_g, ln2_b=v_ln2_b)
    return _train_step(x, loss_target, weights, mom_m, mom_v)


WEIGHT_ORDER = ("meta", "a_w_in", "a_conv_w", "a_conv_b", "a_w_r", "a_b_r", "a_w_i", "a_b_i", "a_lambda",
                "a_w_out", "kv_w", "kv_f_b", "b_w_in", "b_w_out", "f_w_in", "f_conv_w", "f_conv_b",
                "f_w_out", "ln1_g", "ln1_b", "ln2_g", "ln2_b")
BIG = ("a_w_in", "a_w_out", "kv_w", "b_w_in", "b_w_out", "f_w_in", "f_w_out")
OUT_TYPE = ("a_w_out", "b_w_out", "f_w_out")
SMALL_SHARDED = (("meta", 1), ("a_conv_w", 2), ("a_conv_b", 1), ("a_b_r", 1), ("a_b_i", 1), ("a_lambda", 1),
                 ("f_conv_w", 2))
SMALL_REPLICATED = ("a_w_r", "a_w_i", "kv_f_b", "f_conv_b", "ln1_g", "ln1_b", "ln2_g", "ln2_b")


def _train_step(x, loss_target, weights, mom_m, mom_v):
    S = N_SHARDS
    seq, d = x.shape[1], x.shape[2]
    nm = weights["meta"].shape[0]
    la = weights["a_w_in"].shape[0]
    lb = weights["b_w_in"].shape[0]
    depth = la + lb
    dr = weights["a_w_out"].shape[1] * S
    nb, bs = weights["a_w_r"].shape[1], weights["a_w_r"].shape[2]
    per_group = (LANES // math.gcd(bs, LANES))
    gs = per_group * bs
    heads = weights["kv_f_b"].shape[0]
    dff = weights["f_w_out"].shape[1] * S
    nkv = 2 * d + heads
    nkv_s = weights["kv_w"].shape[1]
    nkvp = _round_up(2 * d + LANES, 768) if 2 * d + LANES > 768 else 2 * d + LANES
    alpha = (2 * depth) ** 0.25
    xi, yi, ci = _coords()
    shard = 2 * xi + yi
    core_arr = jnp.reshape(ci, (1,)).astype(jnp.int32)
    shard_arr = jnp.reshape(shard, (1,)).astype(jnp.int32)
    shard_core_arr = jnp.stack([shard, ci]).astype(jnp.int32)

    def mixer_keys(l):
        if l < la:
            return [("a_w_in", l), ("a_w_out", l)]
        return ([("kv_w", 0)] if l == la else []) + [("b_w_in", l - la), ("b_w_out", l - la)]

    def ffn_keys(l):
        return [("f_w_in", l), ("f_w_out", l)]

    assert la >= 1
    groups = [mixer_keys(0)[:1], mixer_keys(0)[1:] + ffn_keys(0)]
    groups += [mixer_keys(l) + ffn_keys(l) for l in range(1, depth)]
    keys = [kl for grp in groups for kl in grp]
    shape2d = {(k, i): weights[k].shape[-2:] for k, i in keys}
    small_local = [weights[k] for k, _ in SMALL_SHARDED]
    sm_flat = _pack(small_local, 2 * SUBLANES * LANES).reshape(1, 2, -1, LANES)
    sm_slot = lax.dynamic_update_slice_in_dim(lax.empty((S,) + sm_flat.shape[1:], F32), sm_flat, shard, axis=0)
    parts = [[cast_into_slot(shard_arr, weights[k], i, f"cast_{k}{i}") for k, i in grp] for grp in groups]
    parts[0].append(sm_slot)
    in_flight, start_token = _split_start("gather_start", parts, lambda g, refs, cnt: _gather_ici_plan(refs, cnt))
    gw = {}

    def fetch(g, after):
        arrs = _split_wait(f"gather_wait_{g}", in_flight[g], after, _gather_ici_plan)
        arrs = forward_to_sibling(arrs, f"gather_fwd_{g}")
        for kl, a in zip(groups[g], arrs):
            rows, cols = shape2d[kl]
            gw[kl] = a.reshape(S * rows, cols) if kl[0] in OUT_TYPE else a.reshape(S, rows, cols)
        return arrs

    sm_all = fetch(0, start_token)[-1].reshape(S, -1)
    small_full = {}
    per_shard = [_unpack(sm_all[s], [a.shape for a in small_local]) for s in range(S)]
    for idx, (k, axis) in enumerate(SMALL_SHARDED):
        small_full[k] = jnp.concatenate([per_shard[s][idx] for s in range(S)], axis=axis)
    fb_pad = jnp.pad(weights["kv_f_b"], (0, LANES - heads)).reshape(1, LANES)
    wr_g = [_block_diag(weights["a_w_r"][l], per_group) for l in range(la)]
    wi_g = [_block_diag(weights["a_w_i"][l], per_group) for l in range(la)]
    row = lambda v: v.reshape(1, -1)

    h, hb = embed_fwd(small_full["meta"], x[0], "embed")
    saved = []
    kvz = ct_pad = None
    _, tp, _ = _attn_geometry(nm + seq)
    t = nm + seq
    for l in range(depth):
        sv = {"hb_in": hb}
        if l > 0:
            fetch(l + 1, hb)
        if l < la:
            gr = mm_in(hb, gw[("a_w_in", l)], F32, f"a{l}_in")
            rc, rcb = a_conv_fwd(gr, small_full["a_conv_w"][l], row(small_full["a_conv_b"][l]), f"a{l}_conv")
            r_pre, i_pre = mm_bd(rcb, wr_g[l], wi_g[l], f"a{l}_gates")
            hs, gb = a_elem_fwd(gr, rc, r_pre, i_pre, row(small_full["a_b_r"][l]), row(small_full["a_b_i"][l]),
                                row(small_full["a_lambda"][l]), f"a{l}_lru")
            if l == 0:
                fetch(1, gb)
            mixed, w_mix = gb, gw[("a_w_out", l)]
            sv.update(gr=gr, rc=rc, rcb=rcb, r_pre=r_pre, i_pre=i_pre, hs=hs, gb=gb)
        else:
            j = l - la
            if j == 0:
                kv_cat = jnp.moveaxis(gw[("kv_w", 0)], 0, 1).reshape(d, S * nkv_s)
                kv_pad = jnp.pad(kv_cat, ((0, 0), (0, nkvp - nkv))).reshape(1, d, nkvp)
                kvz = mm_in(hb, kv_pad, F32, "kv_proj")
                cum = kv_fwd(kvz, fb_pad, d, "kv_forget")
                ct_pad = jnp.pad(cum[:, :heads].T, ((0, 0), (0, tp - t))).reshape(heads, 1, tp)
                kv_hb = hb
            qg = mm_in(hb, gw[("b_w_in", j)], F32, f"b{j}_in")
            o, mob = attn_fwd(qg, kvz, ct_pad, d, heads, f"b{j}_attn")
            mixed, w_mix = mob, gw[("b_w_out", j)]
            sv.update(qg=qg, o=o, mob=mob)
        h1, h1b, xh1, rs1 = mm_out_ln(mixed, w_mix, h, row(weights["ln1_g"][l]), row(weights["ln1_b"][l]), alpha,
                                      f"mix{l}_out_ln1")
        zf = mm_in(h1b, gw[("f_w_in", l)], F32, f"f{l}_in")
        ffb = f_elem_fwd(zf, small_full["f_conv_w"][l], row(weights["f_conv_b"][l]), f"f{l}_act")
        h2, h2b, xh2, rs2 = mm_out_ln(ffb, gw[("f_w_out", l)], h1, row(weights["ln2_g"][l]), row(weights["ln2_b"][l]),
                                      alpha, f"f{l}_out_ln2")
        sv.update(h1b=h1b, xh1=xh1, rs1=rs1, zf=zf, ffb=ffb, xh2=xh2, rs2=rs2)
        saved.append(sv)
        h, hb = h2, h2b
    loss11, dy = loss_fwd_bwd(h, loss_target[0], nm, "loss")

    grads = {}

    def by_owner(kl, g3):
        rows, cols = shape2d[kl]
        grads[kl] = g3.reshape(S, 2, rows // 2, cols)

    reducing = []

    def send_grads(g, names, arrays):
        from_sib = swap_halves(arrays, f"grad_swap_{g}")
        csums = [add_halves(core_arr, a, o, f"chip_sum_{g}_{i}") for i, (a, o) in enumerate(zip(arrays, from_sib))]
        lands = [lax.empty((3,) + cs.shape[1:], cs.dtype) for cs in csums]
        started, token = _split_start(f"grad_a2a_start_{g}", [csums + lands],
                                      lambda _, refs, cnt: _all_to_all_plan(refs, cnt))
        reducing.append((names, started[0]))
        return token[0:1, 0:1]

    def after_start(vec, zero):
        return vec if zero is None else vec + zero

    pin = None

    g_small = {}
    per_layer = {k: [None] * n for k, n in (
        ("a_conv_w", la), ("a_conv_b", la), ("a_w_r", la), ("a_b_r", la), ("a_w_i", la), ("a_b_i", la),
        ("a_lambda", la), ("f_conv_w", depth), ("f_conv_b", depth), ("ln1_g", depth), ("ln1_b", depth),
        ("ln2_g", depth), ("ln2_b", depth))}
    adds = [(dy, 1.0)]
    dks, dvs, dcs = [], [], []
    for l in reversed(range(depth)):
        sv = saved[l]
        ds2, ds2b, dg2, db2 = ln_bwd(adds, sv["xh2"], sv["rs2"], after_start(row(weights["ln2_g"][l]), pin),
                                     f"ln2_{l}_bwd")
        pin = None
        per_layer["ln2_g"][l], per_layer["ln2_b"][l] = dg2[0], db2[0]
        dff_v = mm_out_nt(ds2b, gw[("f_w_out", l)], f"f{l}_out_dx")
        by_owner(("f_w_out", l), mm_tn(sv["ffb"], [ds2b], 1, f"f{l}_out_dw"))
        dzg, dzv, dwg, dwv, dbg, dbv = f_elem_bwd(sv["zf"], dff_v, small_full["f_conv_w"][l],
                                                  row(weights["f_conv_b"][l]), f"f{l}_act_bwd")
        per_layer["f_conv_w"][l] = jnp.concatenate([dwg, dwv], axis=1)
        per_layer["f_conv_b"][l] = jnp.concatenate([dbg, dbv], axis=1)[0]
        dh1_f = mm_in_nt([dzg, dzv], gw[("f_w_in", l)], f"f{l}_in_dx")
        by_owner(("f_w_in", l), mm_tn(sv["h1b"], [dzg, dzv], S, f"f{l}_in_dw"))
        ds1, ds1b, dg1, db1 = ln_bwd([(ds2, alpha), (dh1_f, 1.0)], sv["xh1"], sv["rs1"],
                                     row(weights["ln1_g"][l]), f"ln1_{l}_bwd")
        per_layer["ln1_g"][l], per_layer["ln1_b"][l] = dg1[0], db1[0]
        if l < la:
            dgv = mm_out_nt(ds1b, gw[("a_w_out", l)], f"a{l}_out_dx")
            by_owner(("a_w_out", l), mm_tn(sv["gb"], [ds1b], 1, f"a{l}_out_dw"))
            if l == 0:
                pin = send_grads(1, groups[1], [grads[kl] for kl in groups[1]])
            dgate_b, drp_b, dip_b, drc_d, dlam, dbr, dbi = a_elem_bwd(
                dgv, sv["gr"], sv["rc"], sv["r_pre"], sv["i_pre"], sv["hs"], row(small_full["a_b_r"][l]),
                row(small_full["a_b_i"][l]), after_start(row(small_full["a_lambda"][l]), pin), f"a{l}_lru_bwd")
            pin = None
            drc_g = mm_bd_nt(drp_b, dip_b, wr_g[l], wi_g[l], f"a{l}_gates_dx")
            dwr_g, dwi_g = mm_bd_tn(sv["rcb"], drp_b, dip_b, gs, f"a{l}_gates_dw")
            dgr_b, dcw, dcb = a_conv_bwd(drc_d, drc_g, sv["gr"], small_full["a_conv_w"][l], dgate_b,
                                         f"a{l}_conv_bwd")
            per_layer["a_w_r"][l] = _block_diag_extract(dwr_g, per_group, bs)
            per_layer["a_w_i"][l] = _block_diag_extract(dwi_g, per_group, bs)
            per_layer["a_lambda"][l], per_layer["a_b_r"][l], per_layer["a_b_i"][l] = dlam[0], dbr[0], dbi[0]
            per_layer["a_conv_w"][l], per_layer["a_conv_b"][l] = dcw, dcb[0]
            dh_m = mm_in_nt([dgr_b], gw[("a_w_in", l)], f"a{l}_in_dx")
            by_owner(("a_w_in", l), mm_tn(sv["hb_in"], [dgr_b], S, f"a{l}_in_dw"))
        else:
            j = l - la
            dmo = mm_out_nt(ds1b, gw[("b_w_out", j)], f"b{j}_out_dx")
            by_owner(("b_w_out", j), mm_tn(sv["mob"], [ds1b], 1, f"b{j}_out_dw"))
            dq_b, dog_b, dk, dv, dct = attn_bwd(dmo, sv["qg"], kvz, sv["o"], ct_pad, d, heads, f"b{j}_attn_bwd")
            dks.append(dk)
            dvs.append(dv)
            dcs.append(jnp.pad(dct[:, 0, :t].T, ((0, 0), (0, LANES - heads))))
            dh_m = mm_in_nt([dq_b, dog_b], gw[("b_w_in", j)], f"b{j}_in_dx")
            by_owner(("b_w_in", j), mm_tn(sv["hb_in"], [dq_b, dog_b], S, f"b{j}_in_dw"))
        adds = [(ds1, alpha), (dh_m, 1.0)]
        if l == la:
            dzf_b, dfb = kv_bwd(dcs, kvz, fb_pad, d, "kv_forget_bwd")
            dk_b = add_cast(dks[0], dks[1], "kv_dk") if lb == 2 else None
            dv_b = add_cast(dvs[0], dvs[1], "kv_dv") if lb == 2 else None
            dz_kv = jnp.concatenate([dk_b, dv_b, dzf_b, jnp.zeros((t, nkvp - 2 * d - LANES), BF16)], axis=1)
            dh_kv = mm_in_nt([dz_kv], kv_pad, "kv_proj_dx")
            kv_dw = mm_tn(kv_hb, [dz_kv], 1, "kv_proj_dw")
            by_owner(("kv_w", 0), jnp.moveaxis(kv_dw[0, :, :nkv].reshape(d, S, nkv_s), 1, 0))
            g_small["kv_f_b"] = dfb[0, :heads]
            adds.append((dh_kv, 1.0))
        if l > 0:
            pin = send_grads(l + 1, groups[l + 1], [grads[kl] for kl in groups[l + 1]])
    g_meta, g_x = embed_bwd(adds, nm, "embed_bwd")

    small_names = list(SMALL_REPLICATED) + [k for k, _ in SMALL_SHARDED]
    g_small["meta"] = g_meta
    for k, vals in per_layer.items():
        g_small[k] = jnp.stack(vals)
    small_shapes = {k: (weights[k].shape if k in SMALL_REPLICATED else g_small[k].shape) for k in small_names}
    sm_g = _pack([g_small[k].reshape(small_shapes[k]) for k in small_names], S * 2 * SUBLANES * LANES)
    sm_g = sm_g.reshape(S, 2, -1, LANES)
    send_grads(0, groups[0] + [("small", 0)], [grads[kl] for kl in groups[0]] + [sm_g])

    fin = {"small": lax.empty(sm_g.shape, F32)}
    for kl in keys:
        n_stack = weights[kl[0]].shape[0] if weights[kl[0]].ndim == 3 else 1
        rows, cols = shape2d[kl]
        fin.setdefault(kl[0], lax.empty((n_stack, 2, rows // 2, cols), F32))
    for g, (names_g, started) in enumerate(reducing):
        arrs = _split_wait(f"grad_a2a_wait_{g}", started, g_x, _all_to_all_plan)
        half = len(names_g)
        for i, (kl, cs, rv) in enumerate(zip(names_g, arrs[:half], arrs[half:])):
            fin[kl[0]] = add_four(shard_core_arr, cs, rv, fin[kl[0]], None if kl[0] == "small" else kl[1],
                                  f"owner_sum_{g}_{i}")
    names = list(BIG) + ["small"]
    joined = dict(zip(names, join_halves([fin[k] for k in BIG], fin["small"], "grad_join")))
    sm_red = joined["small"].reshape(-1)

    out_g, out_d, out_m, out_v = {}, {}, {}, {}
    for k in BIG:
        w2 = weights[k].reshape(-1, weights[k].shape[-1])
        g2 = joined[k].reshape(w2.shape)
        dlt, mn, vn, g_out = adamw(w2, g2, mom_m[k].reshape(w2.shape), mom_v[k].reshape(w2.shape), "adamw_" + k,
                                   emit_g=True)
        shp = weights[k].shape
        out_g[k], out_d[k], out_m[k], out_v[k] = g_out.reshape(shp), dlt.reshape(shp), mn.reshape(shp), vn.reshape(shp)
    sm_vals = dict(zip(small_names, _unpack(sm_red, [small_shapes[k] for k in small_names])))
    local_small = {}
    for k in SMALL_REPLICATED:
        local_small[k] = sm_vals[k]
    for k, axis in SMALL_SHARDED:
        size = weights[k].shape[axis]
        local_small[k] = lax.dynamic_slice_in_dim(sm_vals[k], shard * size, size, axis=axis)
    for k in small_names:
        shp = weights[k].shape
        two_d = (-1, shp[-1]) if len(shp) > 1 else (1, -1)
        dlt, mn, vn = adamw(weights[k].reshape(two_d), local_small[k].reshape(two_d), mom_m[k].reshape(two_d),
                            mom_v[k].reshape(two_d), "adamw_" + k)
        out_g[k], out_d[k], out_m[k], out_v[k] = local_small[k], dlt.reshape(shp), mn.reshape(shp), vn.reshape(shp)

    loss = lax.psum(loss11[0, 0], ("x", "y", "c"))
    return (loss, g_x[None], *[out_g[k] for k in WEIGHT_ORDER], *[out_d[k] for k in WEIGHT_ORDER],
            *[out_m[k] for k in WEIGHT_ORDER], *[out_v[k] for k in WEIGHT_ORDER])
```

```python
import functools
import math

import jax
import jax.numpy as jnp
from jax import lax
from jax.experimental import pallas as pl
from jax.experimental.pallas import tpu as pltpu

F32 = jnp.float32
BF16 = jnp.bfloat16

LRU_C = 8.0
LN_EPS = 1e-5
ADAM_LR = 0.001
ADAM_B1 = 0.9
ADAM_B2 = 0.999
ADAM_EPS = 1e-08
ADAM_WD = 0.01
ADAM_STEP = 10

LANES = 128
SUBLANES = 8
V7X_VMEM_BYTES = 64 * 1024 * 1024
VMEM_LIMIT = V7X_VMEM_BYTES * 7 // 8
N_SHARDS = 4
GELU_C0 = math.sqrt(2.0 / math.pi)
GELU_C1 = 0.044715
NEG_BIG = -1e30


def _cp(*sem):
    return pltpu.CompilerParams(dimension_semantics=tuple(sem), vmem_limit_bytes=VMEM_LIMIT)


def _tile(n, cap, mult=LANES):
    best = None
    d = mult
    while d <= min(n, cap):
        if n % d == 0:
            best = d
        d += mult
    return n if best is None else best


def _row_block(t):
    if t % 3 == 0 and (t // 3) % 16 == 0:
        return t // 3
    return t


def _round_up(n, m):
    return (n + m - 1) // m * m


def _sigmoid(v):
    return 1.0 / (1.0 + jnp.exp(-v))


def _softplus(v):
    return jnp.maximum(v, 0.0) + jnp.log(1.0 + jnp.exp(-jnp.abs(v)))


def _gelu_parts(v):
    v2 = v * v
    u = GELU_C0 * (v + GELU_C1 * v * v2)
    t = jnp.tanh(u)
    g = 0.5 * v * (1.0 + t)
    dg = 0.5 * (1.0 + t) + 0.5 * v * (1.0 - t * t) * (GELU_C0 * (1.0 + 3.0 * GELU_C1 * v2))
    return g, dg


def _gelu(v):
    u = GELU_C0 * (v + GELU_C1 * v * v * v)
    return 0.5 * v * (1.0 + jnp.tanh(u))


def _neg_expm1(v):
    series = -v * (1.0 + 0.5 * v * (1.0 + (v / 3.0) * (1.0 + 0.25 * v)))
    return jnp.where(v > -0.05, series, 1.0 - jnp.exp(v))


def _shift_down(v, j):
    if j == 0:
        return v
    rows = lax.broadcasted_iota(jnp.int32, v.shape, 0)
    return jnp.where(rows >= j, pltpu.roll(v, j, 0), 0.0)


def _shift_up(v, j):
    if j == 0:
        return v
    n = v.shape[0]
    rows = lax.broadcasted_iota(jnp.int32, v.shape, 0)
    return jnp.where(rows < n - j, pltpu.roll(v, n - j, 0), 0.0)


def _scan_rows(a_ref, b_ref, out_ref, n_rows, width, reverse):
    n_groups = n_rows // SUBLANES
    rows = lax.broadcasted_iota(jnp.int32, (SUBLANES, width), 0)
    edge = 0 if reverse else SUBLANES - 1

    def body(g, carry):
        grp = (n_groups - 1 - g) if reverse else g
        off = pl.multiple_of(grp * SUBLANES, SUBLANES)
        b = b_ref[pl.ds(off, SUBLANES), :]
        a = None if a_ref is None else a_ref[pl.ds(off, SUBLANES), :]
        for d in (1, 2, 4):
            if reverse:
                keep = rows < SUBLANES - d
                sh = SUBLANES - d
            else:
                keep = rows >= d
                sh = d
            b_s = jnp.where(keep, pltpu.roll(b, sh, 0), 0.0)
            if a is None:
                b = b + b_s
            else:
                a_s = jnp.where(keep, pltpu.roll(a, sh, 0), 1.0)
                b = a * b_s + b
                a = a * a_s
        h = b + carry if a is None else b + a * carry
        out_ref[pl.ds(off, SUBLANES), :] = h
        return jnp.sum(jnp.where(rows == edge, h, 0.0), axis=0, keepdims=True)

    lax.fori_loop(0, n_groups, body, jnp.zeros((1, width), F32), unroll=2)


def mm_in(x, w, out_dtype, name):
    t, k = x.shape
    s_n, _, ns = w.shape
    tn = _tile(ns, 1408)
    nj = ns // tn
    rb = _row_block(t)

    def body(x_ref, w_ref, o_ref):
        o_ref[...] = jnp.dot(x_ref[...], w_ref[...], preferred_element_type=F32).astype(o_ref.dtype)

    return pl.pallas_call(
        body, name=name, grid=(s_n, nj, t // rb),
        in_specs=[pl.BlockSpec((rb, k), lambda s, j, r: (r, 0)),
                  pl.BlockSpec((None, k, tn), lambda s, j, r: (s, 0, j))],
        out_specs=pl.BlockSpec((rb, tn), lambda s, j, r: (r, s * nj + j)),
        out_shape=jax.ShapeDtypeStruct((t, s_n * ns), out_dtype),
        compiler_params=_cp("parallel", "parallel", "parallel"))(x, w)


def _part_map(p, per_part, nj, lead):
    def index(*grid):
        s, j = grid[-2], grid[-1]
        mine = s // per_part == p
        col = jnp.where(mine, (s - p * per_part) * nj + j, 0)
        return (grid[0], col) if lead else (0, col)
    return index


def mm_in_nt(dy_parts, w, name):
    n_parts = len(dy_parts)
    t = dy_parts[0].shape[0]
    s_n, k, ns = w.shape
    per_part = s_n // n_parts
    tn = _tile(ns, 1408)
    nj = ns // tn
    rb = _row_block(t)

    def body(*refs):
        w_ref, o_ref = refs[n_parts:]

        @pl.when((pl.program_id(1) == 0) & (pl.program_id(2) == 0))
        def _():
            o_ref[...] = jnp.zeros_like(o_ref)

        for p in range(n_parts):
            @pl.when(pl.program_id(1) // per_part == p)
            def _():
                o_ref[...] += lax.dot_general(refs[p][...], w_ref[...], (((1,), (1,)), ((), ())),
                                              preferred_element_type=F32)

    return pl.pallas_call(
        body, name=name, grid=(t // rb, s_n, nj),
        in_specs=[pl.BlockSpec((rb, tn), _part_map(p, per_part, nj, True)) for p in range(n_parts)]
        + [pl.BlockSpec((None, k, tn), lambda r, s, j: (s, 0, j))],
        out_specs=pl.BlockSpec((rb, k), lambda r, s, j: (r, 0)),
        out_shape=jax.ShapeDtypeStruct((t, k), F32),
        compiler_params=_cp("parallel", "arbitrary", "arbitrary"))(*dy_parts, w)


def mm_out_nt(dy, w, name):
    t, n = dy.shape
    k = w.shape[0]
    rb = _row_block(t)

    def body(dy_ref, w_ref, o_ref):
        o_ref[...] = lax.dot_general(dy_ref[...], w_ref[...], (((1,), (1,)), ((), ())),
                                     preferred_element_type=F32)

    return pl.pallas_call(
        body, name=name, grid=(t // rb,),
        in_specs=[pl.BlockSpec((rb, n), lambda r: (r, 0)), pl.BlockSpec((k, n), lambda r: (0, 0))],
        out_specs=pl.BlockSpec((rb, k), lambda r: (r, 0)),
        out_shape=jax.ShapeDtypeStruct((t, k), F32),
        compiler_params=_cp("parallel"))(dy, w)


def mm_tn(x, dy_parts, s_n, name):
    n_parts = len(dy_parts)
    t, kb = x.shape
    nb = dy_parts[0].shape[1] * n_parts // s_n
    per_part = max(s_n // n_parts, 1)
    tk = _tile(kb, 1408)
    tn = _tile(nb, 1408)
    nkb, nnb = kb // tk, nb // tn
    tn_dims = (((0,), (0,)), ((), ()))

    def body(*refs):
        x_ref, o_ref = refs[0], refs[-1]
        for p in range(n_parts):
            @pl.when(pl.program_id(1) // per_part == p)
            def _():
                o_ref[...] = lax.dot_general(x_ref[...], refs[1 + p][...], tn_dims,
                                             preferred_element_type=F32).astype(o_ref.dtype)

    return pl.pallas_call(
        body, name=name, grid=(nkb, s_n, nnb),
        in_specs=[pl.BlockSpec((t, tk), lambda a, s, b: (0, a))]
        + [pl.BlockSpec((t, tn), _part_map(p, per_part, nnb, False)) for p in range(n_parts)],
        out_specs=pl.BlockSpec((None, tk, tn), lambda a, s, b: (s, a, b)),
        out_shape=jax.ShapeDtypeStruct((s_n, kb, nb), BF16),
        compiler_params=_cp("parallel", "parallel", "parallel"))(x, *dy_parts)


def mm_bd(x, wr, wi, name):
    t, _ = x.shape
    g_n, gs, _ = wr.shape
    rb = _row_block(t)

    def body(x_ref, wr_ref, wi_ref, r_ref, i_ref):
        xv = x_ref[...]
        r_ref[...] = jnp.dot(xv, wr_ref[...], preferred_element_type=F32)
        i_ref[...] = jnp.dot(xv, wi_ref[...], preferred_element_type=F32)

    blk = pl.BlockSpec((rb, gs), lambda g, r: (r, g))
    wspec = pl.BlockSpec((None, gs, gs), lambda g, r: (g, 0, 0))
    return pl.pallas_call(
        body, name=name, grid=(g_n, t // rb), in_specs=[blk, wspec, wspec], out_specs=[blk, blk],
        out_shape=[jax.ShapeDtypeStruct((t, g_n * gs), F32)] * 2,
        compiler_params=_cp("parallel", "parallel"))(x, wr, wi)


def mm_bd_nt(dr, di, wr, wi, name):
    t, _ = dr.shape
    g_n, gs, _ = wr.shape
    rb = _row_block(t)
    nt = (((1,), (1,)), ((), ()))

    def body(dr_ref, di_ref, wr_ref, wi_ref, o_ref):
        o_ref[...] = (lax.dot_general(dr_ref[...], wr_ref[...], nt, preferred_element_type=F32)
                      + lax.dot_general(di_ref[...], wi_ref[...], nt, preferred_element_type=F32))

    blk = pl.BlockSpec((rb, gs), lambda g, r: (r, g))
    wspec = pl.BlockSpec((None, gs, gs), lambda g, r: (g, 0, 0))
    return pl.pallas_call(
        body, name=name, grid=(g_n, t // rb), in_specs=[blk, blk, wspec, wspec], out_specs=blk,
        out_shape=jax.ShapeDtypeStruct((t, g_n * gs), F32),
        compiler_params=_cp("parallel", "parallel"))(dr, di, wr, wi)


def mm_bd_tn(x, dr, di, gs, name):
    t, w = x.shape
    g_n = w // gs
    tn_dims = (((0,), (0,)), ((), ()))

    def body(x_ref, dr_ref, di_ref, gr_ref, gi_ref):
        xv = x_ref[...]
        gr_ref[...] = lax.dot_general(xv, dr_ref[...], tn_dims, preferred_element_type=F32)
        gi_ref[...] = lax.dot_general(xv, di_ref[...], tn_dims, preferred_element_type=F32)

    blk = pl.BlockSpec((t, gs), lambda g: (0, g))
    ospec = pl.BlockSpec((None, gs, gs), lambda g: (g, 0, 0))
    return pl.pallas_call(
        body, name=name, grid=(g_n,), in_specs=[blk, blk, blk], out_specs=[ospec, ospec],
        out_shape=[jax.ShapeDtypeStruct((g_n, gs, gs), F32)] * 2,
        compiler_params=_cp("parallel"))(x, dr, di)


def embed_fwd(meta, x2d, name):
    nm, d = meta.shape
    seq = x2d.shape[0]
    t = nm + seq
    cb = _tile(d, 256)

    def body(m_ref, x_ref, h_ref, hb_ref):
        h_ref[pl.ds(0, nm), :] = m_ref[...]
        h_ref[pl.ds(nm, seq), :] = x_ref[...]
        hb_ref[pl.ds(0, nm), :] = m_ref[...].astype(BF16)
        hb_ref[pl.ds(nm, seq), :] = x_ref[...].astype(BF16)

    return pl.pallas_call(
        body, name=name, grid=(d // cb,),
        in_specs=[pl.BlockSpec((nm, cb), lambda j: (0, j)), pl.BlockSpec((seq, cb), lambda j: (0, j))],
        out_specs=[pl.BlockSpec((t, cb), lambda j: (0, j))] * 2,
        out_shape=[jax.ShapeDtypeStruct((t, d), F32), jax.ShapeDtypeStruct((t, d), BF16)],
        compiler_params=_cp("parallel"))(meta, x2d)


def embed_bwd(adds, nm, name):
    t, d = adds[0][0].shape
    seq = t - nm
    cb = _tile(d, 256)
    scales = [s for _, s in adds]
    n = len(adds)

    def body(*refs):
        tot = None
        for r, sc in zip(refs[:n], scales):
            term = r[...] if sc == 1.0 else sc * r[...]
            tot = term if tot is None else tot + term
        gm_ref, gx_ref = refs[n], refs[n + 1]
        gm_ref[...] = tot[0:nm]
        gx_ref[...] = tot[nm:t]

    return pl.pallas_call(
        body, name=name, grid=(d // cb,),
        in_specs=[pl.BlockSpec((t, cb), lambda j: (0, j))] * n,
        out_specs=[pl.BlockSpec((nm, cb), lambda j: (0, j)), pl.BlockSpec((seq, cb), lambda j: (0, j))],
        out_shape=[jax.ShapeDtypeStruct((nm, d), F32), jax.ShapeDtypeStruct((seq, d), F32)],
        compiler_params=_cp("parallel"))(*[a for a, _ in adds])


def loss_fwd_bwd(h, tgt, nm, name):
    t, d = h.shape
    seq = t - nm
    cb = _tile(d, 256)
    inv_d = 1.0 / d

    def body(h_ref, t_ref, loss_ref, dy_ref):
        @pl.when(pl.program_id(0) == 0)
        def _():
            loss_ref[...] = jnp.zeros_like(loss_ref)
        err = h_ref[pl.ds(nm, seq), :] - t_ref[...]
        dy_ref[pl.ds(0, nm), :] = jnp.zeros((nm, cb), F32)
        dy_ref[pl.ds(nm, seq), :] = err * inv_d
        loss_ref[...] += (0.5 * inv_d) * jnp.sum(err * err, keepdims=True)

    return pl.pallas_call(
        body, name=name, grid=(d // cb,),
        in_specs=[pl.BlockSpec((t, cb), lambda j: (0, j)), pl.BlockSpec((seq, cb), lambda j: (0, j))],
        out_specs=[pl.BlockSpec((1, 1), lambda j: (0, 0)), pl.BlockSpec((t, cb), lambda j: (0, j))],
        out_shape=[jax.ShapeDtypeStruct((1, 1), F32), jax.ShapeDtypeStruct((t, d), F32)],
        compiler_params=_cp("arbitrary"))(h, tgt)


def mm_out_ln(x, w, h, g, b, alpha, name):
    t, d = h.shape
    k = x.shape[1]
    rb = _row_block(t)

    def body(x_ref, w_ref, h_ref, g_ref, b_ref, y_ref, yb_ref, xh_ref, rs_ref):
        s = alpha * h_ref[...] + jnp.dot(x_ref[...], w_ref[...], preferred_element_type=F32)
        mu = jnp.mean(s, axis=-1, keepdims=True)
        c = s - mu
        var = jnp.mean(c * c, axis=-1, keepdims=True)
        rstd = lax.rsqrt(var + LN_EPS)
        xh = c * rstd
        y = xh * g_ref[...] + b_ref[...]
        y_ref[...] = y
        yb_ref[...] = y.astype(BF16)
        xh_ref[...] = xh
        rs_ref[...] = rstd

    row = pl.BlockSpec((rb, d), lambda r: (r, 0))
    vec = pl.BlockSpec((1, d), lambda r: (0, 0))
    return pl.pallas_call(
        body, name=name, grid=(t // rb,),
        in_specs=[pl.BlockSpec((rb, k), lambda r: (r, 0)), pl.BlockSpec((k, d), lambda r: (0, 0)), row, vec, vec],
        out_specs=[row, row, row, pl.BlockSpec((rb, 1), lambda r: (r, 0))],
        out_shape=[jax.ShapeDtypeStruct((t, d), F32), jax.ShapeDtypeStruct((t, d), BF16),
                   jax.ShapeDtypeStruct((t, d), F32), jax.ShapeDtypeStruct((t, 1), F32)],
        compiler_params=_cp("parallel"))(x, w, h, g, b)


def ln_bwd(adds, xhat, rstd, g, name):
    t, d = xhat.shape
    rb = _row_block(t)
    scales = [s for _, s in adds]
    n = len(adds)

    def body(*refs):
        xh_ref, rs_ref, g_ref = refs[n:n + 3]
        ds_ref, dsb_ref, dg_ref, db_ref = refs[n + 3:]
        dy = None
        for r, sc in zip(refs[:n], scales):
            term = r[...] if sc == 1.0 else sc * r[...]
            dy = term if dy is None else dy + term

        @pl.when(pl.program_id(0) == 0)
        def _():
            dg_ref[...] = jnp.zeros_like(dg_ref)
            db_ref[...] = jnp.zeros_like(db_ref)

        xh = xh_ref[...]
        dxh = dy * g_ref[...]
        m1 = jnp.mean(dxh, axis=-1, keepdims=True)
        m2 = jnp.mean(dxh * xh, axis=-1, keepdims=True)
        ds = rs_ref[...] * (dxh - m1 - xh * m2)
        ds_ref[...] = ds
        dsb_ref[...] = ds.astype(BF16)
        dg_ref[...] += jnp.sum(dy * xh, axis=0, keepdims=True)
        db_ref[...] += jnp.sum(dy, axis=0, keepdims=True)

    row = pl.BlockSpec((rb, d), lambda r: (r, 0))
    vec = pl.BlockSpec((1, d), lambda r: (0, 0))
    return pl.pallas_call(
        body, name=name, grid=(t // rb,),
        in_specs=[row] * n + [row, pl.BlockSpec((rb, 1), lambda r: (r, 0)), vec],
        out_specs=[row, row, vec, vec],
        out_shape=[jax.ShapeDtypeStruct((t, d), F32), jax.ShapeDtypeStruct((t, d), BF16),
                   jax.ShapeDtypeStruct((1, d), F32), jax.ShapeDtypeStruct((1, d), F32)],
        compiler_params=_cp("arbitrary"))(*[a for a, _ in adds], xhat, rstd, g)


def _conv_fwd_val(xv, w_ref, b_ref, width):
    y = b_ref[...]
    for j in range(width):
        y = y + _shift_down(xv, j) * w_ref[pl.ds(width - 1 - j, 1), :]
    return y


def _conv_bwd_val(dout, xv, w_ref, width):
    dx = None
    dws = [None] * width
    for j in range(width):
        k = width - 1 - j
        term = _shift_up(dout, j) * w_ref[pl.ds(k, 1), :]
        dx = term if dx is None else dx + term
        dws[k] = jnp.sum(dout * _shift_down(xv, j), axis=0, keepdims=True)
    return dx, dws, jnp.sum(dout, axis=0, keepdims=True)


def a_conv_fwd(gr, cw, cbias, name):
    t, two_dr = gr.shape
    dr = two_dr // 2
    width = cw.shape[0]
    cb = _tile(dr, 256)
    off = dr // cb

    def body(x_ref, w_ref, b_ref, rc_ref, rcb_ref):
        y = _conv_fwd_val(x_ref[...], w_ref, b_ref, width)
        rc_ref[...] = y
        rcb_ref[...] = y.astype(BF16)

    return pl.pallas_call(
        body, name=name, grid=(dr // cb,),
        in_specs=[pl.BlockSpec((t, cb), lambda j: (0, off + j)),
                  pl.BlockSpec((width, cb), lambda j: (0, j)), pl.BlockSpec((1, cb), lambda j: (0, j))],
        out_specs=[pl.BlockSpec((t, cb), lambda j: (0, j))] * 2,
        out_shape=[jax.ShapeDtypeStruct((t, dr), F32), jax.ShapeDtypeStruct((t, dr), BF16)],
        compiler_params=_cp("parallel"))(gr, cw, cbias)


def a_conv_bwd(drc_a, drc_b, gr, cw, dgr, name):
    t, two_dr = gr.shape
    dr = two_dr // 2
    width = cw.shape[0]
    cb = _tile(dr, 256)
    off = dr // cb

    def body(da_ref, db_ref, x_ref, w_ref, dgr_in, dx_ref, dw_ref, dbias_ref):
        del dgr_in
        dout = da_ref[...] + db_ref[...]
        dx, dws, dbias = _conv_bwd_val(dout, x_ref[...], w_ref, width)
        dx_ref[...] = dx.astype(BF16)
        for k in range(width):
            dw_ref[pl.ds(k, 1), :] = dws[k]
        dbias_ref[...] = dbias

    col = pl.BlockSpec((t, cb), lambda j: (0, j))
    return pl.pallas_call(
        body, name=name, grid=(dr // cb,),
        in_specs=[col, col, pl.BlockSpec((t, cb), lambda j: (0, off + j)),
                  pl.BlockSpec((width, cb), lambda j: (0, j)), pl.BlockSpec(memory_space=pl.ANY)],
        out_specs=[pl.BlockSpec((t, cb), lambda j: (0, off + j)), pl.BlockSpec((width, cb), lambda j: (0, j)),
                   pl.BlockSpec((1, cb), lambda j: (0, j))],
        out_shape=[jax.ShapeDtypeStruct((t, two_dr), BF16), jax.ShapeDtypeStruct((width, dr), F32),
                   jax.ShapeDtypeStruct((1, dr), F32)],
        input_output_aliases={4: 0},
        compiler_params=_cp("parallel"))(drc_a, drc_b, gr, cw, dgr)


def _lru_gates(r_pre, i_pre, br, bi, lam):
    r = _sigmoid(r_pre + br)
    i = _sigmoid(i_pre + bi)
    sp = _softplus(-lam)
    la = -LRU_C * r * sp
    a = jnp.exp(la)
    m = jnp.sqrt(_neg_expm1(2.0 * la))
    return r, i, sp, la, a, m


def a_elem_fwd(gr, rc, r_pre, i_pre, br, bi, lam, name):
    t, dr = rc.shape
    cb = _tile(dr, 2 * LANES)
    rb = _row_block(t)
    chunks = [pl.ds(r * rb, rb) for r in range(t // rb)]

    def body(gate_ref, rc_ref, rp_ref, ip_ref, br_ref, bi_ref, lam_ref, hs_ref, g_ref, a_s, u_s):
        for rows in chunks:
            _, i, _, _, a, m = _lru_gates(rp_ref[rows, :], ip_ref[rows, :], br_ref[...], bi_ref[...], lam_ref[...])
            a_s[rows, :] = a
            u_s[rows, :] = m * (i * rc_ref[rows, :])
        _scan_rows(a_s, u_s, hs_ref, t, cb, reverse=False)
        for rows in chunks:
            g_ref[rows, :] = (_gelu(gate_ref[rows, :]) * hs_ref[rows, :]).astype(BF16)

    col = pl.BlockSpec((t, cb), lambda j: (0, j))
    vec = pl.BlockSpec((1, cb), lambda j: (0, j))
    return pl.pallas_call(
        body, name=name, grid=(dr // cb,),
        in_specs=[col, col, col, col, vec, vec, vec],
        out_specs=[col, col],
        out_shape=[jax.ShapeDtypeStruct((t, dr), F32), jax.ShapeDtypeStruct((t, dr), BF16)],
        scratch_shapes=[pltpu.VMEM((t, cb), F32), pltpu.VMEM((t, cb), F32)],
        compiler_params=_cp("parallel"))(gr, rc, r_pre, i_pre, br, bi, lam)


def a_elem_bwd(dg, gr, rc, r_pre, i_pre, hs, br, bi, lam, name):
    t, dr = rc.shape
    cb = _tile(dr, 2 * LANES)
    rb = _row_block(t)
    chunks = [pl.ds(r * rb, rb) for r in range(t // rb)]

    def body(dg_ref, gate_ref, rc_ref, rp_ref, ip_ref, hs_ref, br_ref, bi_ref, lam_ref,
             dgate_ref, dr_ref, di_ref, drc_ref, dlam_ref, dbr_ref, dbi_ref, a_s, b_s, g_s, hp_s):
        lamv = lam_ref[...]
        gates = lambda rows: _lru_gates(rp_ref[rows, :], ip_ref[rows, :], br_ref[...], bi_ref[...], lamv)
        for rows in chunks:
            a_s[rows, :] = gates(rows)[4]
            ge, dge = _gelu_parts(gate_ref[rows, :])
            dgv = dg_ref[rows, :]
            dgate_ref[rows, :] = (dgv * hs_ref[rows, :] * dge).astype(BF16)
            b_s[rows, :] = dgv * ge
        a_s[...] = _shift_up(a_s[...], 1)
        hp_s[...] = _shift_down(hs_ref[...], 1)
        _scan_rows(a_s, b_s, g_s, t, cb, reverse=True)
        dsp = dbr = dbi = jnp.zeros((1, cb), F32)
        for rows in chunks:
            r, i, sp, _, a, m = gates(rows)
            rcv = rc_ref[rows, :]
            gsum = g_s[rows, :]
            da = gsum * hp_s[rows, :]
            dm = gsum * (i * rcv)
            d_i = gsum * m * rcv
            drc_ref[rows, :] = gsum * m * i
            dla = a * da - dm * (a * a) / m
            d_r = (-LRU_C) * sp * dla
            dsp = dsp + jnp.sum((-LRU_C) * r * dla, axis=0, keepdims=True)
            d_rp = d_r * r * (1.0 - r)
            d_ip = d_i * i * (1.0 - i)
            dr_ref[rows, :] = d_rp.astype(BF16)
            di_ref[rows, :] = d_ip.astype(BF16)
            dbr = dbr + jnp.sum(d_rp, axis=0, keepdims=True)
            dbi = dbi + jnp.sum(d_ip, axis=0, keepdims=True)
        dlam_ref[...] = -dsp * _sigmoid(-lamv)
        dbr_ref[...] = dbr
        dbi_ref[...] = dbi

    col = pl.BlockSpec((t, cb), lambda j: (0, j))
    vec = pl.BlockSpec((1, cb), lambda j: (0, j))
    big_b = jax.ShapeDtypeStruct((t, dr), BF16)
    vec_s = jax.ShapeDtypeStruct((1, dr), F32)
    return pl.pallas_call(
        body, name=name, grid=(dr // cb,),
        in_specs=[col, col, col, col, col, col, vec, vec, vec],
        out_specs=[col, col, col, col, vec, vec, vec],
        out_shape=[jax.ShapeDtypeStruct((t, 2 * dr), BF16), big_b, big_b, jax.ShapeDtypeStruct((t, dr), F32),
                   vec_s, vec_s, vec_s],
        scratch_shapes=[pltpu.VMEM((t, cb), F32)] * 4,
        compiler_params=_cp("parallel"))(dg, gr, rc, r_pre, i_pre, hs, br, bi, lam)


def f_elem_fwd(z, cw, cbias, name):
    t, two_f = z.shape
    dff = two_f // 2
    width = cw.shape[0]
    cb = _tile(dff, 256)
    off = dff // cb

    def body(zg_ref, zv_ref, wg_ref, wv_ref, bg_ref, bv_ref, o_ref):
        zcg = _conv_fwd_val(zg_ref[...], wg_ref, bg_ref, width)
        zcv = _conv_fwd_val(zv_ref[...], wv_ref, bv_ref, width)
        o_ref[...] = (_gelu(zcg) * zcv).astype(BF16)

    lo = lambda j: (0, j)
    hi = lambda j: (0, off + j)
    return pl.pallas_call(
        body, name=name, grid=(dff // cb,),
        in_specs=[pl.BlockSpec((t, cb), lo), pl.BlockSpec((t, cb), hi),
                  pl.BlockSpec((width, cb), lo), pl.BlockSpec((width, cb), hi),
                  pl.BlockSpec((1, cb), lo), pl.BlockSpec((1, cb), hi)],
        out_specs=pl.BlockSpec((t, cb), lo),
        out_shape=jax.ShapeDtypeStruct((t, dff), BF16),
        compiler_params=_cp("parallel"))(z, z, cw, cw, cbias, cbias)


def f_elem_bwd(z, dff_g, cw, cbias, name):
    t, two_f = z.shape
    dff = two_f // 2
    width = cw.shape[0]
    cb = _tile(dff, 256)
    off = dff // cb

    def body(zg_ref, zv_ref, d_ref, wg_ref, wv_ref, bg_ref, bv_ref,
             dzg_ref, dzv_ref, dwg_ref, dwv_ref, dbg_ref, dbv_ref):
        zg = zg_ref[...]
        zv = zv_ref[...]
        zcg = _conv_fwd_val(zg, wg_ref, bg_ref, width)
        zcv = _conv_fwd_val(zv, wv_ref, bv_ref, width)
        ge, dge = _gelu_parts(zcg)
        dv = d_ref[...]
        dx, dws, dbias = _conv_bwd_val(dv * zcv * dge, zg, wg_ref, width)
        dzg_ref[...] = dx.astype(BF16)
        for k in range(width):
            dwg_ref[pl.ds(k, 1), :] = dws[k]
        dbg_ref[...] = dbias
        dx, dws, dbias = _conv_bwd_val(dv * ge, zv, wv_ref, width)
        dzv_ref[...] = dx.astype(BF16)
        for k in range(width):
            dwv_ref[pl.ds(k, 1), :] = dws[k]
        dbv_ref[...] = dbias

    lo = lambda j: (0, j)
    hi = lambda j: (0, off + j)
    col = pl.BlockSpec((t, cb), lo)
    wsp = pl.BlockSpec((width, cb), lo)
    vsp = pl.BlockSpec((1, cb), lo)
    return pl.pallas_call(
        body, name=name, grid=(dff // cb,),
        in_specs=[col, pl.BlockSpec((t, cb), hi), col, wsp, pl.BlockSpec((width, cb), hi),
                  vsp, pl.BlockSpec((1, cb), hi)],
        out_specs=[col, col, wsp, wsp, vsp, vsp],
        out_shape=[jax.ShapeDtypeStruct((t, dff), BF16)] * 2
        + [jax.ShapeDtypeStruct((width, dff), F32)] * 2 + [jax.ShapeDtypeStruct((1, dff), F32)] * 2,
        compiler_params=_cp("parallel"))(z, z, dff_g, cw, cw, cbias, cbias)


def kv_fwd(z, fb, d_model, name):
    t, _ = z.shape
    blk = 2 * d_model // LANES

    def body(z_ref, fb_ref, c_ref, lf_s):
        v = z_ref[...] + fb_ref[...]
        lf_s[...] = -_softplus(-v)
        _scan_rows(None, lf_s, c_ref, t, LANES, reverse=False)

    return pl.pallas_call(
        body, name=name, grid=(1,),
        in_specs=[pl.BlockSpec((t, LANES), lambda j: (0, blk)), pl.BlockSpec((1, LANES), lambda j: (0, 0))],
        out_specs=pl.BlockSpec((t, LANES), lambda j: (0, 0)),
        out_shape=jax.ShapeDtypeStruct((t, LANES), F32),
        scratch_shapes=[pltpu.VMEM((t, LANES), F32)],
        compiler_params=_cp("arbitrary"))(z, fb)


def kv_bwd(dcs, z, fb, d_model, name):
    t, _ = z.shape
    blk = 2 * d_model // LANES
    n = len(dcs)

    def body(*refs):
        z_ref, fb_ref, dz_ref, dfb_ref, dc_s, dl_s = refs[n:]
        tot = refs[0][...]
        for r in refs[1:n]:
            tot = tot + r[...]
        dc_s[...] = tot
        _scan_rows(None, dc_s, dl_s, t, LANES, reverse=True)
        v = z_ref[...] + fb_ref[...]
        dz = dl_s[...] * _sigmoid(-v)
        dz_ref[...] = dz.astype(BF16)
        dfb_ref[...] = jnp.sum(dz, axis=0, keepdims=True)

    full = pl.BlockSpec((t, LANES), lambda j: (0, 0))
    return pl.pallas_call(
        body, name=name, grid=(1,),
        in_specs=[full] * n + [pl.BlockSpec((t, LANES), lambda j: (0, blk)),
                               pl.BlockSpec((1, LANES), lambda j: (0, 0))],
        out_specs=[full, pl.BlockSpec((1, LANES), lambda j: (0, 0))],
        out_shape=[jax.ShapeDtypeStruct((t, LANES), BF16), jax.ShapeDtypeStruct((1, LANES), F32)],
        scratch_shapes=[pltpu.VMEM((t, LANES), F32)] * 2,
        compiler_params=_cp("arbitrary"))(*dcs, z, fb)


def add_cast(a, b, name):
    t, d = a.shape
    cb = _tile(d, 512)

    def body(a_ref, b_ref, o_ref):
        o_ref[...] = (a_ref[...] + b_ref[...]).astype(BF16)

    col = pl.BlockSpec((t, cb), lambda j: (0, j))
    return pl.pallas_call(body, name=name, grid=(d // cb,), in_specs=[col, col], out_specs=col,
                          out_shape=jax.ShapeDtypeStruct((t, d), BF16),
                          compiler_params=_cp("parallel"))(a, b)


def _attn_geometry(t):
    nqb = 6 if t > 1024 else 2
    tp = _round_up(t, LANES * nqb)
    return nqb, tp, tp // nqb


def _attn_scales(dh):
    scale = dh ** -0.5
    if math.log2(scale).is_integer():
        return scale, 1.0
    return 1.0, scale


def _attn_pieces(qs, ks, crow, j, i, tq, dh, s_mul):
    r0 = i * tq
    lanes = pl.ds(j * dh, dh)
    qi = qs[pl.ds(r0, tq), lanes]
    spans = ([(0, r0)] if i > 0 else []) + [(r0, tq)]
    logits = []
    for k0, n in spans:
        s = lax.dot_general(qi, ks[pl.ds(k0, n), lanes], (((1,), (1,)), ((), ())),
                            preferred_element_type=F32)
        if s_mul != 1.0:
            s = s * s_mul
        s = s - crow[:, k0:k0 + n]
        if k0 == r0:
            rows = lax.broadcasted_iota(jnp.int32, (tq, tq), 0)
            cols = lax.broadcasted_iota(jnp.int32, (tq, tq), 1)
            s = jnp.where(cols <= rows, s, NEG_BIG)
        logits.append(s)
    mx = jnp.max(logits[0], axis=1, keepdims=True)
    for s in logits[1:]:
        mx = jnp.maximum(mx, jnp.max(s, axis=1, keepdims=True))
    es = [jnp.exp(s - mx) for s in logits]
    tot = jnp.sum(es[0], axis=1, keepdims=True)
    for e in es[1:]:
        tot = tot + jnp.sum(e, axis=1, keepdims=True)
    inv = 1.0 / tot
    return [(k0, n, e * inv) for (k0, n), e in zip(spans, es)], qi


def attn_fwd(qg, z, ct_pad, d_model, n_heads, name):
    t = qg.shape[0]
    dh = d_model // n_heads
    hp = LANES // dh
    nqb, tp, tq = _attn_geometry(t)
    nblk = d_model // LANES
    q_mul, s_mul = _attn_scales(dh)

    def body(q_ref, og_ref, k_ref, v_ref, ct_ref, o_ref, mo_ref, qs, ks, vs, os_):
        pad = jnp.zeros((tp - t, LANES), BF16)
        qs[pl.ds(0, t), :] = (q_ref[...] * q_mul).astype(BF16)
        qs[pl.ds(t, tp - t), :] = pad
        for src, dst in ((k_ref, ks), (v_ref, vs)):
            dst[pl.ds(0, t), :] = src[...].astype(BF16)
            dst[pl.ds(t, tp - t), :] = pad
        for j in range(hp):
            crow = ct_ref[j]
            lanes = pl.ds(j * dh, dh)
            for i in range(nqb):
                pieces, _ = _attn_pieces(qs, ks, crow, j, i, tq, dh, s_mul)
                acc = None
                for k0, n, p in pieces:
                    part = jnp.dot(p.astype(BF16), vs[pl.ds(k0, n), lanes], preferred_element_type=F32)
                    acc = part if acc is None else acc + part
                os_[pl.ds(i * tq, tq), lanes] = acc
        o = os_[pl.ds(0, t), :]
        o_ref[...] = o
        mo_ref[...] = (o * _sigmoid(og_ref[...])).astype(BF16)

    col = lambda off: pl.BlockSpec((t, LANES), lambda p: (0, off + p))
    return pl.pallas_call(
        body, name=name, grid=(nblk,),
        in_specs=[col(0), col(nblk), col(0), col(nblk), pl.BlockSpec((hp, 1, tp), lambda p: (p, 0, 0))],
        out_specs=[col(0), col(0)],
        out_shape=[jax.ShapeDtypeStruct((t, d_model), F32), jax.ShapeDtypeStruct((t, d_model), BF16)],
        scratch_shapes=[pltpu.VMEM((tp, LANES), BF16)] * 3 + [pltpu.VMEM((tp, LANES), F32)],
        compiler_params=_cp("parallel"))(qg, qg, z, z, ct_pad)


def attn_bwd(dmo, qg, z, o, ct_pad, d_model, n_heads, name):
    t = qg.shape[0]
    dh = d_model // n_heads
    hp = LANES // dh
    nqb, tp, tq = _attn_geometry(t)
    nblk = d_model // LANES
    q_mul, s_mul = _attn_scales(dh)
    scale = dh ** -0.5
    tn_dims = (((0,), (0,)), ((), ()))
    nt_dims = (((1,), (1,)), ((), ()))

    def body(dmo_ref, q_ref, og_ref, k_ref, v_ref, o_ref, ct_ref,
             dq_ref, dog_ref, dk_ref, dv_ref, dct_ref, qs, ks, vs, dos, dqs, dks, dvs):
        pad = jnp.zeros((tp - t, LANES), BF16)
        sg = _sigmoid(og_ref[...])
        dmo_v = dmo_ref[...]
        dog_ref[...] = (dmo_v * o_ref[...] * sg * (1.0 - sg)).astype(BF16)
        dos[pl.ds(0, t), :] = (dmo_v * sg).astype(BF16)
        dos[pl.ds(t, tp - t), :] = pad
        qs[pl.ds(0, t), :] = (q_ref[...] * q_mul).astype(BF16)
        qs[pl.ds(t, tp - t), :] = pad
        for src, dst in ((k_ref, ks), (v_ref, vs)):
            dst[pl.ds(0, t), :] = src[...].astype(BF16)
            dst[pl.ds(t, tp - t), :] = pad
        dks[...] = jnp.zeros_like(dks)
        dvs[...] = jnp.zeros_like(dvs)
        dct_ref[...] = jnp.zeros_like(dct_ref)
        for j in range(hp):
            crow = ct_ref[j]
            lanes = pl.ds(j * dh, dh)
            for i in range(nqb):
                pieces, qi = _attn_pieces(qs, ks, crow, j, i, tq, dh, s_mul)
                do_i = dos[pl.ds(i * tq, tq), lanes]
                dps = [lax.dot_general(do_i, vs[pl.ds(k0, n), lanes], nt_dims, preferred_element_type=F32)
                       for k0, n, _ in pieces]
                row = None
                for (_, _, p), dp in zip(pieces, dps):
                    part = jnp.sum(p * dp, axis=1, keepdims=True)
                    row = part if row is None else row + part
                dq_i = None
                for (k0, n, p), dp in zip(pieces, dps):
                    ds = p * (dp - row)
                    ds_b = ds.astype(BF16)
                    keys = pl.ds(k0, n)
                    part = jnp.dot(ds_b, ks[keys, lanes], preferred_element_type=F32)
                    dq_i = part if dq_i is None else dq_i + part
                    dks[keys, lanes] += lax.dot_general(ds_b, qi, tn_dims, preferred_element_type=F32) * s_mul
                    dvs[keys, lanes] += lax.dot_general(p.astype(BF16), do_i, tn_dims,
                                                        preferred_element_type=F32)
                    dct_ref[j, :, keys] -= jnp.sum(ds, axis=0, keepdims=True)
                dqs[pl.ds(i * tq, tq), lanes] = dq_i * scale
        dq_ref[...] = dqs[pl.ds(0, t), :].astype(BF16)
        dk_ref[...] = dks[pl.ds(0, t), :]
        dv_ref[...] = dvs[pl.ds(0, t), :]

    col = lambda off: pl.BlockSpec((t, LANES), lambda p: (0, off + p))
    big = lambda dt: jax.ShapeDtypeStruct((t, d_model), dt)
    return pl.pallas_call(
        body, name=name, grid=(nblk,),
        in_specs=[col(0), col(0), col(nblk), col(0), col(nblk), col(0),
                  pl.BlockSpec((hp, 1, tp), lambda p: (p, 0, 0))],
        out_specs=[col(0), col(0), col(0), col(0), pl.BlockSpec((hp, 1, tp), lambda p: (p, 0, 0))],
        out_shape=[big(BF16), big(BF16), big(F32), big(F32),
                   jax.ShapeDtypeStruct((n_heads, 1, tp), F32)],
        scratch_shapes=[pltpu.VMEM((tp, LANES), BF16)] * 4 + [pltpu.VMEM((tp, LANES), F32)] * 3,
        compiler_params=_cp("parallel"))(dmo, qg, qg, z, z, o, ct_pad)


def cast_into_slot(shard, w, index, name):
    r, c = w.shape[-2:]
    rh = r // 2
    tr = _tile(rh, 512, 16)
    n = rh // tr
    if w.ndim == 3:
        w_spec = pl.BlockSpec((None, tr, c), lambda h, i, sh: (index, h * n + i, 0))
    else:
        w_spec = pl.BlockSpec((tr, c), lambda h, i, sh: (h * n + i, 0))

    def body(sh_ref, w_ref, o_ref):
        del sh_ref
        o_ref[...] = w_ref[...].astype(BF16)

    return pl.pallas_call(
        body, name=name,
        grid_spec=pltpu.PrefetchScalarGridSpec(
            num_scalar_prefetch=1, grid=(2, n), in_specs=[w_spec],
            out_specs=pl.BlockSpec((None, None, tr, c), lambda h, i, sh: (sh[0], h, i, 0))),
        out_shape=jax.ShapeDtypeStruct((N_SHARDS, 2, rh, c), BF16),
        compiler_params=_cp("parallel", "parallel"))(shard, w)


def owner_sum(shard_core, g, recv, buf, layer, name):
    _, _, rh, c = g.shape
    n_recv = recv.shape[0]
    tr = _tile(rh, 512, 16)
    slot = (lambda sc: sc[0]) if layer is None else (lambda sc: layer)

    def body(sc_ref, a_ref, r_ref, buf_ref, out_ref):
        del sc_ref, buf_ref
        acc = a_ref[...].astype(F32)
        for k in range(n_recv):
            acc = acc + r_ref[k].astype(F32)
        out_ref[...] = acc

    return pl.pallas_call(
        body, name=name,
        grid_spec=pltpu.PrefetchScalarGridSpec(
            num_scalar_prefetch=1, grid=(rh // tr,),
            in_specs=[pl.BlockSpec((None, None, tr, c), lambda i, sc: (sc[0], sc[1], i, 0)),
                      pl.BlockSpec((n_recv, tr, c), lambda i, sc: (0, i, 0)),
                      pl.BlockSpec(memory_space=pl.ANY)],
            out_specs=pl.BlockSpec((None, None, tr, c), lambda i, sc: (slot(sc), sc[1], i, 0))),
        out_shape=jax.ShapeDtypeStruct(buf.shape, F32),
        input_output_aliases={3: 0},
        compiler_params=_cp("parallel"))(shard_core, g, recv, buf)


def adamw(w, g, m, v, name, emit_g=False):
    r, c = w.shape
    tr = _tile(r, 512, SUBLANES)
    c1 = 1.0 - ADAM_B1 ** ADAM_STEP
    c2 = 1.0 - ADAM_B2 ** ADAM_STEP
    n_out = 4 if emit_g else 3

    def body(w_ref, g_ref, m_ref, v_ref, d_ref, mo_ref, vo_ref, *go_ref):
        gv = g_ref[...]
        if emit_g:
            go_ref[0][...] = gv
        mn = ADAM_B1 * m_ref[...] + (1.0 - ADAM_B1) * gv
        vn = ADAM_B2 * v_ref[...] + (1.0 - ADAM_B2) * (gv * gv)
        m_hat = mn / c1
        v_hat = vn / c2
        d_ref[...] = -ADAM_LR * (m_hat / (jnp.sqrt(v_hat) + ADAM_EPS) + ADAM_WD * w_ref[...])
        mo_ref[...] = mn
        vo_ref[...] = vn

    blk = pl.BlockSpec((tr, c), lambda i: (i, 0))
    return pl.pallas_call(
        body, name=name, grid=(r // tr,), in_specs=[blk] * 4, out_specs=[blk] * n_out,
        out_shape=[jax.ShapeDtypeStruct((r, c), F32)] * n_out,
        compiler_params=_cp("parallel"))(w, g, m, v)


def _coords():
    return lax.axis_index("x"), lax.axis_index("y"), lax.axis_index("c")


def _exchange(name, ins, out_shapes, plan, in_place=False):
    n_in = len(ins)
    n_out = len(out_shapes)
    n_rem = len(plan([None] * n_in, [None] * n_out, True))

    def body(*refs):
        in_refs = refs[:n_in]
        out_refs = refs[n_in:n_in + n_out]
        send_sems, recv_sems = refs[n_in + n_out:]
        copies = [pltpu.make_async_remote_copy(
            src_ref=src, dst_ref=dst, send_sem=send_sems.at[q], recv_sem=recv_sems.at[q],
            device_id=peer, device_id_type=pl.DeviceIdType.MESH)
            for q, (src, dst, peer) in enumerate(plan(list(in_refs), list(out_refs), False))]
        for cp in copies:
            cp.start()
        for cp in copies:
            cp.wait_recv()
        for cp in copies:
            cp.wait_send()

    hbm = pl.BlockSpec(memory_space=pl.ANY)
    return pl.pallas_call(
        body, name=name, in_specs=[hbm] * n_in, out_specs=[hbm] * n_out, out_shape=out_shapes,
        input_output_aliases={i: i for i in range(n_in)} if in_place else {},
        scratch_shapes=[pltpu.SemaphoreType.DMA((n_rem,)), pltpu.SemaphoreType.DMA((n_rem,))],
        compiler_params=pltpu.CompilerParams(has_side_effects=True))(*ins)


def _split_start(name, groups, plan):
    flat = [a for grp in groups for a in grp]
    n, n_grp = len(flat), len(groups)
    counts = [len(plan(g, [None] * len(grp), True)) for g, grp in enumerate(groups)]

    def body(*refs):
        ins, sems, token = refs[:n], refs[n:n + 2 * n_grp], refs[-1]
        pos = 0
        for g, grp in enumerate(groups):
            arrs = list(ins[pos:pos + len(grp)])
            pos += len(grp)
            for q, (src, dst, peer) in enumerate(plan(g, arrs, False)):
                pltpu.make_async_remote_copy(
                    src_ref=src, dst_ref=dst, send_sem=sems[2 * g].at[q], recv_sem=sems[2 * g + 1].at[q],
                    device_id=peer, device_id_type=pl.DeviceIdType.MESH).start()
        token[...] = jnp.zeros_like(token)

    hbm = pl.BlockSpec(memory_space=pltpu.HBM)
    sem = pl.BlockSpec(memory_space=pltpu.SEMAPHORE)
    outs = pl.pallas_call(
        body, name=name,
        out_shape=[pltpu.SemaphoreType.DMA((cnt,)) for cnt in counts for _ in range(2)]
        + [pltpu.HBM(a.shape, a.dtype) for a in flat] + [jax.ShapeDtypeStruct((SUBLANES, LANES), F32)],
        in_specs=[hbm] * n, out_specs=[sem] * (2 * n_grp) + [hbm] * n + [pl.BlockSpec(memory_space=pltpu.VMEM)],
        input_output_aliases={i: 2 * n_grp + i for i in range(n)},
        compiler_params=pltpu.CompilerParams(has_side_effects=pltpu.SideEffectType.DATAFLOW_SIDE_EFFECTING),
    )(*[pltpu.with_memory_space_constraint(a, pltpu.HBM) for a in flat])
    started, pos = [], 2 * n_grp
    for g, grp in enumerate(groups):
        started.append((outs[2 * g], outs[2 * g + 1], list(outs[pos:pos + len(grp)])))
        pos += len(grp)
    return started, outs[-1]


def _split_wait(name, started, after, plan_g):
    send_sems, recv_sems, arrs = started
    n = len(arrs)

    def body(*refs):
        ins, ssem, rsem = list(refs[:n]), refs[n], refs[n + 1]
        for q, (src, dst, peer) in enumerate(plan_g(ins, False)):
            cp = pltpu.make_async_remote_copy(
                src_ref=src, dst_ref=dst, send_sem=ssem.at[q], recv_sem=rsem.at[q],
                device_id=peer, device_id_type=pl.DeviceIdType.MESH)
            cp.wait_send()
            cp.wait_recv()

    hbm = pl.BlockSpec(memory_space=pltpu.HBM)
    sem = pl.BlockSpec(memory_space=pltpu.SEMAPHORE)
    return pl.pallas_call(
        body, name=name, out_shape=[pltpu.HBM(a.shape, a.dtype) for a in arrs],
        in_specs=[hbm] * n + [sem, sem, pl.BlockSpec(memory_space=pl.ANY)], out_specs=[hbm] * n,
        input_output_aliases={i: i for i in range(n)},
        compiler_params=pltpu.CompilerParams(has_side_effects=pltpu.SideEffectType.DATAFLOW_SIDE_EFFECTING),
    )(*arrs, send_sems, recv_sems, after)


def _gather_ici_plan(arrs, count_only):
    if count_only:
        return [None] * (3 * len(arrs))
    x, y, c = _coords()
    pushes = []
    for a in arrs:
        mine = a.at[2 * x + y, c]
        pushes += [(mine, mine, peer) for peer, _ in _other_chips(x, y, c)]
    return pushes


def _all_to_all_plan(arrs, count_only):
    half = len(arrs) // 2
    if count_only:
        return [None] * (7 * half)
    x, y, c = _coords()
    pushes = []
    for src, land in zip(arrs[:half], arrs[half:]):
        for flips in range(1, 8):
            px, py, pc = (1 - x if flips & 4 else x), (1 - y if flips & 2 else y), (1 - c if flips & 1 else c)
            pushes.append((src.at[2 * px + py, pc], land.at[flips - 1], (px, py, pc)))
    return pushes


def forward_to_sibling(bufs, name):
    n = len(bufs)

    def plan(ins, outs, count_only):
        if count_only:
            return [None] * (3 * n)
        x, y, c = _coords()
        pushes = []
        for i in range(n):
            for _, src_shard in _other_chips(x, y, c):
                slab = outs[i].at[src_shard, c]
                pushes.append((slab, slab, (x, y, 1 - c)))
        return pushes

    shapes = [jax.ShapeDtypeStruct(b.shape, b.dtype) for b in bufs]
    return _exchange(name, bufs, shapes, plan, in_place=True)


def _other_chips(x, y, c):
    return [((1 - x, y, c), 2 * (1 - x) + y), ((x, 1 - y, c), 2 * x + 1 - y),
            ((1 - x, 1 - y, c), 2 * (1 - x) + 1 - y)]


def join_halves(bufs, everywhere, name):
    slots = [(i, l) for i, b in enumerate(bufs) for l in range(b.shape[0])]
    n = len(bufs)

    def plan(ins, outs, count_only):
        if count_only:
            return [None] * (len(slots) + 7)
        x, y, c = _coords()
        pushes = [(outs[i].at[l, c], outs[i].at[l, c], (x, y, 1 - c)) for i, l in slots]
        mine = outs[n].at[2 * x + y, c]
        for flips in range(1, 8):
            peer = (1 - x if flips & 4 else x, 1 - y if flips & 2 else y, 1 - c if flips & 1 else c)
            pushes.append((mine, mine, peer))
        return pushes

    arrs = list(bufs) + [everywhere]
    shapes = [jax.ShapeDtypeStruct(b.shape, b.dtype) for b in arrs]
    return _exchange(name, arrs, shapes, plan, in_place=True)


def _pack(arrays, multiple):
    flat = jnp.concatenate([a.reshape(-1) for a in arrays])
    n = flat.shape[0]
    return jnp.pad(flat, (0, _round_up(n, multiple) - n))


def _unpack(flat, shapes):
    out, pos = [], 0
    for shp in shapes:
        n = math.prod(shp)
        out.append(flat[pos:pos + n].reshape(shp))
        pos += n
    return out


def _block_diag(w, per_group):
    nb, bs, _ = w.shape
    g = nb // per_group
    w4 = w.reshape(g, per_group, bs, bs)
    eye = jnp.eye(per_group, dtype=w.dtype)
    full = w4[:, :, :, None, :] * eye[None, :, None, :, None]
    return full.reshape(g, per_group * bs, per_group * bs).astype(BF16)


def _block_diag_extract(full, per_group, bs):
    g = full.shape[0]
    f5 = full.reshape(g, per_group, bs, per_group, bs)
    idx = jnp.arange(per_group)
    picked = f5[:, idx, :, idx, :]
    return jnp.moveaxis(picked, 0, 1).reshape(g * per_group, bs, bs)


def kernel(x, meta, a_w_in, a_conv_w, a_conv_b, a_w_r, a_b_r, a_w_i, a_b_i, a_lambda, a_w_out, kv_w, kv_f_b, b_w_in, b_w_out, f_w_in, f_conv_w, f_conv_b, f_w_out, ln1_g, ln1_b, ln2_g, ln2_b, loss_target, m_meta, m_a_w_in, m_a_conv_w, m_a_conv_b, m_a_w_r, m_a_b_r, m_a_w_i, m_a_b_i, m_a_lambda, m_a_w_out, m_kv_w, m_kv_f_b, m_b_w_in, m_b_w_out, m_f_w_in, m_f_conv_w, m_f_conv_b, m_f_w_out, m_ln1_g, m_ln1_b, m_ln2_g, m_ln2_b, v_meta, v_a_w_in, v_a_conv_w, v_a_conv_b, v_a_w_r, v_a_b_r, v_a_w_i, v_a_b_i, v_a_lambda, v_a_w_out, v_kv_w, v_kv_f_b, v_b_w_in, v_b_w_out, v_f_w_in, v_f_conv_w, v_f_conv_b, v_f_w_out, v_ln1_g, v_ln1_b, v_ln2_g, v_ln2_b):
    weights = dict(meta=meta, a_w_in=a_w_in, a_conv_w=a_conv_w, a_conv_b=a_conv_b, a_w_r=a_w_r, a_b_r=a_b_r,
                   a_w_i=a_w_i, a_b_i=a_b_i, a_lambda=a_lambda, a_w_out=a_w_out, kv_w=kv_w, kv_f_b=kv_f_b,
                   b_w_in=b_w_in, b_w_out=b_w_out, f_w_in=f_w_in, f_conv_w=f_conv_w, f_conv_b=f_conv_b,
                   f_w_out=f_w_out, ln1_g=ln1_g, ln1_b=ln1_b, ln2_g=ln2_g, ln2_b=ln2_b)
    mom_m = dict(meta=m_meta, a_w_in=m_a_w_in, a_conv_w=m_a_conv_w, a_conv_b=m_a_conv_b, a_w_r=m_a_w_r,
                 a_b_r=m_a_b_r, a_w_i=m_a_w_i, a_b_i=m_a_b_i, a_lambda=m_a_lambda, a_w_out=m_a_w_out,
                 kv_w=m_kv_w, kv_f_b=m_kv_f_b, b_w_in=m_b_w_in, b_w_out=m_b_w_out, f_w_in=m_f_w_in,
                 f_conv_w=m_f_conv_w, f_conv_b=m_f_conv_b, f_w_out=m_f_w_out, ln1_g=m_ln1_g, ln1_b=m_ln1_b,
                 ln2_g=m_ln2_g, ln2_b=m_ln2_b)
    mom_v = dict(meta=v_meta, a_w_in=v_a_w_in, a_conv_w=v_a_conv_w, a_conv_b=v_a_conv_b, a_w_r=v_a_w_r,
                 a_b_r=v_a_b_r, a_w_i=v_a_w_i, a_b_i=v_a_b_i, a_lambda=v_a_lambda, a_w_out=v_a_w_out,
                 kv_w=v_kv_w, kv_f_b=v_kv_f_b, b_w_in=v_b_w_in, b_w_out=v_b_w_out, f_w_in=v_f_w_in,
                 f_conv_w=v_f_conv_w, f_conv_b=v_f_conv_b, f_w_out=v_f_w_out, ln1_g=v_ln1_g, ln1_b=v_ln1_b,
                 ln2_g=v_ln2_g, ln2_b=v_ln2_b)
    return _train_step(x, loss_target, weights, mom_m, mom_v)


WEIGHT_ORDER = ("meta", "a_w_in", "a_conv_w", "a_conv_b", "a_w_r", "a_b_r", "a_w_i", "a_b_i", "a_lambda",
                "a_w_out", "kv_w", "kv_f_b", "b_w_in", "b_w_out", "f_w_in", "f_conv_w", "f_conv_b",
                "f_w_out", "ln1_g", "ln1_b", "ln2_g", "ln2_b")
BIG = ("a_w_in", "a_w_out", "kv_w", "b_w_in", "b_w_out", "f_w_in", "f_w_out")
OUT_TYPE = ("a_w_out", "b_w_out", "f_w_out")
SMALL_SHARDED = (("meta", 1), ("a_conv_w", 2), ("a_conv_b", 1), ("a_b_r", 1), ("a_b_i", 1), ("a_lambda", 1),
                 ("f_conv_w", 2))
SMALL_REPLICATED = ("a_w_r", "a_w_i", "kv_f_b", "f_conv_b", "ln1_g", "ln1_b", "ln2_g", "ln2_b")


def _train_step(x, loss_target, weights, mom_m, mom_v):
    S = N_SHARDS
    seq, d = x.shape[1], x.shape[2]
    nm = weights["meta"].shape[0]
    la = weights["a_w_in"].shape[0]
    lb = weights["b_w_in"].shape[0]
    depth = la + lb
    dr = weights["a_w_out"].shape[1] * S
    nb, bs = weights["a_w_r"].shape[1], weights["a_w_r"].shape[2]
    per_group = (LANES // math.gcd(bs, LANES))
    gs = per_group * bs
    heads = weights["kv_f_b"].shape[0]
    dff = weights["f_w_out"].shape[1] * S
    nkv = 2 * d + heads
    nkv_s = weights["kv_w"].shape[1]
    nkvp = _round_up(2 * d + LANES, 768) if 2 * d + LANES > 768 else 2 * d + LANES
    alpha = (2 * depth) ** 0.25
    xi, yi, ci = _coords()
    shard = 2 * xi + yi
    shard_arr = jnp.reshape(shard, (1,)).astype(jnp.int32)
    shard_core_arr = jnp.stack([shard, ci]).astype(jnp.int32)

    def mixer_keys(l):
        if l < la:
            return [("a_w_in", l), ("a_w_out", l)]
        return ([("kv_w", 0)] if l == la else []) + [("b_w_in", l - la), ("b_w_out", l - la)]

    def ffn_keys(l):
        return [("f_w_in", l), ("f_w_out", l)]

    assert la >= 1
    groups = [mixer_keys(0)[:1], mixer_keys(0)[1:] + ffn_keys(0)]
    groups += [mixer_keys(l) + ffn_keys(l) for l in range(1, depth)]
    keys = [kl for grp in groups for kl in grp]
    shape2d = {(k, i): weights[k].shape[-2:] for k, i in keys}
    small_local = [weights[k] for k, _ in SMALL_SHARDED]
    sm_flat = _pack(small_local, 2 * SUBLANES * LANES).reshape(1, 2, -1, LANES)
    sm_slot = lax.dynamic_update_slice_in_dim(lax.empty((S,) + sm_flat.shape[1:], F32), sm_flat, shard, axis=0)
    parts = [[cast_into_slot(shard_arr, weights[k], i, f"cast_{k}{i}") for k, i in grp] for grp in groups]
    parts[0].append(sm_slot)
    in_flight, start_token = _split_start("gather_start", parts, lambda g, refs, cnt: _gather_ici_plan(refs, cnt))
    gw = {}

    def fetch(g, after):
        arrs = _split_wait(f"gather_wait_{g}", in_flight[g], after, _gather_ici_plan)
        arrs = forward_to_sibling(arrs, f"gather_fwd_{g}")
        for kl, a in zip(groups[g], arrs):
            rows, cols = shape2d[kl]
            gw[kl] = a.reshape(S * rows, cols) if kl[0] in OUT_TYPE else a.reshape(S, rows, cols)
        return arrs

    sm_all = fetch(0, start_token)[-1].reshape(S, -1)
    small_full = {}
    per_shard = [_unpack(sm_all[s], [a.shape for a in small_local]) for s in range(S)]
    for idx, (k, axis) in enumerate(SMALL_SHARDED):
        small_full[k] = jnp.concatenate([per_shard[s][idx] for s in range(S)], axis=axis)
    fb_pad = jnp.pad(weights["kv_f_b"], (0, LANES - heads)).reshape(1, LANES)
    wr_g = [_block_diag(weights["a_w_r"][l], per_group) for l in range(la)]
    wi_g = [_block_diag(weights["a_w_i"][l], per_group) for l in range(la)]
    row = lambda v: v.reshape(1, -1)

    h, hb = embed_fwd(small_full["meta"], x[0], "embed")
    saved = []
    kvz = ct_pad = None
    _, tp, _ = _attn_geometry(nm + seq)
    t = nm + seq
    for l in range(depth):
        sv = {"hb_in": hb}
        if l > 0:
            fetch(l + 1, hb)
        if l < la:
            gr = mm_in(hb, gw[("a_w_in", l)], F32, f"a{l}_in")
            rc, rcb = a_conv_fwd(gr, small_full["a_conv_w"][l], row(small_full["a_conv_b"][l]), f"a{l}_conv")
            r_pre, i_pre = mm_bd(rcb, wr_g[l], wi_g[l], f"a{l}_gates")
            hs, gb = a_elem_fwd(gr, rc, r_pre, i_pre, row(small_full["a_b_r"][l]), row(small_full["a_b_i"][l]),
                                row(small_full["a_lambda"][l]), f"a{l}_lru")
            if l == 0:
                fetch(1, gb)
            mixed, w_mix = gb, gw[("a_w_out", l)]
            sv.update(gr=gr, rc=rc, rcb=rcb, r_pre=r_pre, i_pre=i_pre, hs=hs, gb=gb)
        else:
            j = l - la
            if j == 0:
                kv_cat = jnp.moveaxis(gw[("kv_w", 0)], 0, 1).reshape(d, S * nkv_s)
                kv_pad = jnp.pad(kv_cat, ((0, 0), (0, nkvp - nkv))).reshape(1, d, nkvp)
                kvz = mm_in(hb, kv_pad, F32, "kv_proj")
                cum = kv_fwd(kvz, fb_pad, d, "kv_forget")
                ct_pad = jnp.pad(cum[:, :heads].T, ((0, 0), (0, tp - t))).reshape(heads, 1, tp)
                kv_hb = hb
            qg = mm_in(hb, gw[("b_w_in", j)], F32, f"b{j}_in")
            o, mob = attn_fwd(qg, kvz, ct_pad, d, heads, f"b{j}_attn")
            mixed, w_mix = mob, gw[("b_w_out", j)]
            sv.update(qg=qg, o=o, mob=mob)
        h1, h1b, xh1, rs1 = mm_out_ln(mixed, w_mix, h, row(weights["ln1_g"][l]), row(weights["ln1_b"][l]), alpha,
                                      f"mix{l}_out_ln1")
        zf = mm_in(h1b, gw[("f_w_in", l)], F32, f"f{l}_in")
        ffb = f_elem_fwd(zf, small_full["f_conv_w"][l], row(weights["f_conv_b"][l]), f"f{l}_act")
        h2, h2b, xh2, rs2 = mm_out_ln(ffb, gw[("f_w_out", l)], h1, row(weights["ln2_g"][l]), row(weights["ln2_b"][l]),
                                      alpha, f"f{l}_out_ln2")
        sv.update(h1b=h1b, xh1=xh1, rs1=rs1, zf=zf, ffb=ffb, xh2=xh2, rs2=rs2)
        saved.append(sv)
        h, hb = h2, h2b
    loss11, dy = loss_fwd_bwd(h, loss_target[0], nm, "loss")

    grads = {}

    def by_owner(kl, g3):
        rows, cols = shape2d[kl]
        grads[kl] = g3.reshape(S, 2, rows // 2, cols)

    reducing = []

    def send_grads(g, names, arrays):
        lands = [lax.empty((7,) + a.shape[2:], a.dtype) for a in arrays]
        started, token = _split_start(f"grad_a2a_start_{g}", [list(arrays) + lands],
                                      lambda _, refs, cnt: _all_to_all_plan(refs, cnt))
        reducing.append((names, started[0]))
        return token[0:1, 0:1]

    def after_start(vec, zero):
        return vec if zero is None else vec + zero

    pin = None

    g_small = {}
    per_layer = {k: [None] * n for k, n in (
        ("a_conv_w", la), ("a_conv_b", la), ("a_w_r", la), ("a_b_r", la), ("a_w_i", la), ("a_b_i", la),
        ("a_lambda", la), ("f_conv_w", depth), ("f_conv_b", depth), ("ln1_g", depth), ("ln1_b", depth),
        ("ln2_g", depth), ("ln2_b", depth))}
    adds = [(dy, 1.0)]
    dks, dvs, dcs = [], [], []
    for l in reversed(range(depth)):
        sv = saved[l]
        ds2, ds2b, dg2, db2 = ln_bwd(adds, sv["xh2"], sv["rs2"], after_start(row(weights["ln2_g"][l]), pin),
                                     f"ln2_{l}_bwd")
        pin = None
        per_layer["ln2_g"][l], per_layer["ln2_b"][l] = dg2[0], db2[0]
        dff_v = mm_out_nt(ds2b, gw[("f_w_out", l)], f"f{l}_out_dx")
        by_owner(("f_w_out", l), mm_tn(sv["ffb"], [ds2b], 1, f"f{l}_out_dw"))
        dzg, dzv, dwg, dwv, dbg, dbv = f_elem_bwd(sv["zf"], dff_v, small_full["f_conv_w"][l],
                                                  row(weights["f_conv_b"][l]), f"f{l}_act_bwd")
        per_layer["f_conv_w"][l] = jnp.concatenate([dwg, dwv], axis=1)
        per_layer["f_conv_b"][l] = jnp.concatenate([dbg, dbv], axis=1)[0]
        dh1_f = mm_in_nt([dzg, dzv], gw[("f_w_in", l)], f"f{l}_in_dx")
        by_owner(("f_w_in", l), mm_tn(sv["h1b"], [dzg, dzv], S, f"f{l}_in_dw"))
        ds1, ds1b, dg1, db1 = ln_bwd([(ds2, alpha), (dh1_f, 1.0)], sv["xh1"], sv["rs1"],
                                     row(weights["ln1_g"][l]), f"ln1_{l}_bwd")
        per_layer["ln1_g"][l], per_layer["ln1_b"][l] = dg1[0], db1[0]
        if l < la:
            dgv = mm_out_nt(ds1b, gw[("a_w_out", l)], f"a{l}_out_dx")
            by_owner(("a_w_out", l), mm_tn(sv["gb"], [ds1b], 1, f"a{l}_out_dw"))
            if l == 0:
                pin = send_grads(1, groups[1], [grads[kl] for kl in groups[1]])
            dgate_b, drp_b, dip_b, drc_d, dlam, dbr, dbi = a_elem_bwd(
                dgv, sv["gr"], sv["rc"], sv["r_pre"], sv["i_pre"], sv["hs"], row(small_full["a_b_r"][l]),
                row(small_full["a_b_i"][l]), after_start(row(small_full["a_lambda"][l]), pin), f"a{l}_lru_bwd")
            pin = None
            drc_g = mm_bd_nt(drp_b, dip_b, wr_g[l], wi_g[l], f"a{l}_gates_dx")
            dwr_g, dwi_g = mm_bd_tn(sv["rcb"], drp_b, dip_b, gs, f"a{l}_gates_dw")
            dgr_b, dcw, dcb = a_conv_bwd(drc_d, drc_g, sv["gr"], small_full["a_conv_w"][l], dgate_b,
                                         f"a{l}_conv_bwd")
            per_layer["a_w_r"][l] = _block_diag_extract(dwr_g, per_group, bs)
            per_layer["a_w_i"][l] = _block_diag_extract(dwi_g, per_group, bs)
            per_layer["a_lambda"][l], per_layer["a_b_r"][l], per_layer["a_b_i"][l] = dlam[0], dbr[0], dbi[0]
            per_layer["a_conv_w"][l], per_layer["a_conv_b"][l] = dcw, dcb[0]
            dh_m = mm_in_nt([dgr_b], gw[("a_w_in", l)], f"a{l}_in_dx")
            by_owner(("a_w_in", l), mm_tn(sv["hb_in"], [dgr_b], S, f"a{l}_in_dw"))
        else:
            j = l - la
            dmo = mm_out_nt(ds1b, gw[("b_w_out", j)], f"b{j}_out_dx")
            by_owner(("b_w_out", j), mm_tn(sv["mob"], [ds1b], 1, f"b{j}_out_dw"))
            dq_b, dog_b, dk, dv, dct = attn_bwd(dmo, sv["qg"], kvz, sv["o"], ct_pad, d, heads, f"b{j}_attn_bwd")
            dks.append(dk)
            dvs.append(dv)
            dcs.append(jnp.pad(dct[:, 0, :t].T, ((0, 0), (0, LANES - heads))))
            dh_m = mm_in_nt([dq_b, dog_b], gw[("b_w_in", j)], f"b{j}_in_dx")
            by_owner(("b_w_in", j), mm_tn(sv["hb_in"], [dq_b, dog_b], S, f"b{j}_in_dw"))
        adds = [(ds1, alpha), (dh_m, 1.0)]
        if l == la:
            dzf_b, dfb = kv_bwd(dcs, kvz, fb_pad, d, "kv_forget_bwd")
            dk_b = add_cast(dks[0], dks[1], "kv_dk") if lb == 2 else None
            dv_b = add_cast(dvs[0], dvs[1], "kv_dv") if lb == 2 else None
            dz_kv = jnp.concatenate([dk_b, dv_b, dzf_b, jnp.zeros((t, nkvp - 2 * d - LANES), BF16)], axis=1)
            dh_kv = mm_in_nt([dz_kv], kv_pad, "kv_proj_dx")
            kv_dw = mm_tn(kv_hb, [dz_kv], 1, "kv_proj_dw")
            by_owner(("kv_w", 0), jnp.moveaxis(kv_dw[0, :, :nkv].reshape(d, S, nkv_s), 1, 0))
            g_small["kv_f_b"] = dfb[0, :heads]
            adds.append((dh_kv, 1.0))
        if l > 0:
            pin = send_grads(l + 1, groups[l + 1], [grads[kl] for kl in groups[l + 1]])
    g_meta, g_x = embed_bwd(adds, nm, "embed_bwd")

    small_names = list(SMALL_REPLICATED) + [k for k, _ in SMALL_SHARDED]
    g_small["meta"] = g_meta
    for k, vals in per_layer.items():
        g_small[k] = jnp.stack(vals)
    small_shapes = {k: (weights[k].shape if k in SMALL_REPLICATED else g_small[k].shape) for k in small_names}
    sm_g = _pack([g_small[k].reshape(small_shapes[k]) for k in small_names], S * 2 * SUBLANES * LANES)
    sm_g = sm_g.reshape(S, 2, -1, LANES)
    send_grads(0, groups[0] + [("small", 0)], [grads[kl] for kl in groups[0]] + [sm_g])

    fin = {"small": lax.empty(sm_g.shape, F32)}
    for kl in keys:
        n_stack = weights[kl[0]].shape[0] if weights[kl[0]].ndim == 3 else 1
        rows, cols = shape2d[kl]
        fin.setdefault(kl[0], lax.empty((n_stack, 2, rows // 2, cols), F32))
    for g, (names_g, started) in enumerate(reducing):
        arrs = _split_wait(f"grad_a2a_wait_{g}", started, g_x, _all_to_all_plan)
        half = len(names_g)
        for i, (kl, cs, rv) in enumerate(zip(names_g, arrs[:half], arrs[half:])):
            fin[kl[0]] = owner_sum(shard_core_arr, cs, rv, fin[kl[0]], None if kl[0] == "small" else kl[1],
                                   f"owner_sum_{g}_{i}")
    names = list(BIG) + ["small"]
    joined = dict(zip(names, join_halves([fin[k] for k in BIG], fin["small"], "grad_join")))
    sm_red = joined["small"].reshape(-1)

    out_g, out_d, out_m, out_v = {}, {}, {}, {}
    for k in BIG:
        w2 = weights[k].reshape(-1, weights[k].shape[-1])
        g2 = joined[k].reshape(w2.shape)
        dlt, mn, vn, g_out = adamw(w2, g2, mom_m[k].reshape(w2.shape), mom_v[k].reshape(w2.shape), "adamw_" + k,
                                   emit_g=True)
        shp = weights[k].shape
        out_g[k], out_d[k], out_m[k], out_v[k] = g_out.reshape(shp), dlt.reshape(shp), mn.reshape(shp), vn.reshape(shp)
    sm_vals = dict(zip(small_names, _unpack(sm_red, [small_shapes[k] for k in small_names])))
    local_small = {}
    for k in SMALL_REPLICATED:
        local_small[k] = sm_vals[k]
    for k, axis in SMALL_SHARDED:
        size = weights[k].shape[axis]
        local_small[k] = lax.dynamic_slice_in_dim(sm_vals[k], shard * size, size, axis=axis)
    for k in small_names:
        shp = weights[k].shape
        two_d = (-1, shp[-1]) if len(shp) > 1 else (1, -1)
        dlt, mn, vn = adamw(weights[k].reshape(two_d), local_small[k].reshape(two_d), mom_m[k].reshape(two_d),
                            mom_v[k].reshape(two_d), "adamw_" + k)
        out_g[k], out_d[k], out_m[k], out_v[k] = local_small[k], dlt.reshape(shp), mn.reshape(shp), vn.reshape(shp)

    loss = lax.psum(loss11[0, 0], ("x", "y", "c"))
    return (loss, g_x[None], *[out_g[k] for k in WEIGHT_ORDER], *[out_d[k] for k in WEIGHT_ORDER],
            *[out_m[k] for k in WEIGHT_ORDER], *[out_v[k] for k in WEIGHT_ORDER])
```

```python
import functools
import math

import jax
import jax.numpy as jnp
from jax import lax
from jax.experimental import pallas as pl
from jax.experimental.pallas import tpu as pltpu

F32 = jnp.float32
BF16 = jnp.bfloat16

LRU_C = 8.0
LN_EPS = 1e-5
ADAM_LR = 0.001
ADAM_B1 = 0.9
ADAM_B2 = 0.999
ADAM_EPS = 1e-08
ADAM_WD = 0.01
ADAM_STEP = 10

LANES = 128
SUBLANES = 8
V7X_VMEM_BYTES = 64 * 1024 * 1024
VMEM_LIMIT = V7X_VMEM_BYTES * 7 // 8
N_SHARDS = 4
GELU_C0 = math.sqrt(2.0 / math.pi)
GELU_C1 = 0.044715
NEG_BIG = -1e30


def _cp(*sem):
    return pltpu.CompilerParams(dimension_semantics=tuple(sem), vmem_limit_bytes=VMEM_LIMIT)


def _tile(n, cap, mult=LANES):
    best = None
    d = mult
    while d <= min(n, cap):
        if n % d == 0:
            best = d
        d += mult
    return n if best is None else best


def _row_block(t):
    if t % 3 == 0 and (t // 3) % 16 == 0:
        return t // 3
    return t


def _round_up(n, m):
    return (n + m - 1) // m * m


def _sigmoid(v):
    return 1.0 / (1.0 + jnp.exp(-v))


def _softplus(v):
    return jnp.maximum(v, 0.0) + jnp.log(1.0 + jnp.exp(-jnp.abs(v)))


def _gelu_parts(v):
    v2 = v * v
    u = GELU_C0 * (v + GELU_C1 * v * v2)
    t = jnp.tanh(u)
    g = 0.5 * v * (1.0 + t)
    dg = 0.5 * (1.0 + t) + 0.5 * v * (1.0 - t * t) * (GELU_C0 * (1.0 + 3.0 * GELU_C1 * v2))
    return g, dg


def _gelu(v):
    u = GELU_C0 * (v + GELU_C1 * v * v * v)
    return 0.5 * v * (1.0 + jnp.tanh(u))


def _neg_expm1(v):
    series = -v * (1.0 + 0.5 * v * (1.0 + (v / 3.0) * (1.0 + 0.25 * v)))
    return jnp.where(v > -0.05, series, 1.0 - jnp.exp(v))


def _shift_down(v, j):
    if j == 0:
        return v
    rows = lax.broadcasted_iota(jnp.int32, v.shape, 0)
    return jnp.where(rows >= j, pltpu.roll(v, j, 0), 0.0)


def _shift_up(v, j):
    if j == 0:
        return v
    n = v.shape[0]
    rows = lax.broadcasted_iota(jnp.int32, v.shape, 0)
    return jnp.where(rows < n - j, pltpu.roll(v, n - j, 0), 0.0)


def _scan_rows(a_ref, b_ref, out_ref, n_rows, width, reverse):
    n_groups = n_rows // SUBLANES
    rows = lax.broadcasted_iota(jnp.int32, (SUBLANES, width), 0)
    edge = 0 if reverse else SUBLANES - 1

    def body(g, carry):
        grp = (n_groups - 1 - g) if reverse else g
        off = pl.multiple_of(grp * SUBLANES, SUBLANES)
        b = b_ref[pl.ds(off, SUBLANES), :]
        a = None if a_ref is None else a_ref[pl.ds(off, SUBLANES), :]
        for d in (1, 2, 4):
            if reverse:
                keep = rows < SUBLANES - d
                sh = SUBLANES - d
            else:
                keep = rows >= d
                sh = d
            b_s = jnp.where(keep, pltpu.roll(b, sh, 0), 0.0)
            if a is None:
                b = b + b_s
            else:
                a_s = jnp.where(keep, pltpu.roll(a, sh, 0), 1.0)
                b = a * b_s + b
                a = a * a_s
        h = b + carry if a is None else b + a * carry
        out_ref[pl.ds(off, SUBLANES), :] = h
        return jnp.sum(jnp.where(rows == edge, h, 0.0), axis=0, keepdims=True)

    lax.fori_loop(0, n_groups, body, jnp.zeros((1, width), F32), unroll=2)


def mm_in(x, w, out_dtype, name):
    t, k = x.shape
    s_n, _, ns = w.shape
    tn = _tile(ns, 1408)
    nj = ns // tn
    rb = _row_block(t)

    def body(x_ref, w_ref, o_ref):
        o_ref[...] = jnp.dot(x_ref[...], w_ref[...], preferred_element_type=F32).astype(o_ref.dtype)

    return pl.pallas_call(
        body, name=name, grid=(s_n, nj, t // rb),
        in_specs=[pl.BlockSpec((rb, k), lambda s, j, r: (r, 0)),
                  pl.BlockSpec((None, k, tn), lambda s, j, r: (s, 0, j))],
        out_specs=pl.BlockSpec((rb, tn), lambda s, j, r: (r, s * nj + j)),
        out_shape=jax.ShapeDtypeStruct((t, s_n * ns), out_dtype),
        compiler_params=_cp("parallel", "parallel", "parallel"))(x, w)


def _part_map(p, per_part, nj, lead):
    def index(*grid):
        s, j = grid[-2], grid[-1]
        mine = s // per_part == p
        col = jnp.where(mine, (s - p * per_part) * nj + j, 0)
        return (grid[0], col) if lead else (0, col)
    return index


def mm_in_nt(dy_parts, w, name):
    n_parts = len(dy_parts)
    t = dy_parts[0].shape[0]
    s_n, k, ns = w.shape
    per_part = s_n // n_parts
    tn = _tile(ns, 1408)
    nj = ns // tn
    rb = _row_block(t)

    def body(*refs):
        w_ref, o_ref = refs[n_parts:]

        @pl.when((pl.program_id(1) == 0) & (pl.program_id(2) == 0))
        def _():
            o_ref[...] = jnp.zeros_like(o_ref)

        for p in range(n_parts):
            @pl.when(pl.program_id(1) // per_part == p)
            def _():
                o_ref[...] += lax.dot_general(refs[p][...], w_ref[...], (((1,), (1,)), ((), ())),
                                              preferred_element_type=F32)

    return pl.pallas_call(
        body, name=name, grid=(t // rb, s_n, nj),
        in_specs=[pl.BlockSpec((rb, tn), _part_map(p, per_part, nj, True)) for p in range(n_parts)]
        + [pl.BlockSpec((None, k, tn), lambda r, s, j: (s, 0, j))],
        out_specs=pl.BlockSpec((rb, k), lambda r, s, j: (r, 0)),
        out_shape=jax.ShapeDtypeStruct((t, k), F32),
        compiler_params=_cp("parallel", "arbitrary", "arbitrary"))(*dy_parts, w)


def mm_out_nt(dy, w, name):
    t, n = dy.shape
    k = w.shape[0]
    rb = _row_block(t)

    def body(dy_ref, w_ref, o_ref):
        o_ref[...] = lax.dot_general(dy_ref[...], w_ref[...], (((1,), (1,)), ((), ())),
                                     preferred_element_type=F32)

    return pl.pallas_call(
        body, name=name, grid=(t // rb,),
        in_specs=[pl.BlockSpec((rb, n), lambda r: (r, 0)), pl.BlockSpec((k, n), lambda r: (0, 0))],
        out_specs=pl.BlockSpec((rb, k), lambda r: (r, 0)),
        out_shape=jax.ShapeDtypeStruct((t, k), F32),
        compiler_params=_cp("parallel"))(dy, w)


def mm_tn(x, dy_parts, s_n, name):
    n_parts = len(dy_parts)
    t, kb = x.shape
    nb = dy_parts[0].shape[1] * n_parts // s_n
    per_part = max(s_n // n_parts, 1)
    tk = _tile(kb, 1408)
    tn = _tile(nb, 1408)
    nkb, nnb = kb // tk, nb // tn
    tn_dims = (((0,), (0,)), ((), ()))

    def body(*refs):
        x_ref, o_ref = refs[0], refs[-1]
        for p in range(n_parts):
            @pl.when(pl.program_id(1) // per_part == p)
            def _():
                o_ref[...] = lax.dot_general(x_ref[...], refs[1 + p][...], tn_dims,
                                             preferred_element_type=F32).astype(o_ref.dtype)

    return pl.pallas_call(
        body, name=name, grid=(nkb, s_n, nnb),
        in_specs=[pl.BlockSpec((t, tk), lambda a, s, b: (0, a))]
        + [pl.BlockSpec((t, tn), _part_map(p, per_part, nnb, False)) for p in range(n_parts)],
        out_specs=pl.BlockSpec((None, tk, tn), lambda a, s, b: (s, a, b)),
        out_shape=jax.ShapeDtypeStruct((s_n, kb, nb), BF16),
        compiler_params=_cp("parallel", "parallel", "parallel"))(x, *dy_parts)


def mm_bd(x, wr, wi, name):
    t, _ = x.shape
    g_n, gs, _ = wr.shape
    rb = _row_block(t)

    def body(x_ref, wr_ref, wi_ref, r_ref, i_ref):
        xv = x_ref[...]
        r_ref[...] = jnp.dot(xv, wr_ref[...], preferred_element_type=F32)
        i_ref[...] = jnp.dot(xv, wi_ref[...], preferred_element_type=F32)

    blk = pl.BlockSpec((rb, gs), lambda g, r: (r, g))
    wspec = pl.BlockSpec((None, gs, gs), lambda g, r: (g, 0, 0))
    return pl.pallas_call(
        body, name=name, grid=(g_n, t // rb), in_specs=[blk, wspec, wspec], out_specs=[blk, blk],
        out_shape=[jax.ShapeDtypeStruct((t, g_n * gs), F32)] * 2,
        compiler_params=_cp("parallel", "parallel"))(x, wr, wi)


def mm_bd_nt(dr, di, wr, wi, name):
    t, _ = dr.shape
    g_n, gs, _ = wr.shape
    rb = _row_block(t)
    nt = (((1,), (1,)), ((), ()))

    def body(dr_ref, di_ref, wr_ref, wi_ref, o_ref):
        o_ref[...] = (lax.dot_general(dr_ref[...], wr_ref[...], nt, preferred_element_type=F32)
                      + lax.dot_general(di_ref[...], wi_ref[...], nt, preferred_element_type=F32))

    blk = pl.BlockSpec((rb, gs), lambda g, r: (r, g))
    wspec = pl.BlockSpec((None, gs, gs), lambda g, r: (g, 0, 0))
    return pl.pallas_call(
        body, name=name, grid=(g_n, t // rb), in_specs=[blk, blk, wspec, wspec], out_specs=blk,
        out_shape=jax.ShapeDtypeStruct((t, g_n * gs), F32),
        compiler_params=_cp("parallel", "parallel"))(dr, di, wr, wi)


def mm_bd_tn(x, dr, di, gs, name):
    t, w = x.shape
    g_n = w // gs
    tn_dims = (((0,), (0,)), ((), ()))

    def body(x_ref, dr_ref, di_ref, gr_ref, gi_ref):
        xv = x_ref[...]
        gr_ref[...] = lax.dot_general(xv, dr_ref[...], tn_dims, preferred_element_type=F32)
        gi_ref[...] = lax.dot_general(xv, di_ref[...], tn_dims, preferred_element_type=F32)

    blk = pl.BlockSpec((t, gs), lambda g: (0, g))
    ospec = pl.BlockSpec((None, gs, gs), lambda g: (g, 0, 0))
    return pl.pallas_call(
        body, name=name, grid=(g_n,), in_specs=[blk, blk, blk], out_specs=[ospec, ospec],
        out_shape=[jax.ShapeDtypeStruct((g_n, gs, gs), F32)] * 2,
        compiler_params=_cp("parallel"))(x, dr, di)


def embed_fwd(meta, x2d, name):
    nm, d = meta.shape
    seq = x2d.shape[0]
    t = nm + seq
    cb = _tile(d, 256)

    def body(m_ref, x_ref, h_ref, hb_ref):
        h_ref[pl.ds(0, nm), :] = m_ref[...]
        h_ref[pl.ds(nm, seq), :] = x_ref[...]
        hb_ref[pl.ds(0, nm), :] = m_ref[...].astype(BF16)
        hb_ref[pl.ds(nm, seq), :] = x_ref[...].astype(BF16)

    return pl.pallas_call(
        body, name=name, grid=(d // cb,),
        in_specs=[pl.BlockSpec((nm, cb), lambda j: (0, j)), pl.BlockSpec((seq, cb), lambda j: (0, j))],
        out_specs=[pl.BlockSpec((t, cb), lambda j: (0, j))] * 2,
        out_shape=[jax.ShapeDtypeStruct((t, d), F32), jax.ShapeDtypeStruct((t, d), BF16)],
        compiler_params=_cp("parallel"))(meta, x2d)


def embed_bwd(adds, nm, name):
    t, d = adds[0][0].shape
    seq = t - nm
    cb = _tile(d, 256)
    scales = [s for _, s in adds]
    n = len(adds)

    def body(*refs):
        tot = None
        for r, sc in zip(refs[:n], scales):
            term = r[...] if sc == 1.0 else sc * r[...]
            tot = term if tot is None else tot + term
        gm_ref, gx_ref = refs[n], refs[n + 1]
        gm_ref[...] = tot[0:nm]
        gx_ref[...] = tot[nm:t]

    return pl.pallas_call(
        body, name=name, grid=(d // cb,),
        in_specs=[pl.BlockSpec((t, cb), lambda j: (0, j))] * n,
        out_specs=[pl.BlockSpec((nm, cb), lambda j: (0, j)), pl.BlockSpec((seq, cb), lambda j: (0, j))],
        out_shape=[jax.ShapeDtypeStruct((nm, d), F32), jax.ShapeDtypeStruct((seq, d), F32)],
        compiler_params=_cp("parallel"))(*[a for a, _ in adds])


def loss_fwd_bwd(h, tgt, nm, name):
    t, d = h.shape
    seq = t - nm
    cb = _tile(d, 256)
    inv_d = 1.0 / d

    def body(h_ref, t_ref, loss_ref, dy_ref):
        @pl.when(pl.program_id(0) == 0)
        def _():
            loss_ref[...] = jnp.zeros_like(loss_ref)
        err = h_ref[pl.ds(nm, seq), :] - t_ref[...]
        dy_ref[pl.ds(0, nm), :] = jnp.zeros((nm, cb), F32)
        dy_ref[pl.ds(nm, seq), :] = err * inv_d
        loss_ref[...] += (0.5 * inv_d) * jnp.sum(err * err, keepdims=True)

    return pl.pallas_call(
        body, name=name, grid=(d // cb,),
        in_specs=[pl.BlockSpec((t, cb), lambda j: (0, j)), pl.BlockSpec((seq, cb), lambda j: (0, j))],
        out_specs=[pl.BlockSpec((1, 1), lambda j: (0, 0)), pl.BlockSpec((t, cb), lambda j: (0, j))],
        out_shape=[jax.ShapeDtypeStruct((1, 1), F32), jax.ShapeDtypeStruct((t, d), F32)],
        compiler_params=_cp("arbitrary"))(h, tgt)


def mm_out_ln(x, w, h, g, b, alpha, name):
    t, d = h.shape
    k = x.shape[1]
    rb = _row_block(t)

    def body(x_ref, w_ref, h_ref, g_ref, b_ref, y_ref, yb_ref, xh_ref, rs_ref):
        s = alpha * h_ref[...] + jnp.dot(x_ref[...], w_ref[...], preferred_element_type=F32)
        mu = jnp.mean(s, axis=-1, keepdims=True)
        c = s - mu
        var = jnp.mean(c * c, axis=-1, keepdims=True)
        rstd = lax.rsqrt(var + LN_EPS)
        xh = c * rstd
        y = xh * g_ref[...] + b_ref[...]
        y_ref[...] = y
        yb_ref[...] = y.astype(BF16)
        xh_ref[...] = xh
        rs_ref[...] = rstd

    row = pl.BlockSpec((rb, d), lambda r: (r, 0))
    vec = pl.BlockSpec((1, d), lambda r: (0, 0))
    return pl.pallas_call(
        body, name=name, grid=(t // rb,),
        in_specs=[pl.BlockSpec((rb, k), lambda r: (r, 0)), pl.BlockSpec((k, d), lambda r: (0, 0)), row, vec, vec],
        out_specs=[row, row, row, pl.BlockSpec((rb, 1), lambda r: (r, 0))],
        out_shape=[jax.ShapeDtypeStruct((t, d), F32), jax.ShapeDtypeStruct((t, d), BF16),
                   jax.ShapeDtypeStruct((t, d), F32), jax.ShapeDtypeStruct((t, 1), F32)],
        compiler_params=_cp("parallel"))(x, w, h, g, b)


def ln_bwd(adds, xhat, rstd, g, name):
    t, d = xhat.shape
    rb = _row_block(t)
    scales = [s for _, s in adds]
    n = len(adds)

    def body(*refs):
        xh_ref, rs_ref, g_ref = refs[n:n + 3]
        ds_ref, dsb_ref, dg_ref, db_ref = refs[n + 3:]
        dy = None
        for r, sc in zip(refs[:n], scales):
            term = r[...] if sc == 1.0 else sc * r[...]
            dy = term if dy is None else dy + term

        @pl.when(pl.program_id(0) == 0)
        def _():
            dg_ref[...] = jnp.zeros_like(dg_ref)
            db_ref[...] = jnp.zeros_like(db_ref)

        xh = xh_ref[...]
        dxh = dy * g_ref[...]
        m1 = jnp.mean(dxh, axis=-1, keepdims=True)
        m2 = jnp.mean(dxh * xh, axis=-1, keepdims=True)
        ds = rs_ref[...] * (dxh - m1 - xh * m2)
        ds_ref[...] = ds
        dsb_ref[...] = ds.astype(BF16)
        dg_ref[...] += jnp.sum(dy * xh, axis=0, keepdims=True)
        db_ref[...] += jnp.sum(dy, axis=0, keepdims=True)

    row = pl.BlockSpec((rb, d), lambda r: (r, 0))
    vec = pl.BlockSpec((1, d), lambda r: (0, 0))
    return pl.pallas_call(
        body, name=name, grid=(t // rb,),
        in_specs=[row] * n + [row, pl.BlockSpec((rb, 1), lambda r: (r, 0)), vec],
        out_specs=[row, row, vec, vec],
        out_shape=[jax.ShapeDtypeStruct((t, d), F32), jax.ShapeDtypeStruct((t, d), BF16),
                   jax.ShapeDtypeStruct((1, d), F32), jax.ShapeDtypeStruct((1, d), F32)],
        compiler_params=_cp("arbitrary"))(*[a for a, _ in adds], xhat, rstd, g)


def _conv_fwd_val(xv, w_ref, b_ref, width):
    y = b_ref[...]
    for j in range(width):
        y = y + _shift_down(xv, j) * w_ref[pl.ds(width - 1 - j, 1), :]
    return y


def _conv_bwd_val(dout, xv, w_ref, width):
    dx = None
    dws = [None] * width
    for j in range(width):
        k = width - 1 - j
        term = _shift_up(dout, j) * w_ref[pl.ds(k, 1), :]
        dx = term if dx is None else dx + term
        dws[k] = jnp.sum(dout * _shift_down(xv, j), axis=0, keepdims=True)
    return dx, dws, jnp.sum(dout, axis=0, keepdims=True)


def a_conv_fwd(gr, cw, cbias, name):
    t, two_dr = gr.shape
    dr = two_dr // 2
    width = cw.shape[0]
    cb = _tile(dr, 256)
    off = dr // cb

    def body(x_ref, w_ref, b_ref, rc_ref, rcb_ref):
        y = _conv_fwd_val(x_ref[...], w_ref, b_ref, width)
        rc_ref[...] = y
        rcb_ref[...] = y.astype(BF16)

    return pl.pallas_call(
        body, name=name, grid=(dr // cb,),
        in_specs=[pl.BlockSpec((t, cb), lambda j: (0, off + j)),
                  pl.BlockSpec((width, cb), lambda j: (0, j)), pl.BlockSpec((1, cb), lambda j: (0, j))],
        out_specs=[pl.BlockSpec((t, cb), lambda j: (0, j))] * 2,
        out_shape=[jax.ShapeDtypeStruct((t, dr), F32), jax.ShapeDtypeStruct((t, dr), BF16)],
        compiler_params=_cp("parallel"))(gr, cw, cbias)


def a_conv_bwd(drc_a, drc_b, gr, cw, dgr, name):
    t, two_dr = gr.shape
    dr = two_dr // 2
    width = cw.shape[0]
    cb = _tile(dr, 256)
    off = dr // cb

    def body(da_ref, db_ref, x_ref, w_ref, dgr_in, dx_ref, dw_ref, dbias_ref):
        del dgr_in
        dout = da_ref[...] + db_ref[...]
        dx, dws, dbias = _conv_bwd_val(dout, x_ref[...], w_ref, width)
        dx_ref[...] = dx.astype(BF16)
        for k in range(width):
            dw_ref[pl.ds(k, 1), :] = dws[k]
        dbias_ref[...] = dbias

    col = pl.BlockSpec((t, cb), lambda j: (0, j))
    return pl.pallas_call(
        body, name=name, grid=(dr // cb,),
        in_specs=[col, col, pl.BlockSpec((t, cb), lambda j: (0, off + j)),
                  pl.BlockSpec((width, cb), lambda j: (0, j)), pl.BlockSpec(memory_space=pl.ANY)],
        out_specs=[pl.BlockSpec((t, cb), lambda j: (0, off + j)), pl.BlockSpec((width, cb), lambda j: (0, j)),
                   pl.BlockSpec((1, cb), lambda j: (0, j))],
        out_shape=[jax.ShapeDtypeStruct((t, two_dr), BF16), jax.ShapeDtypeStruct((width, dr), F32),
                   jax.ShapeDtypeStruct((1, dr), F32)],
        input_output_aliases={4: 0},
        compiler_params=_cp("parallel"))(drc_a, drc_b, gr, cw, dgr)


def _lru_gates(r_pre, i_pre, br, bi, lam):
    r = _sigmoid(r_pre + br)
    i = _sigmoid(i_pre + bi)
    sp = _softplus(-lam)
    la = -LRU_C * r * sp
    a = jnp.exp(la)
    m = jnp.sqrt(_neg_expm1(2.0 * la))
    return r, i, sp, la, a, m


def a_elem_fwd(gr, rc, r_pre, i_pre, br, bi, lam, name):
    t, dr = rc.shape
    cb = _tile(dr, 2 * LANES)
    rb = _row_block(t)
    chunks = [pl.ds(r * rb, rb) for r in range(t // rb)]

    def body(gate_ref, rc_ref, rp_ref, ip_ref, br_ref, bi_ref, lam_ref, hs_ref, g_ref, a_s, u_s):
        for rows in chunks:
            _, i, _, _, a, m = _lru_gates(rp_ref[rows, :], ip_ref[rows, :], br_ref[...], bi_ref[...], lam_ref[...])
            a_s[rows, :] = a
            u_s[rows, :] = m * (i * rc_ref[rows, :])
        _scan_rows(a_s, u_s, hs_ref, t, cb, reverse=False)
        for rows in chunks:
            g_ref[rows, :] = (_gelu(gate_ref[rows, :]) * hs_ref[rows, :]).astype(BF16)

    col = pl.BlockSpec((t, cb), lambda j: (0, j))
    vec = pl.BlockSpec((1, cb), lambda j: (0, j))
    return pl.pallas_call(
        body, name=name, grid=(dr // cb,),
        in_specs=[col, col, col, col, vec, vec, vec],
        out_specs=[col, col],
        out_shape=[jax.ShapeDtypeStruct((t, dr), F32), jax.ShapeDtypeStruct((t, dr), BF16)],
        scratch_shapes=[pltpu.VMEM((t, cb), F32), pltpu.VMEM((t, cb), F32)],
        compiler_params=_cp("parallel"))(gr, rc, r_pre, i_pre, br, bi, lam)


def a_elem_bwd(dg, gr, rc, r_pre, i_pre, hs, br, bi, lam, name):
    t, dr = rc.shape
    cb = _tile(dr, 2 * LANES)
    rb = _row_block(t)
    chunks = [pl.ds(r * rb, rb) for r in range(t // rb)]

    def body(dg_ref, gate_ref, rc_ref, rp_ref, ip_ref, hs_ref, br_ref, bi_ref, lam_ref,
             dgate_ref, dr_ref, di_ref, drc_ref, dlam_ref, dbr_ref, dbi_ref, a_s, b_s, g_s, hp_s):
        lamv = lam_ref[...]
        gates = lambda rows: _lru_gates(rp_ref[rows, :], ip_ref[rows, :], br_ref[...], bi_ref[...], lamv)
        for rows in chunks:
            a_s[rows, :] = gates(rows)[4]
            ge, dge = _gelu_parts(gate_ref[rows, :])
            dgv = dg_ref[rows, :]
            dgate_ref[rows, :] = (dgv * hs_ref[rows, :] * dge).astype(BF16)
            b_s[rows, :] = dgv * ge
        a_s[...] = _shift_up(a_s[...], 1)
        hp_s[...] = _shift_down(hs_ref[...], 1)
        _scan_rows(a_s, b_s, g_s, t, cb, reverse=True)
        dsp = dbr = dbi = jnp.zeros((1, cb), F32)
        for rows in chunks:
            r, i, sp, _, a, m = gates(rows)
            rcv = rc_ref[rows, :]
            gsum = g_s[rows, :]
            da = gsum * hp_s[rows, :]
            dm = gsum * (i * rcv)
            d_i = gsum * m * rcv
            drc_ref[rows, :] = gsum * m * i
            dla = a * da - dm * (a * a) / m
            d_r = (-LRU_C) * sp * dla
            dsp = dsp + jnp.sum((-LRU_C) * r * dla, axis=0, keepdims=True)
            d_rp = d_r * r * (1.0 - r)
            d_ip = d_i * i * (1.0 - i)
            dr_ref[rows, :] = d_rp.astype(BF16)
            di_ref[rows, :] = d_ip.astype(BF16)
            dbr = dbr + jnp.sum(d_rp, axis=0, keepdims=True)
            dbi = dbi + jnp.sum(d_ip, axis=0, keepdims=True)
        dlam_ref[...] = -dsp * _sigmoid(-lamv)
        dbr_ref[...] = dbr
        dbi_ref[...] = dbi

    col = pl.BlockSpec((t, cb), lambda j: (0, j))
    vec = pl.BlockSpec((1, cb), lambda j: (0, j))
    big_b = jax.ShapeDtypeStruct((t, dr), BF16)
    vec_s = jax.ShapeDtypeStruct((1, dr), F32)
    return pl.pallas_call(
        body, name=name, grid=(dr // cb,),
        in_specs=[col, col, col, col, col, col, vec, vec, vec],
        out_specs=[col, col, col, col, vec, vec, vec],
        out_shape=[jax.ShapeDtypeStruct((t, 2 * dr), BF16), big_b, big_b, jax.ShapeDtypeStruct((t, dr), F32),
                   vec_s, vec_s, vec_s],
        scratch_shapes=[pltpu.VMEM((t, cb), F32)] * 4,
        compiler_params=_cp("parallel"))(dg, gr, rc, r_pre, i_pre, hs, br, bi, lam)


def f_elem_fwd(z, cw, cbias, name):
    t, two_f = z.shape
    dff = two_f // 2
    width = cw.shape[0]
    cb = _tile(dff, 256)
    off = dff // cb

    def body(zg_ref, zv_ref, wg_ref, wv_ref, bg_ref, bv_ref, o_ref):
        zcg = _conv_fwd_val(zg_ref[...], wg_ref, bg_ref, width)
        zcv = _conv_fwd_val(zv_ref[...], wv_ref, bv_ref, width)
        o_ref[...] = (_gelu(zcg) * zcv).astype(BF16)

    lo = lambda j: (0, j)
    hi = lambda j: (0, off + j)
    return pl.pallas_call(
        body, name=name, grid=(dff // cb,),
        in_specs=[pl.BlockSpec((t, cb), lo), pl.BlockSpec((t, cb), hi),
                  pl.BlockSpec((width, cb), lo), pl.BlockSpec((width, cb), hi),
                  pl.BlockSpec((1, cb), lo), pl.BlockSpec((1, cb), hi)],
        out_specs=pl.BlockSpec((t, cb), lo),
        out_shape=jax.ShapeDtypeStruct((t, dff), BF16),
        compiler_params=_cp("parallel"))(z, z, cw, cw, cbias, cbias)


def f_elem_bwd(z, dff_g, cw, cbias, name):
    t, two_f = z.shape
    dff = two_f // 2
    width = cw.shape[0]
    cb = _tile(dff, 256)
    off = dff // cb

    def body(zg_ref, zv_ref, d_ref, wg_ref, wv_ref, bg_ref, bv_ref,
             dzg_ref, dzv_ref, dwg_ref, dwv_ref, dbg_ref, dbv_ref):
        zg = zg_ref[...]
        zv = zv_ref[...]
        zcg = _conv_fwd_val(zg, wg_ref, bg_ref, width)
        zcv = _conv_fwd_val(zv, wv_ref, bv_ref, width)
        ge, dge = _gelu_parts(zcg)
        dv = d_ref[...]
        dx, dws, dbias = _conv_bwd_val(dv * zcv * dge, zg, wg_ref, width)
        dzg_ref[...] = dx.astype(BF16)
        for k in range(width):
            dwg_ref[pl.ds(k, 1), :] = dws[k]
        dbg_ref[...] = dbias
        dx, dws, dbias = _conv_bwd_val(dv * ge, zv, wv_ref, width)
        dzv_ref[...] = dx.astype(BF16)
        for k in range(width):
            dwv_ref[pl.ds(k, 1), :] = dws[k]
        dbv_ref[...] = dbias

    lo = lambda j: (0, j)
    hi = lambda j: (0, off + j)
    col = pl.BlockSpec((t, cb), lo)
    wsp = pl.BlockSpec((width, cb), lo)
    vsp = pl.BlockSpec((1, cb), lo)
    return pl.pallas_call(
        body, name=name, grid=(dff // cb,),
        in_specs=[col, pl.BlockSpec((t, cb), hi), col, wsp, pl.BlockSpec((width, cb), hi),
                  vsp, pl.BlockSpec((1, cb), hi)],
        out_specs=[col, col, wsp, wsp, vsp, vsp],
        out_shape=[jax.ShapeDtypeStruct((t, dff), BF16)] * 2
        + [jax.ShapeDtypeStruct((width, dff), F32)] * 2 + [jax.ShapeDtypeStruct((1, dff), F32)] * 2,
        compiler_params=_cp("parallel"))(z, z, dff_g, cw, cw, cbias, cbias)


def kv_fwd(z, fb, d_model, name):
    t, _ = z.shape
    blk = 2 * d_model // LANES

    def body(z_ref, fb_ref, c_ref, lf_s):
        v = z_ref[...] + fb_ref[...]
        lf_s[...] = -_softplus(-v)
        _scan_rows(None, lf_s, c_ref, t, LANES, reverse=False)

    return pl.pallas_call(
        body, name=name, grid=(1,),
        in_specs=[pl.BlockSpec((t, LANES), lambda j: (0, blk)), pl.BlockSpec((1, LANES), lambda j: (0, 0))],
        out_specs=pl.BlockSpec((t, LANES), lambda j: (0, 0)),
        out_shape=jax.ShapeDtypeStruct((t, LANES), F32),
        scratch_shapes=[pltpu.VMEM((t, LANES), F32)],
        compiler_params=_cp("arbitrary"))(z, fb)


def kv_bwd(dcs, z, fb, d_model, name):
    t, _ = z.shape
    blk = 2 * d_model // LANES
    n = len(dcs)

    def body(*refs):
        z_ref, fb_ref, dz_ref, dfb_ref, dc_s, dl_s = refs[n:]
        tot = refs[0][...]
        for r in refs[1:n]:
            tot = tot + r[...]
        dc_s[...] = tot
        _scan_rows(None, dc_s, dl_s, t, LANES, reverse=True)
        v = z_ref[...] + fb_ref[...]
        dz = dl_s[...] * _sigmoid(-v)
        dz_ref[...] = dz.astype(BF16)
        dfb_ref[...] = jnp.sum(dz, axis=0, keepdims=True)

    full = pl.BlockSpec((t, LANES), lambda j: (0, 0))
    return pl.pallas_call(
        body, name=name, grid=(1,),
        in_specs=[full] * n + [pl.BlockSpec((t, LANES), lambda j: (0, blk)),
                               pl.BlockSpec((1, LANES), lambda j: (0, 0))],
        out_specs=[full, pl.BlockSpec((1, LANES), lambda j: (0, 0))],
        out_shape=[jax.ShapeDtypeStruct((t, LANES), BF16), jax.ShapeDtypeStruct((1, LANES), F32)],
        scratch_shapes=[pltpu.VMEM((t, LANES), F32)] * 2,
        compiler_params=_cp("arbitrary"))(*dcs, z, fb)


def add_cast(a, b, name):
    t, d = a.shape
    cb = _tile(d, 512)

    def body(a_ref, b_ref, o_ref):
        o_ref[...] = (a_ref[...] + b_ref[...]).astype(BF16)

    col = pl.BlockSpec((t, cb), lambda j: (0, j))
    return pl.pallas_call(body, name=name, grid=(d // cb,), in_specs=[col, col], out_specs=col,
                          out_shape=jax.ShapeDtypeStruct((t, d), BF16),
                          compiler_params=_cp("parallel"))(a, b)


def _attn_geometry(t):
    nqb = 6 if t > 1024 else 2
    tp = _round_up(t, LANES * nqb)
    return nqb, tp, tp // nqb


def _attn_scales(dh):
    scale = dh ** -0.5
    if math.log2(scale).is_integer():
        return scale, 1.0
    return 1.0, scale


def _attn_pieces(qs, ks, crow, j, i, tq, dh, s_mul):
    r0 = i * tq
    lanes = pl.ds(j * dh, dh)
    qi = qs[pl.ds(r0, tq), lanes]
    spans = ([(0, r0)] if i > 0 else []) + [(r0, tq)]
    logits = []
    for k0, n in spans:
        s = lax.dot_general(qi, ks[pl.ds(k0, n), lanes], (((1,), (1,)), ((), ())),
                            preferred_element_type=F32)
        if s_mul != 1.0:
            s = s * s_mul
        s = s - crow[:, k0:k0 + n]
        if k0 == r0:
            rows = lax.broadcasted_iota(jnp.int32, (tq, tq), 0)
            cols = lax.broadcasted_iota(jnp.int32, (tq, tq), 1)
            s = jnp.where(cols <= rows, s, NEG_BIG)
        logits.append(s)
    mx = jnp.max(logits[0], axis=1, keepdims=True)
    for s in logits[1:]:
        mx = jnp.maximum(mx, jnp.max(s, axis=1, keepdims=True))
    es = [jnp.exp(s - mx) for s in logits]
    tot = jnp.sum(es[0], axis=1, keepdims=True)
    for e in es[1:]:
        tot = tot + jnp.sum(e, axis=1, keepdims=True)
    inv = 1.0 / tot
    return [(k0, n, e * inv) for (k0, n), e in zip(spans, es)], qi


def attn_fwd(qg, z, ct_pad, d_model, n_heads, name):
    t = qg.shape[0]
    dh = d_model // n_heads
    hp = LANES // dh
    nqb, tp, tq = _attn_geometry(t)
    nblk = d_model // LANES
    q_mul, s_mul = _attn_scales(dh)

    def body(q_ref, og_ref, k_ref, v_ref, ct_ref, o_ref, mo_ref, qs, ks, vs, os_):
        pad = jnp.zeros((tp - t, LANES), BF16)
        qs[pl.ds(0, t), :] = (q_ref[...] * q_mul).astype(BF16)
        qs[pl.ds(t, tp - t), :] = pad
        for src, dst in ((k_ref, ks), (v_ref, vs)):
            dst[pl.ds(0, t), :] = src[...].astype(BF16)
            dst[pl.ds(t, tp - t), :] = pad
        for j in range(hp):
            crow = ct_ref[j]
            lanes = pl.ds(j * dh, dh)
            for i in range(nqb):
                pieces, _ = _attn_pieces(qs, ks, crow, j, i, tq, dh, s_mul)
                acc = None
                for k0, n, p in pieces:
                    part = jnp.dot(p.astype(BF16), vs[pl.ds(k0, n), lanes], preferred_element_type=F32)
                    acc = part if acc is None else acc + part
                os_[pl.ds(i * tq, tq), lanes] = acc
        o = os_[pl.ds(0, t), :]
        o_ref[...] = o
        mo_ref[...] = (o * _sigmoid(og_ref[...])).astype(BF16)

    col = lambda off: pl.BlockSpec((t, LANES), lambda p: (0, off + p))
    return pl.pallas_call(
        body, name=name, grid=(nblk,),
        in_specs=[col(0), col(nblk), col(0), col(nblk), pl.BlockSpec((hp, 1, tp), lambda p: (p, 0, 0))],
        out_specs=[col(0), col(0)],
        out_shape=[jax.ShapeDtypeStruct((t, d_model), F32), jax.ShapeDtypeStruct((t, d_model), BF16)],
        scratch_shapes=[pltpu.VMEM((tp, LANES), BF16)] * 3 + [pltpu.VMEM((tp, LANES), F32)],
        compiler_params=_cp("parallel"))(qg, qg, z, z, ct_pad)


def attn_bwd(dmo, qg, z, o, ct_pad, d_model, n_heads, name):
    t = qg.shape[0]
    dh = d_model // n_heads
    hp = LANES // dh
    nqb, tp, tq = _attn_geometry(t)
    nblk = d_model // LANES
    q_mul, s_mul = _attn_scales(dh)
    scale = dh ** -0.5
    tn_dims = (((0,), (0,)), ((), ()))
    nt_dims = (((1,), (1,)), ((), ()))

    def body(dmo_ref, q_ref, og_ref, k_ref, v_ref, o_ref, ct_ref,
             dq_ref, dog_ref, dk_ref, dv_ref, dct_ref, qs, ks, vs, dos, dqs, dks, dvs):
        pad = jnp.zeros((tp - t, LANES), BF16)
        sg = _sigmoid(og_ref[...])
        dmo_v = dmo_ref[...]
        dog_ref[...] = (dmo_v * o_ref[...] * sg * (1.0 - sg)).astype(BF16)
        dos[pl.ds(0, t), :] = (dmo_v * sg).astype(BF16)
        dos[pl.ds(t, tp - t), :] = pad
        qs[pl.ds(0, t), :] = (q_ref[...] * q_mul).astype(BF16)
        qs[pl.ds(t, tp - t), :] = pad
        for src, dst in ((k_ref, ks), (v_ref, vs)):
            dst[pl.ds(0, t), :] = src[...].astype(BF16)
            dst[pl.ds(t, tp - t), :] = pad
        dks[...] = jnp.zeros_like(dks)
        dvs[...] = jnp.zeros_like(dvs)
        dct_ref[...] = jnp.zeros_like(dct_ref)
        for j in range(hp):
            crow = ct_ref[j]
            lanes = pl.ds(j * dh, dh)
            for i in range(nqb):
                pieces, qi = _attn_pieces(qs, ks, crow, j, i, tq, dh, s_mul)
                do_i = dos[pl.ds(i * tq, tq), lanes]
                dps = [lax.dot_general(do_i, vs[pl.ds(k0, n), lanes], nt_dims, preferred_element_type=F32)
                       for k0, n, _ in pieces]
                row = None
                for (_, _, p), dp in zip(pieces, dps):
                    part = jnp.sum(p * dp, axis=1, keepdims=True)
                    row = part if row is None else row + part
                dq_i = None
                for (k0, n, p), dp in zip(pieces, dps):
                    ds = p * (dp - row)
                    ds_b = ds.astype(BF16)
                    keys = pl.ds(k0, n)
                    part = jnp.dot(ds_b, ks[keys, lanes], preferred_element_type=F32)
                    dq_i = part if dq_i is None else dq_i + part
                    dks[keys, lanes] += lax.dot_general(ds_b, qi, tn_dims, preferred_element_type=F32) * s_mul
                    dvs[keys, lanes] += lax.dot_general(p.astype(BF16), do_i, tn_dims,
                                                        preferred_element_type=F32)
                    dct_ref[j, :, keys] -= jnp.sum(ds, axis=0, keepdims=True)
                dqs[pl.ds(i * tq, tq), lanes] = dq_i * scale
        dq_ref[...] = dqs[pl.ds(0, t), :].astype(BF16)
        dk_ref[...] = dks[pl.ds(0, t), :]
        dv_ref[...] = dvs[pl.ds(0, t), :]

    col = lambda off: pl.BlockSpec((t, LANES), lambda p: (0, off + p))
    big = lambda dt: jax.ShapeDtypeStruct((t, d_model), dt)
    return pl.pallas_call(
        body, name=name, grid=(nblk,),
        in_specs=[col(0), col(0), col(nblk), col(0), col(nblk), col(0),
                  pl.BlockSpec((hp, 1, tp), lambda p: (p, 0, 0))],
        out_specs=[col(0), col(0), col(0), col(0), pl.BlockSpec((hp, 1, tp), lambda p: (p, 0, 0))],
        out_shape=[big(BF16), big(BF16), big(F32), big(F32),
                   jax.ShapeDtypeStruct((n_heads, 1, tp), F32)],
        scratch_shapes=[pltpu.VMEM((tp, LANES), BF16)] * 4 + [pltpu.VMEM((tp, LANES), F32)] * 3,
        compiler_params=_cp("parallel"))(dmo, qg, qg, z, z, o, ct_pad)


def cast_into_slot(shard, w, index, name):
    r, c = w.shape[-2:]
    rh = r // 2
    tr = _tile(rh, 512, 16)
    n = rh // tr
    if w.ndim == 3:
        w_spec = pl.BlockSpec((None, tr, c), lambda h, i, sh: (index, h * n + i, 0))
    else:
        w_spec = pl.BlockSpec((tr, c), lambda h, i, sh: (h * n + i, 0))

    def body(sh_ref, w_ref, o_ref):
        del sh_ref
        o_ref[...] = w_ref[...].astype(BF16)

    return pl.pallas_call(
        body, name=name,
        grid_spec=pltpu.PrefetchScalarGridSpec(
            num_scalar_prefetch=1, grid=(2, n), in_specs=[w_spec],
            out_specs=pl.BlockSpec((None, None, tr, c), lambda h, i, sh: (sh[0], h, i, 0))),
        out_shape=jax.ShapeDtypeStruct((N_SHARDS, 2, rh, c), BF16),
        compiler_params=_cp("parallel", "parallel"))(shard, w)


def owner_sum(shard_core, g, recv, buf, layer, name):
    _, _, rh, c = g.shape
    n_recv = recv.shape[0]
    tr = _tile(rh, 512, 16)
    slot = (lambda sc: sc[0]) if layer is None else (lambda sc: layer)

    def body(sc_ref, a_ref, r_ref, buf_ref, out_ref):
        del sc_ref, buf_ref
        acc = a_ref[...].astype(F32)
        for k in range(n_recv):
            acc = acc + r_ref[k].astype(F32)
        out_ref[...] = acc

    return pl.pallas_call(
        body, name=name,
        grid_spec=pltpu.PrefetchScalarGridSpec(
            num_scalar_prefetch=1, grid=(rh // tr,),
            in_specs=[pl.BlockSpec((None, None, tr, c), lambda i, sc: (sc[0], sc[1], i, 0)),
                      pl.BlockSpec((n_recv, tr, c), lambda i, sc: (0, i, 0)),
                      pl.BlockSpec(memory_space=pl.ANY)],
            out_specs=pl.BlockSpec((None, None, tr, c), lambda i, sc: (slot(sc), sc[1], i, 0))),
        out_shape=jax.ShapeDtypeStruct(buf.shape, F32),
        input_output_aliases={3: 0},
        compiler_params=_cp("parallel"))(shard_core, g, recv, buf)


def adamw(w, g, m, v, name, emit_g=False):
    r, c = w.shape
    tr = _tile(r, 512, SUBLANES)
    c1 = 1.0 - ADAM_B1 ** ADAM_STEP
    c2 = 1.0 - ADAM_B2 ** ADAM_STEP
    n_out = 4 if emit_g else 3

    def body(w_ref, g_ref, m_ref, v_ref, d_ref, mo_ref, vo_ref, *go_ref):
        gv = g_ref[...]
        if emit_g:
            go_ref[0][...] = gv
        mn = ADAM_B1 * m_ref[...] + (1.0 - ADAM_B1) * gv
        vn = ADAM_B2 * v_ref[...] + (1.0 - ADAM_B2) * (gv * gv)
        m_hat = mn / c1
        v_hat = vn / c2
        d_ref[...] = -ADAM_LR * (m_hat / (jnp.sqrt(v_hat) + ADAM_EPS) + ADAM_WD * w_ref[...])
        mo_ref[...] = mn
        vo_ref[...] = vn

    blk = pl.BlockSpec((tr, c), lambda i: (i, 0))
    return pl.pallas_call(
        body, name=name, grid=(r // tr,), in_specs=[blk] * 4, out_specs=[blk] * n_out,
        out_shape=[jax.ShapeDtypeStruct((r, c), F32)] * n_out,
        compiler_params=_cp("parallel"))(w, g, m, v)


def _coords():
    return lax.axis_index("x"), lax.axis_index("y"), lax.axis_index("c")


def _exchange(name, ins, out_shapes, plan, in_place=False):
    n_in = len(ins)
    n_out = len(out_shapes)
    n_rem = len(plan([None] * n_in, [None] * n_out, True))

    def body(*refs):
        in_refs = refs[:n_in]
        out_refs = refs[n_in:n_in + n_out]
        send_sems, recv_sems = refs[n_in + n_out:]
        copies = [pltpu.make_async_remote_copy(
            src_ref=src, dst_ref=dst, send_sem=send_sems.at[q], recv_sem=recv_sems.at[q],
            device_id=peer, device_id_type=pl.DeviceIdType.MESH)
            for q, (src, dst, peer) in enumerate(plan(list(in_refs), list(out_refs), False))]
        for cp in copies:
            cp.start()
        for cp in copies:
            cp.wait_recv()
        for cp in copies:
            cp.wait_send()

    hbm = pl.BlockSpec(memory_space=pl.ANY)
    return pl.pallas_call(
        body, name=name, in_specs=[hbm] * n_in, out_specs=[hbm] * n_out, out_shape=out_shapes,
        input_output_aliases={i: i for i in range(n_in)} if in_place else {},
        scratch_shapes=[pltpu.SemaphoreType.DMA((n_rem,)), pltpu.SemaphoreType.DMA((n_rem,))],
        compiler_params=pltpu.CompilerParams(has_side_effects=True))(*ins)


def _split_start(name, groups, plan):
    flat = [a for grp in groups for a in grp]
    n, n_grp = len(flat), len(groups)
    counts = [len(plan(g, [None] * len(grp), True)) for g, grp in enumerate(groups)]

    def body(*refs):
        ins, sems, token = refs[:n], refs[n:n + 2 * n_grp], refs[-1]
        pos = 0
        for g, grp in enumerate(groups):
            arrs = list(ins[pos:pos + len(grp)])
            pos += len(grp)
            for q, (src, dst, peer) in enumerate(plan(g, arrs, False)):
                pltpu.make_async_remote_copy(
                    src_ref=src, dst_ref=dst, send_sem=sems[2 * g].at[q], recv_sem=sems[2 * g + 1].at[q],
                    device_id=peer, device_id_type=pl.DeviceIdType.MESH).start()
        token[...] = jnp.zeros_like(token)

    hbm = pl.BlockSpec(memory_space=pltpu.HBM)
    sem = pl.BlockSpec(memory_space=pltpu.SEMAPHORE)
    outs = pl.pallas_call(
        body, name=name,
        out_shape=[pltpu.SemaphoreType.DMA((cnt,)) for cnt in counts for _ in range(2)]
        + [pltpu.HBM(a.shape, a.dtype) for a in flat] + [jax.ShapeDtypeStruct((SUBLANES, LANES), F32)],
        in_specs=[hbm] * n, out_specs=[sem] * (2 * n_grp) + [hbm] * n + [pl.BlockSpec(memory_space=pltpu.VMEM)],
        input_output_aliases={i: 2 * n_grp + i for i in range(n)},
        compiler_params=pltpu.CompilerParams(has_side_effects=pltpu.SideEffectType.DATAFLOW_SIDE_EFFECTING),
    )(*[pltpu.with_memory_space_constraint(a, pltpu.HBM) for a in flat])
    started, pos = [], 2 * n_grp
    for g, grp in enumerate(groups):
        started.append((outs[2 * g], outs[2 * g + 1], list(outs[pos:pos + len(grp)])))
        pos += len(grp)
    return started, outs[-1]


def _split_wait(name, started, after, plan_g):
    send_sems, recv_sems, arrs = started
    n = len(arrs)

    def body(*refs):
        ins, ssem, rsem = list(refs[:n]), refs[n], refs[n + 1]
        for q, (src, dst, peer) in enumerate(plan_g(ins, False)):
            cp = pltpu.make_async_remote_copy(
                src_ref=src, dst_ref=dst, send_sem=ssem.at[q], recv_sem=rsem.at[q],
                device_id=peer, device_id_type=pl.DeviceIdType.MESH)
            cp.wait_send()
            cp.wait_recv()

    hbm = pl.BlockSpec(memory_space=pltpu.HBM)
    sem = pl.BlockSpec(memory_space=pltpu.SEMAPHORE)
    return pl.pallas_call(
        body, name=name, out_shape=[pltpu.HBM(a.shape, a.dtype) for a in arrs],
        in_specs=[hbm] * n + [sem, sem, pl.BlockSpec(memory_space=pl.ANY)], out_specs=[hbm] * n,
        input_output_aliases={i: i for i in range(n)},
        compiler_params=pltpu.CompilerParams(has_side_effects=pltpu.SideEffectType.DATAFLOW_SIDE_EFFECTING),
    )(*arrs, send_sems, recv_sems, after)


def _gather_ici_plan(arrs, count_only):
    if count_only:
        return [None] * (3 * len(arrs))
    x, y, c = _coords()
    pushes = []
    for a in arrs:
        mine = a.at[2 * x + y, c]
        pushes += [(mine, mine, peer) for peer, _ in _other_chips(x, y, c)]
    return pushes


def _all_to_all_plan(arrs, count_only):
    half = len(arrs) // 2
    if count_only:
        return [None] * (7 * half)
    x, y, c = _coords()
    pushes = []
    for src, land in zip(arrs[:half], arrs[half:]):
        for flips in range(1, 8):
            px, py, pc = (1 - x if flips & 4 else x), (1 - y if flips & 2 else y), (1 - c if flips & 1 else c)
            pushes.append((src.at[2 * px + py, pc], land.at[flips - 1], (px, py, pc)))
    return pushes


def forward_to_sibling(bufs, name):
    n = len(bufs)

    def plan(ins, outs, count_only):
        if count_only:
            return [None] * (3 * n)
        x, y, c = _coords()
        pushes = []
        for i in range(n):
            for _, src_shard in _other_chips(x, y, c):
                slab = outs[i].at[src_shard, c]
                pushes.append((slab, slab, (x, y, 1 - c)))
        return pushes

    shapes = [jax.ShapeDtypeStruct(b.shape, b.dtype) for b in bufs]
    return _exchange(name, bufs, shapes, plan, in_place=True)


def _other_chips(x, y, c):
    return [((1 - x, y, c), 2 * (1 - x) + y), ((x, 1 - y, c), 2 * x + 1 - y),
            ((1 - x, 1 - y, c), 2 * (1 - x) + 1 - y)]


def join_halves(bufs, everywhere, name):
    slots = [(i, l) for i, b in enumerate(bufs) for l in range(b.shape[0])]
    n = len(bufs)

    def plan(ins, outs, count_only):
        if count_only:
            return [None] * (len(slots) + 7)
        x, y, c = _coords()
        pushes = [(outs[i].at[l, c], outs[i].at[l, c], (x, y, 1 - c)) for i, l in slots]
        mine = outs[n].at[2 * x + y, c]
        for flips in range(1, 8):
            peer = (1 - x if flips & 4 else x, 1 - y if flips & 2 else y, 1 - c if flips & 1 else c)
            pushes.append((mine, mine, peer))
        return pushes

    arrs = list(bufs) + [everywhere]
    shapes = [jax.ShapeDtypeStruct(b.shape, b.dtype) for b in arrs]
    return _exchange(name, arrs, shapes, plan, in_place=True)


def _pack(arrays, multiple):
    flat = jnp.concatenate([a.reshape(-1) for a in arrays])
    n = flat.shape[0]
    return jnp.pad(flat, (0, _round_up(n, multiple) - n))


def _unpack(flat, shapes):
    out, pos = [], 0
    for shp in shapes:
        n = math.prod(shp)
        out.append(flat[pos:pos + n].reshape(shp))
        pos += n
    return out


def _block_diag(w, per_group):
    nb, bs, _ = w.shape
    g = nb // per_group
    w4 = w.reshape(g, per_group, bs, bs)
    eye = jnp.eye(per_group, dtype=w.dtype)
    full = w4[:, :, :, None, :] * eye[None, :, None, :, None]
    return full.reshape(g, per_group * bs, per_group * bs).astype(BF16)


def _block_diag_extract(full, per_group, bs):
    g = full.shape[0]
    f5 = full.reshape(g, per_group, bs, per_group, bs)
    idx = jnp.arange(per_group)
    picked = f5[:, idx, :, idx, :]
    return jnp.moveaxis(picked, 0, 1).reshape(g * per_group, bs, bs)


def kernel(x, meta, a_w_in, a_conv_w, a_conv_b, a_w_r, a_b_r, a_w_i, a_b_i, a_lambda, a_w_out, kv_w, kv_f_b, b_w_in, b_w_out, f_w_in, f_conv_w, f_conv_b, f_w_out, ln1_g, ln1_b, ln2_g, ln2_b, loss_target, m_meta, m_a_w_in, m_a_conv_w, m_a_conv_b, m_a_w_r, m_a_b_r, m_a_w_i, m_a_b_i, m_a_lambda, m_a_w_out, m_kv_w, m_kv_f_b, m_b_w_in, m_b_w_out, m_f_w_in, m_f_conv_w, m_f_conv_b, m_f_w_out, m_ln1_g, m_ln1_b, m_ln2_g, m_ln2_b, v_meta, v_a_w_in, v_a_conv_w, v_a_conv_b, v_a_w_r, v_a_b_r, v_a_w_i, v_a_b_i, v_a_lambda, v_a_w_out, v_kv_w, v_kv_f_b, v_b_w_in, v_b_w_out, v_f_w_in, v_f_conv_w, v_f_conv_b, v_f_w_out, v_ln1_g, v_ln1_b, v_ln2_g, v_ln2_b):
    weights = dict(meta=meta, a_w_in=a_w_in, a_conv_w=a_conv_w, a_conv_b=a_conv_b, a_w_r=a_w_r, a_b_r=a_b_r,
                   a_w_i=a_w_i, a_b_i=a_b_i, a_lambda=a_lambda, a_w_out=a_w_out, kv_w=kv_w, kv_f_b=kv_f_b,
                   b_w_in=b_w_in, b_w_out=b_w_out, f_w_in=f_w_in, f_conv_w=f_conv_w, f_conv_b=f_conv_b,
                   f_w_out=f_w_out, ln1_g=ln1_g, ln1_b=ln1_b, ln2_g=ln2_g, ln2_b=ln2_b)
    mom_m = dict(meta=m_meta, a_w_in=m_a_w_in, a_conv_w=m_a_conv_w, a_conv_b=m_a_conv_b, a_w_r=m_a_w_r,
                 a_b_r=m_a_b_r, a_w_i=m_a_w_i, a_b_i=m_a_b_i, a_lambda=m_a_lambda, a_w_out=m_a_w_out,
                 kv_w=m_kv_w, kv_f_b=m_kv_f_b, b_w_in=m_b_w_in, b_w_out=m_b_w_out, f_w_in=m_f_w_in,
                 f_conv_w=m_f_conv_w, f_conv_b=m_f_conv_b, f_w_out=m_f_w_out, ln1_g=m_ln1_g, ln1_b=m_ln1_b,
                 ln2_g=m_ln2_g, ln2_b=m_ln2_b)
    mom_v = dict(meta=v_meta, a_w_in=v_a_w_in, a_conv_w=v_a_conv_w, a_conv_b=v_a_conv_b, a_w_r=v_a_w_r,
                 a_b_r=v_a_b_r, a_w_i=v_a_w_i, a_b_i=v_a_b_i, a_lambda=v_a_lambda, a_w_out=v_a_w_out,
                 kv_w=v_kv_w, kv_f_b=v_kv_f_b, b_w_in=v_b_w_in, b_w_out=v_b_w_out, f_w_in=v_f_w_in,
                 f_conv_w=v_f_conv_w, f_conv_b=v_f_conv_b, f_w_out=v_f_w_out, ln1_g=v_ln1_g, ln1_b=v_ln1_b,
                 ln2_g=v_ln2_g, ln2_b=v_ln2_b)
    return _train_step(x, loss_target, weights, mom_m, mom_v)


WEIGHT_ORDER = ("meta", "a_w_in", "a_conv_w", "a_conv_b", "a_w_r", "a_b_r", "a_w_i", "a_b_i", "a_lambda",
                "a_w_out", "kv_w", "kv_f_b", "b_w_in", "b_w_out", "f_w_in", "f_conv_w", "f_conv_b",
                "f_w_out", "ln1_g", "ln1_b", "ln2_g", "ln2_b")
BIG = ("a_w_in", "a_w_out", "kv_w", "b_w_in", "b_w_out", "f_w_in", "f_w_out")
OUT_TYPE = ("a_w_out", "b_w_out", "f_w_out")
SMALL_SHARDED = (("meta", 1), ("a_conv_w", 2), ("a_conv_b", 1), ("a_b_r", 1), ("a_b_i", 1), ("a_lambda", 1),
                 ("f_conv_w", 2))
SMALL_REPLICATED = ("a_w_r", "a_w_i", "kv_f_b", "f_conv_b", "ln1_g", "ln1_b", "ln2_g", "ln2_b")


def _train_step(x, loss_target, weights, mom_m, mom_v):
    S = N_SHARDS
    seq, d = x.shape[1], x.shape[2]
    nm = weights["meta"].shape[0]
    la = weights["a_w_in"].shape[0]
    lb = weights["b_w_in"].shape[0]
    depth = la + lb
    dr = weights["a_w_out"].shape[1] * S
    nb, bs = weights["a_w_r"].shape[1], weights["a_w_r"].shape[2]
    per_group = (LANES // math.gcd(bs, LANES))
    gs = per_group * bs
    heads = weights["kv_f_b"].shape[0]
    dff = weights["f_w_out"].shape[1] * S
    nkv = 2 * d + heads
    nkv_s = weights["kv_w"].shape[1]
    nkvp = _round_up(2 * d + LANES, 768) if 2 * d + LANES > 768 else 2 * d + LANES
    alpha = (2 * depth) ** 0.25
    xi, yi, ci = _coords()
    shard = 2 * xi + yi
    shard_arr = jnp.reshape(shard, (1,)).astype(jnp.int32)
    shard_core_arr = jnp.stack([shard, ci]).astype(jnp.int32)

    def mixer_keys(l):
        if l < la:
            return [("a_w_in", l), ("a_w_out", l)]
        return ([("kv_w", 0)] if l == la else []) + [("b_w_in", l - la), ("b_w_out", l - la)]

    def ffn_keys(l):
        return [("f_w_in", l), ("f_w_out", l)]

    assert la >= 1
    groups = [mixer_keys(0)[:1], mixer_keys(0)[1:] + ffn_keys(0)]
    groups += [mixer_keys(l) + ffn_keys(l) for l in range(1, depth)]
    keys = [kl for grp in groups for kl in grp]
    shape2d = {(k, i): weights[k].shape[-2:] for k, i in keys}
    small_local = [weights[k] for k, _ in SMALL_SHARDED]
    sm_flat = _pack(small_local, 2 * SUBLANES * LANES).reshape(1, 2, -1, LANES)
    sm_slot = lax.dynamic_update_slice_in_dim(lax.empty((S,) + sm_flat.shape[1:], F32), sm_flat, shard, axis=0)
    parts = [[cast_into_slot(shard_arr, weights[k], i, f"cast_{k}{i}") for k, i in grp] for grp in groups]
    parts[0].append(sm_slot)
    in_flight, start_token = _split_start("gather_start", parts, lambda g, refs, cnt: _gather_ici_plan(refs, cnt))
    gw = {}

    def fetch(g, after):
        arrs = _split_wait(f"gather_wait_{g}", in_flight[g], after, _gather_ici_plan)
        arrs = forward_to_sibling(arrs, f"gather_fwd_{g}")
        for kl, a in zip(groups[g], arrs):
            rows, cols = shape2d[kl]
            gw[kl] = a.reshape(S * rows, cols) if kl[0] in OUT_TYPE else a.reshape(S, rows, cols)
        return arrs

    sm_all = fetch(0, start_token)[-1].reshape(S, -1)
    small_full = {}
    per_shard = [_unpack(sm_all[s], [a.shape for a in small_local]) for s in range(S)]
    for idx, (k, axis) in enumerate(SMALL_SHARDED):
        small_full[k] = jnp.concatenate([per_shard[s][idx] for s in range(S)], axis=axis)
    fb_pad = jnp.pad(weights["kv_f_b"], (0, LANES - heads)).reshape(1, LANES)
    wr_g = [_block_diag(weights["a_w_r"][l], per_group) for l in range(la)]
    wi_g = [_block_diag(weights["a_w_i"][l], per_group) for l in range(la)]
    row = lambda v: v.reshape(1, -1)

    h, hb = embed_fwd(small_full["meta"], x[0], "embed")
    saved = []
    kvz = ct_pad = None
    _, tp, _ = _attn_geometry(nm + seq)
    t = nm + seq
    for l in range(depth):
        sv = {"hb_in": hb}
        if l > 0:
            fetch(l + 1, hb)
        if l < la:
            gr = mm_in(hb, gw[("a_w_in", l)], F32, f"a{l}_in")
            rc, rcb = a_conv_fwd(gr, small_full["a_conv_w"][l], row(small_full["a_conv_b"][l]), f"a{l}_conv")
            r_pre, i_pre = mm_bd(rcb, wr_g[l], wi_g[l], f"a{l}_gates")
            hs, gb = a_elem_fwd(gr, rc, r_pre, i_pre, row(small_full["a_b_r"][l]), row(small_full["a_b_i"][l]),
                                row(small_full["a_lambda"][l]), f"a{l}_lru")
            if l == 0:
                fetch(1, gb)
            mixed, w_mix = gb, gw[("a_w_out", l)]
            sv.update(gr=gr, rc=rc, rcb=rcb, r_pre=r_pre, i_pre=i_pre, hs=hs, gb=gb)
        else:
            j = l - la
            if j == 0:
                kv_cat = jnp.moveaxis(gw[("kv_w", 0)], 0, 1).reshape(d, S * nkv_s)
                kv_pad = jnp.pad(kv_cat, ((0, 0), (0, nkvp - nkv))).reshape(1, d, nkvp)
                kvz = mm_in(hb, kv_pad, F32, "kv_proj")
                cum = kv_fwd(kvz, fb_pad, d, "kv_forget")
                ct_pad = jnp.pad(cum[:, :heads].T, ((0, 0), (0, tp - t))).reshape(heads, 1, tp)
                kv_hb = hb
            qg = mm_in(hb, gw[("b_w_in", j)], F32, f"b{j}_in")
            o, mob = attn_fwd(qg, kvz, ct_pad, d, heads, f"b{j}_attn")
            mixed, w_mix = mob, gw[("b_w_out", j)]
            sv.update(qg=qg, o=o, mob=mob)
        h1, h1b, xh1, rs1 = mm_out_ln(mixed, w_mix, h, row(weights["ln1_g"][l]), row(weights["ln1_b"][l]), alpha,
                                      f"mix{l}_out_ln1")
        zf = mm_in(h1b, gw[("f_w_in", l)], F32, f"f{l}_in")
        ffb = f_elem_fwd(zf, small_full["f_conv_w"][l], row(weights["f_conv_b"][l]), f"f{l}_act")
        h2, h2b, xh2, rs2 = mm_out_ln(ffb, gw[("f_w_out", l)], h1, row(weights["ln2_g"][l]), row(weights["ln2_b"][l]),
                                      alpha, f"f{l}_out_ln2")
        sv.update(h1b=h1b, xh1=xh1, rs1=rs1, zf=zf, ffb=ffb, xh2=xh2, rs2=rs2)
        saved.append(sv)
        h, hb = h2, h2b
    loss11, dy = loss_fwd_bwd(h, loss_target[0], nm, "loss")

    grads = {}

    def by_owner(kl, g3):
        rows, cols = shape2d[kl]
        grads[kl] = g3.reshape(S, 2, rows // 2, cols)

    reducing = []

    def send_grads(g, names, arrays):
        lands = [lax.empty((7,) + a.shape[2:], a.dtype) for a in arrays]
        started, token = _split_start(f"grad_a2a_start_{g}", [list(arrays) + lands],
                                      lambda _, refs, cnt: _all_to_all_plan(refs, cnt))
        reducing.append((names, started[0]))
        return token[0:1, 0:1]

    def after_start(vec, zero):
        return vec if zero is None else vec + zero

    pin = None

    g_small = {}
    per_layer = {k: [None] * n for k, n in (
        ("a_conv_w", la), ("a_conv_b", la), ("a_w_r", la), ("a_b_r", la), ("a_w_i", la), ("a_b_i", la),
        ("a_lambda", la), ("f_conv_w", depth), ("f_conv_b", depth), ("ln1_g", depth), ("ln1_b", depth),
        ("ln2_g", depth), ("ln2_b", depth))}
    adds = [(dy, 1.0)]
    dks, dvs, dcs = [], [], []
    for l in reversed(range(depth)):
        sv = saved[l]
        ds2, ds2b, dg2, db2 = ln_bwd(adds, sv["xh2"], sv["rs2"], after_start(row(weights["ln2_g"][l]), pin),
                                     f"ln2_{l}_bwd")
        pin = None
        per_layer["ln2_g"][l], per_layer["ln2_b"][l] = dg2[0], db2[0]
        dff_v = mm_out_nt(ds2b, gw[("f_w_out", l)], f"f{l}_out_dx")
        by_owner(("f_w_out", l), mm_tn(sv["ffb"], [ds2b], 1, f"f{l}_out_dw"))
        dzg, dzv, dwg, dwv, dbg, dbv = f_elem_bwd(sv["zf"], dff_v, small_full["f_conv_w"][l],
                                                  row(weights["f_conv_b"][l]), f"f{l}_act_bwd")
        per_layer["f_conv_w"][l] = jnp.concatenate([dwg, dwv], axis=1)
        per_layer["f_conv_b"][l] = jnp.concatenate([dbg, dbv], axis=1)[0]
        dh1_f = mm_in_nt([dzg, dzv], gw[("f_w_in", l)], f"f{l}_in_dx")
        by_owner(("f_w_in", l), mm_tn(sv["h1b"], [dzg, dzv], S, f"f{l}_in_dw"))
        ds1, ds1b, dg1, db1 = ln_bwd([(ds2, alpha), (dh1_f, 1.0)], sv["xh1"], sv["rs1"],
                                     row(weights["ln1_g"][l]), f"ln1_{l}_bwd")
        per_layer["ln1_g"][l], per_layer["ln1_b"][l] = dg1[0], db1[0]
        if l < la:
            dgv = mm_out_nt(ds1b, gw[("a_w_out", l)], f"a{l}_out_dx")
            by_owner(("a_w_out", l), mm_tn(sv["gb"], [ds1b], 1, f"a{l}_out_dw"))
            if l == 0:
                pin = send_grads(1, groups[1], [grads[kl] for kl in groups[1]])
            dgate_b, drp_b, dip_b, drc_d, dlam, dbr, dbi = a_elem_bwd(
                dgv, sv["gr"], sv["rc"], sv["r_pre"], sv["i_pre"], sv["hs"], row(small_full["a_b_r"][l]),
                row(small_full["a_b_i"][l]), after_start(row(small_full["a_lambda"][l]), pin), f"a{l}_lru_bwd")
            pin = None
            drc_g = mm_bd_nt(drp_b, dip_b, wr_g[l], wi_g[l], f"a{l}_gates_dx")
            dwr_g, dwi_g = mm_bd_tn(sv["rcb"], drp_b, dip_b, gs, f"a{l}_gates_dw")
            dgr_b, dcw, dcb = a_conv_bwd(drc_d, drc_g, sv["gr"], small_full["a_conv_w"][l], dgate_b,
                                         f"a{l}_conv_bwd")
            per_layer["a_w_r"][l] = _block_diag_extract(dwr_g, per_group, bs)
            per_layer["a_w_i"][l] = _block_diag_extract(dwi_g, per_group, bs)
            per_layer["a_lambda"][l], per_layer["a_b_r"][l], per_layer["a_b_i"][l] = dlam[0], dbr[0], dbi[0]
            per_layer["a_conv_w"][l], per_layer["a_conv_b"][l] = dcw, dcb[0]
            by_owner(("a_w_in", l), mm_tn(sv["hb_in"], [dgr_b], S, f"a{l}_in_dw"))
            gate_w = gw[("a_w_in", l)]
            if l == 0:
                zero = send_grads(0, groups[0], [grads[kl] for kl in groups[0]])
                gate_w = gate_w + zero.astype(BF16)[None]
            dh_m = mm_in_nt([dgr_b], gate_w, f"a{l}_in_dx")
        else:
            j = l - la
            dmo = mm_out_nt(ds1b, gw[("b_w_out", j)], f"b{j}_out_dx")
            by_owner(("b_w_out", j), mm_tn(sv["mob"], [ds1b], 1, f"b{j}_out_dw"))
            dq_b, dog_b, dk, dv, dct = attn_bwd(dmo, sv["qg"], kvz, sv["o"], ct_pad, d, heads, f"b{j}_attn_bwd")
            dks.append(dk)
            dvs.append(dv)
            dcs.append(jnp.pad(dct[:, 0, :t].T, ((0, 0), (0, LANES - heads))))
            dh_m = mm_in_nt([dq_b, dog_b], gw[("b_w_in", j)], f"b{j}_in_dx")
            by_owner(("b_w_in", j), mm_tn(sv["hb_in"], [dq_b, dog_b], S, f"b{j}_in_dw"))
        adds = [(ds1, alpha), (dh_m, 1.0)]
        if l == la:
            dzf_b, dfb = kv_bwd(dcs, kvz, fb_pad, d, "kv_forget_bwd")
            dk_b = add_cast(dks[0], dks[1], "kv_dk") if lb == 2 else None
            dv_b = add_cast(dvs[0], dvs[1], "kv_dv") if lb == 2 else None
            dz_kv = jnp.concatenate([dk_b, dv_b, dzf_b, jnp.zeros((t, nkvp - 2 * d - LANES), BF16)], axis=1)
            dh_kv = mm_in_nt([dz_kv], kv_pad, "kv_proj_dx")
            kv_dw = mm_tn(kv_hb, [dz_kv], 1, "kv_proj_dw")
            by_owner(("kv_w", 0), jnp.moveaxis(kv_dw[0, :, :nkv].reshape(d, S, nkv_s), 1, 0))
            g_small["kv_f_b"] = dfb[0, :heads]
            adds.append((dh_kv, 1.0))
        if l > 0:
            pin = send_grads(l + 1, groups[l + 1], [grads[kl] for kl in groups[l + 1]])
    g_meta, g_x = embed_bwd(adds, nm, "embed_bwd")

    small_names = list(SMALL_REPLICATED) + [k for k, _ in SMALL_SHARDED]
    g_small["meta"] = g_meta
    for k, vals in per_layer.items():
        g_small[k] = jnp.stack(vals)
    small_shapes = {k: (weights[k].shape if k in SMALL_REPLICATED else g_small[k].shape) for k in small_names}
    sm_g = _pack([g_small[k].reshape(small_shapes[k]) for k in small_names], S * 2 * SUBLANES * LANES)
    sm_g = sm_g.reshape(S, 2, -1, LANES)
    send_grads(len(groups), [("small", 0)], [sm_g])

    fin = {"small": lax.empty(sm_g.shape, F32)}
    for kl in keys:
        n_stack = weights[kl[0]].shape[0] if weights[kl[0]].ndim == 3 else 1
        rows, cols = shape2d[kl]
        fin.setdefault(kl[0], lax.empty((n_stack, 2, rows // 2, cols), F32))
    for g, (names_g, started) in enumerate(reducing):
        arrs = _split_wait(f"grad_a2a_wait_{g}", started, g_x, _all_to_all_plan)
        half = len(names_g)
        for i, (kl, cs, rv) in enumerate(zip(names_g, arrs[:half], arrs[half:])):
            fin[kl[0]] = owner_sum(shard_core_arr, cs, rv, fin[kl[0]], None if kl[0] == "small" else kl[1],
                                   f"owner_sum_{g}_{i}")
    names = list(BIG) + ["small"]
    joined = dict(zip(names, join_halves([fin[k] for k in BIG], fin["small"], "grad_join")))
    sm_red = joined["small"].reshape(-1)

    out_g, out_d, out_m, out_v = {}, {}, {}, {}
    for k in BIG:
        w2 = weights[k].reshape(-1, weights[k].shape[-1])
        g2 = joined[k].reshape(w2.shape)
        dlt, mn, vn, g_out = adamw(w2, g2, mom_m[k].reshape(w2.shape), mom_v[k].reshape(w2.shape), "adamw_" + k,
                                   emit_g=True)
        shp = weights[k].shape
        out_g[k], out_d[k], out_m[k], out_v[k] = g_out.reshape(shp), dlt.reshape(shp), mn.reshape(shp), vn.reshape(shp)
    sm_vals = dict(zip(small_names, _unpack(sm_red, [small_shapes[k] for k in small_names])))
    local_small = {}
    for k in SMALL_REPLICATED:
        local_small[k] = sm_vals[k]
    for k, axis in SMALL_SHARDED:
        size = weights[k].shape[axis]
        local_small[k] = lax.dynamic_slice_in_dim(sm_vals[k], shard * size, size, axis=axis)
    for k in small_names:
        shp = weights[k].shape
        two_d = (-1, shp[-1]) if len(shp) > 1 else (1, -1)
        dlt, mn, vn = adamw(weights[k].reshape(two_d), local_small[k].reshape(two_d), mom_m[k].reshape(two_d),
                            mom_v[k].reshape(two_d), "adamw_" + k)
        out_g[k], out_d[k], out_m[k], out_v[k] = local_small[k], dlt.reshape(shp), mn.reshape(shp), vn.reshape(shp)

    loss = lax.psum(loss11[0, 0], ("x", "y", "c"))
    return (loss, g_x[None], *[out_g[k] for k in WEIGHT_ORDER], *[out_d[k] for k in WEIGHT_ORDER],
            *[out_m[k] for k in WEIGHT_ORDER], *[out_v[k] for k in WEIGHT_ORDER])
```

```python
import functools
import math

import jax
import jax.numpy as jnp
from jax import lax
from jax.experimental import pallas as pl
from jax.experimental.pallas import tpu as pltpu

F32 = jnp.float32
BF16 = jnp.bfloat16

LRU_C = 8.0
LN_EPS = 1e-5
ADAM_LR = 0.001
ADAM_B1 = 0.9
ADAM_B2 = 0.999
ADAM_EPS = 1e-08
ADAM_WD = 0.01
ADAM_STEP = 10

LANES = 128
SUBLANES = 8
V7X_VMEM_BYTES = 64 * 1024 * 1024
VMEM_LIMIT = V7X_VMEM_BYTES * 7 // 8
N_SHARDS = 4
GELU_C0 = math.sqrt(2.0 / math.pi)
GELU_C1 = 0.044715
NEG_BIG = -1e30


def _cp(*sem):
    return pltpu.CompilerParams(dimension_semantics=tuple(sem), vmem_limit_bytes=VMEM_LIMIT)


def _tile(n, cap, mult=LANES):
    best = None
    d = mult
    while d <= min(n, cap):
        if n % d == 0:
            best = d
        d += mult
    return n if best is None else best


def _row_block(t):
    if t % 3 == 0 and (t // 3) % 16 == 0:
        return t // 3
    return t


def _round_up(n, m):
    return (n + m - 1) // m * m


def _sigmoid(v):
    return 1.0 / (1.0 + jnp.exp(-v))


def _softplus(v):
    return jnp.maximum(v, 0.0) + jnp.log(1.0 + jnp.exp(-jnp.abs(v)))


def _gelu_parts(v):
    v2 = v * v
    u = GELU_C0 * (v + GELU_C1 * v * v2)
    t = jnp.tanh(u)
    g = 0.5 * v * (1.0 + t)
    dg = 0.5 * (1.0 + t) + 0.5 * v * (1.0 - t * t) * (GELU_C0 * (1.0 + 3.0 * GELU_C1 * v2))
    return g, dg


def _gelu(v):
    u = GELU_C0 * (v + GELU_C1 * v * v * v)
    return 0.5 * v * (1.0 + jnp.tanh(u))


def _neg_expm1(v):
    series = -v * (1.0 + 0.5 * v * (1.0 + (v / 3.0) * (1.0 + 0.25 * v)))
    return jnp.where(v > -0.05, series, 1.0 - jnp.exp(v))


def _shift_down(v, j):
    if j == 0:
        return v
    rows = lax.broadcasted_iota(jnp.int32, v.shape, 0)
    return jnp.where(rows >= j, pltpu.roll(v, j, 0), 0.0)


def _shift_up(v, j):
    if j == 0:
        return v
    n = v.shape[0]
    rows = lax.broadcasted_iota(jnp.int32, v.shape, 0)
    return jnp.where(rows < n - j, pltpu.roll(v, n - j, 0), 0.0)


def _scan_rows(a_ref, b_ref, out_ref, n_rows, width, reverse):
    n_groups = n_rows // SUBLANES
    rows = lax.broadcasted_iota(jnp.int32, (SUBLANES, width), 0)
    edge = 0 if reverse else SUBLANES - 1

    def body(g, carry):
        grp = (n_groups - 1 - g) if reverse else g
        off = pl.multiple_of(grp * SUBLANES, SUBLANES)
        b = b_ref[pl.ds(off, SUBLANES), :]
        a = None if a_ref is None else a_ref[pl.ds(off, SUBLANES), :]
        for d in (1, 2, 4):
            if reverse:
                keep = rows < SUBLANES - d
                sh = SUBLANES - d
            else:
                keep = rows >= d
                sh = d
            b_s = jnp.where(keep, pltpu.roll(b, sh, 0), 0.0)
            if a is None:
                b = b + b_s
            else:
                a_s = jnp.where(keep, pltpu.roll(a, sh, 0), 1.0)
                b = a * b_s + b
                a = a * a_s
        h = b + carry if a is None else b + a * carry
        out_ref[pl.ds(off, SUBLANES), :] = h
        return jnp.sum(jnp.where(rows == edge, h, 0.0), axis=0, keepdims=True)

    lax.fori_loop(0, n_groups, body, jnp.zeros((1, width), F32), unroll=2)


def mm_in(x, w, out_dtype, name):
    t, k = x.shape
    s_n, _, ns = w.shape
    tn = _tile(ns, 1408)
    nj = ns // tn
    rb = _row_block(t)

    def body(x_ref, w_ref, o_ref):
        o_ref[...] = jnp.dot(x_ref[...], w_ref[...], preferred_element_type=F32).astype(o_ref.dtype)

    return pl.pallas_call(
        body, name=name, grid=(s_n, nj, t // rb),
        in_specs=[pl.BlockSpec((rb, k), lambda s, j, r: (r, 0)),
                  pl.BlockSpec((None, k, tn), lambda s, j, r: (s, 0, j))],
        out_specs=pl.BlockSpec((rb, tn), lambda s, j, r: (r, s * nj + j)),
        out_shape=jax.ShapeDtypeStruct((t, s_n * ns), out_dtype),
        compiler_params=_cp("parallel", "parallel", "parallel"))(x, w)


def _part_map(p, per_part, nj, lead):
    def index(*grid):
        s, j = grid[-2], grid[-1]
        mine = s // per_part == p
        col = jnp.where(mine, (s - p * per_part) * nj + j, 0)
        return (grid[0], col) if lead else (0, col)
    return index


def mm_in_nt(dy_parts, w, name):
    n_parts = len(dy_parts)
    t = dy_parts[0].shape[0]
    s_n, k, ns = w.shape
    per_part = s_n // n_parts
    tn = _tile(ns, 1408)
    nj = ns // tn
    rb = _row_block(t)

    def body(*refs):
        w_ref, o_ref = refs[n_parts:]

        @pl.when((pl.program_id(1) == 0) & (pl.program_id(2) == 0))
        def _():
            o_ref[...] = jnp.zeros_like(o_ref)

        for p in range(n_parts):
            @pl.when(pl.program_id(1) // per_part == p)
            def _():
                o_ref[...] += lax.dot_general(refs[p][...], w_ref[...], (((1,), (1,)), ((), ())),
                                              preferred_element_type=F32)

    return pl.pallas_call(
        body, name=name, grid=(t // rb, s_n, nj),
        in_specs=[pl.BlockSpec((rb, tn), _part_map(p, per_part, nj, True)) for p in range(n_parts)]
        + [pl.BlockSpec((None, k, tn), lambda r, s, j: (s, 0, j))],
        out_specs=pl.BlockSpec((rb, k), lambda r, s, j: (r, 0)),
        out_shape=jax.ShapeDtypeStruct((t, k), F32),
        compiler_params=_cp("parallel", "arbitrary", "arbitrary"))(*dy_parts, w)


def mm_out_nt(dy, w, name):
    t, n = dy.shape
    k = w.shape[0]
    rb = _row_block(t)

    def body(dy_ref, w_ref, o_ref):
        o_ref[...] = lax.dot_general(dy_ref[...], w_ref[...], (((1,), (1,)), ((), ())),
                                     preferred_element_type=F32)

    return pl.pallas_call(
        body, name=name, grid=(t // rb,),
        in_specs=[pl.BlockSpec((rb, n), lambda r: (r, 0)), pl.BlockSpec((k, n), lambda r: (0, 0))],
        out_specs=pl.BlockSpec((rb, k), lambda r: (r, 0)),
        out_shape=jax.ShapeDtypeStruct((t, k), F32),
        compiler_params=_cp("parallel"))(dy, w)


def mm_tn(x, dy_parts, s_n, name):
    n_parts = len(dy_parts)
    t, kb = x.shape
    nb = dy_parts[0].shape[1] * n_parts // s_n
    per_part = max(s_n // n_parts, 1)
    tk = _tile(kb, 1408)
    tn = _tile(nb, 1408)
    nkb, nnb = kb // tk, nb // tn
    tn_dims = (((0,), (0,)), ((), ()))

    def body(*refs):
        x_ref, o_ref = refs[0], refs[-1]
        for p in range(n_parts):
            @pl.when(pl.program_id(1) // per_part == p)
            def _():
                o_ref[...] = lax.dot_general(x_ref[...], refs[1 + p][...], tn_dims,
                                             preferred_element_type=F32).astype(o_ref.dtype)

    return pl.pallas_call(
        body, name=name, grid=(nkb, s_n, nnb),
        in_specs=[pl.BlockSpec((t, tk), lambda a, s, b: (0, a))]
        + [pl.BlockSpec((t, tn), _part_map(p, per_part, nnb, False)) for p in range(n_parts)],
        out_specs=pl.BlockSpec((None, tk, tn), lambda a, s, b: (s, a, b)),
        out_shape=jax.ShapeDtypeStruct((s_n, kb, nb), BF16),
        compiler_params=_cp("parallel", "parallel", "parallel"))(x, *dy_parts)


def mm_bd(x, wr, wi, name):
    t, _ = x.shape
    g_n, gs, _ = wr.shape
    rb = _row_block(t)

    def body(x_ref, wr_ref, wi_ref, r_ref, i_ref):
        xv = x_ref[...]
        r_ref[...] = jnp.dot(xv, wr_ref[...], preferred_element_type=F32)
        i_ref[...] = jnp.dot(xv, wi_ref[...], preferred_element_type=F32)

    blk = pl.BlockSpec((rb, gs), lambda g, r: (r, g))
    wspec = pl.BlockSpec((None, gs, gs), lambda g, r: (g, 0, 0))
    return pl.pallas_call(
        body, name=name, grid=(g_n, t // rb), in_specs=[blk, wspec, wspec], out_specs=[blk, blk],
        out_shape=[jax.ShapeDtypeStruct((t, g_n * gs), F32)] * 2,
        compiler_params=_cp("parallel", "parallel"))(x, wr, wi)


def mm_bd_nt(dr, di, wr, wi, name):
    t, _ = dr.shape
    g_n, gs, _ = wr.shape
    rb = _row_block(t)
    nt = (((1,), (1,)), ((), ()))

    def body(dr_ref, di_ref, wr_ref, wi_ref, o_ref):
        o_ref[...] = (lax.dot_general(dr_ref[...], wr_ref[...], nt, preferred_element_type=F32)
                      + lax.dot_general(di_ref[...], wi_ref[...], nt, preferred_element_type=F32))

    blk = pl.BlockSpec((rb, gs), lambda g, r: (r, g))
    wspec = pl.BlockSpec((None, gs, gs), lambda g, r: (g, 0, 0))
    return pl.pallas_call(
        body, name=name, grid=(g_n, t // rb), in_specs=[blk, blk, wspec, wspec], out_specs=blk,
        out_shape=jax.ShapeDtypeStruct((t, g_n * gs), F32),
        compiler_params=_cp("parallel", "parallel"))(dr, di, wr, wi)


def mm_bd_tn(x, dr, di, gs, name):
    t, w = x.shape
    g_n = w // gs
    tn_dims = (((0,), (0,)), ((), ()))

    def body(x_ref, dr_ref, di_ref, gr_ref, gi_ref):
        xv = x_ref[...]
        gr_ref[...] = lax.dot_general(xv, dr_ref[...], tn_dims, preferred_element_type=F32)
        gi_ref[...] = lax.dot_general(xv, di_ref[...], tn_dims, preferred_element_type=F32)

    blk = pl.BlockSpec((t, gs), lambda g: (0, g))
    ospec = pl.BlockSpec((None, gs, gs), lambda g: (g, 0, 0))
    return pl.pallas_call(
        body, name=name, grid=(g_n,), in_specs=[blk, blk, blk], out_specs=[ospec, ospec],
        out_shape=[jax.ShapeDtypeStruct((g_n, gs, gs), F32)] * 2,
        compiler_params=_cp("parallel"))(x, dr, di)


def embed_fwd(meta, x2d, name):
    nm, d = meta.shape
    seq = x2d.shape[0]
    t = nm + seq
    cb = _tile(d, 256)

    def body(m_ref, x_ref, h_ref, hb_ref):
        h_ref[pl.ds(0, nm), :] = m_ref[...]
        h_ref[pl.ds(nm, seq), :] = x_ref[...]
        hb_ref[pl.ds(0, nm), :] = m_ref[...].astype(BF16)
        hb_ref[pl.ds(nm, seq), :] = x_ref[...].astype(BF16)

    return pl.pallas_call(
        body, name=name, grid=(d // cb,),
        in_specs=[pl.BlockSpec((nm, cb), lambda j: (0, j)), pl.BlockSpec((seq, cb), lambda j: (0, j))],
        out_specs=[pl.BlockSpec((t, cb), lambda j: (0, j))] * 2,
        out_shape=[jax.ShapeDtypeStruct((t, d), F32), jax.ShapeDtypeStruct((t, d), BF16)],
        compiler_params=_cp("parallel"))(meta, x2d)


def embed_bwd(adds, nm, name):
    t, d = adds[0][0].shape
    seq = t - nm
    cb = _tile(d, 256)
    scales = [s for _, s in adds]
    n = len(adds)

    def body(*refs):
        tot = None
        for r, sc in zip(refs[:n], scales):
            term = r[...] if sc == 1.0 else sc * r[...]
            tot = term if tot is None else tot + term
        gm_ref, gx_ref = refs[n], refs[n + 1]
        gm_ref[...] = tot[0:nm]
        gx_ref[...] = tot[nm:t]

    return pl.pallas_call(
        body, name=name, grid=(d // cb,),
        in_specs=[pl.BlockSpec((t, cb), lambda j: (0, j))] * n,
        out_specs=[pl.BlockSpec((nm, cb), lambda j: (0, j)), pl.BlockSpec((seq, cb), lambda j: (0, j))],
        out_shape=[jax.ShapeDtypeStruct((nm, d), F32), jax.ShapeDtypeStruct((seq, d), F32)],
        compiler_params=_cp("parallel"))(*[a for a, _ in adds])


def loss_fwd_bwd(h, tgt, nm, name):
    t, d = h.shape
    seq = t - nm
    cb = _tile(d, 256)
    inv_d = 1.0 / d

    def body(h_ref, t_ref, loss_ref, dy_ref):
        @pl.when(pl.program_id(0) == 0)
        def _():
            loss_ref[...] = jnp.zeros_like(loss_ref)
        err = h_ref[pl.ds(nm, seq), :] - t_ref[...]
        dy_ref[pl.ds(0, nm), :] = jnp.zeros((nm, cb), F32)
        dy_ref[pl.ds(nm, seq), :] = err * inv_d
        loss_ref[...] += (0.5 * inv_d) * jnp.sum(err * err, keepdims=True)

    return pl.pallas_call(
        body, name=name, grid=(d // cb,),
        in_specs=[pl.BlockSpec((t, cb), lambda j: (0, j)), pl.BlockSpec((seq, cb), lambda j: (0, j))],
        out_specs=[pl.BlockSpec((1, 1), lambda j: (0, 0)), pl.BlockSpec((t, cb), lambda j: (0, j))],
        out_shape=[jax.ShapeDtypeStruct((1, 1), F32), jax.ShapeDtypeStruct((t, d), F32)],
        compiler_params=_cp("arbitrary"))(h, tgt)


def mm_out_ln(x, w, h, g, b, alpha, name):
    t, d = h.shape
    k = x.shape[1]
    rb = _row_block(t)

    def body(x_ref, w_ref, h_ref, g_ref, b_ref, y_ref, yb_ref, xh_ref, rs_ref):
        s = alpha * h_ref[...] + jnp.dot(x_ref[...], w_ref[...], preferred_element_type=F32)
        mu = jnp.mean(s, axis=-1, keepdims=True)
        c = s - mu
        var = jnp.mean(c * c, axis=-1, keepdims=True)
        rstd = lax.rsqrt(var + LN_EPS)
        xh = c * rstd
        y = xh * g_ref[...] + b_ref[...]
        y_ref[...] = y
        yb_ref[...] = y.astype(BF16)
        xh_ref[...] = xh
        rs_ref[...] = rstd

    row = pl.BlockSpec((rb, d), lambda r: (r, 0))
    vec = pl.BlockSpec((1, d), lambda r: (0, 0))
    return pl.pallas_call(
        body, name=name, grid=(t // rb,),
        in_specs=[pl.BlockSpec((rb, k), lambda r: (r, 0)), pl.BlockSpec((k, d), lambda r: (0, 0)), row, vec, vec],
        out_specs=[row, row, row, pl.BlockSpec((rb, 1), lambda r: (r, 0))],
        out_shape=[jax.ShapeDtypeStruct((t, d), F32), jax.ShapeDtypeStruct((t, d), BF16),
                   jax.ShapeDtypeStruct((t, d), F32), jax.ShapeDtypeStruct((t, 1), F32)],
        compiler_params=_cp("parallel"))(x, w, h, g, b)


def ln_bwd(adds, xhat, rstd, g, name):
    t, d = xhat.shape
    rb = _row_block(t)
    scales = [s for _, s in adds]
    n = len(adds)

    def body(*refs):
        xh_ref, rs_ref, g_ref = refs[n:n + 3]
        ds_ref, dsb_ref, dg_ref, db_ref = refs[n + 3:]
        dy = None
        for r, sc in zip(refs[:n], scales):
            term = r[...] if sc == 1.0 else sc * r[...]
            dy = term if dy is None else dy + term

        @pl.when(pl.program_id(0) == 0)
        def _():
            dg_ref[...] = jnp.zeros_like(dg_ref)
            db_ref[...] = jnp.zeros_like(db_ref)

        xh = xh_ref[...]
        dxh = dy * g_ref[...]
        m1 = jnp.mean(dxh, axis=-1, keepdims=True)
        m2 = jnp.mean(dxh * xh, axis=-1, keepdims=True)
        ds = rs_ref[...] * (dxh - m1 - xh * m2)
        ds_ref[...] = ds
        dsb_ref[...] = ds.astype(BF16)
        dg_ref[...] += jnp.sum(dy * xh, axis=0, keepdims=True)
        db_ref[...] += jnp.sum(dy, axis=0, keepdims=True)

    row = pl.BlockSpec((rb, d), lambda r: (r, 0))
    vec = pl.BlockSpec((1, d), lambda r: (0, 0))
    return pl.pallas_call(
        body, name=name, grid=(t // rb,),
        in_specs=[row] * n + [row, pl.BlockSpec((rb, 1), lambda r: (r, 0)), vec],
        out_specs=[row, row, vec, vec],
        out_shape=[jax.ShapeDtypeStruct((t, d), F32), jax.ShapeDtypeStruct((t, d), BF16),
                   jax.ShapeDtypeStruct((1, d), F32), jax.ShapeDtypeStruct((1, d), F32)],
        compiler_params=_cp("arbitrary"))(*[a for a, _ in adds], xhat, rstd, g)


def _conv_fwd_val(xv, w_ref, b_ref, width):
    y = b_ref[...]
    for j in range(width):
        y = y + _shift_down(xv, j) * w_ref[pl.ds(width - 1 - j, 1), :]
    return y


def _conv_bwd_val(dout, xv, w_ref, width):
    dx = None
    dws = [None] * width
    for j in range(width):
        k = width - 1 - j
        term = _shift_up(dout, j) * w_ref[pl.ds(k, 1), :]
        dx = term if dx is None else dx + term
        dws[k] = jnp.sum(dout * _shift_down(xv, j), axis=0, keepdims=True)
    return dx, dws, jnp.sum(dout, axis=0, keepdims=True)


def a_conv_fwd(gr, cw, cbias, name):
    t, two_dr = gr.shape
    dr = two_dr // 2
    width = cw.shape[0]
    cb = _tile(dr, 256)
    off = dr // cb

    def body(x_ref, w_ref, b_ref, rc_ref, rcb_ref):
        y = _conv_fwd_val(x_ref[...], w_ref, b_ref, width)
        rc_ref[...] = y
        rcb_ref[...] = y.astype(BF16)

    return pl.pallas_call(
        body, name=name, grid=(dr // cb,),
        in_specs=[pl.BlockSpec((t, cb), lambda j: (0, off + j)),
                  pl.BlockSpec((width, cb), lambda j: (0, j)), pl.BlockSpec((1, cb), lambda j: (0, j))],
        out_specs=[pl.BlockSpec((t, cb), lambda j: (0, j))] * 2,
        out_shape=[jax.ShapeDtypeStruct((t, dr), F32), jax.ShapeDtypeStruct((t, dr), BF16)],
        compiler_params=_cp("parallel"))(gr, cw, cbias)


def a_conv_bwd(drc_a, drc_b, gr, cw, dgr, name):
    t, two_dr = gr.shape
    dr = two_dr // 2
    width = cw.shape[0]
    cb = _tile(dr, 256)
    off = dr // cb

    def body(da_ref, db_ref, x_ref, w_ref, dgr_in, dx_ref, dw_ref, dbias_ref):
        del dgr_in
        dout = da_ref[...] + db_ref[...]
        dx, dws, dbias = _conv_bwd_val(dout, x_ref[...], w_ref, width)
        dx_ref[...] = dx.astype(BF16)
        for k in range(width):
            dw_ref[pl.ds(k, 1), :] = dws[k]
        dbias_ref[...] = dbias

    col = pl.BlockSpec((t, cb), lambda j: (0, j))
    return pl.pallas_call(
        body, name=name, grid=(dr // cb,),
        in_specs=[col, col, pl.BlockSpec((t, cb), lambda j: (0, off + j)),
                  pl.BlockSpec((width, cb), lambda j: (0, j)), pl.BlockSpec(memory_space=pl.ANY)],
        out_specs=[pl.BlockSpec((t, cb), lambda j: (0, off + j)), pl.BlockSpec((width, cb), lambda j: (0, j)),
                   pl.BlockSpec((1, cb), lambda j: (0, j))],
        out_shape=[jax.ShapeDtypeStruct((t, two_dr), BF16), jax.ShapeDtypeStruct((width, dr), F32),
                   jax.ShapeDtypeStruct((1, dr), F32)],
        input_output_aliases={4: 0},
        compiler_params=_cp("parallel"))(drc_a, drc_b, gr, cw, dgr)


def _lru_gates(r_pre, i_pre, br, bi, lam):
    r = _sigmoid(r_pre + br)
    i = _sigmoid(i_pre + bi)
    sp = _softplus(-lam)
    la = -LRU_C * r * sp
    a = jnp.exp(la)
    m = jnp.sqrt(_neg_expm1(2.0 * la))
    return r, i, sp, la, a, m


def a_elem_fwd(gr, rc, r_pre, i_pre, br, bi, lam, name):
    t, dr = rc.shape
    cb = _tile(dr, 2 * LANES)
    rb = _row_block(t)
    chunks = [pl.ds(r * rb, rb) for r in range(t // rb)]

    def body(gate_ref, rc_ref, rp_ref, ip_ref, br_ref, bi_ref, lam_ref, hs_ref, g_ref, a_s, u_s):
        for rows in chunks:
            _, i, _, _, a, m = _lru_gates(rp_ref[rows, :], ip_ref[rows, :], br_ref[...], bi_ref[...], lam_ref[...])
            a_s[rows, :] = a
            u_s[rows, :] = m * (i * rc_ref[rows, :])
        _scan_rows(a_s, u_s, hs_ref, t, cb, reverse=False)
        for rows in chunks:
            g_ref[rows, :] = (_gelu(gate_ref[rows, :]) * hs_ref[rows, :]).astype(BF16)

    col = pl.BlockSpec((t, cb), lambda j: (0, j))
    vec = pl.BlockSpec((1, cb), lambda j: (0, j))
    return pl.pallas_call(
        body, name=name, grid=(dr // cb,),
        in_specs=[col, col, col, col, vec, vec, vec],
        out_specs=[col, col],
        out_shape=[jax.ShapeDtypeStruct((t, dr), F32), jax.ShapeDtypeStruct((t, dr), BF16)],
        scratch_shapes=[pltpu.VMEM((t, cb), F32), pltpu.VMEM((t, cb), F32)],
        compiler_params=_cp("parallel"))(gr, rc, r_pre, i_pre, br, bi, lam)


def a_elem_bwd(dg, gr, rc, r_pre, i_pre, hs, br, bi, lam, name):
    t, dr = rc.shape
    cb = _tile(dr, 2 * LANES)
    rb = _row_block(t)
    chunks = [pl.ds(r * rb, rb) for r in range(t // rb)]

    def body(dg_ref, gate_ref, rc_ref, rp_ref, ip_ref, hs_ref, br_ref, bi_ref, lam_ref,
             dgate_ref, dr_ref, di_ref, drc_ref, dlam_ref, dbr_ref, dbi_ref, a_s, b_s, g_s, hp_s):
        lamv = lam_ref[...]
        gates = lambda rows: _lru_gates(rp_ref[rows, :], ip_ref[rows, :], br_ref[...], bi_ref[...], lamv)
        for rows in chunks:
            a_s[rows, :] = gates(rows)[4]
            ge, dge = _gelu_parts(gate_ref[rows, :])
            dgv = dg_ref[rows, :]
            dgate_ref[rows, :] = (dgv * hs_ref[rows, :] * dge).astype(BF16)
            b_s[rows, :] = dgv * ge
        a_s[...] = _shift_up(a_s[...], 1)
        hp_s[...] = _shift_down(hs_ref[...], 1)
        _scan_rows(a_s, b_s, g_s, t, cb, reverse=True)
        dsp = dbr = dbi = jnp.zeros((1, cb), F32)
        for rows in chunks:
            r, i, sp, _, a, m = gates(rows)
            rcv = rc_ref[rows, :]
            gsum = g_s[rows, :]
            da = gsum * hp_s[rows, :]
            dm = gsum * (i * rcv)
            d_i = gsum * m * rcv
            drc_ref[rows, :] = gsum * m * i
            dla = a * da - dm * (a * a) / m
            d_r = (-LRU_C) * sp * dla
            dsp = dsp + jnp.sum((-LRU_C) * r * dla, axis=0, keepdims=True)
            d_rp = d_r * r * (1.0 - r)
            d_ip = d_i * i * (1.0 - i)
            dr_ref[rows, :] = d_rp.astype(BF16)
            di_ref[rows, :] = d_ip.astype(BF16)
            dbr = dbr + jnp.sum(d_rp, axis=0, keepdims=True)
            dbi = dbi + jnp.sum(d_ip, axis=0, keepdims=True)
        dlam_ref[...] = -dsp * _sigmoid(-lamv)
        dbr_ref[...] = dbr
        dbi_ref[...] = dbi

    col = pl.BlockSpec((t, cb), lambda j: (0, j))
    vec = pl.BlockSpec((1, cb), lambda j: (0, j))
    big_b = jax.ShapeDtypeStruct((t, dr), BF16)
    vec_s = jax.ShapeDtypeStruct((1, dr), F32)
    return pl.pallas_call(
        body, name=name, grid=(dr // cb,),
        in_specs=[col, col, col, col, col, col, vec, vec, vec],
        out_specs=[col, col, col, col, vec, vec, vec],
        out_shape=[jax.ShapeDtypeStruct((t, 2 * dr), BF16), big_b, big_b, jax.ShapeDtypeStruct((t, dr), F32),
                   vec_s, vec_s, vec_s],
        scratch_shapes=[pltpu.VMEM((t, cb), F32)] * 4,
        compiler_params=_cp("parallel"))(dg, gr, rc, r_pre, i_pre, hs, br, bi, lam)


def f_elem_fwd(z, cw, cbias, name):
    t, two_f = z.shape
    dff = two_f // 2
    width = cw.shape[0]
    cb = _tile(dff, 256)
    off = dff // cb

    def body(zg_ref, zv_ref, wg_ref, wv_ref, bg_ref, bv_ref, o_ref):
        zcg = _conv_fwd_val(zg_ref[...], wg_ref, bg_ref, width)
        zcv = _conv_fwd_val(zv_ref[...], wv_ref, bv_ref, width)
        o_ref[...] = (_gelu(zcg) * zcv).astype(BF16)

    lo = lambda j: (0, j)
    hi = lambda j: (0, off + j)
    return pl.pallas_call(
        body, name=name, grid=(dff // cb,),
        in_specs=[pl.BlockSpec((t, cb), lo), pl.BlockSpec((t, cb), hi),
                  pl.BlockSpec((width, cb), lo), pl.BlockSpec((width, cb), hi),
                  pl.BlockSpec((1, cb), lo), pl.BlockSpec((1, cb), hi)],
        out_specs=pl.BlockSpec((t, cb), lo),
        out_shape=jax.ShapeDtypeStruct((t, dff), BF16),
        compiler_params=_cp("parallel"))(z, z, cw, cw, cbias, cbias)


def f_elem_bwd(z, dff_g, cw, cbias, name):
    t, two_f = z.shape
    dff = two_f // 2
    width = cw.shape[0]
    cb = _tile(dff, 256)
    off = dff // cb

    def body(zg_ref, zv_ref, d_ref, wg_ref, wv_ref, bg_ref, bv_ref,
             dzg_ref, dzv_ref, dwg_ref, dwv_ref, dbg_ref, dbv_ref):
        zg = zg_ref[...]
        zv = zv_ref[...]
        zcg = _conv_fwd_val(zg, wg_ref, bg_ref, width)
        zcv = _conv_fwd_val(zv, wv_ref, bv_ref, width)
        ge, dge = _gelu_parts(zcg)
        dv = d_ref[...]
        dx, dws, dbias = _conv_bwd_val(dv * zcv * dge, zg, wg_ref, width)
        dzg_ref[...] = dx.astype(BF16)
        for k in range(width):
            dwg_ref[pl.ds(k, 1), :] = dws[k]
        dbg_ref[...] = dbias
        dx, dws, dbias = _conv_bwd_val(dv * ge, zv, wv_ref, width)
        dzv_ref[...] = dx.astype(BF16)
        for k in range(width):
            dwv_ref[pl.ds(k, 1), :] = dws[k]
        dbv_ref[...] = dbias

    lo = lambda j: (0, j)
    hi = lambda j: (0, off + j)
    col = pl.BlockSpec((t, cb), lo)
    wsp = pl.BlockSpec((width, cb), lo)
    vsp = pl.BlockSpec((1, cb), lo)
    return pl.pallas_call(
        body, name=name, grid=(dff // cb,),
        in_specs=[col, pl.BlockSpec((t, cb), hi), col, wsp, pl.BlockSpec((width, cb), hi),
                  vsp, pl.BlockSpec((1, cb), hi)],
        out_specs=[col, col, wsp, wsp, vsp, vsp],
        out_shape=[jax.ShapeDtypeStruct((t, dff), BF16)] * 2
        + [jax.ShapeDtypeStruct((width, dff), F32)] * 2 + [jax.ShapeDtypeStruct((1, dff), F32)] * 2,
        compiler_params=_cp("parallel"))(z, z, dff_g, cw, cw, cbias, cbias)


def kv_fwd(z, fb, d_model, name):
    t, _ = z.shape
    blk = 2 * d_model // LANES

    def body(z_ref, fb_ref, c_ref, lf_s):
        v = z_ref[...] + fb_ref[...]
        lf_s[...] = -_softplus(-v)
        _scan_rows(None, lf_s, c_ref, t, LANES, reverse=False)

    return pl.pallas_call(
        body, name=name, grid=(1,),
        in_specs=[pl.BlockSpec((t, LANES), lambda j: (0, blk)), pl.BlockSpec((1, LANES), lambda j: (0, 0))],
        out_specs=pl.BlockSpec((t, LANES), lambda j: (0, 0)),
        out_shape=jax.ShapeDtypeStruct((t, LANES), F32),
        scratch_shapes=[pltpu.VMEM((t, LANES), F32)],
        compiler_params=_cp("arbitrary"))(z, fb)


def kv_bwd(dcs, z, fb, d_model, name):
    t, _ = z.shape
    blk = 2 * d_model // LANES
    n = len(dcs)

    def body(*refs):
        z_ref, fb_ref, dz_ref, dfb_ref, dc_s, dl_s = refs[n:]
        tot = refs[0][...]
        for r in refs[1:n]:
            tot = tot + r[...]
        dc_s[...] = tot
        _scan_rows(None, dc_s, dl_s, t, LANES, reverse=True)
        v = z_ref[...] + fb_ref[...]
        dz = dl_s[...] * _sigmoid(-v)
        dz_ref[...] = dz.astype(BF16)
        dfb_ref[...] = jnp.sum(dz, axis=0, keepdims=True)

    full = pl.BlockSpec((t, LANES), lambda j: (0, 0))
    return pl.pallas_call(
        body, name=name, grid=(1,),
        in_specs=[full] * n + [pl.BlockSpec((t, LANES), lambda j: (0, blk)),
                               pl.BlockSpec((1, LANES), lambda j: (0, 0))],
        out_specs=[full, pl.BlockSpec((1, LANES), lambda j: (0, 0))],
        out_shape=[jax.ShapeDtypeStruct((t, LANES), BF16), jax.ShapeDtypeStruct((1, LANES), F32)],
        scratch_shapes=[pltpu.VMEM((t, LANES), F32)] * 2,
        compiler_params=_cp("arbitrary"))(*dcs, z, fb)


def add_cast(a, b, name):
    t, d = a.shape
    cb = _tile(d, 512)

    def body(a_ref, b_ref, o_ref):
        o_ref[...] = (a_ref[...] + b_ref[...]).astype(BF16)

    col = pl.BlockSpec((t, cb), lambda j: (0, j))
    return pl.pallas_call(body, name=name, grid=(d // cb,), in_specs=[col, col], out_specs=col,
                          out_shape=jax.ShapeDtypeStruct((t, d), BF16),
                          compiler_params=_cp("parallel"))(a, b)


def _attn_geometry(t):
    nqb = 6 if t > 1024 else 2
    tp = _round_up(t, LANES * nqb)
    return nqb, tp, tp // nqb


def _attn_scales(dh):
    scale = dh ** -0.5
    if math.log2(scale).is_integer():
        return scale, 1.0
    return 1.0, scale


def _attn_pieces(qs, ks, crow, j, i, tq, dh, s_mul):
    r0 = i * tq
    lanes = pl.ds(j * dh, dh)
    qi = qs[pl.ds(r0, tq), lanes]
    spans = ([(0, r0)] if i > 0 else []) + [(r0, tq)]
    logits = []
    for k0, n in spans:
        s = lax.dot_general(qi, ks[pl.ds(k0, n), lanes], (((1,), (1,)), ((), ())),
                            preferred_element_type=F32)
        if s_mul != 1.0:
            s = s * s_mul
        s = s - crow[:, k0:k0 + n]
        if k0 == r0:
            rows = lax.broadcasted_iota(jnp.int32, (tq, tq), 0)
            cols = lax.broadcasted_iota(jnp.int32, (tq, tq), 1)
            s = jnp.where(cols <= rows, s, NEG_BIG)
        logits.append(s)
    mx = jnp.max(logits[0], axis=1, keepdims=True)
    for s in logits[1:]:
        mx = jnp.maximum(mx, jnp.max(s, axis=1, keepdims=True))
    es = [jnp.exp(s - mx) for s in logits]
    tot = jnp.sum(es[0], axis=1, keepdims=True)
    for e in es[1:]:
        tot = tot + jnp.sum(e, axis=1, keepdims=True)
    inv = 1.0 / tot
    return [(k0, n, e * inv) for (k0, n), e in zip(spans, es)], qi


def attn_fwd(qg, z, ct_pad, d_model, n_heads, name):
    t = qg.shape[0]
    dh = d_model // n_heads
    hp = LANES // dh
    nqb, tp, tq = _attn_geometry(t)
    nblk = d_model // LANES
    q_mul, s_mul = _attn_scales(dh)

    def body(q_ref, og_ref, k_ref, v_ref, ct_ref, o_ref, mo_ref, qs, ks, vs, os_):
        pad = jnp.zeros((tp - t, LANES), BF16)
        qs[pl.ds(0, t), :] = (q_ref[...] * q_mul).astype(BF16)
        qs[pl.ds(t, tp - t), :] = pad
        for src, dst in ((k_ref, ks), (v_ref, vs)):
            dst[pl.ds(0, t), :] = src[...].astype(BF16)
            dst[pl.ds(t, tp - t), :] = pad
        for j in range(hp):
            crow = ct_ref[j]
            lanes = pl.ds(j * dh, dh)
            for i in range(nqb):
                pieces, _ = _attn_pieces(qs, ks, crow, j, i, tq, dh, s_mul)
                acc = None
                for k0, n, p in pieces:
                    part = jnp.dot(p.astype(BF16), vs[pl.ds(k0, n), lanes], preferred_element_type=F32)
                    acc = part if acc is None else acc + part
                os_[pl.ds(i * tq, tq), lanes] = acc
        o = os_[pl.ds(0, t), :]
        o_ref[...] = o
        mo_ref[...] = (o * _sigmoid(og_ref[...])).astype(BF16)

    col = lambda off: pl.BlockSpec((t, LANES), lambda p: (0, off + p))
    return pl.pallas_call(
        body, name=name, grid=(nblk,),
        in_specs=[col(0), col(nblk), col(0), col(nblk), pl.BlockSpec((hp, 1, tp), lambda p: (p, 0, 0))],
        out_specs=[col(0), col(0)],
        out_shape=[jax.ShapeDtypeStruct((t, d_model), F32), jax.ShapeDtypeStruct((t, d_model), BF16)],
        scratch_shapes=[pltpu.VMEM((tp, LANES), BF16)] * 3 + [pltpu.VMEM((tp, LANES), F32)],
        compiler_params=_cp("parallel"))(qg, qg, z, z, ct_pad)


def attn_bwd(dmo, qg, z, o, ct_pad, d_model, n_heads, name):
    t = qg.shape[0]
    dh = d_model // n_heads
    hp = LANES // dh
    nqb, tp, tq = _attn_geometry(t)
    nblk = d_model // LANES
    q_mul, s_mul = _attn_scales(dh)
    scale = dh ** -0.5
    tn_dims = (((0,), (0,)), ((), ()))
    nt_dims = (((1,), (1,)), ((), ()))

    def body(dmo_ref, q_ref, og_ref, k_ref, v_ref, o_ref, ct_ref,
             dq_ref, dog_ref, dk_ref, dv_ref, dct_ref, qs, ks, vs, dos, dqs, dks, dvs):
        pad = jnp.zeros((tp - t, LANES), BF16)
        sg = _sigmoid(og_ref[...])
        dmo_v = dmo_ref[...]
        dog_ref[...] = (dmo_v * o_ref[...] * sg * (1.0 - sg)).astype(BF16)
        dos[pl.ds(0, t), :] = (dmo_v * sg).astype(BF16)
        dos[pl.ds(t, tp - t), :] = pad
        qs[pl.ds(0, t), :] = (q_ref[...] * q_mul).astype(BF16)
        qs[pl.ds(t, tp - t), :] = pad
        for src, dst in ((k_ref, ks), (v_ref, vs)):
            dst[pl.ds(0, t), :] = src[...].astype(BF16)
            dst[pl.ds(t, tp - t), :] = pad
        dks[...] = jnp.zeros_like(dks)
        dvs[...] = jnp.zeros_like(dvs)
        dct_ref[...] = jnp.zeros_like(dct_ref)
        for j in range(hp):
            crow = ct_ref[j]
            lanes = pl.ds(j * dh, dh)
            for i in range(nqb):
                pieces, qi = _attn_pieces(qs, ks, crow, j, i, tq, dh, s_mul)
                do_i = dos[pl.ds(i * tq, tq), lanes]
                dps = [lax.dot_general(do_i, vs[pl.ds(k0, n), lanes], nt_dims, preferred_element_type=F32)
                       for k0, n, _ in pieces]
                row = None
                for (_, _, p), dp in zip(pieces, dps):
                    part = jnp.sum(p * dp, axis=1, keepdims=True)
                    row = part if row is None else row + part
                dq_i = None
                for (k0, n, p), dp in zip(pieces, dps):
                    ds = p * (dp - row)
                    ds_b = ds.astype(BF16)
                    keys = pl.ds(k0, n)
                    part = jnp.dot(ds_b, ks[keys, lanes], preferred_element_type=F32)
                    dq_i = part if dq_i is None else dq_i + part
                    dks[keys, lanes] += lax.dot_general(ds_b, qi, tn_dims, preferred_element_type=F32) * s_mul
                    dvs[keys, lanes] += lax.dot_general(p.astype(BF16), do_i, tn_dims,
                                                        preferred_element_type=F32)
                    dct_ref[j, :, keys] -= jnp.sum(ds, axis=0, keepdims=True)
                dqs[pl.ds(i * tq, tq), lanes] = dq_i * scale
        dq_ref[...] = dqs[pl.ds(0, t), :].astype(BF16)
        dk_ref[...] = dks[pl.ds(0, t), :]
        dv_ref[...] = dvs[pl.ds(0, t), :]

    col = lambda off: pl.BlockSpec((t, LANES), lambda p: (0, off + p))
    big = lambda dt: jax.ShapeDtypeStruct((t, d_model), dt)
    return pl.pallas_call(
        body, name=name, grid=(nblk,),
        in_specs=[col(0), col(0), col(nblk), col(0), col(nblk), col(0),
                  pl.BlockSpec((hp, 1, tp), lambda p: (p, 0, 0))],
        out_specs=[col(0), col(0), col(0), col(0), pl.BlockSpec((hp, 1, tp), lambda p: (p, 0, 0))],
        out_shape=[big(BF16), big(BF16), big(F32), big(F32),
                   jax.ShapeDtypeStruct((n_heads, 1, tp), F32)],
        scratch_shapes=[pltpu.VMEM((tp, LANES), BF16)] * 4 + [pltpu.VMEM((tp, LANES), F32)] * 3,
        compiler_params=_cp("parallel"))(dmo, qg, qg, z, z, o, ct_pad)


def cast_into_slot(shard, w, index, name):
    r, c = w.shape[-2:]
    rh = r // 2
    tr = _tile(rh, 512, 16)
    n = rh // tr
    if w.ndim == 3:
        w_spec = pl.BlockSpec((None, tr, c), lambda h, i, sh: (index, h * n + i, 0))
    else:
        w_spec = pl.BlockSpec((tr, c), lambda h, i, sh: (h * n + i, 0))

    def body(sh_ref, w_ref, o_ref):
        del sh_ref
        o_ref[...] = w_ref[...].astype(BF16)

    return pl.pallas_call(
        body, name=name,
        grid_spec=pltpu.PrefetchScalarGridSpec(
            num_scalar_prefetch=1, grid=(2, n), in_specs=[w_spec],
            out_specs=pl.BlockSpec((None, None, tr, c), lambda h, i, sh: (sh[0], h, i, 0))),
        out_shape=jax.ShapeDtypeStruct((N_SHARDS, 2, rh, c), BF16),
        compiler_params=_cp("parallel", "parallel"))(shard, w)


def owner_sum(shard_core, g, recv, buf, layer, name):
    _, _, rh, c = g.shape
    n_recv = recv.shape[0]
    tr = _tile(rh, 512, 16)
    slot = (lambda sc: sc[0]) if layer is None else (lambda sc: layer)

    def body(sc_ref, a_ref, r_ref, buf_ref, out_ref):
        del sc_ref, buf_ref
        acc = a_ref[...].astype(F32)
        for k in range(n_recv):
            acc = acc + r_ref[k].astype(F32)
        out_ref[...] = acc

    return pl.pallas_call(
        body, name=name,
        grid_spec=pltpu.PrefetchScalarGridSpec(
            num_scalar_prefetch=1, grid=(rh // tr,),
            in_specs=[pl.BlockSpec((None, None, tr, c), lambda i, sc: (sc[0], sc[1], i, 0)),
                      pl.BlockSpec((n_recv, tr, c), lambda i, sc: (0, i, 0)),
                      pl.BlockSpec(memory_space=pl.ANY)],
            out_specs=pl.BlockSpec((None, None, tr, c), lambda i, sc: (slot(sc), sc[1], i, 0))),
        out_shape=jax.ShapeDtypeStruct(buf.shape, F32),
        input_output_aliases={3: 0},
        compiler_params=_cp("parallel"))(shard_core, g, recv, buf)


def adamw(w, g, m, v, name, emit_g=False):
    r, c = w.shape
    tr = _tile(r, 512, SUBLANES)
    c1 = 1.0 - ADAM_B1 ** ADAM_STEP
    c2 = 1.0 - ADAM_B2 ** ADAM_STEP
    n_out = 4 if emit_g else 3

    def body(w_ref, g_ref, m_ref, v_ref, d_ref, mo_ref, vo_ref, *go_ref):
        gv = g_ref[...]
        if emit_g:
            go_ref[0][...] = gv
        mn = ADAM_B1 * m_ref[...] + (1.0 - ADAM_B1) * gv
        vn = ADAM_B2 * v_ref[...] + (1.0 - ADAM_B2) * (gv * gv)
        m_hat = mn / c1
        v_hat = vn / c2
        d_ref[...] = -ADAM_LR * (m_hat / (jnp.sqrt(v_hat) + ADAM_EPS) + ADAM_WD * w_ref[...])
        mo_ref[...] = mn
        vo_ref[...] = vn

    blk = pl.BlockSpec((tr, c), lambda i: (i, 0))
    return pl.pallas_call(
        body, name=name, grid=(r // tr,), in_specs=[blk] * 4, out_specs=[blk] * n_out,
        out_shape=[jax.ShapeDtypeStruct((r, c), F32)] * n_out,
        compiler_params=_cp("parallel"))(w, g, m, v)


def _coords():
    return lax.axis_index("x"), lax.axis_index("y"), lax.axis_index("c")


def _exchange(name, ins, out_shapes, plan, in_place=False):
    n_in = len(ins)
    n_out = len(out_shapes)
    n_rem = len(plan([None] * n_in, [None] * n_out, True))

    def body(*refs):
        in_refs = refs[:n_in]
        out_refs = refs[n_in:n_in + n_out]
        send_sems, recv_sems = refs[n_in + n_out:]
        copies = [pltpu.make_async_remote_copy(
            src_ref=src, dst_ref=dst, send_sem=send_sems.at[q], recv_sem=recv_sems.at[q],
            device_id=peer, device_id_type=pl.DeviceIdType.MESH)
            for q, (src, dst, peer) in enumerate(plan(list(in_refs), list(out_refs), False))]
        for cp in copies:
            cp.start()
        for cp in copies:
            cp.wait_recv()
        for cp in copies:
            cp.wait_send()

    hbm = pl.BlockSpec(memory_space=pl.ANY)
    return pl.pallas_call(
        body, name=name, in_specs=[hbm] * n_in, out_specs=[hbm] * n_out, out_shape=out_shapes,
        input_output_aliases={i: i for i in range(n_in)} if in_place else {},
        scratch_shapes=[pltpu.SemaphoreType.DMA((n_rem,)), pltpu.SemaphoreType.DMA((n_rem,))],
        compiler_params=pltpu.CompilerParams(has_side_effects=True))(*ins)


def _split_start(name, groups, plan):
    flat = [a for grp in groups for a in grp]
    n, n_grp = len(flat), len(groups)
    counts = [len(plan(g, [None] * len(grp), True)) for g, grp in enumerate(groups)]

    def body(*refs):
        ins, sems, token = refs[:n], refs[n:n + 2 * n_grp], refs[-1]
        pos = 0
        for g, grp in enumerate(groups):
            arrs = list(ins[pos:pos + len(grp)])
            pos += len(grp)
            for q, (src, dst, peer) in enumerate(plan(g, arrs, False)):
                pltpu.make_async_remote_copy(
                    src_ref=src, dst_ref=dst, send_sem=sems[2 * g].at[q], recv_sem=sems[2 * g + 1].at[q],
                    device_id=peer, device_id_type=pl.DeviceIdType.MESH).start()
        token[...] = jnp.zeros_like(token)

    hbm = pl.BlockSpec(memory_space=pltpu.HBM)
    sem = pl.BlockSpec(memory_space=pltpu.SEMAPHORE)
    outs = pl.pallas_call(
        body, name=name,
        out_shape=[pltpu.SemaphoreType.DMA((cnt,)) for cnt in counts for _ in range(2)]
        + [pltpu.HBM(a.shape, a.dtype) for a in flat] + [jax.ShapeDtypeStruct((SUBLANES, LANES), F32)],
        in_specs=[hbm] * n, out_specs=[sem] * (2 * n_grp) + [hbm] * n + [pl.BlockSpec(memory_space=pltpu.VMEM)],
        input_output_aliases={i: 2 * n_grp + i for i in range(n)},
        compiler_params=pltpu.CompilerParams(has_side_effects=pltpu.SideEffectType.DATAFLOW_SIDE_EFFECTING),
    )(*[pltpu.with_memory_space_constraint(a, pltpu.HBM) for a in flat])
    started, pos = [], 2 * n_grp
    for g, grp in enumerate(groups):
        started.append((outs[2 * g], outs[2 * g + 1], list(outs[pos:pos + len(grp)])))
        pos += len(grp)
    return started, outs[-1]


def _split_wait(name, started, after, plan_g):
    send_sems, recv_sems, arrs = started
    n = len(arrs)

    def body(*refs):
        ins, ssem, rsem = list(refs[:n]), refs[n], refs[n + 1]
        for q, (src, dst, peer) in enumerate(plan_g(ins, False)):
            cp = pltpu.make_async_remote_copy(
                src_ref=src, dst_ref=dst, send_sem=ssem.at[q], recv_sem=rsem.at[q],
                device_id=peer, device_id_type=pl.DeviceIdType.MESH)
            cp.wait_send()
            cp.wait_recv()

    hbm = pl.BlockSpec(memory_space=pltpu.HBM)
    sem = pl.BlockSpec(memory_space=pltpu.SEMAPHORE)
    return pl.pallas_call(
        body, name=name, out_shape=[pltpu.HBM(a.shape, a.dtype) for a in arrs],
        in_specs=[hbm] * n + [sem, sem, pl.BlockSpec(memory_space=pl.ANY)], out_specs=[hbm] * n,
        input_output_aliases={i: i for i in range(n)},
        compiler_params=pltpu.CompilerParams(has_side_effects=pltpu.SideEffectType.DATAFLOW_SIDE_EFFECTING),
    )(*arrs, send_sems, recv_sems, after)


def _gather_ici_plan(arrs, count_only):
    if count_only:
        return [None] * (3 * len(arrs))
    x, y, c = _coords()
    pushes = []
    for a in arrs:
        mine = a.at[2 * x + y, c]
        pushes += [(mine, mine, peer) for peer, _ in _other_chips(x, y, c)]
    return pushes


def _all_to_all_plan(arrs, count_only):
    half = len(arrs) // 2
    if count_only:
        return [None] * (7 * half)
    x, y, c = _coords()
    pushes = []
    for src, land in zip(arrs[:half], arrs[half:]):
        for flips in range(1, 8):
            px, py, pc = (1 - x if flips & 4 else x), (1 - y if flips & 2 else y), (1 - c if flips & 1 else c)
            pushes.append((src.at[2 * px + py, pc], land.at[flips - 1], (px, py, pc)))
    return pushes


def forward_to_sibling(bufs, name):
    n = len(bufs)

    def plan(ins, outs, count_only):
        if count_only:
            return [None] * (3 * n)
        x, y, c = _coords()
        pushes = []
        for i in range(n):
            for _, src_shard in _other_chips(x, y, c):
                slab = outs[i].at[src_shard, c]
                pushes.append((slab, slab, (x, y, 1 - c)))
        return pushes

    shapes = [jax.ShapeDtypeStruct(b.shape, b.dtype) for b in bufs]
    return _exchange(name, bufs, shapes, plan, in_place=True)


def _other_chips(x, y, c):
    return [((1 - x, y, c), 2 * (1 - x) + y), ((x, 1 - y, c), 2 * x + 1 - y),
            ((1 - x, 1 - y, c), 2 * (1 - x) + 1 - y)]


def join_halves(bufs, name):
    slots = [(i, l) for i, b in enumerate(bufs) for l in range(b.shape[0])]

    def plan(ins, outs, count_only):
        if count_only:
            return [None] * len(slots)
        x, y, c = _coords()
        return [(outs[i].at[l, c], outs[i].at[l, c], (x, y, 1 - c)) for i, l in slots]

    shapes = [jax.ShapeDtypeStruct(b.shape, b.dtype) for b in bufs]
    return _exchange(name, list(bufs), shapes, plan, in_place=True)


def spread_eighth(buf, name):
    def plan(ins, outs, count_only):
        if count_only:
            return [None] * 7
        x, y, c = _coords()
        mine = outs[0].at[2 * x + y, c]
        pushes = []
        for flips in range(1, 8):
            peer = (1 - x if flips & 4 else x, 1 - y if flips & 2 else y, 1 - c if flips & 1 else c)
            pushes.append((mine, mine, peer))
        return pushes

    return _exchange(name, [buf], [jax.ShapeDtypeStruct(buf.shape, buf.dtype)], plan, in_place=True)[0]


def _pack(arrays, multiple):
    flat = jnp.concatenate([a.reshape(-1) for a in arrays])
    n = flat.shape[0]
    return jnp.pad(flat, (0, _round_up(n, multiple) - n))


def _unpack(flat, shapes):
    out, pos = [], 0
    for shp in shapes:
        n = math.prod(shp)
        out.append(flat[pos:pos + n].reshape(shp))
        pos += n
    return out


def _block_diag(w, per_group):
    nb, bs, _ = w.shape
    g = nb // per_group
    w4 = w.reshape(g, per_group, bs, bs)
    eye = jnp.eye(per_group, dtype=w.dtype)
    full = w4[:, :, :, None, :] * eye[None, :, None, :, None]
    return full.reshape(g, per_group * bs, per_group * bs).astype(BF16)


def _block_diag_extract(full, per_group, bs):
    g = full.shape[0]
    f5 = full.reshape(g, per_group, bs, per_group, bs)
    idx = jnp.arange(per_group)
    picked = f5[:, idx, :, idx, :]
    return jnp.moveaxis(picked, 0, 1).reshape(g * per_group, bs, bs)


def kernel(x, meta, a_w_in, a_conv_w, a_conv_b, a_w_r, a_b_r, a_w_i, a_b_i, a_lambda, a_w_out, kv_w, kv_f_b, b_w_in, b_w_out, f_w_in, f_conv_w, f_conv_b, f_w_out, ln1_g, ln1_b, ln2_g, ln2_b, loss_target, m_meta, m_a_w_in, m_a_conv_w, m_a_conv_b, m_a_w_r, m_a_b_r, m_a_w_i, m_a_b_i, m_a_lambda, m_a_w_out, m_kv_w, m_kv_f_b, m_b_w_in, m_b_w_out, m_f_w_in, m_f_conv_w, m_f_conv_b, m_f_w_out, m_ln1_g, m_ln1_b, m_ln2_g, m_ln2_b, v_meta, v_a_w_in, v_a_conv_w, v_a_conv_b, v_a_w_r, v_a_b_r, v_a_w_i, v_a_b_i, v_a_lambda, v_a_w_out, v_kv_w, v_kv_f_b, v_b_w_in, v_b_w_out, v_f_w_in, v_f_conv_w, v_f_conv_b, v_f_w_out, v_ln1_g, v_ln1_b, v_ln2_g, v_ln2_b):
    weights = dict(meta=meta, a_w_in=a_w_in, a_conv_w=a_conv_w, a_conv_b=a_conv_b, a_w_r=a_w_r, a_b_r=a_b_r,
                   a_w_i=a_w_i, a_b_i=a_b_i, a_lambda=a_lambda, a_w_out=a_w_out, kv_w=kv_w, kv_f_b=kv_f_b,
                   b_w_in=b_w_in, b_w_out=b_w_out, f_w_in=f_w_in, f_conv_w=f_conv_w, f_conv_b=f_conv_b,
                   f_w_out=f_w_out, ln1_g=ln1_g, ln1_b=ln1_b, ln2_g=ln2_g, ln2_b=ln2_b)
    mom_m = dict(meta=m_meta, a_w_in=m_a_w_in, a_conv_w=m_a_conv_w, a_conv_b=m_a_conv_b, a_w_r=m_a_w_r,
                 a_b_r=m_a_b_r, a_w_i=m_a_w_i, a_b_i=m_a_b_i, a_lambda=m_a_lambda, a_w_out=m_a_w_out,
                 kv_w=m_kv_w, kv_f_b=m_kv_f_b, b_w_in=m_b_w_in, b_w_out=m_b_w_out, f_w_in=m_f_w_in,
                 f_conv_w=m_f_conv_w, f_conv_b=m_f_conv_b, f_w_out=m_f_w_out, ln1_g=m_ln1_g, ln1_b=m_ln1_b,
                 ln2_g=m_ln2_g, ln2_b=m_ln2_b)
    mom_v = dict(meta=v_meta, a_w_in=v_a_w_in, a_conv_w=v_a_conv_w, a_conv_b=v_a_conv_b, a_w_r=v_a_w_r,
                 a_b_r=v_a_b_r, a_w_i=v_a_w_i, a_b_i=v_a_b_i, a_lambda=v_a_lambda, a_w_out=v_a_w_out,
                 kv_w=v_kv_w, kv_f_b=v_kv_f_b, b_w_in=v_b_w_in, b_w_out=v_b_w_out, f_w_in=v_f_w_in,
                 f_conv_w=v_f_conv_w, f_conv_b=v_f_conv_b, f_w_out=v_f_w_out, ln1_g=v_ln1_g, ln1_b=v_ln1_b,
                 ln2_g=v_ln2_g, ln2_b=v_ln2_b)
    return _train_step(x, loss_target, weights, mom_m, mom_v)


WEIGHT_ORDER = ("meta", "a_w_in", "a_conv_w", "a_conv_b", "a_w_r", "a_b_r", "a_w_i", "a_b_i", "a_lambda",
                "a_w_out", "kv_w", "kv_f_b", "b_w_in", "b_w_out", "f_w_in", "f_conv_w", "f_conv_b",
                "f_w_out", "ln1_g", "ln1_b", "ln2_g", "ln2_b")
BIG = ("a_w_in", "a_w_out", "kv_w", "b_w_in", "b_w_out", "f_w_in", "f_w_out")
OUT_TYPE = ("a_w_out", "b_w_out", "f_w_out")
SMALL_SHARDED = (("meta", 1), ("a_conv_w", 2), ("a_conv_b", 1), ("a_b_r", 1), ("a_b_i", 1), ("a_lambda", 1),
                 ("f_conv_w", 2))
SMALL_REPLICATED = ("a_w_r", "a_w_i", "kv_f_b", "f_conv_b", "ln1_g", "ln1_b", "ln2_g", "ln2_b")


def _train_step(x, loss_target, weights, mom_m, mom_v):
    S = N_SHARDS
    seq, d = x.shape[1], x.shape[2]
    nm = weights["meta"].shape[0]
    la = weights["a_w_in"].shape[0]
    lb = weights["b_w_in"].shape[0]
    depth = la + lb
    dr = weights["a_w_out"].shape[1] * S
    nb, bs = weights["a_w_r"].shape[1], weights["a_w_r"].shape[2]
    per_group = (LANES // math.gcd(bs, LANES))
    gs = per_group * bs
    heads = weights["kv_f_b"].shape[0]
    dff = weights["f_w_out"].shape[1] * S
    nkv = 2 * d + heads
    nkv_s = weights["kv_w"].shape[1]
    nkvp = _round_up(2 * d + LANES, 768) if 2 * d + LANES > 768 else 2 * d + LANES
    alpha = (2 * depth) ** 0.25
    xi, yi, ci = _coords()
    shard = 2 * xi + yi
    shard_arr = jnp.reshape(shard, (1,)).astype(jnp.int32)
    shard_core_arr = jnp.stack([shard, ci]).astype(jnp.int32)

    def mixer_keys(l):
        if l < la:
            return [("a_w_in", l), ("a_w_out", l)]
        return ([("kv_w", 0)] if l == la else []) + [("b_w_in", l - la), ("b_w_out", l - la)]

    def ffn_keys(l):
        return [("f_w_in", l), ("f_w_out", l)]

    assert la >= 1
    groups = [mixer_keys(0)[:1], mixer_keys(0)[1:] + ffn_keys(0)]
    groups += [mixer_keys(l) + ffn_keys(l) for l in range(1, depth)]
    keys = [kl for grp in groups for kl in grp]
    shape2d = {(k, i): weights[k].shape[-2:] for k, i in keys}
    small_local = [weights[k] for k, _ in SMALL_SHARDED]
    sm_flat = _pack(small_local, 2 * SUBLANES * LANES).reshape(1, 2, -1, LANES)
    sm_slot = lax.dynamic_update_slice_in_dim(lax.empty((S,) + sm_flat.shape[1:], F32), sm_flat, shard, axis=0)
    parts = [[cast_into_slot(shard_arr, weights[k], i, f"cast_{k}{i}") for k, i in grp] for grp in groups]
    parts[0].append(sm_slot)
    in_flight, start_token = _split_start("gather_start", parts, lambda g, refs, cnt: _gather_ici_plan(refs, cnt))
    gw = {}

    def fetch(g, after):
        arrs = _split_wait(f"gather_wait_{g}", in_flight[g], after, _gather_ici_plan)
        arrs = forward_to_sibling(arrs, f"gather_fwd_{g}")
        for kl, a in zip(groups[g], arrs):
            rows, cols = shape2d[kl]
            gw[kl] = a.reshape(S * rows, cols) if kl[0] in OUT_TYPE else a.reshape(S, rows, cols)
        return arrs

    sm_all = fetch(0, start_token)[-1].reshape(S, -1)
    small_full = {}
    per_shard = [_unpack(sm_all[s], [a.shape for a in small_local]) for s in range(S)]
    for idx, (k, axis) in enumerate(SMALL_SHARDED):
        small_full[k] = jnp.concatenate([per_shard[s][idx] for s in range(S)], axis=axis)
    fb_pad = jnp.pad(weights["kv_f_b"], (0, LANES - heads)).reshape(1, LANES)
    wr_g = [_block_diag(weights["a_w_r"][l], per_group) for l in range(la)]
    wi_g = [_block_diag(weights["a_w_i"][l], per_group) for l in range(la)]
    row = lambda v: v.reshape(1, -1)

    h, hb = embed_fwd(small_full["meta"], x[0], "embed")
    saved = []
    kvz = ct_pad = None
    _, tp, _ = _attn_geometry(nm + seq)
    t = nm + seq
    for l in range(depth):
        sv = {"hb_in": hb}
        if l > 0:
            fetch(l + 1, hb)
        if l < la:
            gr = mm_in(hb, gw[("a_w_in", l)], F32, f"a{l}_in")
            rc, rcb = a_conv_fwd(gr, small_full["a_conv_w"][l], row(small_full["a_conv_b"][l]), f"a{l}_conv")
            r_pre, i_pre = mm_bd(rcb, wr_g[l], wi_g[l], f"a{l}_gates")
            hs, gb = a_elem_fwd(gr, rc, r_pre, i_pre, row(small_full["a_b_r"][l]), row(small_full["a_b_i"][l]),
                                row(small_full["a_lambda"][l]), f"a{l}_lru")
            if l == 0:
                fetch(1, gb)
            mixed, w_mix = gb, gw[("a_w_out", l)]
            sv.update(gr=gr, rc=rc, rcb=rcb, r_pre=r_pre, i_pre=i_pre, hs=hs, gb=gb)
        else:
            j = l - la
            if j == 0:
                kv_cat = jnp.moveaxis(gw[("kv_w", 0)], 0, 1).reshape(d, S * nkv_s)
                kv_pad = jnp.pad(kv_cat, ((0, 0), (0, nkvp - nkv))).reshape(1, d, nkvp)
                kvz = mm_in(hb, kv_pad, F32, "kv_proj")
                cum = kv_fwd(kvz, fb_pad, d, "kv_forget")
                ct_pad = jnp.pad(cum[:, :heads].T, ((0, 0), (0, tp - t))).reshape(heads, 1, tp)
                kv_hb = hb
            qg = mm_in(hb, gw[("b_w_in", j)], F32, f"b{j}_in")
            o, mob = attn_fwd(qg, kvz, ct_pad, d, heads, f"b{j}_attn")
            mixed, w_mix = mob, gw[("b_w_out", j)]
            sv.update(qg=qg, o=o, mob=mob)
        h1, h1b, xh1, rs1 = mm_out_ln(mixed, w_mix, h, row(weights["ln1_g"][l]), row(weights["ln1_b"][l]), alpha,
                                      f"mix{l}_out_ln1")
        zf = mm_in(h1b, gw[("f_w_in", l)], F32, f"f{l}_in")
        ffb = f_elem_fwd(zf, small_full["f_conv_w"][l], row(weights["f_conv_b"][l]), f"f{l}_act")
        h2, h2b, xh2, rs2 = mm_out_ln(ffb, gw[("f_w_out", l)], h1, row(weights["ln2_g"][l]), row(weights["ln2_b"][l]),
                                      alpha, f"f{l}_out_ln2")
        sv.update(h1b=h1b, xh1=xh1, rs1=rs1, zf=zf, ffb=ffb, xh2=xh2, rs2=rs2)
        saved.append(sv)
        h, hb = h2, h2b
    loss11, dy = loss_fwd_bwd(h, loss_target[0], nm, "loss")

    grads = {}

    def by_owner(kl, g3):
        rows, cols = shape2d[kl]
        grads[kl] = g3.reshape(S, 2, rows // 2, cols)

    reducing = []

    def send_grads(g, names, arrays):
        lands = [lax.empty((7,) + a.shape[2:], a.dtype) for a in arrays]
        started, token = _split_start(f"grad_a2a_start_{g}", [list(arrays) + lands],
                                      lambda _, refs, cnt: _all_to_all_plan(refs, cnt))
        reducing.append((names, started[0]))
        return token[0:1, 0:1]

    def after_start(vec, zero):
        return vec if zero is None else vec + zero

    pin = None

    g_small = {}
    per_layer = {k: [None] * n for k, n in (
        ("a_conv_w", la), ("a_conv_b", la), ("a_w_r", la), ("a_b_r", la), ("a_w_i", la), ("a_b_i", la),
        ("a_lambda", la), ("f_conv_w", depth), ("f_conv_b", depth), ("ln1_g", depth), ("ln1_b", depth),
        ("ln2_g", depth), ("ln2_b", depth))}
    adds = [(dy, 1.0)]
    dks, dvs, dcs = [], [], []
    for l in reversed(range(depth)):
        sv = saved[l]
        ds2, ds2b, dg2, db2 = ln_bwd(adds, sv["xh2"], sv["rs2"], after_start(row(weights["ln2_g"][l]), pin),
                                     f"ln2_{l}_bwd")
        pin = None
        per_layer["ln2_g"][l], per_layer["ln2_b"][l] = dg2[0], db2[0]
        dff_v = mm_out_nt(ds2b, gw[("f_w_out", l)], f"f{l}_out_dx")
        by_owner(("f_w_out", l), mm_tn(sv["ffb"], [ds2b], 1, f"f{l}_out_dw"))
        dzg, dzv, dwg, dwv, dbg, dbv = f_elem_bwd(sv["zf"], dff_v, small_full["f_conv_w"][l],
                                                  row(weights["f_conv_b"][l]), f"f{l}_act_bwd")
        per_layer["f_conv_w"][l] = jnp.concatenate([dwg, dwv], axis=1)
        per_layer["f_conv_b"][l] = jnp.concatenate([dbg, dbv], axis=1)[0]
        dh1_f = mm_in_nt([dzg, dzv], gw[("f_w_in", l)], f"f{l}_in_dx")
        by_owner(("f_w_in", l), mm_tn(sv["h1b"], [dzg, dzv], S, f"f{l}_in_dw"))
        ds1, ds1b, dg1, db1 = ln_bwd([(ds2, alpha), (dh1_f, 1.0)], sv["xh1"], sv["rs1"],
                                     row(weights["ln1_g"][l]), f"ln1_{l}_bwd")
        per_layer["ln1_g"][l], per_layer["ln1_b"][l] = dg1[0], db1[0]
        if l < la:
            dgv = mm_out_nt(ds1b, gw[("a_w_out", l)], f"a{l}_out_dx")
            by_owner(("a_w_out", l), mm_tn(sv["gb"], [ds1b], 1, f"a{l}_out_dw"))
            if l == 0:
                pin = send_grads(1, groups[1], [grads[kl] for kl in groups[1]])
            dgate_b, drp_b, dip_b, drc_d, dlam, dbr, dbi = a_elem_bwd(
                dgv, sv["gr"], sv["rc"], sv["r_pre"], sv["i_pre"], sv["hs"], row(small_full["a_b_r"][l]),
                row(small_full["a_b_i"][l]), after_start(row(small_full["a_lambda"][l]), pin), f"a{l}_lru_bwd")
            pin = None
            drc_g = mm_bd_nt(drp_b, dip_b, wr_g[l], wi_g[l], f"a{l}_gates_dx")
            dwr_g, dwi_g = mm_bd_tn(sv["rcb"], drp_b, dip_b, gs, f"a{l}_gates_dw")
            dgr_b, dcw, dcb = a_conv_bwd(drc_d, drc_g, sv["gr"], small_full["a_conv_w"][l], dgate_b,
                                         f"a{l}_conv_bwd")
            per_layer["a_w_r"][l] = _block_diag_extract(dwr_g, per_group, bs)
            per_layer["a_w_i"][l] = _block_diag_extract(dwi_g, per_group, bs)
            per_layer["a_lambda"][l], per_layer["a_b_r"][l], per_layer["a_b_i"][l] = dlam[0], dbr[0], dbi[0]
            per_layer["a_conv_w"][l], per_layer["a_conv_b"][l] = dcw, dcb[0]
            by_owner(("a_w_in", l), mm_tn(sv["hb_in"], [dgr_b], S, f"a{l}_in_dw"))
            gate_w = gw[("a_w_in", l)]
            if l == 0:
                zero = send_grads(0, groups[0], [grads[kl] for kl in groups[0]])
                gate_w = gate_w + zero.astype(BF16)[None]
            dh_m = mm_in_nt([dgr_b], gate_w, f"a{l}_in_dx")
        else:
            j = l - la
            dmo = mm_out_nt(ds1b, gw[("b_w_out", j)], f"b{j}_out_dx")
            by_owner(("b_w_out", j), mm_tn(sv["mob"], [ds1b], 1, f"b{j}_out_dw"))
            dq_b, dog_b, dk, dv, dct = attn_bwd(dmo, sv["qg"], kvz, sv["o"], ct_pad, d, heads, f"b{j}_attn_bwd")
            dks.append(dk)
            dvs.append(dv)
            dcs.append(jnp.pad(dct[:, 0, :t].T, ((0, 0), (0, LANES - heads))))
            dh_m = mm_in_nt([dq_b, dog_b], gw[("b_w_in", j)], f"b{j}_in_dx")
            by_owner(("b_w_in", j), mm_tn(sv["hb_in"], [dq_b, dog_b], S, f"b{j}_in_dw"))
        adds = [(ds1, alpha), (dh_m, 1.0)]
        if l == la:
            dzf_b, dfb = kv_bwd(dcs, kvz, fb_pad, d, "kv_forget_bwd")
            dk_b = add_cast(dks[0], dks[1], "kv_dk") if lb == 2 else None
            dv_b = add_cast(dvs[0], dvs[1], "kv_dv") if lb == 2 else None
            dz_kv = jnp.concatenate([dk_b, dv_b, dzf_b, jnp.zeros((t, nkvp - 2 * d - LANES), BF16)], axis=1)
            dh_kv = mm_in_nt([dz_kv], kv_pad, "kv_proj_dx")
            kv_dw = mm_tn(kv_hb, [dz_kv], 1, "kv_proj_dw")
            by_owner(("kv_w", 0), jnp.moveaxis(kv_dw[0, :, :nkv].reshape(d, S, nkv_s), 1, 0))
            g_small["kv_f_b"] = dfb[0, :heads]
            adds.append((dh_kv, 1.0))
        if l > 0:
            pin = send_grads(l + 1, groups[l + 1], [grads[kl] for kl in groups[l + 1]])
    g_meta, g_x = embed_bwd(adds, nm, "embed_bwd")

    small_names = list(SMALL_REPLICATED) + [k for k, _ in SMALL_SHARDED]
    g_small["meta"] = g_meta
    for k, vals in per_layer.items():
        g_small[k] = jnp.stack(vals)
    small_shapes = {k: (weights[k].shape if k in SMALL_REPLICATED else g_small[k].shape) for k in small_names}
    sm_g = _pack([g_small[k].reshape(small_shapes[k]) for k in small_names], S * 2 * SUBLANES * LANES)
    sm_g = sm_g.reshape(S, 2, -1, LANES)
    send_grads(len(groups), [("small", 0)], [sm_g])

    fin = {"small": lax.empty(sm_g.shape, F32)}
    for kl in keys:
        n_stack = weights[kl[0]].shape[0] if weights[kl[0]].ndim == 3 else 1
        rows, cols = shape2d[kl]
        fin.setdefault(kl[0], lax.empty((n_stack, 2, rows // 2, cols), F32))
    def finish(g, after):
        names_g, started = reducing[g]
        arrs = _split_wait(f"grad_a2a_wait_{g}", started, after, _all_to_all_plan)
        half = len(names_g)
        for i, (kl, cs, rv) in enumerate(zip(names_g, arrs[:half], arrs[half:])):
            fin[kl[0]] = owner_sum(shard_core_arr, cs, rv, fin[kl[0]], None if kl[0] == "small" else kl[1],
                                   f"owner_sum_{g}_{i}")

    for g in range(len(reducing) - 1):
        finish(g, g_x)
    joined = dict(zip(BIG, join_halves([fin[k] for k in BIG], "grad_join")))

    out_g, out_d, out_m, out_v = {}, {}, {}, {}
    for k in BIG:
        w2 = weights[k].reshape(-1, weights[k].shape[-1])
        g2 = joined[k].reshape(w2.shape)
        dlt, mn, vn, g_out = adamw(w2, g2, mom_m[k].reshape(w2.shape), mom_v[k].reshape(w2.shape), "adamw_" + k,
                                   emit_g=True)
        shp = weights[k].shape
        out_g[k], out_d[k], out_m[k], out_v[k] = g_out.reshape(shp), dlt.reshape(shp), mn.reshape(shp), vn.reshape(shp)
    finish(len(reducing) - 1, out_d[BIG[-1]])
    sm_red = spread_eighth(fin["small"], "small_spread").reshape(-1)
    sm_vals = dict(zip(small_names, _unpack(sm_red, [small_shapes[k] for k in small_names])))
    local_small = {}
    for k in SMALL_REPLICATED:
        local_small[k] = sm_vals[k]
    for k, axis in SMALL_SHARDED:
        size = weights[k].shape[axis]
        local_small[k] = lax.dynamic_slice_in_dim(sm_vals[k], shard * size, size, axis=axis)
    for k in small_names:
        shp = weights[k].shape
        two_d = (-1, shp[-1]) if len(shp) > 1 else (1, -1)
        dlt, mn, vn = adamw(weights[k].reshape(two_d), local_small[k].reshape(two_d), mom_m[k].reshape(two_d),
                            mom_v[k].reshape(two_d), "adamw_" + k)
        out_g[k], out_d[k], out_m[k], out_v[k] = local_small[k], dlt.reshape(shp), mn.reshape(shp), vn.reshape(shp)

    loss = lax.psum(loss11[0, 0], ("x", "y", "c"))
    return (loss, g_x[None], *[out_g[k] for k in WEIGHT_ORDER], *[out_d[k] for k in WEIGHT_ORDER],
            *[out_m[k] for k in WEIGHT_ORDER], *[out_v[k] for k in WEIGHT_ORDER])
```

```python
import functools
import math

import jax
import jax.numpy as jnp
from jax import lax
from jax.experimental import pallas as pl
from jax.experimental.pallas import tpu as pltpu

F32 = jnp.float32
BF16 = jnp.bfloat16

LRU_C = 8.0
LN_EPS = 1e-5
ADAM_LR = 0.001
ADAM_B1 = 0.9
ADAM_B2 = 0.999
ADAM_EPS = 1e-08
ADAM_WD = 0.01
ADAM_STEP = 10

LANES = 128
SUBLANES = 8
V7X_VMEM_BYTES = 64 * 1024 * 1024
VMEM_LIMIT = V7X_VMEM_BYTES * 7 // 8
N_SHARDS = 4
GELU_C0 = math.sqrt(2.0 / math.pi)
GELU_C1 = 0.044715
NEG_BIG = -1e30


def _cp(*sem):
    return pltpu.CompilerParams(dimension_semantics=tuple(sem), vmem_limit_bytes=VMEM_LIMIT)


def _tile(n, cap, mult=LANES):
    best = None
    d = mult
    while d <= min(n, cap):
        if n % d == 0:
            best = d
        d += mult
    return n if best is None else best


def _row_block(t):
    if t % 3 == 0 and (t // 3) % 16 == 0:
        return t // 3
    return t


def _round_up(n, m):
    return (n + m - 1) // m * m


def _sigmoid(v):
    return 1.0 / (1.0 + jnp.exp(-v))


def _softplus(v):
    return jnp.maximum(v, 0.0) + jnp.log(1.0 + jnp.exp(-jnp.abs(v)))


def _gelu_parts(v):
    v2 = v * v
    u = GELU_C0 * (v + GELU_C1 * v * v2)
    t = jnp.tanh(u)
    g = 0.5 * v * (1.0 + t)
    dg = 0.5 * (1.0 + t) + 0.5 * v * (1.0 - t * t) * (GELU_C0 * (1.0 + 3.0 * GELU_C1 * v2))
    return g, dg


def _gelu(v):
    u = GELU_C0 * (v + GELU_C1 * v * v * v)
    return 0.5 * v * (1.0 + jnp.tanh(u))


def _neg_expm1(v):
    series = -v * (1.0 + 0.5 * v * (1.0 + (v / 3.0) * (1.0 + 0.25 * v)))
    return jnp.where(v > -0.05, series, 1.0 - jnp.exp(v))


def _shift_down(v, j):
    if j == 0:
        return v
    rows = lax.broadcasted_iota(jnp.int32, v.shape, 0)
    return jnp.where(rows >= j, pltpu.roll(v, j, 0), 0.0)


def _shift_up(v, j):
    if j == 0:
        return v
    n = v.shape[0]
    rows = lax.broadcasted_iota(jnp.int32, v.shape, 0)
    return jnp.where(rows < n - j, pltpu.roll(v, n - j, 0), 0.0)


def _scan_rows(a_ref, b_ref, out_ref, n_rows, width, reverse):
    n_groups = n_rows // SUBLANES
    rows = lax.broadcasted_iota(jnp.int32, (SUBLANES, width), 0)
    edge = 0 if reverse else SUBLANES - 1

    def body(g, carry):
        grp = (n_groups - 1 - g) if reverse else g
        off = pl.multiple_of(grp * SUBLANES, SUBLANES)
        b = b_ref[pl.ds(off, SUBLANES), :]
        a = None if a_ref is None else a_ref[pl.ds(off, SUBLANES), :]
        for d in (1, 2, 4):
            if reverse:
                keep = rows < SUBLANES - d
                sh = SUBLANES - d
            else:
                keep = rows >= d
                sh = d
            b_s = jnp.where(keep, pltpu.roll(b, sh, 0), 0.0)
            if a is None:
                b = b + b_s
            else:
                a_s = jnp.where(keep, pltpu.roll(a, sh, 0), 1.0)
                b = a * b_s + b
                a = a * a_s
        h = b + carry if a is None else b + a * carry
        out_ref[pl.ds(off, SUBLANES), :] = h
        return jnp.sum(jnp.where(rows == edge, h, 0.0), axis=0, keepdims=True)

    lax.fori_loop(0, n_groups, body, jnp.zeros((1, width), F32), unroll=2)


def mm_in(x, w, out_dtype, name):
    t, k = x.shape
    s_n, _, ns = w.shape
    tn = _tile(ns, 1408)
    nj = ns // tn
    rb = _row_block(t)

    def body(x_ref, w_ref, o_ref):
        o_ref[...] = jnp.dot(x_ref[...], w_ref[...], preferred_element_type=F32).astype(o_ref.dtype)

    return pl.pallas_call(
        body, name=name, grid=(s_n, nj, t // rb),
        in_specs=[pl.BlockSpec((rb, k), lambda s, j, r: (r, 0)),
                  pl.BlockSpec((None, k, tn), lambda s, j, r: (s, 0, j))],
        out_specs=pl.BlockSpec((rb, tn), lambda s, j, r: (r, s * nj + j)),
        out_shape=jax.ShapeDtypeStruct((t, s_n * ns), out_dtype),
        compiler_params=_cp("parallel", "parallel", "parallel"))(x, w)


def _part_map(p, per_part, nj, lead):
    def index(*grid):
        s, j = grid[-2], grid[-1]
        mine = s // per_part == p
        col = jnp.where(mine, (s - p * per_part) * nj + j, 0)
        return (grid[0], col) if lead else (0, col)
    return index


def mm_in_nt(dy_parts, w, name):
    n_parts = len(dy_parts)
    t = dy_parts[0].shape[0]
    s_n, k, ns = w.shape
    per_part = s_n // n_parts
    tn = _tile(ns, 1408)
    nj = ns // tn
    rb = _row_block(t)

    def body(*refs):
        w_ref, o_ref = refs[n_parts:]

        @pl.when((pl.program_id(1) == 0) & (pl.program_id(2) == 0))
        def _():
            o_ref[...] = jnp.zeros_like(o_ref)

        for p in range(n_parts):
            @pl.when(pl.program_id(1) // per_part == p)
            def _():
                o_ref[...] += lax.dot_general(refs[p][...], w_ref[...], (((1,), (1,)), ((), ())),
                                              preferred_element_type=F32)

    return pl.pallas_call(
        body, name=name, grid=(t // rb, s_n, nj),
        in_specs=[pl.BlockSpec((rb, tn), _part_map(p, per_part, nj, True)) for p in range(n_parts)]
        + [pl.BlockSpec((None, k, tn), lambda r, s, j: (s, 0, j))],
        out_specs=pl.BlockSpec((rb, k), lambda r, s, j: (r, 0)),
        out_shape=jax.ShapeDtypeStruct((t, k), F32),
        compiler_params=_cp("parallel", "arbitrary", "arbitrary"))(*dy_parts, w)


def mm_out_nt(dy, w, name):
    t, n = dy.shape
    k = w.shape[0]
    rb = _row_block(t)

    def body(dy_ref, w_ref, o_ref):
        o_ref[...] = lax.dot_general(dy_ref[...], w_ref[...], (((1,), (1,)), ((), ())),
                                     preferred_element_type=F32)

    return pl.pallas_call(
        body, name=name, grid=(t // rb,),
        in_specs=[pl.BlockSpec((rb, n), lambda r: (r, 0)), pl.BlockSpec((k, n), lambda r: (0, 0))],
        out_specs=pl.BlockSpec((rb, k), lambda r: (r, 0)),
        out_shape=jax.ShapeDtypeStruct((t, k), F32),
        compiler_params=_cp("parallel"))(dy, w)


def mm_tn(x, dy_parts, s_n, name):
    n_parts = len(dy_parts)
    t, kb = x.shape
    nb = dy_parts[0].shape[1] * n_parts // s_n
    per_part = max(s_n // n_parts, 1)
    tk = _tile(kb, 1408)
    tn = _tile(nb, 1408)
    nkb, nnb = kb // tk, nb // tn
    tn_dims = (((0,), (0,)), ((), ()))

    def body(*refs):
        x_ref, o_ref = refs[0], refs[-1]
        for p in range(n_parts):
            @pl.when(pl.program_id(1) // per_part == p)
            def _():
                o_ref[...] = lax.dot_general(x_ref[...], refs[1 + p][...], tn_dims,
                                             preferred_element_type=F32).astype(o_ref.dtype)

    return pl.pallas_call(
        body, name=name, grid=(nkb, s_n, nnb),
        in_specs=[pl.BlockSpec((t, tk), lambda a, s, b: (0, a))]
        + [pl.BlockSpec((t, tn), _part_map(p, per_part, nnb, False)) for p in range(n_parts)],
        out_specs=pl.BlockSpec((None, tk, tn), lambda a, s, b: (s, a, b)),
        out_shape=jax.ShapeDtypeStruct((s_n, kb, nb), BF16),
        compiler_params=_cp("parallel", "parallel", "parallel"))(x, *dy_parts)


def mm_bd(x, wr, wi, name):
    t, _ = x.shape
    g_n, gs, _ = wr.shape
    rb = _row_block(t)

    def body(x_ref, wr_ref, wi_ref, r_ref, i_ref):
        xv = x_ref[...]
        r_ref[...] = jnp.dot(xv, wr_ref[...], preferred_element_type=F32)
        i_ref[...] = jnp.dot(xv, wi_ref[...], preferred_element_type=F32)

    blk = pl.BlockSpec((rb, gs), lambda g, r: (r, g))
    wspec = pl.BlockSpec((None, gs, gs), lambda g, r: (g, 0, 0))
    return pl.pallas_call(
        body, name=name, grid=(g_n, t // rb), in_specs=[blk, wspec, wspec], out_specs=[blk, blk],
        out_shape=[jax.ShapeDtypeStruct((t, g_n * gs), F32)] * 2,
        compiler_params=_cp("parallel", "parallel"))(x, wr, wi)


def mm_bd_nt(dr, di, wr, wi, name):
    t, _ = dr.shape
    g_n, gs, _ = wr.shape
    rb = _row_block(t)
    nt = (((1,), (1,)), ((), ()))

    def body(dr_ref, di_ref, wr_ref, wi_ref, o_ref):
        o_ref[...] = (lax.dot_general(dr_ref[...], wr_ref[...], nt, preferred_element_type=F32)
                      + lax.dot_general(di_ref[...], wi_ref[...], nt, preferred_element_type=F32))

    blk = pl.BlockSpec((rb, gs), lambda g, r: (r, g))
    wspec = pl.BlockSpec((None, gs, gs), lambda g, r: (g, 0, 0))
    return pl.pallas_call(
        body, name=name, grid=(g_n, t // rb), in_specs=[blk, blk, wspec, wspec], out_specs=blk,
        out_shape=jax.ShapeDtypeStruct((t, g_n * gs), F32),
        compiler_params=_cp("parallel", "parallel"))(dr, di, wr, wi)


def mm_bd_tn(x, dr, di, gs, name):
    t, w = x.shape
    g_n = w // gs
    tn_dims = (((0,), (0,)), ((), ()))

    def body(x_ref, dr_ref, di_ref, gr_ref, gi_ref):
        xv = x_ref[...]
        gr_ref[...] = lax.dot_general(xv, dr_ref[...], tn_dims, preferred_element_type=F32)
        gi_ref[...] = lax.dot_general(xv, di_ref[...], tn_dims, preferred_element_type=F32)

    blk = pl.BlockSpec((t, gs), lambda g: (0, g))
    ospec = pl.BlockSpec((None, gs, gs), lambda g: (g, 0, 0))
    return pl.pallas_call(
        body, name=name, grid=(g_n,), in_specs=[blk, blk, blk], out_specs=[ospec, ospec],
        out_shape=[jax.ShapeDtypeStruct((g_n, gs, gs), F32)] * 2,
        compiler_params=_cp("parallel"))(x, dr, di)


def embed_fwd(meta, x2d, name):
    nm, d = meta.shape
    seq = x2d.shape[0]
    t = nm + seq
    cb = _tile(d, 256)

    def body(m_ref, x_ref, h_ref, hb_ref):
        h_ref[pl.ds(0, nm), :] = m_ref[...]
        h_ref[pl.ds(nm, seq), :] = x_ref[...]
        hb_ref[pl.ds(0, nm), :] = m_ref[...].astype(BF16)
        hb_ref[pl.ds(nm, seq), :] = x_ref[...].astype(BF16)

    return pl.pallas_call(
        body, name=name, grid=(d // cb,),
        in_specs=[pl.BlockSpec((nm, cb), lambda j: (0, j)), pl.BlockSpec((seq, cb), lambda j: (0, j))],
        out_specs=[pl.BlockSpec((t, cb), lambda j: (0, j))] * 2,
        out_shape=[jax.ShapeDtypeStruct((t, d), F32), jax.ShapeDtypeStruct((t, d), BF16)],
        compiler_params=_cp("parallel"))(meta, x2d)


def embed_bwd(adds, nm, name):
    t, d = adds[0][0].shape
    seq = t - nm
    cb = _tile(d, 256)
    scales = [s for _, s in adds]
    n = len(adds)

    def body(*refs):
        tot = None
        for r, sc in zip(refs[:n], scales):
            term = r[...] if sc == 1.0 else sc * r[...]
            tot = term if tot is None else tot + term
        gm_ref, gx_ref = refs[n], refs[n + 1]
        gm_ref[...] = tot[0:nm]
        gx_ref[...] = tot[nm:t]

    return pl.pallas_call(
        body, name=name, grid=(d // cb,),
        in_specs=[pl.BlockSpec((t, cb), lambda j: (0, j))] * n,
        out_specs=[pl.BlockSpec((nm, cb), lambda j: (0, j)), pl.BlockSpec((seq, cb), lambda j: (0, j))],
        out_shape=[jax.ShapeDtypeStruct((nm, d), F32), jax.ShapeDtypeStruct((seq, d), F32)],
        compiler_params=_cp("parallel"))(*[a for a, _ in adds])


def loss_fwd_bwd(h, tgt, nm, name):
    t, d = h.shape
    seq = t - nm
    cb = _tile(d, 256)
    inv_d = 1.0 / d

    def body(h_ref, t_ref, loss_ref, dy_ref):
        @pl.when(pl.program_id(0) == 0)
        def _():
            loss_ref[...] = jnp.zeros_like(loss_ref)
        err = h_ref[pl.ds(nm, seq), :] - t_ref[...]
        dy_ref[pl.ds(0, nm), :] = jnp.zeros((nm, cb), F32)
        dy_ref[pl.ds(nm, seq), :] = err * inv_d
        loss_ref[...] += (0.5 * inv_d) * jnp.sum(err * err, keepdims=True)

    return pl.pallas_call(
        body, name=name, grid=(d // cb,),
        in_specs=[pl.BlockSpec((t, cb), lambda j: (0, j)), pl.BlockSpec((seq, cb), lambda j: (0, j))],
        out_specs=[pl.BlockSpec((1, 1), lambda j: (0, 0)), pl.BlockSpec((t, cb), lambda j: (0, j))],
        out_shape=[jax.ShapeDtypeStruct((1, 1), F32), jax.ShapeDtypeStruct((t, d), F32)],
        compiler_params=_cp("arbitrary"))(h, tgt)


def mm_out_ln(x, w, h, g, b, alpha, name):
    t, d = h.shape
    k = x.shape[1]
    rb = _row_block(t)

    def body(x_ref, w_ref, h_ref, g_ref, b_ref, y_ref, yb_ref, xh_ref, rs_ref):
        s = alpha * h_ref[...] + jnp.dot(x_ref[...], w_ref[...], preferred_element_type=F32)
        mu = jnp.mean(s, axis=-1, keepdims=True)
        c = s - mu
        var = jnp.mean(c * c, axis=-1, keepdims=True)
        rstd = lax.rsqrt(var + LN_EPS)
        xh = c * rstd
        y = xh * g_ref[...] + b_ref[...]
        y_ref[...] = y
        yb_ref[...] = y.astype(BF16)
        xh_ref[...] = xh
        rs_ref[...] = rstd

    row = pl.BlockSpec((rb, d), lambda r: (r, 0))
    vec = pl.BlockSpec((1, d), lambda r: (0, 0))
    return pl.pallas_call(
        body, name=name, grid=(t // rb,),
        in_specs=[pl.BlockSpec((rb, k), lambda r: (r, 0)), pl.BlockSpec((k, d), lambda r: (0, 0)), row, vec, vec],
        out_specs=[row, row, row, pl.BlockSpec((rb, 1), lambda r: (r, 0))],
        out_shape=[jax.ShapeDtypeStruct((t, d), F32), jax.ShapeDtypeStruct((t, d), BF16),
                   jax.ShapeDtypeStruct((t, d), F32), jax.ShapeDtypeStruct((t, 1), F32)],
        compiler_params=_cp("parallel"))(x, w, h, g, b)


def ln_bwd(adds, xhat, rstd, g, name):
    t, d = xhat.shape
    rb = _row_block(t)
    scales = [s for _, s in adds]
    n = len(adds)

    def body(*refs):
        xh_ref, rs_ref, g_ref = refs[n:n + 3]
        ds_ref, dsb_ref, dg_ref, db_ref = refs[n + 3:]
        dy = None
        for r, sc in zip(refs[:n], scales):
            term = r[...] if sc == 1.0 else sc * r[...]
            dy = term if dy is None else dy + term

        @pl.when(pl.program_id(0) == 0)
        def _():
            dg_ref[...] = jnp.zeros_like(dg_ref)
            db_ref[...] = jnp.zeros_like(db_ref)

        xh = xh_ref[...]
        dxh = dy * g_ref[...]
        m1 = jnp.mean(dxh, axis=-1, keepdims=True)
        m2 = jnp.mean(dxh * xh, axis=-1, keepdims=True)
        ds = rs_ref[...] * (dxh - m1 - xh * m2)
        ds_ref[...] = ds
        dsb_ref[...] = ds.astype(BF16)
        dg_ref[...] += jnp.sum(dy * xh, axis=0, keepdims=True)
        db_ref[...] += jnp.sum(dy, axis=0, keepdims=True)

    row = pl.BlockSpec((rb, d), lambda r: (r, 0))
    vec = pl.BlockSpec((1, d), lambda r: (0, 0))
    return pl.pallas_call(
        body, name=name, grid=(t // rb,),
        in_specs=[row] * n + [row, pl.BlockSpec((rb, 1), lambda r: (r, 0)), vec],
        out_specs=[row, row, vec, vec],
        out_shape=[jax.ShapeDtypeStruct((t, d), F32), jax.ShapeDtypeStruct((t, d), BF16),
                   jax.ShapeDtypeStruct((1, d), F32), jax.ShapeDtypeStruct((1, d), F32)],
        compiler_params=_cp("arbitrary"))(*[a for a, _ in adds], xhat, rstd, g)


def _conv_fwd_val(xv, w_ref, b_ref, width):
    y = b_ref[...]
    for j in range(width):
        y = y + _shift_down(xv, j) * w_ref[pl.ds(width - 1 - j, 1), :]
    return y


def _conv_bwd_val(dout, xv, w_ref, width):
    dx = None
    dws = [None] * width
    for j in range(width):
        k = width - 1 - j
        term = _shift_up(dout, j) * w_ref[pl.ds(k, 1), :]
        dx = term if dx is None else dx + term
        dws[k] = jnp.sum(dout * _shift_down(xv, j), axis=0, keepdims=True)
    return dx, dws, jnp.sum(dout, axis=0, keepdims=True)


def a_conv_fwd(gr, cw, cbias, name):
    t, two_dr = gr.shape
    dr = two_dr // 2
    width = cw.shape[0]
    cb = _tile(dr, 256)
    off = dr // cb

    def body(x_ref, w_ref, b_ref, rc_ref, rcb_ref):
        y = _conv_fwd_val(x_ref[...], w_ref, b_ref, width)
        rc_ref[...] = y
        rcb_ref[...] = y.astype(BF16)

    return pl.pallas_call(
        body, name=name, grid=(dr // cb,),
        in_specs=[pl.BlockSpec((t, cb), lambda j: (0, off + j)),
                  pl.BlockSpec((width, cb), lambda j: (0, j)), pl.BlockSpec((1, cb), lambda j: (0, j))],
        out_specs=[pl.BlockSpec((t, cb), lambda j: (0, j))] * 2,
        out_shape=[jax.ShapeDtypeStruct((t, dr), F32), jax.ShapeDtypeStruct((t, dr), BF16)],
        compiler_params=_cp("parallel"))(gr, cw, cbias)


def a_conv_bwd(drc_a, drc_b, gr, cw, dgr, name):
    t, two_dr = gr.shape
    dr = two_dr // 2
    width = cw.shape[0]
    cb = _tile(dr, 256)
    off = dr // cb

    def body(da_ref, db_ref, x_ref, w_ref, dgr_in, dx_ref, dw_ref, dbias_ref):
        del dgr_in
        dout = da_ref[...] + db_ref[...]
        dx, dws, dbias = _conv_bwd_val(dout, x_ref[...], w_ref, width)
        dx_ref[...] = dx.astype(BF16)
        for k in range(width):
            dw_ref[pl.ds(k, 1), :] = dws[k]
        dbias_ref[...] = dbias

    col = pl.BlockSpec((t, cb), lambda j: (0, j))
    return pl.pallas_call(
        body, name=name, grid=(dr // cb,),
        in_specs=[col, col, pl.BlockSpec((t, cb), lambda j: (0, off + j)),
                  pl.BlockSpec((width, cb), lambda j: (0, j)), pl.BlockSpec(memory_space=pl.ANY)],
        out_specs=[pl.BlockSpec((t, cb), lambda j: (0, off + j)), pl.BlockSpec((width, cb), lambda j: (0, j)),
                   pl.BlockSpec((1, cb), lambda j: (0, j))],
        out_shape=[jax.ShapeDtypeStruct((t, two_dr), BF16), jax.ShapeDtypeStruct((width, dr), F32),
                   jax.ShapeDtypeStruct((1, dr), F32)],
        input_output_aliases={4: 0},
        compiler_params=_cp("parallel"))(drc_a, drc_b, gr, cw, dgr)


def _lru_gates(r_pre, i_pre, br, bi, lam):
    r = _sigmoid(r_pre + br)
    i = _sigmoid(i_pre + bi)
    sp = _softplus(-lam)
    la = -LRU_C * r * sp
    a = jnp.exp(la)
    m = jnp.sqrt(_neg_expm1(2.0 * la))
    return r, i, sp, la, a, m


def a_elem_fwd(gr, rc, r_pre, i_pre, br, bi, lam, name):
    t, dr = rc.shape
    cb = _tile(dr, 2 * LANES)
    rb = _row_block(t)
    chunks = [pl.ds(r * rb, rb) for r in range(t // rb)]

    def body(gate_ref, rc_ref, rp_ref, ip_ref, br_ref, bi_ref, lam_ref, hs_ref, g_ref, a_s, u_s):
        for rows in chunks:
            _, i, _, _, a, m = _lru_gates(rp_ref[rows, :], ip_ref[rows, :], br_ref[...], bi_ref[...], lam_ref[...])
            a_s[rows, :] = a
            u_s[rows, :] = m * (i * rc_ref[rows, :])
        _scan_rows(a_s, u_s, hs_ref, t, cb, reverse=False)
        for rows in chunks:
            g_ref[rows, :] = (_gelu(gate_ref[rows, :]) * hs_ref[rows, :]).astype(BF16)

    col = pl.BlockSpec((t, cb), lambda j: (0, j))
    vec = pl.BlockSpec((1, cb), lambda j: (0, j))
    return pl.pallas_call(
        body, name=name, grid=(dr // cb,),
        in_specs=[col, col, col, col, vec, vec, vec],
        out_specs=[col, col],
        out_shape=[jax.ShapeDtypeStruct((t, dr), F32), jax.ShapeDtypeStruct((t, dr), BF16)],
        scratch_shapes=[pltpu.VMEM((t, cb), F32), pltpu.VMEM((t, cb), F32)],
        compiler_params=_cp("parallel"))(gr, rc, r_pre, i_pre, br, bi, lam)


def a_elem_bwd(dg, gr, rc, r_pre, i_pre, hs, br, bi, lam, name):
    t, dr = rc.shape
    cb = _tile(dr, 2 * LANES)
    rb = _row_block(t)
    chunks = [pl.ds(r * rb, rb) for r in range(t // rb)]

    def body(dg_ref, gate_ref, rc_ref, rp_ref, ip_ref, hs_ref, br_ref, bi_ref, lam_ref,
             dgate_ref, dr_ref, di_ref, drc_ref, dlam_ref, dbr_ref, dbi_ref, a_s, b_s, g_s, hp_s):
        lamv = lam_ref[...]
        gates = lambda rows: _lru_gates(rp_ref[rows, :], ip_ref[rows, :], br_ref[...], bi_ref[...], lamv)
        for rows in chunks:
            a_s[rows, :] = gates(rows)[4]
            ge, dge = _gelu_parts(gate_ref[rows, :])
            dgv = dg_ref[rows, :]
            dgate_ref[rows, :] = (dgv * hs_ref[rows, :] * dge).astype(BF16)
            b_s[rows, :] = dgv * ge
        a_s[...] = _shift_up(a_s[...], 1)
        hp_s[...] = _shift_down(hs_ref[...], 1)
        _scan_rows(a_s, b_s, g_s, t, cb, reverse=True)
        dsp = dbr = dbi = jnp.zeros((1, cb), F32)
        for rows in chunks:
            r, i, sp, _, a, m = gates(rows)
            rcv = rc_ref[rows, :]
            gsum = g_s[rows, :]
            da = gsum * hp_s[rows, :]
            dm = gsum * (i * rcv)
            d_i = gsum * m * rcv
            drc_ref[rows, :] = gsum * m * i
            dla = a * da - dm * (a * a) / m
            d_r = (-LRU_C) * sp * dla
            dsp = dsp + jnp.sum((-LRU_C) * r * dla, axis=0, keepdims=True)
            d_rp = d_r * r * (1.0 - r)
            d_ip = d_i * i * (1.0 - i)
            dr_ref[rows, :] = d_rp.astype(BF16)
            di_ref[rows, :] = d_ip.astype(BF16)
            dbr = dbr + jnp.sum(d_rp, axis=0, keepdims=True)
            dbi = dbi + jnp.sum(d_ip, axis=0, keepdims=True)
        dlam_ref[...] = -dsp * _sigmoid(-lamv)
        dbr_ref[...] = dbr
        dbi_ref[...] = dbi

    col = pl.BlockSpec((t, cb), lambda j: (0, j))
    vec = pl.BlockSpec((1, cb), lambda j: (0, j))
    big_b = jax.ShapeDtypeStruct((t, dr), BF16)
    vec_s = jax.ShapeDtypeStruct((1, dr), F32)
    return pl.pallas_call(
        body, name=name, grid=(dr // cb,),
        in_specs=[col, col, col, col, col, col, vec, vec, vec],
        out_specs=[col, col, col, col, vec, vec, vec],
        out_shape=[jax.ShapeDtypeStruct((t, 2 * dr), BF16), big_b, big_b, jax.ShapeDtypeStruct((t, dr), F32),
                   vec_s, vec_s, vec_s],
        scratch_shapes=[pltpu.VMEM((t, cb), F32)] * 4,
        compiler_params=_cp("parallel"))(dg, gr, rc, r_pre, i_pre, hs, br, bi, lam)


def f_elem_fwd(z, cw, cbias, name):
    t, two_f = z.shape
    dff = two_f // 2
    width = cw.shape[0]
    cb = _tile(dff, 256)
    off = dff // cb

    def body(zg_ref, zv_ref, wg_ref, wv_ref, bg_ref, bv_ref, o_ref):
        zcg = _conv_fwd_val(zg_ref[...], wg_ref, bg_ref, width)
        zcv = _conv_fwd_val(zv_ref[...], wv_ref, bv_ref, width)
        o_ref[...] = (_gelu(zcg) * zcv).astype(BF16)

    lo = lambda j: (0, j)
    hi = lambda j: (0, off + j)
    return pl.pallas_call(
        body, name=name, grid=(dff // cb,),
        in_specs=[pl.BlockSpec((t, cb), lo), pl.BlockSpec((t, cb), hi),
                  pl.BlockSpec((width, cb), lo), pl.BlockSpec((width, cb), hi),
                  pl.BlockSpec((1, cb), lo), pl.BlockSpec((1, cb), hi)],
        out_specs=pl.BlockSpec((t, cb), lo),
        out_shape=jax.ShapeDtypeStruct((t, dff), BF16),
        compiler_params=_cp("parallel"))(z, z, cw, cw, cbias, cbias)


def f_elem_bwd(z, dff_g, cw, cbias, name):
    t, two_f = z.shape
    dff = two_f // 2
    width = cw.shape[0]
    cb = _tile(dff, 256)
    off = dff // cb

    def body(zg_ref, zv_ref, d_ref, wg_ref, wv_ref, bg_ref, bv_ref,
             dzg_ref, dzv_ref, dwg_ref, dwv_ref, dbg_ref, dbv_ref):
        zg = zg_ref[...]
        zv = zv_ref[...]
        zcg = _conv_fwd_val(zg, wg_ref, bg_ref, width)
        zcv = _conv_fwd_val(zv, wv_ref, bv_ref, width)
        ge, dge = _gelu_parts(zcg)
        dv = d_ref[...]
        dx, dws, dbias = _conv_bwd_val(dv * zcv * dge, zg, wg_ref, width)
        dzg_ref[...] = dx.astype(BF16)
        for k in range(width):
            dwg_ref[pl.ds(k, 1), :] = dws[k]
        dbg_ref[...] = dbias
        dx, dws, dbias = _conv_bwd_val(dv * ge, zv, wv_ref, width)
        dzv_ref[...] = dx.astype(BF16)
        for k in range(width):
            dwv_ref[pl.ds(k, 1), :] = dws[k]
        dbv_ref[...] = dbias

    lo = lambda j: (0, j)
    hi = lambda j: (0, off + j)
    col = pl.BlockSpec((t, cb), lo)
    wsp = pl.BlockSpec((width, cb), lo)
    vsp = pl.BlockSpec((1, cb), lo)
    return pl.pallas_call(
        body, name=name, grid=(dff // cb,),
        in_specs=[col, pl.BlockSpec((t, cb), hi), col, wsp, pl.BlockSpec((width, cb), hi),
                  vsp, pl.BlockSpec((1, cb), hi)],
        out_specs=[col, col, wsp, wsp, vsp, vsp],
        out_shape=[jax.ShapeDtypeStruct((t, dff), BF16)] * 2
        + [jax.ShapeDtypeStruct((width, dff), F32)] * 2 + [jax.ShapeDtypeStruct((1, dff), F32)] * 2,
        compiler_params=_cp("parallel"))(z, z, dff_g, cw, cw, cbias, cbias)


def kv_fwd(z, fb, d_model, name):
    t, _ = z.shape
    blk = 2 * d_model // LANES

    def body(z_ref, fb_ref, c_ref, lf_s):
        v = z_ref[...] + fb_ref[...]
        lf_s[...] = -_softplus(-v)
        _scan_rows(None, lf_s, c_ref, t, LANES, reverse=False)

    return pl.pallas_call(
        body, name=name, grid=(1,),
        in_specs=[pl.BlockSpec((t, LANES), lambda j: (0, blk)), pl.BlockSpec((1, LANES), lambda j: (0, 0))],
        out_specs=pl.BlockSpec((t, LANES), lambda j: (0, 0)),
        out_shape=jax.ShapeDtypeStruct((t, LANES), F32),
        scratch_shapes=[pltpu.VMEM((t, LANES), F32)],
        compiler_params=_cp("arbitrary"))(z, fb)


def kv_bwd(dcs, z, fb, d_model, name):
    t, _ = z.shape
    blk = 2 * d_model // LANES
    n = len(dcs)

    def body(*refs):
        z_ref, fb_ref, dz_ref, dfb_ref, dc_s, dl_s = refs[n:]
        tot = refs[0][...]
        for r in refs[1:n]:
            tot = tot + r[...]
        dc_s[...] = tot
        _scan_rows(None, dc_s, dl_s, t, LANES, reverse=True)
        v = z_ref[...] + fb_ref[...]
        dz = dl_s[...] * _sigmoid(-v)
        dz_ref[...] = dz.astype(BF16)
        dfb_ref[...] = jnp.sum(dz, axis=0, keepdims=True)

    full = pl.BlockSpec((t, LANES), lambda j: (0, 0))
    return pl.pallas_call(
        body, name=name, grid=(1,),
        in_specs=[full] * n + [pl.BlockSpec((t, LANES), lambda j: (0, blk)),
                               pl.BlockSpec((1, LANES), lambda j: (0, 0))],
        out_specs=[full, pl.BlockSpec((1, LANES), lambda j: (0, 0))],
        out_shape=[jax.ShapeDtypeStruct((t, LANES), BF16), jax.ShapeDtypeStruct((1, LANES), F32)],
        scratch_shapes=[pltpu.VMEM((t, LANES), F32)] * 2,
        compiler_params=_cp("arbitrary"))(*dcs, z, fb)


def add_cast(a, b, name):
    t, d = a.shape
    cb = _tile(d, 512)

    def body(a_ref, b_ref, o_ref):
        o_ref[...] = (a_ref[...] + b_ref[...]).astype(BF16)

    col = pl.BlockSpec((t, cb), lambda j: (0, j))
    return pl.pallas_call(body, name=name, grid=(d // cb,), in_specs=[col, col], out_specs=col,
                          out_shape=jax.ShapeDtypeStruct((t, d), BF16),
                          compiler_params=_cp("parallel"))(a, b)


def _attn_geometry(t):
    nqb = 6 if t > 1024 else 2
    tp = _round_up(t, LANES * nqb)
    return nqb, tp, tp // nqb


def _attn_scales(dh):
    scale = dh ** -0.5
    if math.log2(scale).is_integer():
        return scale, 1.0
    return 1.0, scale


def _attn_pieces(qs, ks, crow, j, i, tq, dh, s_mul):
    r0 = i * tq
    lanes = pl.ds(j * dh, dh)
    qi = qs[pl.ds(r0, tq), lanes]
    spans = ([(0, r0)] if i > 0 else []) + [(r0, tq)]
    logits = []
    for k0, n in spans:
        s = lax.dot_general(qi, ks[pl.ds(k0, n), lanes], (((1,), (1,)), ((), ())),
                            preferred_element_type=F32)
        if s_mul != 1.0:
            s = s * s_mul
        s = s - crow[:, k0:k0 + n]
        if k0 == r0:
            rows = lax.broadcasted_iota(jnp.int32, (tq, tq), 0)
            cols = lax.broadcasted_iota(jnp.int32, (tq, tq), 1)
            s = jnp.where(cols <= rows, s, NEG_BIG)
        logits.append(s)
    mx = jnp.max(logits[0], axis=1, keepdims=True)
    for s in logits[1:]:
        mx = jnp.maximum(mx, jnp.max(s, axis=1, keepdims=True))
    es = [jnp.exp(s - mx) for s in logits]
    tot = jnp.sum(es[0], axis=1, keepdims=True)
    for e in es[1:]:
        tot = tot + jnp.sum(e, axis=1, keepdims=True)
    inv = 1.0 / tot
    return [(k0, n, e * inv) for (k0, n), e in zip(spans, es)], qi


def attn_fwd(qg, z, ct_pad, d_model, n_heads, name):
    t = qg.shape[0]
    dh = d_model // n_heads
    hp = LANES // dh
    nqb, tp, tq = _attn_geometry(t)
    nblk = d_model // LANES
    q_mul, s_mul = _attn_scales(dh)

    def body(q_ref, og_ref, k_ref, v_ref, ct_ref, o_ref, mo_ref, qs, ks, vs, os_):
        pad = jnp.zeros((tp - t, LANES), BF16)
        qs[pl.ds(0, t), :] = (q_ref[...] * q_mul).astype(BF16)
        qs[pl.ds(t, tp - t), :] = pad
        for src, dst in ((k_ref, ks), (v_ref, vs)):
            dst[pl.ds(0, t), :] = src[...].astype(BF16)
            dst[pl.ds(t, tp - t), :] = pad
        for j in range(hp):
            crow = ct_ref[j]
            lanes = pl.ds(j * dh, dh)
            for i in range(nqb):
                pieces, _ = _attn_pieces(qs, ks, crow, j, i, tq, dh, s_mul)
                acc = None
                for k0, n, p in pieces:
                    part = jnp.dot(p.astype(BF16), vs[pl.ds(k0, n), lanes], preferred_element_type=F32)
                    acc = part if acc is None else acc + part
                os_[pl.ds(i * tq, tq), lanes] = acc
        o = os_[pl.ds(0, t), :]
        o_ref[...] = o
        mo_ref[...] = (o * _sigmoid(og_ref[...])).astype(BF16)

    col = lambda off: pl.BlockSpec((t, LANES), lambda p: (0, off + p))
    return pl.pallas_call(
        body, name=name, grid=(nblk,),
        in_specs=[col(0), col(nblk), col(0), col(nblk), pl.BlockSpec((hp, 1, tp), lambda p: (p, 0, 0))],
        out_specs=[col(0), col(0)],
        out_shape=[jax.ShapeDtypeStruct((t, d_model), F32), jax.ShapeDtypeStruct((t, d_model), BF16)],
        scratch_shapes=[pltpu.VMEM((tp, LANES), BF16)] * 3 + [pltpu.VMEM((tp, LANES), F32)],
        compiler_params=_cp("parallel"))(qg, qg, z, z, ct_pad)


def attn_bwd(dmo, qg, z, o, ct_pad, d_model, n_heads, name):
    t = qg.shape[0]
    dh = d_model // n_heads
    hp = LANES // dh
    nqb, tp, tq = _attn_geometry(t)
    nblk = d_model // LANES
    q_mul, s_mul = _attn_scales(dh)
    scale = dh ** -0.5
    tn_dims = (((0,), (0,)), ((), ()))
    nt_dims = (((1,), (1,)), ((), ()))

    def body(dmo_ref, q_ref, og_ref, k_ref, v_ref, o_ref, ct_ref,
             dq_ref, dog_ref, dk_ref, dv_ref, dct_ref, qs, ks, vs, dos, dqs, dks, dvs):
        pad = jnp.zeros((tp - t, LANES), BF16)
        sg = _sigmoid(og_ref[...])
        dmo_v = dmo_ref[...]
        dog_ref[...] = (dmo_v * o_ref[...] * sg * (1.0 - sg)).astype(BF16)
        dos[pl.ds(0, t), :] = (dmo_v * sg).astype(BF16)
        dos[pl.ds(t, tp - t), :] = pad
        qs[pl.ds(0, t), :] = (q_ref[...] * q_mul).astype(BF16)
        qs[pl.ds(t, tp - t), :] = pad
        for src, dst in ((k_ref, ks), (v_ref, vs)):
            dst[pl.ds(0, t), :] = src[...].astype(BF16)
            dst[pl.ds(t, tp - t), :] = pad
        dks[...] = jnp.zeros_like(dks)
        dvs[...] = jnp.zeros_like(dvs)
        dct_ref[...] = jnp.zeros_like(dct_ref)
        for j in range(hp):
            crow = ct_ref[j]
            lanes = pl.ds(j * dh, dh)
            for i in range(nqb):
                pieces, qi = _attn_pieces(qs, ks, crow, j, i, tq, dh, s_mul)
                do_i = dos[pl.ds(i * tq, tq), lanes]
                dps = [lax.dot_general(do_i, vs[pl.ds(k0, n), lanes], nt_dims, preferred_element_type=F32)
                       for k0, n, _ in pieces]
                row = None
                for (_, _, p), dp in zip(pieces, dps):
                    part = jnp.sum(p * dp, axis=1, keepdims=True)
                    row = part if row is None else row + part
                dq_i = None
                for (k0, n, p), dp in zip(pieces, dps):
                    ds = p * (dp - row)
                    ds_b = ds.astype(BF16)
                    keys = pl.ds(k0, n)
                    part = jnp.dot(ds_b, ks[keys, lanes], preferred_element_type=F32)
                    dq_i = part if dq_i is None else dq_i + part
                    dks[keys, lanes] += lax.dot_general(ds_b, qi, tn_dims, preferred_element_type=F32) * s_mul
                    dvs[keys, lanes] += lax.dot_general(p.astype(BF16), do_i, tn_dims,
                                                        preferred_element_type=F32)
                    dct_ref[j, :, keys] -= jnp.sum(ds, axis=0, keepdims=True)
                dqs[pl.ds(i * tq, tq), lanes] = dq_i * scale
        dq_ref[...] = dqs[pl.ds(0, t), :].astype(BF16)
        dk_ref[...] = dks[pl.ds(0, t), :]
        dv_ref[...] = dvs[pl.ds(0, t), :]

    col = lambda off: pl.BlockSpec((t, LANES), lambda p: (0, off + p))
    big = lambda dt: jax.ShapeDtypeStruct((t, d_model), dt)
    return pl.pallas_call(
        body, name=name, grid=(nblk,),
        in_specs=[col(0), col(0), col(nblk), col(0), col(nblk), col(0),
                  pl.BlockSpec((hp, 1, tp), lambda p: (p, 0, 0))],
        out_specs=[col(0), col(0), col(0), col(0), pl.BlockSpec((hp, 1, tp), lambda p: (p, 0, 0))],
        out_shape=[big(BF16), big(BF16), big(F32), big(F32),
                   jax.ShapeDtypeStruct((n_heads, 1, tp), F32)],
        scratch_shapes=[pltpu.VMEM((tp, LANES), BF16)] * 4 + [pltpu.VMEM((tp, LANES), F32)] * 3,
        compiler_params=_cp("parallel"))(dmo, qg, qg, z, z, o, ct_pad)


def cast_into_slot(shard, w, index, name):
    r, c = w.shape[-2:]
    rh = r // 2
    tr = _tile(rh, 512, 16)
    n = rh // tr
    if w.ndim == 3:
        w_spec = pl.BlockSpec((None, tr, c), lambda h, i, sh: (index, h * n + i, 0))
    else:
        w_spec = pl.BlockSpec((tr, c), lambda h, i, sh: (h * n + i, 0))

    def body(sh_ref, w_ref, o_ref):
        del sh_ref
        o_ref[...] = w_ref[...].astype(BF16)

    return pl.pallas_call(
        body, name=name,
        grid_spec=pltpu.PrefetchScalarGridSpec(
            num_scalar_prefetch=1, grid=(2, n), in_specs=[w_spec],
            out_specs=pl.BlockSpec((None, None, tr, c), lambda h, i, sh: (sh[0], h, i, 0))),
        out_shape=jax.ShapeDtypeStruct((N_SHARDS, 2, rh, c), BF16),
        compiler_params=_cp("parallel", "parallel"))(shard, w)


def owner_sum(shard_core, g, recv, buf, layer, name):
    _, _, rh, c = g.shape
    n_recv = recv.shape[0]
    tr = _tile(rh, 512, 16)
    slot = (lambda sc: sc[0]) if layer is None else (lambda sc: layer)

    def body(sc_ref, a_ref, r_ref, buf_ref, out_ref):
        del sc_ref, buf_ref
        acc = a_ref[...].astype(F32)
        for k in range(n_recv):
            acc = acc + r_ref[k].astype(F32)
        out_ref[...] = acc

    return pl.pallas_call(
        body, name=name,
        grid_spec=pltpu.PrefetchScalarGridSpec(
            num_scalar_prefetch=1, grid=(rh // tr,),
            in_specs=[pl.BlockSpec((None, None, tr, c), lambda i, sc: (sc[0], sc[1], i, 0)),
                      pl.BlockSpec((n_recv, tr, c), lambda i, sc: (0, i, 0)),
                      pl.BlockSpec(memory_space=pl.ANY)],
            out_specs=pl.BlockSpec((None, None, tr, c), lambda i, sc: (slot(sc), sc[1], i, 0))),
        out_shape=jax.ShapeDtypeStruct(buf.shape, F32),
        input_output_aliases={3: 0},
        compiler_params=_cp("parallel"))(shard_core, g, recv, buf)


def adamw(w, g, m, v, name, emit_g=False):
    r, c = w.shape
    tr = _tile(r, 512, SUBLANES)
    c1 = 1.0 - ADAM_B1 ** ADAM_STEP
    c2 = 1.0 - ADAM_B2 ** ADAM_STEP
    n_out = 4 if emit_g else 3

    def body(w_ref, g_ref, m_ref, v_ref, d_ref, mo_ref, vo_ref, *go_ref):
        gv = g_ref[...]
        if emit_g:
            go_ref[0][...] = gv
        mn = ADAM_B1 * m_ref[...] + (1.0 - ADAM_B1) * gv
        vn = ADAM_B2 * v_ref[...] + (1.0 - ADAM_B2) * (gv * gv)
        m_hat = mn / c1
        v_hat = vn / c2
        d_ref[...] = -ADAM_LR * (m_hat / (jnp.sqrt(v_hat) + ADAM_EPS) + ADAM_WD * w_ref[...])
        mo_ref[...] = mn
        vo_ref[...] = vn

    blk = pl.BlockSpec((tr, c), lambda i: (i, 0))
    return pl.pallas_call(
        body, name=name, grid=(r // tr,), in_specs=[blk] * 4, out_specs=[blk] * n_out,
        out_shape=[jax.ShapeDtypeStruct((r, c), F32)] * n_out,
        compiler_params=_cp("parallel"))(w, g, m, v)


def _coords():
    return lax.axis_index("x"), lax.axis_index("y"), lax.axis_index("c")


def _exchange(name, ins, out_shapes, plan, in_place=False):
    n_in = len(ins)
    n_out = len(out_shapes)
    n_rem = len(plan([None] * n_in, [None] * n_out, True))

    def body(*refs):
        in_refs = refs[:n_in]
        out_refs = refs[n_in:n_in + n_out]
        send_sems, recv_sems = refs[n_in + n_out:]
        copies = [pltpu.make_async_remote_copy(
            src_ref=src, dst_ref=dst, send_sem=send_sems.at[q], recv_sem=recv_sems.at[q],
            device_id=peer, device_id_type=pl.DeviceIdType.MESH)
            for q, (src, dst, peer) in enumerate(plan(list(in_refs), list(out_refs), False))]
        for cp in copies:
            cp.start()
        for cp in copies:
            cp.wait_recv()
        for cp in copies:
            cp.wait_send()

    hbm = pl.BlockSpec(memory_space=pl.ANY)
    return pl.pallas_call(
        body, name=name, in_specs=[hbm] * n_in, out_specs=[hbm] * n_out, out_shape=out_shapes,
        input_output_aliases={i: i for i in range(n_in)} if in_place else {},
        scratch_shapes=[pltpu.SemaphoreType.DMA((n_rem,)), pltpu.SemaphoreType.DMA((n_rem,))],
        compiler_params=pltpu.CompilerParams(has_side_effects=True))(*ins)


def _split_start(name, groups, plan):
    flat = [a for grp in groups for a in grp]
    n, n_grp = len(flat), len(groups)
    counts = [len(plan(g, [None] * len(grp), True)) for g, grp in enumerate(groups)]

    def body(*refs):
        ins, sems, token = refs[:n], refs[n:n + 2 * n_grp], refs[-1]
        pos = 0
        for g, grp in enumerate(groups):
            arrs = list(ins[pos:pos + len(grp)])
            pos += len(grp)
            for q, (src, dst, peer) in enumerate(plan(g, arrs, False)):
                pltpu.make_async_remote_copy(
                    src_ref=src, dst_ref=dst, send_sem=sems[2 * g].at[q], recv_sem=sems[2 * g + 1].at[q],
                    device_id=peer, device_id_type=pl.DeviceIdType.MESH).start()
        token[...] = jnp.zeros_like(token)

    hbm = pl.BlockSpec(memory_space=pltpu.HBM)
    sem = pl.BlockSpec(memory_space=pltpu.SEMAPHORE)
    outs = pl.pallas_call(
        body, name=name,
        out_shape=[pltpu.SemaphoreType.DMA((cnt,)) for cnt in counts for _ in range(2)]
        + [pltpu.HBM(a.shape, a.dtype) for a in flat] + [jax.ShapeDtypeStruct((SUBLANES, LANES), F32)],
        in_specs=[hbm] * n, out_specs=[sem] * (2 * n_grp) + [hbm] * n + [pl.BlockSpec(memory_space=pltpu.VMEM)],
        input_output_aliases={i: 2 * n_grp + i for i in range(n)},
        compiler_params=pltpu.CompilerParams(has_side_effects=pltpu.SideEffectType.DATAFLOW_SIDE_EFFECTING),
    )(*[pltpu.with_memory_space_constraint(a, pltpu.HBM) for a in flat])
    started, pos = [], 2 * n_grp
    for g, grp in enumerate(groups):
        started.append((outs[2 * g], outs[2 * g + 1], list(outs[pos:pos + len(grp)])))
        pos += len(grp)
    return started, outs[-1]


def _split_wait(name, started, after, plan_g):
    send_sems, recv_sems, arrs = started
    n = len(arrs)

    def body(*refs):
        ins, ssem, rsem = list(refs[:n]), refs[n], refs[n + 1]
        for q, (src, dst, peer) in enumerate(plan_g(ins, False)):
            cp = pltpu.make_async_remote_copy(
                src_ref=src, dst_ref=dst, send_sem=ssem.at[q], recv_sem=rsem.at[q],
                device_id=peer, device_id_type=pl.DeviceIdType.MESH)
            cp.wait_send()
            cp.wait_recv()

    hbm = pl.BlockSpec(memory_space=pltpu.HBM)
    sem = pl.BlockSpec(memory_space=pltpu.SEMAPHORE)
    return pl.pallas_call(
        body, name=name, out_shape=[pltpu.HBM(a.shape, a.dtype) for a in arrs],
        in_specs=[hbm] * n + [sem, sem, pl.BlockSpec(memory_space=pl.ANY)], out_specs=[hbm] * n,
        input_output_aliases={i: i for i in range(n)},
        compiler_params=pltpu.CompilerParams(has_side_effects=pltpu.SideEffectType.DATAFLOW_SIDE_EFFECTING),
    )(*arrs, send_sems, recv_sems, after)


def _gather_ici_plan(arrs, count_only):
    if count_only:
        return [None] * (3 * len(arrs))
    x, y, c = _coords()
    pushes = []
    for a in arrs:
        mine = a.at[2 * x + y, c]
        pushes += [(mine, mine, peer) for peer, _ in _other_chips(x, y, c)]
    return pushes


def _all_to_all_plan(arrs, count_only):
    half = len(arrs) // 2
    if count_only:
        return [None] * (7 * half)
    x, y, c = _coords()
    pushes = []
    for src, land in zip(arrs[:half], arrs[half:]):
        for flips in range(1, 8):
            px, py, pc = (1 - x if flips & 4 else x), (1 - y if flips & 2 else y), (1 - c if flips & 1 else c)
            pushes.append((src.at[2 * px + py, pc], land.at[flips - 1], (px, py, pc)))
    return pushes


def _forward_plan(arrs, count_only):
    if count_only:
        return [None] * (3 * len(arrs))
    x, y, c = _coords()
    pushes = []
    for a in arrs:
        for _, src_shard in _other_chips(x, y, c):
            slab = a.at[src_shard, c]
            pushes.append((slab, slab, (x, y, 1 - c)))
    return pushes


def forward_to_sibling(bufs, name):
    shapes = [jax.ShapeDtypeStruct(b.shape, b.dtype) for b in bufs]
    return _exchange(name, bufs, shapes, lambda ins, outs, cnt: _forward_plan(outs, cnt), in_place=True)


def _other_chips(x, y, c):
    return [((1 - x, y, c), 2 * (1 - x) + y), ((x, 1 - y, c), 2 * x + 1 - y),
            ((1 - x, 1 - y, c), 2 * (1 - x) + 1 - y)]


def join_halves(bufs, everywhere, name):
    slots = [(i, l) for i, b in enumerate(bufs) for l in range(b.shape[0])]
    n = len(bufs)

    def plan(ins, outs, count_only):
        if count_only:
            return [None] * (len(slots) + 7)
        x, y, c = _coords()
        pushes = [(outs[i].at[l, c], outs[i].at[l, c], (x, y, 1 - c)) for i, l in slots]
        mine = outs[n].at[2 * x + y, c]
        for flips in range(1, 8):
            peer = (1 - x if flips & 4 else x, 1 - y if flips & 2 else y, 1 - c if flips & 1 else c)
            pushes.append((mine, mine, peer))
        return pushes

    arrs = list(bufs) + [everywhere]
    shapes = [jax.ShapeDtypeStruct(b.shape, b.dtype) for b in arrs]
    return _exchange(name, arrs, shapes, plan, in_place=True)


def _pack(arrays, multiple):
    flat = jnp.concatenate([a.reshape(-1) for a in arrays])
    n = flat.shape[0]
    return jnp.pad(flat, (0, _round_up(n, multiple) - n))


def _unpack(flat, shapes):
    out, pos = [], 0
    for shp in shapes:
        n = math.prod(shp)
        out.append(flat[pos:pos + n].reshape(shp))
        pos += n
    return out


def _block_diag(w, per_group):
    nb, bs, _ = w.shape
    g = nb // per_group
    w4 = w.reshape(g, per_group, bs, bs)
    eye = jnp.eye(per_group, dtype=w.dtype)
    full = w4[:, :, :, None, :] * eye[None, :, None, :, None]
    return full.reshape(g, per_group * bs, per_group * bs).astype(BF16)


def _block_diag_extract(full, per_group, bs):
    g = full.shape[0]
    f5 = full.reshape(g, per_group, bs, per_group, bs)
    idx = jnp.arange(per_group)
    picked = f5[:, idx, :, idx, :]
    return jnp.moveaxis(picked, 0, 1).reshape(g * per_group, bs, bs)


def kernel(x, meta, a_w_in, a_conv_w, a_conv_b, a_w_r, a_b_r, a_w_i, a_b_i, a_lambda, a_w_out, kv_w, kv_f_b, b_w_in, b_w_out, f_w_in, f_conv_w, f_conv_b, f_w_out, ln1_g, ln1_b, ln2_g, ln2_b, loss_target, m_meta, m_a_w_in, m_a_conv_w, m_a_conv_b, m_a_w_r, m_a_b_r, m_a_w_i, m_a_b_i, m_a_lambda, m_a_w_out, m_kv_w, m_kv_f_b, m_b_w_in, m_b_w_out, m_f_w_in, m_f_conv_w, m_f_conv_b, m_f_w_out, m_ln1_g, m_ln1_b, m_ln2_g, m_ln2_b, v_meta, v_a_w_in, v_a_conv_w, v_a_conv_b, v_a_w_r, v_a_b_r, v_a_w_i, v_a_b_i, v_a_lambda, v_a_w_out, v_kv_w, v_kv_f_b, v_b_w_in, v_b_w_out, v_f_w_in, v_f_conv_w, v_f_conv_b, v_f_w_out, v_ln1_g, v_ln1_b, v_ln2_g, v_ln2_b):
    weights = dict(meta=meta, a_w_in=a_w_in, a_conv_w=a_conv_w, a_conv_b=a_conv_b, a_w_r=a_w_r, a_b_r=a_b_r,
                   a_w_i=a_w_i, a_b_i=a_b_i, a_lambda=a_lambda, a_w_out=a_w_out, kv_w=kv_w, kv_f_b=kv_f_b,
                   b_w_in=b_w_in, b_w_out=b_w_out, f_w_in=f_w_in, f_conv_w=f_conv_w, f_conv_b=f_conv_b,
                   f_w_out=f_w_out, ln1_g=ln1_g, ln1_b=ln1_b, ln2_g=ln2_g, ln2_b=ln2_b)
    mom_m = dict(meta=m_meta, a_w_in=m_a_w_in, a_conv_w=m_a_conv_w, a_conv_b=m_a_conv_b, a_w_r=m_a_w_r,
                 a_b_r=m_a_b_r, a_w_i=m_a_w_i, a_b_i=m_a_b_i, a_lambda=m_a_lambda, a_w_out=m_a_w_out,
                 kv_w=m_kv_w, kv_f_b=m_kv_f_b, b_w_in=m_b_w_in, b_w_out=m_b_w_out, f_w_in=m_f_w_in,
                 f_conv_w=m_f_conv_w, f_conv_b=m_f_conv_b, f_w_out=m_f_w_out, ln1_g=m_ln1_g, ln1_b=m_ln1_b,
                 ln2_g=m_ln2_g, ln2_b=m_ln2_b)
    mom_v = dict(meta=v_meta, a_w_in=v_a_w_in, a_conv_w=v_a_conv_w, a_conv_b=v_a_conv_b, a_w_r=v_a_w_r,
                 a_b_r=v_a_b_r, a_w_i=v_a_w_i, a_b_i=v_a_b_i, a_lambda=v_a_lambda, a_w_out=v_a_w_out,
                 kv_w=v_kv_w, kv_f_b=v_kv_f_b, b_w_in=v_b_w_in, b_w_out=v_b_w_out, f_w_in=v_f_w_in,
                 f_conv_w=v_f_conv_w, f_conv_b=v_f_conv_b, f_w_out=v_f_w_out, ln1_g=v_ln1_g, ln1_b=v_ln1_b,
                 ln2_g=v_ln2_g, ln2_b=v_ln2_b)
    return _train_step(x, loss_target, weights, mom_m, mom_v)


WEIGHT_ORDER = ("meta", "a_w_in", "a_conv_w", "a_conv_b", "a_w_r", "a_b_r", "a_w_i", "a_b_i", "a_lambda",
                "a_w_out", "kv_w", "kv_f_b", "b_w_in", "b_w_out", "f_w_in", "f_conv_w", "f_conv_b",
                "f_w_out", "ln1_g", "ln1_b", "ln2_g", "ln2_b")
BIG = ("a_w_in", "a_w_out", "kv_w", "b_w_in", "b_w_out", "f_w_in", "f_w_out")
OUT_TYPE = ("a_w_out", "b_w_out", "f_w_out")
SMALL_SHARDED = (("meta", 1), ("a_conv_w", 2), ("a_conv_b", 1), ("a_b_r", 1), ("a_b_i", 1), ("a_lambda", 1),
                 ("f_conv_w", 2))
SMALL_REPLICATED = ("a_w_r", "a_w_i", "kv_f_b", "f_conv_b", "ln1_g", "ln1_b", "ln2_g", "ln2_b")


def _train_step(x, loss_target, weights, mom_m, mom_v):
    S = N_SHARDS
    seq, d = x.shape[1], x.shape[2]
    nm = weights["meta"].shape[0]
    la = weights["a_w_in"].shape[0]
    lb = weights["b_w_in"].shape[0]
    depth = la + lb
    dr = weights["a_w_out"].shape[1] * S
    nb, bs = weights["a_w_r"].shape[1], weights["a_w_r"].shape[2]
    per_group = (LANES // math.gcd(bs, LANES))
    gs = per_group * bs
    heads = weights["kv_f_b"].shape[0]
    dff = weights["f_w_out"].shape[1] * S
    nkv = 2 * d + heads
    nkv_s = weights["kv_w"].shape[1]
    nkvp = _round_up(2 * d + LANES, 768) if 2 * d + LANES > 768 else 2 * d + LANES
    alpha = (2 * depth) ** 0.25
    xi, yi, ci = _coords()
    shard = 2 * xi + yi
    shard_arr = jnp.reshape(shard, (1,)).astype(jnp.int32)
    shard_core_arr = jnp.stack([shard, ci]).astype(jnp.int32)

    def mixer_keys(l):
        if l < la:
            return [("a_w_in", l), ("a_w_out", l)]
        return ([("kv_w", 0)] if l == la else []) + [("b_w_in", l - la), ("b_w_out", l - la)]

    def ffn_keys(l):
        return [("f_w_in", l), ("f_w_out", l)]

    assert la >= 1
    groups = [mixer_keys(0)[:1], mixer_keys(0)[1:] + ffn_keys(0)]
    groups += [mixer_keys(l) + ffn_keys(l) for l in range(1, depth)]
    keys = [kl for grp in groups for kl in grp]
    shape2d = {(k, i): weights[k].shape[-2:] for k, i in keys}
    small_local = [weights[k] for k, _ in SMALL_SHARDED]
    sm_flat = _pack(small_local, 2 * SUBLANES * LANES).reshape(1, 2, -1, LANES)
    sm_slot = lax.dynamic_update_slice_in_dim(lax.empty((S,) + sm_flat.shape[1:], F32), sm_flat, shard, axis=0)
    parts = [[cast_into_slot(shard_arr, weights[k], i, f"cast_{k}{i}") for k, i in grp] for grp in groups]
    parts[0].append(sm_slot)
    in_flight, start_token = _split_start("gather_start", parts, lambda g, refs, cnt: _gather_ici_plan(refs, cnt))
    gw = {}

    def use(g, arrs):
        for kl, a in zip(groups[g], arrs):
            rows, cols = shape2d[kl]
            gw[kl] = a.reshape(S * rows, cols) if kl[0] in OUT_TYPE else a.reshape(S, rows, cols)
        return arrs

    def fetch(g, after):
        arrs = _split_wait(f"gather_wait_{g}", in_flight[g], after, _gather_ici_plan)
        return use(g, forward_to_sibling(arrs, f"gather_fwd_{g}"))

    forwarding = {}

    def fetch_early(g, after):
        arrs = _split_wait(f"gather_wait_{g}", in_flight[g], after, _gather_ici_plan)
        started, token = _split_start(f"gather_fwd_start_{g}", [arrs], lambda _, refs, cnt: _forward_plan(refs, cnt))
        forwarding[g] = started[0]
        return token[0:1, 0:1]

    def fetch_end(g, after):
        return use(g, _split_wait(f"gather_fwd_wait_{g}", forwarding[g], after, _forward_plan))

    sm_all = fetch(0, start_token)[-1].reshape(S, -1)
    small_full = {}
    per_shard = [_unpack(sm_all[s], [a.shape for a in small_local]) for s in range(S)]
    for idx, (k, axis) in enumerate(SMALL_SHARDED):
        small_full[k] = jnp.concatenate([per_shard[s][idx] for s in range(S)], axis=axis)
    fb_pad = jnp.pad(weights["kv_f_b"], (0, LANES - heads)).reshape(1, LANES)
    wr_g = [_block_diag(weights["a_w_r"][l], per_group) for l in range(la)]
    wi_g = [_block_diag(weights["a_w_i"][l], per_group) for l in range(la)]
    row = lambda v: v.reshape(1, -1)

    h, hb = embed_fwd(small_full["meta"], x[0], "embed")
    saved = []
    kvz = ct_pad = None
    _, tp, _ = _attn_geometry(nm + seq)
    t = nm + seq
    for l in range(depth):
        sv = {"hb_in": hb}
        if l > 0:
            (fetch_end if l + 1 in forwarding else fetch)(l + 1, hb)
        if l < la:
            gr = mm_in(hb, gw[("a_w_in", l)], F32, f"a{l}_in")
            rc, rcb = a_conv_fwd(gr, small_full["a_conv_w"][l], row(small_full["a_conv_b"][l]), f"a{l}_conv")
            r_pre, i_pre = mm_bd(rcb, wr_g[l], wi_g[l], f"a{l}_gates")
            hs, gb = a_elem_fwd(gr, rc, r_pre, i_pre, row(small_full["a_b_r"][l]), row(small_full["a_b_i"][l]),
                                row(small_full["a_lambda"][l]), f"a{l}_lru")
            if l == 0:
                fetch(1, gb)
            mixed, w_mix = gb, gw[("a_w_out", l)]
            sv.update(gr=gr, rc=rc, rcb=rcb, r_pre=r_pre, i_pre=i_pre, hs=hs, gb=gb)
        else:
            j = l - la
            if j == 0:
                kv_cat = jnp.moveaxis(gw[("kv_w", 0)], 0, 1).reshape(d, S * nkv_s)
                kv_pad = jnp.pad(kv_cat, ((0, 0), (0, nkvp - nkv))).reshape(1, d, nkvp)
                kvz = mm_in(hb, kv_pad, F32, "kv_proj")
                cum = kv_fwd(kvz, fb_pad, d, "kv_forget")
                ct_pad = jnp.pad(cum[:, :heads].T, ((0, 0), (0, tp - t))).reshape(heads, 1, tp)
                kv_hb = hb
            qg = mm_in(hb, gw[("b_w_in", j)], F32, f"b{j}_in")
            o, mob = attn_fwd(qg, kvz, ct_pad, d, heads, f"b{j}_attn")
            mixed, w_mix = mob, gw[("b_w_out", j)]
            sv.update(qg=qg, o=o, mob=mob)
        h1, h1b, xh1, rs1 = mm_out_ln(mixed, w_mix, h, row(weights["ln1_g"][l]), row(weights["ln1_b"][l]), alpha,
                                      f"mix{l}_out_ln1")
        f_bias = row(weights["f_conv_b"][l])
        if l >= 1 and l + 2 < len(groups):
            f_bias = f_bias + fetch_early(l + 2, h1b)
        zf = mm_in(h1b, gw[("f_w_in", l)], F32, f"f{l}_in")
        ffb = f_elem_fwd(zf, small_full["f_conv_w"][l], f_bias, f"f{l}_act")
        h2, h2b, xh2, rs2 = mm_out_ln(ffb, gw[("f_w_out", l)], h1, row(weights["ln2_g"][l]), row(weights["ln2_b"][l]),
                                      alpha, f"f{l}_out_ln2")
        sv.update(h1b=h1b, xh1=xh1, rs1=rs1, zf=zf, ffb=ffb, xh2=xh2, rs2=rs2)
        saved.append(sv)
        h, hb = h2, h2b
    loss11, dy = loss_fwd_bwd(h, loss_target[0], nm, "loss")

    grads = {}

    def by_owner(kl, g3):
        rows, cols = shape2d[kl]
        grads[kl] = g3.reshape(S, 2, rows // 2, cols)

    reducing = []

    def send_grads(g, names, arrays):
        lands = [lax.empty((7,) + a.shape[2:], a.dtype) for a in arrays]
        started, token = _split_start(f"grad_a2a_start_{g}", [list(arrays) + lands],
                                      lambda _, refs, cnt: _all_to_all_plan(refs, cnt))
        reducing.append((names, started[0]))
        return token[0:1, 0:1]

    def after_start(vec, zero):
        return vec if zero is None else vec + zero

    pin = None

    g_small = {}
    per_layer = {k: [None] * n for k, n in (
        ("a_conv_w", la), ("a_conv_b", la), ("a_w_r", la), ("a_b_r", la), ("a_w_i", la), ("a_b_i", la),
        ("a_lambda", la), ("f_conv_w", depth), ("f_conv_b", depth), ("ln1_g", depth), ("ln1_b", depth),
        ("ln2_g", depth), ("ln2_b", depth))}
    adds = [(dy, 1.0)]
    dks, dvs, dcs = [], [], []
    for l in reversed(range(depth)):
        sv = saved[l]
        ds2, ds2b, dg2, db2 = ln_bwd(adds, sv["xh2"], sv["rs2"], after_start(row(weights["ln2_g"][l]), pin),
                                     f"ln2_{l}_bwd")
        pin = None
        per_layer["ln2_g"][l], per_layer["ln2_b"][l] = dg2[0], db2[0]
        dff_v = mm_out_nt(ds2b, gw[("f_w_out", l)], f"f{l}_out_dx")
        by_owner(("f_w_out", l), mm_tn(sv["ffb"], [ds2b], 1, f"f{l}_out_dw"))
        dzg, dzv, dwg, dwv, dbg, dbv = f_elem_bwd(sv["zf"], dff_v, small_full["f_conv_w"][l],
                                                  row(weights["f_conv_b"][l]), f"f{l}_act_bwd")
        per_layer["f_conv_w"][l] = jnp.concatenate([dwg, dwv], axis=1)
        per_layer["f_conv_b"][l] = jnp.concatenate([dbg, dbv], axis=1)[0]
        dh1_f = mm_in_nt([dzg, dzv], gw[("f_w_in", l)], f"f{l}_in_dx")
        by_owner(("f_w_in", l), mm_tn(sv["h1b"], [dzg, dzv], S, f"f{l}_in_dw"))
        ds1, ds1b, dg1, db1 = ln_bwd([(ds2, alpha), (dh1_f, 1.0)], sv["xh1"], sv["rs1"],
                                     row(weights["ln1_g"][l]), f"ln1_{l}_bwd")
        per_layer["ln1_g"][l], per_layer["ln1_b"][l] = dg1[0], db1[0]
        if l < la:
            dgv = mm_out_nt(ds1b, gw[("a_w_out", l)], f"a{l}_out_dx")
            by_owner(("a_w_out", l), mm_tn(sv["gb"], [ds1b], 1, f"a{l}_out_dw"))
            if l == 0:
                pin = send_grads(1, groups[1], [grads[kl] for kl in groups[1]])
            dgate_b, drp_b, dip_b, drc_d, dlam, dbr, dbi = a_elem_bwd(
                dgv, sv["gr"], sv["rc"], sv["r_pre"], sv["i_pre"], sv["hs"], row(small_full["a_b_r"][l]),
                row(small_full["a_b_i"][l]), after_start(row(small_full["a_lambda"][l]), pin), f"a{l}_lru_bwd")
            pin = None
            drc_g = mm_bd_nt(drp_b, dip_b, wr_g[l], wi_g[l], f"a{l}_gates_dx")
            dwr_g, dwi_g = mm_bd_tn(sv["rcb"], drp_b, dip_b, gs, f"a{l}_gates_dw")
            dgr_b, dcw, dcb = a_conv_bwd(drc_d, drc_g, sv["gr"], small_full["a_conv_w"][l], dgate_b,
                                         f"a{l}_conv_bwd")
            per_layer["a_w_r"][l] = _block_diag_extract(dwr_g, per_group, bs)
            per_layer["a_w_i"][l] = _block_diag_extract(dwi_g, per_group, bs)
            per_layer["a_lambda"][l], per_layer["a_b_r"][l], per_layer["a_b_i"][l] = dlam[0], dbr[0], dbi[0]
            per_layer["a_conv_w"][l], per_layer["a_conv_b"][l] = dcw, dcb[0]
            by_owner(("a_w_in", l), mm_tn(sv["hb_in"], [dgr_b], S, f"a{l}_in_dw"))
            gate_w = gw[("a_w_in", l)]
            if l == 0:
                zero = send_grads(0, groups[0], [grads[kl] for kl in groups[0]])
                gate_w = gate_w + zero.astype(BF16)[None]
            dh_m = mm_in_nt([dgr_b], gate_w, f"a{l}_in_dx")
        else:
            j = l - la
            dmo = mm_out_nt(ds1b, gw[("b_w_out", j)], f"b{j}_out_dx")
            by_owner(("b_w_out", j), mm_tn(sv["mob"], [ds1b], 1, f"b{j}_out_dw"))
            dq_b, dog_b, dk, dv, dct = attn_bwd(dmo, sv["qg"], kvz, sv["o"], ct_pad, d, heads, f"b{j}_attn_bwd")
            dks.append(dk)
            dvs.append(dv)
            dcs.append(jnp.pad(dct[:, 0, :t].T, ((0, 0), (0, LANES - heads))))
            dh_m = mm_in_nt([dq_b, dog_b], gw[("b_w_in", j)], f"b{j}_in_dx")
            by_owner(("b_w_in", j), mm_tn(sv["hb_in"], [dq_b, dog_b], S, f"b{j}_in_dw"))
        adds = [(ds1, alpha), (dh_m, 1.0)]
        if l == la:
            dzf_b, dfb = kv_bwd(dcs, kvz, fb_pad, d, "kv_forget_bwd")
            dk_b = add_cast(dks[0], dks[1], "kv_dk") if lb == 2 else None
            dv_b = add_cast(dvs[0], dvs[1], "kv_dv") if lb == 2 else None
            dz_kv = jnp.concatenate([dk_b, dv_b, dzf_b, jnp.zeros((t, nkvp - 2 * d - LANES), BF16)], axis=1)
            dh_kv = mm_in_nt([dz_kv], kv_pad, "kv_proj_dx")
            kv_dw = mm_tn(kv_hb, [dz_kv], 1, "kv_proj_dw")
            by_owner(("kv_w", 0), jnp.moveaxis(kv_dw[0, :, :nkv].reshape(d, S, nkv_s), 1, 0))
            g_small["kv_f_b"] = dfb[0, :heads]
            adds.append((dh_kv, 1.0))
        if l > 0:
            pin = send_grads(l + 1, groups[l + 1], [grads[kl] for kl in groups[l + 1]])
    g_meta, g_x = embed_bwd(adds, nm, "embed_bwd")

    small_names = list(SMALL_REPLICATED) + [k for k, _ in SMALL_SHARDED]
    g_small["meta"] = g_meta
    for k, vals in per_layer.items():
        g_small[k] = jnp.stack(vals)
    small_shapes = {k: (weights[k].shape if k in SMALL_REPLICATED else g_small[k].shape) for k in small_names}
    sm_g = _pack([g_small[k].reshape(small_shapes[k]) for k in small_names] + [loss11.reshape(1)],
                 S * 2 * SUBLANES * LANES)
    sm_g = sm_g.reshape(S, 2, -1, LANES)
    send_grads(len(groups), [("small", 0)], [sm_g])

    fin = {"small": lax.empty(sm_g.shape, F32)}
    for kl in keys:
        n_stack = weights[kl[0]].shape[0] if weights[kl[0]].ndim == 3 else 1
        rows, cols = shape2d[kl]
        fin.setdefault(kl[0], lax.empty((n_stack, 2, rows // 2, cols), F32))
    for g, (names_g, started) in enumerate(reducing):
        arrs = _split_wait(f"grad_a2a_wait_{g}", started, g_x, _all_to_all_plan)
        half = len(names_g)
        for i, (kl, cs, rv) in enumerate(zip(names_g, arrs[:half], arrs[half:])):
            fin[kl[0]] = owner_sum(shard_core_arr, cs, rv, fin[kl[0]], None if kl[0] == "small" else kl[1],
                                   f"owner_sum_{g}_{i}")
    names = list(BIG) + ["small"]
    joined = dict(zip(names, join_halves([fin[k] for k in BIG], fin["small"], "grad_join")))
    sm_red = joined["small"].reshape(-1)

    out_g, out_d, out_m, out_v = {}, {}, {}, {}
    for k in BIG:
        w2 = weights[k].reshape(-1, weights[k].shape[-1])
        g2 = joined[k].reshape(w2.shape)
        dlt, mn, vn, g_out = adamw(w2, g2, mom_m[k].reshape(w2.shape), mom_v[k].reshape(w2.shape), "adamw_" + k,
                                   emit_g=True)
        shp = weights[k].shape
        out_g[k], out_d[k], out_m[k], out_v[k] = g_out.reshape(shp), dlt.reshape(shp), mn.reshape(shp), vn.reshape(shp)
    *small_sums, loss_sum = _unpack(sm_red, [small_shapes[k] for k in small_names] + [(1,)])
    sm_vals = dict(zip(small_names, small_sums))
    local_small = {}
    for k in SMALL_REPLICATED:
        local_small[k] = sm_vals[k]
    for k, axis in SMALL_SHARDED:
        size = weights[k].shape[axis]
        local_small[k] = lax.dynamic_slice_in_dim(sm_vals[k], shard * size, size, axis=axis)
    for k in small_names:
        shp = weights[k].shape
        two_d = (-1, shp[-1]) if len(shp) > 1 else (1, -1)
        dlt, mn, vn = adamw(weights[k].reshape(two_d), local_small[k].reshape(two_d), mom_m[k].reshape(two_d),
                            mom_v[k].reshape(two_d), "adamw_" + k)
        out_g[k], out_d[k], out_m[k], out_v[k] = local_small[k], dlt.reshape(shp), mn.reshape(shp), vn.reshape(shp)

    return (loss_sum[0], g_x[None], *[out_g[k] for k in WEIGHT_ORDER], *[out_d[k] for k in WEIGHT_ORDER],
            *[out_m[k] for k in WEIGHT_ORDER], *[out_v[k] for k in WEIGHT_ORDER])
```

```python
import functools
import math

import jax
import jax.numpy as jnp
from jax import lax
from jax.experimental import pallas as pl
from jax.experimental.pallas import tpu as pltpu

F32 = jnp.float32
BF16 = jnp.bfloat16

LRU_C = 8.0
LN_EPS = 1e-5
ADAM_LR = 0.001
ADAM_B1 = 0.9
ADAM_B2 = 0.999
ADAM_EPS = 1e-08
ADAM_WD = 0.01
ADAM_STEP = 10

LANES = 128
SUBLANES = 8
V7X_VMEM_BYTES = 64 * 1024 * 1024
VMEM_LIMIT = V7X_VMEM_BYTES * 7 // 8
N_SHARDS = 4
GELU_C0 = math.sqrt(2.0 / math.pi)
GELU_C1 = 0.044715
NEG_BIG = -1e30


def _cp(*sem):
    return pltpu.CompilerParams(dimension_semantics=tuple(sem), vmem_limit_bytes=VMEM_LIMIT)


def _tile(n, cap, mult=LANES):
    best = None
    d = mult
    while d <= min(n, cap):
        if n % d == 0:
            best = d
        d += mult
    return n if best is None else best


def _row_block(t):
    if t % 3 == 0 and (t // 3) % 16 == 0:
        return t // 3
    return t


def _round_up(n, m):
    return (n + m - 1) // m * m


def _sigmoid(v):
    return 1.0 / (1.0 + jnp.exp(-v))


def _softplus(v):
    return jnp.maximum(v, 0.0) + jnp.log(1.0 + jnp.exp(-jnp.abs(v)))


def _gelu_parts(v):
    v2 = v * v
    u = GELU_C0 * (v + GELU_C1 * v * v2)
    t = jnp.tanh(u)
    g = 0.5 * v * (1.0 + t)
    dg = 0.5 * (1.0 + t) + 0.5 * v * (1.0 - t * t) * (GELU_C0 * (1.0 + 3.0 * GELU_C1 * v2))
    return g, dg


def _gelu(v):
    u = GELU_C0 * (v + GELU_C1 * v * v * v)
    return 0.5 * v * (1.0 + jnp.tanh(u))


def _neg_expm1(v):
    series = -v * (1.0 + 0.5 * v * (1.0 + (v / 3.0) * (1.0 + 0.25 * v)))
    return jnp.where(v > -0.05, series, 1.0 - jnp.exp(v))


def _shift_down(v, j):
    if j == 0:
        return v
    rows = lax.broadcasted_iota(jnp.int32, v.shape, 0)
    return jnp.where(rows >= j, pltpu.roll(v, j, 0), 0.0)


def _shift_up(v, j):
    if j == 0:
        return v
    n = v.shape[0]
    rows = lax.broadcasted_iota(jnp.int32, v.shape, 0)
    return jnp.where(rows < n - j, pltpu.roll(v, n - j, 0), 0.0)


def _scan_rows(a_ref, b_ref, out_ref, n_rows, width, reverse):
    n_groups = n_rows // SUBLANES
    rows = lax.broadcasted_iota(jnp.int32, (SUBLANES, width), 0)
    edge = 0 if reverse else SUBLANES - 1

    def body(g, carry):
        grp = (n_groups - 1 - g) if reverse else g
        off = pl.multiple_of(grp * SUBLANES, SUBLANES)
        b = b_ref[pl.ds(off, SUBLANES), :]
        a = None if a_ref is None else a_ref[pl.ds(off, SUBLANES), :]
        for d in (1, 2, 4):
            if reverse:
                keep = rows < SUBLANES - d
                sh = SUBLANES - d
            else:
                keep = rows >= d
                sh = d
            b_s = jnp.where(keep, pltpu.roll(b, sh, 0), 0.0)
            if a is None:
                b = b + b_s
            else:
                a_s = jnp.where(keep, pltpu.roll(a, sh, 0), 1.0)
                b = a * b_s + b
                a = a * a_s
        h = b + carry if a is None else b + a * carry
        out_ref[pl.ds(off, SUBLANES), :] = h
        return jnp.sum(jnp.where(rows == edge, h, 0.0), axis=0, keepdims=True)

    lax.fori_loop(0, n_groups, body, jnp.zeros((1, width), F32), unroll=2)


def mm_in(x, w, out_dtype, name):
    t, k = x.shape
    s_n, _, ns = w.shape
    tn = _tile(ns, 1408)
    nj = ns // tn
    rb = _row_block(t)

    def body(x_ref, w_ref, o_ref):
        o_ref[...] = jnp.dot(x_ref[...], w_ref[...], preferred_element_type=F32).astype(o_ref.dtype)

    return pl.pallas_call(
        body, name=name, grid=(s_n, nj, t // rb),
        in_specs=[pl.BlockSpec((rb, k), lambda s, j, r: (r, 0)),
                  pl.BlockSpec((None, k, tn), lambda s, j, r: (s, 0, j))],
        out_specs=pl.BlockSpec((rb, tn), lambda s, j, r: (r, s * nj + j)),
        out_shape=jax.ShapeDtypeStruct((t, s_n * ns), out_dtype),
        compiler_params=_cp("parallel", "parallel", "parallel"))(x, w)


def _part_map(p, per_part, nj, lead):
    def index(*grid):
        s, j = grid[-2], grid[-1]
        mine = s // per_part == p
        col = jnp.where(mine, (s - p * per_part) * nj + j, 0)
        return (grid[0], col) if lead else (0, col)
    return index


def mm_in_nt(dy_parts, w, name):
    n_parts = len(dy_parts)
    t = dy_parts[0].shape[0]
    s_n, k, ns = w.shape
    per_part = s_n // n_parts
    tn = _tile(ns, 1408)
    nj = ns // tn
    rb = _row_block(t)

    def body(*refs):
        w_ref, o_ref = refs[n_parts:]

        @pl.when((pl.program_id(1) == 0) & (pl.program_id(2) == 0))
        def _():
            o_ref[...] = jnp.zeros_like(o_ref)

        for p in range(n_parts):
            @pl.when(pl.program_id(1) // per_part == p)
            def _():
                o_ref[...] += lax.dot_general(refs[p][...], w_ref[...], (((1,), (1,)), ((), ())),
                                              preferred_element_type=F32)

    return pl.pallas_call(
        body, name=name, grid=(t // rb, s_n, nj),
        in_specs=[pl.BlockSpec((rb, tn), _part_map(p, per_part, nj, True)) for p in range(n_parts)]
        + [pl.BlockSpec((None, k, tn), lambda r, s, j: (s, 0, j))],
        out_specs=pl.BlockSpec((rb, k), lambda r, s, j: (r, 0)),
        out_shape=jax.ShapeDtypeStruct((t, k), F32),
        compiler_params=_cp("parallel", "arbitrary", "arbitrary"))(*dy_parts, w)


def mm_out_nt(dy, w, name):
    t, n = dy.shape
    k = w.shape[0]
    rb = _row_block(t)

    def body(dy_ref, w_ref, o_ref):
        o_ref[...] = lax.dot_general(dy_ref[...], w_ref[...], (((1,), (1,)), ((), ())),
                                     preferred_element_type=F32)

    return pl.pallas_call(
        body, name=name, grid=(t // rb,),
        in_specs=[pl.BlockSpec((rb, n), lambda r: (r, 0)), pl.BlockSpec((k, n), lambda r: (0, 0))],
        out_specs=pl.BlockSpec((rb, k), lambda r: (r, 0)),
        out_shape=jax.ShapeDtypeStruct((t, k), F32),
        compiler_params=_cp("parallel"))(dy, w)


def mm_tn(x, dy_parts, s_n, name):
    n_parts = len(dy_parts)
    t, kb = x.shape
    nb = dy_parts[0].shape[1] * n_parts // s_n
    per_part = max(s_n // n_parts, 1)
    tk = _tile(kb, 1408)
    tn = _tile(nb, 1408)
    nkb, nnb = kb // tk, nb // tn
    tn_dims = (((0,), (0,)), ((), ()))

    def body(*refs):
        x_ref, o_ref = refs[0], refs[-1]
        for p in range(n_parts):
            @pl.when(pl.program_id(1) // per_part == p)
            def _():
                o_ref[...] = lax.dot_general(x_ref[...], refs[1 + p][...], tn_dims,
                                             preferred_element_type=F32).astype(o_ref.dtype)

    return pl.pallas_call(
        body, name=name, grid=(nkb, s_n, nnb),
        in_specs=[pl.BlockSpec((t, tk), lambda a, s, b: (0, a))]
        + [pl.BlockSpec((t, tn), _part_map(p, per_part, nnb, False)) for p in range(n_parts)],
        out_specs=pl.BlockSpec((None, tk, tn), lambda a, s, b: (s, a, b)),
        out_shape=jax.ShapeDtypeStruct((s_n, kb, nb), BF16),
        compiler_params=_cp("parallel", "parallel", "parallel"))(x, *dy_parts)


def mm_bd(x, wr, wi, name):
    t, _ = x.shape
    g_n, gs, _ = wr.shape
    rb = _row_block(t)

    def body(x_ref, wr_ref, wi_ref, r_ref, i_ref):
        xv = x_ref[...]
        r_ref[...] = jnp.dot(xv, wr_ref[...], preferred_element_type=F32)
        i_ref[...] = jnp.dot(xv, wi_ref[...], preferred_element_type=F32)

    blk = pl.BlockSpec((rb, gs), lambda g, r: (r, g))
    wspec = pl.BlockSpec((None, gs, gs), lambda g, r: (g, 0, 0))
    return pl.pallas_call(
        body, name=name, grid=(g_n, t // rb), in_specs=[blk, wspec, wspec], out_specs=[blk, blk],
        out_shape=[jax.ShapeDtypeStruct((t, g_n * gs), F32)] * 2,
        compiler_params=_cp("parallel", "parallel"))(x, wr, wi)


def mm_bd_nt(dr, di, wr, wi, name):
    t, _ = dr.shape
    g_n, gs, _ = wr.shape
    rb = _row_block(t)
    nt = (((1,), (1,)), ((), ()))

    def body(dr_ref, di_ref, wr_ref, wi_ref, o_ref):
        o_ref[...] = (lax.dot_general(dr_ref[...], wr_ref[...], nt, preferred_element_type=F32)
                      + lax.dot_general(di_ref[...], wi_ref[...], nt, preferred_element_type=F32))

    blk = pl.BlockSpec((rb, gs), lambda g, r: (r, g))
    wspec = pl.BlockSpec((None, gs, gs), lambda g, r: (g, 0, 0))
    return pl.pallas_call(
        body, name=name, grid=(g_n, t // rb), in_specs=[blk, blk, wspec, wspec], out_specs=blk,
        out_shape=jax.ShapeDtypeStruct((t, g_n * gs), F32),
        compiler_params=_cp("parallel", "parallel"))(dr, di, wr, wi)


def mm_bd_tn(x, dr, di, gs, name):
    t, w = x.shape
    g_n = w // gs
    tn_dims = (((0,), (0,)), ((), ()))

    def body(x_ref, dr_ref, di_ref, gr_ref, gi_ref):
        xv = x_ref[...]
        gr_ref[...] = lax.dot_general(xv, dr_ref[...], tn_dims, preferred_element_type=F32)
        gi_ref[...] = lax.dot_general(xv, di_ref[...], tn_dims, preferred_element_type=F32)

    blk = pl.BlockSpec((t, gs), lambda g: (0, g))
    ospec = pl.BlockSpec((None, gs, gs), lambda g: (g, 0, 0))
    return pl.pallas_call(
        body, name=name, grid=(g_n,), in_specs=[blk, blk, blk], out_specs=[ospec, ospec],
        out_shape=[jax.ShapeDtypeStruct((g_n, gs, gs), F32)] * 2,
        compiler_params=_cp("parallel"))(x, dr, di)


def embed_fwd(meta, x2d, name):
    nm, d = meta.shape
    seq = x2d.shape[0]
    t = nm + seq
    cb = _tile(d, 256)

    def body(m_ref, x_ref, h_ref, hb_ref):
        h_ref[pl.ds(0, nm), :] = m_ref[...]
        h_ref[pl.ds(nm, seq), :] = x_ref[...]
        hb_ref[pl.ds(0, nm), :] = m_ref[...].astype(BF16)
        hb_ref[pl.ds(nm, seq), :] = x_ref[...].astype(BF16)

    return pl.pallas_call(
        body, name=name, grid=(d // cb,),
        in_specs=[pl.BlockSpec((nm, cb), lambda j: (0, j)), pl.BlockSpec((seq, cb), lambda j: (0, j))],
        out_specs=[pl.BlockSpec((t, cb), lambda j: (0, j))] * 2,
        out_shape=[jax.ShapeDtypeStruct((t, d), F32), jax.ShapeDtypeStruct((t, d), BF16)],
        compiler_params=_cp("parallel"))(meta, x2d)


def embed_bwd(adds, nm, name):
    t, d = adds[0][0].shape
    seq = t - nm
    cb = _tile(d, 256)
    scales = [s for _, s in adds]
    n = len(adds)

    def body(*refs):
        tot = None
        for r, sc in zip(refs[:n], scales):
            term = r[...] if sc == 1.0 else sc * r[...]
            tot = term if tot is None else tot + term
        gm_ref, gx_ref = refs[n], refs[n + 1]
        gm_ref[...] = tot[0:nm]
        gx_ref[...] = tot[nm:t]

    return pl.pallas_call(
        body, name=name, grid=(d // cb,),
        in_specs=[pl.BlockSpec((t, cb), lambda j: (0, j))] * n,
        out_specs=[pl.BlockSpec((nm, cb), lambda j: (0, j)), pl.BlockSpec((seq, cb), lambda j: (0, j))],
        out_shape=[jax.ShapeDtypeStruct((nm, d), F32), jax.ShapeDtypeStruct((seq, d), F32)],
        compiler_params=_cp("parallel"))(*[a for a, _ in adds])


def loss_fwd_bwd(h, tgt, nm, name):
    t, d = h.shape
    seq = t - nm
    cb = _tile(d, 256)
    inv_d = 1.0 / d

    def body(h_ref, t_ref, loss_ref, dy_ref):
        @pl.when(pl.program_id(0) == 0)
        def _():
            loss_ref[...] = jnp.zeros_like(loss_ref)
        err = h_ref[pl.ds(nm, seq), :] - t_ref[...]
        dy_ref[pl.ds(0, nm), :] = jnp.zeros((nm, cb), F32)
        dy_ref[pl.ds(nm, seq), :] = err * inv_d
        loss_ref[...] += (0.5 * inv_d) * jnp.sum(err * err, keepdims=True)

    return pl.pallas_call(
        body, name=name, grid=(d // cb,),
        in_specs=[pl.BlockSpec((t, cb), lambda j: (0, j)), pl.BlockSpec((seq, cb), lambda j: (0, j))],
        out_specs=[pl.BlockSpec((1, 1), lambda j: (0, 0)), pl.BlockSpec((t, cb), lambda j: (0, j))],
        out_shape=[jax.ShapeDtypeStruct((1, 1), F32), jax.ShapeDtypeStruct((t, d), F32)],
        compiler_params=_cp("arbitrary"))(h, tgt)


def mm_out_ln(x, w, h, g, b, alpha, name):
    t, d = h.shape
    k = x.shape[1]
    rb = _row_block(t)

    def body(x_ref, w_ref, h_ref, g_ref, b_ref, y_ref, yb_ref, xh_ref, rs_ref):
        s = alpha * h_ref[...] + jnp.dot(x_ref[...], w_ref[...], preferred_element_type=F32)
        mu = jnp.mean(s, axis=-1, keepdims=True)
        c = s - mu
        var = jnp.mean(c * c, axis=-1, keepdims=True)
        rstd = lax.rsqrt(var + LN_EPS)
        xh = c * rstd
        y = xh * g_ref[...] + b_ref[...]
        y_ref[...] = y
        yb_ref[...] = y.astype(BF16)
        xh_ref[...] = xh
        rs_ref[...] = rstd

    row = pl.BlockSpec((rb, d), lambda r: (r, 0))
    vec = pl.BlockSpec((1, d), lambda r: (0, 0))
    return pl.pallas_call(
        body, name=name, grid=(t // rb,),
        in_specs=[pl.BlockSpec((rb, k), lambda r: (r, 0)), pl.BlockSpec((k, d), lambda r: (0, 0)), row, vec, vec],
        out_specs=[row, row, row, pl.BlockSpec((rb, 1), lambda r: (r, 0))],
        out_shape=[jax.ShapeDtypeStruct((t, d), F32), jax.ShapeDtypeStruct((t, d), BF16),
                   jax.ShapeDtypeStruct((t, d), F32), jax.ShapeDtypeStruct((t, 1), F32)],
        compiler_params=_cp("parallel"))(x, w, h, g, b)


def ln_bwd(adds, xhat, rstd, g, name, w=None):
    t, d = xhat.shape
    rb = _row_block(t)
    scales = [s for _, s in adds]
    n = len(adds)
    n_in = n + 3 + (w is not None)

    def body(*refs):
        xh_ref, rs_ref, g_ref = refs[n:n + 3]
        ds_ref, dsb_ref, dg_ref, db_ref = refs[n_in:n_in + 4]
        dy = None
        for r, sc in zip(refs[:n], scales):
            term = r[...] if sc == 1.0 else sc * r[...]
            dy = term if dy is None else dy + term

        @pl.when(pl.program_id(0) == 0)
        def _():
            dg_ref[...] = jnp.zeros_like(dg_ref)
            db_ref[...] = jnp.zeros_like(db_ref)

        xh = xh_ref[...]
        dxh = dy * g_ref[...]
        m1 = jnp.mean(dxh, axis=-1, keepdims=True)
        m2 = jnp.mean(dxh * xh, axis=-1, keepdims=True)
        ds = rs_ref[...] * (dxh - m1 - xh * m2)
        ds_b = ds.astype(BF16)
        ds_ref[...] = ds
        dsb_ref[...] = ds_b
        dg_ref[...] += jnp.sum(dy * xh, axis=0, keepdims=True)
        db_ref[...] += jnp.sum(dy, axis=0, keepdims=True)
        if w is not None:
            refs[-1][...] = lax.dot_general(ds_b, refs[n + 3][...], (((1,), (1,)), ((), ())),
                                            preferred_element_type=F32)

    row = pl.BlockSpec((rb, d), lambda r: (r, 0))
    vec = pl.BlockSpec((1, d), lambda r: (0, 0))
    in_specs = [row] * n + [row, pl.BlockSpec((rb, 1), lambda r: (r, 0)), vec]
    out_specs = [row, row, vec, vec]
    out_shape = [jax.ShapeDtypeStruct((t, d), F32), jax.ShapeDtypeStruct((t, d), BF16),
                 jax.ShapeDtypeStruct((1, d), F32), jax.ShapeDtypeStruct((1, d), F32)]
    operands = [a for a, _ in adds] + [xhat, rstd, g]
    if w is not None:
        k = w.shape[0]
        in_specs.append(pl.BlockSpec((k, d), lambda r: (0, 0)))
        out_specs.append(pl.BlockSpec((rb, k), lambda r: (r, 0)))
        out_shape.append(jax.ShapeDtypeStruct((t, k), F32))
        operands.append(w)
    return pl.pallas_call(
        body, name=name, grid=(t // rb,), in_specs=in_specs, out_specs=out_specs, out_shape=out_shape,
        compiler_params=_cp("arbitrary"))(*operands)


def _conv_fwd_val(xv, w_ref, b_ref, width):
    y = b_ref[...]
    for j in range(width):
        y = y + _shift_down(xv, j) * w_ref[pl.ds(width - 1 - j, 1), :]
    return y


def _conv_bwd_val(dout, xv, w_ref, width):
    dx = None
    dws = [None] * width
    for j in range(width):
        k = width - 1 - j
        term = _shift_up(dout, j) * w_ref[pl.ds(k, 1), :]
        dx = term if dx is None else dx + term
        dws[k] = jnp.sum(dout * _shift_down(xv, j), axis=0, keepdims=True)
    return dx, dws, jnp.sum(dout, axis=0, keepdims=True)


def a_conv_fwd(gr, cw, cbias, name):
    t, two_dr = gr.shape
    dr = two_dr // 2
    width = cw.shape[0]
    cb = _tile(dr, 256)
    off = dr // cb

    def body(x_ref, w_ref, b_ref, rc_ref, rcb_ref):
        y = _conv_fwd_val(x_ref[...], w_ref, b_ref, width)
        rc_ref[...] = y
        rcb_ref[...] = y.astype(BF16)

    return pl.pallas_call(
        body, name=name, grid=(dr // cb,),
        in_specs=[pl.BlockSpec((t, cb), lambda j: (0, off + j)),
                  pl.BlockSpec((width, cb), lambda j: (0, j)), pl.BlockSpec((1, cb), lambda j: (0, j))],
        out_specs=[pl.BlockSpec((t, cb), lambda j: (0, j))] * 2,
        out_shape=[jax.ShapeDtypeStruct((t, dr), F32), jax.ShapeDtypeStruct((t, dr), BF16)],
        compiler_params=_cp("parallel"))(gr, cw, cbias)


def a_conv_bwd(drc_a, drc_b, gr, cw, dgr, name):
    t, two_dr = gr.shape
    dr = two_dr // 2
    width = cw.shape[0]
    cb = _tile(dr, 256)
    off = dr // cb

    def body(da_ref, db_ref, x_ref, w_ref, dgr_in, dx_ref, dw_ref, dbias_ref):
        del dgr_in
        dout = da_ref[...] + db_ref[...]
        dx, dws, dbias = _conv_bwd_val(dout, x_ref[...], w_ref, width)
        dx_ref[...] = dx.astype(BF16)
        for k in range(width):
            dw_ref[pl.ds(k, 1), :] = dws[k]
        dbias_ref[...] = dbias

    col = pl.BlockSpec((t, cb), lambda j: (0, j))
    return pl.pallas_call(
        body, name=name, grid=(dr // cb,),
        in_specs=[col, col, pl.BlockSpec((t, cb), lambda j: (0, off + j)),
                  pl.BlockSpec((width, cb), lambda j: (0, j)), pl.BlockSpec(memory_space=pl.ANY)],
        out_specs=[pl.BlockSpec((t, cb), lambda j: (0, off + j)), pl.BlockSpec((width, cb), lambda j: (0, j)),
                   pl.BlockSpec((1, cb), lambda j: (0, j))],
        out_shape=[jax.ShapeDtypeStruct((t, two_dr), BF16), jax.ShapeDtypeStruct((width, dr), F32),
                   jax.ShapeDtypeStruct((1, dr), F32)],
        input_output_aliases={4: 0},
        compiler_params=_cp("parallel"))(drc_a, drc_b, gr, cw, dgr)


def _lru_gates(r_pre, i_pre, br, bi, lam):
    r = _sigmoid(r_pre + br)
    i = _sigmoid(i_pre + bi)
    sp = _softplus(-lam)
    la = -LRU_C * r * sp
    a = jnp.exp(la)
    m = jnp.sqrt(_neg_expm1(2.0 * la))
    return r, i, sp, la, a, m


def a_elem_fwd(gr, rc, r_pre, i_pre, br, bi, lam, name):
    t, dr = rc.shape
    cb = _tile(dr, 2 * LANES)
    rb = _row_block(t)
    chunks = [pl.ds(r * rb, rb) for r in range(t // rb)]

    def body(gate_ref, rc_ref, rp_ref, ip_ref, br_ref, bi_ref, lam_ref, hs_ref, g_ref, a_s, u_s):
        for rows in chunks:
            _, i, _, _, a, m = _lru_gates(rp_ref[rows, :], ip_ref[rows, :], br_ref[...], bi_ref[...], lam_ref[...])
            a_s[rows, :] = a
            u_s[rows, :] = m * (i * rc_ref[rows, :])
        _scan_rows(a_s, u_s, hs_ref, t, cb, reverse=False)
        for rows in chunks:
            g_ref[rows, :] = (_gelu(gate_ref[rows, :]) * hs_ref[rows, :]).astype(BF16)

    col = pl.BlockSpec((t, cb), lambda j: (0, j))
    vec = pl.BlockSpec((1, cb), lambda j: (0, j))
    return pl.pallas_call(
        body, name=name, grid=(dr // cb,),
        in_specs=[col, col, col, col, vec, vec, vec],
        out_specs=[col, col],
        out_shape=[jax.ShapeDtypeStruct((t, dr), F32), jax.ShapeDtypeStruct((t, dr), BF16)],
        scratch_shapes=[pltpu.VMEM((t, cb), F32), pltpu.VMEM((t, cb), F32)],
        compiler_params=_cp("parallel"))(gr, rc, r_pre, i_pre, br, bi, lam)


def a_elem_bwd(dg, gr, rc, r_pre, i_pre, hs, br, bi, lam, name):
    t, dr = rc.shape
    cb = _tile(dr, 2 * LANES)
    rb = _row_block(t)
    chunks = [pl.ds(r * rb, rb) for r in range(t // rb)]

    def body(dg_ref, gate_ref, rc_ref, rp_ref, ip_ref, hs_ref, br_ref, bi_ref, lam_ref,
             dgate_ref, dr_ref, di_ref, drc_ref, dlam_ref, dbr_ref, dbi_ref, a_s, b_s, g_s, hp_s):
        lamv = lam_ref[...]
        gates = lambda rows: _lru_gates(rp_ref[rows, :], ip_ref[rows, :], br_ref[...], bi_ref[...], lamv)
        for rows in chunks:
            a_s[rows, :] = gates(rows)[4]
            ge, dge = _gelu_parts(gate_ref[rows, :])
            dgv = dg_ref[rows, :]
            dgate_ref[rows, :] = (dgv * hs_ref[rows, :] * dge).astype(BF16)
            b_s[rows, :] = dgv * ge
        a_s[...] = _shift_up(a_s[...], 1)
        hp_s[...] = _shift_down(hs_ref[...], 1)
        _scan_rows(a_s, b_s, g_s, t, cb, reverse=True)
        dsp = dbr = dbi = jnp.zeros((1, cb), F32)
        for rows in chunks:
            r, i, sp, _, a, m = gates(rows)
            rcv = rc_ref[rows, :]
            gsum = g_s[rows, :]
            da = gsum * hp_s[rows, :]
            dm = gsum * (i * rcv)
            d_i = gsum * m * rcv
            drc_ref[rows, :] = gsum * m * i
            dla = a * da - dm * (a * a) / m
            d_r = (-LRU_C) * sp * dla
            dsp = dsp + jnp.sum((-LRU_C) * r * dla, axis=0, keepdims=True)
            d_rp = d_r * r * (1.0 - r)
            d_ip = d_i * i * (1.0 - i)
            dr_ref[rows, :] = d_rp.astype(BF16)
            di_ref[rows, :] = d_ip.astype(BF16)
            dbr = dbr + jnp.sum(d_rp, axis=0, keepdims=True)
            dbi = dbi + jnp.sum(d_ip, axis=0, keepdims=True)
        dlam_ref[...] = -dsp * _sigmoid(-lamv)
        dbr_ref[...] = dbr
        dbi_ref[...] = dbi

    col = pl.BlockSpec((t, cb), lambda j: (0, j))
    vec = pl.BlockSpec((1, cb), lambda j: (0, j))
    big_b = jax.ShapeDtypeStruct((t, dr), BF16)
    vec_s = jax.ShapeDtypeStruct((1, dr), F32)
    return pl.pallas_call(
        body, name=name, grid=(dr // cb,),
        in_specs=[col, col, col, col, col, col, vec, vec, vec],
        out_specs=[col, col, col, col, vec, vec, vec],
        out_shape=[jax.ShapeDtypeStruct((t, 2 * dr), BF16), big_b, big_b, jax.ShapeDtypeStruct((t, dr), F32),
                   vec_s, vec_s, vec_s],
        scratch_shapes=[pltpu.VMEM((t, cb), F32)] * 4,
        compiler_params=_cp("parallel"))(dg, gr, rc, r_pre, i_pre, hs, br, bi, lam)


def f_elem_fwd(z, cw, cbias, name):
    t, two_f = z.shape
    dff = two_f // 2
    width = cw.shape[0]
    cb = _tile(dff, 256)
    off = dff // cb

    def body(zg_ref, zv_ref, wg_ref, wv_ref, bg_ref, bv_ref, o_ref):
        zcg = _conv_fwd_val(zg_ref[...], wg_ref, bg_ref, width)
        zcv = _conv_fwd_val(zv_ref[...], wv_ref, bv_ref, width)
        o_ref[...] = (_gelu(zcg) * zcv).astype(BF16)

    lo = lambda j: (0, j)
    hi = lambda j: (0, off + j)
    return pl.pallas_call(
        body, name=name, grid=(dff // cb,),
        in_specs=[pl.BlockSpec((t, cb), lo), pl.BlockSpec((t, cb), hi),
                  pl.BlockSpec((width, cb), lo), pl.BlockSpec((width, cb), hi),
                  pl.BlockSpec((1, cb), lo), pl.BlockSpec((1, cb), hi)],
        out_specs=pl.BlockSpec((t, cb), lo),
        out_shape=jax.ShapeDtypeStruct((t, dff), BF16),
        compiler_params=_cp("parallel"))(z, z, cw, cw, cbias, cbias)


def f_elem_bwd(z, dff_g, cw, cbias, name):
    t, two_f = z.shape
    dff = two_f // 2
    width = cw.shape[0]
    cb = _tile(dff, 256)
    off = dff // cb

    def body(zg_ref, zv_ref, d_ref, wg_ref, wv_ref, bg_ref, bv_ref,
             dzg_ref, dzv_ref, dwg_ref, dwv_ref, dbg_ref, dbv_ref):
        zg = zg_ref[...]
        zv = zv_ref[...]
        zcg = _conv_fwd_val(zg, wg_ref, bg_ref, width)
        zcv = _conv_fwd_val(zv, wv_ref, bv_ref, width)
        ge, dge = _gelu_parts(zcg)
        dv = d_ref[...]
        dx, dws, dbias = _conv_bwd_val(dv * zcv * dge, zg, wg_ref, width)
        dzg_ref[...] = dx.astype(BF16)
        for k in range(width):
            dwg_ref[pl.ds(k, 1), :] = dws[k]
        dbg_ref[...] = dbias
        dx, dws, dbias = _conv_bwd_val(dv * ge, zv, wv_ref, width)
        dzv_ref[...] = dx.astype(BF16)
        for k in range(width):
            dwv_ref[pl.ds(k, 1), :] = dws[k]
        dbv_ref[...] = dbias

    lo = lambda j: (0, j)
    hi = lambda j: (0, off + j)
    col = pl.BlockSpec((t, cb), lo)
    wsp = pl.BlockSpec((width, cb), lo)
    vsp = pl.BlockSpec((1, cb), lo)
    return pl.pallas_call(
        body, name=name, grid=(dff // cb,),
        in_specs=[col, pl.BlockSpec((t, cb), hi), col, wsp, pl.BlockSpec((width, cb), hi),
                  vsp, pl.BlockSpec((1, cb), hi)],
        out_specs=[col, col, wsp, wsp, vsp, vsp],
        out_shape=[jax.ShapeDtypeStruct((t, dff), BF16)] * 2
        + [jax.ShapeDtypeStruct((width, dff), F32)] * 2 + [jax.ShapeDtypeStruct((1, dff), F32)] * 2,
        compiler_params=_cp("parallel"))(z, z, dff_g, cw, cw, cbias, cbias)


def kv_fwd(z, fb, d_model, name):
    t, _ = z.shape
    blk = 2 * d_model // LANES

    def body(z_ref, fb_ref, c_ref, lf_s):
        v = z_ref[...] + fb_ref[...]
        lf_s[...] = -_softplus(-v)
        _scan_rows(None, lf_s, c_ref, t, LANES, reverse=False)

    return pl.pallas_call(
        body, name=name, grid=(1,),
        in_specs=[pl.BlockSpec((t, LANES), lambda j: (0, blk)), pl.BlockSpec((1, LANES), lambda j: (0, 0))],
        out_specs=pl.BlockSpec((t, LANES), lambda j: (0, 0)),
        out_shape=jax.ShapeDtypeStruct((t, LANES), F32),
        scratch_shapes=[pltpu.VMEM((t, LANES), F32)],
        compiler_params=_cp("arbitrary"))(z, fb)


def kv_bwd(dcs, z, fb, d_model, name):
    t, _ = z.shape
    blk = 2 * d_model // LANES
    n = len(dcs)

    def body(*refs):
        z_ref, fb_ref, dz_ref, dfb_ref, dc_s, dl_s = refs[n:]
        tot = refs[0][...]
        for r in refs[1:n]:
            tot = tot + r[...]
        dc_s[...] = tot
        _scan_rows(None, dc_s, dl_s, t, LANES, reverse=True)
        v = z_ref[...] + fb_ref[...]
        dz = dl_s[...] * _sigmoid(-v)
        dz_ref[...] = dz.astype(BF16)
        dfb_ref[...] = jnp.sum(dz, axis=0, keepdims=True)

    full = pl.BlockSpec((t, LANES), lambda j: (0, 0))
    return pl.pallas_call(
        body, name=name, grid=(1,),
        in_specs=[full] * n + [pl.BlockSpec((t, LANES), lambda j: (0, blk)),
                               pl.BlockSpec((1, LANES), lambda j: (0, 0))],
        out_specs=[full, pl.BlockSpec((1, LANES), lambda j: (0, 0))],
        out_shape=[jax.ShapeDtypeStruct((t, LANES), BF16), jax.ShapeDtypeStruct((1, LANES), F32)],
        scratch_shapes=[pltpu.VMEM((t, LANES), F32)] * 2,
        compiler_params=_cp("arbitrary"))(*dcs, z, fb)


def add_cast(a, b, name):
    t, d = a.shape
    cb = _tile(d, 512)

    def body(a_ref, b_ref, o_ref):
        o_ref[...] = (a_ref[...] + b_ref[...]).astype(BF16)

    col = pl.BlockSpec((t, cb), lambda j: (0, j))
    return pl.pallas_call(body, name=name, grid=(d // cb,), in_specs=[col, col], out_specs=col,
                          out_shape=jax.ShapeDtypeStruct((t, d), BF16),
                          compiler_params=_cp("parallel"))(a, b)


def _attn_geometry(t):
    nqb = 6 if t > 1024 else 2
    tp = _round_up(t, LANES * nqb)
    return nqb, tp, tp // nqb


def _attn_scales(dh):
    scale = dh ** -0.5
    if math.log2(scale).is_integer():
        return scale, 1.0
    return 1.0, scale


def _attn_pieces(qs, ks, crow, j, i, tq, dh, s_mul):
    r0 = i * tq
    lanes = pl.ds(j * dh, dh)
    qi = qs[pl.ds(r0, tq), lanes]
    spans = ([(0, r0)] if i > 0 else []) + [(r0, tq)]
    logits = []
    for k0, n in spans:
        s = lax.dot_general(qi, ks[pl.ds(k0, n), lanes], (((1,), (1,)), ((), ())),
                            preferred_element_type=F32)
        if s_mul != 1.0:
            s = s * s_mul
        s = s - crow[:, k0:k0 + n]
        if k0 == r0:
            rows = lax.broadcasted_iota(jnp.int32, (tq, tq), 0)
            cols = lax.broadcasted_iota(jnp.int32, (tq, tq), 1)
            s = jnp.where(cols <= rows, s, NEG_BIG)
        logits.append(s)
    mx = jnp.max(logits[0], axis=1, keepdims=True)
    for s in logits[1:]:
        mx = jnp.maximum(mx, jnp.max(s, axis=1, keepdims=True))
    es = [jnp.exp(s - mx) for s in logits]
    tot = jnp.sum(es[0], axis=1, keepdims=True)
    for e in es[1:]:
        tot = tot + jnp.sum(e, axis=1, keepdims=True)
    inv = 1.0 / tot
    return [(k0, n, e * inv) for (k0, n), e in zip(spans, es)], qi


def attn_fwd(qg, z, ct_pad, d_model, n_heads, name):
    t = qg.shape[0]
    dh = d_model // n_heads
    hp = LANES // dh
    nqb, tp, tq = _attn_geometry(t)
    nblk = d_model // LANES
    q_mul, s_mul = _attn_scales(dh)

    def body(q_ref, og_ref, k_ref, v_ref, ct_ref, o_ref, mo_ref, qs, ks, vs, os_):
        pad = jnp.zeros((tp - t, LANES), BF16)
        qs[pl.ds(0, t), :] = (q_ref[...] * q_mul).astype(BF16)
        qs[pl.ds(t, tp - t), :] = pad
        for src, dst in ((k_ref, ks), (v_ref, vs)):
            dst[pl.ds(0, t), :] = src[...].astype(BF16)
            dst[pl.ds(t, tp - t), :] = pad
        for j in range(hp):
            crow = ct_ref[j]
            lanes = pl.ds(j * dh, dh)
            for i in range(nqb):
                pieces, _ = _attn_pieces(qs, ks, crow, j, i, tq, dh, s_mul)
                acc = None
                for k0, n, p in pieces:
                    part = jnp.dot(p.astype(BF16), vs[pl.ds(k0, n), lanes], preferred_element_type=F32)
                    acc = part if acc is None else acc + part
                os_[pl.ds(i * tq, tq), lanes] = acc
        o = os_[pl.ds(0, t), :]
        o_ref[...] = o
        mo_ref[...] = (o * _sigmoid(og_ref[...])).astype(BF16)

    col = lambda off: pl.BlockSpec((t, LANES), lambda p: (0, off + p))
    return pl.pallas_call(
        body, name=name, grid=(nblk,),
        in_specs=[col(0), col(nblk), col(0), col(nblk), pl.BlockSpec((hp, 1, tp), lambda p: (p, 0, 0))],
        out_specs=[col(0), col(0)],
        out_shape=[jax.ShapeDtypeStruct((t, d_model), F32), jax.ShapeDtypeStruct((t, d_model), BF16)],
        scratch_shapes=[pltpu.VMEM((tp, LANES), BF16)] * 3 + [pltpu.VMEM((tp, LANES), F32)],
        compiler_params=_cp("parallel"))(qg, qg, z, z, ct_pad)


def attn_bwd(dmo, qg, z, o, ct_pad, d_model, n_heads, name):
    t = qg.shape[0]
    dh = d_model // n_heads
    hp = LANES // dh
    nqb, tp, tq = _attn_geometry(t)
    nblk = d_model // LANES
    q_mul, s_mul = _attn_scales(dh)
    scale = dh ** -0.5
    tn_dims = (((0,), (0,)), ((), ()))
    nt_dims = (((1,), (1,)), ((), ()))

    def body(dmo_ref, q_ref, og_ref, k_ref, v_ref, o_ref, ct_ref,
             dq_ref, dog_ref, dk_ref, dv_ref, dct_ref, qs, ks, vs, dos, dqs, dks, dvs):
        pad = jnp.zeros((tp - t, LANES), BF16)
        sg = _sigmoid(og_ref[...])
        dmo_v = dmo_ref[...]
        dog_ref[...] = (dmo_v * o_ref[...] * sg * (1.0 - sg)).astype(BF16)
        dos[pl.ds(0, t), :] = (dmo_v * sg).astype(BF16)
        dos[pl.ds(t, tp - t), :] = pad
        qs[pl.ds(0, t), :] = (q_ref[...] * q_mul).astype(BF16)
        qs[pl.ds(t, tp - t), :] = pad
        for src, dst in ((k_ref, ks), (v_ref, vs)):
            dst[pl.ds(0, t), :] = src[...].astype(BF16)
            dst[pl.ds(t, tp - t), :] = pad
        dks[...] = jnp.zeros_like(dks)
        dvs[...] = jnp.zeros_like(dvs)
        dct_ref[...] = jnp.zeros_like(dct_ref)
        for j in range(hp):
            crow = ct_ref[j]
            lanes = pl.ds(j * dh, dh)
            for i in range(nqb):
                pieces, qi = _attn_pieces(qs, ks, crow, j, i, tq, dh, s_mul)
                do_i = dos[pl.ds(i * tq, tq), lanes]
                dps = [lax.dot_general(do_i, vs[pl.ds(k0, n), lanes], nt_dims, preferred_element_type=F32)
                       for k0, n, _ in pieces]
                row = None
                for (_, _, p), dp in zip(pieces, dps):
                    part = jnp.sum(p * dp, axis=1, keepdims=True)
                    row = part if row is None else row + part
                dq_i = None
                for (k0, n, p), dp in zip(pieces, dps):
                    ds = p * (dp - row)
                    ds_b = ds.astype(BF16)
                    keys = pl.ds(k0, n)
                    part = jnp.dot(ds_b, ks[keys, lanes], preferred_element_type=F32)
                    dq_i = part if dq_i is None else dq_i + part
                    dks[keys, lanes] += lax.dot_general(ds_b, qi, tn_dims, preferred_element_type=F32) * s_mul
                    dvs[keys, lanes] += lax.dot_general(p.astype(BF16), do_i, tn_dims,
                                                        preferred_element_type=F32)
                    dct_ref[j, :, keys] -= jnp.sum(ds, axis=0, keepdims=True)
                dqs[pl.ds(i * tq, tq), lanes] = dq_i * scale
        dq_ref[...] = dqs[pl.ds(0, t), :].astype(BF16)
        dk_ref[...] = dks[pl.ds(0, t), :]
        dv_ref[...] = dvs[pl.ds(0, t), :]

    col = lambda off: pl.BlockSpec((t, LANES), lambda p: (0, off + p))
    big = lambda dt: jax.ShapeDtypeStruct((t, d_model), dt)
    return pl.pallas_call(
        body, name=name, grid=(nblk,),
        in_specs=[col(0), col(0), col(nblk), col(0), col(nblk), col(0),
                  pl.BlockSpec((hp, 1, tp), lambda p: (p, 0, 0))],
        out_specs=[col(0), col(0), col(0), col(0), pl.BlockSpec((hp, 1, tp), lambda p: (p, 0, 0))],
        out_shape=[big(BF16), big(BF16), big(F32), big(F32),
                   jax.ShapeDtypeStruct((n_heads, 1, tp), F32)],
        scratch_shapes=[pltpu.VMEM((tp, LANES), BF16)] * 4 + [pltpu.VMEM((tp, LANES), F32)] * 3,
        compiler_params=_cp("parallel"))(dmo, qg, qg, z, z, o, ct_pad)


def cast_into_slot(shard, w, index, name):
    r, c = w.shape[-2:]
    rh = r // 2
    tr = _tile(rh, 512, 16)
    n = rh // tr
    if w.ndim == 3:
        w_spec = pl.BlockSpec((None, tr, c), lambda h, i, sh: (index, h * n + i, 0))
    else:
        w_spec = pl.BlockSpec((tr, c), lambda h, i, sh: (h * n + i, 0))

    def body(sh_ref, w_ref, o_ref):
        del sh_ref
        o_ref[...] = w_ref[...].astype(BF16)

    return pl.pallas_call(
        body, name=name,
        grid_spec=pltpu.PrefetchScalarGridSpec(
            num_scalar_prefetch=1, grid=(2, n), in_specs=[w_spec],
            out_specs=pl.BlockSpec((None, None, tr, c), lambda h, i, sh: (sh[0], h, i, 0))),
        out_shape=jax.ShapeDtypeStruct((N_SHARDS, 2, rh, c), BF16),
        compiler_params=_cp("parallel", "parallel"))(shard, w)


def owner_sum(shard_core, g, recv, buf, layer, name):
    _, _, rh, c = g.shape
    n_recv = recv.shape[0]
    tr = _tile(rh, 512, 16)
    slot = (lambda sc: sc[0]) if layer is None else (lambda sc: layer)

    def body(sc_ref, a_ref, r_ref, buf_ref, out_ref):
        del sc_ref, buf_ref
        acc = a_ref[...].astype(F32)
        for k in range(n_recv):
            acc = acc + r_ref[k].astype(F32)
        out_ref[...] = acc

    return pl.pallas_call(
        body, name=name,
        grid_spec=pltpu.PrefetchScalarGridSpec(
            num_scalar_prefetch=1, grid=(rh // tr,),
            in_specs=[pl.BlockSpec((None, None, tr, c), lambda i, sc: (sc[0], sc[1], i, 0)),
                      pl.BlockSpec((n_recv, tr, c), lambda i, sc: (0, i, 0)),
                      pl.BlockSpec(memory_space=pl.ANY)],
            out_specs=pl.BlockSpec((None, None, tr, c), lambda i, sc: (slot(sc), sc[1], i, 0))),
        out_shape=jax.ShapeDtypeStruct(buf.shape, F32),
        input_output_aliases={3: 0},
        compiler_params=_cp("parallel"))(shard_core, g, recv, buf)


def adamw(w, g, m, v, name, emit_g=False):
    r, c = w.shape
    tr = _tile(r, 512, SUBLANES)
    c1 = 1.0 - ADAM_B1 ** ADAM_STEP
    c2 = 1.0 - ADAM_B2 ** ADAM_STEP
    n_out = 4 if emit_g else 3

    def body(w_ref, g_ref, m_ref, v_ref, d_ref, mo_ref, vo_ref, *go_ref):
        gv = g_ref[...]
        if emit_g:
            go_ref[0][...] = gv
        mn = ADAM_B1 * m_ref[...] + (1.0 - ADAM_B1) * gv
        vn = ADAM_B2 * v_ref[...] + (1.0 - ADAM_B2) * (gv * gv)
        m_hat = mn / c1
        v_hat = vn / c2
        d_ref[...] = -ADAM_LR * (m_hat / (jnp.sqrt(v_hat) + ADAM_EPS) + ADAM_WD * w_ref[...])
        mo_ref[...] = mn
        vo_ref[...] = vn

    blk = pl.BlockSpec((tr, c), lambda i: (i, 0))
    return pl.pallas_call(
        body, name=name, grid=(r // tr,), in_specs=[blk] * 4, out_specs=[blk] * n_out,
        out_shape=[jax.ShapeDtypeStruct((r, c), F32)] * n_out,
        compiler_params=_cp("parallel"))(w, g, m, v)


def _coords():
    return lax.axis_index("x"), lax.axis_index("y"), lax.axis_index("c")


def _exchange(name, ins, out_shapes, plan, in_place=False):
    n_in = len(ins)
    n_out = len(out_shapes)
    n_rem = len(plan([None] * n_in, [None] * n_out, True))

    def body(*refs):
        in_refs = refs[:n_in]
        out_refs = refs[n_in:n_in + n_out]
        send_sems, recv_sems = refs[n_in + n_out:]
        copies = [pltpu.make_async_remote_copy(
            src_ref=src, dst_ref=dst, send_sem=send_sems.at[q], recv_sem=recv_sems.at[q],
            device_id=peer, device_id_type=pl.DeviceIdType.MESH)
            for q, (src, dst, peer) in enumerate(plan(list(in_refs), list(out_refs), False))]
        for cp in copies:
            cp.start()
        for cp in copies:
            cp.wait_recv()
        for cp in copies:
            cp.wait_send()

    hbm = pl.BlockSpec(memory_space=pl.ANY)
    return pl.pallas_call(
        body, name=name, in_specs=[hbm] * n_in, out_specs=[hbm] * n_out, out_shape=out_shapes,
        input_output_aliases={i: i for i in range(n_in)} if in_place else {},
        scratch_shapes=[pltpu.SemaphoreType.DMA((n_rem,)), pltpu.SemaphoreType.DMA((n_rem,))],
        compiler_params=pltpu.CompilerParams(has_side_effects=True))(*ins)


def _split_start(name, groups, plan):
    flat = [a for grp in groups for a in grp]
    n, n_grp = len(flat), len(groups)
    counts = [len(plan(g, [None] * len(grp), True)) for g, grp in enumerate(groups)]

    def body(*refs):
        ins, sems, token = refs[:n], refs[n:n + 2 * n_grp], refs[-1]
        pos = 0
        for g, grp in enumerate(groups):
            arrs = list(ins[pos:pos + len(grp)])
            pos += len(grp)
            for q, (src, dst, peer) in enumerate(plan(g, arrs, False)):
                pltpu.make_async_remote_copy(
                    src_ref=src, dst_ref=dst, send_sem=sems[2 * g].at[q], recv_sem=sems[2 * g + 1].at[q],
                    device_id=peer, device_id_type=pl.DeviceIdType.MESH).start()
        token[...] = jnp.zeros_like(token)

    hbm = pl.BlockSpec(memory_space=pltpu.HBM)
    sem = pl.BlockSpec(memory_space=pltpu.SEMAPHORE)
    outs = pl.pallas_call(
        body, name=name,
        out_shape=[pltpu.SemaphoreType.DMA((cnt,)) for cnt in counts for _ in range(2)]
        + [pltpu.HBM(a.shape, a.dtype) for a in flat] + [jax.ShapeDtypeStruct((SUBLANES, LANES), F32)],
        in_specs=[hbm] * n, out_specs=[sem] * (2 * n_grp) + [hbm] * n + [pl.BlockSpec(memory_space=pltpu.VMEM)],
        input_output_aliases={i: 2 * n_grp + i for i in range(n)},
        compiler_params=pltpu.CompilerParams(has_side_effects=pltpu.SideEffectType.DATAFLOW_SIDE_EFFECTING),
    )(*[pltpu.with_memory_space_constraint(a, pltpu.HBM) for a in flat])
    started, pos = [], 2 * n_grp
    for g, grp in enumerate(groups):
        started.append((outs[2 * g], outs[2 * g + 1], list(outs[pos:pos + len(grp)])))
        pos += len(grp)
    return started, outs[-1]


def _split_wait(name, started, after, plan_g):
    send_sems, recv_sems, arrs = started
    n = len(arrs)

    def body(*refs):
        ins, ssem, rsem = list(refs[:n]), refs[n], refs[n + 1]
        for q, (src, dst, peer) in enumerate(plan_g(ins, False)):
            cp = pltpu.make_async_remote_copy(
                src_ref=src, dst_ref=dst, send_sem=ssem.at[q], recv_sem=rsem.at[q],
                device_id=peer, device_id_type=pl.DeviceIdType.MESH)
            cp.wait_send()
            cp.wait_recv()

    hbm = pl.BlockSpec(memory_space=pltpu.HBM)
    sem = pl.BlockSpec(memory_space=pltpu.SEMAPHORE)
    return pl.pallas_call(
        body, name=name, out_shape=[pltpu.HBM(a.shape, a.dtype) for a in arrs],
        in_specs=[hbm] * n + [sem, sem, pl.BlockSpec(memory_space=pl.ANY)], out_specs=[hbm] * n,
        input_output_aliases={i: i for i in range(n)},
        compiler_params=pltpu.CompilerParams(has_side_effects=pltpu.SideEffectType.DATAFLOW_SIDE_EFFECTING),
    )(*arrs, send_sems, recv_sems, after)


def _gather_ici_plan(arrs, count_only):
    if count_only:
        return [None] * (3 * len(arrs))
    x, y, c = _coords()
    pushes = []
    for a in arrs:
        mine = a.at[2 * x + y, c]
        pushes += [(mine, mine, peer) for peer, _ in _other_chips(x, y, c)]
    return pushes


def _all_to_all_plan(arrs, count_only):
    half = len(arrs) // 2
    if count_only:
        return [None] * (7 * half)
    x, y, c = _coords()
    pushes = []
    for src, land in zip(arrs[:half], arrs[half:]):
        for flips in range(1, 8):
            px, py, pc = (1 - x if flips & 4 else x), (1 - y if flips & 2 else y), (1 - c if flips & 1 else c)
            pushes.append((src.at[2 * px + py, pc], land.at[flips - 1], (px, py, pc)))
    return pushes


def _forward_plan(arrs, count_only):
    if count_only:
        return [None] * (3 * len(arrs))
    x, y, c = _coords()
    pushes = []
    for a in arrs:
        for _, src_shard in _other_chips(x, y, c):
            slab = a.at[src_shard, c]
            pushes.append((slab, slab, (x, y, 1 - c)))
    return pushes


def forward_to_sibling(bufs, name):
    shapes = [jax.ShapeDtypeStruct(b.shape, b.dtype) for b in bufs]
    return _exchange(name, bufs, shapes, lambda ins, outs, cnt: _forward_plan(outs, cnt), in_place=True)


def _other_chips(x, y, c):
    return [((1 - x, y, c), 2 * (1 - x) + y), ((x, 1 - y, c), 2 * x + 1 - y),
            ((1 - x, 1 - y, c), 2 * (1 - x) + 1 - y)]


def join_halves(bufs, everywhere, name):
    slots = [(i, l) for i, b in enumerate(bufs) for l in range(b.shape[0])]
    n = len(bufs)

    def plan(ins, outs, count_only):
        if count_only:
            return [None] * (len(slots) + 7)
        x, y, c = _coords()
        pushes = [(outs[i].at[l, c], outs[i].at[l, c], (x, y, 1 - c)) for i, l in slots]
        mine = outs[n].at[2 * x + y, c]
        for flips in range(1, 8):
            peer = (1 - x if flips & 4 else x, 1 - y if flips & 2 else y, 1 - c if flips & 1 else c)
            pushes.append((mine, mine, peer))
        return pushes

    arrs = list(bufs) + [everywhere]
    shapes = [jax.ShapeDtypeStruct(b.shape, b.dtype) for b in arrs]
    return _exchange(name, arrs, shapes, plan, in_place=True)


def _pack(arrays, multiple):
    flat = jnp.concatenate([a.reshape(-1) for a in arrays])
    n = flat.shape[0]
    return jnp.pad(flat, (0, _round_up(n, multiple) - n))


def _unpack(flat, shapes):
    out, pos = [], 0
    for shp in shapes:
        n = math.prod(shp)
        out.append(flat[pos:pos + n].reshape(shp))
        pos += n
    return out


def _block_diag(w, per_group):
    nb, bs, _ = w.shape
    g = nb // per_group
    w4 = w.reshape(g, per_group, bs, bs)
    eye = jnp.eye(per_group, dtype=w.dtype)
    full = w4[:, :, :, None, :] * eye[None, :, None, :, None]
    return full.reshape(g, per_group * bs, per_group * bs).astype(BF16)


def _block_diag_extract(full, per_group, bs):
    g = full.shape[0]
    f5 = full.reshape(g, per_group, bs, per_group, bs)
    idx = jnp.arange(per_group)
    picked = f5[:, idx, :, idx, :]
    return jnp.moveaxis(picked, 0, 1).reshape(g * per_group, bs, bs)


def kernel(x, meta, a_w_in, a_conv_w, a_conv_b, a_w_r, a_b_r, a_w_i, a_b_i, a_lambda, a_w_out, kv_w, kv_f_b, b_w_in, b_w_out, f_w_in, f_conv_w, f_conv_b, f_w_out, ln1_g, ln1_b, ln2_g, ln2_b, loss_target, m_meta, m_a_w_in, m_a_conv_w, m_a_conv_b, m_a_w_r, m_a_b_r, m_a_w_i, m_a_b_i, m_a_lambda, m_a_w_out, m_kv_w, m_kv_f_b, m_b_w_in, m_b_w_out, m_f_w_in, m_f_conv_w, m_f_conv_b, m_f_w_out, m_ln1_g, m_ln1_b, m_ln2_g, m_ln2_b, v_meta, v_a_w_in, v_a_conv_w, v_a_conv_b, v_a_w_r, v_a_b_r, v_a_w_i, v_a_b_i, v_a_lambda, v_a_w_out, v_kv_w, v_kv_f_b, v_b_w_in, v_b_w_out, v_f_w_in, v_f_conv_w, v_f_conv_b, v_f_w_out, v_ln1_g, v_ln1_b, v_ln2_g, v_ln2_b):
    weights = dict(meta=meta, a_w_in=a_w_in, a_conv_w=a_conv_w, a_conv_b=a_conv_b, a_w_r=a_w_r, a_b_r=a_b_r,
                   a_w_i=a_w_i, a_b_i=a_b_i, a_lambda=a_lambda, a_w_out=a_w_out, kv_w=kv_w, kv_f_b=kv_f_b,
                   b_w_in=b_w_in, b_w_out=b_w_out, f_w_in=f_w_in, f_conv_w=f_conv_w, f_conv_b=f_conv_b,
                   f_w_out=f_w_out, ln1_g=ln1_g, ln1_b=ln1_b, ln2_g=ln2_g, ln2_b=ln2_b)
    mom_m = dict(meta=m_meta, a_w_in=m_a_w_in, a_conv_w=m_a_conv_w, a_conv_b=m_a_conv_b, a_w_r=m_a_w_r,
                 a_b_r=m_a_b_r, a_w_i=m_a_w_i, a_b_i=m_a_b_i, a_lambda=m_a_lambda, a_w_out=m_a_w_out,
                 kv_w=m_kv_w, kv_f_b=m_kv_f_b, b_w_in=m_b_w_in, b_w_out=m_b_w_out, f_w_in=m_f_w_in,
                 f_conv_w=m_f_conv_w, f_conv_b=m_f_conv_b, f_w_out=m_f_w_out, ln1_g=m_ln1_g, ln1_b=m_ln1_b,
                 ln2_g=m_ln2_g, ln2_b=m_ln2_b)
    mom_v = dict(meta=v_meta, a_w_in=v_a_w_in, a_conv_w=v_a_conv_w, a_conv_b=v_a_conv_b, a_w_r=v_a_w_r,
                 a_b_r=v_a_b_r, a_w_i=v_a_w_i, a_b_i=v_a_b_i, a_lambda=v_a_lambda, a_w_out=v_a_w_out,
                 kv_w=v_kv_w, kv_f_b=v_kv_f_b, b_w_in=v_b_w_in, b_w_out=v_b_w_out, f_w_in=v_f_w_in,
                 f_conv_w=v_f_conv_w, f_conv_b=v_f_conv_b, f_w_out=v_f_w_out, ln1_g=v_ln1_g, ln1_b=v_ln1_b,
                 ln2_g=v_ln2_g, ln2_b=v_ln2_b)
    return _train_step(x, loss_target, weights, mom_m, mom_v)


WEIGHT_ORDER = ("meta", "a_w_in", "a_conv_w", "a_conv_b", "a_w_r", "a_b_r", "a_w_i", "a_b_i", "a_lambda",
                "a_w_out", "kv_w", "kv_f_b", "b_w_in", "b_w_out", "f_w_in", "f_conv_w", "f_conv_b",
                "f_w_out", "ln1_g", "ln1_b", "ln2_g", "ln2_b")
BIG = ("a_w_in", "a_w_out", "kv_w", "b_w_in", "b_w_out", "f_w_in", "f_w_out")
OUT_TYPE = ("a_w_out", "b_w_out", "f_w_out")
SMALL_SHARDED = (("meta", 1), ("a_conv_w", 2), ("a_conv_b", 1), ("a_b_r", 1), ("a_b_i", 1), ("a_lambda", 1),
                 ("f_conv_w", 2))
SMALL_REPLICATED = ("a_w_r", "a_w_i", "kv_f_b", "f_conv_b", "ln1_g", "ln1_b", "ln2_g", "ln2_b")


def _train_step(x, loss_target, weights, mom_m, mom_v):
    S = N_SHARDS
    seq, d = x.shape[1], x.shape[2]
    nm = weights["meta"].shape[0]
    la = weights["a_w_in"].shape[0]
    lb = weights["b_w_in"].shape[0]
    depth = la + lb
    dr = weights["a_w_out"].shape[1] * S
    nb, bs = weights["a_w_r"].shape[1], weights["a_w_r"].shape[2]
    per_group = (LANES // math.gcd(bs, LANES))
    gs = per_group * bs
    heads = weights["kv_f_b"].shape[0]
    dff = weights["f_w_out"].shape[1] * S
    nkv = 2 * d + heads
    nkv_s = weights["kv_w"].shape[1]
    nkvp = _round_up(2 * d + LANES, 768) if 2 * d + LANES > 768 else 2 * d + LANES
    alpha = (2 * depth) ** 0.25
    xi, yi, ci = _coords()
    shard = 2 * xi + yi
    shard_arr = jnp.reshape(shard, (1,)).astype(jnp.int32)
    shard_core_arr = jnp.stack([shard, ci]).astype(jnp.int32)

    def mixer_keys(l):
        if l < la:
            return [("a_w_in", l), ("a_w_out", l)]
        return ([("kv_w", 0)] if l == la else []) + [("b_w_in", l - la), ("b_w_out", l - la)]

    def ffn_keys(l):
        return [("f_w_in", l), ("f_w_out", l)]

    assert la >= 1
    groups = [mixer_keys(0)[:1], mixer_keys(0)[1:] + ffn_keys(0)]
    groups += [mixer_keys(l) + ffn_keys(l) for l in range(1, depth)]
    keys = [kl for grp in groups for kl in grp]
    shape2d = {(k, i): weights[k].shape[-2:] for k, i in keys}
    small_local = [weights[k] for k, _ in SMALL_SHARDED]
    sm_flat = _pack(small_local, 2 * SUBLANES * LANES).reshape(1, 2, -1, LANES)
    sm_slot = lax.dynamic_update_slice_in_dim(lax.empty((S,) + sm_flat.shape[1:], F32), sm_flat, shard, axis=0)
    parts = [[cast_into_slot(shard_arr, weights[k], i, f"cast_{k}{i}") for k, i in grp] for grp in groups]
    parts[0].append(sm_slot)
    in_flight, start_token = _split_start("gather_start", parts, lambda g, refs, cnt: _gather_ici_plan(refs, cnt))
    gw = {}

    def use(g, arrs):
        for kl, a in zip(groups[g], arrs):
            rows, cols = shape2d[kl]
            gw[kl] = a.reshape(S * rows, cols) if kl[0] in OUT_TYPE else a.reshape(S, rows, cols)
        return arrs

    def fetch(g, after):
        arrs = _split_wait(f"gather_wait_{g}", in_flight[g], after, _gather_ici_plan)
        return use(g, forward_to_sibling(arrs, f"gather_fwd_{g}"))

    forwarding = {}

    def fetch_early(g, after):
        arrs = _split_wait(f"gather_wait_{g}", in_flight[g], after, _gather_ici_plan)
        started, token = _split_start(f"gather_fwd_start_{g}", [arrs], lambda _, refs, cnt: _forward_plan(refs, cnt))
        forwarding[g] = started[0]
        return token[0:1, 0:1]

    def fetch_end(g, after):
        return use(g, _split_wait(f"gather_fwd_wait_{g}", forwarding[g], after, _forward_plan))

    sm_all = fetch(0, start_token)[-1].reshape(S, -1)
    small_full = {}
    per_shard = [_unpack(sm_all[s], [a.shape for a in small_local]) for s in range(S)]
    for idx, (k, axis) in enumerate(SMALL_SHARDED):
        small_full[k] = jnp.concatenate([per_shard[s][idx] for s in range(S)], axis=axis)
    fb_pad = jnp.pad(weights["kv_f_b"], (0, LANES - heads)).reshape(1, LANES)
    wr_g = [_block_diag(weights["a_w_r"][l], per_group) for l in range(la)]
    wi_g = [_block_diag(weights["a_w_i"][l], per_group) for l in range(la)]
    row = lambda v: v.reshape(1, -1)

    h, hb = embed_fwd(small_full["meta"], x[0], "embed")
    saved = []
    kvz = ct_pad = None
    _, tp, _ = _attn_geometry(nm + seq)
    t = nm + seq
    for l in range(depth):
        sv = {"hb_in": hb}
        if l > 0:
            (fetch_end if l + 1 in forwarding else fetch)(l + 1, hb)
        if l < la:
            gr = mm_in(hb, gw[("a_w_in", l)], F32, f"a{l}_in")
            rc, rcb = a_conv_fwd(gr, small_full["a_conv_w"][l], row(small_full["a_conv_b"][l]), f"a{l}_conv")
            r_pre, i_pre = mm_bd(rcb, wr_g[l], wi_g[l], f"a{l}_gates")
            hs, gb = a_elem_fwd(gr, rc, r_pre, i_pre, row(small_full["a_b_r"][l]), row(small_full["a_b_i"][l]),
                                row(small_full["a_lambda"][l]), f"a{l}_lru")
            if l == 0:
                fetch(1, gb)
            mixed, w_mix = gb, gw[("a_w_out", l)]
            sv.update(gr=gr, rc=rc, rcb=rcb, r_pre=r_pre, i_pre=i_pre, hs=hs, gb=gb)
        else:
            j = l - la
            if j == 0:
                kv_cat = jnp.moveaxis(gw[("kv_w", 0)], 0, 1).reshape(d, S * nkv_s)
                kv_pad = jnp.pad(kv_cat, ((0, 0), (0, nkvp - nkv))).reshape(1, d, nkvp)
                kvz = mm_in(hb, kv_pad, F32, "kv_proj")
                cum = kv_fwd(kvz, fb_pad, d, "kv_forget")
                ct_pad = jnp.pad(cum[:, :heads].T, ((0, 0), (0, tp - t))).reshape(heads, 1, tp)
                kv_hb = hb
            qg = mm_in(hb, gw[("b_w_in", j)], F32, f"b{j}_in")
            o, mob = attn_fwd(qg, kvz, ct_pad, d, heads, f"b{j}_attn")
            mixed, w_mix = mob, gw[("b_w_out", j)]
            sv.update(qg=qg, o=o, mob=mob)
        h1, h1b, xh1, rs1 = mm_out_ln(mixed, w_mix, h, row(weights["ln1_g"][l]), row(weights["ln1_b"][l]), alpha,
                                      f"mix{l}_out_ln1")
        f_bias = row(weights["f_conv_b"][l])
        if l >= 1 and l + 2 < len(groups):
            f_bias = f_bias + fetch_early(l + 2, h1b)
        zf = mm_in(h1b, gw[("f_w_in", l)], F32, f"f{l}_in")
        ffb = f_elem_fwd(zf, small_full["f_conv_w"][l], f_bias, f"f{l}_act")
        h2, h2b, xh2, rs2 = mm_out_ln(ffb, gw[("f_w_out", l)], h1, row(weights["ln2_g"][l]), row(weights["ln2_b"][l]),
                                      alpha, f"f{l}_out_ln2")
        sv.update(h1b=h1b, xh1=xh1, rs1=rs1, zf=zf, ffb=ffb, xh2=xh2, rs2=rs2)
        saved.append(sv)
        h, hb = h2, h2b
    loss11, dy = loss_fwd_bwd(h, loss_target[0], nm, "loss")

    grads = {}

    def by_owner(kl, g3):
        rows, cols = shape2d[kl]
        grads[kl] = g3.reshape(S, 2, rows // 2, cols)

    reducing = []

    def send_grads(g, names, arrays):
        lands = [lax.empty((7,) + a.shape[2:], a.dtype) for a in arrays]
        started, token = _split_start(f"grad_a2a_start_{g}", [list(arrays) + lands],
                                      lambda _, refs, cnt: _all_to_all_plan(refs, cnt))
        reducing.append((names, started[0]))
        return token[0:1, 0:1]

    def after_start(vec, zero):
        return vec if zero is None else vec + zero

    pin = None

    g_small = {}
    per_layer = {k: [None] * n for k, n in (
        ("a_conv_w", la), ("a_conv_b", la), ("a_w_r", la), ("a_b_r", la), ("a_w_i", la), ("a_b_i", la),
        ("a_lambda", la), ("f_conv_w", depth), ("f_conv_b", depth), ("ln1_g", depth), ("ln1_b", depth),
        ("ln2_g", depth), ("ln2_b", depth))}
    adds = [(dy, 1.0)]
    dks, dvs, dcs = [], [], []
    for l in reversed(range(depth)):
        sv = saved[l]
        ds2, ds2b, dg2, db2 = ln_bwd(adds, sv["xh2"], sv["rs2"], after_start(row(weights["ln2_g"][l]), pin),
                                     f"ln2_{l}_bwd")
        pin = None
        per_layer["ln2_g"][l], per_layer["ln2_b"][l] = dg2[0], db2[0]
        dff_v = mm_out_nt(ds2b, gw[("f_w_out", l)], f"f{l}_out_dx")
        by_owner(("f_w_out", l), mm_tn(sv["ffb"], [ds2b], 1, f"f{l}_out_dw"))
        dzg, dzv, dwg, dwv, dbg, dbv = f_elem_bwd(sv["zf"], dff_v, small_full["f_conv_w"][l],
                                                  row(weights["f_conv_b"][l]), f"f{l}_act_bwd")
        per_layer["f_conv_w"][l] = jnp.concatenate([dwg, dwv], axis=1)
        per_layer["f_conv_b"][l] = jnp.concatenate([dbg, dbv], axis=1)[0]
        dh1_f = mm_in_nt([dzg, dzv], gw[("f_w_in", l)], f"f{l}_in_dx")
        by_owner(("f_w_in", l), mm_tn(sv["h1b"], [dzg, dzv], S, f"f{l}_in_dw"))
        w_mix = gw[("a_w_out", l)] if l < la else gw[("b_w_out", l - la)]
        ds1, ds1b, dg1, db1, d_mixed = ln_bwd([(ds2, alpha), (dh1_f, 1.0)], sv["xh1"], sv["rs1"],
                                              row(weights["ln1_g"][l]), f"ln1_{l}_bwd_mix_dx", w=w_mix)
        per_layer["ln1_g"][l], per_layer["ln1_b"][l] = dg1[0], db1[0]
        if l < la:
            dgv = d_mixed
            by_owner(("a_w_out", l), mm_tn(sv["gb"], [ds1b], 1, f"a{l}_out_dw"))
            if l == 0:
                pin = send_grads(1, groups[1], [grads[kl] for kl in groups[1]])
            dgate_b, drp_b, dip_b, drc_d, dlam, dbr, dbi = a_elem_bwd(
                dgv, sv["gr"], sv["rc"], sv["r_pre"], sv["i_pre"], sv["hs"], row(small_full["a_b_r"][l]),
                row(small_full["a_b_i"][l]), after_start(row(small_full["a_lambda"][l]), pin), f"a{l}_lru_bwd")
            pin = None
            drc_g = mm_bd_nt(drp_b, dip_b, wr_g[l], wi_g[l], f"a{l}_gates_dx")
            dwr_g, dwi_g = mm_bd_tn(sv["rcb"], drp_b, dip_b, gs, f"a{l}_gates_dw")
            dgr_b, dcw, dcb = a_conv_bwd(drc_d, drc_g, sv["gr"], small_full["a_conv_w"][l], dgate_b,
                                         f"a{l}_conv_bwd")
            per_layer["a_w_r"][l] = _block_diag_extract(dwr_g, per_group, bs)
            per_layer["a_w_i"][l] = _block_diag_extract(dwi_g, per_group, bs)
            per_layer["a_lambda"][l], per_layer["a_b_r"][l], per_layer["a_b_i"][l] = dlam[0], dbr[0], dbi[0]
            per_layer["a_conv_w"][l], per_layer["a_conv_b"][l] = dcw, dcb[0]
            by_owner(("a_w_in", l), mm_tn(sv["hb_in"], [dgr_b], S, f"a{l}_in_dw"))
            gate_w = gw[("a_w_in", l)]
            if l == 0:
                zero = send_grads(0, groups[0], [grads[kl] for kl in groups[0]])
                gate_w = gate_w + zero.astype(BF16)[None]
            dh_m = mm_in_nt([dgr_b], gate_w, f"a{l}_in_dx")
        else:
            j = l - la
            dmo = d_mixed
            by_owner(("b_w_out", j), mm_tn(sv["mob"], [ds1b], 1, f"b{j}_out_dw"))
            dq_b, dog_b, dk, dv, dct = attn_bwd(dmo, sv["qg"], kvz, sv["o"], ct_pad, d, heads, f"b{j}_attn_bwd")
            dks.append(dk)
            dvs.append(dv)
            dcs.append(jnp.pad(dct[:, 0, :t].T, ((0, 0), (0, LANES - heads))))
            dh_m = mm_in_nt([dq_b, dog_b], gw[("b_w_in", j)], f"b{j}_in_dx")
            by_owner(("b_w_in", j), mm_tn(sv["hb_in"], [dq_b, dog_b], S, f"b{j}_in_dw"))
        adds = [(ds1, alpha), (dh_m, 1.0)]
        if l == la:
            dzf_b, dfb = kv_bwd(dcs, kvz, fb_pad, d, "kv_forget_bwd")
            dk_b = add_cast(dks[0], dks[1], "kv_dk") if lb == 2 else None
            dv_b = add_cast(dvs[0], dvs[1], "kv_dv") if lb == 2 else None
            dz_kv = jnp.concatenate([dk_b, dv_b, dzf_b, jnp.zeros((t, nkvp - 2 * d - LANES), BF16)], axis=1)
            dh_kv = mm_in_nt([dz_kv], kv_pad, "kv_proj_dx")
            kv_dw = mm_tn(kv_hb, [dz_kv], 1, "kv_proj_dw")
            by_owner(("kv_w", 0), jnp.moveaxis(kv_dw[0, :, :nkv].reshape(d, S, nkv_s), 1, 0))
            g_small["kv_f_b"] = dfb[0, :heads]
            adds.append((dh_kv, 1.0))
        if l > 0:
            pin = send_grads(l + 1, groups[l + 1], [grads[kl] for kl in groups[l + 1]])
    g_meta, g_x = embed_bwd(adds, nm, "embed_bwd")

    small_names = list(SMALL_REPLICATED) + [k for k, _ in SMALL_SHARDED]
    g_small["meta"] = g_meta
    for k, vals in per_layer.items():
        g_small[k] = jnp.stack(vals)
    small_shapes = {k: (weights[k].shape if k in SMALL_REPLICATED else g_small[k].shape) for k in small_names}
    sm_g = _pack([g_small[k].reshape(small_shapes[k]) for k in small_names] + [loss11.reshape(1)],
                 S * 2 * SUBLANES * LANES)
    sm_g = sm_g.reshape(S, 2, -1, LANES)
    send_grads(len(groups), [("small", 0)], [sm_g])

    fin = {"small": lax.empty(sm_g.shape, F32)}
    for kl in keys:
        n_stack = weights[kl[0]].shape[0] if weights[kl[0]].ndim == 3 else 1
        rows, cols = shape2d[kl]
        fin.setdefault(kl[0], lax.empty((n_stack, 2, rows // 2, cols), F32))
    for g, (names_g, started) in enumerate(reducing):
        arrs = _split_wait(f"grad_a2a_wait_{g}", started, g_x, _all_to_all_plan)
        half = len(names_g)
        for i, (kl, cs, rv) in enumerate(zip(names_g, arrs[:half], arrs[half:])):
            fin[kl[0]] = owner_sum(shard_core_arr, cs, rv, fin[kl[0]], None if kl[0] == "small" else kl[1],
                                   f"owner_sum_{g}_{i}")
    names = list(BIG) + ["small"]
    joined = dict(zip(names, join_halves([fin[k] for k in BIG], fin["small"], "grad_join")))
    sm_red = joined["small"].reshape(-1)

    out_g, out_d, out_m, out_v = {}, {}, {}, {}
    for k in BIG:
        w2 = weights[k].reshape(-1, weights[k].shape[-1])
        g2 = joined[k].reshape(w2.shape)
        dlt, mn, vn, g_out = adamw(w2, g2, mom_m[k].reshape(w2.shape), mom_v[k].reshape(w2.shape), "adamw_" + k,
                                   emit_g=True)
        shp = weights[k].shape
        out_g[k], out_d[k], out_m[k], out_v[k] = g_out.reshape(shp), dlt.reshape(shp), mn.reshape(shp), vn.reshape(shp)
    *small_sums, loss_sum = _unpack(sm_red, [small_shapes[k] for k in small_names] + [(1,)])
    sm_vals = dict(zip(small_names, small_sums))
    local_small = {}
    for k in SMALL_REPLICATED:
        local_small[k] = sm_vals[k]
    for k, axis in SMALL_SHARDED:
        size = weights[k].shape[axis]
        local_small[k] = lax.dynamic_slice_in_dim(sm_vals[k], shard * size, size, axis=axis)
    for k in small_names:
        shp = weights[k].shape
        two_d = (-1, shp[-1]) if len(shp) > 1 else (1, -1)
        dlt, mn, vn = adamw(weights[k].reshape(two_d), local_small[k].reshape(two_d), mom_m[k].reshape(two_d),
                            mom_v[k].reshape(two_d), "adamw_" + k)
        out_g[k], out_d[k], out_m[k], out_v[k] = local_small[k], dlt.reshape(shp), mn.reshape(shp), vn.reshape(shp)

    return (loss_sum[0], g_x[None], *[out_g[k] for k in WEIGHT_ORDER], *[out_d[k] for k in WEIGHT_ORDER],
            *[out_m[k] for k in WEIGHT_ORDER], *[out_v[k] for k in WEIGHT_ORDER])
```

```python
import functools
import math

import jax
import jax.numpy as jnp
from jax import lax
from jax.experimental import pallas as pl
from jax.experimental.pallas import tpu as pltpu

F32 = jnp.float32
BF16 = jnp.bfloat16

LRU_C = 8.0
LN_EPS = 1e-5
ADAM_LR = 0.001
ADAM_B1 = 0.9
ADAM_B2 = 0.999
ADAM_EPS = 1e-08
ADAM_WD = 0.01
ADAM_STEP = 10

LANES = 128
SUBLANES = 8
V7X_VMEM_BYTES = 64 * 1024 * 1024
VMEM_LIMIT = V7X_VMEM_BYTES * 7 // 8
N_SHARDS = 4
GELU_C0 = math.sqrt(2.0 / math.pi)
GELU_C1 = 0.044715
NEG_BIG = -1e30


def _cp(*sem):
    return pltpu.CompilerParams(dimension_semantics=tuple(sem), vmem_limit_bytes=VMEM_LIMIT)


def _tile(n, cap, mult=LANES):
    best = None
    d = mult
    while d <= min(n, cap):
        if n % d == 0:
            best = d
        d += mult
    return n if best is None else best


def _row_block(t):
    if t % 3 == 0 and (t // 3) % 16 == 0:
        return t // 3
    return t


def _round_up(n, m):
    return (n + m - 1) // m * m


def _sigmoid(v):
    return 1.0 / (1.0 + jnp.exp(-v))


def _softplus(v):
    return jnp.maximum(v, 0.0) + jnp.log(1.0 + jnp.exp(-jnp.abs(v)))


def _gelu_parts(v):
    v2 = v * v
    u = GELU_C0 * (v + GELU_C1 * v * v2)
    t = jnp.tanh(u)
    g = 0.5 * v * (1.0 + t)
    dg = 0.5 * (1.0 + t) + 0.5 * v * (1.0 - t * t) * (GELU_C0 * (1.0 + 3.0 * GELU_C1 * v2))
    return g, dg


def _gelu(v):
    u = GELU_C0 * (v + GELU_C1 * v * v * v)
    return 0.5 * v * (1.0 + jnp.tanh(u))


def _neg_expm1(v):
    series = -v * (1.0 + 0.5 * v * (1.0 + (v / 3.0) * (1.0 + 0.25 * v)))
    return jnp.where(v > -0.05, series, 1.0 - jnp.exp(v))


def _shift_down(v, j):
    if j == 0:
        return v
    rows = lax.broadcasted_iota(jnp.int32, v.shape, 0)
    return jnp.where(rows >= j, pltpu.roll(v, j, 0), 0.0)


def _shift_up(v, j):
    if j == 0:
        return v
    n = v.shape[0]
    rows = lax.broadcasted_iota(jnp.int32, v.shape, 0)
    return jnp.where(rows < n - j, pltpu.roll(v, n - j, 0), 0.0)


def _scan_rows(a_ref, b_ref, out_ref, n_rows, width, reverse):
    n_groups = n_rows // SUBLANES
    rows = lax.broadcasted_iota(jnp.int32, (SUBLANES, width), 0)
    edge = 0 if reverse else SUBLANES - 1

    def body(g, carry):
        grp = (n_groups - 1 - g) if reverse else g
        off = pl.multiple_of(grp * SUBLANES, SUBLANES)
        b = b_ref[pl.ds(off, SUBLANES), :]
        a = None if a_ref is None else a_ref[pl.ds(off, SUBLANES), :]
        for d in (1, 2, 4):
            if reverse:
                keep = rows < SUBLANES - d
                sh = SUBLANES - d
            else:
                keep = rows >= d
                sh = d
            b_s = jnp.where(keep, pltpu.roll(b, sh, 0), 0.0)
            if a is None:
                b = b + b_s
            else:
                a_s = jnp.where(keep, pltpu.roll(a, sh, 0), 1.0)
                b = a * b_s + b
                a = a * a_s
        h = b + carry if a is None else b + a * carry
        out_ref[pl.ds(off, SUBLANES), :] = h
        return jnp.sum(jnp.where(rows == edge, h, 0.0), axis=0, keepdims=True)

    lax.fori_loop(0, n_groups, body, jnp.zeros((1, width), F32), unroll=2)


def mm_in(x, w, out_dtype, name):
    t, k = x.shape
    s_n, _, ns = w.shape
    tn = _tile(ns, 1408)
    nj = ns // tn
    rb = _row_block(t)

    def body(x_ref, w_ref, o_ref):
        o_ref[...] = jnp.dot(x_ref[...], w_ref[...], preferred_element_type=F32).astype(o_ref.dtype)

    return pl.pallas_call(
        body, name=name, grid=(s_n, nj, t // rb),
        in_specs=[pl.BlockSpec((rb, k), lambda s, j, r: (r, 0)),
                  pl.BlockSpec((None, k, tn), lambda s, j, r: (s, 0, j))],
        out_specs=pl.BlockSpec((rb, tn), lambda s, j, r: (r, s * nj + j)),
        out_shape=jax.ShapeDtypeStruct((t, s_n * ns), out_dtype),
        compiler_params=_cp("parallel", "parallel", "parallel"))(x, w)


def _part_map(p, per_part, nj, lead):
    def index(*grid):
        s, j = grid[-2], grid[-1]
        mine = s // per_part == p
        col = jnp.where(mine, (s - p * per_part) * nj + j, 0)
        return (grid[0], col) if lead else (0, col)
    return index


def mm_in_nt(dy_parts, w, name):
    n_parts = len(dy_parts)
    t = dy_parts[0].shape[0]
    s_n, k, ns = w.shape
    per_part = s_n // n_parts
    tn = _tile(ns, 1408)
    nj = ns // tn
    rb = _row_block(t)

    def body(*refs):
        w_ref, o_ref = refs[n_parts:]

        @pl.when((pl.program_id(1) == 0) & (pl.program_id(2) == 0))
        def _():
            o_ref[...] = jnp.zeros_like(o_ref)

        for p in range(n_parts):
            @pl.when(pl.program_id(1) // per_part == p)
            def _():
                o_ref[...] += lax.dot_general(refs[p][...], w_ref[...], (((1,), (1,)), ((), ())),
                                              preferred_element_type=F32)

    return pl.pallas_call(
        body, name=name, grid=(t // rb, s_n, nj),
        in_specs=[pl.BlockSpec((rb, tn), _part_map(p, per_part, nj, True)) for p in range(n_parts)]
        + [pl.BlockSpec((None, k, tn), lambda r, s, j: (s, 0, j))],
        out_specs=pl.BlockSpec((rb, k), lambda r, s, j: (r, 0)),
        out_shape=jax.ShapeDtypeStruct((t, k), F32),
        compiler_params=_cp("parallel", "arbitrary", "arbitrary"))(*dy_parts, w)


def mm_out_nt(dy, w, name):
    t, n = dy.shape
    k = w.shape[0]
    rb = _row_block(t)

    def body(dy_ref, w_ref, o_ref):
        o_ref[...] = lax.dot_general(dy_ref[...], w_ref[...], (((1,), (1,)), ((), ())),
                                     preferred_element_type=F32)

    return pl.pallas_call(
        body, name=name, grid=(t // rb,),
        in_specs=[pl.BlockSpec((rb, n), lambda r: (r, 0)), pl.BlockSpec((k, n), lambda r: (0, 0))],
        out_specs=pl.BlockSpec((rb, k), lambda r: (r, 0)),
        out_shape=jax.ShapeDtypeStruct((t, k), F32),
        compiler_params=_cp("parallel"))(dy, w)


def mm_tn(x, dy_parts, s_n, name):
    n_parts = len(dy_parts)
    t, kb = x.shape
    nb = dy_parts[0].shape[1] * n_parts // s_n
    per_part = max(s_n // n_parts, 1)
    tk = _tile(kb, 1408)
    tn = _tile(nb, 1408)
    nkb, nnb = kb // tk, nb // tn
    tn_dims = (((0,), (0,)), ((), ()))

    def body(*refs):
        x_ref, o_ref = refs[0], refs[-1]
        for p in range(n_parts):
            @pl.when(pl.program_id(1) // per_part == p)
            def _():
                o_ref[...] = lax.dot_general(x_ref[...], refs[1 + p][...], tn_dims,
                                             preferred_element_type=F32).astype(o_ref.dtype)

    return pl.pallas_call(
        body, name=name, grid=(nkb, s_n, nnb),
        in_specs=[pl.BlockSpec((t, tk), lambda a, s, b: (0, a))]
        + [pl.BlockSpec((t, tn), _part_map(p, per_part, nnb, False)) for p in range(n_parts)],
        out_specs=pl.BlockSpec((None, tk, tn), lambda a, s, b: (s, a, b)),
        out_shape=jax.ShapeDtypeStruct((s_n, kb, nb), BF16),
        compiler_params=_cp("parallel", "parallel", "parallel"))(x, *dy_parts)


def mm_bd(x, wr, wi, name):
    t, _ = x.shape
    g_n, gs, _ = wr.shape
    rb = _row_block(t)

    def body(x_ref, wr_ref, wi_ref, r_ref, i_ref):
        xv = x_ref[...]
        r_ref[...] = jnp.dot(xv, wr_ref[...], preferred_element_type=F32)
        i_ref[...] = jnp.dot(xv, wi_ref[...], preferred_element_type=F32)

    blk = pl.BlockSpec((rb, gs), lambda g, r: (r, g))
    wspec = pl.BlockSpec((None, gs, gs), lambda g, r: (g, 0, 0))
    return pl.pallas_call(
        body, name=name, grid=(g_n, t // rb), in_specs=[blk, wspec, wspec], out_specs=[blk, blk],
        out_shape=[jax.ShapeDtypeStruct((t, g_n * gs), F32)] * 2,
        compiler_params=_cp("parallel", "parallel"))(x, wr, wi)


def mm_bd_nt(dr, di, wr, wi, name):
    t, _ = dr.shape
    g_n, gs, _ = wr.shape
    rb = _row_block(t)
    nt = (((1,), (1,)), ((), ()))

    def body(dr_ref, di_ref, wr_ref, wi_ref, o_ref):
        o_ref[...] = (lax.dot_general(dr_ref[...], wr_ref[...], nt, preferred_element_type=F32)
                      + lax.dot_general(di_ref[...], wi_ref[...], nt, preferred_element_type=F32))

    blk = pl.BlockSpec((rb, gs), lambda g, r: (r, g))
    wspec = pl.BlockSpec((None, gs, gs), lambda g, r: (g, 0, 0))
    return pl.pallas_call(
        body, name=name, grid=(g_n, t // rb), in_specs=[blk, blk, wspec, wspec], out_specs=blk,
        out_shape=jax.ShapeDtypeStruct((t, g_n * gs), F32),
        compiler_params=_cp("parallel", "parallel"))(dr, di, wr, wi)


def mm_bd_tn(x, dr, di, gs, name):
    t, w = x.shape
    g_n = w // gs
    tn_dims = (((0,), (0,)), ((), ()))

    def body(x_ref, dr_ref, di_ref, gr_ref, gi_ref):
        xv = x_ref[...]
        gr_ref[...] = lax.dot_general(xv, dr_ref[...], tn_dims, preferred_element_type=F32)
        gi_ref[...] = lax.dot_general(xv, di_ref[...], tn_dims, preferred_element_type=F32)

    blk = pl.BlockSpec((t, gs), lambda g: (0, g))
    ospec = pl.BlockSpec((None, gs, gs), lambda g: (g, 0, 0))
    return pl.pallas_call(
        body, name=name, grid=(g_n,), in_specs=[blk, blk, blk], out_specs=[ospec, ospec],
        out_shape=[jax.ShapeDtypeStruct((g_n, gs, gs), F32)] * 2,
        compiler_params=_cp("parallel"))(x, dr, di)


def embed_fwd(meta, x2d, name):
    nm, d = meta.shape
    seq = x2d.shape[0]
    t = nm + seq
    cb = _tile(d, 256)

    def body(m_ref, x_ref, h_ref, hb_ref):
        h_ref[pl.ds(0, nm), :] = m_ref[...]
        h_ref[pl.ds(nm, seq), :] = x_ref[...]
        hb_ref[pl.ds(0, nm), :] = m_ref[...].astype(BF16)
        hb_ref[pl.ds(nm, seq), :] = x_ref[...].astype(BF16)

    return pl.pallas_call(
        body, name=name, grid=(d // cb,),
        in_specs=[pl.BlockSpec((nm, cb), lambda j: (0, j)), pl.BlockSpec((seq, cb), lambda j: (0, j))],
        out_specs=[pl.BlockSpec((t, cb), lambda j: (0, j))] * 2,
        out_shape=[jax.ShapeDtypeStruct((t, d), F32), jax.ShapeDtypeStruct((t, d), BF16)],
        compiler_params=_cp("parallel"))(meta, x2d)


def embed_bwd(adds, nm, name):
    t, d = adds[0][0].shape
    seq = t - nm
    cb = _tile(d, 256)
    scales = [s for _, s in adds]
    n = len(adds)

    def body(*refs):
        tot = None
        for r, sc in zip(refs[:n], scales):
            term = r[...] if sc == 1.0 else sc * r[...]
            tot = term if tot is None else tot + term
        gm_ref, gx_ref = refs[n], refs[n + 1]
        gm_ref[...] = tot[0:nm]
        gx_ref[...] = tot[nm:t]

    return pl.pallas_call(
        body, name=name, grid=(d // cb,),
        in_specs=[pl.BlockSpec((t, cb), lambda j: (0, j))] * n,
        out_specs=[pl.BlockSpec((nm, cb), lambda j: (0, j)), pl.BlockSpec((seq, cb), lambda j: (0, j))],
        out_shape=[jax.ShapeDtypeStruct((nm, d), F32), jax.ShapeDtypeStruct((seq, d), F32)],
        compiler_params=_cp("parallel"))(*[a for a, _ in adds])


def loss_fwd_bwd(h, tgt, nm, name):
    t, d = h.shape
    seq = t - nm
    cb = _tile(d, 256)
    inv_d = 1.0 / d

    def body(h_ref, t_ref, loss_ref, dy_ref):
        @pl.when(pl.program_id(0) == 0)
        def _():
            loss_ref[...] = jnp.zeros_like(loss_ref)
        err = h_ref[pl.ds(nm, seq), :] - t_ref[...]
        dy_ref[pl.ds(0, nm), :] = jnp.zeros((nm, cb), F32)
        dy_ref[pl.ds(nm, seq), :] = err * inv_d
        loss_ref[...] += (0.5 * inv_d) * jnp.sum(err * err, keepdims=True)

    return pl.pallas_call(
        body, name=name, grid=(d // cb,),
        in_specs=[pl.BlockSpec((t, cb), lambda j: (0, j)), pl.BlockSpec((seq, cb), lambda j: (0, j))],
        out_specs=[pl.BlockSpec((1, 1), lambda j: (0, 0)), pl.BlockSpec((t, cb), lambda j: (0, j))],
        out_shape=[jax.ShapeDtypeStruct((1, 1), F32), jax.ShapeDtypeStruct((t, d), F32)],
        compiler_params=_cp("arbitrary"))(h, tgt)


def mm_out_ln(x, w, h, g, b, alpha, name):
    t, d = h.shape
    k = x.shape[1]
    rb = _row_block(t)

    def body(x_ref, w_ref, h_ref, g_ref, b_ref, y_ref, yb_ref, xh_ref, rs_ref):
        s = alpha * h_ref[...] + jnp.dot(x_ref[...], w_ref[...], preferred_element_type=F32)
        mu = jnp.mean(s, axis=-1, keepdims=True)
        c = s - mu
        var = jnp.mean(c * c, axis=-1, keepdims=True)
        rstd = lax.rsqrt(var + LN_EPS)
        xh = c * rstd
        y = xh * g_ref[...] + b_ref[...]
        y_ref[...] = y
        yb_ref[...] = y.astype(BF16)
        xh_ref[...] = xh
        rs_ref[...] = rstd

    row = pl.BlockSpec((rb, d), lambda r: (r, 0))
    vec = pl.BlockSpec((1, d), lambda r: (0, 0))
    return pl.pallas_call(
        body, name=name, grid=(t // rb,),
        in_specs=[pl.BlockSpec((rb, k), lambda r: (r, 0)), pl.BlockSpec((k, d), lambda r: (0, 0)), row, vec, vec],
        out_specs=[row, row, row, pl.BlockSpec((rb, 1), lambda r: (r, 0))],
        out_shape=[jax.ShapeDtypeStruct((t, d), F32), jax.ShapeDtypeStruct((t, d), BF16),
                   jax.ShapeDtypeStruct((t, d), F32), jax.ShapeDtypeStruct((t, 1), F32)],
        compiler_params=_cp("parallel"))(x, w, h, g, b)


def ln_bwd(adds, xhat, rstd, g, name, w=None):
    t, d = xhat.shape
    rb = _row_block(t)
    scales = [s for _, s in adds]
    n = len(adds)
    n_in = n + 3 + (w is not None)

    def body(*refs):
        xh_ref, rs_ref, g_ref = refs[n:n + 3]
        ds_ref, dsb_ref, dg_ref, db_ref = refs[n_in:n_in + 4]
        dy = None
        for r, sc in zip(refs[:n], scales):
            term = r[...] if sc == 1.0 else sc * r[...]
            dy = term if dy is None else dy + term

        @pl.when(pl.program_id(0) == 0)
        def _():
            dg_ref[...] = jnp.zeros_like(dg_ref)
            db_ref[...] = jnp.zeros_like(db_ref)

        xh = xh_ref[...]
        dxh = dy * g_ref[...]
        m1 = jnp.mean(dxh, axis=-1, keepdims=True)
        m2 = jnp.mean(dxh * xh, axis=-1, keepdims=True)
        ds = rs_ref[...] * (dxh - m1 - xh * m2)
        ds_b = ds.astype(BF16)
        ds_ref[...] = ds
        dsb_ref[...] = ds_b
        dg_ref[...] += jnp.sum(dy * xh, axis=0, keepdims=True)
        db_ref[...] += jnp.sum(dy, axis=0, keepdims=True)
        if w is not None:
            refs[-1][...] = lax.dot_general(ds_b, refs[n + 3][...], (((1,), (1,)), ((), ())),
                                            preferred_element_type=F32)

    row = pl.BlockSpec((rb, d), lambda r: (r, 0))
    vec = pl.BlockSpec((1, d), lambda r: (0, 0))
    in_specs = [row] * n + [row, pl.BlockSpec((rb, 1), lambda r: (r, 0)), vec]
    out_specs = [row, row, vec, vec]
    out_shape = [jax.ShapeDtypeStruct((t, d), F32), jax.ShapeDtypeStruct((t, d), BF16),
                 jax.ShapeDtypeStruct((1, d), F32), jax.ShapeDtypeStruct((1, d), F32)]
    operands = [a for a, _ in adds] + [xhat, rstd, g]
    if w is not None:
        k = w.shape[0]
        in_specs.append(pl.BlockSpec((k, d), lambda r: (0, 0)))
        out_specs.append(pl.BlockSpec((rb, k), lambda r: (r, 0)))
        out_shape.append(jax.ShapeDtypeStruct((t, k), F32))
        operands.append(w)
    return pl.pallas_call(
        body, name=name, grid=(t // rb,), in_specs=in_specs, out_specs=out_specs, out_shape=out_shape,
        compiler_params=_cp("arbitrary"))(*operands)


def _conv_fwd_val(xv, w_ref, b_ref, width):
    y = b_ref[...]
    for j in range(width):
        y = y + _shift_down(xv, j) * w_ref[pl.ds(width - 1 - j, 1), :]
    return y


def _conv_bwd_val(dout, xv, w_ref, width):
    dx = None
    dws = [None] * width
    for j in range(width):
        k = width - 1 - j
        term = _shift_up(dout, j) * w_ref[pl.ds(k, 1), :]
        dx = term if dx is None else dx + term
        dws[k] = jnp.sum(dout * _shift_down(xv, j), axis=0, keepdims=True)
    return dx, dws, jnp.sum(dout, axis=0, keepdims=True)


def a_conv_fwd(gr, cw, cbias, name):
    t, two_dr = gr.shape
    dr = two_dr // 2
    width = cw.shape[0]
    cb = _tile(dr, 256)
    off = dr // cb

    def body(x_ref, w_ref, b_ref, rc_ref, rcb_ref):
        y = _conv_fwd_val(x_ref[...], w_ref, b_ref, width)
        rc_ref[...] = y
        rcb_ref[...] = y.astype(BF16)

    return pl.pallas_call(
        body, name=name, grid=(dr // cb,),
        in_specs=[pl.BlockSpec((t, cb), lambda j: (0, off + j)),
                  pl.BlockSpec((width, cb), lambda j: (0, j)), pl.BlockSpec((1, cb), lambda j: (0, j))],
        out_specs=[pl.BlockSpec((t, cb), lambda j: (0, j))] * 2,
        out_shape=[jax.ShapeDtypeStruct((t, dr), F32), jax.ShapeDtypeStruct((t, dr), BF16)],
        compiler_params=_cp("parallel"))(gr, cw, cbias)


def a_conv_bwd(drc_a, drc_b, gr, cw, dgr, name):
    t, two_dr = gr.shape
    dr = two_dr // 2
    width = cw.shape[0]
    cb = _tile(dr, 256)
    off = dr // cb

    def body(da_ref, db_ref, x_ref, w_ref, dgr_in, dx_ref, dw_ref, dbias_ref):
        del dgr_in
        dout = da_ref[...] + db_ref[...]
        dx, dws, dbias = _conv_bwd_val(dout, x_ref[...], w_ref, width)
        dx_ref[...] = dx.astype(BF16)
        for k in range(width):
            dw_ref[pl.ds(k, 1), :] = dws[k]
        dbias_ref[...] = dbias

    col = pl.BlockSpec((t, cb), lambda j: (0, j))
    return pl.pallas_call(
        body, name=name, grid=(dr // cb,),
        in_specs=[col, col, pl.BlockSpec((t, cb), lambda j: (0, off + j)),
                  pl.BlockSpec((width, cb), lambda j: (0, j)), pl.BlockSpec(memory_space=pl.ANY)],
        out_specs=[pl.BlockSpec((t, cb), lambda j: (0, off + j)), pl.BlockSpec((width, cb), lambda j: (0, j)),
                   pl.BlockSpec((1, cb), lambda j: (0, j))],
        out_shape=[jax.ShapeDtypeStruct((t, two_dr), BF16), jax.ShapeDtypeStruct((width, dr), F32),
                   jax.ShapeDtypeStruct((1, dr), F32)],
        input_output_aliases={4: 0},
        compiler_params=_cp("parallel"))(drc_a, drc_b, gr, cw, dgr)


def _lru_gates(r_pre, i_pre, br, bi, lam):
    r = _sigmoid(r_pre + br)
    i = _sigmoid(i_pre + bi)
    sp = _softplus(-lam)
    la = -LRU_C * r * sp
    a = jnp.exp(la)
    m = jnp.sqrt(_neg_expm1(2.0 * la))
    return r, i, sp, la, a, m


def a_elem_fwd(gr, rc, r_pre, i_pre, br, bi, lam, name):
    t, dr = rc.shape
    cb = _tile(dr, 3 * LANES)
    rb = _row_block(t)
    chunks = [pl.ds(r * rb, rb) for r in range(t // rb)]

    def body(gate_ref, rc_ref, rp_ref, ip_ref, br_ref, bi_ref, lam_ref, hs_ref, g_ref, a_s, u_s):
        for rows in chunks:
            _, i, _, _, a, m = _lru_gates(rp_ref[rows, :], ip_ref[rows, :], br_ref[...], bi_ref[...], lam_ref[...])
            a_s[rows, :] = a
            u_s[rows, :] = m * (i * rc_ref[rows, :])
        _scan_rows(a_s, u_s, hs_ref, t, cb, reverse=False)
        for rows in chunks:
            g_ref[rows, :] = (_gelu(gate_ref[rows, :]) * hs_ref[rows, :]).astype(BF16)

    col = pl.BlockSpec((t, cb), lambda j: (0, j))
    vec = pl.BlockSpec((1, cb), lambda j: (0, j))
    return pl.pallas_call(
        body, name=name, grid=(dr // cb,),
        in_specs=[col, col, col, col, vec, vec, vec],
        out_specs=[col, col],
        out_shape=[jax.ShapeDtypeStruct((t, dr), F32), jax.ShapeDtypeStruct((t, dr), BF16)],
        scratch_shapes=[pltpu.VMEM((t, cb), F32), pltpu.VMEM((t, cb), F32)],
        compiler_params=_cp("parallel"))(gr, rc, r_pre, i_pre, br, bi, lam)


def a_elem_bwd(dg, gr, rc, r_pre, i_pre, hs, br, bi, lam, name):
    t, dr = rc.shape
    cb = _tile(dr, 2 * LANES)
    rb = _row_block(t)
    chunks = [pl.ds(r * rb, rb) for r in range(t // rb)]

    def body(dg_ref, gate_ref, rc_ref, rp_ref, ip_ref, hs_ref, br_ref, bi_ref, lam_ref,
             dgate_ref, dr_ref, di_ref, drc_ref, dlam_ref, dbr_ref, dbi_ref, a_s, b_s, g_s, hp_s):
        lamv = lam_ref[...]
        gates = lambda rows: _lru_gates(rp_ref[rows, :], ip_ref[rows, :], br_ref[...], bi_ref[...], lamv)
        for rows in chunks:
            a_s[rows, :] = gates(rows)[4]
            ge, dge = _gelu_parts(gate_ref[rows, :])
            dgv = dg_ref[rows, :]
            dgate_ref[rows, :] = (dgv * hs_ref[rows, :] * dge).astype(BF16)
            b_s[rows, :] = dgv * ge
        a_s[...] = _shift_up(a_s[...], 1)
        hp_s[...] = _shift_down(hs_ref[...], 1)
        _scan_rows(a_s, b_s, g_s, t, cb, reverse=True)
        dsp = dbr = dbi = jnp.zeros((1, cb), F32)
        for rows in chunks:
            r, i, sp, _, a, m = gates(rows)
            rcv = rc_ref[rows, :]
            gsum = g_s[rows, :]
            da = gsum * hp_s[rows, :]
            dm = gsum * (i * rcv)
            d_i = gsum * m * rcv
            drc_ref[rows, :] = gsum * m * i
            dla = a * da - dm * (a * a) / m
            d_r = (-LRU_C) * sp * dla
            dsp = dsp + jnp.sum((-LRU_C) * r * dla, axis=0, keepdims=True)
            d_rp = d_r * r * (1.0 - r)
            d_ip = d_i * i * (1.0 - i)
            dr_ref[rows, :] = d_rp.astype(BF16)
            di_ref[rows, :] = d_ip.astype(BF16)
            dbr = dbr + jnp.sum(d_rp, axis=0, keepdims=True)
            dbi = dbi + jnp.sum(d_ip, axis=0, keepdims=True)
        dlam_ref[...] = -dsp * _sigmoid(-lamv)
        dbr_ref[...] = dbr
        dbi_ref[...] = dbi

    col = pl.BlockSpec((t, cb), lambda j: (0, j))
    vec = pl.BlockSpec((1, cb), lambda j: (0, j))
    big_b = jax.ShapeDtypeStruct((t, dr), BF16)
    vec_s = jax.ShapeDtypeStruct((1, dr), F32)
    return pl.pallas_call(
        body, name=name, grid=(dr // cb,),
        in_specs=[col, col, col, col, col, col, vec, vec, vec],
        out_specs=[col, col, col, col, vec, vec, vec],
        out_shape=[jax.ShapeDtypeStruct((t, 2 * dr), BF16), big_b, big_b, jax.ShapeDtypeStruct((t, dr), F32),
                   vec_s, vec_s, vec_s],
        scratch_shapes=[pltpu.VMEM((t, cb), F32)] * 4,
        compiler_params=_cp("parallel"))(dg, gr, rc, r_pre, i_pre, hs, br, bi, lam)


def f_elem_fwd(z, cw, cbias, name):
    t, two_f = z.shape
    dff = two_f // 2
    width = cw.shape[0]
    cb = _tile(dff, 256)
    off = dff // cb

    def body(zg_ref, zv_ref, wg_ref, wv_ref, bg_ref, bv_ref, o_ref):
        zcg = _conv_fwd_val(zg_ref[...], wg_ref, bg_ref, width)
        zcv = _conv_fwd_val(zv_ref[...], wv_ref, bv_ref, width)
        o_ref[...] = (_gelu(zcg) * zcv).astype(BF16)

    lo = lambda j: (0, j)
    hi = lambda j: (0, off + j)
    return pl.pallas_call(
        body, name=name, grid=(dff // cb,),
        in_specs=[pl.BlockSpec((t, cb), lo), pl.BlockSpec((t, cb), hi),
                  pl.BlockSpec((width, cb), lo), pl.BlockSpec((width, cb), hi),
                  pl.BlockSpec((1, cb), lo), pl.BlockSpec((1, cb), hi)],
        out_specs=pl.BlockSpec((t, cb), lo),
        out_shape=jax.ShapeDtypeStruct((t, dff), BF16),
        compiler_params=_cp("parallel"))(z, z, cw, cw, cbias, cbias)


def f_elem_bwd(z, dff_g, cw, cbias, name):
    t, two_f = z.shape
    dff = two_f // 2
    width = cw.shape[0]
    cb = _tile(dff, 256)
    off = dff // cb

    def body(zg_ref, zv_ref, d_ref, wg_ref, wv_ref, bg_ref, bv_ref,
             dzg_ref, dzv_ref, dwg_ref, dwv_ref, dbg_ref, dbv_ref):
        zg = zg_ref[...]
        zv = zv_ref[...]
        zcg = _conv_fwd_val(zg, wg_ref, bg_ref, width)
        zcv = _conv_fwd_val(zv, wv_ref, bv_ref, width)
        ge, dge = _gelu_parts(zcg)
        dv = d_ref[...]
        dx, dws, dbias = _conv_bwd_val(dv * zcv * dge, zg, wg_ref, width)
        dzg_ref[...] = dx.astype(BF16)
        for k in range(width):
            dwg_ref[pl.ds(k, 1), :] = dws[k]
        dbg_ref[...] = dbias
        dx, dws, dbias = _conv_bwd_val(dv * ge, zv, wv_ref, width)
        dzv_ref[...] = dx.astype(BF16)
        for k in range(width):
            dwv_ref[pl.ds(k, 1), :] = dws[k]
        dbv_ref[...] = dbias

    lo = lambda j: (0, j)
    hi = lambda j: (0, off + j)
    col = pl.BlockSpec((t, cb), lo)
    wsp = pl.BlockSpec((width, cb), lo)
    vsp = pl.BlockSpec((1, cb), lo)
    return pl.pallas_call(
        body, name=name, grid=(dff // cb,),
        in_specs=[col, pl.BlockSpec((t, cb), hi), col, wsp, pl.BlockSpec((width, cb), hi),
                  vsp, pl.BlockSpec((1, cb), hi)],
        out_specs=[col, col, wsp, wsp, vsp, vsp],
        out_shape=[jax.ShapeDtypeStruct((t, dff), BF16)] * 2
        + [jax.ShapeDtypeStruct((width, dff), F32)] * 2 + [jax.ShapeDtypeStruct((1, dff), F32)] * 2,
        compiler_params=_cp("parallel"))(z, z, dff_g, cw, cw, cbias, cbias)


def kv_fwd(z, fb, d_model, name):
    t, _ = z.shape
    blk = 2 * d_model // LANES

    def body(z_ref, fb_ref, c_ref, lf_s):
        v = z_ref[...] + fb_ref[...]
        lf_s[...] = -_softplus(-v)
        _scan_rows(None, lf_s, c_ref, t, LANES, reverse=False)

    return pl.pallas_call(
        body, name=name, grid=(1,),
        in_specs=[pl.BlockSpec((t, LANES), lambda j: (0, blk)), pl.BlockSpec((1, LANES), lambda j: (0, 0))],
        out_specs=pl.BlockSpec((t, LANES), lambda j: (0, 0)),
        out_shape=jax.ShapeDtypeStruct((t, LANES), F32),
        scratch_shapes=[pltpu.VMEM((t, LANES), F32)],
        compiler_params=_cp("arbitrary"))(z, fb)


def kv_bwd(dcs, z, fb, d_model, name):
    t, _ = z.shape
    blk = 2 * d_model // LANES
    n = len(dcs)

    def body(*refs):
        z_ref, fb_ref, dz_ref, dfb_ref, dc_s, dl_s = refs[n:]
        tot = refs[0][...]
        for r in refs[1:n]:
            tot = tot + r[...]
        dc_s[...] = tot
        _scan_rows(None, dc_s, dl_s, t, LANES, reverse=True)
        v = z_ref[...] + fb_ref[...]
        dz = dl_s[...] * _sigmoid(-v)
        dz_ref[...] = dz.astype(BF16)
        dfb_ref[...] = jnp.sum(dz, axis=0, keepdims=True)

    full = pl.BlockSpec((t, LANES), lambda j: (0, 0))
    return pl.pallas_call(
        body, name=name, grid=(1,),
        in_specs=[full] * n + [pl.BlockSpec((t, LANES), lambda j: (0, blk)),
                               pl.BlockSpec((1, LANES), lambda j: (0, 0))],
        out_specs=[full, pl.BlockSpec((1, LANES), lambda j: (0, 0))],
        out_shape=[jax.ShapeDtypeStruct((t, LANES), BF16), jax.ShapeDtypeStruct((1, LANES), F32)],
        scratch_shapes=[pltpu.VMEM((t, LANES), F32)] * 2,
        compiler_params=_cp("arbitrary"))(*dcs, z, fb)


def add_cast(a, b, name):
    t, d = a.shape
    cb = _tile(d, 512)

    def body(a_ref, b_ref, o_ref):
        o_ref[...] = (a_ref[...] + b_ref[...]).astype(BF16)

    col = pl.BlockSpec((t, cb), lambda j: (0, j))
    return pl.pallas_call(body, name=name, grid=(d // cb,), in_specs=[col, col], out_specs=col,
                          out_shape=jax.ShapeDtypeStruct((t, d), BF16),
                          compiler_params=_cp("parallel"))(a, b)


def _attn_geometry(t):
    nqb = 6 if t > 1024 else 2
    tp = _round_up(t, LANES * nqb)
    return nqb, tp, tp // nqb


def _attn_scales(dh):
    scale = dh ** -0.5
    if math.log2(scale).is_integer():
        return scale, 1.0
    return 1.0, scale


def _attn_pieces(qs, ks, crow, j, i, tq, dh, s_mul):
    r0 = i * tq
    lanes = pl.ds(j * dh, dh)
    qi = qs[pl.ds(r0, tq), lanes]
    spans = ([(0, r0)] if i > 0 else []) + [(r0, tq)]
    logits = []
    for k0, n in spans:
        s = lax.dot_general(qi, ks[pl.ds(k0, n), lanes], (((1,), (1,)), ((), ())),
                            preferred_element_type=F32)
        if s_mul != 1.0:
            s = s * s_mul
        s = s - crow[:, k0:k0 + n]
        if k0 == r0:
            rows = lax.broadcasted_iota(jnp.int32, (tq, tq), 0)
            cols = lax.broadcasted_iota(jnp.int32, (tq, tq), 1)
            s = jnp.where(cols <= rows, s, NEG_BIG)
        logits.append(s)
    mx = jnp.max(logits[0], axis=1, keepdims=True)
    for s in logits[1:]:
        mx = jnp.maximum(mx, jnp.max(s, axis=1, keepdims=True))
    es = [jnp.exp(s - mx) for s in logits]
    tot = jnp.sum(es[0], axis=1, keepdims=True)
    for e in es[1:]:
        tot = tot + jnp.sum(e, axis=1, keepdims=True)
    inv = 1.0 / tot
    return [(k0, n, e * inv) for (k0, n), e in zip(spans, es)], qi


def attn_fwd(qg, z, ct_pad, d_model, n_heads, name):
    t = qg.shape[0]
    dh = d_model // n_heads
    hp = LANES // dh
    nqb, tp, tq = _attn_geometry(t)
    nblk = d_model // LANES
    q_mul, s_mul = _attn_scales(dh)

    def body(q_ref, og_ref, k_ref, v_ref, ct_ref, o_ref, mo_ref, qs, ks, vs, os_):
        pad = jnp.zeros((tp - t, LANES), BF16)
        qs[pl.ds(0, t), :] = (q_ref[...] * q_mul).astype(BF16)
        qs[pl.ds(t, tp - t), :] = pad
        for src, dst in ((k_ref, ks), (v_ref, vs)):
            dst[pl.ds(0, t), :] = src[...].astype(BF16)
            dst[pl.ds(t, tp - t), :] = pad
        for j in range(hp):
            crow = ct_ref[j]
            lanes = pl.ds(j * dh, dh)
            for i in range(nqb):
                pieces, _ = _attn_pieces(qs, ks, crow, j, i, tq, dh, s_mul)
                acc = None
                for k0, n, p in pieces:
                    part = jnp.dot(p.astype(BF16), vs[pl.ds(k0, n), lanes], preferred_element_type=F32)
                    acc = part if acc is None else acc + part
                os_[pl.ds(i * tq, tq), lanes] = acc
        o = os_[pl.ds(0, t), :]
        o_ref[...] = o
        mo_ref[...] = (o * _sigmoid(og_ref[...])).astype(BF16)

    col = lambda off: pl.BlockSpec((t, LANES), lambda p: (0, off + p))
    return pl.pallas_call(
        body, name=name, grid=(nblk,),
        in_specs=[col(0), col(nblk), col(0), col(nblk), pl.BlockSpec((hp, 1, tp), lambda p: (p, 0, 0))],
        out_specs=[col(0), col(0)],
        out_shape=[jax.ShapeDtypeStruct((t, d_model), F32), jax.ShapeDtypeStruct((t, d_model), BF16)],
        scratch_shapes=[pltpu.VMEM((tp, LANES), BF16)] * 3 + [pltpu.VMEM((tp, LANES), F32)],
        compiler_params=_cp("parallel"))(qg, qg, z, z, ct_pad)


def attn_bwd(dmo, qg, z, o, ct_pad, d_model, n_heads, name):
    t = qg.shape[0]
    dh = d_model // n_heads
    hp = LANES // dh
    nqb, tp, tq = _attn_geometry(t)
    nblk = d_model // LANES
    q_mul, s_mul = _attn_scales(dh)
    scale = dh ** -0.5
    tn_dims = (((0,), (0,)), ((), ()))
    nt_dims = (((1,), (1,)), ((), ()))

    def body(dmo_ref, q_ref, og_ref, k_ref, v_ref, o_ref, ct_ref,
             dq_ref, dog_ref, dk_ref, dv_ref, dct_ref, qs, ks, vs, dos, dqs, dks, dvs):
        pad = jnp.zeros((tp - t, LANES), BF16)
        sg = _sigmoid(og_ref[...])
        dmo_v = dmo_ref[...]
        dog_ref[...] = (dmo_v * o_ref[...] * sg * (1.0 - sg)).astype(BF16)
        dos[pl.ds(0, t), :] = (dmo_v * sg).astype(BF16)
        dos[pl.ds(t, tp - t), :] = pad
        qs[pl.ds(0, t), :] = (q_ref[...] * q_mul).astype(BF16)
        qs[pl.ds(t, tp - t), :] = pad
        for src, dst in ((k_ref, ks), (v_ref, vs)):
            dst[pl.ds(0, t), :] = src[...].astype(BF16)
            dst[pl.ds(t, tp - t), :] = pad
        dks[...] = jnp.zeros_like(dks)
        dvs[...] = jnp.zeros_like(dvs)
        dct_ref[...] = jnp.zeros_like(dct_ref)
        for j in range(hp):
            crow = ct_ref[j]
            lanes = pl.ds(j * dh, dh)
            for i in range(nqb):
                pieces, qi = _attn_pieces(qs, ks, crow, j, i, tq, dh, s_mul)
                do_i = dos[pl.ds(i * tq, tq), lanes]
                dps = [lax.dot_general(do_i, vs[pl.ds(k0, n), lanes], nt_dims, preferred_element_type=F32)
                       for k0, n, _ in pieces]
                row = None
                for (_, _, p), dp in zip(pieces, dps):
                    part = jnp.sum(p * dp, axis=1, keepdims=True)
                    row = part if row is None else row + part
                dq_i = None
                for (k0, n, p), dp in zip(pieces, dps):
                    ds = p * (dp - row)
                    ds_b = ds.astype(BF16)
                    keys = pl.ds(k0, n)
                    part = jnp.dot(ds_b, ks[keys, lanes], preferred_element_type=F32)
                    dq_i = part if dq_i is None else dq_i + part
                    dks[keys, lanes] += lax.dot_general(ds_b, qi, tn_dims, preferred_element_type=F32) * s_mul
                    dvs[keys, lanes] += lax.dot_general(p.astype(BF16), do_i, tn_dims,
                                                        preferred_element_type=F32)
                    dct_ref[j, :, keys] -= jnp.sum(ds, axis=0, keepdims=True)
                dqs[pl.ds(i * tq, tq), lanes] = dq_i * scale
        dq_ref[...] = dqs[pl.ds(0, t), :].astype(BF16)
        dk_ref[...] = dks[pl.ds(0, t), :]
        dv_ref[...] = dvs[pl.ds(0, t), :]

    col = lambda off: pl.BlockSpec((t, LANES), lambda p: (0, off + p))
    big = lambda dt: jax.ShapeDtypeStruct((t, d_model), dt)
    return pl.pallas_call(
        body, name=name, grid=(nblk,),
        in_specs=[col(0), col(0), col(nblk), col(0), col(nblk), col(0),
                  pl.BlockSpec((hp, 1, tp), lambda p: (p, 0, 0))],
        out_specs=[col(0), col(0), col(0), col(0), pl.BlockSpec((hp, 1, tp), lambda p: (p, 0, 0))],
        out_shape=[big(BF16), big(BF16), big(F32), big(F32),
                   jax.ShapeDtypeStruct((n_heads, 1, tp), F32)],
        scratch_shapes=[pltpu.VMEM((tp, LANES), BF16)] * 4 + [pltpu.VMEM((tp, LANES), F32)] * 3,
        compiler_params=_cp("parallel"))(dmo, qg, qg, z, z, o, ct_pad)


def cast_into_slot(shard, w, index, name):
    r, c = w.shape[-2:]
    rh = r // 2
    tr = _tile(rh, 512, 16)
    n = rh // tr
    if w.ndim == 3:
        w_spec = pl.BlockSpec((None, tr, c), lambda h, i, sh: (index, h * n + i, 0))
    else:
        w_spec = pl.BlockSpec((tr, c), lambda h, i, sh: (h * n + i, 0))

    def body(sh_ref, w_ref, o_ref):
        del sh_ref
        o_ref[...] = w_ref[...].astype(BF16)

    return pl.pallas_call(
        body, name=name,
        grid_spec=pltpu.PrefetchScalarGridSpec(
            num_scalar_prefetch=1, grid=(2, n), in_specs=[w_spec],
            out_specs=pl.BlockSpec((None, None, tr, c), lambda h, i, sh: (sh[0], h, i, 0))),
        out_shape=jax.ShapeDtypeStruct((N_SHARDS, 2, rh, c), BF16),
        compiler_params=_cp("parallel", "parallel"))(shard, w)


def owner_sum(shard_core, g, recv, buf, layer, name):
    _, _, rh, c = g.shape
    n_recv = recv.shape[0]
    tr = _tile(rh, 512, 16)
    slot = (lambda sc: sc[0]) if layer is None else (lambda sc: layer)

    def body(sc_ref, a_ref, r_ref, buf_ref, out_ref):
        del sc_ref, buf_ref
        acc = a_ref[...].astype(F32)
        for k in range(n_recv):
            acc = acc + r_ref[k].astype(F32)
        out_ref[...] = acc

    return pl.pallas_call(
        body, name=name,
        grid_spec=pltpu.PrefetchScalarGridSpec(
            num_scalar_prefetch=1, grid=(rh // tr,),
            in_specs=[pl.BlockSpec((None, None, tr, c), lambda i, sc: (sc[0], sc[1], i, 0)),
                      pl.BlockSpec((n_recv, tr, c), lambda i, sc: (0, i, 0)),
                      pl.BlockSpec(memory_space=pl.ANY)],
            out_specs=pl.BlockSpec((None, None, tr, c), lambda i, sc: (slot(sc), sc[1], i, 0))),
        out_shape=jax.ShapeDtypeStruct(buf.shape, F32),
        input_output_aliases={3: 0},
        compiler_params=_cp("parallel"))(shard_core, g, recv, buf)


def adamw(w, g, m, v, name, emit_g=False):
    r, c = w.shape
    tr = _tile(r, 512, SUBLANES)
    c1 = 1.0 - ADAM_B1 ** ADAM_STEP
    c2 = 1.0 - ADAM_B2 ** ADAM_STEP
    n_out = 4 if emit_g else 3

    def body(w_ref, g_ref, m_ref, v_ref, d_ref, mo_ref, vo_ref, *go_ref):
        gv = g_ref[...]
        if emit_g:
            go_ref[0][...] = gv
        mn = ADAM_B1 * m_ref[...] + (1.0 - ADAM_B1) * gv
        vn = ADAM_B2 * v_ref[...] + (1.0 - ADAM_B2) * (gv * gv)
        m_hat = mn / c1
        v_hat = vn / c2
        d_ref[...] = -ADAM_LR * (m_hat / (jnp.sqrt(v_hat) + ADAM_EPS) + ADAM_WD * w_ref[...])
        mo_ref[...] = mn
        vo_ref[...] = vn

    blk = pl.BlockSpec((tr, c), lambda i: (i, 0))
    return pl.pallas_call(
        body, name=name, grid=(r // tr,), in_specs=[blk] * 4, out_specs=[blk] * n_out,
        out_shape=[jax.ShapeDtypeStruct((r, c), F32)] * n_out,
        compiler_params=_cp("parallel"))(w, g, m, v)


def _coords():
    return lax.axis_index("x"), lax.axis_index("y"), lax.axis_index("c")


def _exchange(name, ins, out_shapes, plan, in_place=False):
    n_in = len(ins)
    n_out = len(out_shapes)
    n_rem = len(plan([None] * n_in, [None] * n_out, True))

    def body(*refs):
        in_refs = refs[:n_in]
        out_refs = refs[n_in:n_in + n_out]
        send_sems, recv_sems = refs[n_in + n_out:]
        copies = [pltpu.make_async_remote_copy(
            src_ref=src, dst_ref=dst, send_sem=send_sems.at[q], recv_sem=recv_sems.at[q],
            device_id=peer, device_id_type=pl.DeviceIdType.MESH)
            for q, (src, dst, peer) in enumerate(plan(list(in_refs), list(out_refs), False))]
        for cp in copies:
            cp.start()
        for cp in copies:
            cp.wait_recv()
        for cp in copies:
            cp.wait_send()

    hbm = pl.BlockSpec(memory_space=pl.ANY)
    return pl.pallas_call(
        body, name=name, in_specs=[hbm] * n_in, out_specs=[hbm] * n_out, out_shape=out_shapes,
        input_output_aliases={i: i for i in range(n_in)} if in_place else {},
        scratch_shapes=[pltpu.SemaphoreType.DMA((n_rem,)), pltpu.SemaphoreType.DMA((n_rem,))],
        compiler_params=pltpu.CompilerParams(has_side_effects=True))(*ins)


def _split_start(name, groups, plan):
    flat = [a for grp in groups for a in grp]
    n, n_grp = len(flat), len(groups)
    counts = [len(plan(g, [None] * len(grp), True)) for g, grp in enumerate(groups)]

    def body(*refs):
        ins, sems, token = refs[:n], refs[n:n + 2 * n_grp], refs[-1]
        pos = 0
        for g, grp in enumerate(groups):
            arrs = list(ins[pos:pos + len(grp)])
            pos += len(grp)
            for q, (src, dst, peer) in enumerate(plan(g, arrs, False)):
                pltpu.make_async_remote_copy(
                    src_ref=src, dst_ref=dst, send_sem=sems[2 * g].at[q], recv_sem=sems[2 * g + 1].at[q],
                    device_id=peer, device_id_type=pl.DeviceIdType.MESH).start()
        token[...] = jnp.zeros_like(token)

    hbm = pl.BlockSpec(memory_space=pltpu.HBM)
    sem = pl.BlockSpec(memory_space=pltpu.SEMAPHORE)
    outs = pl.pallas_call(
        body, name=name,
        out_shape=[pltpu.SemaphoreType.DMA((cnt,)) for cnt in counts for _ in range(2)]
        + [pltpu.HBM(a.shape, a.dtype) for a in flat] + [jax.ShapeDtypeStruct((SUBLANES, LANES), F32)],
        in_specs=[hbm] * n, out_specs=[sem] * (2 * n_grp) + [hbm] * n + [pl.BlockSpec(memory_space=pltpu.VMEM)],
        input_output_aliases={i: 2 * n_grp + i for i in range(n)},
        compiler_params=pltpu.CompilerParams(has_side_effects=pltpu.SideEffectType.DATAFLOW_SIDE_EFFECTING),
    )(*[pltpu.with_memory_space_constraint(a, pltpu.HBM) for a in flat])
    started, pos = [], 2 * n_grp
    for g, grp in enumerate(groups):
        started.append((outs[2 * g], outs[2 * g + 1], list(outs[pos:pos + len(grp)])))
        pos += len(grp)
    return started, outs[-1]


def _split_wait(name, started, after, plan_g):
    send_sems, recv_sems, arrs = started
    n = len(arrs)

    def body(*refs):
        ins, ssem, rsem = list(refs[:n]), refs[n], refs[n + 1]
        for q, (src, dst, peer) in enumerate(plan_g(ins, False)):
            cp = pltpu.make_async_remote_copy(
                src_ref=src, dst_ref=dst, send_sem=ssem.at[q], recv_sem=rsem.at[q],
                device_id=peer, device_id_type=pl.DeviceIdType.MESH)
            cp.wait_send()
            cp.wait_recv()

    hbm = pl.BlockSpec(memory_space=pltpu.HBM)
    sem = pl.BlockSpec(memory_space=pltpu.SEMAPHORE)
    return pl.pallas_call(
        body, name=name, out_shape=[pltpu.HBM(a.shape, a.dtype) for a in arrs],
        in_specs=[hbm] * n + [sem, sem, pl.BlockSpec(memory_space=pl.ANY)], out_specs=[hbm] * n,
        input_output_aliases={i: i for i in range(n)},
        compiler_params=pltpu.CompilerParams(has_side_effects=pltpu.SideEffectType.DATAFLOW_SIDE_EFFECTING),
    )(*arrs, send_sems, recv_sems, after)


def _gather_ici_plan(arrs, count_only):
    if count_only:
        return [None] * (3 * len(arrs))
    x, y, c = _coords()
    pushes = []
    for a in arrs:
        mine = a.at[2 * x + y, c]
        pushes += [(mine, mine, peer) for peer, _ in _other_chips(x, y, c)]
    return pushes


def _all_to_all_plan(arrs, count_only):
    half = len(arrs) // 2
    if count_only:
        return [None] * (7 * half)
    x, y, c = _coords()
    pushes = []
    for src, land in zip(arrs[:half], arrs[half:]):
        for flips in range(1, 8):
            px, py, pc = (1 - x if flips & 4 else x), (1 - y if flips & 2 else y), (1 - c if flips & 1 else c)
            pushes.append((src.at[2 * px + py, pc], land.at[flips - 1], (px, py, pc)))
    return pushes


def _forward_plan(arrs, count_only):
    if count_only:
        return [None] * (3 * len(arrs))
    x, y, c = _coords()
    pushes = []
    for a in arrs:
        for _, src_shard in _other_chips(x, y, c):
            slab = a.at[src_shard, c]
            pushes.append((slab, slab, (x, y, 1 - c)))
    return pushes


def forward_to_sibling(bufs, name):
    shapes = [jax.ShapeDtypeStruct(b.shape, b.dtype) for b in bufs]
    return _exchange(name, bufs, shapes, lambda ins, outs, cnt: _forward_plan(outs, cnt), in_place=True)


def _other_chips(x, y, c):
    return [((1 - x, y, c), 2 * (1 - x) + y), ((x, 1 - y, c), 2 * x + 1 - y),
            ((1 - x, 1 - y, c), 2 * (1 - x) + 1 - y)]


def join_halves(bufs, everywhere, name):
    slots = [(i, l) for i, b in enumerate(bufs) for l in range(b.shape[0])]
    n = len(bufs)

    def plan(ins, outs, count_only):
        if count_only:
            return [None] * (len(slots) + 7)
        x, y, c = _coords()
        pushes = [(outs[i].at[l, c], outs[i].at[l, c], (x, y, 1 - c)) for i, l in slots]
        mine = outs[n].at[2 * x + y, c]
        for flips in range(1, 8):
            peer = (1 - x if flips & 4 else x, 1 - y if flips & 2 else y, 1 - c if flips & 1 else c)
            pushes.append((mine, mine, peer))
        return pushes

    arrs = list(bufs) + [everywhere]
    shapes = [jax.ShapeDtypeStruct(b.shape, b.dtype) for b in arrs]
    return _exchange(name, arrs, shapes, plan, in_place=True)


def _pack(arrays, multiple):
    flat = jnp.concatenate([a.reshape(-1) for a in arrays])
    n = flat.shape[0]
    return jnp.pad(flat, (0, _round_up(n, multiple) - n))


def _unpack(flat, shapes):
    out, pos = [], 0
    for shp in shapes:
        n = math.prod(shp)
        out.append(flat[pos:pos + n].reshape(shp))
        pos += n
    return out


def _block_diag(w, per_group):
    nb, bs, _ = w.shape
    g = nb // per_group
    w4 = w.reshape(g, per_group, bs, bs)
    eye = jnp.eye(per_group, dtype=w.dtype)
    full = w4[:, :, :, None, :] * eye[None, :, None, :, None]
    return full.reshape(g, per_group * bs, per_group * bs).astype(BF16)


def _block_diag_extract(full, per_group, bs):
    g = full.shape[0]
    f5 = full.reshape(g, per_group, bs, per_group, bs)
    idx = jnp.arange(per_group)
    picked = f5[:, idx, :, idx, :]
    return jnp.moveaxis(picked, 0, 1).reshape(g * per_group, bs, bs)


def kernel(x, meta, a_w_in, a_conv_w, a_conv_b, a_w_r, a_b_r, a_w_i, a_b_i, a_lambda, a_w_out, kv_w, kv_f_b, b_w_in, b_w_out, f_w_in, f_conv_w, f_conv_b, f_w_out, ln1_g, ln1_b, ln2_g, ln2_b, loss_target, m_meta, m_a_w_in, m_a_conv_w, m_a_conv_b, m_a_w_r, m_a_b_r, m_a_w_i, m_a_b_i, m_a_lambda, m_a_w_out, m_kv_w, m_kv_f_b, m_b_w_in, m_b_w_out, m_f_w_in, m_f_conv_w, m_f_conv_b, m_f_w_out, m_ln1_g, m_ln1_b, m_ln2_g, m_ln2_b, v_meta, v_a_w_in, v_a_conv_w, v_a_conv_b, v_a_w_r, v_a_b_r, v_a_w_i, v_a_b_i, v_a_lambda, v_a_w_out, v_kv_w, v_kv_f_b, v_b_w_in, v_b_w_out, v_f_w_in, v_f_conv_w, v_f_conv_b, v_f_w_out, v_ln1_g, v_ln1_b, v_ln2_g, v_ln2_b):
    weights = dict(meta=meta, a_w_in=a_w_in, a_conv_w=a_conv_w, a_conv_b=a_conv_b, a_w_r=a_w_r, a_b_r=a_b_r,
                   a_w_i=a_w_i, a_b_i=a_b_i, a_lambda=a_lambda, a_w_out=a_w_out, kv_w=kv_w, kv_f_b=kv_f_b,
                   b_w_in=b_w_in, b_w_out=b_w_out, f_w_in=f_w_in, f_conv_w=f_conv_w, f_conv_b=f_conv_b,
                   f_w_out=f_w_out, ln1_g=ln1_g, ln1_b=ln1_b, ln2_g=ln2_g, ln2_b=ln2_b)
    mom_m = dict(meta=m_meta, a_w_in=m_a_w_in, a_conv_w=m_a_conv_w, a_conv_b=m_a_conv_b, a_w_r=m_a_w_r,
                 a_b_r=m_a_b_r, a_w_i=m_a_w_i, a_b_i=m_a_b_i, a_lambda=m_a_lambda, a_w_out=m_a_w_out,
                 kv_w=m_kv_w, kv_f_b=m_kv_f_b, b_w_in=m_b_w_in, b_w_out=m_b_w_out, f_w_in=m_f_w_in,
                 f_conv_w=m_f_conv_w, f_conv_b=m_f_conv_b, f_w_out=m_f_w_out, ln1_g=m_ln1_g, ln1_b=m_ln1_b,
                 ln2_g=m_ln2_g, ln2_b=m_ln2_b)
    mom_v = dict(meta=v_meta, a_w_in=v_a_w_in, a_conv_w=v_a_conv_w, a_conv_b=v_a_conv_b, a_w_r=v_a_w_r,
                 a_b_r=v_a_b_r, a_w_i=v_a_w_i, a_b_i=v_a_b_i, a_lambda=v_a_lambda, a_w_out=v_a_w_out,
                 kv_w=v_kv_w, kv_f_b=v_kv_f_b, b_w_in=v_b_w_in, b_w_out=v_b_w_out, f_w_in=v_f_w_in,
                 f_conv_w=v_f_conv_w, f_conv_b=v_f_conv_b, f_w_out=v_f_w_out, ln1_g=v_ln1_g, ln1_b=v_ln1_b,
                 ln2_g=v_ln2_g, ln2_b=v_ln2_b)
    return _train_step(x, loss_target, weights, mom_m, mom_v)


WEIGHT_ORDER = ("meta", "a_w_in", "a_conv_w", "a_conv_b", "a_w_r", "a_b_r", "a_w_i", "a_b_i", "a_lambda",
                "a_w_out", "kv_w", "kv_f_b", "b_w_in", "b_w_out", "f_w_in", "f_conv_w", "f_conv_b",
                "f_w_out", "ln1_g", "ln1_b", "ln2_g", "ln2_b")
BIG = ("a_w_in", "a_w_out", "kv_w", "b_w_in", "b_w_out", "f_w_in", "f_w_out")
OUT_TYPE = ("a_w_out", "b_w_out", "f_w_out")
SMALL_SHARDED = (("meta", 1), ("a_conv_w", 2), ("a_conv_b", 1), ("a_b_r", 1), ("a_b_i", 1), ("a_lambda", 1),
                 ("f_conv_w", 2))
SMALL_REPLICATED = ("a_w_r", "a_w_i", "kv_f_b", "f_conv_b", "ln1_g", "ln1_b", "ln2_g", "ln2_b")


def _train_step(x, loss_target, weights, mom_m, mom_v):
    S = N_SHARDS
    seq, d = x.shape[1], x.shape[2]
    nm = weights["meta"].shape[0]
    la = weights["a_w_in"].shape[0]
    lb = weights["b_w_in"].shape[0]
    depth = la + lb
    dr = weights["a_w_out"].shape[1] * S
    nb, bs = weights["a_w_r"].shape[1], weights["a_w_r"].shape[2]
    per_group = (LANES // math.gcd(bs, LANES))
    gs = per_group * bs
    heads = weights["kv_f_b"].shape[0]
    dff = weights["f_w_out"].shape[1] * S
    nkv = 2 * d + heads
    nkv_s = weights["kv_w"].shape[1]
    nkvp = _round_up(2 * d + LANES, 768) if 2 * d + LANES > 768 else 2 * d + LANES
    alpha = (2 * depth) ** 0.25
    xi, yi, ci = _coords()
    shard = 2 * xi + yi
    shard_arr = jnp.reshape(shard, (1,)).astype(jnp.int32)
    shard_core_arr = jnp.stack([shard, ci]).astype(jnp.int32)

    def mixer_keys(l):
        if l < la:
            return [("a_w_in", l), ("a_w_out", l)]
        return ([("kv_w", 0)] if l == la else []) + [("b_w_in", l - la), ("b_w_out", l - la)]

    def ffn_keys(l):
        return [("f_w_in", l), ("f_w_out", l)]

    assert la >= 1
    groups = [mixer_keys(0)[:1], mixer_keys(0)[1:] + ffn_keys(0)]
    groups += [mixer_keys(l) + ffn_keys(l) for l in range(1, depth)]
    keys = [kl for grp in groups for kl in grp]
    shape2d = {(k, i): weights[k].shape[-2:] for k, i in keys}
    small_local = [weights[k] for k, _ in SMALL_SHARDED]
    sm_flat = _pack(small_local, 2 * SUBLANES * LANES).reshape(1, 2, -1, LANES)
    sm_slot = lax.dynamic_update_slice_in_dim(lax.empty((S,) + sm_flat.shape[1:], F32), sm_flat, shard, axis=0)
    parts = [[cast_into_slot(shard_arr, weights[k], i, f"cast_{k}{i}") for k, i in grp] for grp in groups]
    parts[0].append(sm_slot)
    in_flight, start_token = _split_start("gather_start", parts, lambda g, refs, cnt: _gather_ici_plan(refs, cnt))
    gw = {}

    def use(g, arrs):
        for kl, a in zip(groups[g], arrs):
            rows, cols = shape2d[kl]
            gw[kl] = a.reshape(S * rows, cols) if kl[0] in OUT_TYPE else a.reshape(S, rows, cols)
        return arrs

    def fetch(g, after):
        arrs = _split_wait(f"gather_wait_{g}", in_flight[g], after, _gather_ici_plan)
        return use(g, forward_to_sibling(arrs, f"gather_fwd_{g}"))

    forwarding = {}

    def fetch_early(g, after):
        arrs = _split_wait(f"gather_wait_{g}", in_flight[g], after, _gather_ici_plan)
        started, token = _split_start(f"gather_fwd_start_{g}", [arrs], lambda _, refs, cnt: _forward_plan(refs, cnt))
        forwarding[g] = started[0]
        return token[0:1, 0:1]

    def fetch_end(g, after):
        return use(g, _split_wait(f"gather_fwd_wait_{g}", forwarding[g], after, _forward_plan))

    sm_all = fetch(0, start_token)[-1].reshape(S, -1)
    small_full = {}
    per_shard = [_unpack(sm_all[s], [a.shape for a in small_local]) for s in range(S)]
    for idx, (k, axis) in enumerate(SMALL_SHARDED):
        small_full[k] = jnp.concatenate([per_shard[s][idx] for s in range(S)], axis=axis)
    fb_pad = jnp.pad(weights["kv_f_b"], (0, LANES - heads)).reshape(1, LANES)
    wr_g = [_block_diag(weights["a_w_r"][l], per_group) for l in range(la)]
    wi_g = [_block_diag(weights["a_w_i"][l], per_group) for l in range(la)]
    row = lambda v: v.reshape(1, -1)

    h, hb = embed_fwd(small_full["meta"], x[0], "embed")
    saved = []
    kvz = ct_pad = None
    _, tp, _ = _attn_geometry(nm + seq)
    t = nm + seq
    for l in range(depth):
        sv = {"hb_in": hb}
        if l > 0:
            (fetch_end if l + 1 in forwarding else fetch)(l + 1, hb)
        if l < la:
            gr = mm_in(hb, gw[("a_w_in", l)], F32, f"a{l}_in")
            rc, rcb = a_conv_fwd(gr, small_full["a_conv_w"][l], row(small_full["a_conv_b"][l]), f"a{l}_conv")
            r_pre, i_pre = mm_bd(rcb, wr_g[l], wi_g[l], f"a{l}_gates")
            hs, gb = a_elem_fwd(gr, rc, r_pre, i_pre, row(small_full["a_b_r"][l]), row(small_full["a_b_i"][l]),
                                row(small_full["a_lambda"][l]), f"a{l}_lru")
            if l == 0:
                fetch(1, gb)
            mixed, w_mix = gb, gw[("a_w_out", l)]
            sv.update(gr=gr, rc=rc, rcb=rcb, r_pre=r_pre, i_pre=i_pre, hs=hs, gb=gb)
        else:
            j = l - la
            if j == 0:
                kv_cat = jnp.moveaxis(gw[("kv_w", 0)], 0, 1).reshape(d, S * nkv_s)
                kv_pad = jnp.pad(kv_cat, ((0, 0), (0, nkvp - nkv))).reshape(1, d, nkvp)
                kvz = mm_in(hb, kv_pad, F32, "kv_proj")
                cum = kv_fwd(kvz, fb_pad, d, "kv_forget")
                ct_pad = jnp.pad(cum[:, :heads].T, ((0, 0), (0, tp - t))).reshape(heads, 1, tp)
                kv_hb = hb
            qg = mm_in(hb, gw[("b_w_in", j)], F32, f"b{j}_in")
            o, mob = attn_fwd(qg, kvz, ct_pad, d, heads, f"b{j}_attn")
            mixed, w_mix = mob, gw[("b_w_out", j)]
            sv.update(qg=qg, o=o, mob=mob)
        h1, h1b, xh1, rs1 = mm_out_ln(mixed, w_mix, h, row(weights["ln1_g"][l]), row(weights["ln1_b"][l]), alpha,
                                      f"mix{l}_out_ln1")
        f_bias = row(weights["f_conv_b"][l])
        if l >= 1 and l + 2 < len(groups):
            f_bias = f_bias + fetch_early(l + 2, h1b)
        zf = mm_in(h1b, gw[("f_w_in", l)], F32, f"f{l}_in")
        ffb = f_elem_fwd(zf, small_full["f_conv_w"][l], f_bias, f"f{l}_act")
        h2, h2b, xh2, rs2 = mm_out_ln(ffb, gw[("f_w_out", l)], h1, row(weights["ln2_g"][l]), row(weights["ln2_b"][l]),
                                      alpha, f"f{l}_out_ln2")
        sv.update(h1b=h1b, xh1=xh1, rs1=rs1, zf=zf, ffb=ffb, xh2=xh2, rs2=rs2)
        saved.append(sv)
        h, hb = h2, h2b
    loss11, dy = loss_fwd_bwd(h, loss_target[0], nm, "loss")

    grads = {}

    def by_owner(kl, g3):
        rows, cols = shape2d[kl]
        grads[kl] = g3.reshape(S, 2, rows // 2, cols)

    reducing = []

    def send_grads(g, names, arrays):
        lands = [lax.empty((7,) + a.shape[2:], a.dtype) for a in arrays]
        started, token = _split_start(f"grad_a2a_start_{g}", [list(arrays) + lands],
                                      lambda _, refs, cnt: _all_to_all_plan(refs, cnt))
        reducing.append((names, started[0]))
        return token[0:1, 0:1]

    def after_start(vec, zero):
        return vec if zero is None else vec + zero

    pin = None

    g_small = {}
    per_layer = {k: [None] * n for k, n in (
        ("a_conv_w", la), ("a_conv_b", la), ("a_w_r", la), ("a_b_r", la), ("a_w_i", la), ("a_b_i", la),
        ("a_lambda", la), ("f_conv_w", depth), ("f_conv_b", depth), ("ln1_g", depth), ("ln1_b", depth),
        ("ln2_g", depth), ("ln2_b", depth))}
    adds = [(dy, 1.0)]
    dks, dvs, dcs = [], [], []
    for l in reversed(range(depth)):
        sv = saved[l]
        ds2, ds2b, dg2, db2 = ln_bwd(adds, sv["xh2"], sv["rs2"], after_start(row(weights["ln2_g"][l]), pin),
                                     f"ln2_{l}_bwd")
        pin = None
        per_layer["ln2_g"][l], per_layer["ln2_b"][l] = dg2[0], db2[0]
        dff_v = mm_out_nt(ds2b, gw[("f_w_out", l)], f"f{l}_out_dx")
        by_owner(("f_w_out", l), mm_tn(sv["ffb"], [ds2b], 1, f"f{l}_out_dw"))
        dzg, dzv, dwg, dwv, dbg, dbv = f_elem_bwd(sv["zf"], dff_v, small_full["f_conv_w"][l],
                                                  row(weights["f_conv_b"][l]), f"f{l}_act_bwd")
        per_layer["f_conv_w"][l] = jnp.concatenate([dwg, dwv], axis=1)
        per_layer["f_conv_b"][l] = jnp.concatenate([dbg, dbv], axis=1)[0]
        dh1_f = mm_in_nt([dzg, dzv], gw[("f_w_in", l)], f"f{l}_in_dx")
        by_owner(("f_w_in", l), mm_tn(sv["h1b"], [dzg, dzv], S, f"f{l}_in_dw"))
        w_mix = gw[("a_w_out", l)] if l < la else gw[("b_w_out", l - la)]
        ds1, ds1b, dg1, db1, d_mixed = ln_bwd([(ds2, alpha), (dh1_f, 1.0)], sv["xh1"], sv["rs1"],
                                              row(weights["ln1_g"][l]), f"ln1_{l}_bwd_mix_dx", w=w_mix)
        per_layer["ln1_g"][l], per_layer["ln1_b"][l] = dg1[0], db1[0]
        if l < la:
            dgv = d_mixed
            by_owner(("a_w_out", l), mm_tn(sv["gb"], [ds1b], 1, f"a{l}_out_dw"))
            if l == 0:
                pin = send_grads(1, groups[1], [grads[kl] for kl in groups[1]])
            dgate_b, drp_b, dip_b, drc_d, dlam, dbr, dbi = a_elem_bwd(
                dgv, sv["gr"], sv["rc"], sv["r_pre"], sv["i_pre"], sv["hs"], row(small_full["a_b_r"][l]),
                row(small_full["a_b_i"][l]), after_start(row(small_full["a_lambda"][l]), pin), f"a{l}_lru_bwd")
            pin = None
            drc_g = mm_bd_nt(drp_b, dip_b, wr_g[l], wi_g[l], f"a{l}_gates_dx")
            dwr_g, dwi_g = mm_bd_tn(sv["rcb"], drp_b, dip_b, gs, f"a{l}_gates_dw")
            dgr_b, dcw, dcb = a_conv_bwd(drc_d, drc_g, sv["gr"], small_full["a_conv_w"][l], dgate_b,
                                         f"a{l}_conv_bwd")
            per_layer["a_w_r"][l] = _block_diag_extract(dwr_g, per_group, bs)
            per_layer["a_w_i"][l] = _block_diag_extract(dwi_g, per_group, bs)
            per_layer["a_lambda"][l], per_layer["a_b_r"][l], per_layer["a_b_i"][l] = dlam[0], dbr[0], dbi[0]
            per_layer["a_conv_w"][l], per_layer["a_conv_b"][l] = dcw, dcb[0]
            by_owner(("a_w_in", l), mm_tn(sv["hb_in"], [dgr_b], S, f"a{l}_in_dw"))
            gate_w = gw[("a_w_in", l)]
            if l == 0:
                zero = send_grads(0, groups[0], [grads[kl] for kl in groups[0]])
                gate_w = gate_w + zero.astype(BF16)[None]
            dh_m = mm_in_nt([dgr_b], gate_w, f"a{l}_in_dx")
        else:
            j = l - la
            dmo = d_mixed
            by_owner(("b_w_out", j), mm_tn(sv["mob"], [ds1b], 1, f"b{j}_out_dw"))
            dq_b, dog_b, dk, dv, dct = attn_bwd(dmo, sv["qg"], kvz, sv["o"], ct_pad, d, heads, f"b{j}_attn_bwd")
            dks.append(dk)
            dvs.append(dv)
            dcs.append(jnp.pad(dct[:, 0, :t].T, ((0, 0), (0, LANES - heads))))
            dh_m = mm_in_nt([dq_b, dog_b], gw[("b_w_in", j)], f"b{j}_in_dx")
            by_owner(("b_w_in", j), mm_tn(sv["hb_in"], [dq_b, dog_b], S, f"b{j}_in_dw"))
        adds = [(ds1, alpha), (dh_m, 1.0)]
        if l == la:
            dzf_b, dfb = kv_bwd(dcs, kvz, fb_pad, d, "kv_forget_bwd")
            dk_b = add_cast(dks[0], dks[1], "kv_dk") if lb == 2 else None
            dv_b = add_cast(dvs[0], dvs[1], "kv_dv") if lb == 2 else None
            dz_kv = jnp.concatenate([dk_b, dv_b, dzf_b, jnp.zeros((t, nkvp - 2 * d - LANES), BF16)], axis=1)
            dh_kv = mm_in_nt([dz_kv], kv_pad, "kv_proj_dx")
            kv_dw = mm_tn(kv_hb, [dz_kv], 1, "kv_proj_dw")
            by_owner(("kv_w", 0), jnp.moveaxis(kv_dw[0, :, :nkv].reshape(d, S, nkv_s), 1, 0))
            g_small["kv_f_b"] = dfb[0, :heads]
            adds.append((dh_kv, 1.0))
        if l > 0:
            pin = send_grads(l + 1, groups[l + 1], [grads[kl] for kl in groups[l + 1]])
    g_meta, g_x = embed_bwd(adds, nm, "embed_bwd")

    small_names = list(SMALL_REPLICATED) + [k for k, _ in SMALL_SHARDED]
    g_small["meta"] = g_meta
    for k, vals in per_layer.items():
        g_small[k] = jnp.stack(vals)
    small_shapes = {k: (weights[k].shape if k in SMALL_REPLICATED else g_small[k].shape) for k in small_names}
    sm_g = _pack([g_small[k].reshape(small_shapes[k]) for k in small_names] + [loss11.reshape(1)],
                 S * 2 * SUBLANES * LANES)
    sm_g = sm_g.reshape(S, 2, -1, LANES)
    send_grads(len(groups), [("small", 0)], [sm_g])

    fin = {"small": lax.empty(sm_g.shape, F32)}
    for kl in keys:
        n_stack = weights[kl[0]].shape[0] if weights[kl[0]].ndim == 3 else 1
        rows, cols = shape2d[kl]
        fin.setdefault(kl[0], lax.empty((n_stack, 2, rows // 2, cols), F32))
    for g, (names_g, started) in enumerate(reducing):
        arrs = _split_wait(f"grad_a2a_wait_{g}", started, g_x, _all_to_all_plan)
        half = len(names_g)
        for i, (kl, cs, rv) in enumerate(zip(names_g, arrs[:half], arrs[half:])):
            fin[kl[0]] = owner_sum(shard_core_arr, cs, rv, fin[kl[0]], None if kl[0] == "small" else kl[1],
                                   f"owner_sum_{g}_{i}")
    names = list(BIG) + ["small"]
    joined = dict(zip(names, join_halves([fin[k] for k in BIG], fin["small"], "grad_join")))
    sm_red = joined["small"].reshape(-1)

    out_g, out_d, out_m, out_v = {}, {}, {}, {}
    for k in BIG:
        w2 = weights[k].reshape(-1, weights[k].shape[-1])
        g2 = joined[k].reshape(w2.shape)
        dlt, mn, vn, g_out = adamw(w2, g2, mom_m[k].reshape(w2.shape), mom_v[k].reshape(w2.shape), "adamw_" + k,
                                   emit_g=True)
        shp = weights[k].shape
        out_g[k], out_d[k], out_m[k], out_v[k] = g_out.reshape(shp), dlt.reshape(shp), mn.reshape(shp), vn.reshape(shp)
    *small_sums, loss_sum = _unpack(sm_red, [small_shapes[k] for k in small_names] + [(1,)])
    sm_vals = dict(zip(small_names, small_sums))
    local_small = {}
    for k in SMALL_REPLICATED:
        local_small[k] = sm_vals[k]
    for k, axis in SMALL_SHARDED:
        size = weights[k].shape[axis]
        local_small[k] = lax.dynamic_slice_in_dim(sm_vals[k], shard * size, size, axis=axis)
    for k in small_names:
        shp = weights[k].shape
        two_d = (-1, shp[-1]) if len(shp) > 1 else (1, -1)
        dlt, mn, vn = adamw(weights[k].reshape(two_d), local_small[k].reshape(two_d), mom_m[k].reshape(two_d),
                            mom_v[k].reshape(two_d), "adamw_" + k)
        out_g[k], out_d[k], out_m[k], out_v[k] = local_small[k], dlt.reshape(shp), mn.reshape(shp), vn.reshape(shp)

    return (loss_sum[0], g_x[None], *[out_g[k] for k in WEIGHT_ORDER], *[out_d[k] for k in WEIGHT_ORDER],
            *[out_m[k] for k in WEIGHT_ORDER], *[out_v[k] for k in WEIGHT_ORDER])
```

```python
import functools
import math

import jax
import jax.numpy as jnp
from jax import lax
from jax.experimental import pallas as pl
from jax.experimental.pallas import tpu as pltpu

F32 = jnp.float32
BF16 = jnp.bfloat16

LRU_C = 8.0
LN_EPS = 1e-5
ADAM_LR = 0.001
ADAM_B1 = 0.9
ADAM_B2 = 0.999
ADAM_EPS = 1e-08
ADAM_WD = 0.01
ADAM_STEP = 10

LANES = 128
SUBLANES = 8
V7X_VMEM_BYTES = 64 * 1024 * 1024
VMEM_LIMIT = V7X_VMEM_BYTES * 7 // 8
N_SHARDS = 4
GELU_C0 = math.sqrt(2.0 / math.pi)
GELU_C1 = 0.044715
NEG_BIG = -1e30


def _cp(*sem):
    return pltpu.CompilerParams(dimension_semantics=tuple(sem), vmem_limit_bytes=VMEM_LIMIT)


def _tile(n, cap, mult=LANES):
    best = None
    d = mult
    while d <= min(n, cap):
        if n % d == 0:
            best = d
        d += mult
    return n if best is None else best


def _row_block(t):
    if t % 3 == 0 and (t // 3) % 16 == 0:
        return t // 3
    return t


def _round_up(n, m):
    return (n + m - 1) // m * m


def _sigmoid(v):
    return 1.0 / (1.0 + jnp.exp(-v))


def _softplus(v):
    return jnp.maximum(v, 0.0) + jnp.log(1.0 + jnp.exp(-jnp.abs(v)))


def _gelu_parts(v):
    v2 = v * v
    u = GELU_C0 * (v + GELU_C1 * v * v2)
    t = jnp.tanh(u)
    g = 0.5 * v * (1.0 + t)
    dg = 0.5 * (1.0 + t) + 0.5 * v * (1.0 - t * t) * (GELU_C0 * (1.0 + 3.0 * GELU_C1 * v2))
    return g, dg


def _gelu(v):
    u = GELU_C0 * (v + GELU_C1 * v * v * v)
    return 0.5 * v * (1.0 + jnp.tanh(u))


def _neg_expm1(v):
    series = -v * (1.0 + 0.5 * v * (1.0 + (v / 3.0) * (1.0 + 0.25 * v)))
    return jnp.where(v > -0.05, series, 1.0 - jnp.exp(v))


def _shift_down(v, j):
    if j == 0:
        return v
    rows = lax.broadcasted_iota(jnp.int32, v.shape, 0)
    return jnp.where(rows >= j, pltpu.roll(v, j, 0), 0.0)


def _shift_up(v, j):
    if j == 0:
        return v
    n = v.shape[0]
    rows = lax.broadcasted_iota(jnp.int32, v.shape, 0)
    return jnp.where(rows < n - j, pltpu.roll(v, n - j, 0), 0.0)


def _scan_rows(a_ref, b_ref, out_ref, n_rows, width, reverse):
    n_groups = n_rows // SUBLANES
    rows = lax.broadcasted_iota(jnp.int32, (SUBLANES, width), 0)
    edge = 0 if reverse else SUBLANES - 1

    def body(g, carry):
        grp = (n_groups - 1 - g) if reverse else g
        off = pl.multiple_of(grp * SUBLANES, SUBLANES)
        b = b_ref[pl.ds(off, SUBLANES), :]
        a = None if a_ref is None else a_ref[pl.ds(off, SUBLANES), :]
        for d in (1, 2, 4):
            if reverse:
                keep = rows < SUBLANES - d
                sh = SUBLANES - d
            else:
                keep = rows >= d
                sh = d
            b_s = jnp.where(keep, pltpu.roll(b, sh, 0), 0.0)
            if a is None:
                b = b + b_s
            else:
                a_s = jnp.where(keep, pltpu.roll(a, sh, 0), 1.0)
                b = a * b_s + b
                a = a * a_s
        h = b + carry if a is None else b + a * carry
        out_ref[pl.ds(off, SUBLANES), :] = h
        return jnp.sum(jnp.where(rows == edge, h, 0.0), axis=0, keepdims=True)

    lax.fori_loop(0, n_groups, body, jnp.zeros((1, width), F32), unroll=2)


def mm_in(x, w, out_dtype, name):
    t, k = x.shape
    s_n, _, ns = w.shape
    tn = _tile(ns, 1408)
    nj = ns // tn
    rb = _row_block(t)

    def body(x_ref, w_ref, o_ref):
        o_ref[...] = jnp.dot(x_ref[...], w_ref[...], preferred_element_type=F32).astype(o_ref.dtype)

    return pl.pallas_call(
        body, name=name, grid=(s_n, nj, t // rb),
        in_specs=[pl.BlockSpec((rb, k), lambda s, j, r: (r, 0)),
                  pl.BlockSpec((None, k, tn), lambda s, j, r: (s, 0, j))],
        out_specs=pl.BlockSpec((rb, tn), lambda s, j, r: (r, s * nj + j)),
        out_shape=jax.ShapeDtypeStruct((t, s_n * ns), out_dtype),
        compiler_params=_cp("parallel", "parallel", "parallel"))(x, w)


def _part_map(p, per_part, nj, lead):
    def index(*grid):
        s, j = grid[-2], grid[-1]
        mine = s // per_part == p
        col = jnp.where(mine, (s - p * per_part) * nj + j, 0)
        return (grid[0], col) if lead else (0, col)
    return index


def mm_in_nt(dy_parts, w, name):
    n_parts = len(dy_parts)
    t = dy_parts[0].shape[0]
    s_n, k, ns = w.shape
    per_part = s_n // n_parts
    tn = _tile(ns, 1408)
    nj = ns // tn
    rb = _row_block(t)

    def body(*refs):
        w_ref, o_ref = refs[n_parts:]

        @pl.when((pl.program_id(1) == 0) & (pl.program_id(2) == 0))
        def _():
            o_ref[...] = jnp.zeros_like(o_ref)

        for p in range(n_parts):
            @pl.when(pl.program_id(1) // per_part == p)
            def _():
                o_ref[...] += lax.dot_general(refs[p][...], w_ref[...], (((1,), (1,)), ((), ())),
                                              preferred_element_type=F32)

    return pl.pallas_call(
        body, name=name, grid=(t // rb, s_n, nj),
        in_specs=[pl.BlockSpec((rb, tn), _part_map(p, per_part, nj, True)) for p in range(n_parts)]
        + [pl.BlockSpec((None, k, tn), lambda r, s, j: (s, 0, j))],
        out_specs=pl.BlockSpec((rb, k), lambda r, s, j: (r, 0)),
        out_shape=jax.ShapeDtypeStruct((t, k), F32),
        compiler_params=_cp("parallel", "arbitrary", "arbitrary"))(*dy_parts, w)


def mm_out_nt(dy, w, name):
    t, n = dy.shape
    k = w.shape[0]
    rb = _row_block(t)

    def body(dy_ref, w_ref, o_ref):
        o_ref[...] = lax.dot_general(dy_ref[...], w_ref[...], (((1,), (1,)), ((), ())),
                                     preferred_element_type=F32)

    return pl.pallas_call(
        body, name=name, grid=(t // rb,),
        in_specs=[pl.BlockSpec((rb, n), lambda r: (r, 0)), pl.BlockSpec((k, n), lambda r: (0, 0))],
        out_specs=pl.BlockSpec((rb, k), lambda r: (r, 0)),
        out_shape=jax.ShapeDtypeStruct((t, k), F32),
        compiler_params=_cp("parallel"))(dy, w)


def mm_tn(x, dy_parts, s_n, name):
    n_parts = len(dy_parts)
    t, kb = x.shape
    nb = dy_parts[0].shape[1] * n_parts // s_n
    per_part = max(s_n // n_parts, 1)
    tk = _tile(kb, 1408)
    tn = _tile(nb, 1408)
    nkb, nnb = kb // tk, nb // tn
    tn_dims = (((0,), (0,)), ((), ()))

    def body(*refs):
        x_ref, o_ref = refs[0], refs[-1]
        for p in range(n_parts):
            @pl.when(pl.program_id(1) // per_part == p)
            def _():
                o_ref[...] = lax.dot_general(x_ref[...], refs[1 + p][...], tn_dims,
                                             preferred_element_type=F32).astype(o_ref.dtype)

    return pl.pallas_call(
        body, name=name, grid=(nkb, s_n, nnb),
        in_specs=[pl.BlockSpec((t, tk), lambda a, s, b: (0, a))]
        + [pl.BlockSpec((t, tn), _part_map(p, per_part, nnb, False)) for p in range(n_parts)],
        out_specs=pl.BlockSpec((None, tk, tn), lambda a, s, b: (s, a, b)),
        out_shape=jax.ShapeDtypeStruct((s_n, kb, nb), BF16),
        compiler_params=_cp("parallel", "parallel", "parallel"))(x, *dy_parts)


def mm_bd(x, wr, wi, name):
    t, _ = x.shape
    g_n, gs, _ = wr.shape
    rb = _row_block(t)

    def body(x_ref, wr_ref, wi_ref, r_ref, i_ref):
        xv = x_ref[...]
        r_ref[...] = jnp.dot(xv, wr_ref[...], preferred_element_type=F32)
        i_ref[...] = jnp.dot(xv, wi_ref[...], preferred_element_type=F32)

    blk = pl.BlockSpec((rb, gs), lambda g, r: (r, g))
    wspec = pl.BlockSpec((None, gs, gs), lambda g, r: (g, 0, 0))
    return pl.pallas_call(
        body, name=name, grid=(g_n, t // rb), in_specs=[blk, wspec, wspec], out_specs=[blk, blk],
        out_shape=[jax.ShapeDtypeStruct((t, g_n * gs), F32)] * 2,
        compiler_params=_cp("parallel", "parallel"))(x, wr, wi)


def mm_bd_nt(dr, di, wr, wi, name):
    t, _ = dr.shape
    g_n, gs, _ = wr.shape
    rb = _row_block(t)
    nt = (((1,), (1,)), ((), ()))

    def body(dr_ref, di_ref, wr_ref, wi_ref, o_ref):
        o_ref[...] = (lax.dot_general(dr_ref[...], wr_ref[...], nt, preferred_element_type=F32)
                      + lax.dot_general(di_ref[...], wi_ref[...], nt, preferred_element_type=F32))

    blk = pl.BlockSpec((rb, gs), lambda g, r: (r, g))
    wspec = pl.BlockSpec((None, gs, gs), lambda g, r: (g, 0, 0))
    return pl.pallas_call(
        body, name=name, grid=(g_n, t // rb), in_specs=[blk, blk, wspec, wspec], out_specs=blk,
        out_shape=jax.ShapeDtypeStruct((t, g_n * gs), F32),
        compiler_params=_cp("parallel", "parallel"))(dr, di, wr, wi)


def mm_bd_tn(x, dr, di, gs, name):
    t, w = x.shape
    g_n = w // gs
    tn_dims = (((0,), (0,)), ((), ()))

    def body(x_ref, dr_ref, di_ref, gr_ref, gi_ref):
        xv = x_ref[...]
        gr_ref[...] = lax.dot_general(xv, dr_ref[...], tn_dims, preferred_element_type=F32)
        gi_ref[...] = lax.dot_general(xv, di_ref[...], tn_dims, preferred_element_type=F32)

    blk = pl.BlockSpec((t, gs), lambda g: (0, g))
    ospec = pl.BlockSpec((None, gs, gs), lambda g: (g, 0, 0))
    return pl.pallas_call(
        body, name=name, grid=(g_n,), in_specs=[blk, blk, blk], out_specs=[ospec, ospec],
        out_shape=[jax.ShapeDtypeStruct((g_n, gs, gs), F32)] * 2,
        compiler_params=_cp("parallel"))(x, dr, di)


def embed_fwd(meta, x2d, name):
    nm, d = meta.shape
    seq = x2d.shape[0]
    t = nm + seq
    cb = _tile(d, 256)

    def body(m_ref, x_ref, h_ref, hb_ref):
        h_ref[pl.ds(0, nm), :] = m_ref[...]
        h_ref[pl.ds(nm, seq), :] = x_ref[...]
        hb_ref[pl.ds(0, nm), :] = m_ref[...].astype(BF16)
        hb_ref[pl.ds(nm, seq), :] = x_ref[...].astype(BF16)

    return pl.pallas_call(
        body, name=name, grid=(d // cb,),
        in_specs=[pl.BlockSpec((nm, cb), lambda j: (0, j)), pl.BlockSpec((seq, cb), lambda j: (0, j))],
        out_specs=[pl.BlockSpec((t, cb), lambda j: (0, j))] * 2,
        out_shape=[jax.ShapeDtypeStruct((t, d), F32), jax.ShapeDtypeStruct((t, d), BF16)],
        compiler_params=_cp("parallel"))(meta, x2d)


def embed_bwd(adds, nm, name):
    t, d = adds[0][0].shape
    seq = t - nm
    cb = _tile(d, 256)
    scales = [s for _, s in adds]
    n = len(adds)

    def body(*refs):
        tot = None
        for r, sc in zip(refs[:n], scales):
            term = r[...] if sc == 1.0 else sc * r[...]
            tot = term if tot is None else tot + term
        gm_ref, gx_ref = refs[n], refs[n + 1]
        gm_ref[...] = tot[0:nm]
        gx_ref[...] = tot[nm:t]

    return pl.pallas_call(
        body, name=name, grid=(d // cb,),
        in_specs=[pl.BlockSpec((t, cb), lambda j: (0, j))] * n,
        out_specs=[pl.BlockSpec((nm, cb), lambda j: (0, j)), pl.BlockSpec((seq, cb), lambda j: (0, j))],
        out_shape=[jax.ShapeDtypeStruct((nm, d), F32), jax.ShapeDtypeStruct((seq, d), F32)],
        compiler_params=_cp("parallel"))(*[a for a, _ in adds])


def loss_fwd_bwd(h, tgt, nm, name):
    t, d = h.shape
    seq = t - nm
    cb = _tile(d, 256)
    inv_d = 1.0 / d

    def body(h_ref, t_ref, loss_ref, dy_ref):
        @pl.when(pl.program_id(0) == 0)
        def _():
            loss_ref[...] = jnp.zeros_like(loss_ref)
        err = h_ref[pl.ds(nm, seq), :] - t_ref[...]
        dy_ref[pl.ds(0, nm), :] = jnp.zeros((nm, cb), F32)
        dy_ref[pl.ds(nm, seq), :] = err * inv_d
        loss_ref[...] += (0.5 * inv_d) * jnp.sum(err * err, keepdims=True)

    return pl.pallas_call(
        body, name=name, grid=(d // cb,),
        in_specs=[pl.BlockSpec((t, cb), lambda j: (0, j)), pl.BlockSpec((seq, cb), lambda j: (0, j))],
        out_specs=[pl.BlockSpec((1, 1), lambda j: (0, 0)), pl.BlockSpec((t, cb), lambda j: (0, j))],
        out_shape=[jax.ShapeDtypeStruct((1, 1), F32), jax.ShapeDtypeStruct((t, d), F32)],
        compiler_params=_cp("arbitrary"))(h, tgt)


def mm_out_ln(x, w, h, g, b, alpha, name):
    t, d = h.shape
    k = x.shape[1]
    rb = _row_block(t)

    def body(x_ref, w_ref, h_ref, g_ref, b_ref, y_ref, yb_ref, xh_ref, rs_ref):
        s = alpha * h_ref[...] + jnp.dot(x_ref[...], w_ref[...], preferred_element_type=F32)
        mu = jnp.mean(s, axis=-1, keepdims=True)
        c = s - mu
        var = jnp.mean(c * c, axis=-1, keepdims=True)
        rstd = lax.rsqrt(var + LN_EPS)
        xh = c * rstd
        y = xh * g_ref[...] + b_ref[...]
        y_ref[...] = y
        yb_ref[...] = y.astype(BF16)
        xh_ref[...] = xh
        rs_ref[...] = rstd

    row = pl.BlockSpec((rb, d), lambda r: (r, 0))
    vec = pl.BlockSpec((1, d), lambda r: (0, 0))
    return pl.pallas_call(
        body, name=name, grid=(t // rb,),
        in_specs=[pl.BlockSpec((rb, k), lambda r: (r, 0)), pl.BlockSpec((k, d), lambda r: (0, 0)), row, vec, vec],
        out_specs=[row, row, row, pl.BlockSpec((rb, 1), lambda r: (r, 0))],
        out_shape=[jax.ShapeDtypeStruct((t, d), F32), jax.ShapeDtypeStruct((t, d), BF16),
                   jax.ShapeDtypeStruct((t, d), F32), jax.ShapeDtypeStruct((t, 1), F32)],
        compiler_params=_cp("parallel"))(x, w, h, g, b)


def ln_bwd(adds, xhat, rstd, g, name, w=None):
    t, d = xhat.shape
    rb = _row_block(t)
    scales = [s for _, s in adds]
    n = len(adds)
    n_in = n + 3 + (w is not None)

    def body(*refs):
        xh_ref, rs_ref, g_ref = refs[n:n + 3]
        ds_ref, dsb_ref, dg_ref, db_ref = refs[n_in:n_in + 4]
        dy = None
        for r, sc in zip(refs[:n], scales):
            term = r[...] if sc == 1.0 else sc * r[...]
            dy = term if dy is None else dy + term

        @pl.when(pl.program_id(0) == 0)
        def _():
            dg_ref[...] = jnp.zeros_like(dg_ref)
            db_ref[...] = jnp.zeros_like(db_ref)

        xh = xh_ref[...]
        dxh = dy * g_ref[...]
        m1 = jnp.mean(dxh, axis=-1, keepdims=True)
        m2 = jnp.mean(dxh * xh, axis=-1, keepdims=True)
        ds = rs_ref[...] * (dxh - m1 - xh * m2)
        ds_b = ds.astype(BF16)
        ds_ref[...] = ds
        dsb_ref[...] = ds_b
        dg_ref[...] += jnp.sum(dy * xh, axis=0, keepdims=True)
        db_ref[...] += jnp.sum(dy, axis=0, keepdims=True)
        if w is not None:
            refs[-1][...] = lax.dot_general(ds_b, refs[n + 3][...], (((1,), (1,)), ((), ())),
                                            preferred_element_type=F32)

    row = pl.BlockSpec((rb, d), lambda r: (r, 0))
    vec = pl.BlockSpec((1, d), lambda r: (0, 0))
    in_specs = [row] * n + [row, pl.BlockSpec((rb, 1), lambda r: (r, 0)), vec]
    out_specs = [row, row, vec, vec]
    out_shape = [jax.ShapeDtypeStruct((t, d), F32), jax.ShapeDtypeStruct((t, d), BF16),
                 jax.ShapeDtypeStruct((1, d), F32), jax.ShapeDtypeStruct((1, d), F32)]
    operands = [a for a, _ in adds] + [xhat, rstd, g]
    if w is not None:
        k = w.shape[0]
        in_specs.append(pl.BlockSpec((k, d), lambda r: (0, 0)))
        out_specs.append(pl.BlockSpec((rb, k), lambda r: (r, 0)))
        out_shape.append(jax.ShapeDtypeStruct((t, k), F32))
        operands.append(w)
    return pl.pallas_call(
        body, name=name, grid=(t // rb,), in_specs=in_specs, out_specs=out_specs, out_shape=out_shape,
        compiler_params=_cp("arbitrary"))(*operands)


def _conv_fwd_val(xv, w_ref, b_ref, width):
    y = b_ref[...]
    for j in range(width):
        y = y + _shift_down(xv, j) * w_ref[pl.ds(width - 1 - j, 1), :]
    return y


def _conv_bwd_val(dout, xv, w_ref, width):
    dx = None
    dws = [None] * width
    for j in range(width):
        k = width - 1 - j
        term = _shift_up(dout, j) * w_ref[pl.ds(k, 1), :]
        dx = term if dx is None else dx + term
        dws[k] = jnp.sum(dout * _shift_down(xv, j), axis=0, keepdims=True)
    return dx, dws, jnp.sum(dout, axis=0, keepdims=True)


def a_conv_fwd(gr, cw, cbias, name):
    t, two_dr = gr.shape
    dr = two_dr // 2
    width = cw.shape[0]
    cb = _tile(dr, 256)
    off = dr // cb

    def body(x_ref, w_ref, b_ref, rc_ref, rcb_ref):
        y = _conv_fwd_val(x_ref[...], w_ref, b_ref, width)
        rc_ref[...] = y
        rcb_ref[...] = y.astype(BF16)

    return pl.pallas_call(
        body, name=name, grid=(dr // cb,),
        in_specs=[pl.BlockSpec((t, cb), lambda j: (0, off + j)),
                  pl.BlockSpec((width, cb), lambda j: (0, j)), pl.BlockSpec((1, cb), lambda j: (0, j))],
        out_specs=[pl.BlockSpec((t, cb), lambda j: (0, j))] * 2,
        out_shape=[jax.ShapeDtypeStruct((t, dr), F32), jax.ShapeDtypeStruct((t, dr), BF16)],
        compiler_params=_cp("parallel"))(gr, cw, cbias)


def a_conv_bwd(drc_a, drc_b, gr, cw, dgr, name):
    t, two_dr = gr.shape
    dr = two_dr // 2
    width = cw.shape[0]
    cb = _tile(dr, 256)
    off = dr // cb

    def body(da_ref, db_ref, x_ref, w_ref, dgr_in, dx_ref, dw_ref, dbias_ref):
        del dgr_in
        dout = da_ref[...] + db_ref[...]
        dx, dws, dbias = _conv_bwd_val(dout, x_ref[...], w_ref, width)
        dx_ref[...] = dx.astype(BF16)
        for k in range(width):
            dw_ref[pl.ds(k, 1), :] = dws[k]
        dbias_ref[...] = dbias

    col = pl.BlockSpec((t, cb), lambda j: (0, j))
    return pl.pallas_call(
        body, name=name, grid=(dr // cb,),
        in_specs=[col, col, pl.BlockSpec((t, cb), lambda j: (0, off + j)),
                  pl.BlockSpec((width, cb), lambda j: (0, j)), pl.BlockSpec(memory_space=pl.ANY)],
        out_specs=[pl.BlockSpec((t, cb), lambda j: (0, off + j)), pl.BlockSpec((width, cb), lambda j: (0, j)),
                   pl.BlockSpec((1, cb), lambda j: (0, j))],
        out_shape=[jax.ShapeDtypeStruct((t, two_dr), BF16), jax.ShapeDtypeStruct((width, dr), F32),
                   jax.ShapeDtypeStruct((1, dr), F32)],
        input_output_aliases={4: 0},
        compiler_params=_cp("parallel"))(drc_a, drc_b, gr, cw, dgr)


def _lru_gates(r_pre, i_pre, br, bi, lam):
    r = _sigmoid(r_pre + br)
    i = _sigmoid(i_pre + bi)
    sp = _softplus(-lam)
    la = -LRU_C * r * sp
    a = jnp.exp(la)
    m = jnp.sqrt(_neg_expm1(2.0 * la))
    return r, i, sp, la, a, m


def a_elem_fwd(gr, rc, r_pre, i_pre, br, bi, lam, name):
    t, dr = rc.shape
    cb = _tile(dr, 3 * LANES)
    rb = _row_block(t)
    chunks = [pl.ds(r * rb, rb) for r in range(t // rb)]

    def body(gate_ref, rc_ref, rp_ref, ip_ref, br_ref, bi_ref, lam_ref, hs_ref, g_ref, a_s, u_s):
        for rows in chunks:
            _, i, _, _, a, m = _lru_gates(rp_ref[rows, :], ip_ref[rows, :], br_ref[...], bi_ref[...], lam_ref[...])
            a_s[rows, :] = a
            u_s[rows, :] = m * (i * rc_ref[rows, :])
        _scan_rows(a_s, u_s, hs_ref, t, cb, reverse=False)
        for rows in chunks:
            g_ref[rows, :] = (_gelu(gate_ref[rows, :]) * hs_ref[rows, :]).astype(BF16)

    col = pl.BlockSpec((t, cb), lambda j: (0, j))
    vec = pl.BlockSpec((1, cb), lambda j: (0, j))
    return pl.pallas_call(
        body, name=name, grid=(dr // cb,),
        in_specs=[col, col, col, col, vec, vec, vec],
        out_specs=[col, col],
        out_shape=[jax.ShapeDtypeStruct((t, dr), F32), jax.ShapeDtypeStruct((t, dr), BF16)],
        scratch_shapes=[pltpu.VMEM((t, cb), F32), pltpu.VMEM((t, cb), F32)],
        compiler_params=_cp("parallel"))(gr, rc, r_pre, i_pre, br, bi, lam)


def a_elem_bwd(dg, gr, rc, r_pre, i_pre, hs, br, bi, lam, name):
    t, dr = rc.shape
    cb = _tile(dr, 2 * LANES)
    rb = _row_block(t)
    chunks = [pl.ds(r * rb, rb) for r in range(t // rb)]

    def body(dg_ref, gate_ref, rc_ref, rp_ref, ip_ref, hs_ref, br_ref, bi_ref, lam_ref,
             dgate_ref, dr_ref, di_ref, drc_ref, dlam_ref, dbr_ref, dbi_ref, a_s, b_s, g_s, hp_s):
        lamv = lam_ref[...]
        gates = lambda rows: _lru_gates(rp_ref[rows, :], ip_ref[rows, :], br_ref[...], bi_ref[...], lamv)
        for rows in chunks:
            a_s[rows, :] = gates(rows)[4]
            ge, dge = _gelu_parts(gate_ref[rows, :])
            dgv = dg_ref[rows, :]
            dgate_ref[rows, :] = (dgv * hs_ref[rows, :] * dge).astype(BF16)
            b_s[rows, :] = dgv * ge
        a_s[...] = _shift_up(a_s[...], 1)
        hp_s[...] = _shift_down(hs_ref[...], 1)
        _scan_rows(a_s, b_s, g_s, t, cb, reverse=True)
        dsp = dbr = dbi = jnp.zeros((1, cb), F32)
        for rows in chunks:
            r, i, sp, _, a, m = gates(rows)
            rcv = rc_ref[rows, :]
            gsum = g_s[rows, :]
            da = gsum * hp_s[rows, :]
            dm = gsum * (i * rcv)
            d_i = gsum * m * rcv
            drc_ref[rows, :] = gsum * m * i
            dla = a * da - dm * (a * a) / m
            d_r = (-LRU_C) * sp * dla
            dsp = dsp + jnp.sum((-LRU_C) * r * dla, axis=0, keepdims=True)
            d_rp = d_r * r * (1.0 - r)
            d_ip = d_i * i * (1.0 - i)
            dr_ref[rows, :] = d_rp.astype(BF16)
            di_ref[rows, :] = d_ip.astype(BF16)
            dbr = dbr + jnp.sum(d_rp, axis=0, keepdims=True)
            dbi = dbi + jnp.sum(d_ip, axis=0, keepdims=True)
        dlam_ref[...] = -dsp * _sigmoid(-lamv)
        dbr_ref[...] = dbr
        dbi_ref[...] = dbi

    col = pl.BlockSpec((t, cb), lambda j: (0, j))
    vec = pl.BlockSpec((1, cb), lambda j: (0, j))
    big_b = jax.ShapeDtypeStruct((t, dr), BF16)
    vec_s = jax.ShapeDtypeStruct((1, dr), F32)
    return pl.pallas_call(
        body, name=name, grid=(dr // cb,),
        in_specs=[col, col, col, col, col, col, vec, vec, vec],
        out_specs=[col, col, col, col, vec, vec, vec],
        out_shape=[jax.ShapeDtypeStruct((t, 2 * dr), BF16), big_b, big_b, jax.ShapeDtypeStruct((t, dr), F32),
                   vec_s, vec_s, vec_s],
        scratch_shapes=[pltpu.VMEM((t, cb), F32)] * 4,
        compiler_params=_cp("parallel"))(dg, gr, rc, r_pre, i_pre, hs, br, bi, lam)


def f_elem_fwd(z, cw, cbias, name):
    t, two_f = z.shape
    dff = two_f // 2
    width = cw.shape[0]
    cb = _tile(dff, 256)
    off = dff // cb

    def body(zg_ref, zv_ref, wg_ref, wv_ref, bg_ref, bv_ref, o_ref):
        zcg = _conv_fwd_val(zg_ref[...], wg_ref, bg_ref, width)
        zcv = _conv_fwd_val(zv_ref[...], wv_ref, bv_ref, width)
        o_ref[...] = (_gelu(zcg) * zcv).astype(BF16)

    lo = lambda j: (0, j)
    hi = lambda j: (0, off + j)
    return pl.pallas_call(
        body, name=name, grid=(dff // cb,),
        in_specs=[pl.BlockSpec((t, cb), lo), pl.BlockSpec((t, cb), hi),
                  pl.BlockSpec((width, cb), lo), pl.BlockSpec((width, cb), hi),
                  pl.BlockSpec((1, cb), lo), pl.BlockSpec((1, cb), hi)],
        out_specs=pl.BlockSpec((t, cb), lo),
        out_shape=jax.ShapeDtypeStruct((t, dff), BF16),
        compiler_params=_cp("parallel"))(z, z, cw, cw, cbias, cbias)


def f_elem_bwd(z, dff_g, cw, cbias, name):
    t, two_f = z.shape
    dff = two_f // 2
    width = cw.shape[0]
    cb = _tile(dff, 256)
    off = dff // cb

    def body(zg_ref, zv_ref, d_ref, wg_ref, wv_ref, bg_ref, bv_ref,
             dzg_ref, dzv_ref, dwg_ref, dwv_ref, dbg_ref, dbv_ref):
        zg = zg_ref[...]
        zv = zv_ref[...]
        zcg = _conv_fwd_val(zg, wg_ref, bg_ref, width)
        zcv = _conv_fwd_val(zv, wv_ref, bv_ref, width)
        ge, dge = _gelu_parts(zcg)
        dv = d_ref[...]
        dx, dws, dbias = _conv_bwd_val(dv * zcv * dge, zg, wg_ref, width)
        dzg_ref[...] = dx.astype(BF16)
        for k in range(width):
            dwg_ref[pl.ds(k, 1), :] = dws[k]
        dbg_ref[...] = dbias
        dx, dws, dbias = _conv_bwd_val(dv * ge, zv, wv_ref, width)
        dzv_ref[...] = dx.astype(BF16)
        for k in range(width):
            dwv_ref[pl.ds(k, 1), :] = dws[k]
        dbv_ref[...] = dbias

    lo = lambda j: (0, j)
    hi = lambda j: (0, off + j)
    col = pl.BlockSpec((t, cb), lo)
    wsp = pl.BlockSpec((width, cb), lo)
    vsp = pl.BlockSpec((1, cb), lo)
    return pl.pallas_call(
        body, name=name, grid=(dff // cb,),
        in_specs=[col, pl.BlockSpec((t, cb), hi), col, wsp, pl.BlockSpec((width, cb), hi),
                  vsp, pl.BlockSpec((1, cb), hi)],
        out_specs=[col, col, wsp, wsp, vsp, vsp],
        out_shape=[jax.ShapeDtypeStruct((t, dff), BF16)] * 2
        + [jax.ShapeDtypeStruct((width, dff), F32)] * 2 + [jax.ShapeDtypeStruct((1, dff), F32)] * 2,
        compiler_params=_cp("parallel"))(z, z, dff_g, cw, cw, cbias, cbias)


def kv_fwd(z, fb, d_model, name):
    t, _ = z.shape
    blk = 2 * d_model // LANES

    def body(z_ref, fb_ref, c_ref, lf_s):
        v = z_ref[...] + fb_ref[...]
        lf_s[...] = -_softplus(-v)
        _scan_rows(None, lf_s, c_ref, t, LANES, reverse=False)

    return pl.pallas_call(
        body, name=name, grid=(1,),
        in_specs=[pl.BlockSpec((t, LANES), lambda j: (0, blk)), pl.BlockSpec((1, LANES), lambda j: (0, 0))],
        out_specs=pl.BlockSpec((t, LANES), lambda j: (0, 0)),
        out_shape=jax.ShapeDtypeStruct((t, LANES), F32),
        scratch_shapes=[pltpu.VMEM((t, LANES), F32)],
        compiler_params=_cp("arbitrary"))(z, fb)


def kv_bwd(dcs, z, fb, d_model, name):
    t, _ = z.shape
    blk = 2 * d_model // LANES
    n = len(dcs)

    def body(*refs):
        z_ref, fb_ref, dz_ref, dfb_ref, dc_s, dl_s = refs[n:]
        tot = refs[0][...]
        for r in refs[1:n]:
            tot = tot + r[...]
        dc_s[...] = tot
        _scan_rows(None, dc_s, dl_s, t, LANES, reverse=True)
        v = z_ref[...] + fb_ref[...]
        dz = dl_s[...] * _sigmoid(-v)
        dz_ref[...] = dz.astype(BF16)
        dfb_ref[...] = jnp.sum(dz, axis=0, keepdims=True)

    full = pl.BlockSpec((t, LANES), lambda j: (0, 0))
    return pl.pallas_call(
        body, name=name, grid=(1,),
        in_specs=[full] * n + [pl.BlockSpec((t, LANES), lambda j: (0, blk)),
                               pl.BlockSpec((1, LANES), lambda j: (0, 0))],
        out_specs=[full, pl.BlockSpec((1, LANES), lambda j: (0, 0))],
        out_shape=[jax.ShapeDtypeStruct((t, LANES), BF16), jax.ShapeDtypeStruct((1, LANES), F32)],
        scratch_shapes=[pltpu.VMEM((t, LANES), F32)] * 2,
        compiler_params=_cp("arbitrary"))(*dcs, z, fb)


def add_cast(a, b, name):
    t, d = a.shape
    cb = _tile(d, 512)

    def body(a_ref, b_ref, o_ref):
        o_ref[...] = (a_ref[...] + b_ref[...]).astype(BF16)

    col = pl.BlockSpec((t, cb), lambda j: (0, j))
    return pl.pallas_call(body, name=name, grid=(d // cb,), in_specs=[col, col], out_specs=col,
                          out_shape=jax.ShapeDtypeStruct((t, d), BF16),
                          compiler_params=_cp("parallel"))(a, b)


def _attn_geometry(t):
    nqb = 6 if t > 1024 else 2
    tp = _round_up(t, LANES * nqb)
    return nqb, tp, tp // nqb


def _attn_scales(dh):
    scale = dh ** -0.5
    if math.log2(scale).is_integer():
        return scale, 1.0
    return 1.0, scale


def _attn_pieces(qs, ks, crow, j, i, tq, dh, s_mul):
    r0 = i * tq
    lanes = pl.ds(j * dh, dh)
    qi = qs[pl.ds(r0, tq), lanes]
    spans = ([(0, r0)] if i > 0 else []) + [(r0, tq)]
    logits = []
    for k0, n in spans:
        s = lax.dot_general(qi, ks[pl.ds(k0, n), lanes], (((1,), (1,)), ((), ())),
                            preferred_element_type=F32)
        if s_mul != 1.0:
            s = s * s_mul
        s = s - crow[:, k0:k0 + n]
        if k0 == r0:
            rows = lax.broadcasted_iota(jnp.int32, (tq, tq), 0)
            cols = lax.broadcasted_iota(jnp.int32, (tq, tq), 1)
            s = jnp.where(cols <= rows, s, NEG_BIG)
        logits.append(s)
    mx = jnp.max(logits[0], axis=1, keepdims=True)
    for s in logits[1:]:
        mx = jnp.maximum(mx, jnp.max(s, axis=1, keepdims=True))
    es = [jnp.exp(s - mx) for s in logits]
    tot = jnp.sum(es[0], axis=1, keepdims=True)
    for e in es[1:]:
        tot = tot + jnp.sum(e, axis=1, keepdims=True)
    inv = 1.0 / tot
    return [(k0, n, e * inv) for (k0, n), e in zip(spans, es)], qi


def attn_fwd(qg, z, ct_pad, d_model, n_heads, name):
    t = qg.shape[0]
    dh = d_model // n_heads
    hp = LANES // dh
    nqb, tp, tq = _attn_geometry(t)
    nblk = d_model // LANES
    q_mul, s_mul = _attn_scales(dh)

    def body(q_ref, og_ref, k_ref, v_ref, ct_ref, o_ref, mo_ref, qs, ks, vs, os_):
        pad = jnp.zeros((tp - t, LANES), BF16)
        qs[pl.ds(0, t), :] = (q_ref[...] * q_mul).astype(BF16)
        qs[pl.ds(t, tp - t), :] = pad
        for src, dst in ((k_ref, ks), (v_ref, vs)):
            dst[pl.ds(0, t), :] = src[...].astype(BF16)
            dst[pl.ds(t, tp - t), :] = pad
        for j in range(hp):
            crow = ct_ref[j]
            lanes = pl.ds(j * dh, dh)
            for i in range(nqb):
                pieces, _ = _attn_pieces(qs, ks, crow, j, i, tq, dh, s_mul)
                acc = None
                for k0, n, p in pieces:
                    part = jnp.dot(p.astype(BF16), vs[pl.ds(k0, n), lanes], preferred_element_type=F32)
                    acc = part if acc is None else acc + part
                os_[pl.ds(i * tq, tq), lanes] = acc
        o = os_[pl.ds(0, t), :]
        o_ref[...] = o
        mo_ref[...] = (o * _sigmoid(og_ref[...])).astype(BF16)

    col = lambda off: pl.BlockSpec((t, LANES), lambda p: (0, off + p))
    return pl.pallas_call(
        body, name=name, grid=(nblk,),
        in_specs=[col(0), col(nblk), col(0), col(nblk), pl.BlockSpec((hp, 1, tp), lambda p: (p, 0, 0))],
        out_specs=[col(0), col(0)],
        out_shape=[jax.ShapeDtypeStruct((t, d_model), F32), jax.ShapeDtypeStruct((t, d_model), BF16)],
        scratch_shapes=[pltpu.VMEM((tp, LANES), BF16)] * 3 + [pltpu.VMEM((tp, LANES), F32)],
        compiler_params=_cp("parallel"))(qg, qg, z, z, ct_pad)


def attn_bwd(dmo, qg, z, o, ct_pad, d_model, n_heads, name):
    t = qg.shape[0]
    dh = d_model // n_heads
    hp = LANES // dh
    nqb, tp, tq = _attn_geometry(t)
    nblk = d_model // LANES
    q_mul, s_mul = _attn_scales(dh)
    scale = dh ** -0.5
    tn_dims = (((0,), (0,)), ((), ()))
    nt_dims = (((1,), (1,)), ((), ()))

    def body(dmo_ref, q_ref, og_ref, k_ref, v_ref, o_ref, ct_ref,
             dq_ref, dog_ref, dk_ref, dv_ref, dct_ref, qs, ks, vs, dos, dqs, dks, dvs):
        pad = jnp.zeros((tp - t, LANES), BF16)
        sg = _sigmoid(og_ref[...])
        dmo_v = dmo_ref[...]
        dog_ref[...] = (dmo_v * o_ref[...] * sg * (1.0 - sg)).astype(BF16)
        dos[pl.ds(0, t), :] = (dmo_v * sg).astype(BF16)
        dos[pl.ds(t, tp - t), :] = pad
        qs[pl.ds(0, t), :] = (q_ref[...] * q_mul).astype(BF16)
        qs[pl.ds(t, tp - t), :] = pad
        for src, dst in ((k_ref, ks), (v_ref, vs)):
            dst[pl.ds(0, t), :] = src[...].astype(BF16)
            dst[pl.ds(t, tp - t), :] = pad
        dks[...] = jnp.zeros_like(dks)
        dvs[...] = jnp.zeros_like(dvs)
        dct_ref[...] = jnp.zeros_like(dct_ref)
        for j in range(hp):
            crow = ct_ref[j]
            lanes = pl.ds(j * dh, dh)
            for i in range(nqb):
                pieces, qi = _attn_pieces(qs, ks, crow, j, i, tq, dh, s_mul)
                do_i = dos[pl.ds(i * tq, tq), lanes]
                dps = [lax.dot_general(do_i, vs[pl.ds(k0, n), lanes], nt_dims, preferred_element_type=F32)
                       for k0, n, _ in pieces]
                row = None
                for (_, _, p), dp in zip(pieces, dps):
                    part = jnp.sum(p * dp, axis=1, keepdims=True)
                    row = part if row is None else row + part
                dq_i = None
                for (k0, n, p), dp in zip(pieces, dps):
                    ds = p * (dp - row)
                    ds_b = ds.astype(BF16)
                    keys = pl.ds(k0, n)
                    part = jnp.dot(ds_b, ks[keys, lanes], preferred_element_type=F32)
                    dq_i = part if dq_i is None else dq_i + part
                    dks[keys, lanes] += lax.dot_general(ds_b, qi, tn_dims, preferred_element_type=F32) * s_mul
                    dvs[keys, lanes] += lax.dot_general(p.astype(BF16), do_i, tn_dims,
                                                        preferred_element_type=F32)
                    dct_ref[j, :, keys] -= jnp.sum(ds, axis=0, keepdims=True)
                dqs[pl.ds(i * tq, tq), lanes] = dq_i * scale
        dq_ref[...] = dqs[pl.ds(0, t), :].astype(BF16)
        dk_ref[...] = dks[pl.ds(0, t), :]
        dv_ref[...] = dvs[pl.ds(0, t), :]

    col = lambda off: pl.BlockSpec((t, LANES), lambda p: (0, off + p))
    big = lambda dt: jax.ShapeDtypeStruct((t, d_model), dt)
    return pl.pallas_call(
        body, name=name, grid=(nblk,),
        in_specs=[col(0), col(0), col(nblk), col(0), col(nblk), col(0),
                  pl.BlockSpec((hp, 1, tp), lambda p: (p, 0, 0))],
        out_specs=[col(0), col(0), col(0), col(0), pl.BlockSpec((hp, 1, tp), lambda p: (p, 0, 0))],
        out_shape=[big(BF16), big(BF16), big(F32), big(F32),
                   jax.ShapeDtypeStruct((n_heads, 1, tp), F32)],
        scratch_shapes=[pltpu.VMEM((tp, LANES), BF16)] * 4 + [pltpu.VMEM((tp, LANES), F32)] * 3,
        compiler_params=_cp("parallel"))(dmo, qg, qg, z, z, o, ct_pad)


def cast_into_slot(shard, w, index, name):
    r, c = w.shape[-2:]
    rh = r // 2
    tr = _tile(rh, 512, 16)
    n = rh // tr
    if w.ndim == 3:
        w_spec = pl.BlockSpec((None, tr, c), lambda h, i, sh: (index, h * n + i, 0))
    else:
        w_spec = pl.BlockSpec((tr, c), lambda h, i, sh: (h * n + i, 0))

    def body(sh_ref, w_ref, o_ref):
        del sh_ref
        o_ref[...] = w_ref[...].astype(BF16)

    return pl.pallas_call(
        body, name=name,
        grid_spec=pltpu.PrefetchScalarGridSpec(
            num_scalar_prefetch=1, grid=(2, n), in_specs=[w_spec],
            out_specs=pl.BlockSpec((None, None, tr, c), lambda h, i, sh: (sh[0], h, i, 0))),
        out_shape=jax.ShapeDtypeStruct((N_SHARDS, 2, rh, c), BF16),
        compiler_params=_cp("parallel", "parallel"))(shard, w)


def owner_sum(shard_core, g, recv, buf, layer, name):
    _, _, rh, c = g.shape
    n_recv = recv.shape[0]
    tr = _tile(rh, 512, 16)
    slot = (lambda sc: sc[0]) if layer is None else (lambda sc: layer)

    def body(sc_ref, a_ref, r_ref, buf_ref, out_ref):
        del sc_ref, buf_ref
        acc = a_ref[...].astype(F32)
        for k in range(n_recv):
            acc = acc + r_ref[k].astype(F32)
        out_ref[...] = acc

    return pl.pallas_call(
        body, name=name,
        grid_spec=pltpu.PrefetchScalarGridSpec(
            num_scalar_prefetch=1, grid=(rh // tr,),
            in_specs=[pl.BlockSpec((None, None, tr, c), lambda i, sc: (sc[0], sc[1], i, 0)),
                      pl.BlockSpec((n_recv, tr, c), lambda i, sc: (0, i, 0)),
                      pl.BlockSpec(memory_space=pl.ANY)],
            out_specs=pl.BlockSpec((None, None, tr, c), lambda i, sc: (slot(sc), sc[1], i, 0))),
        out_shape=jax.ShapeDtypeStruct(buf.shape, F32),
        input_output_aliases={3: 0},
        compiler_params=_cp("parallel"))(shard_core, g, recv, buf)


def adamw(w, g, m, v, name, emit_g=False):
    r, c = w.shape
    tr = _tile(r, 512, SUBLANES)
    c1 = 1.0 - ADAM_B1 ** ADAM_STEP
    c2 = 1.0 - ADAM_B2 ** ADAM_STEP
    n_out = 4 if emit_g else 3

    def body(w_ref, g_ref, m_ref, v_ref, d_ref, mo_ref, vo_ref, *go_ref):
        gv = g_ref[...]
        if emit_g:
            go_ref[0][...] = gv
        mn = ADAM_B1 * m_ref[...] + (1.0 - ADAM_B1) * gv
        vn = ADAM_B2 * v_ref[...] + (1.0 - ADAM_B2) * (gv * gv)
        m_hat = mn / c1
        v_hat = vn / c2
        d_ref[...] = -ADAM_LR * (m_hat / (jnp.sqrt(v_hat) + ADAM_EPS) + ADAM_WD * w_ref[...])
        mo_ref[...] = mn
        vo_ref[...] = vn

    blk = pl.BlockSpec((tr, c), lambda i: (i, 0))
    return pl.pallas_call(
        body, name=name, grid=(r // tr,), in_specs=[blk] * 4, out_specs=[blk] * n_out,
        out_shape=[jax.ShapeDtypeStruct((r, c), F32)] * n_out,
        compiler_params=_cp("parallel"))(w, g, m, v)


def _coords():
    return lax.axis_index("x"), lax.axis_index("y"), lax.axis_index("c")


def _exchange(name, ins, out_shapes, plan, in_place=False):
    n_in = len(ins)
    n_out = len(out_shapes)
    n_rem = len(plan([None] * n_in, [None] * n_out, True))

    def body(*refs):
        in_refs = refs[:n_in]
        out_refs = refs[n_in:n_in + n_out]
        send_sems, recv_sems = refs[n_in + n_out:]
        copies = [pltpu.make_async_remote_copy(
            src_ref=src, dst_ref=dst, send_sem=send_sems.at[q], recv_sem=recv_sems.at[q],
            device_id=peer, device_id_type=pl.DeviceIdType.MESH)
            for q, (src, dst, peer) in enumerate(plan(list(in_refs), list(out_refs), False))]
        for cp in copies:
            cp.start()
        for cp in copies:
            cp.wait_recv()
        for cp in copies:
            cp.wait_send()

    hbm = pl.BlockSpec(memory_space=pl.ANY)
    return pl.pallas_call(
        body, name=name, in_specs=[hbm] * n_in, out_specs=[hbm] * n_out, out_shape=out_shapes,
        input_output_aliases={i: i for i in range(n_in)} if in_place else {},
        scratch_shapes=[pltpu.SemaphoreType.DMA((n_rem,)), pltpu.SemaphoreType.DMA((n_rem,))],
        compiler_params=pltpu.CompilerParams(has_side_effects=True))(*ins)


def _split_start(name, groups, plan):
    flat = [a for grp in groups for a in grp]
    n, n_grp = len(flat), len(groups)
    counts = [len(plan(g, [None] * len(grp), True)) for g, grp in enumerate(groups)]

    def body(*refs):
        ins, sems, token = refs[:n], refs[n:n + 2 * n_grp], refs[-1]
        pos = 0
        for g, grp in enumerate(groups):
            arrs = list(ins[pos:pos + len(grp)])
            pos += len(grp)
            for q, (src, dst, peer) in enumerate(plan(g, arrs, False)):
                pltpu.make_async_remote_copy(
                    src_ref=src, dst_ref=dst, send_sem=sems[2 * g].at[q], recv_sem=sems[2 * g + 1].at[q],
                    device_id=peer, device_id_type=pl.DeviceIdType.MESH).start()
        token[...] = jnp.zeros_like(token)

    hbm = pl.BlockSpec(memory_space=pltpu.HBM)
    sem = pl.BlockSpec(memory_space=pltpu.SEMAPHORE)
    outs = pl.pallas_call(
        body, name=name,
        out_shape=[pltpu.SemaphoreType.DMA((cnt,)) for cnt in counts for _ in range(2)]
        + [pltpu.HBM(a.shape, a.dtype) for a in flat] + [jax.ShapeDtypeStruct((SUBLANES, LANES), F32)],
        in_specs=[hbm] * n, out_specs=[sem] * (2 * n_grp) + [hbm] * n + [pl.BlockSpec(memory_space=pltpu.VMEM)],
        input_output_aliases={i: 2 * n_grp + i for i in range(n)},
        compiler_params=pltpu.CompilerParams(has_side_effects=pltpu.SideEffectType.DATAFLOW_SIDE_EFFECTING),
    )(*[pltpu.with_memory_space_constraint(a, pltpu.HBM) for a in flat])
    started, pos = [], 2 * n_grp
    for g, grp in enumerate(groups):
        started.append((outs[2 * g], outs[2 * g + 1], list(outs[pos:pos + len(grp)])))
        pos += len(grp)
    return started, outs[-1]


def _split_wait(name, started, after, plan_g):
    send_sems, recv_sems, arrs = started
    n = len(arrs)

    def body(*refs):
        ins, ssem, rsem = list(refs[:n]), refs[n], refs[n + 1]
        for q, (src, dst, peer) in enumerate(plan_g(ins, False)):
            cp = pltpu.make_async_remote_copy(
                src_ref=src, dst_ref=dst, send_sem=ssem.at[q], recv_sem=rsem.at[q],
                device_id=peer, device_id_type=pl.DeviceIdType.MESH)
            cp.wait_send()
            cp.wait_recv()

    hbm = pl.BlockSpec(memory_space=pltpu.HBM)
    sem = pl.BlockSpec(memory_space=pltpu.SEMAPHORE)
    return pl.pallas_call(
        body, name=name, out_shape=[pltpu.HBM(a.shape, a.dtype) for a in arrs],
        in_specs=[hbm] * n + [sem, sem, pl.BlockSpec(memory_space=pl.ANY)], out_specs=[hbm] * n,
        input_output_aliases={i: i for i in range(n)},
        compiler_params=pltpu.CompilerParams(has_side_effects=pltpu.SideEffectType.DATAFLOW_SIDE_EFFECTING),
    )(*arrs, send_sems, recv_sems, after)


def _gather_ici_plan(arrs, count_only):
    if count_only:
        return [None] * (3 * len(arrs))
    x, y, c = _coords()
    pushes = []
    for a in arrs:
        mine = a.at[2 * x + y, c]
        pushes += [(mine, mine, peer) for peer, _ in _other_chips(x, y, c)]
    return pushes


def _all_to_all_plan(arrs, count_only):
    half = len(arrs) // 2
    if count_only:
        return [None] * (7 * half)
    x, y, c = _coords()
    pushes = []
    for src, land in zip(arrs[:half], arrs[half:]):
        for flips in range(1, 8):
            px, py, pc = (1 - x if flips & 4 else x), (1 - y if flips & 2 else y), (1 - c if flips & 1 else c)
            pushes.append((src.at[2 * px + py, pc], land.at[flips - 1], (px, py, pc)))
    return pushes


def _forward_plan(arrs, count_only):
    if count_only:
        return [None] * (3 * len(arrs))
    x, y, c = _coords()
    pushes = []
    for a in arrs:
        for _, src_shard in _other_chips(x, y, c):
            slab = a.at[src_shard, c]
            pushes.append((slab, slab, (x, y, 1 - c)))
    return pushes


def forward_to_sibling(bufs, name):
    shapes = [jax.ShapeDtypeStruct(b.shape, b.dtype) for b in bufs]
    return _exchange(name, bufs, shapes, lambda ins, outs, cnt: _forward_plan(outs, cnt), in_place=True)


def _other_chips(x, y, c):
    return [((1 - x, y, c), 2 * (1 - x) + y), ((x, 1 - y, c), 2 * x + 1 - y),
            ((1 - x, 1 - y, c), 2 * (1 - x) + 1 - y)]


def join_halves(bufs, everywhere, name):
    slots = [(i, l) for i, b in enumerate(bufs) for l in range(b.shape[0])]
    n = len(bufs)

    def plan(ins, outs, count_only):
        if count_only:
            return [None] * (len(slots) + 7)
        x, y, c = _coords()
        pushes = [(outs[i].at[l, c], outs[i].at[l, c], (x, y, 1 - c)) for i, l in slots]
        mine = outs[n].at[2 * x + y, c]
        for flips in range(1, 8):
            peer = (1 - x if flips & 4 else x, 1 - y if flips & 2 else y, 1 - c if flips & 1 else c)
            pushes.append((mine, mine, peer))
        return pushes

    arrs = list(bufs) + [everywhere]
    shapes = [jax.ShapeDtypeStruct(b.shape, b.dtype) for b in arrs]
    return _exchange(name, arrs, shapes, plan, in_place=True)


def _pack(arrays, multiple):
    flat = jnp.concatenate([a.reshape(-1) for a in arrays])
    n = flat.shape[0]
    return jnp.pad(flat, (0, _round_up(n, multiple) - n))


def _unpack(flat, shapes):
    out, pos = [], 0
    for shp in shapes:
        n = math.prod(shp)
        out.append(flat[pos:pos + n].reshape(shp))
        pos += n
    return out


def _block_diag(w, per_group):
    nb, bs, _ = w.shape
    g = nb // per_group
    w4 = w.reshape(g, per_group, bs, bs)
    eye = jnp.eye(per_group, dtype=w.dtype)
    full = w4[:, :, :, None, :] * eye[None, :, None, :, None]
    return full.reshape(g, per_group * bs, per_group * bs).astype(BF16)


def _block_diag_extract(full, per_group, bs):
    g = full.shape[0]
    f5 = full.reshape(g, per_group, bs, per_group, bs)
    idx = jnp.arange(per_group)
    picked = f5[:, idx, :, idx, :]
    return jnp.moveaxis(picked, 0, 1).reshape(g * per_group, bs, bs)


def kernel(x, meta, a_w_in, a_conv_w, a_conv_b, a_w_r, a_b_r, a_w_i, a_b_i, a_lambda, a_w_out, kv_w, kv_f_b, b_w_in, b_w_out, f_w_in, f_conv_w, f_conv_b, f_w_out, ln1_g, ln1_b, ln2_g, ln2_b, loss_target, m_meta, m_a_w_in, m_a_conv_w, m_a_conv_b, m_a_w_r, m_a_b_r, m_a_w_i, m_a_b_i, m_a_lambda, m_a_w_out, m_kv_w, m_kv_f_b, m_b_w_in, m_b_w_out, m_f_w_in, m_f_conv_w, m_f_conv_b, m_f_w_out, m_ln1_g, m_ln1_b, m_ln2_g, m_ln2_b, v_meta, v_a_w_in, v_a_conv_w, v_a_conv_b, v_a_w_r, v_a_b_r, v_a_w_i, v_a_b_i, v_a_lambda, v_a_w_out, v_kv_w, v_kv_f_b, v_b_w_in, v_b_w_out, v_f_w_in, v_f_conv_w, v_f_conv_b, v_f_w_out, v_ln1_g, v_ln1_b, v_ln2_g, v_ln2_b):
    weights = dict(meta=meta, a_w_in=a_w_in, a_conv_w=a_conv_w, a_conv_b=a_conv_b, a_w_r=a_w_r, a_b_r=a_b_r,
                   a_w_i=a_w_i, a_b_i=a_b_i, a_lambda=a_lambda, a_w_out=a_w_out, kv_w=kv_w, kv_f_b=kv_f_b,
                   b_w_in=b_w_in, b_w_out=b_w_out, f_w_in=f_w_in, f_conv_w=f_conv_w, f_conv_b=f_conv_b,
                   f_w_out=f_w_out, ln1_g=ln1_g, ln1_b=ln1_b, ln2_g=ln2_g, ln2_b=ln2_b)
    mom_m = dict(meta=m_meta, a_w_in=m_a_w_in, a_conv_w=m_a_conv_w, a_conv_b=m_a_conv_b, a_w_r=m_a_w_r,
                 a_b_r=m_a_b_r, a_w_i=m_a_w_i, a_b_i=m_a_b_i, a_lambda=m_a_lambda, a_w_out=m_a_w_out,
                 kv_w=m_kv_w, kv_f_b=m_kv_f_b, b_w_in=m_b_w_in, b_w_out=m_b_w_out, f_w_in=m_f_w_in,
                 f_conv_w=m_f_conv_w, f_conv_b=m_f_conv_b, f_w_out=m_f_w_out, ln1_g=m_ln1_g, ln1_b=m_ln1_b,
                 ln2_g=m_ln2_g, ln2_b=m_ln2_b)
    mom_v = dict(meta=v_meta, a_w_in=v_a_w_in, a_conv_w=v_a_conv_w, a_conv_b=v_a_conv_b, a_w_r=v_a_w_r,
                 a_b_r=v_a_b_r, a_w_i=v_a_w_i, a_b_i=v_a_b_i, a_lambda=v_a_lambda, a_w_out=v_a_w_out,
                 kv_w=v_kv_w, kv_f_b=v_kv_f_b, b_w_in=v_b_w_in, b_w_out=v_b_w_out, f_w_in=v_f_w_in,
                 f_conv_w=v_f_conv_w, f_conv_b=v_f_conv_b, f_w_out=v_f_w_out, ln1_g=v_ln1_g, ln1_b=v_ln1_b,
                 ln2_g=v_ln2_g, ln2_b=v_ln2_b)
    return _train_step(x, loss_target, weights, mom_m, mom_v)


WEIGHT_ORDER = ("meta", "a_w_in", "a_conv_w", "a_conv_b", "a_w_r", "a_b_r", "a_w_i", "a_b_i", "a_lambda",
                "a_w_out", "kv_w", "kv_f_b", "b_w_in", "b_w_out", "f_w_in", "f_conv_w", "f_conv_b",
                "f_w_out", "ln1_g", "ln1_b", "ln2_g", "ln2_b")
BIG = ("a_w_in", "a_w_out", "kv_w", "b_w_in", "b_w_out", "f_w_in", "f_w_out")
OUT_TYPE = ("a_w_out", "b_w_out", "f_w_out")
SMALL_SHARDED = (("meta", 1), ("a_conv_w", 2), ("a_conv_b", 1), ("a_b_r", 1), ("a_b_i", 1), ("a_lambda", 1),
                 ("f_conv_w", 2))
SMALL_REPLICATED = ("a_w_r", "a_w_i", "kv_f_b", "f_conv_b", "ln1_g", "ln1_b", "ln2_g", "ln2_b")


def _train_step(x, loss_target, weights, mom_m, mom_v):
    S = N_SHARDS
    seq, d = x.shape[1], x.shape[2]
    nm = weights["meta"].shape[0]
    la = weights["a_w_in"].shape[0]
    lb = weights["b_w_in"].shape[0]
    depth = la + lb
    dr = weights["a_w_out"].shape[1] * S
    nb, bs = weights["a_w_r"].shape[1], weights["a_w_r"].shape[2]
    per_group = (LANES // math.gcd(bs, LANES))
    gs = per_group * bs
    heads = weights["kv_f_b"].shape[0]
    dff = weights["f_w_out"].shape[1] * S
    nkv = 2 * d + heads
    nkv_s = weights["kv_w"].shape[1]
    nkvp = _round_up(2 * d + LANES, 768) if 2 * d + LANES > 768 else 2 * d + LANES
    alpha = (2 * depth) ** 0.25
    xi, yi, ci = _coords()
    shard = 2 * xi + yi
    shard_arr = jnp.reshape(shard, (1,)).astype(jnp.int32)
    shard_core_arr = jnp.stack([shard, ci]).astype(jnp.int32)

    def mixer_keys(l):
        if l < la:
            return [("a_w_in", l), ("a_w_out", l)]
        return ([("kv_w", 0)] if l == la else []) + [("b_w_in", l - la), ("b_w_out", l - la)]

    def ffn_keys(l):
        return [("f_w_in", l), ("f_w_out", l)]

    assert la >= 1
    groups = [mixer_keys(0)[:1], mixer_keys(0)[1:] + ffn_keys(0)]
    groups += [mixer_keys(l) + ffn_keys(l) for l in range(1, depth)]
    keys = [kl for grp in groups for kl in grp]
    shape2d = {(k, i): weights[k].shape[-2:] for k, i in keys}
    small_local = [weights[k] for k, _ in SMALL_SHARDED]
    sm_flat = _pack(small_local, 2 * SUBLANES * LANES).reshape(1, 2, -1, LANES)
    sm_slot = lax.dynamic_update_slice_in_dim(lax.empty((S,) + sm_flat.shape[1:], F32), sm_flat, shard, axis=0)
    def cast_group(grp):
        return [cast_into_slot(shard_arr, weights[k], i, f"cast_{k}{i}") for k, i in grp]

    ici_plan = lambda g, refs, cnt: _gather_ici_plan(refs, cnt)
    first, _ = _split_start("gather_start_first", [cast_group(groups[0]) + [sm_slot]], ici_plan)
    rest, start_token = _split_start("gather_start", [cast_group(grp) for grp in groups[1:]], ici_plan)
    in_flight = first + rest
    gw = {}

    def use(g, arrs):
        for kl, a in zip(groups[g], arrs):
            rows, cols = shape2d[kl]
            gw[kl] = a.reshape(S * rows, cols) if kl[0] in OUT_TYPE else a.reshape(S, rows, cols)
        return arrs

    def fetch(g, after):
        arrs = _split_wait(f"gather_wait_{g}", in_flight[g], after, _gather_ici_plan)
        return use(g, forward_to_sibling(arrs, f"gather_fwd_{g}"))

    forwarding = {}

    def fetch_early(g, after):
        arrs = _split_wait(f"gather_wait_{g}", in_flight[g], after, _gather_ici_plan)
        started, token = _split_start(f"gather_fwd_start_{g}", [arrs], lambda _, refs, cnt: _forward_plan(refs, cnt))
        forwarding[g] = started[0]
        return token[0:1, 0:1]

    def fetch_end(g, after):
        return use(g, _split_wait(f"gather_fwd_wait_{g}", forwarding[g], after, _forward_plan))

    sm_all = fetch(0, start_token)[-1].reshape(S, -1)
    small_full = {}
    per_shard = [_unpack(sm_all[s], [a.shape for a in small_local]) for s in range(S)]
    for idx, (k, axis) in enumerate(SMALL_SHARDED):
        small_full[k] = jnp.concatenate([per_shard[s][idx] for s in range(S)], axis=axis)
    fb_pad = jnp.pad(weights["kv_f_b"], (0, LANES - heads)).reshape(1, LANES)
    wr_g = [_block_diag(weights["a_w_r"][l], per_group) for l in range(la)]
    wi_g = [_block_diag(weights["a_w_i"][l], per_group) for l in range(la)]
    row = lambda v: v.reshape(1, -1)

    h, hb = embed_fwd(small_full["meta"], x[0], "embed")
    saved = []
    kvz = ct_pad = None
    _, tp, _ = _attn_geometry(nm + seq)
    t = nm + seq
    for l in range(depth):
        sv = {"hb_in": hb}
        if l > 0:
            (fetch_end if l + 1 in forwarding else fetch)(l + 1, hb)
        if l < la:
            gr = mm_in(hb, gw[("a_w_in", l)], F32, f"a{l}_in")
            rc, rcb = a_conv_fwd(gr, small_full["a_conv_w"][l], row(small_full["a_conv_b"][l]), f"a{l}_conv")
            r_pre, i_pre = mm_bd(rcb, wr_g[l], wi_g[l], f"a{l}_gates")
            hs, gb = a_elem_fwd(gr, rc, r_pre, i_pre, row(small_full["a_b_r"][l]), row(small_full["a_b_i"][l]),
                                row(small_full["a_lambda"][l]), f"a{l}_lru")
            if l == 0:
                fetch(1, gb)
            mixed, w_mix = gb, gw[("a_w_out", l)]
            sv.update(gr=gr, rc=rc, rcb=rcb, r_pre=r_pre, i_pre=i_pre, hs=hs, gb=gb)
        else:
            j = l - la
            if j == 0:
                kv_cat = jnp.moveaxis(gw[("kv_w", 0)], 0, 1).reshape(d, S * nkv_s)
                kv_pad = jnp.pad(kv_cat, ((0, 0), (0, nkvp - nkv))).reshape(1, d, nkvp)
                kvz = mm_in(hb, kv_pad, F32, "kv_proj")
                cum = kv_fwd(kvz, fb_pad, d, "kv_forget")
                ct_pad = jnp.pad(cum[:, :heads].T, ((0, 0), (0, tp - t))).reshape(heads, 1, tp)
                kv_hb = hb
            qg = mm_in(hb, gw[("b_w_in", j)], F32, f"b{j}_in")
            o, mob = attn_fwd(qg, kvz, ct_pad, d, heads, f"b{j}_attn")
            mixed, w_mix = mob, gw[("b_w_out", j)]
            sv.update(qg=qg, o=o, mob=mob)
        h1, h1b, xh1, rs1 = mm_out_ln(mixed, w_mix, h, row(weights["ln1_g"][l]), row(weights["ln1_b"][l]), alpha,
                                      f"mix{l}_out_ln1")
        f_bias = row(weights["f_conv_b"][l])
        if l >= 1 and l + 2 < len(groups):
            f_bias = f_bias + fetch_early(l + 2, h1b)
        zf = mm_in(h1b, gw[("f_w_in", l)], F32, f"f{l}_in")
        ffb = f_elem_fwd(zf, small_full["f_conv_w"][l], f_bias, f"f{l}_act")
        h2, h2b, xh2, rs2 = mm_out_ln(ffb, gw[("f_w_out", l)], h1, row(weights["ln2_g"][l]), row(weights["ln2_b"][l]),
                                      alpha, f"f{l}_out_ln2")
        sv.update(h1b=h1b, xh1=xh1, rs1=rs1, zf=zf, ffb=ffb, xh2=xh2, rs2=rs2)
        saved.append(sv)
        h, hb = h2, h2b
    loss11, dy = loss_fwd_bwd(h, loss_target[0], nm, "loss")

    grads = {}

    def by_owner(kl, g3):
        rows, cols = shape2d[kl]
        grads[kl] = g3.reshape(S, 2, rows // 2, cols)

    reducing = []

    def send_grads(g, names, arrays):
        lands = [lax.empty((7,) + a.shape[2:], a.dtype) for a in arrays]
        started, token = _split_start(f"grad_a2a_start_{g}", [list(arrays) + lands],
                                      lambda _, refs, cnt: _all_to_all_plan(refs, cnt))
        reducing.append((names, started[0]))
        return token[0:1, 0:1]

    def after_start(vec, zero):
        return vec if zero is None else vec + zero

    pin = None

    g_small = {}
    per_layer = {k: [None] * n for k, n in (
        ("a_conv_w", la), ("a_conv_b", la), ("a_w_r", la), ("a_b_r", la), ("a_w_i", la), ("a_b_i", la),
        ("a_lambda", la), ("f_conv_w", depth), ("f_conv_b", depth), ("ln1_g", depth), ("ln1_b", depth),
        ("ln2_g", depth), ("ln2_b", depth))}
    adds = [(dy, 1.0)]
    dks, dvs, dcs = [], [], []
    for l in reversed(range(depth)):
        sv = saved[l]
        ds2, ds2b, dg2, db2 = ln_bwd(adds, sv["xh2"], sv["rs2"], after_start(row(weights["ln2_g"][l]), pin),
                                     f"ln2_{l}_bwd")
        pin = None
        per_layer["ln2_g"][l], per_layer["ln2_b"][l] = dg2[0], db2[0]
        dff_v = mm_out_nt(ds2b, gw[("f_w_out", l)], f"f{l}_out_dx")
        by_owner(("f_w_out", l), mm_tn(sv["ffb"], [ds2b], 1, f"f{l}_out_dw"))
        dzg, dzv, dwg, dwv, dbg, dbv = f_elem_bwd(sv["zf"], dff_v, small_full["f_conv_w"][l],
                                                  row(weights["f_conv_b"][l]), f"f{l}_act_bwd")
        per_layer["f_conv_w"][l] = jnp.concatenate([dwg, dwv], axis=1)
        per_layer["f_conv_b"][l] = jnp.concatenate([dbg, dbv], axis=1)[0]
        dh1_f = mm_in_nt([dzg, dzv], gw[("f_w_in", l)], f"f{l}_in_dx")
        by_owner(("f_w_in", l), mm_tn(sv["h1b"], [dzg, dzv], S, f"f{l}_in_dw"))
        w_mix = gw[("a_w_out", l)] if l < la else gw[("b_w_out", l - la)]
        ds1, ds1b, dg1, db1, d_mixed = ln_bwd([(ds2, alpha), (dh1_f, 1.0)], sv["xh1"], sv["rs1"],
                                              row(weights["ln1_g"][l]), f"ln1_{l}_bwd_mix_dx", w=w_mix)
        per_layer["ln1_g"][l], per_layer["ln1_b"][l] = dg1[0], db1[0]
        if l < la:
            dgv = d_mixed
            by_owner(("a_w_out", l), mm_tn(sv["gb"], [ds1b], 1, f"a{l}_out_dw"))
            if l == 0:
                pin = send_grads(1, groups[1], [grads[kl] for kl in groups[1]])
            dgate_b, drp_b, dip_b, drc_d, dlam, dbr, dbi = a_elem_bwd(
                dgv, sv["gr"], sv["rc"], sv["r_pre"], sv["i_pre"], sv["hs"], row(small_full["a_b_r"][l]),
                row(small_full["a_b_i"][l]), after_start(row(small_full["a_lambda"][l]), pin), f"a{l}_lru_bwd")
            pin = None
            drc_g = mm_bd_nt(drp_b, dip_b, wr_g[l], wi_g[l], f"a{l}_gates_dx")
            dwr_g, dwi_g = mm_bd_tn(sv["rcb"], drp_b, dip_b, gs, f"a{l}_gates_dw")
            dgr_b, dcw, dcb = a_conv_bwd(drc_d, drc_g, sv["gr"], small_full["a_conv_w"][l], dgate_b,
                                         f"a{l}_conv_bwd")
            per_layer["a_w_r"][l] = _block_diag_extract(dwr_g, per_group, bs)
            per_layer["a_w_i"][l] = _block_diag_extract(dwi_g, per_group, bs)
            per_layer["a_lambda"][l], per_layer["a_b_r"][l], per_layer["a_b_i"][l] = dlam[0], dbr[0], dbi[0]
            per_layer["a_conv_w"][l], per_layer["a_conv_b"][l] = dcw, dcb[0]
            by_owner(("a_w_in", l), mm_tn(sv["hb_in"], [dgr_b], S, f"a{l}_in_dw"))
            gate_w = gw[("a_w_in", l)]
            if l == 0:
                zero = send_grads(0, groups[0], [grads[kl] for kl in groups[0]])
                gate_w = gate_w + zero.astype(BF16)[None]
            dh_m = mm_in_nt([dgr_b], gate_w, f"a{l}_in_dx")
        else:
            j = l - la
            dmo = d_mixed
            by_owner(("b_w_out", j), mm_tn(sv["mob"], [ds1b], 1, f"b{j}_out_dw"))
            dq_b, dog_b, dk, dv, dct = attn_bwd(dmo, sv["qg"], kvz, sv["o"], ct_pad, d, heads, f"b{j}_attn_bwd")
            dks.append(dk)
            dvs.append(dv)
            dcs.append(jnp.pad(dct[:, 0, :t].T, ((0, 0), (0, LANES - heads))))
            dh_m = mm_in_nt([dq_b, dog_b], gw[("b_w_in", j)], f"b{j}_in_dx")
            by_owner(("b_w_in", j), mm_tn(sv["hb_in"], [dq_b, dog_b], S, f"b{j}_in_dw"))
        adds = [(ds1, alpha), (dh_m, 1.0)]
        if l == la:
            dzf_b, dfb = kv_bwd(dcs, kvz, fb_pad, d, "kv_forget_bwd")
            dk_b = add_cast(dks[0], dks[1], "kv_dk") if lb == 2 else None
            dv_b = add_cast(dvs[0], dvs[1], "kv_dv") if lb == 2 else None
            dz_kv = jnp.concatenate([dk_b, dv_b, dzf_b, jnp.zeros((t, nkvp - 2 * d - LANES), BF16)], axis=1)
            dh_kv = mm_in_nt([dz_kv], kv_pad, "kv_proj_dx")
            kv_dw = mm_tn(kv_hb, [dz_kv], 1, "kv_proj_dw")
            by_owner(("kv_w", 0), jnp.moveaxis(kv_dw[0, :, :nkv].reshape(d, S, nkv_s), 1, 0))
            g_small["kv_f_b"] = dfb[0, :heads]
            adds.append((dh_kv, 1.0))
        if l > 0:
            pin = send_grads(l + 1, groups[l + 1], [grads[kl] for kl in groups[l + 1]])
    g_meta, g_x = embed_bwd(adds, nm, "embed_bwd")

    small_names = list(SMALL_REPLICATED) + [k for k, _ in SMALL_SHARDED]
    g_small["meta"] = g_meta
    for k, vals in per_layer.items():
        g_small[k] = jnp.stack(vals)
    small_shapes = {k: (weights[k].shape if k in SMALL_REPLICATED else g_small[k].shape) for k in small_names}
    sm_g = _pack([g_small[k].reshape(small_shapes[k]) for k in small_names] + [loss11.reshape(1)],
                 S * 2 * SUBLANES * LANES)
    sm_g = sm_g.reshape(S, 2, -1, LANES)
    send_grads(len(groups), [("small", 0)], [sm_g])

    fin = {"small": lax.empty(sm_g.shape, F32)}
    for kl in keys:
        n_stack = weights[kl[0]].shape[0] if weights[kl[0]].ndim == 3 else 1
        rows, cols = shape2d[kl]
        fin.setdefault(kl[0], lax.empty((n_stack, 2, rows // 2, cols), F32))
    for g, (names_g, started) in enumerate(reducing):
        arrs = _split_wait(f"grad_a2a_wait_{g}", started, g_x, _all_to_all_plan)
        half = len(names_g)
        for i, (kl, cs, rv) in enumerate(zip(names_g, arrs[:half], arrs[half:])):
            fin[kl[0]] = owner_sum(shard_core_arr, cs, rv, fin[kl[0]], None if kl[0] == "small" else kl[1],
                                   f"owner_sum_{g}_{i}")
    names = list(BIG) + ["small"]
    joined = dict(zip(names, join_halves([fin[k] for k in BIG], fin["small"], "grad_join")))
    sm_red = joined["small"].reshape(-1)

    out_g, out_d, out_m, out_v = {}, {}, {}, {}
    for k in BIG:
        w2 = weights[k].reshape(-1, weights[k].shape[-1])
        g2 = joined[k].reshape(w2.shape)
        dlt, mn, vn, g_out = adamw(w2, g2, mom_m[k].reshape(w2.shape), mom_v[k].reshape(w2.shape), "adamw_" + k,
                                   emit_g=True)
        shp = weights[k].shape
        out_g[k], out_d[k], out_m[k], out_v[k] = g_out.reshape(shp), dlt.reshape(shp), mn.reshape(shp), vn.reshape(shp)
    *small_sums, loss_sum = _unpack(sm_red, [small_shapes[k] for k in small_names] + [(1,)])
    sm_vals = dict(zip(small_names, small_sums))
    local_small = {}
    for k in SMALL_REPLICATED:
        local_small[k] = sm_vals[k]
    for k, axis in SMALL_SHARDED:
        size = weights[k].shape[axis]
        local_small[k] = lax.dynamic_slice_in_dim(sm_vals[k], shard * size, size, axis=axis)
    for k in small_names:
        shp = weights[k].shape
        two_d = (-1, shp[-1]) if len(shp) > 1 else (1, -1)
        dlt, mn, vn = adamw(weights[k].reshape(two_d), local_small[k].reshape(two_d), mom_m[k].reshape(two_d),
                            mom_v[k].reshape(two_d), "adamw_" + k)
        out_g[k], out_d[k], out_m[k], out_v[k] = local_small[k], dlt.reshape(shp), mn.reshape(shp), vn.reshape(shp)

    return (loss_sum[0], g_x[None], *[out_g[k] for k in WEIGHT_ORDER], *[out_d[k] for k in WEIGHT_ORDER],
            *[out_m[k] for k in WEIGHT_ORDER], *[out_v[k] for k in WEIGHT_ORDER])
```

```python
import functools
import math

import jax
import jax.numpy as jnp
from jax import lax
from jax.experimental import pallas as pl
from jax.experimental.pallas import tpu as pltpu

F32 = jnp.float32
BF16 = jnp.bfloat16

LRU_C = 8.0
LN_EPS = 1e-5
ADAM_LR = 0.001
ADAM_B1 = 0.9
ADAM_B2 = 0.999
ADAM_EPS = 1e-08
ADAM_WD = 0.01
ADAM_STEP = 10

LANES = 128
SUBLANES = 8
V7X_VMEM_BYTES = 64 * 1024 * 1024
VMEM_LIMIT = V7X_VMEM_BYTES * 7 // 8
N_SHARDS = 4
GELU_C0 = math.sqrt(2.0 / math.pi)
GELU_C1 = 0.044715
NEG_BIG = -1e30


def _cp(*sem):
    return pltpu.CompilerParams(dimension_semantics=tuple(sem), vmem_limit_bytes=VMEM_LIMIT)


def _tile(n, cap, mult=LANES):
    best = None
    d = mult
    while d <= min(n, cap):
        if n % d == 0:
            best = d
        d += mult
    return n if best is None else best


def _row_block(t):
    if t % 3 == 0 and (t // 3) % 16 == 0:
        return t // 3
    return t


def _round_up(n, m):
    return (n + m - 1) // m * m


def _sigmoid(v):
    return 1.0 / (1.0 + jnp.exp(-v))


def _softplus(v):
    return jnp.maximum(v, 0.0) + jnp.log(1.0 + jnp.exp(-jnp.abs(v)))


def _gelu_parts(v):
    v2 = v * v
    w = 0.5 + 0.5 * jnp.tanh(v * (GELU_C0 + (GELU_C0 * GELU_C1) * v2))
    dg = w + (v * (w * (1.0 - w))) * (2.0 * GELU_C0 + (6.0 * GELU_C0 * GELU_C1) * v2)
    return v * w, dg


def _gelu(v):
    return v * (0.5 + 0.5 * jnp.tanh(v * (GELU_C0 + (GELU_C0 * GELU_C1) * (v * v))))


def _neg_expm1(v):
    series = -v * (1.0 + 0.5 * v * (1.0 + (v / 3.0) * (1.0 + 0.25 * v)))
    return jnp.where(v > -0.05, series, 1.0 - jnp.exp(v))


def _shift_down(v, j):
    if j == 0:
        return v
    rows = lax.broadcasted_iota(jnp.int32, v.shape, 0)
    return jnp.where(rows >= j, pltpu.roll(v, j, 0), 0.0)


def _shift_up(v, j):
    if j == 0:
        return v
    n = v.shape[0]
    rows = lax.broadcasted_iota(jnp.int32, v.shape, 0)
    return jnp.where(rows < n - j, pltpu.roll(v, n - j, 0), 0.0)


def _scan_rows(a_ref, b_ref, out_ref, n_rows, width, reverse):
    n_groups = n_rows // SUBLANES
    rows = lax.broadcasted_iota(jnp.int32, (SUBLANES, width), 0)
    edge = 0 if reverse else SUBLANES - 1

    def body(g, carry):
        grp = (n_groups - 1 - g) if reverse else g
        off = pl.multiple_of(grp * SUBLANES, SUBLANES)
        b = b_ref[pl.ds(off, SUBLANES), :]
        a = None if a_ref is None else a_ref[pl.ds(off, SUBLANES), :]
        for d in (1, 2, 4):
            if reverse:
                keep = rows < SUBLANES - d
                sh = SUBLANES - d
            else:
                keep = rows >= d
                sh = d
            b_s = jnp.where(keep, pltpu.roll(b, sh, 0), 0.0)
            if a is None:
                b = b + b_s
            else:
                a_s = jnp.where(keep, pltpu.roll(a, sh, 0), 1.0)
                b = a * b_s + b
                a = a * a_s
        h = b + carry if a is None else b + a * carry
        out_ref[pl.ds(off, SUBLANES), :] = h
        return jnp.sum(jnp.where(rows == edge, h, 0.0), axis=0, keepdims=True)

    lax.fori_loop(0, n_groups, body, jnp.zeros((1, width), F32), unroll=2)


def mm_in(x, w, out_dtype, name):
    t, k = x.shape
    s_n, _, ns = w.shape
    tn = _tile(ns, 1408)
    nj = ns // tn
    rb = _row_block(t)

    def body(x_ref, w_ref, o_ref):
        o_ref[...] = jnp.dot(x_ref[...], w_ref[...], preferred_element_type=F32).astype(o_ref.dtype)

    return pl.pallas_call(
        body, name=name, grid=(s_n, nj, t // rb),
        in_specs=[pl.BlockSpec((rb, k), lambda s, j, r: (r, 0)),
                  pl.BlockSpec((None, k, tn), lambda s, j, r: (s, 0, j))],
        out_specs=pl.BlockSpec((rb, tn), lambda s, j, r: (r, s * nj + j)),
        out_shape=jax.ShapeDtypeStruct((t, s_n * ns), out_dtype),
        compiler_params=_cp("parallel", "parallel", "parallel"))(x, w)


def _part_map(p, per_part, nj, lead):
    def index(*grid):
        s, j = grid[-2], grid[-1]
        mine = s // per_part == p
        col = jnp.where(mine, (s - p * per_part) * nj + j, 0)
        return (grid[0], col) if lead else (0, col)
    return index


def mm_in_nt(dy_parts, w, name):
    n_parts = len(dy_parts)
    t = dy_parts[0].shape[0]
    s_n, k, ns = w.shape
    per_part = s_n // n_parts
    tn = _tile(ns, 1408)
    nj = ns // tn
    rb = _row_block(t)

    def body(*refs):
        w_ref, o_ref = refs[n_parts:]

        @pl.when((pl.program_id(1) == 0) & (pl.program_id(2) == 0))
        def _():
            o_ref[...] = jnp.zeros_like(o_ref)

        for p in range(n_parts):
            @pl.when(pl.program_id(1) // per_part == p)
            def _():
                o_ref[...] += lax.dot_general(refs[p][...], w_ref[...], (((1,), (1,)), ((), ())),
                                              preferred_element_type=F32)

    return pl.pallas_call(
        body, name=name, grid=(t // rb, s_n, nj),
        in_specs=[pl.BlockSpec((rb, tn), _part_map(p, per_part, nj, True)) for p in range(n_parts)]
        + [pl.BlockSpec((None, k, tn), lambda r, s, j: (s, 0, j))],
        out_specs=pl.BlockSpec((rb, k), lambda r, s, j: (r, 0)),
        out_shape=jax.ShapeDtypeStruct((t, k), F32),
        compiler_params=_cp("parallel", "arbitrary", "arbitrary"))(*dy_parts, w)


def mm_out_nt(dy, w, name):
    t, n = dy.shape
    k = w.shape[0]
    rb = _row_block(t)

    def body(dy_ref, w_ref, o_ref):
        o_ref[...] = lax.dot_general(dy_ref[...], w_ref[...], (((1,), (1,)), ((), ())),
                                     preferred_element_type=F32)

    return pl.pallas_call(
        body, name=name, grid=(t // rb,),
        in_specs=[pl.BlockSpec((rb, n), lambda r: (r, 0)), pl.BlockSpec((k, n), lambda r: (0, 0))],
        out_specs=pl.BlockSpec((rb, k), lambda r: (r, 0)),
        out_shape=jax.ShapeDtypeStruct((t, k), F32),
        compiler_params=_cp("parallel"))(dy, w)


def mm_tn(x, dy_parts, s_n, name):
    n_parts = len(dy_parts)
    t, kb = x.shape
    nb = dy_parts[0].shape[1] * n_parts // s_n
    per_part = max(s_n // n_parts, 1)
    tk = _tile(kb, 1408)
    tn = _tile(nb, 1408)
    nkb, nnb = kb // tk, nb // tn
    tn_dims = (((0,), (0,)), ((), ()))

    def body(*refs):
        x_ref, o_ref = refs[0], refs[-1]
        for p in range(n_parts):
            @pl.when(pl.program_id(1) // per_part == p)
            def _():
                o_ref[...] = lax.dot_general(x_ref[...], refs[1 + p][...], tn_dims,
                                             preferred_element_type=F32).astype(o_ref.dtype)

    return pl.pallas_call(
        body, name=name, grid=(nkb, s_n, nnb),
        in_specs=[pl.BlockSpec((t, tk), lambda a, s, b: (0, a))]
        + [pl.BlockSpec((t, tn), _part_map(p, per_part, nnb, False)) for p in range(n_parts)],
        out_specs=pl.BlockSpec((None, tk, tn), lambda a, s, b: (s, a, b)),
        out_shape=jax.ShapeDtypeStruct((s_n, kb, nb), BF16),
        compiler_params=_cp("parallel", "parallel", "parallel"))(x, *dy_parts)


def mm_bd(x, wr, wi, name):
    t, _ = x.shape
    g_n, gs, _ = wr.shape
    rb = _row_block(t)

    def body(x_ref, wr_ref, wi_ref, r_ref, i_ref):
        xv = x_ref[...]
        r_ref[...] = jnp.dot(xv, wr_ref[...], preferred_element_type=F32)
        i_ref[...] = jnp.dot(xv, wi_ref[...], preferred_element_type=F32)

    blk = pl.BlockSpec((rb, gs), lambda g, r: (r, g))
    wspec = pl.BlockSpec((None, gs, gs), lambda g, r: (g, 0, 0))
    return pl.pallas_call(
        body, name=name, grid=(g_n, t // rb), in_specs=[blk, wspec, wspec], out_specs=[blk, blk],
        out_shape=[jax.ShapeDtypeStruct((t, g_n * gs), F32)] * 2,
        compiler_params=_cp("parallel", "parallel"))(x, wr, wi)


def mm_bd_nt(dr, di, wr, wi, name):
    t, _ = dr.shape
    g_n, gs, _ = wr.shape
    rb = _row_block(t)
    nt = (((1,), (1,)), ((), ()))

    def body(dr_ref, di_ref, wr_ref, wi_ref, o_ref):
        o_ref[...] = (lax.dot_general(dr_ref[...], wr_ref[...], nt, preferred_element_type=F32)
                      + lax.dot_general(di_ref[...], wi_ref[...], nt, preferred_element_type=F32))

    blk = pl.BlockSpec((rb, gs), lambda g, r: (r, g))
    wspec = pl.BlockSpec((None, gs, gs), lambda g, r: (g, 0, 0))
    return pl.pallas_call(
        body, name=name, grid=(g_n, t // rb), in_specs=[blk, blk, wspec, wspec], out_specs=blk,
        out_shape=jax.ShapeDtypeStruct((t, g_n * gs), F32),
        compiler_params=_cp("parallel", "parallel"))(dr, di, wr, wi)


def mm_bd_tn(x, dr, di, gs, name):
    t, w = x.shape
    g_n = w // gs
    tn_dims = (((0,), (0,)), ((), ()))

    def body(x_ref, dr_ref, di_ref, gr_ref, gi_ref):
        xv = x_ref[...]
        gr_ref[...] = lax.dot_general(xv, dr_ref[...], tn_dims, preferred_element_type=F32)
        gi_ref[...] = lax.dot_general(xv, di_ref[...], tn_dims, preferred_element_type=F32)

    blk = pl.BlockSpec((t, gs), lambda g: (0, g))
    ospec = pl.BlockSpec((None, gs, gs), lambda g: (g, 0, 0))
    return pl.pallas_call(
        body, name=name, grid=(g_n,), in_specs=[blk, blk, blk], out_specs=[ospec, ospec],
        out_shape=[jax.ShapeDtypeStruct((g_n, gs, gs), F32)] * 2,
        compiler_params=_cp("parallel"))(x, dr, di)


def embed_fwd(meta, x2d, name):
    nm, d = meta.shape
    seq = x2d.shape[0]
    t = nm + seq
    cb = _tile(d, 256)

    def body(m_ref, x_ref, h_ref, hb_ref):
        h_ref[pl.ds(0, nm), :] = m_ref[...]
        h_ref[pl.ds(nm, seq), :] = x_ref[...]
        hb_ref[pl.ds(0, nm), :] = m_ref[...].astype(BF16)
        hb_ref[pl.ds(nm, seq), :] = x_ref[...].astype(BF16)

    return pl.pallas_call(
        body, name=name, grid=(d // cb,),
        in_specs=[pl.BlockSpec((nm, cb), lambda j: (0, j)), pl.BlockSpec((seq, cb), lambda j: (0, j))],
        out_specs=[pl.BlockSpec((t, cb), lambda j: (0, j))] * 2,
        out_shape=[jax.ShapeDtypeStruct((t, d), F32), jax.ShapeDtypeStruct((t, d), BF16)],
        compiler_params=_cp("parallel"))(meta, x2d)


def embed_bwd(adds, nm, name):
    t, d = adds[0][0].shape
    seq = t - nm
    cb = _tile(d, 256)
    scales = [s for _, s in adds]
    n = len(adds)

    def body(*refs):
        tot = None
        for r, sc in zip(refs[:n], scales):
            term = r[...] if sc == 1.0 else sc * r[...]
            tot = term if tot is None else tot + term
        gm_ref, gx_ref = refs[n], refs[n + 1]
        gm_ref[...] = tot[0:nm]
        gx_ref[...] = tot[nm:t]

    return pl.pallas_call(
        body, name=name, grid=(d // cb,),
        in_specs=[pl.BlockSpec((t, cb), lambda j: (0, j))] * n,
        out_specs=[pl.BlockSpec((nm, cb), lambda j: (0, j)), pl.BlockSpec((seq, cb), lambda j: (0, j))],
        out_shape=[jax.ShapeDtypeStruct((nm, d), F32), jax.ShapeDtypeStruct((seq, d), F32)],
        compiler_params=_cp("parallel"))(*[a for a, _ in adds])


def loss_fwd_bwd(h, tgt, nm, name):
    t, d = h.shape
    seq = t - nm
    cb = _tile(d, 256)
    inv_d = 1.0 / d

    def body(h_ref, t_ref, loss_ref, dy_ref):
        @pl.when(pl.program_id(0) == 0)
        def _():
            loss_ref[...] = jnp.zeros_like(loss_ref)
        err = h_ref[pl.ds(nm, seq), :] - t_ref[...]
        dy_ref[pl.ds(0, nm), :] = jnp.zeros((nm, cb), F32)
        dy_ref[pl.ds(nm, seq), :] = err * inv_d
        loss_ref[...] += (0.5 * inv_d) * jnp.sum(err * err, keepdims=True)

    return pl.pallas_call(
        body, name=name, grid=(d // cb,),
        in_specs=[pl.BlockSpec((t, cb), lambda j: (0, j)), pl.BlockSpec((seq, cb), lambda j: (0, j))],
        out_specs=[pl.BlockSpec((1, 1), lambda j: (0, 0)), pl.BlockSpec((t, cb), lambda j: (0, j))],
        out_shape=[jax.ShapeDtypeStruct((1, 1), F32), jax.ShapeDtypeStruct((t, d), F32)],
        compiler_params=_cp("arbitrary"))(h, tgt)


def mm_out_ln(x, w, h, g, b, alpha, name):
    t, d = h.shape
    k = x.shape[1]
    rb = _row_block(t)

    def body(x_ref, w_ref, h_ref, g_ref, b_ref, y_ref, yb_ref, xh_ref, rs_ref):
        s = alpha * h_ref[...] + jnp.dot(x_ref[...], w_ref[...], preferred_element_type=F32)
        mu = jnp.mean(s, axis=-1, keepdims=True)
        c = s - mu
        var = jnp.mean(c * c, axis=-1, keepdims=True)
        rstd = lax.rsqrt(var + LN_EPS)
        xh = c * rstd
        y = xh * g_ref[...] + b_ref[...]
        y_ref[...] = y
        yb_ref[...] = y.astype(BF16)
        xh_ref[...] = xh
        rs_ref[...] = rstd

    row = pl.BlockSpec((rb, d), lambda r: (r, 0))
    vec = pl.BlockSpec((1, d), lambda r: (0, 0))
    return pl.pallas_call(
        body, name=name, grid=(t // rb,),
        in_specs=[pl.BlockSpec((rb, k), lambda r: (r, 0)), pl.BlockSpec((k, d), lambda r: (0, 0)), row, vec, vec],
        out_specs=[row, row, row, pl.BlockSpec((rb, 1), lambda r: (r, 0))],
        out_shape=[jax.ShapeDtypeStruct((t, d), F32), jax.ShapeDtypeStruct((t, d), BF16),
                   jax.ShapeDtypeStruct((t, d), F32), jax.ShapeDtypeStruct((t, 1), F32)],
        compiler_params=_cp("parallel"))(x, w, h, g, b)


def ln_bwd(adds, xhat, rstd, g, name, w=None):
    t, d = xhat.shape
    rb = _row_block(t)
    scales = [s for _, s in adds]
    n = len(adds)
    n_in = n + 3 + (w is not None)

    def body(*refs):
        xh_ref, rs_ref, g_ref = refs[n:n + 3]
        ds_ref, dsb_ref, dg_ref, db_ref = refs[n_in:n_in + 4]
        dy = None
        for r, sc in zip(refs[:n], scales):
            term = r[...] if sc == 1.0 else sc * r[...]
            dy = term if dy is None else dy + term

        @pl.when(pl.program_id(0) == 0)
        def _():
            dg_ref[...] = jnp.zeros_like(dg_ref)
            db_ref[...] = jnp.zeros_like(db_ref)

        xh = xh_ref[...]
        dxh = dy * g_ref[...]
        m1 = jnp.mean(dxh, axis=-1, keepdims=True)
        m2 = jnp.mean(dxh * xh, axis=-1, keepdims=True)
        ds = rs_ref[...] * (dxh - m1 - xh * m2)
        ds_b = ds.astype(BF16)
        ds_ref[...] = ds
        dsb_ref[...] = ds_b
        dg_ref[...] += jnp.sum(dy * xh, axis=0, keepdims=True)
        db_ref[...] += jnp.sum(dy, axis=0, keepdims=True)
        if w is not None:
            refs[-1][...] = lax.dot_general(ds_b, refs[n + 3][...], (((1,), (1,)), ((), ())),
                                            preferred_element_type=F32)

    row = pl.BlockSpec((rb, d), lambda r: (r, 0))
    vec = pl.BlockSpec((1, d), lambda r: (0, 0))
    in_specs = [row] * n + [row, pl.BlockSpec((rb, 1), lambda r: (r, 0)), vec]
    out_specs = [row, row, vec, vec]
    out_shape = [jax.ShapeDtypeStruct((t, d), F32), jax.ShapeDtypeStruct((t, d), BF16),
                 jax.ShapeDtypeStruct((1, d), F32), jax.ShapeDtypeStruct((1, d), F32)]
    operands = [a for a, _ in adds] + [xhat, rstd, g]
    if w is not None:
        k = w.shape[0]
        in_specs.append(pl.BlockSpec((k, d), lambda r: (0, 0)))
        out_specs.append(pl.BlockSpec((rb, k), lambda r: (r, 0)))
        out_shape.append(jax.ShapeDtypeStruct((t, k), F32))
        operands.append(w)
    return pl.pallas_call(
        body, name=name, grid=(t // rb,), in_specs=in_specs, out_specs=out_specs, out_shape=out_shape,
        compiler_params=_cp("arbitrary"))(*operands)


def _conv_fwd_val(xv, w_ref, b_ref, width):
    y = b_ref[...]
    for j in range(width):
        y = y + _shift_down(xv, j) * w_ref[pl.ds(width - 1 - j, 1), :]
    return y


def _conv_bwd_val(dout, xv, w_ref, width):
    dx = None
    dws = [None] * width
    for j in range(width):
        k = width - 1 - j
        term = _shift_up(dout, j) * w_ref[pl.ds(k, 1), :]
        dx = term if dx is None else dx + term
        dws[k] = jnp.sum(dout * _shift_down(xv, j), axis=0, keepdims=True)
    return dx, dws, jnp.sum(dout, axis=0, keepdims=True)


def a_conv_fwd(gr, cw, cbias, name):
    t, two_dr = gr.shape
    dr = two_dr // 2
    width = cw.shape[0]
    cb = _tile(dr, 256)
    off = dr // cb

    def body(x_ref, w_ref, b_ref, rc_ref, rcb_ref):
        y = _conv_fwd_val(x_ref[...], w_ref, b_ref, width)
        rc_ref[...] = y
        rcb_ref[...] = y.astype(BF16)

    return pl.pallas_call(
        body, name=name, grid=(dr // cb,),
        in_specs=[pl.BlockSpec((t, cb), lambda j: (0, off + j)),
                  pl.BlockSpec((width, cb), lambda j: (0, j)), pl.BlockSpec((1, cb), lambda j: (0, j))],
        out_specs=[pl.BlockSpec((t, cb), lambda j: (0, j))] * 2,
        out_shape=[jax.ShapeDtypeStruct((t, dr), F32), jax.ShapeDtypeStruct((t, dr), BF16)],
        compiler_params=_cp("parallel"))(gr, cw, cbias)


def a_conv_bwd(drc_a, drc_b, gr, cw, dgr, name):
    t, two_dr = gr.shape
    dr = two_dr // 2
    width = cw.shape[0]
    cb = _tile(dr, 256)
    off = dr // cb

    def body(da_ref, db_ref, x_ref, w_ref, dgr_in, dx_ref, dw_ref, dbias_ref):
        del dgr_in
        dout = da_ref[...] + db_ref[...]
        dx, dws, dbias = _conv_bwd_val(dout, x_ref[...], w_ref, width)
        dx_ref[...] = dx.astype(BF16)
        for k in range(width):
            dw_ref[pl.ds(k, 1), :] = dws[k]
        dbias_ref[...] = dbias

    col = pl.BlockSpec((t, cb), lambda j: (0, j))
    return pl.pallas_call(
        body, name=name, grid=(dr // cb,),
        in_specs=[col, col, pl.BlockSpec((t, cb), lambda j: (0, off + j)),
                  pl.BlockSpec((width, cb), lambda j: (0, j)), pl.BlockSpec(memory_space=pl.ANY)],
        out_specs=[pl.BlockSpec((t, cb), lambda j: (0, off + j)), pl.BlockSpec((width, cb), lambda j: (0, j)),
                   pl.BlockSpec((1, cb), lambda j: (0, j))],
        out_shape=[jax.ShapeDtypeStruct((t, two_dr), BF16), jax.ShapeDtypeStruct((width, dr), F32),
                   jax.ShapeDtypeStruct((1, dr), F32)],
        input_output_aliases={4: 0},
        compiler_params=_cp("parallel"))(drc_a, drc_b, gr, cw, dgr)


def _lru_gates(r_pre, i_pre, br, bi, lam):
    r = _sigmoid(r_pre + br)
    i = _sigmoid(i_pre + bi)
    sp = _softplus(-lam)
    la = -LRU_C * r * sp
    a = jnp.exp(la)
    m = jnp.sqrt(_neg_expm1(2.0 * la))
    return r, i, sp, la, a, m


def a_elem_fwd(gr, rc, r_pre, i_pre, br, bi, lam, name):
    t, dr = rc.shape
    cb = _tile(dr, 3 * LANES)
    rb = _row_block(t)
    chunks = [pl.ds(r * rb, rb) for r in range(t // rb)]

    def body(gate_ref, rc_ref, rp_ref, ip_ref, br_ref, bi_ref, lam_ref, hs_ref, g_ref, a_s, u_s):
        for rows in chunks:
            _, i, _, _, a, m = _lru_gates(rp_ref[rows, :], ip_ref[rows, :], br_ref[...], bi_ref[...], lam_ref[...])
            a_s[rows, :] = a
            u_s[rows, :] = m * (i * rc_ref[rows, :])
        _scan_rows(a_s, u_s, hs_ref, t, cb, reverse=False)
        for rows in chunks:
            g_ref[rows, :] = (_gelu(gate_ref[rows, :]) * hs_ref[rows, :]).astype(BF16)

    col = pl.BlockSpec((t, cb), lambda j: (0, j))
    vec = pl.BlockSpec((1, cb), lambda j: (0, j))
    return pl.pallas_call(
        body, name=name, grid=(dr // cb,),
        in_specs=[col, col, col, col, vec, vec, vec],
        out_specs=[col, col],
        out_shape=[jax.ShapeDtypeStruct((t, dr), F32), jax.ShapeDtypeStruct((t, dr), BF16)],
        scratch_shapes=[pltpu.VMEM((t, cb), F32), pltpu.VMEM((t, cb), F32)],
        compiler_params=_cp("parallel"))(gr, rc, r_pre, i_pre, br, bi, lam)


def a_elem_bwd(dg, gr, rc, r_pre, i_pre, hs, br, bi, lam, name):
    t, dr = rc.shape
    cb = _tile(dr, 2 * LANES)
    rb = _row_block(t)
    chunks = [pl.ds(r * rb, rb) for r in range(t // rb)]

    def body(dg_ref, gate_ref, rc_ref, rp_ref, ip_ref, hs_ref, br_ref, bi_ref, lam_ref,
             dgate_ref, dr_ref, di_ref, drc_ref, dlam_ref, dbr_ref, dbi_ref, a_s, b_s, g_s, hp_s):
        lamv = lam_ref[...]
        gates = lambda rows: _lru_gates(rp_ref[rows, :], ip_ref[rows, :], br_ref[...], bi_ref[...], lamv)
        for rows in chunks:
            a_s[rows, :] = gates(rows)[4]
            ge, dge = _gelu_parts(gate_ref[rows, :])
            dgv = dg_ref[rows, :]
            dgate_ref[rows, :] = (dgv * hs_ref[rows, :] * dge).astype(BF16)
            b_s[rows, :] = dgv * ge
        a_s[...] = _shift_up(a_s[...], 1)
        hp_s[...] = _shift_down(hs_ref[...], 1)
        _scan_rows(a_s, b_s, g_s, t, cb, reverse=True)
        dsp = dbr = dbi = jnp.zeros((1, cb), F32)
        for rows in chunks:
            r, i, sp, _, a, m = gates(rows)
            rcv = rc_ref[rows, :]
            gsum = g_s[rows, :]
            da = gsum * hp_s[rows, :]
            dm = gsum * (i * rcv)
            d_i = gsum * m * rcv
            drc_ref[rows, :] = gsum * m * i
            dla = a * da - dm * (a * a) / m
            d_r = (-LRU_C) * sp * dla
            dsp = dsp + jnp.sum((-LRU_C) * r * dla, axis=0, keepdims=True)
            d_rp = d_r * r * (1.0 - r)
            d_ip = d_i * i * (1.0 - i)
            dr_ref[rows, :] = d_rp.astype(BF16)
            di_ref[rows, :] = d_ip.astype(BF16)
            dbr = dbr + jnp.sum(d_rp, axis=0, keepdims=True)
            dbi = dbi + jnp.sum(d_ip, axis=0, keepdims=True)
        dlam_ref[...] = -dsp * _sigmoid(-lamv)
        dbr_ref[...] = dbr
        dbi_ref[...] = dbi

    col = pl.BlockSpec((t, cb), lambda j: (0, j))
    vec = pl.BlockSpec((1, cb), lambda j: (0, j))
    big_b = jax.ShapeDtypeStruct((t, dr), BF16)
    vec_s = jax.ShapeDtypeStruct((1, dr), F32)
    return pl.pallas_call(
        body, name=name, grid=(dr // cb,),
        in_specs=[col, col, col, col, col, col, vec, vec, vec],
        out_specs=[col, col, col, col, vec, vec, vec],
        out_shape=[jax.ShapeDtypeStruct((t, 2 * dr), BF16), big_b, big_b, jax.ShapeDtypeStruct((t, dr), F32),
                   vec_s, vec_s, vec_s],
        scratch_shapes=[pltpu.VMEM((t, cb), F32)] * 4,
        compiler_params=_cp("parallel"))(dg, gr, rc, r_pre, i_pre, hs, br, bi, lam)


def f_elem_fwd(z, cw, cbias, name):
    t, two_f = z.shape
    dff = two_f // 2
    width = cw.shape[0]
    cb = _tile(dff, 256)
    off = dff // cb

    def body(zg_ref, zv_ref, wg_ref, wv_ref, bg_ref, bv_ref, o_ref):
        zcg = _conv_fwd_val(zg_ref[...], wg_ref, bg_ref, width)
        zcv = _conv_fwd_val(zv_ref[...], wv_ref, bv_ref, width)
        o_ref[...] = (_gelu(zcg) * zcv).astype(BF16)

    lo = lambda j: (0, j)
    hi = lambda j: (0, off + j)
    return pl.pallas_call(
        body, name=name, grid=(dff // cb,),
        in_specs=[pl.BlockSpec((t, cb), lo), pl.BlockSpec((t, cb), hi),
                  pl.BlockSpec((width, cb), lo), pl.BlockSpec((width, cb), hi),
                  pl.BlockSpec((1, cb), lo), pl.BlockSpec((1, cb), hi)],
        out_specs=pl.BlockSpec((t, cb), lo),
        out_shape=jax.ShapeDtypeStruct((t, dff), BF16),
        compiler_params=_cp("parallel"))(z, z, cw, cw, cbias, cbias)


def f_elem_bwd(z, dff_g, cw, cbias, name):
    t, two_f = z.shape
    dff = two_f // 2
    width = cw.shape[0]
    cb = _tile(dff, 256)
    off = dff // cb

    def body(zg_ref, zv_ref, d_ref, wg_ref, wv_ref, bg_ref, bv_ref,
             dzg_ref, dzv_ref, dwg_ref, dwv_ref, dbg_ref, dbv_ref):
        zg = zg_ref[...]
        zv = zv_ref[...]
        zcg = _conv_fwd_val(zg, wg_ref, bg_ref, width)
        zcv = _conv_fwd_val(zv, wv_ref, bv_ref, width)
        ge, dge = _gelu_parts(zcg)
        dv = d_ref[...]
        dx, dws, dbias = _conv_bwd_val(dv * zcv * dge, zg, wg_ref, width)
        dzg_ref[...] = dx.astype(BF16)
        for k in range(width):
            dwg_ref[pl.ds(k, 1), :] = dws[k]
        dbg_ref[...] = dbias
        dx, dws, dbias = _conv_bwd_val(dv * ge, zv, wv_ref, width)
        dzv_ref[...] = dx.astype(BF16)
        for k in range(width):
            dwv_ref[pl.ds(k, 1), :] = dws[k]
        dbv_ref[...] = dbias

    lo = lambda j: (0, j)
    hi = lambda j: (0, off + j)
    col = pl.BlockSpec((t, cb), lo)
    wsp = pl.BlockSpec((width, cb), lo)
    vsp = pl.BlockSpec((1, cb), lo)
    return pl.pallas_call(
        body, name=name, grid=(dff // cb,),
        in_specs=[col, pl.BlockSpec((t, cb), hi), col, wsp, pl.BlockSpec((width, cb), hi),
                  vsp, pl.BlockSpec((1, cb), hi)],
        out_specs=[col, col, wsp, wsp, vsp, vsp],
        out_shape=[jax.ShapeDtypeStruct((t, dff), BF16)] * 2
        + [jax.ShapeDtypeStruct((width, dff), F32)] * 2 + [jax.ShapeDtypeStruct((1, dff), F32)] * 2,
        compiler_params=_cp("parallel"))(z, z, dff_g, cw, cw, cbias, cbias)


def kv_fwd(z, fb, d_model, name):
    t, _ = z.shape
    blk = 2 * d_model // LANES

    def body(z_ref, fb_ref, c_ref, lf_s):
        v = z_ref[...] + fb_ref[...]
        lf_s[...] = -_softplus(-v)
        _scan_rows(None, lf_s, c_ref, t, LANES, reverse=False)

    return pl.pallas_call(
        body, name=name, grid=(1,),
        in_specs=[pl.BlockSpec((t, LANES), lambda j: (0, blk)), pl.BlockSpec((1, LANES), lambda j: (0, 0))],
        out_specs=pl.BlockSpec((t, LANES), lambda j: (0, 0)),
        out_shape=jax.ShapeDtypeStruct((t, LANES), F32),
        scratch_shapes=[pltpu.VMEM((t, LANES), F32)],
        compiler_params=_cp("arbitrary"))(z, fb)


def kv_bwd(dcs, z, fb, d_model, name):
    t, _ = z.shape
    blk = 2 * d_model // LANES
    n = len(dcs)

    def body(*refs):
        z_ref, fb_ref, dz_ref, dfb_ref, dc_s, dl_s = refs[n:]
        tot = refs[0][...]
        for r in refs[1:n]:
            tot = tot + r[...]
        dc_s[...] = tot
        _scan_rows(None, dc_s, dl_s, t, LANES, reverse=True)
        v = z_ref[...] + fb_ref[...]
        dz = dl_s[...] * _sigmoid(-v)
        dz_ref[...] = dz.astype(BF16)
        dfb_ref[...] = jnp.sum(dz, axis=0, keepdims=True)

    full = pl.BlockSpec((t, LANES), lambda j: (0, 0))
    return pl.pallas_call(
        body, name=name, grid=(1,),
        in_specs=[full] * n + [pl.BlockSpec((t, LANES), lambda j: (0, blk)),
                               pl.BlockSpec((1, LANES), lambda j: (0, 0))],
        out_specs=[full, pl.BlockSpec((1, LANES), lambda j: (0, 0))],
        out_shape=[jax.ShapeDtypeStruct((t, LANES), BF16), jax.ShapeDtypeStruct((1, LANES), F32)],
        scratch_shapes=[pltpu.VMEM((t, LANES), F32)] * 2,
        compiler_params=_cp("arbitrary"))(*dcs, z, fb)


def add_cast(a, b, name):
    t, d = a.shape
    cb = _tile(d, 512)

    def body(a_ref, b_ref, o_ref):
        o_ref[...] = (a_ref[...] + b_ref[...]).astype(BF16)

    col = pl.BlockSpec((t, cb), lambda j: (0, j))
    return pl.pallas_call(body, name=name, grid=(d // cb,), in_specs=[col, col], out_specs=col,
                          out_shape=jax.ShapeDtypeStruct((t, d), BF16),
                          compiler_params=_cp("parallel"))(a, b)


def _attn_geometry(t):
    nqb = 6 if t > 1024 else 2
    tp = _round_up(t, LANES * nqb)
    return nqb, tp, tp // nqb


def _attn_scales(dh):
    scale = dh ** -0.5
    if math.log2(scale).is_integer():
        return scale, 1.0
    return 1.0, scale


def _attn_pieces(qs, ks, crow, j, i, tq, dh, s_mul):
    r0 = i * tq
    lanes = pl.ds(j * dh, dh)
    qi = qs[pl.ds(r0, tq), lanes]
    spans = ([(0, r0)] if i > 0 else []) + [(r0, tq)]
    logits = []
    for k0, n in spans:
        s = lax.dot_general(qi, ks[pl.ds(k0, n), lanes], (((1,), (1,)), ((), ())),
                            preferred_element_type=F32)
        if s_mul != 1.0:
            s = s * s_mul
        s = s - crow[:, k0:k0 + n]
        if k0 == r0:
            rows = lax.broadcasted_iota(jnp.int32, (tq, tq), 0)
            cols = lax.broadcasted_iota(jnp.int32, (tq, tq), 1)
            s = jnp.where(cols <= rows, s, NEG_BIG)
        logits.append(s)
    mx = jnp.max(logits[0], axis=1, keepdims=True)
    for s in logits[1:]:
        mx = jnp.maximum(mx, jnp.max(s, axis=1, keepdims=True))
    es = [jnp.exp(s - mx) for s in logits]
    tot = jnp.sum(es[0], axis=1, keepdims=True)
    for e in es[1:]:
        tot = tot + jnp.sum(e, axis=1, keepdims=True)
    inv = 1.0 / tot
    return [(k0, n, e * inv) for (k0, n), e in zip(spans, es)], qi


def attn_fwd(qg, z, ct_pad, d_model, n_heads, name):
    t = qg.shape[0]
    dh = d_model // n_heads
    hp = LANES // dh
    nqb, tp, tq = _attn_geometry(t)
    nblk = d_model // LANES
    q_mul, s_mul = _attn_scales(dh)

    def body(q_ref, og_ref, k_ref, v_ref, ct_ref, o_ref, mo_ref, qs, ks, vs, os_):
        pad = jnp.zeros((tp - t, LANES), BF16)
        qs[pl.ds(0, t), :] = (q_ref[...] * q_mul).astype(BF16)
        qs[pl.ds(t, tp - t), :] = pad
        for src, dst in ((k_ref, ks), (v_ref, vs)):
            dst[pl.ds(0, t), :] = src[...].astype(BF16)
            dst[pl.ds(t, tp - t), :] = pad
        for j in range(hp):
            crow = ct_ref[j]
            lanes = pl.ds(j * dh, dh)
            for i in range(nqb):
                pieces, _ = _attn_pieces(qs, ks, crow, j, i, tq, dh, s_mul)
                acc = None
                for k0, n, p in pieces:
                    part = jnp.dot(p.astype(BF16), vs[pl.ds(k0, n), lanes], preferred_element_type=F32)
                    acc = part if acc is None else acc + part
                os_[pl.ds(i * tq, tq), lanes] = acc
        o = os_[pl.ds(0, t), :]
        o_ref[...] = o
        mo_ref[...] = (o * _sigmoid(og_ref[...])).astype(BF16)

    col = lambda off: pl.BlockSpec((t, LANES), lambda p: (0, off + p))
    return pl.pallas_call(
        body, name=name, grid=(nblk,),
        in_specs=[col(0), col(nblk), col(0), col(nblk), pl.BlockSpec((hp, 1, tp), lambda p: (p, 0, 0))],
        out_specs=[col(0), col(0)],
        out_shape=[jax.ShapeDtypeStruct((t, d_model), F32), jax.ShapeDtypeStruct((t, d_model), BF16)],
        scratch_shapes=[pltpu.VMEM((tp, LANES), BF16)] * 3 + [pltpu.VMEM((tp, LANES), F32)],
        compiler_params=_cp("parallel"))(qg, qg, z, z, ct_pad)


def attn_bwd(dmo, qg, z, o, ct_pad, d_model, n_heads, name):
    t = qg.shape[0]
    dh = d_model // n_heads
    hp = LANES // dh
    nqb, tp, tq = _attn_geometry(t)
    nblk = d_model // LANES
    q_mul, s_mul = _attn_scales(dh)
    scale = dh ** -0.5
    tn_dims = (((0,), (0,)), ((), ()))
    nt_dims = (((1,), (1,)), ((), ()))

    def body(dmo_ref, q_ref, og_ref, k_ref, v_ref, o_ref, ct_ref,
             dq_ref, dog_ref, dk_ref, dv_ref, dct_ref, qs, ks, vs, dos, dqs, dks, dvs):
        pad = jnp.zeros((tp - t, LANES), BF16)
        sg = _sigmoid(og_ref[...])
        dmo_v = dmo_ref[...]
        dog_ref[...] = (dmo_v * o_ref[...] * sg * (1.0 - sg)).astype(BF16)
        dos[pl.ds(0, t), :] = (dmo_v * sg).astype(BF16)
        dos[pl.ds(t, tp - t), :] = pad
        qs[pl.ds(0, t), :] = (q_ref[...] * q_mul).astype(BF16)
        qs[pl.ds(t, tp - t), :] = pad
        for src, dst in ((k_ref, ks), (v_ref, vs)):
            dst[pl.ds(0, t), :] = src[...].astype(BF16)
            dst[pl.ds(t, tp - t), :] = pad
        dks[...] = jnp.zeros_like(dks)
        dvs[...] = jnp.zeros_like(dvs)
        dct_ref[...] = jnp.zeros_like(dct_ref)
        for j in range(hp):
            crow = ct_ref[j]
            lanes = pl.ds(j * dh, dh)
            for i in range(nqb):
                pieces, qi = _attn_pieces(qs, ks, crow, j, i, tq, dh, s_mul)
                do_i = dos[pl.ds(i * tq, tq), lanes]
                dps = [lax.dot_general(do_i, vs[pl.ds(k0, n), lanes], nt_dims, preferred_element_type=F32)
                       for k0, n, _ in pieces]
                row = None
                for (_, _, p), dp in zip(pieces, dps):
                    part = jnp.sum(p * dp, axis=1, keepdims=True)
                    row = part if row is None else row + part
                dq_i = None
                for (k0, n, p), dp in zip(pieces, dps):
                    ds = p * (dp - row)
                    ds_b = ds.astype(BF16)
                    keys = pl.ds(k0, n)
                    part = jnp.dot(ds_b, ks[keys, lanes], preferred_element_type=F32)
                    dq_i = part if dq_i is None else dq_i + part
                    dks[keys, lanes] += lax.dot_general(ds_b, qi, tn_dims, preferred_element_type=F32) * s_mul
                    dvs[keys, lanes] += lax.dot_general(p.astype(BF16), do_i, tn_dims,
                                                        preferred_element_type=F32)
                    dct_ref[j, :, keys] -= jnp.sum(ds, axis=0, keepdims=True)
                dqs[pl.ds(i * tq, tq), lanes] = dq_i * scale
        dq_ref[...] = dqs[pl.ds(0, t), :].astype(BF16)
        dk_ref[...] = dks[pl.ds(0, t), :]
        dv_ref[...] = dvs[pl.ds(0, t), :]

    col = lambda off: pl.BlockSpec((t, LANES), lambda p: (0, off + p))
    big = lambda dt: jax.ShapeDtypeStruct((t, d_model), dt)
    return pl.pallas_call(
        body, name=name, grid=(nblk,),
        in_specs=[col(0), col(0), col(nblk), col(0), col(nblk), col(0),
                  pl.BlockSpec((hp, 1, tp), lambda p: (p, 0, 0))],
        out_specs=[col(0), col(0), col(0), col(0), pl.BlockSpec((hp, 1, tp), lambda p: (p, 0, 0))],
        out_shape=[big(BF16), big(BF16), big(F32), big(F32),
                   jax.ShapeDtypeStruct((n_heads, 1, tp), F32)],
        scratch_shapes=[pltpu.VMEM((tp, LANES), BF16)] * 4 + [pltpu.VMEM((tp, LANES), F32)] * 3,
        compiler_params=_cp("parallel"))(dmo, qg, qg, z, z, o, ct_pad)


def cast_into_slot(shard, w, index, name):
    r, c = w.shape[-2:]
    rh = r // 2
    tr = _tile(rh, 512, 16)
    n = rh // tr
    if w.ndim == 3:
        w_spec = pl.BlockSpec((None, tr, c), lambda h, i, sh: (index, h * n + i, 0))
    else:
        w_spec = pl.BlockSpec((tr, c), lambda h, i, sh: (h * n + i, 0))

    def body(sh_ref, w_ref, o_ref):
        del sh_ref
        o_ref[...] = w_ref[...].astype(BF16)

    return pl.pallas_call(
        body, name=name,
        grid_spec=pltpu.PrefetchScalarGridSpec(
            num_scalar_prefetch=1, grid=(2, n), in_specs=[w_spec],
            out_specs=pl.BlockSpec((None, None, tr, c), lambda h, i, sh: (sh[0], h, i, 0))),
        out_shape=jax.ShapeDtypeStruct((N_SHARDS, 2, rh, c), BF16),
        compiler_params=_cp("parallel", "parallel"))(shard, w)


def owner_sum(shard_core, g, recv, buf, layer, name):
    _, _, rh, c = g.shape
    n_recv = recv.shape[0]
    tr = _tile(rh, 512, 16)
    slot = (lambda sc: sc[0]) if layer is None else (lambda sc: layer)

    def body(sc_ref, a_ref, r_ref, buf_ref, out_ref):
        del sc_ref, buf_ref
        acc = a_ref[...].astype(F32)
        for k in range(n_recv):
            acc = acc + r_ref[k].astype(F32)
        out_ref[...] = acc

    return pl.pallas_call(
        body, name=name,
        grid_spec=pltpu.PrefetchScalarGridSpec(
            num_scalar_prefetch=1, grid=(rh // tr,),
            in_specs=[pl.BlockSpec((None, None, tr, c), lambda i, sc: (sc[0], sc[1], i, 0)),
                      pl.BlockSpec((n_recv, tr, c), lambda i, sc: (0, i, 0)),
                      pl.BlockSpec(memory_space=pl.ANY)],
            out_specs=pl.BlockSpec((None, None, tr, c), lambda i, sc: (slot(sc), sc[1], i, 0))),
        out_shape=jax.ShapeDtypeStruct(buf.shape, F32),
        input_output_aliases={3: 0},
        compiler_params=_cp("parallel"))(shard_core, g, recv, buf)


def adamw(w, g, m, v, name, emit_g=False):
    r, c = w.shape
    tr = _tile(r, 512, SUBLANES)
    c1 = 1.0 - ADAM_B1 ** ADAM_STEP
    c2 = 1.0 - ADAM_B2 ** ADAM_STEP
    n_out = 4 if emit_g else 3

    def body(w_ref, g_ref, m_ref, v_ref, d_ref, mo_ref, vo_ref, *go_ref):
        gv = g_ref[...]
        if emit_g:
            go_ref[0][...] = gv
        mn = ADAM_B1 * m_ref[...] + (1.0 - ADAM_B1) * gv
        vn = ADAM_B2 * v_ref[...] + (1.0 - ADAM_B2) * (gv * gv)
        m_hat = mn / c1
        v_hat = vn / c2
        d_ref[...] = -ADAM_LR * (m_hat / (jnp.sqrt(v_hat) + ADAM_EPS) + ADAM_WD * w_ref[...])
        mo_ref[...] = mn
        vo_ref[...] = vn

    blk = pl.BlockSpec((tr, c), lambda i: (i, 0))
    return pl.pallas_call(
        body, name=name, grid=(r // tr,), in_specs=[blk] * 4, out_specs=[blk] * n_out,
        out_shape=[jax.ShapeDtypeStruct((r, c), F32)] * n_out,
        compiler_params=_cp("parallel"))(w, g, m, v)


def _coords():
    return lax.axis_index("x"), lax.axis_index("y"), lax.axis_index("c")


def _exchange(name, ins, out_shapes, plan, in_place=False):
    n_in = len(ins)
    n_out = len(out_shapes)
    n_rem = len(plan([None] * n_in, [None] * n_out, True))

    def body(*refs):
        in_refs = refs[:n_in]
        out_refs = refs[n_in:n_in + n_out]
        send_sems, recv_sems = refs[n_in + n_out:]
        copies = [pltpu.make_async_remote_copy(
            src_ref=src, dst_ref=dst, send_sem=send_sems.at[q], recv_sem=recv_sems.at[q],
            device_id=peer, device_id_type=pl.DeviceIdType.MESH)
            for q, (src, dst, peer) in enumerate(plan(list(in_refs), list(out_refs), False))]
        for cp in copies:
            cp.start()
        for cp in copies:
            cp.wait_recv()
        for cp in copies:
            cp.wait_send()

    hbm = pl.BlockSpec(memory_space=pl.ANY)
    return pl.pallas_call(
        body, name=name, in_specs=[hbm] * n_in, out_specs=[hbm] * n_out, out_shape=out_shapes,
        input_output_aliases={i: i for i in range(n_in)} if in_place else {},
        scratch_shapes=[pltpu.SemaphoreType.DMA((n_rem,)), pltpu.SemaphoreType.DMA((n_rem,))],
        compiler_params=pltpu.CompilerParams(has_side_effects=True))(*ins)


def _split_start(name, groups, plan):
    flat = [a for grp in groups for a in grp]
    n, n_grp = len(flat), len(groups)
    counts = [len(plan(g, [None] * len(grp), True)) for g, grp in enumerate(groups)]

    def body(*refs):
        ins, sems, token = refs[:n], refs[n:n + 2 * n_grp], refs[-1]
        pos = 0
        for g, grp in enumerate(groups):
            arrs = list(ins[pos:pos + len(grp)])
            pos += len(grp)
            for q, (src, dst, peer) in enumerate(plan(g, arrs, False)):
                pltpu.make_async_remote_copy(
                    src_ref=src, dst_ref=dst, send_sem=sems[2 * g].at[q], recv_sem=sems[2 * g + 1].at[q],
                    device_id=peer, device_id_type=pl.DeviceIdType.MESH).start()
        token[...] = jnp.zeros_like(token)

    hbm = pl.BlockSpec(memory_space=pltpu.HBM)
    sem = pl.BlockSpec(memory_space=pltpu.SEMAPHORE)
    outs = pl.pallas_call(
        body, name=name,
        out_shape=[pltpu.SemaphoreType.DMA((cnt,)) for cnt in counts for _ in range(2)]
        + [pltpu.HBM(a.shape, a.dtype) for a in flat] + [jax.ShapeDtypeStruct((SUBLANES, LANES), F32)],
        in_specs=[hbm] * n, out_specs=[sem] * (2 * n_grp) + [hbm] * n + [pl.BlockSpec(memory_space=pltpu.VMEM)],
        input_output_aliases={i: 2 * n_grp + i for i in range(n)},
        compiler_params=pltpu.CompilerParams(has_side_effects=pltpu.SideEffectType.DATAFLOW_SIDE_EFFECTING),
    )(*[pltpu.with_memory_space_constraint(a, pltpu.HBM) for a in flat])
    started, pos = [], 2 * n_grp
    for g, grp in enumerate(groups):
        started.append((outs[2 * g], outs[2 * g + 1], list(outs[pos:pos + len(grp)])))
        pos += len(grp)
    return started, outs[-1]


def _split_wait(name, started, after, plan_g):
    send_sems, recv_sems, arrs = started
    n = len(arrs)

    def body(*refs):
        ins, ssem, rsem = list(refs[:n]), refs[n], refs[n + 1]
        for q, (src, dst, peer) in enumerate(plan_g(ins, False)):
            cp = pltpu.make_async_remote_copy(
                src_ref=src, dst_ref=dst, send_sem=ssem.at[q], recv_sem=rsem.at[q],
                device_id=peer, device_id_type=pl.DeviceIdType.MESH)
            cp.wait_send()
            cp.wait_recv()

    hbm = pl.BlockSpec(memory_space=pltpu.HBM)
    sem = pl.BlockSpec(memory_space=pltpu.SEMAPHORE)
    return pl.pallas_call(
        body, name=name, out_shape=[pltpu.HBM(a.shape, a.dtype) for a in arrs],
        in_specs=[hbm] * n + [sem, sem, pl.BlockSpec(memory_space=pl.ANY)], out_specs=[hbm] * n,
        input_output_aliases={i: i for i in range(n)},
        compiler_params=pltpu.CompilerParams(has_side_effects=pltpu.SideEffectType.DATAFLOW_SIDE_EFFECTING),
    )(*arrs, send_sems, recv_sems, after)


def _gather_ici_plan(arrs, count_only):
    if count_only:
        return [None] * (3 * len(arrs))
    x, y, c = _coords()
    pushes = []
    for a in arrs:
        mine = a.at[2 * x + y, c]
        pushes += [(mine, mine, peer) for peer, _ in _other_chips(x, y, c)]
    return pushes


def _all_to_all_plan(arrs, count_only):
    half = len(arrs) // 2
    if count_only:
        return [None] * (7 * half)
    x, y, c = _coords()
    pushes = []
    for src, land in zip(arrs[:half], arrs[half:]):
        for flips in range(1, 8):
            px, py, pc = (1 - x if flips & 4 else x), (1 - y if flips & 2 else y), (1 - c if flips & 1 else c)
            pushes.append((src.at[2 * px + py, pc], land.at[flips - 1], (px, py, pc)))
    return pushes


def _forward_plan(arrs, count_only):
    if count_only:
        return [None] * (3 * len(arrs))
    x, y, c = _coords()
    pushes = []
    for a in arrs:
        for _, src_shard in _other_chips(x, y, c):
            slab = a.at[src_shard, c]
            pushes.append((slab, slab, (x, y, 1 - c)))
    return pushes


def forward_to_sibling(bufs, name):
    shapes = [jax.ShapeDtypeStruct(b.shape, b.dtype) for b in bufs]
    return _exchange(name, bufs, shapes, lambda ins, outs, cnt: _forward_plan(outs, cnt), in_place=True)


def _other_chips(x, y, c):
    return [((1 - x, y, c), 2 * (1 - x) + y), ((x, 1 - y, c), 2 * x + 1 - y),
            ((1 - x, 1 - y, c), 2 * (1 - x) + 1 - y)]


def join_halves(bufs, everywhere, name):
    slots = [(i, l) for i, b in enumerate(bufs) for l in range(b.shape[0])]
    n = len(bufs)

    def plan(ins, outs, count_only):
        if count_only:
            return [None] * (len(slots) + 7)
        x, y, c = _coords()
        pushes = [(outs[i].at[l, c], outs[i].at[l, c], (x, y, 1 - c)) for i, l in slots]
        mine = outs[n].at[2 * x + y, c]
        for flips in range(1, 8):
            peer = (1 - x if flips & 4 else x, 1 - y if flips & 2 else y, 1 - c if flips & 1 else c)
            pushes.append((mine, mine, peer))
        return pushes

    arrs = list(bufs) + [everywhere]
    shapes = [jax.ShapeDtypeStruct(b.shape, b.dtype) for b in arrs]
    return _exchange(name, arrs, shapes, plan, in_place=True)


def _pack(arrays, multiple):
    flat = jnp.concatenate([a.reshape(-1) for a in arrays])
    n = flat.shape[0]
    return jnp.pad(flat, (0, _round_up(n, multiple) - n))


def _unpack(flat, shapes):
    out, pos = [], 0
    for shp in shapes:
        n = math.prod(shp)
        out.append(flat[pos:pos + n].reshape(shp))
        pos += n
    return out


def _block_diag(w, per_group):
    nb, bs, _ = w.shape
    g = nb // per_group
    w4 = w.reshape(g, per_group, bs, bs)
    eye = jnp.eye(per_group, dtype=w.dtype)
    full = w4[:, :, :, None, :] * eye[None, :, None, :, None]
    return full.reshape(g, per_group * bs, per_group * bs).astype(BF16)


def _block_diag_extract(full, per_group, bs):
    g = full.shape[0]
    f5 = full.reshape(g, per_group, bs, per_group, bs)
    idx = jnp.arange(per_group)
    picked = f5[:, idx, :, idx, :]
    return jnp.moveaxis(picked, 0, 1).reshape(g * per_group, bs, bs)


def kernel(x, meta, a_w_in, a_conv_w, a_conv_b, a_w_r, a_b_r, a_w_i, a_b_i, a_lambda, a_w_out, kv_w, kv_f_b, b_w_in, b_w_out, f_w_in, f_conv_w, f_conv_b, f_w_out, ln1_g, ln1_b, ln2_g, ln2_b, loss_target, m_meta, m_a_w_in, m_a_conv_w, m_a_conv_b, m_a_w_r, m_a_b_r, m_a_w_i, m_a_b_i, m_a_lambda, m_a_w_out, m_kv_w, m_kv_f_b, m_b_w_in, m_b_w_out, m_f_w_in, m_f_conv_w, m_f_conv_b, m_f_w_out, m_ln1_g, m_ln1_b, m_ln2_g, m_ln2_b, v_meta, v_a_w_in, v_a_conv_w, v_a_conv_b, v_a_w_r, v_a_b_r, v_a_w_i, v_a_b_i, v_a_lambda, v_a_w_out, v_kv_w, v_kv_f_b, v_b_w_in, v_b_w_out, v_f_w_in, v_f_conv_w, v_f_conv_b, v_f_w_out, v_ln1_g, v_ln1_b, v_ln2_g, v_ln2_b):
    weights = dict(meta=meta, a_w_in=a_w_in, a_conv_w=a_conv_w, a_conv_b=a_conv_b, a_w_r=a_w_r, a_b_r=a_b_r,
                   a_w_i=a_w_i, a_b_i=a_b_i, a_lambda=a_lambda, a_w_out=a_w_out, kv_w=kv_w, kv_f_b=kv_f_b,
                   b_w_in=b_w_in, b_w_out=b_w_out, f_w_in=f_w_in, f_conv_w=f_conv_w, f_conv_b=f_conv_b,
                   f_w_out=f_w_out, ln1_g=ln1_g, ln1_b=ln1_b, ln2_g=ln2_g, ln2_b=ln2_b)
    mom_m = dict(meta=m_meta, a_w_in=m_a_w_in, a_conv_w=m_a_conv_w, a_conv_b=m_a_conv_b, a_w_r=m_a_w_r,
                 a_b_r=m_a_b_r, a_w_i=m_a_w_i, a_b_i=m_a_b_i, a_lambda=m_a_lambda, a_w_out=m_a_w_out,
                 kv_w=m_kv_w, kv_f_b=m_kv_f_b, b_w_in=m_b_w_in, b_w_out=m_b_w_out, f_w_in=m_f_w_in,
                 f_conv_w=m_f_conv_w, f_conv_b=m_f_conv_b, f_w_out=m_f_w_out, ln1_g=m_ln1_g, ln1_b=m_ln1_b,
                 ln2_g=m_ln2_g, ln2_b=m_ln2_b)
    mom_v = dict(meta=v_meta, a_w_in=v_a_w_in, a_conv_w=v_a_conv_w, a_conv_b=v_a_conv_b, a_w_r=v_a_w_r,
                 a_b_r=v_a_b_r, a_w_i=v_a_w_i, a_b_i=v_a_b_i, a_lambda=v_a_lambda, a_w_out=v_a_w_out,
                 kv_w=v_kv_w, kv_f_b=v_kv_f_b, b_w_in=v_b_w_in, b_w_out=v_b_w_out, f_w_in=v_f_w_in,
                 f_conv_w=v_f_conv_w, f_conv_b=v_f_conv_b, f_w_out=v_f_w_out, ln1_g=v_ln1_g, ln1_b=v_ln1_b,
                 ln2_g=v_ln2_g, ln2_b=v_ln2_b)
    return _train_step(x, loss_target, weights, mom_m, mom_v)


WEIGHT_ORDER = ("meta", "a_w_in", "a_conv_w", "a_conv_b", "a_w_r", "a_b_r", "a_w_i", "a_b_i", "a_lambda",
                "a_w_out", "kv_w", "kv_f_b", "b_w_in", "b_w_out", "f_w_in", "f_conv_w", "f_conv_b",
                "f_w_out", "ln1_g", "ln1_b", "ln2_g", "ln2_b")
BIG = ("a_w_in", "a_w_out", "kv_w", "b_w_in", "b_w_out", "f_w_in", "f_w_out")
OUT_TYPE = ("a_w_out", "b_w_out", "f_w_out")
SMALL_SHARDED = (("meta", 1), ("a_conv_w", 2), ("a_conv_b", 1), ("a_b_r", 1), ("a_b_i", 1), ("a_lambda", 1),
                 ("f_conv_w", 2))
SMALL_REPLICATED = ("a_w_r", "a_w_i", "kv_f_b", "f_conv_b", "ln1_g", "ln1_b", "ln2_g", "ln2_b")


def _train_step(x, loss_target, weights, mom_m, mom_v):
    S = N_SHARDS
    seq, d = x.shape[1], x.shape[2]
    nm = weights["meta"].shape[0]
    la = weights["a_w_in"].shape[0]
    lb = weights["b_w_in"].shape[0]
    depth = la + lb
    dr = weights["a_w_out"].shape[1] * S
    nb, bs = weights["a_w_r"].shape[1], weights["a_w_r"].shape[2]
    per_group = (LANES // math.gcd(bs, LANES))
    gs = per_group * bs
    heads = weights["kv_f_b"].shape[0]
    dff = weights["f_w_out"].shape[1] * S
    nkv = 2 * d + heads
    nkv_s = weights["kv_w"].shape[1]
    nkvp = _round_up(2 * d + LANES, 768) if 2 * d + LANES > 768 else 2 * d + LANES
    alpha = (2 * depth) ** 0.25
    xi, yi, ci = _coords()
    shard = 2 * xi + yi
    shard_arr = jnp.reshape(shard, (1,)).astype(jnp.int32)
    shard_core_arr = jnp.stack([shard, ci]).astype(jnp.int32)

    def mixer_keys(l):
        if l < la:
            return [("a_w_in", l), ("a_w_out", l)]
        return ([("kv_w", 0)] if l == la else []) + [("b_w_in", l - la), ("b_w_out", l - la)]

    def ffn_keys(l):
        return [("f_w_in", l), ("f_w_out", l)]

    assert la >= 1
    groups = [mixer_keys(0)[:1], mixer_keys(0)[1:] + ffn_keys(0)]
    groups += [mixer_keys(l) + ffn_keys(l) for l in range(1, depth)]
    keys = [kl for grp in groups for kl in grp]
    shape2d = {(k, i): weights[k].shape[-2:] for k, i in keys}
    small_local = [weights[k] for k, _ in SMALL_SHARDED]
    sm_flat = _pack(small_local, 2 * SUBLANES * LANES).reshape(1, 2, -1, LANES)
    sm_slot = lax.dynamic_update_slice_in_dim(lax.empty((S,) + sm_flat.shape[1:], F32), sm_flat, shard, axis=0)
    parts = [[cast_into_slot(shard_arr, weights[k], i, f"cast_{k}{i}") for k, i in grp] for grp in groups]
    parts[0].append(sm_slot)
    in_flight, start_token = _split_start("gather_start", parts, lambda g, refs, cnt: _gather_ici_plan(refs, cnt))
    gw = {}

    def use(g, arrs):
        for kl, a in zip(groups[g], arrs):
            rows, cols = shape2d[kl]
            gw[kl] = a.reshape(S * rows, cols) if kl[0] in OUT_TYPE else a.reshape(S, rows, cols)
        return arrs

    def fetch(g, after):
        arrs = _split_wait(f"gather_wait_{g}", in_flight[g], after, _gather_ici_plan)
        return use(g, forward_to_sibling(arrs, f"gather_fwd_{g}"))

    forwarding = {}

    def fetch_early(g, after):
        arrs = _split_wait(f"gather_wait_{g}", in_flight[g], after, _gather_ici_plan)
        started, token = _split_start(f"gather_fwd_start_{g}", [arrs], lambda _, refs, cnt: _forward_plan(refs, cnt))
        forwarding[g] = started[0]
        return token[0:1, 0:1]

    def fetch_end(g, after):
        return use(g, _split_wait(f"gather_fwd_wait_{g}", forwarding[g], after, _forward_plan))

    sm_all = fetch(0, start_token)[-1].reshape(S, -1)
    small_full = {}
    per_shard = [_unpack(sm_all[s], [a.shape for a in small_local]) for s in range(S)]
    for idx, (k, axis) in enumerate(SMALL_SHARDED):
        small_full[k] = jnp.concatenate([per_shard[s][idx] for s in range(S)], axis=axis)
    fb_pad = jnp.pad(weights["kv_f_b"], (0, LANES - heads)).reshape(1, LANES)
    wr_g = [_block_diag(weights["a_w_r"][l], per_group) for l in range(la)]
    wi_g = [_block_diag(weights["a_w_i"][l], per_group) for l in range(la)]
    row = lambda v: v.reshape(1, -1)

    h, hb = embed_fwd(small_full["meta"], x[0], "embed")
    saved = []
    kvz = ct_pad = None
    _, tp, _ = _attn_geometry(nm + seq)
    t = nm + seq
    for l in range(depth):
        sv = {"hb_in": hb}
        if l > 0:
            (fetch_end if l + 1 in forwarding else fetch)(l + 1, hb)
        if l < la:
            gr = mm_in(hb, gw[("a_w_in", l)], F32, f"a{l}_in")
            rc, rcb = a_conv_fwd(gr, small_full["a_conv_w"][l], row(small_full["a_conv_b"][l]), f"a{l}_conv")
            r_pre, i_pre = mm_bd(rcb, wr_g[l], wi_g[l], f"a{l}_gates")
            hs, gb = a_elem_fwd(gr, rc, r_pre, i_pre, row(small_full["a_b_r"][l]), row(small_full["a_b_i"][l]),
                                row(small_full["a_lambda"][l]), f"a{l}_lru")
            if l == 0:
                fetch(1, gb)
            mixed, w_mix = gb, gw[("a_w_out", l)]
            sv.update(gr=gr, rc=rc, rcb=rcb, r_pre=r_pre, i_pre=i_pre, hs=hs, gb=gb)
        else:
            j = l - la
            if j == 0:
                kv_cat = jnp.moveaxis(gw[("kv_w", 0)], 0, 1).reshape(d, S * nkv_s)
                kv_pad = jnp.pad(kv_cat, ((0, 0), (0, nkvp - nkv))).reshape(1, d, nkvp)
                kvz = mm_in(hb, kv_pad, F32, "kv_proj")
                cum = kv_fwd(kvz, fb_pad, d, "kv_forget")
                ct_pad = jnp.pad(cum[:, :heads].T, ((0, 0), (0, tp - t))).reshape(heads, 1, tp)
                kv_hb = hb
            qg = mm_in(hb, gw[("b_w_in", j)], F32, f"b{j}_in")
            o, mob = attn_fwd(qg, kvz, ct_pad, d, heads, f"b{j}_attn")
            mixed, w_mix = mob, gw[("b_w_out", j)]
            sv.update(qg=qg, o=o, mob=mob)
        h1, h1b, xh1, rs1 = mm_out_ln(mixed, w_mix, h, row(weights["ln1_g"][l]), row(weights["ln1_b"][l]), alpha,
                                      f"mix{l}_out_ln1")
        f_bias = row(weights["f_conv_b"][l])
        if l >= 1 and l + 2 < len(groups):
            f_bias = f_bias + fetch_early(l + 2, h1b)
        zf = mm_in(h1b, gw[("f_w_in", l)], F32, f"f{l}_in")
        ffb = f_elem_fwd(zf, small_full["f_conv_w"][l], f_bias, f"f{l}_act")
        h2, h2b, xh2, rs2 = mm_out_ln(ffb, gw[("f_w_out", l)], h1, row(weights["ln2_g"][l]), row(weights["ln2_b"][l]),
                                      alpha, f"f{l}_out_ln2")
        sv.update(h1b=h1b, xh1=xh1, rs1=rs1, zf=zf, ffb=ffb, xh2=xh2, rs2=rs2)
        saved.append(sv)
        h, hb = h2, h2b
    loss11, dy = loss_fwd_bwd(h, loss_target[0], nm, "loss")

    grads = {}

    def by_owner(kl, g3):
        rows, cols = shape2d[kl]
        grads[kl] = g3.reshape(S, 2, rows // 2, cols)

    reducing = []

    def send_grads(g, names, arrays):
        lands = [lax.empty((7,) + a.shape[2:], a.dtype) for a in arrays]
        started, token = _split_start(f"grad_a2a_start_{g}", [list(arrays) + lands],
                                      lambda _, refs, cnt: _all_to_all_plan(refs, cnt))
        reducing.append((names, started[0]))
        return token[0:1, 0:1]

    def after_start(vec, zero):
        return vec if zero is None else vec + zero

    pin = None

    g_small = {}
    per_layer = {k: [None] * n for k, n in (
        ("a_conv_w", la), ("a_conv_b", la), ("a_w_r", la), ("a_b_r", la), ("a_w_i", la), ("a_b_i", la),
        ("a_lambda", la), ("f_conv_w", depth), ("f_conv_b", depth), ("ln1_g", depth), ("ln1_b", depth),
        ("ln2_g", depth), ("ln2_b", depth))}
    adds = [(dy, 1.0)]
    dks, dvs, dcs = [], [], []
    for l in reversed(range(depth)):
        sv = saved[l]
        ds2, ds2b, dg2, db2 = ln_bwd(adds, sv["xh2"], sv["rs2"], after_start(row(weights["ln2_g"][l]), pin),
                                     f"ln2_{l}_bwd")
        pin = None
        per_layer["ln2_g"][l], per_layer["ln2_b"][l] = dg2[0], db2[0]
        dff_v = mm_out_nt(ds2b, gw[("f_w_out", l)], f"f{l}_out_dx")
        by_owner(("f_w_out", l), mm_tn(sv["ffb"], [ds2b], 1, f"f{l}_out_dw"))
        dzg, dzv, dwg, dwv, dbg, dbv = f_elem_bwd(sv["zf"], dff_v, small_full["f_conv_w"][l],
                                                  row(weights["f_conv_b"][l]), f"f{l}_act_bwd")
        per_layer["f_conv_w"][l] = jnp.concatenate([dwg, dwv], axis=1)
        per_layer["f_conv_b"][l] = jnp.concatenate([dbg, dbv], axis=1)[0]
        dh1_f = mm_in_nt([dzg, dzv], gw[("f_w_in", l)], f"f{l}_in_dx")
        by_owner(("f_w_in", l), mm_tn(sv["h1b"], [dzg, dzv], S, f"f{l}_in_dw"))
        w_mix = gw[("a_w_out", l)] if l < la else gw[("b_w_out", l - la)]
        ds1, ds1b, dg1, db1, d_mixed = ln_bwd([(ds2, alpha), (dh1_f, 1.0)], sv["xh1"], sv["rs1"],
                                              row(weights["ln1_g"][l]), f"ln1_{l}_bwd_mix_dx", w=w_mix)
        per_layer["ln1_g"][l], per_layer["ln1_b"][l] = dg1[0], db1[0]
        if l < la:
            dgv = d_mixed
            by_owner(("a_w_out", l), mm_tn(sv["gb"], [ds1b], 1, f"a{l}_out_dw"))
            if l == 0:
                pin = send_grads(1, groups[1], [grads[kl] for kl in groups[1]])
            dgate_b, drp_b, dip_b, drc_d, dlam, dbr, dbi = a_elem_bwd(
                dgv, sv["gr"], sv["rc"], sv["r_pre"], sv["i_pre"], sv["hs"], row(small_full["a_b_r"][l]),
                row(small_full["a_b_i"][l]), after_start(row(small_full["a_lambda"][l]), pin), f"a{l}_lru_bwd")
            pin = None
            drc_g = mm_bd_nt(drp_b, dip_b, wr_g[l], wi_g[l], f"a{l}_gates_dx")
            dwr_g, dwi_g = mm_bd_tn(sv["rcb"], drp_b, dip_b, gs, f"a{l}_gates_dw")
            dgr_b, dcw, dcb = a_conv_bwd(drc_d, drc_g, sv["gr"], small_full["a_conv_w"][l], dgate_b,
                                         f"a{l}_conv_bwd")
            per_layer["a_w_r"][l] = _block_diag_extract(dwr_g, per_group, bs)
            per_layer["a_w_i"][l] = _block_diag_extract(dwi_g, per_group, bs)
            per_layer["a_lambda"][l], per_layer["a_b_r"][l], per_layer["a_b_i"][l] = dlam[0], dbr[0], dbi[0]
            per_layer["a_conv_w"][l], per_layer["a_conv_b"][l] = dcw, dcb[0]
            by_owner(("a_w_in", l), mm_tn(sv["hb_in"], [dgr_b], S, f"a{l}_in_dw"))
            gate_w = gw[("a_w_in", l)]
            if l == 0:
                zero = send_grads(0, groups[0], [grads[kl] for kl in groups[0]])
                gate_w = gate_w + zero.astype(BF16)[None]
            dh_m = mm_in_nt([dgr_b], gate_w, f"a{l}_in_dx")
        else:
            j = l - la
            dmo = d_mixed
            by_owner(("b_w_out", j), mm_tn(sv["mob"], [ds1b], 1, f"b{j}_out_dw"))
            dq_b, dog_b, dk, dv, dct = attn_bwd(dmo, sv["qg"], kvz, sv["o"], ct_pad, d, heads, f"b{j}_attn_bwd")
            dks.append(dk)
            dvs.append(dv)
            dcs.append(jnp.pad(dct[:, 0, :t].T, ((0, 0), (0, LANES - heads))))
            dh_m = mm_in_nt([dq_b, dog_b], gw[("b_w_in", j)], f"b{j}_in_dx")
            by_owner(("b_w_in", j), mm_tn(sv["hb_in"], [dq_b, dog_b], S, f"b{j}_in_dw"))
        adds = [(ds1, alpha), (dh_m, 1.0)]
        if l == la:
            dzf_b, dfb = kv_bwd(dcs, kvz, fb_pad, d, "kv_forget_bwd")
            dk_b = add_cast(dks[0], dks[1], "kv_dk") if lb == 2 else None
            dv_b = add_cast(dvs[0], dvs[1], "kv_dv") if lb == 2 else None
            dz_kv = jnp.concatenate([dk_b, dv_b, dzf_b, jnp.zeros((t, nkvp - 2 * d - LANES), BF16)], axis=1)
            dh_kv = mm_in_nt([dz_kv], kv_pad, "kv_proj_dx")
            kv_dw = mm_tn(kv_hb, [dz_kv], 1, "kv_proj_dw")
            by_owner(("kv_w", 0), jnp.moveaxis(kv_dw[0, :, :nkv].reshape(d, S, nkv_s), 1, 0))
            g_small["kv_f_b"] = dfb[0, :heads]
            adds.append((dh_kv, 1.0))
        if l > 0:
            pin = send_grads(l + 1, groups[l + 1], [grads[kl] for kl in groups[l + 1]])
    g_meta, g_x = embed_bwd(adds, nm, "embed_bwd")

    small_names = list(SMALL_REPLICATED) + [k for k, _ in SMALL_SHARDED]
    g_small["meta"] = g_meta
    for k, vals in per_layer.items():
        g_small[k] = jnp.stack(vals)
    small_shapes = {k: (weights[k].shape if k in SMALL_REPLICATED else g_small[k].shape) for k in small_names}
    sm_g = _pack([g_small[k].reshape(small_shapes[k]) for k in small_names] + [loss11.reshape(1)],
                 S * 2 * SUBLANES * LANES)
    sm_g = sm_g.reshape(S, 2, -1, LANES)
    send_grads(len(groups), [("small", 0)], [sm_g])

    fin = {"small": lax.empty(sm_g.shape, F32)}
    for kl in keys:
        n_stack = weights[kl[0]].shape[0] if weights[kl[0]].ndim == 3 else 1
        rows, cols = shape2d[kl]
        fin.setdefault(kl[0], lax.empty((n_stack, 2, rows // 2, cols), F32))
    for g, (names_g, started) in enumerate(reducing):
        arrs = _split_wait(f"grad_a2a_wait_{g}", started, g_x, _all_to_all_plan)
        half = len(names_g)
        for i, (kl, cs, rv) in enumerate(zip(names_g, arrs[:half], arrs[half:])):
            fin[kl[0]] = owner_sum(shard_core_arr, cs, rv, fin[kl[0]], None if kl[0] == "small" else kl[1],
                                   f"owner_sum_{g}_{i}")
    names = list(BIG) + ["small"]
    joined = dict(zip(names, join_halves([fin[k] for k in BIG], fin["small"], "grad_join")))
    sm_red = joined["small"].reshape(-1)

    out_g, out_d, out_m, out_v = {}, {}, {}, {}
    for k in BIG:
        w2 = weights[k].reshape(-1, weights[k].shape[-1])
        g2 = joined[k].reshape(w2.shape)
        dlt, mn, vn, g_out = adamw(w2, g2, mom_m[k].reshape(w2.shape), mom_v[k].reshape(w2.shape), "adamw_" + k,
                                   emit_g=True)
        shp = weights[k].shape
        out_g[k], out_d[k], out_m[k], out_v[k] = g_out.reshape(shp), dlt.reshape(shp), mn.reshape(shp), vn.reshape(shp)
    *small_sums, loss_sum = _unpack(sm_red, [small_shapes[k] for k in small_names] + [(1,)])
    sm_vals = dict(zip(small_names, small_sums))
    local_small = {}
    for k in SMALL_REPLICATED:
        local_small[k] = sm_vals[k]
    for k, axis in SMALL_SHARDED:
        size = weights[k].shape[axis]
        local_small[k] = lax.dynamic_slice_in_dim(sm_vals[k], shard * size, size, axis=axis)
    for k in small_names:
        shp = weights[k].shape
        two_d = (-1, shp[-1]) if len(shp) > 1 else (1, -1)
        dlt, mn, vn = adamw(weights[k].reshape(two_d), local_small[k].reshape(two_d), mom_m[k].reshape(two_d),
                            mom_v[k].reshape(two_d), "adamw_" + k)
        out_g[k], out_d[k], out_m[k], out_v[k] = local_small[k], dlt.reshape(shp), mn.reshape(shp), vn.reshape(shp)

    return (loss_sum[0], g_x[None], *[out_g[k] for k in WEIGHT_ORDER], *[out_d[k] for k in WEIGHT_ORDER],
            *[out_m[k] for k in WEIGHT_ORDER], *[out_v[k] for k in WEIGHT_ORDER])
```
